```python
import jax, jax.numpy as jnp
from jax import lax
import numpy as np

D_MODEL = 1024
BATCH = 16
SEQ = 256
DEPTH = 2
DEC_BATCH = 4
DEC_SEQ = 1024
PAST_LEN = 256

GRID_W = 64
MA_HEADS = 4
MA_DK = 128
MA_DV = 128
MA_CHUNK = 64
NA_HEADS = 8
NA_DH = 64
NA_KR_MAX = 8
NA_KC = 16
NA_QB = 16
NA_KB = NA_QB + NA_KC
CTX_QBLOCK = 128
HG_HEADS = 4
HG_DK = 128
HG_DV = 128
HG_CHUNK = 32
BRANCH_W = 512
MA_W = MA_HEADS * MA_DK
NA_W = NA_HEADS * NA_DH
HG_W = HG_HEADS * HG_DK
N_GROUPS = 4
EXP_PER_GROUP = 4
N_EXPERTS = N_GROUPS * EXP_PER_GROUP
TOP_K_INNER = 2
D_EXPERT = 512
ROPE_BASE = 10000.0
LN_EPS = 1e-5
RMS_EPS = 1e-6
DEEPNORM_ALPHA = (2 * DEPTH) ** 0.25
DEEPNORM_BETA = (8 * DEPTH) ** -0.25
IN_SIZES = (MA_W, MA_W, MA_HEADS * MA_DV, MA_HEADS * MA_DV,
            MA_HEADS, MA_HEADS, MA_HEADS, MA_HEADS,
            NA_W, NA_W, NA_W,
            HG_W, HG_W, HG_W, HG_HEADS * HG_DV, HG_HEADS * HG_DV,
            D_MODEL, D_MODEL, D_MODEL)
N_IN = sum(IN_SIZES)

kernel_name = "hybrid_mlstm_natten_hgrn2_hmoe_diffusion_step"


def layer_norm(x, g, b):
    xf = x.astype(jnp.float32)
    mu = jnp.mean(xf, -1, keepdims=True)
    var = jnp.mean(jnp.square(xf - mu), -1, keepdims=True)
    return ((xf - mu) * lax.rsqrt(var + LN_EPS)).astype(x.dtype) * g + b


def split_heads(a, n):
    B_, T_, W_ = a.shape
    return a.reshape(B_, T_, n, W_ // n).transpose(0, 2, 1, 3)


def merge_heads(a):
    B_, n, T_, d = a.shape
    return a.transpose(0, 2, 1, 3).reshape(B_, T_, n * d)


def flip_t(a):
    return jnp.flip(a, axis=2)


def to_chunks(a, L):
    B_, H_, T_ = a.shape[:3]
    return jnp.moveaxis(a.reshape(B_, H_, T_ // L, L, *a.shape[3:]), 2, 0)


def from_chunks(a):
    a = jnp.moveaxis(a, 0, 2)
    return a.reshape(a.shape[0], a.shape[1], -1, *a.shape[4:])


def axial_rope(x):
    T_ = x.shape[2]
    t = jnp.arange(T_)
    row = (t // GRID_W).astype(jnp.float32)
    col = (t % GRID_W).astype(jnp.float32)
    half = x.shape[-1] // 2
    inv = ROPE_BASE ** (-jnp.arange(0, half, 2, dtype=jnp.float32) / half)

    def rot(xp, pos):
        ang = pos[:, None] * inv[None, :]
        cos, sin = jnp.cos(ang).astype(x.dtype), jnp.sin(ang).astype(x.dtype)
        x1, x2 = xp[..., :half // 2], xp[..., half // 2:]
        return jnp.concatenate([x1 * cos - x2 * sin, x1 * sin + x2 * cos], -1)

    return jnp.concatenate([rot(x[..., :half], row), rot(x[..., half:], col)], -1)


def mlstm_chunkwise(q, k, v, ig, lf, C0, n0, m0):
    f32 = jnp.float32
    L = MA_CHUNK
    causal = jnp.tril(jnp.ones((L, L), bool))

    def step(carry, inp):
        C, n, m = carry
        qc, kc, vc, ic, fc = inp
        b = jnp.cumsum(fc, axis=-1)
        dmat = jnp.where(causal, b[..., :, None] - b[..., None, :] + ic[..., None, :], -jnp.inf)
        g = b + m[..., None]
        m_t = jnp.maximum(g, jnp.max(dmat, -1))
        w_inter = jnp.exp(g - m_t)
        s = jnp.einsum('bhtd,bhsd->bhts', qc, kc) * jnp.exp(dmat - m_t[..., None])
        num = w_inter[..., None] * jnp.einsum('bhtd,bhde->bhte', qc, C) + jnp.einsum('bhts,bhse->bhte', s, vc)
        den = w_inter * jnp.einsum('bhtd,bhd->bht', qc, n) + jnp.sum(s, -1)
        h = num / jnp.maximum(jnp.abs(den), jnp.exp(-m_t))[..., None]
        m_new = m_t[..., -1]
        decay_state = jnp.exp(b[..., -1] + m - m_new)
        kw = kc * jnp.exp(b[..., -1:] - b + ic - m_new[..., None])[..., None]
        C_new = decay_state[..., None, None] * C + jnp.einsum('bhsd,bhse->bhde', kw, vc)
        n_new = decay_state[..., None] * n + jnp.sum(kw, -2)
        return (C_new, n_new, m_new), h

    xs = (to_chunks(q, L), to_chunks(k, L), to_chunks(v, L), to_chunks(ig, L), to_chunks(lf, L))
    (C, n, m), hs = lax.scan(step, (C0.astype(f32), n0.astype(f32), m0.astype(f32)), xs)
    return from_chunks(hs), C, n, m


def hgrn2_chunkwise(q, k, i, logf, S0):
    L = HG_CHUNK
    causal = jnp.tril(jnp.ones((L, L), bool))[:, :, None]

    def step(S, inp):
        qc, kc, ic, lc = inp
        A = jnp.cumsum(lc, axis=2)
        o_inter = jnp.einsum('bhtd,bhde->bhte', qc * jnp.exp(A), S)
        diff = jnp.where(causal, A[:, :, :, None, :] - A[:, :, None, :, :], -jnp.inf)
        att = jnp.einsum('bhtd,bhtsd,bhsd->bhts', qc, jnp.exp(diff), kc)
        o = o_inter + jnp.einsum('bhts,bhse->bhte', att, ic)
        A_last = A[:, :, -1]
        S_new = jnp.exp(A_last)[..., None] * S + jnp.einsum('bhsd,bhse->bhde', kc * jnp.exp(A_last[:, :, None] - A), ic)
        return S_new, o

    xs = (to_chunks(q, L), to_chunks(k, L), to_chunks(i, L), to_chunks(logf, L))
    S, os_ = lax.scan(step, S0.astype(jnp.float32), xs)
    return from_chunks(os_), S


def context_attention(q, k, v):
    B_, H_, S_, D_ = q.shape
    nb = S_ // CTX_QBLOCK
    qb = jnp.moveaxis(q.reshape(B_, H_, nb, CTX_QBLOCK, D_), 2, 0)

    def one(qblk):
        s = jnp.einsum('bhqd,bhkd->bhqk', qblk, k).astype(jnp.float32)
        p = jax.nn.softmax(s, axis=-1).astype(v.dtype)
        return jnp.einsum('bhqk,bhkd->bhqd', p, v)

    o = lax.map(one, qb)
    return jnp.moveaxis(o, 0, 2).reshape(B_, H_, S_, D_)


def latent_neighbourhood_attention(q, k, v, ck, cv, rpb):
    B_, H_, T_, D_ = q.shape
    rows = T_ // GRID_W
    kr = min(NA_KR_MAX, rows)
    nj = GRID_W // NA_QB
    qcol = np.arange(GRID_W)
    c0 = np.clip(qcol - NA_KC // 2, 0, GRID_W - NA_KC)
    cb = np.clip(np.arange(nj) * NA_QB - NA_KC // 2, 0, GRID_W - NA_KB)
    col_idx = cb[:, None] + np.arange(NA_KB)[None, :]
    kcol = col_idx[:, None, :]
    c0b = c0.reshape(nj, NA_QB)[:, :, None]
    col_mask = (kcol >= c0b) & (kcol < c0b + NA_KC)
    dc_idx = np.clip(kcol - qcol.reshape(nj, NA_QB)[:, :, None] + NA_KC - 1, 0, 2 * NA_KC - 2)
    rpb_cols = rpb[:, :, dc_idx]
    kg = k.reshape(B_, H_, rows, GRID_W, D_)
    vg = v.reshape(B_, H_, rows, GRID_W, D_)
    q_rows = jnp.moveaxis(q.reshape(B_, H_, rows, nj, NA_QB, D_), 2, 0)
    n_loc = kr * NA_KB

    def one_row(args):
        r, qr = args
        r0 = jnp.clip(r - kr // 2, 0, rows - kr)
        ks = lax.dynamic_slice_in_dim(kg, r0, kr, axis=2)[:, :, :, col_idx]
        vs = lax.dynamic_slice_in_dim(vg, r0, kr, axis=2)[:, :, :, col_idx]
        dr_idx = r0 + jnp.arange(kr) - r + (NA_KR_MAX - 1)
        bias = jnp.transpose(jnp.take(rpb_cols, dr_idx, axis=1), (0, 2, 3, 1, 4)).astype(jnp.float32)
        s_loc = jnp.einsum('bhjqd,bhrjkd->bhjqrk', qr, ks).astype(jnp.float32) + bias
        s_loc = jnp.where(col_mask[:, :, None, :], s_loc, -jnp.inf)
        s_ctx = jnp.einsum('bhjqd,bhpd->bhjqp', qr, ck).astype(jnp.float32)
        s = jnp.concatenate([s_loc.reshape(B_, H_, nj, NA_QB, n_loc), s_ctx], -1)
        p = jax.nn.softmax(s, axis=-1).astype(v.dtype)
        p_loc = p[..., :n_loc].reshape(B_, H_, nj, NA_QB, kr, NA_KB)
        return jnp.einsum('bhjqrk,bhrjkd->bhjqd', p_loc, vs) + jnp.einsum('bhjqp,bhpd->bhjqd', p[..., n_loc:], cv)

    out = lax.map(one_row, (jnp.arange(rows), q_rows))
    return jnp.moveaxis(out, 0, 2).reshape(B_, H_, T_, D_)


def token_mixing(h, latent, mC0, mn0, mm0, hS0, ctx_k, ctx_v,
                 w_in_l, b_in_l, fbias_l, lb_l, rpb_l, w_branch_l, w_out_l):
    f32 = jnp.float32
    idx = np.cumsum(np.array(IN_SIZES))[:-1].tolist()
    (aq, ak, av, ao, ai_f, af_f, ai_b, af_b, bq, bk, bv,
     cf_f, cf_b, cq, ci, cg, ga, gb, gc) = jnp.split(h @ w_in_l + b_in_l, idx, axis=-1)
    q = split_heads(aq, MA_HEADS)
    k = split_heads(ak, MA_HEADS) * (MA_DK ** -0.5)
    v = split_heads(av, MA_HEADS)
    if latent:
        q, k = axial_rope(q), axial_rope(k)
    q, k, v = q.astype(f32), k.astype(f32), v.astype(f32)
    fb = fbias_l.astype(f32)
    ig_f = jnp.swapaxes(ai_f, 1, 2).astype(f32)
    ig_b = jnp.swapaxes(ai_b, 1, 2).astype(f32)
    lf_f = jax.nn.log_sigmoid(jnp.swapaxes(af_f, 1, 2).astype(f32) + fb[0][:, None])
    lf_b = jax.nn.log_sigmoid(jnp.swapaxes(af_b, 1, 2).astype(f32) + fb[1][:, None])
    hf, Cf, nf, mf = mlstm_chunkwise(q, k, v, ig_f, lf_f, mC0[:, 0], mn0[:, 0], mm0[:, 0])
    hb, Cb, nb_, mb = mlstm_chunkwise(flip_t(q), flip_t(k), flip_t(v), flip_t(ig_b), flip_t(lf_b),
                                      mC0[:, 1], mn0[:, 1], mm0[:, 1])
    a_out = merge_heads(jax.nn.sigmoid(split_heads(ao, MA_HEADS).astype(f32)) * (hf + flip_t(hb))).astype(h.dtype)
    nq = split_heads(bq, NA_HEADS) * (NA_DH ** -0.5)
    nk = split_heads(bk, NA_HEADS)
    nv = split_heads(bv, NA_HEADS)
    if latent:
        b_heads = latent_neighbourhood_attention(nq, nk, nv, ctx_k, ctx_v, rpb_l)
    else:
        b_heads = context_attention(nq, nk, nv)
        ctx_k, ctx_v = nk, nv
    b_out = merge_heads(b_heads)
    f_f = lb_l[0] + (1.0 - lb_l[0]) * jax.nn.sigmoid(cf_f.astype(f32))
    f_b = lb_l[1] + (1.0 - lb_l[1]) * jax.nn.sigmoid(cf_b.astype(f32))
    f_f, f_b = split_heads(f_f, HG_HEADS), split_heads(f_b, HG_HEADS)
    hq = split_heads(jax.nn.silu(cq.astype(f32)), HG_HEADS)
    hi = split_heads(ci.astype(f32), HG_HEADS)
    of, Sf = hgrn2_chunkwise(hq, 1.0 - f_f, hi, jnp.log(f_f), hS0[:, 0])
    ob, Sb = hgrn2_chunkwise(flip_t(hq), flip_t(1.0 - f_b), flip_t(hi), flip_t(jnp.log(f_b)), hS0[:, 1])
    o = of + flip_t(ob)
    o = o * lax.rsqrt(jnp.mean(o * o, -1, keepdims=True) + RMS_EPS)
    c_out = (merge_heads(o) * jax.nn.silu(cg.astype(f32))).astype(h.dtype)
    y = (jax.nn.sigmoid(ga) * (a_out @ w_branch_l[0])
         + jax.nn.sigmoid(gb) * (b_out @ w_branch_l[1])
         + jax.nn.sigmoid(gc) * (c_out @ w_branch_l[2])) @ w_out_l
    if latent:
        return y, None
    states = (ctx_k, ctx_v, jnp.stack([Cf, Cb], 1), jnp.stack([nf, nb_], 1),
              jnp.stack([mf, mb], 1), jnp.stack([Sf, Sb], 1))
    return y, states


def hier_moe(h, w_rg_l, w_re_l, w_e1_l, w_e3_l, w_e2_l):
    f32 = jnp.float32
    B_, T_, D_ = h.shape
    x = h.reshape(-1, D_)
    lg = (x @ w_rg_l).astype(f32)
    g_sel = jnp.argmax(lg, -1)
    p_sel = jnp.take_along_axis(jax.nn.softmax(lg, -1), g_sel[:, None], -1)
    le = (x @ w_re_l).astype(f32).reshape(-1, N_GROUPS, EXP_PER_GROUP)
    le_sel = jnp.take_along_axis(le, g_sel[:, None, None], 1)[:, 0]
    top_v, top_i = lax.top_k(le_sel, TOP_K_INNER)
    w_top = jax.nn.softmax(top_v, -1) * p_sel
    expert_id = g_sel[:, None] * EXP_PER_GROUP + top_i
    gate = jnp.einsum('nk,nke->ne', w_top, jax.nn.one_hot(expert_id, N_EXPERTS, dtype=f32))
    hid = jax.nn.silu(jnp.einsum('nd,edf->nef', x, w_e1_l)) * jnp.einsum('nd,edf->nef', x, w_e3_l)
    y = jnp.einsum('nef,efd->nd', hid * gate.astype(x.dtype)[..., None], w_e2_l)
    return y.reshape(B_, T_, D_)


def trunk_layer(x, mod, latent, mC0, mn0, mm0, hS0, ctx_k, ctx_v,
                w_in_l, b_in_l, fbias_l, lb_l, rpb_l, w_branch_l, w_out_l,
                ln_g_l, ln_b_l, w_rg_l, w_re_l, w_e1_l, w_e3_l, w_e2_l):
    shift1, scale1, gate1, shift2, scale2, gate2 = jnp.split(mod, 6, axis=-1)
    y, states = token_mixing(x * (1 + scale1) + shift1, latent, mC0, mn0, mm0, hS0, ctx_k, ctx_v,
                             w_in_l, b_in_l, fbias_l, lb_l, rpb_l, w_branch_l, w_out_l)
    x = layer_norm(DEEPNORM_ALPHA * x + gate1 * y, ln_g_l[0], ln_b_l[0])
    y = hier_moe(x * (1 + scale2) + shift2, w_rg_l, w_re_l, w_e1_l, w_e3_l, w_e2_l)
    x = layer_norm(DEEPNORM_ALPHA * x + gate2 * y, ln_g_l[1], ln_b_l[1])
    return x, states


def setup_inputs(seed: int = 0) -> dict:
    key = jax.random.key(seed)
    ks = jax.random.split(key, 32)
    D = D_MODEL

    def nrm(k, shape, s):
        return jax.random.normal(k, shape, jnp.float32) * s

    return {
        'x_prompt': nrm(ks[0], (BATCH, SEQ, D), 1.0),
        'x_sample': nrm(ks[1], (DEC_BATCH, DEC_SEQ, D), 1.0),
        'c': nrm(ks[2], (DEC_BATCH, D), 1.0),
        'cache_na_k': nrm(ks[3], (DEC_BATCH, DEPTH, NA_HEADS, PAST_LEN, NA_DH), 1.0),
        'cache_na_v': nrm(ks[4], (DEC_BATCH, DEPTH, NA_HEADS, PAST_LEN, NA_DH), 1.0),
        'state_mlstm_C': nrm(ks[5], (DEC_BATCH, DEPTH, 2, MA_HEADS, MA_DK, MA_DV), 0.1),
        'state_mlstm_n': nrm(ks[6], (DEC_BATCH, DEPTH, 2, MA_HEADS, MA_DK), 0.1),
        'state_mlstm_m': nrm(ks[7], (DEC_BATCH, DEPTH, 2, MA_HEADS), 1.0),
        'state_hgrn': nrm(ks[8], (DEC_BATCH, DEPTH, 2, HG_HEADS, HG_DK, HG_DV), 1.0),
        'c_ctx': nrm(ks[9], (D,), 1.0),
        'w_mod': nrm(ks[10], (DEPTH, D, 6 * D), 0.5 * D ** -0.5),
        'b_mod': nrm(ks[11], (DEPTH, 6 * D), 0.02),
        'w_in': nrm(ks[12], (DEPTH, D, N_IN), D ** -0.5),
        'b_in': nrm(ks[13], (DEPTH, N_IN), 0.02),
        'mlstm_fbias': jnp.linspace(3.0, 6.0, MA_HEADS)[None, None, :] + nrm(ks[14], (DEPTH, 2, MA_HEADS), 0.1),
        'hgrn_lb_logits': nrm(ks[15], (2, DEPTH, HG_W), 1.0),
        'na_rpb': nrm(ks[16], (DEPTH, NA_HEADS, 2 * NA_KR_MAX - 1, 2 * NA_KC - 1), 0.1),
        'w_branch': nrm(ks[17], (DEPTH, 3, BRANCH_W, D), DEEPNORM_BETA * BRANCH_W ** -0.5),
        'w_out': nrm(ks[18], (DEPTH, D, D), DEEPNORM_BETA * D ** -0.5),
        'ln_g': 1.0 + nrm(ks[19], (DEPTH, 2, D), 0.02),
        'ln_b': nrm(ks[20], (DEPTH, 2, D), 0.02),
        'w_rg': nrm(ks[21], (DEPTH, D, N_GROUPS), D ** -0.5),
        'w_re': nrm(ks[22], (DEPTH, D, N_EXPERTS), D ** -0.5),
        'w_e1': nrm(ks[23], (DEPTH, N_EXPERTS, D, D_EXPERT), D ** -0.5),
        'w_e3': nrm(ks[24], (DEPTH, N_EXPERTS, D, D_EXPERT), D ** -0.5),
        'w_e2': nrm(ks[25], (DEPTH, N_EXPERTS, D_EXPERT, D), DEEPNORM_BETA * D_EXPERT ** -0.5),
    }


def reference(x_prompt, x_sample, c, cache_na_k, cache_na_v, state_mlstm_C, state_mlstm_n,
              state_mlstm_m, state_hgrn, c_ctx, w_mod, b_mod, w_in, b_in, mlstm_fbias,
              hgrn_lb_logits, na_rpb, w_branch, w_out, ln_g, ln_b, w_rg, w_re, w_e1, w_e3, w_e2):
    f32 = jnp.float32
    lb_cum = jnp.cumsum(jax.nn.softmax(hgrn_lb_logits.astype(f32), axis=1), axis=1)
    lb_all = lb_cum - lb_cum[:, :1]

    xp = x_prompt
    B_ = xp.shape[0]
    zC = jnp.zeros((B_, 2, MA_HEADS, MA_DK, MA_DV), f32)
    zn = jnp.zeros((B_, 2, MA_HEADS, MA_DK), f32)
    zm = jnp.zeros((B_, 2, MA_HEADS), f32)
    zS = jnp.zeros((B_, 2, HG_HEADS, HG_DK, HG_DV), f32)
    ks_, vs_, Cs, ns, ms, Ss = [], [], [], [], [], []
    for l in range(DEPTH):
        mod = (jax.nn.silu(c_ctx) @ w_mod[l] + b_mod[l])[None, None, :]
        xp, st = trunk_layer(xp, mod, False, zC, zn, zm, zS, None, None,
                             w_in[l], b_in[l], mlstm_fbias[l], lb_all[:, l], na_rpb[l], w_branch[l], w_out[l],
                             ln_g[l], ln_b[l], w_rg[l], w_re[l], w_e1[l], w_e3[l], w_e2[l])
        ks_.append(st[0]); vs_.append(st[1]); Cs.append(st[2]); ns.append(st[3]); ms.append(st[4]); Ss.append(st[5])
    dt = x_prompt.dtype
    new_na_k = jnp.stack(ks_, 1)
    new_na_v = jnp.stack(vs_, 1)
    new_mlstm_C = jnp.stack(Cs, 1).astype(dt)
    new_mlstm_n = jnp.stack(ns, 1).astype(dt)
    new_mlstm_m = jnp.stack(ms, 1).astype(dt)
    new_hgrn = jnp.stack(Ss, 1).astype(dt)
    y_prompt = xp

    xs = x_sample
    for l in range(DEPTH):
        mod = (jax.nn.silu(c) @ w_mod[l] + b_mod[l])[:, None, :]
        xs, _ = trunk_layer(xs, mod, True, state_mlstm_C[:, l], state_mlstm_n[:, l], state_mlstm_m[:, l],
                            state_hgrn[:, l], cache_na_k[:, l], cache_na_v[:, l],
                            w_in[l], b_in[l], mlstm_fbias[l], lb_all[:, l], na_rpb[l], w_branch[l], w_out[l],
                            ln_g[l], ln_b[l], w_rg[l], w_re[l], w_e1[l], w_e3[l], w_e2[l])
    y_sample = xs
    return (y_prompt, y_sample, new_na_k, new_na_v, new_mlstm_C, new_mlstm_n, new_mlstm_m, new_hgrn)
```

```python
import functools

import numpy as np
import jax
import jax.numpy as jnp
from jax import lax
from jax.experimental import pallas as pl
from jax.experimental.pallas import tpu as pltpu

F32 = jnp.float32
BF16 = jnp.bfloat16

D_MODEL = 1024
BATCH = 16
SEQ = 256
DEPTH = 2
DEC_BATCH = 4
DEC_SEQ = 1024
PAST_LEN = 256
GRID_W = 64
MA_HEADS = 4
MA_DK = 128
MA_DV = 128
MA_CHUNK = 64
NA_HEADS = 8
NA_DH = 64
NA_KR_MAX = 8
NA_KC = 16
NA_QB = 16
NA_KB = NA_QB + NA_KC
CTX_QBLOCK = 128
HG_HEADS = 4
HG_DK = 128
HG_DV = 128
HG_CHUNK = 32
BRANCH_W = 512
N_GROUPS = 4
EXP_PER_GROUP = 4
N_EXPERTS = N_GROUPS * EXP_PER_GROUP
D_EXPERT = 512
ROPE_BASE = 10000.0
LN_EPS = 1e-5
RMS_EPS = 1e-6
DEEPNORM_ALPHA = (2 * DEPTH) ** 0.25

N_CTX = BATCH * SEQ
N_LAT = DEC_BATCH * DEC_SEQ
N_TOK = N_CTX + N_LAT
N_MODROWS = 8
GATE_COL0 = 4 * BRANCH_W
N_GATES = 4 * MA_HEADS
P_COLS = 9216
MOE_TM = 256
MOE_NT = N_TOK // MOE_TM + N_GROUPS
VMEM_LIMIT = 48 * 1024 * 1024


def _cparams(sem):
    return pltpu.CompilerParams(dimension_semantics=sem, vmem_limit_bytes=VMEM_LIMIT)


def _mod_row(tile, tm):
    return jnp.maximum((tile * tm) // DEC_SEQ - (N_CTX // DEC_SEQ - 1), 0)


def _mod_spec(part, tm):
    return pl.BlockSpec((None, None, 1, D_MODEL), lambda i: (_mod_row(i, tm), part, 0, 0))


def _silu(x):
    return x * jax.nn.sigmoid(x)


def _mod_kernel(c_ref, w_ref, b_ref, o_ref):
    s = _silu(c_ref[...])
    o_ref[...] = jnp.dot(s.astype(BF16), w_ref[...].astype(BF16), preferred_element_type=F32) + b_ref[...]


def _modulation(cs, w_mod, b_mod):
    tn = 1024
    return pl.pallas_call(
        _mod_kernel,
        out_shape=jax.ShapeDtypeStruct((DEPTH, N_MODROWS, 6 * D_MODEL), F32),
        grid=(DEPTH, 6 * D_MODEL // tn),
        in_specs=[pl.BlockSpec((N_MODROWS, D_MODEL), lambda l, j: (0, 0)),
                  pl.BlockSpec((None, D_MODEL, tn), lambda l, j: (l, 0, j)),
                  pl.BlockSpec((None, 1, tn), lambda l, j: (l, 0, j))],
        out_specs=pl.BlockSpec((None, N_MODROWS, tn), lambda l, j: (l, 0, j)),
        compiler_params=_cparams(("arbitrary", "arbitrary")),
        name="modulation",
    )(cs, w_mod, b_mod.reshape(DEPTH, 1, 6 * D_MODEL))


def _prep_kernel(x_ref, sh_ref, sc_ref, h_ref):
    h_ref[...] = (x_ref[...] * (1.0 + sc_ref[...]) + sh_ref[...]).astype(BF16)


def _prep(x, modl):
    tm = 1024
    return pl.pallas_call(
        _prep_kernel,
        out_shape=jax.ShapeDtypeStruct((N_TOK, D_MODEL), BF16),
        grid=(N_TOK // tm,),
        in_specs=[pl.BlockSpec((tm, D_MODEL), lambda i: (i, 0)), _mod_spec(0, tm), _mod_spec(1, tm)],
        out_specs=pl.BlockSpec((tm, D_MODEL), lambda i: (i, 0)),
        compiler_params=_cparams(("arbitrary",)),
        name="prep",
    )(x, modl, modl)


def _inproj_kernel(h_ref, w_ref, b_ref, o_ref):
    o_ref[...] = jnp.dot(h_ref[...], w_ref[...], preferred_element_type=F32) + b_ref[...]


def _inproj(h, w, b):
    tm, tn = 2048, 512
    return pl.pallas_call(
        _inproj_kernel,
        out_shape=jax.ShapeDtypeStruct((N_TOK, P_COLS), F32),
        grid=(N_TOK // tm, P_COLS // tn),
        in_specs=[pl.BlockSpec((tm, D_MODEL), lambda i, j: (i, 0)),
                  pl.BlockSpec((D_MODEL, tn), lambda i, j: (0, j)),
                  pl.BlockSpec((1, tn), lambda i, j: (0, j))],
        out_specs=pl.BlockSpec((tm, tn), lambda i, j: (i, j)),
        compiler_params=_cparams(("arbitrary", "arbitrary")),
        name="inproj",
    )(h, w, b)


def _gates_kernel(h_ref, w_ref, wt_ref, b_ref, bt_ref, gc_ref, gt_ref):
    h = h_ref[...]
    gc_ref[...] = jnp.dot(h, w_ref[...], preferred_element_type=F32) + b_ref[...]
    gt_ref[...] = lax.dot_general(wt_ref[...], h, (((1,), (1,)), ((), ())), preferred_element_type=F32) + bt_ref[...]


def _gates(h, wg, bg):
    tm = 1024
    w = jnp.zeros((D_MODEL, 128), BF16).at[:, :N_GATES].set(wg.astype(BF16))
    b = jnp.zeros((1, 128), F32).at[0, :N_GATES].set(bg)
    return pl.pallas_call(
        _gates_kernel,
        out_shape=(jax.ShapeDtypeStruct((N_TOK, 128), F32), jax.ShapeDtypeStruct((N_GATES, N_TOK), F32)),
        grid=(N_TOK // tm,),
        in_specs=[pl.BlockSpec((tm, D_MODEL), lambda i: (i, 0)),
                  pl.BlockSpec((D_MODEL, 128), lambda i: (0, 0)),
                  pl.BlockSpec((N_GATES, D_MODEL), lambda i: (0, 0)),
                  pl.BlockSpec((1, 128), lambda i: (0, 0)),
                  pl.BlockSpec((N_GATES, 1), lambda i: (0, 0))],
        out_specs=(pl.BlockSpec((tm, 128), lambda i: (i, 0)), pl.BlockSpec((N_GATES, tm), lambda i: (0, i))),
        compiler_params=_cparams(("arbitrary",)),
        name="gates",
    )(h, w, wg.T.astype(BF16), b, bg.reshape(N_GATES, 1))


def _layer_norm(x, g, b):
    mu = jnp.mean(x, -1, keepdims=True)
    xc = x - mu
    var = jnp.mean(xc * xc, -1, keepdims=True)
    return xc * lax.rsqrt(var + LN_EPS) * g + b


def _merge_kernel(a_ref, b_ref, c_ref, ga_ref, gb_ref, gc_ref, x_ref, g1_ref, sh2_ref, sc2_ref,
                  wb_ref, wo_ref, lng_ref, lnb_ref, wr_ref, x1_ref, h2_ref, gate_ref, gid_ref):
    def br(v_ref, g_ref, k):
        return jax.nn.sigmoid(g_ref[...]) * jnp.dot(v_ref[...], wb_ref[k], preferred_element_type=F32)

    mix = br(a_ref, ga_ref, 0) + br(b_ref, gb_ref, 1) + br(c_ref, gc_ref, 2)
    y = jnp.dot(mix.astype(BF16), wo_ref[...], preferred_element_type=F32)
    x1 = _layer_norm(DEEPNORM_ALPHA * x_ref[...] + g1_ref[...] * y, lng_ref[...], lnb_ref[...])
    x1_ref[...] = x1
    h2 = x1 * (1.0 + sc2_ref[...]) + sh2_ref[...]
    h2_ref[...] = h2.astype(BF16)
    logits = jnp.dot(h2, wr_ref[...], preferred_element_type=F32, precision=lax.Precision.HIGHEST)
    lane = lax.broadcasted_iota(jnp.int32, logits.shape, 1)
    neg = -jnp.inf
    lg = jnp.where(lane < N_GROUPS, logits, neg)
    mg = jnp.max(lg, -1, keepdims=True)
    g_sel = jnp.min(jnp.where(lg == mg, lane, 128), -1, keepdims=True)
    p_sel = 1.0 / jnp.sum(jnp.where(lane < N_GROUPS, jnp.exp(lg - mg), 0.0), -1, keepdims=True)
    lo = N_GROUPS + EXP_PER_GROUP * g_sel
    le = jnp.where((lane >= lo) & (lane < lo + EXP_PER_GROUP), logits, neg)
    v1 = jnp.max(le, -1, keepdims=True)
    i1 = jnp.min(jnp.where(le == v1, lane, 128), -1, keepdims=True)
    le2 = jnp.where(lane == i1, neg, le)
    v2 = jnp.max(le2, -1, keepdims=True)
    i2 = jnp.min(jnp.where(le2 == v2, lane, 128), -1, keepdims=True)
    e2 = jnp.exp(v2 - v1)
    w1 = p_sel / (1.0 + e2)
    w2 = p_sel * e2 / (1.0 + e2)
    gate_ref[...] = jnp.where(lane == i1 - lo, w1, jnp.where(lane == i2 - lo, w2, 0.0))
    gid_ref[...] = jnp.broadcast_to(g_sel, logits.shape)


def _merge(a, b, c, proj, x, modl, wb, wo, lng, lnb, wr):
    tm = 512
    tok = lambda i: (i, 0)
    full2 = lambda i: (0, 0)
    return pl.pallas_call(
        _merge_kernel,
        out_shape=(jax.ShapeDtypeStruct((N_TOK, D_MODEL), F32), jax.ShapeDtypeStruct((N_TOK, D_MODEL), BF16),
                   jax.ShapeDtypeStruct((N_TOK, 128), F32), jax.ShapeDtypeStruct((N_TOK, 128), jnp.int32)),
        grid=(N_TOK // tm,),
        in_specs=[pl.BlockSpec((tm, BRANCH_W), tok), pl.BlockSpec((tm, BRANCH_W), tok), pl.BlockSpec((tm, BRANCH_W), tok),
                  pl.BlockSpec((tm, D_MODEL), lambda i: (i, 6)), pl.BlockSpec((tm, D_MODEL), lambda i: (i, 7)),
                  pl.BlockSpec((tm, D_MODEL), lambda i: (i, 8)),
                  pl.BlockSpec((tm, D_MODEL), tok), _mod_spec(2, tm), _mod_spec(3, tm), _mod_spec(4, tm),
                  pl.BlockSpec((3, BRANCH_W, D_MODEL), lambda i: (0, 0, 0)), pl.BlockSpec((D_MODEL, D_MODEL), full2),
                  pl.BlockSpec((1, D_MODEL), full2), pl.BlockSpec((1, D_MODEL), full2),
                  pl.BlockSpec((D_MODEL, 128), full2)],
        out_specs=(pl.BlockSpec((tm, D_MODEL), tok), pl.BlockSpec((tm, D_MODEL), tok),
                   pl.BlockSpec((tm, 128), tok), pl.BlockSpec((tm, 128), tok)),
        compiler_params=_cparams(("arbitrary",)),
        name="merge",
    )(a, b, c, proj, proj, proj, x, modl, modl, modl, wb, wo, lng, lnb, wr)


def _moe_up_kernel(gid_ref, x_ref, gate_ref, w1_ref, w3_ref, hid_ref, w1b, w3b):
    f = pl.program_id(0)
    t = pl.program_id(1)
    first = jnp.logical_or(t == 0, gid_ref[t] != gid_ref[jnp.maximum(t - 1, 0)])

    @pl.when(first)
    def _():
        w1b[...] = w1_ref[...].astype(BF16)
        w3b[...] = w3_ref[...].astype(BF16)

    x = x_ref[...]
    a = jnp.dot(x, w1b[...], preferred_element_type=F32)
    b = jnp.dot(x, w3b[...], preferred_element_type=F32)
    gate = gate_ref[...]
    lane = lax.broadcasted_iota(jnp.int32, gate.shape, 1)
    gcol = jnp.sum(jnp.where(lane == f, gate, 0.0), -1, keepdims=True)
    hid_ref[...] = (_silu(a) * b * gcol).astype(BF16)


def _moe_up(gid, xs, gates, w1, w3):
    tm = MOE_TM
    npad = MOE_NT * tm
    grid_spec = pltpu.PrefetchScalarGridSpec(
        num_scalar_prefetch=1,
        grid=(EXP_PER_GROUP, MOE_NT),
        in_specs=[pl.BlockSpec((tm, D_MODEL), lambda f, t, g: (t, 0)),
                  pl.BlockSpec((tm, 128), lambda f, t, g: (t, 0)),
                  pl.BlockSpec((None, D_MODEL, D_EXPERT), lambda f, t, g: (EXP_PER_GROUP * g[t] + f, 0, 0)),
                  pl.BlockSpec((None, D_MODEL, D_EXPERT), lambda f, t, g: (EXP_PER_GROUP * g[t] + f, 0, 0))],
        out_specs=pl.BlockSpec((tm, D_EXPERT), lambda f, t, g: (t, f)),
        scratch_shapes=[pltpu.VMEM((D_MODEL, D_EXPERT), BF16), pltpu.VMEM((D_MODEL, D_EXPERT), BF16)],
    )
    return pl.pallas_call(
        _moe_up_kernel,
        out_shape=jax.ShapeDtypeStruct((npad, EXP_PER_GROUP * D_EXPERT), BF16),
        grid_spec=grid_spec,
        compiler_params=_cparams(("arbitrary", "arbitrary")),
        name="moe_up",
    )(gid, xs, gates, w1, w3)


def _moe_down_kernel(gid_ref, hid_ref, w2_ref, y_ref, w2b):
    t = pl.program_id(0)
    first = jnp.logical_or(t == 0, gid_ref[t] != gid_ref[jnp.maximum(t - 1, 0)])

    @pl.when(first)
    def _():
        w2b[...] = w2_ref[...].astype(BF16)

    y_ref[...] = jnp.dot(hid_ref[...], w2b[...], preferred_element_type=F32)


def _moe_down(gid, hid, w2g):
    tm = MOE_TM
    npad = MOE_NT * tm
    hw = EXP_PER_GROUP * D_EXPERT
    grid_spec = pltpu.PrefetchScalarGridSpec(
        num_scalar_prefetch=1,
        grid=(MOE_NT,),
        in_specs=[pl.BlockSpec((tm, hw), lambda t, g: (t, 0)),
                  pl.BlockSpec((None, hw, D_MODEL), lambda t, g: (g[t], 0, 0))],
        out_specs=pl.BlockSpec((tm, D_MODEL), lambda t, g: (t, 0)),
        scratch_shapes=[pltpu.VMEM((hw, D_MODEL), BF16)],
    )
    return pl.pallas_call(
        _moe_down_kernel,
        out_shape=jax.ShapeDtypeStruct((npad, D_MODEL), F32),
        grid_spec=grid_spec,
        compiler_params=_cparams(("arbitrary",)),
        name="moe_down",
    )(gid, hid, w2g)


def _moe(h2, gate, gsel, w1, w3, w2):
    tm = MOE_TM
    npad = MOE_NT * tm
    g = gsel[:, 0]
    onehot = (g[:, None] == jnp.arange(N_GROUPS)[None, :]).astype(jnp.int32)
    counts = jnp.sum(onehot, 0)
    rank = jnp.sum((jnp.cumsum(onehot, 0) - onehot) * onehot, 1)
    padded = (counts + tm - 1) // tm * tm
    ends = jnp.cumsum(padded)
    offs = ends - padded
    dest = offs[g] + rank
    src = jnp.zeros((npad,), jnp.int32).at[dest].set(jnp.arange(N_TOK, dtype=jnp.int32))
    valid = jnp.zeros((npad,), F32).at[dest].set(1.0)
    tile_gid = jnp.minimum(jnp.searchsorted(ends, jnp.arange(MOE_NT, dtype=jnp.int32) * tm, side="right"),
                           N_GROUPS - 1).astype(jnp.int32)
    xs = jnp.take(h2, src, axis=0)
    gs = jnp.take(gate, src, axis=0) * valid[:, None]
    hid = _moe_up(tile_gid, xs, gs, w1, w3)
    ys = _moe_down(tile_gid, hid, w2.reshape(N_GROUPS, EXP_PER_GROUP * D_EXPERT, D_MODEL))
    return jnp.take(ys, dest, axis=0)


def _final_kernel(x1_ref, y_ref, g2_ref, lng_ref, lnb_ref, sh_ref, sc_ref, x2_ref, h_ref):
    x2 = _layer_norm(DEEPNORM_ALPHA * x1_ref[...] + g2_ref[...] * y_ref[...], lng_ref[...], lnb_ref[...])
    x2_ref[...] = x2
    h_ref[...] = (x2 * (1.0 + sc_ref[...]) + sh_ref[...]).astype(BF16)


def _final(x1, y, modl, lng, lnb, mod_next):
    tm = 1024
    tok = lambda i: (i, 0)
    full2 = lambda i: (0, 0)
    return pl.pallas_call(
        _final_kernel,
        out_shape=(jax.ShapeDtypeStruct((N_TOK, D_MODEL), F32), jax.ShapeDtypeStruct((N_TOK, D_MODEL), BF16)),
        grid=(N_TOK // tm,),
        in_specs=[pl.BlockSpec((tm, D_MODEL), tok), pl.BlockSpec((tm, D_MODEL), tok), _mod_spec(5, tm),
                  pl.BlockSpec((1, D_MODEL), full2), pl.BlockSpec((1, D_MODEL), full2),
                  _mod_spec(0, tm), _mod_spec(1, tm)],
        out_specs=(pl.BlockSpec((tm, D_MODEL), tok), pl.BlockSpec((tm, D_MODEL), tok)),
        compiler_params=_cparams(("arbitrary",)),
        name="final",
    )(x1, y, modl, lng, lnb, mod_next, mod_next)


def _split_heads(a, n):
    B_, T_, W_ = a.shape
    return a.reshape(B_, T_, n, W_ // n).transpose(0, 2, 1, 3)


def _merge_heads(a):
    B_, n, T_, d = a.shape
    return a.transpose(0, 2, 1, 3).reshape(B_, T_, n * d)


def _flip_t(a):
    return jnp.flip(a, axis=2)


def _to_chunks(a, L):
    B_, H_, T_ = a.shape[:3]
    return jnp.moveaxis(a.reshape(B_, H_, T_ // L, L, *a.shape[3:]), 2, 0)


def _from_chunks(a):
    a = jnp.moveaxis(a, 0, 2)
    return a.reshape(a.shape[0], a.shape[1], -1, *a.shape[4:])


def _axial_rope(x):
    T_ = x.shape[2]
    t = jnp.arange(T_)
    row = (t // GRID_W).astype(F32)
    col = (t % GRID_W).astype(F32)
    half = x.shape[-1] // 2
    inv = ROPE_BASE ** (-jnp.arange(0, half, 2, dtype=F32) / half)

    def rot(xp, pos):
        ang = pos[:, None] * inv[None, :]
        cos, sin = jnp.cos(ang).astype(x.dtype), jnp.sin(ang).astype(x.dtype)
        x1, x2 = xp[..., :half // 2], xp[..., half // 2:]
        return jnp.concatenate([x1 * cos - x2 * sin, x1 * sin + x2 * cos], -1)

    return jnp.concatenate([rot(x[..., :half], row), rot(x[..., half:], col)], -1)


def _mlstm_chunkwise(q, k, v, ig, lf, C0, n0, m0):
    L = MA_CHUNK
    causal = jnp.tril(jnp.ones((L, L), bool))

    def step(carry, inp):
        C, n, m = carry
        qc, kc, vc, ic, fc = inp
        b = jnp.cumsum(fc, axis=-1)
        dmat = jnp.where(causal, b[..., :, None] - b[..., None, :] + ic[..., None, :], -jnp.inf)
        g = b + m[..., None]
        m_t = jnp.maximum(g, jnp.max(dmat, -1))
        w_inter = jnp.exp(g - m_t)
        s = jnp.einsum('bhtd,bhsd->bhts', qc, kc) * jnp.exp(dmat - m_t[..., None])
        num = w_inter[..., None] * jnp.einsum('bhtd,bhde->bhte', qc, C) + jnp.einsum('bhts,bhse->bhte', s, vc)
        den = w_inter * jnp.einsum('bhtd,bhd->bht', qc, n) + jnp.sum(s, -1)
        h = num / jnp.maximum(jnp.abs(den), jnp.exp(-m_t))[..., None]
        m_new = m_t[..., -1]
        decay_state = jnp.exp(b[..., -1] + m - m_new)
        kw = kc * jnp.exp(b[..., -1:] - b + ic - m_new[..., None])[..., None]
        C_new = decay_state[..., None, None] * C + jnp.einsum('bhsd,bhse->bhde', kw, vc)
        n_new = decay_state[..., None] * n + jnp.sum(kw, -2)
        return (C_new, n_new, m_new), h

    xs = (_to_chunks(q, L), _to_chunks(k, L), _to_chunks(v, L), _to_chunks(ig, L), _to_chunks(lf, L))
    (C, n, m), hs = lax.scan(step, (C0.astype(F32), n0.astype(F32), m0.astype(F32)), xs)
    return _from_chunks(hs), C, n, m


def _hgrn2_chunkwise(q, k, i, logf, S0):
    L = HG_CHUNK
    causal = jnp.tril(jnp.ones((L, L), bool))[:, :, None]

    def step(S, inp):
        qc, kc, ic, lc = inp
        A = jnp.cumsum(lc, axis=2)
        o_inter = jnp.einsum('bhtd,bhde->bhte', qc * jnp.exp(A), S)
        diff = jnp.where(causal, A[:, :, :, None, :] - A[:, :, None, :, :], -jnp.inf)
        att = jnp.einsum('bhtd,bhtsd,bhsd->bhts', qc, jnp.exp(diff), kc)
        o = o_inter + jnp.einsum('bhts,bhse->bhte', att, ic)
        A_last = A[:, :, -1]
        S_new = jnp.exp(A_last)[..., None] * S + jnp.einsum('bhsd,bhse->bhde', kc * jnp.exp(A_last[:, :, None] - A), ic)
        return S_new, o

    xs = (_to_chunks(q, L), _to_chunks(k, L), _to_chunks(i, L), _to_chunks(logf, L))
    S, os_ = lax.scan(step, S0.astype(F32), xs)
    return _from_chunks(os_), S


def _context_attention(q, k, v):
    B_, H_, S_, D_ = q.shape
    nb = S_ // CTX_QBLOCK
    qb = jnp.moveaxis(q.reshape(B_, H_, nb, CTX_QBLOCK, D_), 2, 0)

    def one(qblk):
        s = jnp.einsum('bhqd,bhkd->bhqk', qblk, k).astype(F32)
        p = jax.nn.softmax(s, axis=-1).astype(v.dtype)
        return jnp.einsum('bhqk,bhkd->bhqd', p, v)

    o = lax.map(one, qb)
    return jnp.moveaxis(o, 0, 2).reshape(B_, H_, S_, D_)


def _latent_na(q, k, v, ck, cv, rpb):
    B_, H_, T_, D_ = q.shape
    rows = T_ // GRID_W
    kr = min(NA_KR_MAX, rows)
    nj = GRID_W // NA_QB
    qcol = np.arange(GRID_W)
    c0 = np.clip(qcol - NA_KC // 2, 0, GRID_W - NA_KC)
    cb = np.clip(np.arange(nj) * NA_QB - NA_KC // 2, 0, GRID_W - NA_KB)
    col_idx = cb[:, None] + np.arange(NA_KB)[None, :]
    kcol = col_idx[:, None, :]
    c0b = c0.reshape(nj, NA_QB)[:, :, None]
    col_mask = (kcol >= c0b) & (kcol < c0b + NA_KC)
    dc_idx = np.clip(kcol - qcol.reshape(nj, NA_QB)[:, :, None] + NA_KC - 1, 0, 2 * NA_KC - 2)
    rpb_cols = rpb[:, :, dc_idx]
    kg = k.reshape(B_, H_, rows, GRID_W, D_)
    vg = v.reshape(B_, H_, rows, GRID_W, D_)
    q_rows = jnp.moveaxis(q.reshape(B_, H_, rows, nj, NA_QB, D_), 2, 0)
    n_loc = kr * NA_KB

    def one_row(args):
        r, qr = args
        r0 = jnp.clip(r - kr // 2, 0, rows - kr)
        ks = lax.dynamic_slice_in_dim(kg, r0, kr, axis=2)[:, :, :, col_idx]
        vs = lax.dynamic_slice_in_dim(vg, r0, kr, axis=2)[:, :, :, col_idx]
        dr_idx = r0 + jnp.arange(kr) - r + (NA_KR_MAX - 1)
        bias = jnp.transpose(jnp.take(rpb_cols, dr_idx, axis=1), (0, 2, 3, 1, 4)).astype(F32)
        s_loc = jnp.einsum('bhjqd,bhrjkd->bhjqrk', qr, ks).astype(F32) + bias
        s_loc = jnp.where(col_mask[:, :, None, :], s_loc, -jnp.inf)
        s_ctx = jnp.einsum('bhjqd,bhpd->bhjqp', qr, ck).astype(F32)
        s = jnp.concatenate([s_loc.reshape(B_, H_, nj, NA_QB, n_loc), s_ctx], -1)
        p = jax.nn.softmax(s, axis=-1).astype(v.dtype)
        p_loc = p[..., :n_loc].reshape(B_, H_, nj, NA_QB, kr, NA_KB)
        return jnp.einsum('bhjqrk,bhrjkd->bhjqd', p_loc, vs) + jnp.einsum('bhjqp,bhpd->bhjqd', p[..., n_loc:], cv)

    out = lax.map(one_row, (jnp.arange(rows), q_rows))
    return jnp.moveaxis(out, 0, 2).reshape(B_, H_, T_, D_)


def _mixers(pr, gcol, latent, mC0, mn0, mm0, hS0, ctx_k, ctx_v, fbias_l, lb_l, rpb_l):
    W = BRANCH_W
    aq, ak, av, ao = (pr[..., i * W:(i + 1) * W] for i in range(4))
    bq, bk, bv = (pr[..., (4 + i) * W:(5 + i) * W] for i in range(3))
    cf_f, cf_b, cq, ci, cg = (pr[..., (7 + i) * W:(8 + i) * W] for i in range(5))
    ai_f, af_f, ai_b, af_b = (gcol[..., 4 * i:4 * i + 4] for i in range(4))
    q = _split_heads(aq, MA_HEADS)
    k = _split_heads(ak, MA_HEADS) * (MA_DK ** -0.5)
    v = _split_heads(av, MA_HEADS)
    if latent:
        q, k = _axial_rope(q), _axial_rope(k)
    fb = fbias_l.astype(F32)
    ig_f = jnp.swapaxes(ai_f, 1, 2)
    ig_b = jnp.swapaxes(ai_b, 1, 2)
    lf_f = jax.nn.log_sigmoid(jnp.swapaxes(af_f, 1, 2) + fb[0][:, None])
    lf_b = jax.nn.log_sigmoid(jnp.swapaxes(af_b, 1, 2) + fb[1][:, None])
    hf, Cf, nf, mf = _mlstm_chunkwise(q, k, v, ig_f, lf_f, mC0[:, 0], mn0[:, 0], mm0[:, 0])
    hb, Cb, nb_, mb = _mlstm_chunkwise(_flip_t(q), _flip_t(k), _flip_t(v), _flip_t(ig_b), _flip_t(lf_b),
                                       mC0[:, 1], mn0[:, 1], mm0[:, 1])
    a_out = _merge_heads(jax.nn.sigmoid(_split_heads(ao, MA_HEADS)) * (hf + _flip_t(hb)))
    nq = _split_heads(bq, NA_HEADS) * (NA_DH ** -0.5)
    nk = _split_heads(bk, NA_HEADS)
    nv = _split_heads(bv, NA_HEADS)
    if latent:
        b_heads = _latent_na(nq, nk, nv, ctx_k, ctx_v, rpb_l)
    else:
        b_heads = _context_attention(nq, nk, nv)
        ctx_k, ctx_v = nk, nv
    b_out = _merge_heads(b_heads)
    f_f = lb_l[0] + (1.0 - lb_l[0]) * jax.nn.sigmoid(cf_f)
    f_b = lb_l[1] + (1.0 - lb_l[1]) * jax.nn.sigmoid(cf_b)
    f_f, f_b = _split_heads(f_f, HG_HEADS), _split_heads(f_b, HG_HEADS)
    hq = _split_heads(jax.nn.silu(cq), HG_HEADS)
    hi = _split_heads(ci, HG_HEADS)
    of, Sf = _hgrn2_chunkwise(hq, 1.0 - f_f, hi, jnp.log(f_f), hS0[:, 0])
    ob, Sb = _hgrn2_chunkwise(_flip_t(hq), _flip_t(1.0 - f_b), _flip_t(hi), _flip_t(jnp.log(f_b)), hS0[:, 1])
    o = of + _flip_t(ob)
    o = o * lax.rsqrt(jnp.mean(o * o, -1, keepdims=True) + RMS_EPS)
    c_out = _merge_heads(o) * jax.nn.silu(cg)
    states = (ctx_k, ctx_v, jnp.stack([Cf, Cb], 1), jnp.stack([nf, nb_], 1), jnp.stack([mf, mb], 1), jnp.stack([Sf, Sb], 1))
    return a_out.astype(BF16), b_out.astype(BF16), c_out.astype(BF16), states


def kernel(x_prompt, x_sample, c, cache_na_k, cache_na_v, state_mlstm_C, state_mlstm_n, state_mlstm_m, state_hgrn,
           c_ctx, w_mod, b_mod, w_in, b_in, mlstm_fbias, hgrn_lb_logits, na_rpb, w_branch, w_out, ln_g, ln_b,
           w_rg, w_re, w_e1, w_e3, w_e2):
    lb_cum = jnp.cumsum(jax.nn.softmax(hgrn_lb_logits.astype(F32), axis=1), axis=1)
    lb_all = lb_cum - lb_cum[:, :1]

    cs = jnp.zeros((N_MODROWS, D_MODEL), F32).at[0].set(c_ctx).at[1:1 + DEC_BATCH].set(c)
    mod = _modulation(cs, w_mod, b_mod).reshape(DEPTH, N_MODROWS, 6, 1, D_MODEL)

    x = jnp.concatenate([x_prompt.reshape(N_CTX, D_MODEL), x_sample.reshape(N_LAT, D_MODEL)], 0)
    h = _prep(x, mod[0])

    zC = jnp.zeros((BATCH, 2, MA_HEADS, MA_DK, MA_DV), F32)
    zn = jnp.zeros((BATCH, 2, MA_HEADS, MA_DK), F32)
    zm = jnp.zeros((BATCH, 2, MA_HEADS), F32)
    zS = jnp.zeros((BATCH, 2, HG_HEADS, HG_DK, HG_DV), F32)
    sts = []
    for l in range(DEPTH):
        w_main = jnp.concatenate([w_in[l][:, :GATE_COL0], w_in[l][:, GATE_COL0 + N_GATES:]], 1).astype(BF16)
        b_main = jnp.concatenate([b_in[l][:GATE_COL0], b_in[l][GATE_COL0 + N_GATES:]])[None, :]
        proj = _inproj(h, w_main, b_main)
        gcol, _gt = _gates(h, w_in[l][:, GATE_COL0:GATE_COL0 + N_GATES], b_in[l][GATE_COL0:GATE_COL0 + N_GATES])

        a_c, b_c, c_c, st = _mixers(proj[:N_CTX].reshape(BATCH, SEQ, P_COLS), gcol[:N_CTX, :N_GATES].reshape(BATCH, SEQ, N_GATES),
                                    False, zC, zn, zm, zS, None, None, mlstm_fbias[l], lb_all[:, l], na_rpb[l])
        a_l, b_l, c_l, _ = _mixers(proj[N_CTX:].reshape(DEC_BATCH, DEC_SEQ, P_COLS),
                                   gcol[N_CTX:, :N_GATES].reshape(DEC_BATCH, DEC_SEQ, N_GATES), True,
                                   state_mlstm_C[:, l], state_mlstm_n[:, l], state_mlstm_m[:, l], state_hgrn[:, l],
                                   cache_na_k[:, l], cache_na_v[:, l], mlstm_fbias[l], lb_all[:, l], na_rpb[l])
        sts.append(st)
        a = jnp.concatenate([a_c.reshape(N_CTX, BRANCH_W), a_l.reshape(N_LAT, BRANCH_W)], 0)
        b = jnp.concatenate([b_c.reshape(N_CTX, BRANCH_W), b_l.reshape(N_LAT, BRANCH_W)], 0)
        cc = jnp.concatenate([c_c.reshape(N_CTX, BRANCH_W), c_l.reshape(N_LAT, BRANCH_W)], 0)

        wr = jnp.zeros((D_MODEL, 128), F32).at[:, :N_GROUPS].set(w_rg[l]).at[:, N_GROUPS:N_GROUPS + N_EXPERTS].set(w_re[l])
        x1, h2, gate, gsel = _merge(a, b, cc, proj, x, mod[l], w_branch[l].astype(BF16), w_out[l].astype(BF16),
                                    ln_g[l, 0][None], ln_b[l, 0][None], wr)
        y2 = _moe(h2, gate, gsel, w_e1[l], w_e3[l], w_e2[l])
        x, h = _final(x1, y2, mod[l], ln_g[l, 1][None], ln_b[l, 1][None], mod[min(l + 1, DEPTH - 1)])

    dt = x_prompt.dtype
    new_na_k = jnp.stack([s[0] for s in sts], 1)
    new_na_v = jnp.stack([s[1] for s in sts], 1)
    new_C = jnp.stack([s[2] for s in sts], 1).astype(dt)
    new_n = jnp.stack([s[3] for s in sts], 1).astype(dt)
    new_m = jnp.stack([s[4] for s in sts], 1).astype(dt)
    new_S = jnp.stack([s[5] for s in sts], 1).astype(dt)
    y_prompt = x[:N_CTX].reshape(BATCH, SEQ, D_MODEL)
    y_sample = x[N_CTX:].reshape(DEC_BATCH, DEC_SEQ, D_MODEL)
    return (y_prompt, y_sample, new_na_k, new_na_v, new_C, new_n, new_m, new_S)
```

```python
import functools

import numpy as np
import jax
import jax.numpy as jnp
from jax import lax
from jax.experimental import pallas as pl
from jax.experimental.pallas import tpu as pltpu

F32 = jnp.float32
BF16 = jnp.bfloat16

D_MODEL = 1024
BATCH = 16
SEQ = 256
DEPTH = 2
DEC_BATCH = 4
DEC_SEQ = 1024
PAST_LEN = 256
GRID_W = 64
MA_HEADS = 4
MA_DK = 128
MA_DV = 128
MA_CHUNK = 64
NA_HEADS = 8
NA_DH = 64
NA_KR_MAX = 8
NA_KC = 16
NA_QB = 16
NA_KB = NA_QB + NA_KC
CTX_QBLOCK = 128
HG_HEADS = 4
HG_DK = 128
HG_DV = 128
HG_CHUNK = 32
BRANCH_W = 512
N_GROUPS = 4
EXP_PER_GROUP = 4
N_EXPERTS = N_GROUPS * EXP_PER_GROUP
D_EXPERT = 512
ROPE_BASE = 10000.0
LN_EPS = 1e-5
RMS_EPS = 1e-6
DEEPNORM_ALPHA = (2 * DEPTH) ** 0.25

N_CTX = BATCH * SEQ
N_LAT = DEC_BATCH * DEC_SEQ
N_TOK = N_CTX + N_LAT
N_MODROWS = 8
GATE_COL0 = 4 * BRANCH_W
N_GATES = 4 * MA_HEADS
P_COLS = 9216
MOE_TM = 256
MOE_NT = N_TOK // MOE_TM + N_GROUPS
VMEM_LIMIT = 48 * 1024 * 1024


def _cparams(sem):
    return pltpu.CompilerParams(dimension_semantics=sem, vmem_limit_bytes=VMEM_LIMIT)


def _mod_row(tile, tm):
    return jnp.maximum((tile * tm) // DEC_SEQ - (N_CTX // DEC_SEQ - 1), 0)


def _mod_spec(part, tm):
    return pl.BlockSpec((None, None, 1, D_MODEL), lambda i: (_mod_row(i, tm), part, 0, 0))


def _silu(x):
    return x * jax.nn.sigmoid(x)


def _mod_kernel(c_ref, w_ref, b_ref, o_ref):
    s = _silu(c_ref[...])
    o_ref[...] = jnp.dot(s.astype(BF16), w_ref[...].astype(BF16), preferred_element_type=F32) + b_ref[...]


def _modulation(cs, w_mod, b_mod):
    tn = 1024
    return pl.pallas_call(
        _mod_kernel,
        out_shape=jax.ShapeDtypeStruct((DEPTH, N_MODROWS, 6 * D_MODEL), F32),
        grid=(DEPTH, 6 * D_MODEL // tn),
        in_specs=[pl.BlockSpec((N_MODROWS, D_MODEL), lambda l, j: (0, 0)),
                  pl.BlockSpec((None, D_MODEL, tn), lambda l, j: (l, 0, j)),
                  pl.BlockSpec((None, 1, tn), lambda l, j: (l, 0, j))],
        out_specs=pl.BlockSpec((None, N_MODROWS, tn), lambda l, j: (l, 0, j)),
        compiler_params=_cparams(("arbitrary", "arbitrary")),
        name="modulation",
    )(cs, w_mod, b_mod.reshape(DEPTH, 1, 6 * D_MODEL))


def _prep_kernel(x_ref, sh_ref, sc_ref, h_ref):
    h_ref[...] = (x_ref[...] * (1.0 + sc_ref[...]) + sh_ref[...]).astype(BF16)


def _prep(x, modl):
    tm = 1024
    return pl.pallas_call(
        _prep_kernel,
        out_shape=jax.ShapeDtypeStruct((N_TOK, D_MODEL), BF16),
        grid=(N_TOK // tm,),
        in_specs=[pl.BlockSpec((tm, D_MODEL), lambda i: (i, 0)), _mod_spec(0, tm), _mod_spec(1, tm)],
        out_specs=pl.BlockSpec((tm, D_MODEL), lambda i: (i, 0)),
        compiler_params=_cparams(("arbitrary",)),
        name="prep",
    )(x, modl, modl)


def _inproj_kernel(h_ref, w_ref, b_ref, o_ref):
    o_ref[...] = jnp.dot(h_ref[...], w_ref[...], preferred_element_type=F32) + b_ref[...]


def _inproj(h, w, b):
    tm, tn = 2048, 512
    return pl.pallas_call(
        _inproj_kernel,
        out_shape=jax.ShapeDtypeStruct((N_TOK, P_COLS), F32),
        grid=(N_TOK // tm, P_COLS // tn),
        in_specs=[pl.BlockSpec((tm, D_MODEL), lambda i, j: (i, 0)),
                  pl.BlockSpec((D_MODEL, tn), lambda i, j: (0, j)),
                  pl.BlockSpec((1, tn), lambda i, j: (0, j))],
        out_specs=pl.BlockSpec((tm, tn), lambda i, j: (i, j)),
        compiler_params=_cparams(("arbitrary", "arbitrary")),
        name="inproj",
    )(h, w, b)


def _gates_kernel(h_ref, w_ref, wt_ref, b_ref, bt_ref, gc_ref, gt_ref):
    h = h_ref[...]
    gc_ref[...] = jnp.dot(h, w_ref[...], preferred_element_type=F32) + b_ref[...]
    gt_ref[...] = lax.dot_general(wt_ref[...], h, (((1,), (1,)), ((), ())), preferred_element_type=F32) + bt_ref[...]


def _gates(h, wg, bg):
    tm = 1024
    w = jnp.zeros((D_MODEL, 128), BF16).at[:, :N_GATES].set(wg.astype(BF16))
    b = jnp.zeros((1, 128), F32).at[0, :N_GATES].set(bg)
    return pl.pallas_call(
        _gates_kernel,
        out_shape=(jax.ShapeDtypeStruct((N_TOK, 128), F32), jax.ShapeDtypeStruct((N_GATES, N_TOK), F32)),
        grid=(N_TOK // tm,),
        in_specs=[pl.BlockSpec((tm, D_MODEL), lambda i: (i, 0)),
                  pl.BlockSpec((D_MODEL, 128), lambda i: (0, 0)),
                  pl.BlockSpec((N_GATES, D_MODEL), lambda i: (0, 0)),
                  pl.BlockSpec((1, 128), lambda i: (0, 0)),
                  pl.BlockSpec((N_GATES, 1), lambda i: (0, 0))],
        out_specs=(pl.BlockSpec((tm, 128), lambda i: (i, 0)), pl.BlockSpec((N_GATES, tm), lambda i: (0, i))),
        compiler_params=_cparams(("arbitrary",)),
        name="gates",
    )(h, w, wg.T.astype(BF16), b, bg.reshape(N_GATES, 1))


def _layer_norm(x, g, b):
    mu = jnp.mean(x, -1, keepdims=True)
    xc = x - mu
    var = jnp.mean(xc * xc, -1, keepdims=True)
    return xc * lax.rsqrt(var + LN_EPS) * g + b


def _merge_kernel(a_ref, b_ref, c_ref, ga_ref, gb_ref, gc_ref, x_ref, g1_ref, sh2_ref, sc2_ref,
                  wb_ref, wo_ref, lng_ref, lnb_ref, wr_ref, x1_ref, h2_ref, gate_ref, gid_ref):
    def br(v_ref, g_ref, k):
        return jax.nn.sigmoid(g_ref[...]) * jnp.dot(v_ref[...], wb_ref[k], preferred_element_type=F32)

    mix = br(a_ref, ga_ref, 0) + br(b_ref, gb_ref, 1) + br(c_ref, gc_ref, 2)
    y = jnp.dot(mix.astype(BF16), wo_ref[...], preferred_element_type=F32)
    x1 = _layer_norm(DEEPNORM_ALPHA * x_ref[...] + g1_ref[...] * y, lng_ref[...], lnb_ref[...])
    x1_ref[...] = x1
    h2 = x1 * (1.0 + sc2_ref[...]) + sh2_ref[...]
    h2_ref[...] = h2.astype(BF16)
    logits = jnp.dot(h2, wr_ref[...], preferred_element_type=F32, precision=lax.Precision.HIGHEST)
    lane = lax.broadcasted_iota(jnp.int32, logits.shape, 1)
    neg = -jnp.inf
    lg = jnp.where(lane < N_GROUPS, logits, neg)
    mg = jnp.max(lg, -1, keepdims=True)
    g_sel = jnp.min(jnp.where(lg == mg, lane, 128), -1, keepdims=True)
    p_sel = 1.0 / jnp.sum(jnp.where(lane < N_GROUPS, jnp.exp(lg - mg), 0.0), -1, keepdims=True)
    lo = N_GROUPS + EXP_PER_GROUP * g_sel
    le = jnp.where((lane >= lo) & (lane < lo + EXP_PER_GROUP), logits, neg)
    v1 = jnp.max(le, -1, keepdims=True)
    i1 = jnp.min(jnp.where(le == v1, lane, 128), -1, keepdims=True)
    le2 = jnp.where(lane == i1, neg, le)
    v2 = jnp.max(le2, -1, keepdims=True)
    i2 = jnp.min(jnp.where(le2 == v2, lane, 128), -1, keepdims=True)
    e2 = jnp.exp(v2 - v1)
    w1 = p_sel / (1.0 + e2)
    w2 = p_sel * e2 / (1.0 + e2)
    gate_ref[...] = jnp.where(lane == i1 - lo, w1, jnp.where(lane == i2 - lo, w2, 0.0))
    gid_ref[...] = jnp.broadcast_to(g_sel, logits.shape)


def _merge(a, b, c, proj, x, modl, wb, wo, lng, lnb, wr):
    tm = 512
    tok = lambda i: (i, 0)
    full2 = lambda i: (0, 0)
    return pl.pallas_call(
        _merge_kernel,
        out_shape=(jax.ShapeDtypeStruct((N_TOK, D_MODEL), F32), jax.ShapeDtypeStruct((N_TOK, D_MODEL), BF16),
                   jax.ShapeDtypeStruct((N_TOK, 128), F32), jax.ShapeDtypeStruct((N_TOK, 128), jnp.int32)),
        grid=(N_TOK // tm,),
        in_specs=[pl.BlockSpec((tm, BRANCH_W), tok), pl.BlockSpec((tm, BRANCH_W), tok), pl.BlockSpec((tm, BRANCH_W), tok),
                  pl.BlockSpec((tm, D_MODEL), lambda i: (i, 6)), pl.BlockSpec((tm, D_MODEL), lambda i: (i, 7)),
                  pl.BlockSpec((tm, D_MODEL), lambda i: (i, 8)),
                  pl.BlockSpec((tm, D_MODEL), tok), _mod_spec(2, tm), _mod_spec(3, tm), _mod_spec(4, tm),
                  pl.BlockSpec((3, BRANCH_W, D_MODEL), lambda i: (0, 0, 0)), pl.BlockSpec((D_MODEL, D_MODEL), full2),
                  pl.BlockSpec((1, D_MODEL), full2), pl.BlockSpec((1, D_MODEL), full2),
                  pl.BlockSpec((D_MODEL, 128), full2)],
        out_specs=(pl.BlockSpec((tm, D_MODEL), tok), pl.BlockSpec((tm, D_MODEL), tok),
                   pl.BlockSpec((tm, 128), tok), pl.BlockSpec((tm, 128), tok)),
        compiler_params=_cparams(("arbitrary",)),
        name="merge",
    )(a, b, c, proj, proj, proj, x, modl, modl, modl, wb, wo, lng, lnb, wr)


def _moe_up_kernel(gid_ref, x_ref, gate_ref, w1_ref, w3_ref, hid_ref, w1b, w3b):
    f = pl.program_id(0)
    t = pl.program_id(1)
    first = jnp.logical_or(t == 0, gid_ref[t] != gid_ref[jnp.maximum(t - 1, 0)])

    @pl.when(first)
    def _():
        w1b[...] = w1_ref[...].astype(BF16)
        w3b[...] = w3_ref[...].astype(BF16)

    x = x_ref[...]
    a = jnp.dot(x, w1b[...], preferred_element_type=F32)
    b = jnp.dot(x, w3b[...], preferred_element_type=F32)
    gate = gate_ref[...]
    lane = lax.broadcasted_iota(jnp.int32, gate.shape, 1)
    gcol = jnp.sum(jnp.where(lane == f, gate, 0.0), -1, keepdims=True)
    hid_ref[...] = (_silu(a) * b * gcol).astype(BF16)


def _moe_up(gid, xs, gates, w1, w3):
    tm = MOE_TM
    npad = MOE_NT * tm
    grid_spec = pltpu.PrefetchScalarGridSpec(
        num_scalar_prefetch=1,
        grid=(EXP_PER_GROUP, MOE_NT),
        in_specs=[pl.BlockSpec((tm, D_MODEL), lambda f, t, g: (t, 0)),
                  pl.BlockSpec((tm, 128), lambda f, t, g: (t, 0)),
                  pl.BlockSpec((None, D_MODEL, D_EXPERT), lambda f, t, g: (EXP_PER_GROUP * g[t] + f, 0, 0)),
                  pl.BlockSpec((None, D_MODEL, D_EXPERT), lambda f, t, g: (EXP_PER_GROUP * g[t] + f, 0, 0))],
        out_specs=pl.BlockSpec((tm, D_EXPERT), lambda f, t, g: (t, f)),
        scratch_shapes=[pltpu.VMEM((D_MODEL, D_EXPERT), BF16), pltpu.VMEM((D_MODEL, D_EXPERT), BF16)],
    )
    return pl.pallas_call(
        _moe_up_kernel,
        out_shape=jax.ShapeDtypeStruct((npad, EXP_PER_GROUP * D_EXPERT), BF16),
        grid_spec=grid_spec,
        compiler_params=_cparams(("arbitrary", "arbitrary")),
        name="moe_up",
    )(gid, xs, gates, w1, w3)


def _moe_down_kernel(gid_ref, hid_ref, w2_ref, y_ref, w2b):
    t = pl.program_id(0)
    first = jnp.logical_or(t == 0, gid_ref[t] != gid_ref[jnp.maximum(t - 1, 0)])

    @pl.when(first)
    def _():
        w2b[...] = w2_ref[...].astype(BF16)

    y_ref[...] = jnp.dot(hid_ref[...], w2b[...], preferred_element_type=F32)


def _moe_down(gid, hid, w2g):
    tm = MOE_TM
    npad = MOE_NT * tm
    hw = EXP_PER_GROUP * D_EXPERT
    grid_spec = pltpu.PrefetchScalarGridSpec(
        num_scalar_prefetch=1,
        grid=(MOE_NT,),
        in_specs=[pl.BlockSpec((tm, hw), lambda t, g: (t, 0)),
                  pl.BlockSpec((None, hw, D_MODEL), lambda t, g: (g[t], 0, 0))],
        out_specs=pl.BlockSpec((tm, D_MODEL), lambda t, g: (t, 0)),
        scratch_shapes=[pltpu.VMEM((hw, D_MODEL), BF16)],
    )
    return pl.pallas_call(
        _moe_down_kernel,
        out_shape=jax.ShapeDtypeStruct((npad, D_MODEL), F32),
        grid_spec=grid_spec,
        compiler_params=_cparams(("arbitrary",)),
        name="moe_down",
    )(gid, hid, w2g)


def _moe(h2, gate, gsel, w1, w3, w2):
    tm = MOE_TM
    npad = MOE_NT * tm
    g = gsel[:, 0]
    onehot = (g[:, None] == jnp.arange(N_GROUPS)[None, :]).astype(jnp.int32)
    counts = jnp.sum(onehot, 0)
    rank = jnp.sum((jnp.cumsum(onehot, 0) - onehot) * onehot, 1)
    padded = (counts + tm - 1) // tm * tm
    ends = jnp.cumsum(padded)
    offs = ends - padded
    dest = offs[g] + rank
    src = jnp.zeros((npad,), jnp.int32).at[dest].set(jnp.arange(N_TOK, dtype=jnp.int32))
    valid = jnp.zeros((npad,), F32).at[dest].set(1.0)
    starts = jnp.arange(MOE_NT, dtype=jnp.int32) * tm
    tile_gid = jnp.minimum(jnp.sum((ends[None, :] <= starts[:, None]).astype(jnp.int32), 1), N_GROUPS - 1)
    xs = jnp.take(h2, src, axis=0)
    gs = jnp.take(gate, src, axis=0) * valid[:, None]
    hid = _moe_up(tile_gid, xs, gs, w1, w3)
    ys = _moe_down(tile_gid, hid, w2.reshape(N_GROUPS, EXP_PER_GROUP * D_EXPERT, D_MODEL))
    return jnp.take(ys, dest, axis=0)


def _final_kernel(x1_ref, y_ref, g2_ref, lng_ref, lnb_ref, sh_ref, sc_ref, x2_ref, h_ref):
    x2 = _layer_norm(DEEPNORM_ALPHA * x1_ref[...] + g2_ref[...] * y_ref[...], lng_ref[...], lnb_ref[...])
    x2_ref[...] = x2
    h_ref[...] = (x2 * (1.0 + sc_ref[...]) + sh_ref[...]).astype(BF16)


def _final(x1, y, modl, lng, lnb, mod_next):
    tm = 1024
    tok = lambda i: (i, 0)
    full2 = lambda i: (0, 0)
    return pl.pallas_call(
        _final_kernel,
        out_shape=(jax.ShapeDtypeStruct((N_TOK, D_MODEL), F32), jax.ShapeDtypeStruct((N_TOK, D_MODEL), BF16)),
        grid=(N_TOK // tm,),
        in_specs=[pl.BlockSpec((tm, D_MODEL), tok), pl.BlockSpec((tm, D_MODEL), tok), _mod_spec(5, tm),
                  pl.BlockSpec((1, D_MODEL), full2), pl.BlockSpec((1, D_MODEL), full2),
                  _mod_spec(0, tm), _mod_spec(1, tm)],
        out_specs=(pl.BlockSpec((tm, D_MODEL), tok), pl.BlockSpec((tm, D_MODEL), tok)),
        compiler_params=_cparams(("arbitrary",)),
        name="final",
    )(x1, y, modl, lng, lnb, mod_next, mod_next)


_NT = (((1,), (1,)), ((), ()))
_TN = (((0,), (0,)), ((), ()))
HEADS_PER_BLK = 128 // NA_DH
NA_QSCALE = NA_DH ** -0.5


def _softmax_rows(parts):
    m = functools.reduce(jnp.maximum, [jnp.max(s, -1, keepdims=True) for s in parts])
    es = [jnp.exp(s - m) for s in parts]
    inv = 1.0 / functools.reduce(jnp.add, [jnp.sum(e, -1, keepdims=True) for e in es])
    return [e * inv for e in es]


def _ctx_attn_kernel(q_ref, k_ref, v_ref, o_ref, ko_ref, vo_ref):
    q = q_ref[...] * NA_QSCALE
    k = k_ref[...]
    v = v_ref[...]
    outs = []
    for hh in range(HEADS_PER_BLK):
        sl = slice(hh * NA_DH, (hh + 1) * NA_DH)
        kh, vh = k[:, sl], v[:, sl]
        ko_ref[hh] = kh
        vo_ref[hh] = vh
        s = lax.dot_general(q[:, sl].astype(BF16), kh.astype(BF16), _NT, preferred_element_type=F32)
        (p,) = _softmax_rows([s])
        outs.append(jnp.dot(p.astype(BF16), vh.astype(BF16), preferred_element_type=F32))
    o_ref[...] = jnp.concatenate(outs, -1).astype(BF16)


def _ctx_attention(proj):
    nblk = NA_HEADS // HEADS_PER_BLK
    cb = lambda base: (lambda b, j: (b, base + j))
    kv_shape = jax.ShapeDtypeStruct((BATCH, NA_HEADS, SEQ, NA_DH), F32)
    kv_spec = pl.BlockSpec((None, HEADS_PER_BLK, SEQ, NA_DH), lambda b, j: (b, j, 0, 0))
    return pl.pallas_call(
        _ctx_attn_kernel,
        out_shape=(jax.ShapeDtypeStruct((N_CTX, BRANCH_W), BF16), kv_shape, kv_shape),
        grid=(BATCH, nblk),
        in_specs=[pl.BlockSpec((SEQ, 128), cb(16)), pl.BlockSpec((SEQ, 128), cb(20)), pl.BlockSpec((SEQ, 128), cb(24))],
        out_specs=(pl.BlockSpec((SEQ, 128), lambda b, j: (b, j)), kv_spec, kv_spec),
        compiler_params=_cparams(("arbitrary", "arbitrary")),
        name="ctx_attention",
    )(proj, proj, proj)


NA_ROWS = DEC_SEQ // GRID_W
NA_KR = min(NA_KR_MAX, NA_ROWS)


def _na_bias_table(rpb):
    r = np.arange(NA_ROWS)
    r0 = np.clip(r - NA_KR // 2, 0, NA_ROWS - NA_KR)
    dr = r0[:, None] + np.arange(NA_KR)[None, :] - r[:, None] + (NA_KR_MAX - 1)
    c = np.arange(GRID_W)
    c0 = np.clip(c - NA_KC // 2, 0, GRID_W - NA_KC)
    kc = np.arange(GRID_W)
    valid = (kc[None, :] >= c0[:, None]) & (kc[None, :] < c0[:, None] + NA_KC)
    dc = np.clip(kc[None, :] - c[:, None] + NA_KC - 1, 0, 2 * NA_KC - 2)
    tab = rpb[:, dr[:, None, :, None], dc[None, :, None, :]]
    tab = jnp.where(valid[None, None, :, None, :], tab.astype(F32), -jnp.inf)
    return tab.reshape(NA_HEADS, NA_ROWS, GRID_W, NA_KR * GRID_W)


def _lat_attn_kernel(q_ref, k_ref, v_ref, ck_ref, cv_ref, bias_ref, o_ref):
    q = (q_ref[...] * NA_QSCALE).astype(BF16)
    k = k_ref[...].astype(BF16)
    v = v_ref[...].astype(BF16)
    for r in range(NA_ROWS):
        r0 = min(max(r - NA_KR // 2, 0), NA_ROWS - NA_KR)
        qs = slice(r * GRID_W, (r + 1) * GRID_W)
        ws = slice(r0 * GRID_W, (r0 + NA_KR) * GRID_W)
        outs = []
        for hh in range(HEADS_PER_BLK):
            sl = slice(hh * NA_DH, (hh + 1) * NA_DH)
            qh = q[qs, sl]
            s_loc = lax.dot_general(qh, k[ws, sl], _NT, preferred_element_type=F32) + bias_ref[hh, r]
            s_ctx = lax.dot_general(qh, ck_ref[hh].astype(BF16), _NT, preferred_element_type=F32)
            p_loc, p_ctx = _softmax_rows([s_loc, s_ctx])
            outs.append(jnp.dot(p_loc.astype(BF16), v[ws, sl], preferred_element_type=F32)
                        + jnp.dot(p_ctx.astype(BF16), cv_ref[hh].astype(BF16), preferred_element_type=F32))
        o_ref[qs, :] = jnp.concatenate(outs, -1).astype(BF16)


def _lat_attention(proj, ck, cv, bias):
    nblk = NA_HEADS // HEADS_PER_BLK
    rb0 = N_CTX // DEC_SEQ
    cb = lambda base: (lambda j, b: (rb0 + b, base + j))
    c_spec = pl.BlockSpec((None, HEADS_PER_BLK, PAST_LEN, NA_DH), lambda j, b: (b, j, 0, 0))
    return pl.pallas_call(
        _lat_attn_kernel,
        out_shape=jax.ShapeDtypeStruct((N_LAT, BRANCH_W), BF16),
        grid=(nblk, DEC_BATCH),
        in_specs=[pl.BlockSpec((DEC_SEQ, 128), cb(16)), pl.BlockSpec((DEC_SEQ, 128), cb(20)),
                  pl.BlockSpec((DEC_SEQ, 128), cb(24)), c_spec, c_spec,
                  pl.BlockSpec((HEADS_PER_BLK, NA_ROWS, GRID_W, NA_KR * GRID_W), lambda j, b: (j, 0, 0, 0))],
        out_specs=pl.BlockSpec((DEC_SEQ, 128), lambda j, b: (b, j)),
        compiler_params=_cparams(("arbitrary", "arbitrary")),
        name="lat_attention",
    )(proj, proj, proj, ck, cv, bias)


HI = lax.Precision.HIGHEST
MA_KSCALE = MA_DK ** -0.5


def _log_sigmoid(x):
    return jnp.minimum(x, 0.0) - jnp.log(1.0 + jnp.exp(-jnp.abs(x)))


def _tri(n, upper):
    r = lax.broadcasted_iota(jnp.int32, (n, n), 0)
    c = lax.broadcasted_iota(jnp.int32, (n, n), 1)
    return jnp.where((r <= c) if upper else (r >= c), 1.0, 0.0).astype(F32)


def _rope_tables(T):
    t = np.arange(T)
    half = MA_DK // 2
    inv = ROPE_BASE ** (-jnp.arange(0, half, 2, dtype=F32) / half)
    ang_r = jnp.asarray((t // GRID_W).astype(np.float32))[:, None] * inv[None, :]
    ang_c = jnp.asarray((t % GRID_W).astype(np.float32))[:, None] * inv[None, :]
    cos = jnp.concatenate([jnp.cos(ang_r)] * 2 + [jnp.cos(ang_c)] * 2, -1)
    sin = jnp.concatenate([-jnp.sin(ang_r), jnp.sin(ang_r), -jnp.sin(ang_c), jnp.sin(ang_c)], -1)
    return cos, sin


def _mlstm_kernel(*refs, T, latent):
    if latent:
        (p_ref, gc_ref, gt_ref, fbc_ref, fbr_ref, cos_ref, sin_ref, c0_ref, n0_ref, m0_ref,
         a_ref, qs, ks, vs, hf, hb, Cs, ns, ms) = refs
    else:
        (p_ref, gc_ref, gt_ref, fbc_ref, fbr_ref, a_ref, co_ref, no_ref, mo_ref,
         qs, ks, vs, hf, hb, Cs, ns, ms) = refs
    L = MA_CHUNK
    NC = T // L
    W = BRANCH_W

    lane = lax.broadcasted_iota(jnp.int32, (T, MA_DK), 1)
    lo_half = (lane % (MA_DK // 2)) < (MA_DK // 4)

    def rope(x):
        if not latent:
            return x
        swapped = jnp.where(lo_half, pltpu.roll(x, MA_DK - MA_DK // 4, 1), pltpu.roll(x, MA_DK // 4, 1))
        return x * cos_ref[...] + swapped * sin_ref[...]

    for h in range(MA_HEADS):
        hs = slice(h * MA_DK, (h + 1) * MA_DK)
        qs[:, hs] = rope(p_ref[:, hs]).astype(BF16)
        ks[:, hs] = rope(p_ref[:, W + h * MA_DK:W + (h + 1) * MA_DK] * MA_KSCALE).astype(BF16)
    vs[...] = p_ref[:, 2 * W:3 * W].astype(BF16)

    for d in range(2):
        for h in range(MA_HEADS):
            Cs[d, h] = c0_ref[d, h] if latent else jnp.zeros((MA_DK, MA_DV), F32)
    ns[...] = n0_ref[...] if latent else jnp.zeros_like(ns)
    ms[...] = m0_ref[...] if latent else jnp.zeros_like(ms)

    low = _tri(L, False)
    upp = _tri(L, True)
    rr = lax.broadcasted_iota(jnp.int32, (L, L), 0)
    cc = lax.broadcasted_iota(jnp.int32, (L, L), 1)
    fbc = fbc_ref[...]
    fbr = fbr_ref[...]

    def chunk(c, d):
        t0 = pl.multiple_of(c * L, L)
        gc = gc_ref[pl.ds(t0, L), :]
        gt = gt_ref[c]
        lfc = _log_sigmoid(gc + fbc)
        lfr = _log_sigmoid(gt + fbr)
        if d == 0:
            bcol = jnp.dot(low, lfc, precision=HI, preferred_element_type=F32)
            brow = jnp.dot(lfr, upp, precision=HI, preferred_element_type=F32)
        else:
            bcol = jnp.dot(upp, lfc, precision=HI, preferred_element_type=F32)
            brow = jnp.dot(lfr, low, precision=HI, preferred_element_type=F32)
        mask = (cc <= rr) if d == 0 else (cc >= rr)
        last = L - 1 if d == 0 else 0
        for h in range(MA_HEADS):
            gi = 2 * d * MA_HEADS + h
            gf = gi + MA_HEADS
            sidx = d * MA_HEADS + h
            hs = slice(h * MA_DK, (h + 1) * MA_DK)
            bc = bcol[:, gf:gf + 1]
            br = brow[gf:gf + 1, :]
            ir = gt[gi:gi + 1, :]
            ic = gc[:, gi:gi + 1]
            m = ms[pl.ds(sidx, 1), :][:, 0:1]
            n = ns[pl.ds(sidx, 1), :]
            C = Cs[d, h]
            q = qs[pl.ds(t0, L), hs]
            k = ks[pl.ds(t0, L), hs]
            v = vs[pl.ds(t0, L), hs]
            dmat = jnp.where(mask, bc - br + ir, -jnp.inf)
            g = bc + m
            m_t = jnp.maximum(g, jnp.max(dmat, -1, keepdims=True))
            w_inter = jnp.exp(g - m_t)
            s = lax.dot_general(q, k, _NT, preferred_element_type=F32) * jnp.exp(dmat - m_t)
            num = (w_inter * jnp.dot(q, C.astype(BF16), preferred_element_type=F32)
                   + jnp.dot(s.astype(BF16), v, preferred_element_type=F32))
            den = w_inter * jnp.sum(q.astype(F32) * n, -1, keepdims=True) + jnp.sum(s, -1, keepdims=True)
            hout = num / jnp.maximum(jnp.abs(den), jnp.exp(-m_t))
            (hf if d == 0 else hb)[pl.ds(t0, L), hs] = hout
            m_new = m_t[last:last + 1, :]
            b_last = bc[last:last + 1, :]
            decay = jnp.exp(b_last + m - m_new)
            kw = k.astype(F32) * jnp.exp(b_last - bc + ic - m_new)
            Cs[d, h] = decay * C + lax.dot_general(kw.astype(BF16), v, _TN, preferred_element_type=F32)
            ns[pl.ds(sidx, 1), :] = decay * n + jnp.sum(kw, 0, keepdims=True)
            ms[pl.ds(sidx, 1), :] = jnp.broadcast_to(m_new, (1, 128))

    def body(i, carry):
        chunk(i, 0)
        chunk(NC - 1 - i, 1)
        return carry

    lax.fori_loop(0, NC, body, 0)

    a_ref[...] = (jax.nn.sigmoid(p_ref[:, 3 * W:4 * W]) * (hf[...] + hb[...])).astype(BF16)
    if not latent:
        for d in range(2):
            for h in range(MA_HEADS):
                co_ref[d, h] = Cs[d, h]
        no_ref[...] = ns[...]
        mo_ref[...] = ms[...]


def _mlstm(proj, gcol, gt3, fbias_l, latent, C0=None, n0=None, m0=None):
    T = DEC_SEQ if latent else SEQ
    B = DEC_BATCH if latent else BATCH
    rb0 = N_CTX // DEC_SEQ if latent else 0
    fb = fbias_l.astype(F32)
    fbc = jnp.zeros((1, 128), F32).at[0, MA_HEADS:2 * MA_HEADS].set(fb[0]).at[0, 3 * MA_HEADS:4 * MA_HEADS].set(fb[1])
    fbr = fbc[0, :N_GATES].reshape(N_GATES, 1)
    full2 = lambda b: (0, 0)
    in_specs = [pl.BlockSpec((T, 4 * BRANCH_W), lambda b: (rb0 + b, 0)),
                pl.BlockSpec((T, 128), lambda b: (rb0 + b, 0)),
                pl.BlockSpec((T // MA_CHUNK, N_GATES, MA_CHUNK), lambda b: (rb0 + b, 0, 0)),
                pl.BlockSpec((1, 128), full2), pl.BlockSpec((N_GATES, 1), full2)]
    args = [proj, gcol, gt3, fbc, fbr]
    a_shape = jax.ShapeDtypeStruct((B * T, BRANCH_W), BF16)
    a_spec = pl.BlockSpec((T, BRANCH_W), lambda b: (b, 0))
    c_spec = pl.BlockSpec((None, 2, MA_HEADS, MA_DK, MA_DV), lambda b: (b, 0, 0, 0, 0))
    nm_spec = pl.BlockSpec((None, 2 * MA_HEADS, 128), lambda b: (b, 0, 0))
    if latent:
        cos, sin = _rope_tables(T)
        in_specs += [pl.BlockSpec((T, MA_DK), full2), pl.BlockSpec((T, MA_DK), full2), c_spec, nm_spec, nm_spec]
        args += [cos, sin, C0, n0.reshape(B, 2 * MA_HEADS, MA_DK),
                 jnp.broadcast_to(m0.reshape(B, 2 * MA_HEADS, 1), (B, 2 * MA_HEADS, 128))]
        out_shape, out_specs = a_shape, a_spec
    else:
        nm_shape = jax.ShapeDtypeStruct((B, 2 * MA_HEADS, 128), F32)
        out_shape = (a_shape, jax.ShapeDtypeStruct((B, 2, MA_HEADS, MA_DK, MA_DV), F32), nm_shape, nm_shape)
        out_specs = (a_spec, c_spec, nm_spec, nm_spec)
    scratch = [pltpu.VMEM((T, BRANCH_W), BF16)] * 3 + [pltpu.VMEM((T, BRANCH_W), F32)] * 2 + [
        pltpu.VMEM((2, MA_HEADS, MA_DK, MA_DV), F32), pltpu.VMEM((2 * MA_HEADS, MA_DK), F32),
        pltpu.VMEM((2 * MA_HEADS, 128), F32)]
    return pl.pallas_call(
        functools.partial(_mlstm_kernel, T=T, latent=latent),
        out_shape=out_shape, grid=(B,), in_specs=in_specs, out_specs=out_specs, scratch_shapes=scratch,
        compiler_params=_cparams(("arbitrary",)),
        name="mlstm_lat" if latent else "mlstm_ctx",
    )(*args)


HG_SUB = 8


def _hgrn_kernel(*refs, T, latent):
    if latent:
        ff_ref, fb_ref, q_ref, i_ref, g_ref, lbf_ref, lbb_ref, s0_ref, c_ref, of, ob, ST = refs
    else:
        ff_ref, fb_ref, q_ref, i_ref, g_ref, lbf_ref, lbb_ref, c_ref, so_ref, of, ob, ST = refs
    L = HG_CHUNK
    NC = T // L
    NB = L // HG_SUB
    DK = HG_DK

    for d in range(2):
        for h in range(HG_HEADS):
            ST[d, h] = s0_ref[d, h].T if latent else jnp.zeros((HG_DV, DK), F32)

    low = _tri(L, False)
    upp = _tri(L, True)
    row8 = lax.broadcasted_iota(jnp.int32, (HG_SUB, L), 0)
    lane_s = lax.broadcasted_iota(jnp.int32, (HG_SUB, L), 1)

    def chunk(c, d):
        t0 = pl.multiple_of(c * L, L)
        fpre = (ff_ref if d == 0 else fb_ref)[pl.ds(t0, L), :]
        lb = (lbf_ref if d == 0 else lbb_ref)[...]
        f = lb + (1.0 - lb) * jax.nn.sigmoid(fpre)
        logf = jnp.log(f)
        kk = 1.0 - f
        qq = _silu(q_ref[pl.ds(t0, L), :])
        iv = i_ref[pl.ds(t0, L), :].astype(BF16)
        A_all = jnp.dot(low if d == 0 else upp, logf, precision=HI, preferred_element_type=F32)
        last = L - 1 if d == 0 else 0
        for h in range(HG_HEADS):
            hs = slice(h * DK, (h + 1) * DK)
            A, k, q, ivh = A_all[:, hs], kk[:, hs], qq[:, hs], iv[:, hs]
            st = ST[d, h]
            o = lax.dot_general((q * jnp.exp(A)).astype(BF16), st.astype(BF16), _NT, preferred_element_type=F32)
            a_last = A[last:last + 1, :]
            kd = k * jnp.exp(a_last - A)
            rows = []
            for I in range(NB):
                bs = slice(I * HG_SUB, (I + 1) * HG_SUB)
                A_I, q_I = A[bs], q[bs]
                if d == 0:
                    has_off, ref_row, off_mask = I > 0, I * HG_SUB - 1, lane_s < I * HG_SUB
                else:
                    has_off, ref_row, off_mask = I < NB - 1, (I + 1) * HG_SUB, lane_s >= (I + 1) * HG_SUB
                att = jnp.zeros((HG_SUB, L), F32)
                if has_off:
                    R = A[ref_row:ref_row + 1, :]
                    qsc = (q_I * jnp.exp(A_I - R)).astype(BF16)
                    ksc = (k * jnp.exp(jnp.minimum(R - A, 0.0))).astype(BF16)
                    att = jnp.where(off_mask, lax.dot_general(qsc, ksc, _NT, preferred_element_type=F32), 0.0)
                for j in range(HG_SUB):
                    s = I * HG_SUB + j
                    e = jnp.exp(jnp.minimum(A_I - A[s:s + 1, :], 0.0))
                    col = jnp.sum(q_I * k[s:s + 1, :] * e, -1, keepdims=True)
                    keep = (lane_s == s) & ((row8 >= j) if d == 0 else (row8 <= j))
                    att = jnp.where(keep, col, att)
                rows.append(att)
            att = jnp.concatenate(rows, 0)
            o = o + jnp.dot(att.astype(BF16), ivh, preferred_element_type=F32)
            (of if d == 0 else ob)[pl.ds(t0, L), hs] = o
            ST[d, h] = st * jnp.exp(a_last) + lax.dot_general(ivh, kd.astype(BF16), _TN, preferred_element_type=F32)

    def body(i, carry):
        chunk(i, 0)
        chunk(NC - 1 - i, 1)
        return carry

    lax.fori_loop(0, NC, body, 0)

    def epilogue(r, carry):
        t0 = pl.multiple_of(r * 128, 128)
        o = of[pl.ds(t0, 128), :] + ob[pl.ds(t0, 128), :]
        gsil = _silu(g_ref[pl.ds(t0, 128), :])
        outs = []
        for h in range(HG_HEADS):
            oh = o[:, h * HG_DV:(h + 1) * HG_DV]
            outs.append(oh * lax.rsqrt(jnp.mean(oh * oh, -1, keepdims=True) + RMS_EPS))
        c_ref[pl.ds(t0, 128), :] = (jnp.concatenate(outs, -1) * gsil).astype(BF16)
        return carry

    lax.fori_loop(0, T // 128, epilogue, 0)
    if not latent:
        for d in range(2):
            for h in range(HG_HEADS):
                so_ref[d, h] = ST[d, h].T


def _hgrn(proj, lb_l, latent, S0=None):
    T = DEC_SEQ if latent else SEQ
    B = DEC_BATCH if latent else BATCH
    rb0 = N_CTX // DEC_SEQ if latent else 0
    W = BRANCH_W
    full2 = lambda b: (0, 0)
    col = lambda j: pl.BlockSpec((T, W), lambda b: (rb0 + b, j))
    s_spec = pl.BlockSpec((None, 2, HG_HEADS, HG_DK, HG_DV), lambda b: (b, 0, 0, 0, 0))
    in_specs = [col(7), col(8), col(9), col(10), col(11), pl.BlockSpec((1, W), full2), pl.BlockSpec((1, W), full2)]
    args = [proj] * 5 + [lb_l[0][None, :], lb_l[1][None, :]]
    c_shape = jax.ShapeDtypeStruct((B * T, W), BF16)
    c_spec = pl.BlockSpec((T, W), lambda b: (b, 0))
    if latent:
        in_specs.append(s_spec)
        args.append(S0)
        out_shape, out_specs = c_shape, c_spec
    else:
        out_shape = (c_shape, jax.ShapeDtypeStruct((B, 2, HG_HEADS, HG_DK, HG_DV), F32))
        out_specs = (c_spec, s_spec)
    scratch = [pltpu.VMEM((T, W), F32), pltpu.VMEM((T, W), F32), pltpu.VMEM((2, HG_HEADS, HG_DV, HG_DK), F32)]
    return pl.pallas_call(
        functools.partial(_hgrn_kernel, T=T, latent=latent),
        out_shape=out_shape, grid=(B,), in_specs=in_specs, out_specs=out_specs, scratch_shapes=scratch,
        compiler_params=_cparams(("arbitrary",)),
        name="hgrn_lat" if latent else "hgrn_ctx",
    )(*args)


def _split_heads(a, n):
    B_, T_, W_ = a.shape
    return a.reshape(B_, T_, n, W_ // n).transpose(0, 2, 1, 3)


def _merge_heads(a):
    B_, n, T_, d = a.shape
    return a.transpose(0, 2, 1, 3).reshape(B_, T_, n * d)


def _flip_t(a):
    return jnp.flip(a, axis=2)


def _to_chunks(a, L):
    B_, H_, T_ = a.shape[:3]
    return jnp.moveaxis(a.reshape(B_, H_, T_ // L, L, *a.shape[3:]), 2, 0)


def _from_chunks(a):
    a = jnp.moveaxis(a, 0, 2)
    return a.reshape(a.shape[0], a.shape[1], -1, *a.shape[4:])


def _axial_rope(x):
    T_ = x.shape[2]
    t = jnp.arange(T_)
    row = (t // GRID_W).astype(F32)
    col = (t % GRID_W).astype(F32)
    half = x.shape[-1] // 2
    inv = ROPE_BASE ** (-jnp.arange(0, half, 2, dtype=F32) / half)

    def rot(xp, pos):
        ang = pos[:, None] * inv[None, :]
        cos, sin = jnp.cos(ang).astype(x.dtype), jnp.sin(ang).astype(x.dtype)
        x1, x2 = xp[..., :half // 2], xp[..., half // 2:]
        return jnp.concatenate([x1 * cos - x2 * sin, x1 * sin + x2 * cos], -1)

    return jnp.concatenate([rot(x[..., :half], row), rot(x[..., half:], col)], -1)


def _mlstm_chunkwise(q, k, v, ig, lf, C0, n0, m0):
    L = MA_CHUNK
    causal = jnp.tril(jnp.ones((L, L), bool))

    def step(carry, inp):
        C, n, m = carry
        qc, kc, vc, ic, fc = inp
        b = jnp.cumsum(fc, axis=-1)
        dmat = jnp.where(causal, b[..., :, None] - b[..., None, :] + ic[..., None, :], -jnp.inf)
        g = b + m[..., None]
        m_t = jnp.maximum(g, jnp.max(dmat, -1))
        w_inter = jnp.exp(g - m_t)
        s = jnp.einsum('bhtd,bhsd->bhts', qc, kc) * jnp.exp(dmat - m_t[..., None])
        num = w_inter[..., None] * jnp.einsum('bhtd,bhde->bhte', qc, C) + jnp.einsum('bhts,bhse->bhte', s, vc)
        den = w_inter * jnp.einsum('bhtd,bhd->bht', qc, n) + jnp.sum(s, -1)
        h = num / jnp.maximum(jnp.abs(den), jnp.exp(-m_t))[..., None]
        m_new = m_t[..., -1]
        decay_state = jnp.exp(b[..., -1] + m - m_new)
        kw = kc * jnp.exp(b[..., -1:] - b + ic - m_new[..., None])[..., None]
        C_new = decay_state[..., None, None] * C + jnp.einsum('bhsd,bhse->bhde', kw, vc)
        n_new = decay_state[..., None] * n + jnp.sum(kw, -2)
        return (C_new, n_new, m_new), h

    xs = (_to_chunks(q, L), _to_chunks(k, L), _to_chunks(v, L), _to_chunks(ig, L), _to_chunks(lf, L))
    (C, n, m), hs = lax.scan(step, (C0.astype(F32), n0.astype(F32), m0.astype(F32)), xs)
    return _from_chunks(hs), C, n, m


def _hgrn2_chunkwise(q, k, i, logf, S0):
    L = HG_CHUNK
    causal = jnp.tril(jnp.ones((L, L), bool))[:, :, None]

    def step(S, inp):
        qc, kc, ic, lc = inp
        A = jnp.cumsum(lc, axis=2)
        o_inter = jnp.einsum('bhtd,bhde->bhte', qc * jnp.exp(A), S)
        diff = jnp.where(causal, A[:, :, :, None, :] - A[:, :, None, :, :], -jnp.inf)
        att = jnp.einsum('bhtd,bhtsd,bhsd->bhts', qc, jnp.exp(diff), kc)
        o = o_inter + jnp.einsum('bhts,bhse->bhte', att, ic)
        A_last = A[:, :, -1]
        S_new = jnp.exp(A_last)[..., None] * S + jnp.einsum('bhsd,bhse->bhde', kc * jnp.exp(A_last[:, :, None] - A), ic)
        return S_new, o

    xs = (_to_chunks(q, L), _to_chunks(k, L), _to_chunks(i, L), _to_chunks(logf, L))
    S, os_ = lax.scan(step, S0.astype(F32), xs)
    return _from_chunks(os_), S


def _context_attention(q, k, v):
    B_, H_, S_, D_ = q.shape
    nb = S_ // CTX_QBLOCK
    qb = jnp.moveaxis(q.reshape(B_, H_, nb, CTX_QBLOCK, D_), 2, 0)

    def one(qblk):
        s = jnp.einsum('bhqd,bhkd->bhqk', qblk, k).astype(F32)
        p = jax.nn.softmax(s, axis=-1).astype(v.dtype)
        return jnp.einsum('bhqk,bhkd->bhqd', p, v)

    o = lax.map(one, qb)
    return jnp.moveaxis(o, 0, 2).reshape(B_, H_, S_, D_)


def _latent_na(q, k, v, ck, cv, rpb):
    B_, H_, T_, D_ = q.shape
    rows = T_ // GRID_W
    kr = min(NA_KR_MAX, rows)
    nj = GRID_W // NA_QB
    qcol = np.arange(GRID_W)
    c0 = np.clip(qcol - NA_KC // 2, 0, GRID_W - NA_KC)
    cb = np.clip(np.arange(nj) * NA_QB - NA_KC // 2, 0, GRID_W - NA_KB)
    col_idx = cb[:, None] + np.arange(NA_KB)[None, :]
    kcol = col_idx[:, None, :]
    c0b = c0.reshape(nj, NA_QB)[:, :, None]
    col_mask = (kcol >= c0b) & (kcol < c0b + NA_KC)
    dc_idx = np.clip(kcol - qcol.reshape(nj, NA_QB)[:, :, None] + NA_KC - 1, 0, 2 * NA_KC - 2)
    rpb_cols = rpb[:, :, dc_idx]
    kg = k.reshape(B_, H_, rows, GRID_W, D_)
    vg = v.reshape(B_, H_, rows, GRID_W, D_)
    q_rows = jnp.moveaxis(q.reshape(B_, H_, rows, nj, NA_QB, D_), 2, 0)
    n_loc = kr * NA_KB

    def one_row(args):
        r, qr = args
        r0 = jnp.clip(r - kr // 2, 0, rows - kr)
        ks = lax.dynamic_slice_in_dim(kg, r0, kr, axis=2)[:, :, :, col_idx]
        vs = lax.dynamic_slice_in_dim(vg, r0, kr, axis=2)[:, :, :, col_idx]
        dr_idx = r0 + jnp.arange(kr) - r + (NA_KR_MAX - 1)
        bias = jnp.transpose(jnp.take(rpb_cols, dr_idx, axis=1), (0, 2, 3, 1, 4)).astype(F32)
        s_loc = jnp.einsum('bhjqd,bhrjkd->bhjqrk', qr, ks).astype(F32) + bias
        s_loc = jnp.where(col_mask[:, :, None, :], s_loc, -jnp.inf)
        s_ctx = jnp.einsum('bhjqd,bhpd->bhjqp', qr, ck).astype(F32)
        s = jnp.concatenate([s_loc.reshape(B_, H_, nj, NA_QB, n_loc), s_ctx], -1)
        p = jax.nn.softmax(s, axis=-1).astype(v.dtype)
        p_loc = p[..., :n_loc].reshape(B_, H_, nj, NA_QB, kr, NA_KB)
        return jnp.einsum('bhjqrk,bhrjkd->bhjqd', p_loc, vs) + jnp.einsum('bhjqp,bhpd->bhjqd', p[..., n_loc:], cv)

    out = lax.map(one_row, (jnp.arange(rows), q_rows))
    return jnp.moveaxis(out, 0, 2).reshape(B_, H_, T_, D_)


def _mixers(pr, gcol, latent, mC0, mn0, mm0, hS0, ctx_k, ctx_v, fbias_l, lb_l, rpb_l):
    W = BRANCH_W
    aq, ak, av, ao = (pr[..., i * W:(i + 1) * W] for i in range(4))
    bq, bk, bv = (pr[..., (4 + i) * W:(5 + i) * W] for i in range(3))
    cf_f, cf_b, cq, ci, cg = (pr[..., (7 + i) * W:(8 + i) * W] for i in range(5))
    ai_f, af_f, ai_b, af_b = (gcol[..., 4 * i:4 * i + 4] for i in range(4))
    q = _split_heads(aq, MA_HEADS)
    k = _split_heads(ak, MA_HEADS) * (MA_DK ** -0.5)
    v = _split_heads(av, MA_HEADS)
    if latent:
        q, k = _axial_rope(q), _axial_rope(k)
    fb = fbias_l.astype(F32)
    ig_f = jnp.swapaxes(ai_f, 1, 2)
    ig_b = jnp.swapaxes(ai_b, 1, 2)
    lf_f = jax.nn.log_sigmoid(jnp.swapaxes(af_f, 1, 2) + fb[0][:, None])
    lf_b = jax.nn.log_sigmoid(jnp.swapaxes(af_b, 1, 2) + fb[1][:, None])
    hf, Cf, nf, mf = _mlstm_chunkwise(q, k, v, ig_f, lf_f, mC0[:, 0], mn0[:, 0], mm0[:, 0])
    hb, Cb, nb_, mb = _mlstm_chunkwise(_flip_t(q), _flip_t(k), _flip_t(v), _flip_t(ig_b), _flip_t(lf_b),
                                       mC0[:, 1], mn0[:, 1], mm0[:, 1])
    a_out = _merge_heads(jax.nn.sigmoid(_split_heads(ao, MA_HEADS)) * (hf + _flip_t(hb)))
    nq = _split_heads(bq, NA_HEADS) * (NA_DH ** -0.5)
    nk = _split_heads(bk, NA_HEADS)
    nv = _split_heads(bv, NA_HEADS)
    if latent:
        b_heads = _latent_na(nq, nk, nv, ctx_k, ctx_v, rpb_l)
    else:
        b_heads = _context_attention(nq, nk, nv)
        ctx_k, ctx_v = nk, nv
    b_out = _merge_heads(b_heads)
    f_f = lb_l[0] + (1.0 - lb_l[0]) * jax.nn.sigmoid(cf_f)
    f_b = lb_l[1] + (1.0 - lb_l[1]) * jax.nn.sigmoid(cf_b)
    f_f, f_b = _split_heads(f_f, HG_HEADS), _split_heads(f_b, HG_HEADS)
    hq = _split_heads(jax.nn.silu(cq), HG_HEADS)
    hi = _split_heads(ci, HG_HEADS)
    of, Sf = _hgrn2_chunkwise(hq, 1.0 - f_f, hi, jnp.log(f_f), hS0[:, 0])
    ob, Sb = _hgrn2_chunkwise(_flip_t(hq), _flip_t(1.0 - f_b), _flip_t(hi), _flip_t(jnp.log(f_b)), hS0[:, 1])
    o = of + _flip_t(ob)
    o = o * lax.rsqrt(jnp.mean(o * o, -1, keepdims=True) + RMS_EPS)
    c_out = _merge_heads(o) * jax.nn.silu(cg)
    states = (ctx_k, ctx_v, jnp.stack([Cf, Cb], 1), jnp.stack([nf, nb_], 1), jnp.stack([mf, mb], 1), jnp.stack([Sf, Sb], 1))
    return a_out.astype(BF16), b_out.astype(BF16), c_out.astype(BF16), states


def kernel(x_prompt, x_sample, c, cache_na_k, cache_na_v, state_mlstm_C, state_mlstm_n, state_mlstm_m, state_hgrn,
           c_ctx, w_mod, b_mod, w_in, b_in, mlstm_fbias, hgrn_lb_logits, na_rpb, w_branch, w_out, ln_g, ln_b,
           w_rg, w_re, w_e1, w_e3, w_e2):
    lb_cum = jnp.cumsum(jax.nn.softmax(hgrn_lb_logits.astype(F32), axis=1), axis=1)
    lb_all = lb_cum - lb_cum[:, :1]

    cs = jnp.zeros((N_MODROWS, D_MODEL), F32).at[0].set(c_ctx).at[1:1 + DEC_BATCH].set(c)
    mod = _modulation(cs, w_mod, b_mod).reshape(DEPTH, N_MODROWS, 6, 1, D_MODEL)

    x = jnp.concatenate([x_prompt.reshape(N_CTX, D_MODEL), x_sample.reshape(N_LAT, D_MODEL)], 0)
    h = _prep(x, mod[0])

    sts = []
    for l in range(DEPTH):
        w_main = jnp.concatenate([w_in[l][:, :GATE_COL0], w_in[l][:, GATE_COL0 + N_GATES:]], 1).astype(BF16)
        b_main = jnp.concatenate([b_in[l][:GATE_COL0], b_in[l][GATE_COL0 + N_GATES:]])[None, :]
        proj = _inproj(h, w_main, b_main)
        gcol, gt = _gates(h, w_in[l][:, GATE_COL0:GATE_COL0 + N_GATES], b_in[l][GATE_COL0:GATE_COL0 + N_GATES])
        gt3 = gt.reshape(N_GATES, N_TOK // MA_CHUNK, MA_CHUNK).transpose(1, 0, 2)

        a_c, Cc, nc, mc = _mlstm(proj, gcol, gt3, mlstm_fbias[l], False)
        a_l = _mlstm(proj, gcol, gt3, mlstm_fbias[l], True, state_mlstm_C[:, l], state_mlstm_n[:, l], state_mlstm_m[:, l])
        b_c, kc, vc = _ctx_attention(proj)
        b_l = _lat_attention(proj, cache_na_k[:, l], cache_na_v[:, l], _na_bias_table(na_rpb[l]))
        c_c, Sc = _hgrn(proj, lb_all[:, l], False)
        c_l = _hgrn(proj, lb_all[:, l], True, state_hgrn[:, l])
        sts.append((kc, vc, Cc, nc.reshape(BATCH, 2, MA_HEADS, MA_DK), mc[:, :, 0].reshape(BATCH, 2, MA_HEADS), Sc))
        a = jnp.concatenate([a_c, a_l], 0)
        b = jnp.concatenate([b_c, b_l], 0)
        cc = jnp.concatenate([c_c, c_l], 0)

        wr = jnp.zeros((D_MODEL, 128), F32).at[:, :N_GROUPS].set(w_rg[l]).at[:, N_GROUPS:N_GROUPS + N_EXPERTS].set(w_re[l])
        x1, h2, gate, gsel = _merge(a, b, cc, proj, x, mod[l], w_branch[l].astype(BF16), w_out[l].astype(BF16),
                                    ln_g[l, 0][None], ln_b[l, 0][None], wr)
        y2 = _moe(h2, gate, gsel, w_e1[l], w_e3[l], w_e2[l])
        x, h = _final(x1, y2, mod[l], ln_g[l, 1][None], ln_b[l, 1][None], mod[min(l + 1, DEPTH - 1)])

    dt = x_prompt.dtype
    new_na_k = jnp.stack([s[0] for s in sts], 1)
    new_na_v = jnp.stack([s[1] for s in sts], 1)
    new_C = jnp.stack([s[2] for s in sts], 1).astype(dt)
    new_n = jnp.stack([s[3] for s in sts], 1).astype(dt)
    new_m = jnp.stack([s[4] for s in sts], 1).astype(dt)
    new_S = jnp.stack([s[5] for s in sts], 1).astype(dt)
    y_prompt = x[:N_CTX].reshape(BATCH, SEQ, D_MODEL)
    y_sample = x[N_CTX:].reshape(DEC_BATCH, DEC_SEQ, D_MODEL)
    return (y_prompt, y_sample, new_na_k, new_na_v, new_C, new_n, new_m, new_S)
```

```python
import functools

import numpy as np
import jax
import jax.numpy as jnp
from jax import lax
from jax.experimental import pallas as pl
from jax.experimental.pallas import tpu as pltpu

F32 = jnp.float32
BF16 = jnp.bfloat16

D_MODEL = 1024
BATCH = 16
SEQ = 256
DEPTH = 2
DEC_BATCH = 4
DEC_SEQ = 1024
PAST_LEN = 256
GRID_W = 64
MA_HEADS = 4
MA_DK = 128
MA_DV = 128
MA_CHUNK = 64
NA_HEADS = 8
NA_DH = 64
NA_KR_MAX = 8
NA_KC = 16
NA_QB = 16
NA_KB = NA_QB + NA_KC
CTX_QBLOCK = 128
HG_HEADS = 4
HG_DK = 128
HG_DV = 128
HG_CHUNK = 32
BRANCH_W = 512
N_GROUPS = 4
EXP_PER_GROUP = 4
N_EXPERTS = N_GROUPS * EXP_PER_GROUP
D_EXPERT = 512
ROPE_BASE = 10000.0
LN_EPS = 1e-5
RMS_EPS = 1e-6
DEEPNORM_ALPHA = (2 * DEPTH) ** 0.25

N_CTX = BATCH * SEQ
N_LAT = DEC_BATCH * DEC_SEQ
N_TOK = N_CTX + N_LAT
N_MODROWS = 8
GATE_COL0 = 4 * BRANCH_W
N_GATES = 4 * MA_HEADS
P_COLS = 9216
MOE_TM = 256
MOE_NT = N_TOK // MOE_TM + N_GROUPS
VMEM_LIMIT = 48 * 1024 * 1024


def _cparams(sem):
    return pltpu.CompilerParams(dimension_semantics=sem, vmem_limit_bytes=VMEM_LIMIT)


def _mod_row(tile, tm):
    return jnp.maximum((tile * tm) // DEC_SEQ - (N_CTX // DEC_SEQ - 1), 0)


def _mod_spec(part, tm):
    return pl.BlockSpec((None, None, 1, D_MODEL), lambda i: (_mod_row(i, tm), part, 0, 0))


def _silu(x):
    return x * jax.nn.sigmoid(x)


def _mod_kernel(c_ref, w_ref, b_ref, o_ref):
    s = _silu(c_ref[...])
    o_ref[...] = jnp.dot(s.astype(BF16), w_ref[...].astype(BF16), preferred_element_type=F32) + b_ref[...]


def _modulation(cs, w_mod, b_mod):
    tn = 1024
    return pl.pallas_call(
        _mod_kernel,
        out_shape=jax.ShapeDtypeStruct((DEPTH, N_MODROWS, 6 * D_MODEL), F32),
        grid=(DEPTH, 6 * D_MODEL // tn),
        in_specs=[pl.BlockSpec((N_MODROWS, D_MODEL), lambda l, j: (0, 0)),
                  pl.BlockSpec((None, D_MODEL, tn), lambda l, j: (l, 0, j)),
                  pl.BlockSpec((None, 1, tn), lambda l, j: (l, 0, j))],
        out_specs=pl.BlockSpec((None, N_MODROWS, tn), lambda l, j: (l, 0, j)),
        compiler_params=_cparams(("arbitrary", "arbitrary")),
        name="modulation",
    )(cs, w_mod, b_mod.reshape(DEPTH, 1, 6 * D_MODEL))


def _prep_kernel(x_ref, sh_ref, sc_ref, h_ref):
    h_ref[...] = (x_ref[...] * (1.0 + sc_ref[...]) + sh_ref[...]).astype(BF16)


def _prep(x, modl):
    tm = 1024
    return pl.pallas_call(
        _prep_kernel,
        out_shape=jax.ShapeDtypeStruct((N_TOK, D_MODEL), BF16),
        grid=(N_TOK // tm,),
        in_specs=[pl.BlockSpec((tm, D_MODEL), lambda i: (i, 0)), _mod_spec(0, tm), _mod_spec(1, tm)],
        out_specs=pl.BlockSpec((tm, D_MODEL), lambda i: (i, 0)),
        compiler_params=_cparams(("arbitrary",)),
        name="prep",
    )(x, modl, modl)


def _inproj_kernel(h_ref, w_ref, b_ref, o_ref):
    o_ref[...] = jnp.dot(h_ref[...], w_ref[...], preferred_element_type=F32) + b_ref[...]


def _inproj(h, w, b):
    tm, tn = 2048, 512
    return pl.pallas_call(
        _inproj_kernel,
        out_shape=jax.ShapeDtypeStruct((N_TOK, P_COLS), F32),
        grid=(N_TOK // tm, P_COLS // tn),
        in_specs=[pl.BlockSpec((tm, D_MODEL), lambda i, j: (i, 0)),
                  pl.BlockSpec((D_MODEL, tn), lambda i, j: (0, j)),
                  pl.BlockSpec((1, tn), lambda i, j: (0, j))],
        out_specs=pl.BlockSpec((tm, tn), lambda i, j: (i, j)),
        compiler_params=_cparams(("arbitrary", "arbitrary")),
        name="inproj",
    )(h, w, b)


def _gates_kernel(h_ref, w_ref, wt_ref, b_ref, bt_ref, gc_ref, gt_ref):
    h = h_ref[...]
    gc_ref[...] = jnp.dot(h, w_ref[...], preferred_element_type=F32) + b_ref[...]
    gt_ref[...] = lax.dot_general(wt_ref[...], h, (((1,), (1,)), ((), ())), preferred_element_type=F32) + bt_ref[...]


def _gates(h, wg, bg):
    tm = 1024
    w = jnp.zeros((D_MODEL, 128), BF16).at[:, :N_GATES].set(wg.astype(BF16))
    b = jnp.zeros((1, 128), F32).at[0, :N_GATES].set(bg)
    return pl.pallas_call(
        _gates_kernel,
        out_shape=(jax.ShapeDtypeStruct((N_TOK, 128), F32), jax.ShapeDtypeStruct((N_GATES, N_TOK), F32)),
        grid=(N_TOK // tm,),
        in_specs=[pl.BlockSpec((tm, D_MODEL), lambda i: (i, 0)),
                  pl.BlockSpec((D_MODEL, 128), lambda i: (0, 0)),
                  pl.BlockSpec((N_GATES, D_MODEL), lambda i: (0, 0)),
                  pl.BlockSpec((1, 128), lambda i: (0, 0)),
                  pl.BlockSpec((N_GATES, 1), lambda i: (0, 0))],
        out_specs=(pl.BlockSpec((tm, 128), lambda i: (i, 0)), pl.BlockSpec((N_GATES, tm), lambda i: (0, i))),
        compiler_params=_cparams(("arbitrary",)),
        name="gates",
    )(h, w, wg.T.astype(BF16), b, bg.reshape(N_GATES, 1))


def _layer_norm(x, g, b):
    mu = jnp.mean(x, -1, keepdims=True)
    xc = x - mu
    var = jnp.mean(xc * xc, -1, keepdims=True)
    return xc * lax.rsqrt(var + LN_EPS) * g + b


def _merge_kernel(a_ref, b_ref, c_ref, ga_ref, gb_ref, gc_ref, x_ref, g1_ref, sh2_ref, sc2_ref,
                  wb_ref, wo_ref, lng_ref, lnb_ref, wr_ref, x1_ref, h2_ref, gate_ref, gid_ref):
    def br(v_ref, g_ref, k):
        return jax.nn.sigmoid(g_ref[...]) * jnp.dot(v_ref[...], wb_ref[k], preferred_element_type=F32)

    mix = br(a_ref, ga_ref, 0) + br(b_ref, gb_ref, 1) + br(c_ref, gc_ref, 2)
    y = jnp.dot(mix.astype(BF16), wo_ref[...], preferred_element_type=F32)
    x1 = _layer_norm(DEEPNORM_ALPHA * x_ref[...] + g1_ref[...] * y, lng_ref[...], lnb_ref[...])
    x1_ref[...] = x1
    h2 = x1 * (1.0 + sc2_ref[...]) + sh2_ref[...]
    h2_ref[...] = h2.astype(BF16)
    logits = jnp.dot(h2, wr_ref[...], preferred_element_type=F32, precision=lax.Precision.HIGHEST)
    lane = lax.broadcasted_iota(jnp.int32, logits.shape, 1)
    neg = -jnp.inf
    lg = jnp.where(lane < N_GROUPS, logits, neg)
    mg = jnp.max(lg, -1, keepdims=True)
    g_sel = jnp.min(jnp.where(lg == mg, lane, 128), -1, keepdims=True)
    p_sel = 1.0 / jnp.sum(jnp.where(lane < N_GROUPS, jnp.exp(lg - mg), 0.0), -1, keepdims=True)
    lo = N_GROUPS + EXP_PER_GROUP * g_sel
    le = jnp.where((lane >= lo) & (lane < lo + EXP_PER_GROUP), logits, neg)
    v1 = jnp.max(le, -1, keepdims=True)
    i1 = jnp.min(jnp.where(le == v1, lane, 128), -1, keepdims=True)
    le2 = jnp.where(lane == i1, neg, le)
    v2 = jnp.max(le2, -1, keepdims=True)
    i2 = jnp.min(jnp.where(le2 == v2, lane, 128), -1, keepdims=True)
    e2 = jnp.exp(v2 - v1)
    w1 = p_sel / (1.0 + e2)
    w2 = p_sel * e2 / (1.0 + e2)
    gate_ref[...] = jnp.where(lane == i1 - lo, w1, jnp.where(lane == i2 - lo, w2, 0.0))
    gid_ref[...] = jnp.broadcast_to(g_sel, logits.shape)


def _merge(a, b, c, proj, x, modl, wb, wo, lng, lnb, wr):
    tm = 512
    tok = lambda i: (i, 0)
    full2 = lambda i: (0, 0)
    return pl.pallas_call(
        _merge_kernel,
        out_shape=(jax.ShapeDtypeStruct((N_TOK, D_MODEL), F32), jax.ShapeDtypeStruct((N_TOK, D_MODEL), BF16),
                   jax.ShapeDtypeStruct((N_TOK, 128), F32), jax.ShapeDtypeStruct((N_TOK, 128), jnp.int32)),
        grid=(N_TOK // tm,),
        in_specs=[pl.BlockSpec((tm, BRANCH_W), tok), pl.BlockSpec((tm, BRANCH_W), tok), pl.BlockSpec((tm, BRANCH_W), tok),
                  pl.BlockSpec((tm, D_MODEL), lambda i: (i, 6)), pl.BlockSpec((tm, D_MODEL), lambda i: (i, 7)),
                  pl.BlockSpec((tm, D_MODEL), lambda i: (i, 8)),
                  pl.BlockSpec((tm, D_MODEL), tok), _mod_spec(2, tm), _mod_spec(3, tm), _mod_spec(4, tm),
                  pl.BlockSpec((3, BRANCH_W, D_MODEL), lambda i: (0, 0, 0)), pl.BlockSpec((D_MODEL, D_MODEL), full2),
                  pl.BlockSpec((1, D_MODEL), full2), pl.BlockSpec((1, D_MODEL), full2),
                  pl.BlockSpec((D_MODEL, 128), full2)],
        out_specs=(pl.BlockSpec((tm, D_MODEL), tok), pl.BlockSpec((tm, D_MODEL), tok),
                   pl.BlockSpec((tm, 128), tok), pl.BlockSpec((tm, 128), tok)),
        compiler_params=_cparams(("arbitrary",)),
        name="merge",
    )(a, b, c, proj, proj, proj, x, modl, modl, modl, wb, wo, lng, lnb, wr)


def _moe_up_kernel(gid_ref, x_ref, gate_ref, w1_ref, w3_ref, hid_ref, w1b, w3b):
    f = pl.program_id(0)
    t = pl.program_id(1)
    first = jnp.logical_or(t == 0, gid_ref[t] != gid_ref[jnp.maximum(t - 1, 0)])

    @pl.when(first)
    def _():
        w1b[...] = w1_ref[...].astype(BF16)
        w3b[...] = w3_ref[...].astype(BF16)

    x = x_ref[...]
    a = jnp.dot(x, w1b[...], preferred_element_type=F32)
    b = jnp.dot(x, w3b[...], preferred_element_type=F32)
    gate = gate_ref[...]
    lane = lax.broadcasted_iota(jnp.int32, gate.shape, 1)
    gcol = jnp.sum(jnp.where(lane == f, gate, 0.0), -1, keepdims=True)
    hid_ref[...] = (_silu(a) * b * gcol).astype(BF16)


def _moe_up(gid, xs, gates, w1, w3):
    tm = MOE_TM
    npad = MOE_NT * tm
    grid_spec = pltpu.PrefetchScalarGridSpec(
        num_scalar_prefetch=1,
        grid=(EXP_PER_GROUP, MOE_NT),
        in_specs=[pl.BlockSpec((tm, D_MODEL), lambda f, t, g: (t, 0)),
                  pl.BlockSpec((tm, 128), lambda f, t, g: (t, 0)),
                  pl.BlockSpec((None, D_MODEL, D_EXPERT), lambda f, t, g: (EXP_PER_GROUP * g[t] + f, 0, 0)),
                  pl.BlockSpec((None, D_MODEL, D_EXPERT), lambda f, t, g: (EXP_PER_GROUP * g[t] + f, 0, 0))],
        out_specs=pl.BlockSpec((tm, D_EXPERT), lambda f, t, g: (t, f)),
        scratch_shapes=[pltpu.VMEM((D_MODEL, D_EXPERT), BF16), pltpu.VMEM((D_MODEL, D_EXPERT), BF16)],
    )
    return pl.pallas_call(
        _moe_up_kernel,
        out_shape=jax.ShapeDtypeStruct((npad, EXP_PER_GROUP * D_EXPERT), BF16),
        grid_spec=grid_spec,
        compiler_params=_cparams(("arbitrary", "arbitrary")),
        name="moe_up",
    )(gid, xs, gates, w1, w3)


def _moe_down_kernel(gid_ref, hid_ref, w2_ref, y_ref, w2b):
    t = pl.program_id(0)
    first = jnp.logical_or(t == 0, gid_ref[t] != gid_ref[jnp.maximum(t - 1, 0)])

    @pl.when(first)
    def _():
        w2b[...] = w2_ref[...].astype(BF16)

    y_ref[...] = jnp.dot(hid_ref[...], w2b[...], preferred_element_type=F32)


def _moe_down(gid, hid, w2g):
    tm = MOE_TM
    npad = MOE_NT * tm
    hw = EXP_PER_GROUP * D_EXPERT
    grid_spec = pltpu.PrefetchScalarGridSpec(
        num_scalar_prefetch=1,
        grid=(MOE_NT,),
        in_specs=[pl.BlockSpec((tm, hw), lambda t, g: (t, 0)),
                  pl.BlockSpec((None, hw, D_MODEL), lambda t, g: (g[t], 0, 0))],
        out_specs=pl.BlockSpec((tm, D_MODEL), lambda t, g: (t, 0)),
        scratch_shapes=[pltpu.VMEM((hw, D_MODEL), BF16)],
    )
    return pl.pallas_call(
        _moe_down_kernel,
        out_shape=jax.ShapeDtypeStruct((npad, D_MODEL), F32),
        grid_spec=grid_spec,
        compiler_params=_cparams(("arbitrary",)),
        name="moe_down",
    )(gid, hid, w2g)


def _moe(h2, gate, gsel, w1, w3, w2):
    tm = MOE_TM
    npad = MOE_NT * tm
    g = gsel[:, 0]
    onehot = (g[:, None] == jnp.arange(N_GROUPS)[None, :]).astype(jnp.int32)
    counts = jnp.sum(onehot, 0)
    rank = jnp.sum((jnp.cumsum(onehot, 0) - onehot) * onehot, 1)
    padded = (counts + tm - 1) // tm * tm
    ends = jnp.cumsum(padded)
    offs = ends - padded
    dest = offs[g] + rank
    src = jnp.zeros((npad,), jnp.int32).at[dest].set(jnp.arange(N_TOK, dtype=jnp.int32))
    valid = jnp.zeros((npad,), F32).at[dest].set(1.0)
    starts = jnp.arange(MOE_NT, dtype=jnp.int32) * tm
    tile_gid = jnp.minimum(jnp.sum((ends[None, :] <= starts[:, None]).astype(jnp.int32), 1), N_GROUPS - 1)
    xs = jnp.take(h2, src, axis=0)
    gs = jnp.take(gate, src, axis=0) * valid[:, None]
    hid = _moe_up(tile_gid, xs, gs, w1, w3)
    ys = _moe_down(tile_gid, hid, w2.reshape(N_GROUPS, EXP_PER_GROUP * D_EXPERT, D_MODEL))
    return jnp.take(ys, dest, axis=0)


def _final_kernel(x1_ref, y_ref, g2_ref, lng_ref, lnb_ref, sh_ref, sc_ref, x2_ref, h_ref):
    x2 = _layer_norm(DEEPNORM_ALPHA * x1_ref[...] + g2_ref[...] * y_ref[...], lng_ref[...], lnb_ref[...])
    x2_ref[...] = x2
    h_ref[...] = (x2 * (1.0 + sc_ref[...]) + sh_ref[...]).astype(BF16)


def _final(x1, y, modl, lng, lnb, mod_next):
    tm = 1024
    tok = lambda i: (i, 0)
    full2 = lambda i: (0, 0)
    return pl.pallas_call(
        _final_kernel,
        out_shape=(jax.ShapeDtypeStruct((N_TOK, D_MODEL), F32), jax.ShapeDtypeStruct((N_TOK, D_MODEL), BF16)),
        grid=(N_TOK // tm,),
        in_specs=[pl.BlockSpec((tm, D_MODEL), tok), pl.BlockSpec((tm, D_MODEL), tok), _mod_spec(5, tm),
                  pl.BlockSpec((1, D_MODEL), full2), pl.BlockSpec((1, D_MODEL), full2),
                  _mod_spec(0, tm), _mod_spec(1, tm)],
        out_specs=(pl.BlockSpec((tm, D_MODEL), tok), pl.BlockSpec((tm, D_MODEL), tok)),
        compiler_params=_cparams(("arbitrary",)),
        name="final",
    )(x1, y, modl, lng, lnb, mod_next, mod_next)


_NT = (((1,), (1,)), ((), ()))
_TN = (((0,), (0,)), ((), ()))
HEADS_PER_BLK = 128 // NA_DH
NA_QSCALE = NA_DH ** -0.5


def _softmax_rows(parts):
    m = functools.reduce(jnp.maximum, [jnp.max(s, -1, keepdims=True) for s in parts])
    es = [jnp.exp(s - m) for s in parts]
    inv = 1.0 / functools.reduce(jnp.add, [jnp.sum(e, -1, keepdims=True) for e in es])
    return [e * inv for e in es]


def _ctx_attn_kernel(q_ref, k_ref, v_ref, o_ref, ko_ref, vo_ref):
    q = q_ref[...] * NA_QSCALE
    k = k_ref[...]
    v = v_ref[...]
    outs = []
    for hh in range(HEADS_PER_BLK):
        sl = slice(hh * NA_DH, (hh + 1) * NA_DH)
        kh, vh = k[:, sl], v[:, sl]
        ko_ref[hh] = kh
        vo_ref[hh] = vh
        s = lax.dot_general(q[:, sl].astype(BF16), kh.astype(BF16), _NT, preferred_element_type=F32)
        (p,) = _softmax_rows([s])
        outs.append(jnp.dot(p.astype(BF16), vh.astype(BF16), preferred_element_type=F32))
    o_ref[...] = jnp.concatenate(outs, -1).astype(BF16)


def _ctx_attention(proj):
    nblk = NA_HEADS // HEADS_PER_BLK
    cb = lambda base: (lambda b, j: (b, base + j))
    kv_shape = jax.ShapeDtypeStruct((BATCH, NA_HEADS, SEQ, NA_DH), F32)
    kv_spec = pl.BlockSpec((None, HEADS_PER_BLK, SEQ, NA_DH), lambda b, j: (b, j, 0, 0))
    return pl.pallas_call(
        _ctx_attn_kernel,
        out_shape=(jax.ShapeDtypeStruct((N_CTX, BRANCH_W), BF16), kv_shape, kv_shape),
        grid=(BATCH, nblk),
        in_specs=[pl.BlockSpec((SEQ, 128), cb(16)), pl.BlockSpec((SEQ, 128), cb(20)), pl.BlockSpec((SEQ, 128), cb(24))],
        out_specs=(pl.BlockSpec((SEQ, 128), lambda b, j: (b, j)), kv_spec, kv_spec),
        compiler_params=_cparams(("arbitrary", "arbitrary")),
        name="ctx_attention",
    )(proj, proj, proj)


NA_ROWS = DEC_SEQ // GRID_W
NA_KR = min(NA_KR_MAX, NA_ROWS)


def _na_bias_table(rpb):
    c = np.arange(GRID_W)
    c0 = np.clip(c - NA_KC // 2, 0, GRID_W - NA_KC)
    kc = np.arange(GRID_W)
    valid = (kc[None, :] >= c0[:, None]) & (kc[None, :] < c0[:, None] + NA_KC)
    dc = kc[None, :] - c[:, None] + NA_KC - 1
    onehot = (dc[None] == np.arange(2 * NA_KC - 1)[:, None, None]) & valid[None]
    toep = jnp.einsum('hrd,dcx->hrcx', rpb.astype(F32), jnp.asarray(onehot, F32), precision=HI)
    toep = jnp.where(valid[None, None], toep, -jnp.inf)
    rows = []
    for r in range(NA_ROWS):
        r0 = min(max(r - NA_KR // 2, 0), NA_ROWS - NA_KR)
        d0 = r0 - r + NA_KR_MAX - 1
        rows.append(toep[:, d0:d0 + NA_KR].transpose(0, 2, 1, 3).reshape(NA_HEADS, GRID_W, NA_KR * GRID_W))
    return jnp.stack(rows, 1)


def _lat_attn_kernel(q_ref, k_ref, v_ref, ck_ref, cv_ref, bias_ref, o_ref):
    q = (q_ref[...] * NA_QSCALE).astype(BF16)
    k = k_ref[...].astype(BF16)
    v = v_ref[...].astype(BF16)
    for r in range(NA_ROWS):
        r0 = min(max(r - NA_KR // 2, 0), NA_ROWS - NA_KR)
        qs = slice(r * GRID_W, (r + 1) * GRID_W)
        ws = slice(r0 * GRID_W, (r0 + NA_KR) * GRID_W)
        outs = []
        for hh in range(HEADS_PER_BLK):
            sl = slice(hh * NA_DH, (hh + 1) * NA_DH)
            qh = q[qs, sl]
            s_loc = lax.dot_general(qh, k[ws, sl], _NT, preferred_element_type=F32) + bias_ref[hh, r]
            s_ctx = lax.dot_general(qh, ck_ref[hh].astype(BF16), _NT, preferred_element_type=F32)
            p_loc, p_ctx = _softmax_rows([s_loc, s_ctx])
            outs.append(jnp.dot(p_loc.astype(BF16), v[ws, sl], preferred_element_type=F32)
                        + jnp.dot(p_ctx.astype(BF16), cv_ref[hh].astype(BF16), preferred_element_type=F32))
        o_ref[qs, :] = jnp.concatenate(outs, -1).astype(BF16)


def _lat_attention(proj, ck, cv, bias):
    nblk = NA_HEADS // HEADS_PER_BLK
    rb0 = N_CTX // DEC_SEQ
    cb = lambda base: (lambda j, b: (rb0 + b, base + j))
    c_spec = pl.BlockSpec((None, HEADS_PER_BLK, PAST_LEN, NA_DH), lambda j, b: (b, j, 0, 0))
    return pl.pallas_call(
        _lat_attn_kernel,
        out_shape=jax.ShapeDtypeStruct((N_LAT, BRANCH_W), BF16),
        grid=(nblk, DEC_BATCH),
        in_specs=[pl.BlockSpec((DEC_SEQ, 128), cb(16)), pl.BlockSpec((DEC_SEQ, 128), cb(20)),
                  pl.BlockSpec((DEC_SEQ, 128), cb(24)), c_spec, c_spec,
                  pl.BlockSpec((HEADS_PER_BLK, NA_ROWS, GRID_W, NA_KR * GRID_W), lambda j, b: (j, 0, 0, 0))],
        out_specs=pl.BlockSpec((DEC_SEQ, 128), lambda j, b: (b, j)),
        compiler_params=_cparams(("arbitrary", "arbitrary")),
        name="lat_attention",
    )(proj, proj, proj, ck, cv, bias)


HI = lax.Precision.HIGHEST
MA_KSCALE = MA_DK ** -0.5


def _log_sigmoid(x):
    return jnp.minimum(x, 0.0) - jnp.log(1.0 + jnp.exp(-jnp.abs(x)))


def _tri(n, upper):
    r = lax.broadcasted_iota(jnp.int32, (n, n), 0)
    c = lax.broadcasted_iota(jnp.int32, (n, n), 1)
    return jnp.where((r <= c) if upper else (r >= c), 1.0, 0.0).astype(F32)


def _rope_tables(T):
    t = np.arange(T)
    half = MA_DK // 2
    inv = ROPE_BASE ** (-jnp.arange(0, half, 2, dtype=F32) / half)
    ang_r = jnp.asarray((t // GRID_W).astype(np.float32))[:, None] * inv[None, :]
    ang_c = jnp.asarray((t % GRID_W).astype(np.float32))[:, None] * inv[None, :]
    cos = jnp.concatenate([jnp.cos(ang_r)] * 2 + [jnp.cos(ang_c)] * 2, -1)
    sin = jnp.concatenate([-jnp.sin(ang_r), jnp.sin(ang_r), -jnp.sin(ang_c), jnp.sin(ang_c)], -1)
    return cos, sin


def _mlstm_kernel(*refs, T, latent):
    if latent:
        (p_ref, gc_ref, gt_ref, fbc_ref, fbr_ref, cos_ref, sin_ref, c0_ref, n0_ref, m0_ref,
         a_ref, qs, ks, vs, hf, hb, Cs, ns, ms) = refs
    else:
        (p_ref, gc_ref, gt_ref, fbc_ref, fbr_ref, a_ref, co_ref, no_ref, mo_ref,
         qs, ks, vs, hf, hb, Cs, ns, ms) = refs
    L = MA_CHUNK
    NC = T // L
    W = BRANCH_W

    lane = lax.broadcasted_iota(jnp.int32, (T, MA_DK), 1)
    lo_half = (lane % (MA_DK // 2)) < (MA_DK // 4)

    def rope(x):
        if not latent:
            return x
        swapped = jnp.where(lo_half, pltpu.roll(x, MA_DK - MA_DK // 4, 1), pltpu.roll(x, MA_DK // 4, 1))
        return x * cos_ref[...] + swapped * sin_ref[...]

    for h in range(MA_HEADS):
        hs = slice(h * MA_DK, (h + 1) * MA_DK)
        qs[:, hs] = rope(p_ref[:, hs]).astype(BF16)
        ks[:, hs] = rope(p_ref[:, W + h * MA_DK:W + (h + 1) * MA_DK] * MA_KSCALE).astype(BF16)
    vs[...] = p_ref[:, 2 * W:3 * W].astype(BF16)

    for d in range(2):
        for h in range(MA_HEADS):
            Cs[d, h] = c0_ref[d, h] if latent else jnp.zeros((MA_DK, MA_DV), F32)
    ns[...] = n0_ref[...] if latent else jnp.zeros_like(ns)
    ms[...] = m0_ref[...] if latent else jnp.zeros_like(ms)

    low = _tri(L, False)
    upp = _tri(L, True)
    rr = lax.broadcasted_iota(jnp.int32, (L, L), 0)
    cc = lax.broadcasted_iota(jnp.int32, (L, L), 1)
    fbc = fbc_ref[...]
    fbr = fbr_ref[...]

    def chunk(c, d):
        t0 = pl.multiple_of(c * L, L)
        gc = gc_ref[pl.ds(t0, L), :]
        gt = gt_ref[c]
        lfc = _log_sigmoid(gc + fbc)
        lfr = _log_sigmoid(gt + fbr)
        if d == 0:
            bcol = jnp.dot(low, lfc, precision=HI, preferred_element_type=F32)
            brow = jnp.dot(lfr, upp, precision=HI, preferred_element_type=F32)
        else:
            bcol = jnp.dot(upp, lfc, precision=HI, preferred_element_type=F32)
            brow = jnp.dot(lfr, low, precision=HI, preferred_element_type=F32)
        mask = (cc <= rr) if d == 0 else (cc >= rr)
        last = L - 1 if d == 0 else 0
        for h in range(MA_HEADS):
            gi = 2 * d * MA_HEADS + h
            gf = gi + MA_HEADS
            sidx = d * MA_HEADS + h
            hs = slice(h * MA_DK, (h + 1) * MA_DK)
            bc = bcol[:, gf:gf + 1]
            br = brow[gf:gf + 1, :]
            ir = gt[gi:gi + 1, :]
            ic = gc[:, gi:gi + 1]
            m = ms[pl.ds(sidx, 1), :][:, 0:1]
            n = ns[pl.ds(sidx, 1), :]
            C = Cs[d, h]
            q = qs[pl.ds(t0, L), hs]
            k = ks[pl.ds(t0, L), hs]
            v = vs[pl.ds(t0, L), hs]
            dmat = jnp.where(mask, bc - br + ir, -jnp.inf)
            g = bc + m
            m_t = jnp.maximum(g, jnp.max(dmat, -1, keepdims=True))
            w_inter = jnp.exp(g - m_t)
            s = lax.dot_general(q, k, _NT, preferred_element_type=F32) * jnp.exp(dmat - m_t)
            num = (w_inter * jnp.dot(q, C.astype(BF16), preferred_element_type=F32)
                   + jnp.dot(s.astype(BF16), v, preferred_element_type=F32))
            den = w_inter * jnp.sum(q.astype(F32) * n, -1, keepdims=True) + jnp.sum(s, -1, keepdims=True)
            hout = num / jnp.maximum(jnp.abs(den), jnp.exp(-m_t))
            (hf if d == 0 else hb)[pl.ds(t0, L), hs] = hout
            m_new = m_t[last:last + 1, :]
            b_last = bc[last:last + 1, :]
            decay = jnp.exp(b_last + m - m_new)
            kw = k.astype(F32) * jnp.exp(b_last - bc + ic - m_new)
            Cs[d, h] = decay * C + lax.dot_general(kw.astype(BF16), v, _TN, preferred_element_type=F32)
            ns[pl.ds(sidx, 1), :] = decay * n + jnp.sum(kw, 0, keepdims=True)
            ms[pl.ds(sidx, 1), :] = jnp.broadcast_to(m_new, (1, 128))

    def body(i, carry):
        chunk(i, 0)
        chunk(NC - 1 - i, 1)
        return carry

    lax.fori_loop(0, NC, body, 0)

    a_ref[...] = (jax.nn.sigmoid(p_ref[:, 3 * W:4 * W]) * (hf[...] + hb[...])).astype(BF16)
    if not latent:
        for d in range(2):
            for h in range(MA_HEADS):
                co_ref[d, h] = Cs[d, h]
        no_ref[...] = ns[...]
        mo_ref[...] = ms[...]


def _mlstm(proj, gcol, gt3, fbias_l, latent, C0=None, n0=None, m0=None):
    T = DEC_SEQ if latent else SEQ
    B = DEC_BATCH if latent else BATCH
    rb0 = N_CTX // DEC_SEQ if latent else 0
    fb = fbias_l.astype(F32)
    fbc = jnp.zeros((1, 128), F32).at[0, MA_HEADS:2 * MA_HEADS].set(fb[0]).at[0, 3 * MA_HEADS:4 * MA_HEADS].set(fb[1])
    fbr = fbc[0, :N_GATES].reshape(N_GATES, 1)
    full2 = lambda b: (0, 0)
    in_specs = [pl.BlockSpec((T, 4 * BRANCH_W), lambda b: (rb0 + b, 0)),
                pl.BlockSpec((T, 128), lambda b: (rb0 + b, 0)),
                pl.BlockSpec((T // MA_CHUNK, N_GATES, MA_CHUNK), lambda b: (rb0 + b, 0, 0)),
                pl.BlockSpec((1, 128), full2), pl.BlockSpec((N_GATES, 1), full2)]
    args = [proj, gcol, gt3, fbc, fbr]
    a_shape = jax.ShapeDtypeStruct((B * T, BRANCH_W), BF16)
    a_spec = pl.BlockSpec((T, BRANCH_W), lambda b: (b, 0))
    c_spec = pl.BlockSpec((None, 2, MA_HEADS, MA_DK, MA_DV), lambda b: (b, 0, 0, 0, 0))
    nm_spec = pl.BlockSpec((None, 2 * MA_HEADS, 128), lambda b: (b, 0, 0))
    if latent:
        cos, sin = _rope_tables(T)
        in_specs += [pl.BlockSpec((T, MA_DK), full2), pl.BlockSpec((T, MA_DK), full2), c_spec, nm_spec, nm_spec]
        args += [cos, sin, C0, n0.reshape(B, 2 * MA_HEADS, MA_DK),
                 jnp.broadcast_to(m0.reshape(B, 2 * MA_HEADS, 1), (B, 2 * MA_HEADS, 128))]
        out_shape, out_specs = a_shape, a_spec
    else:
        nm_shape = jax.ShapeDtypeStruct((B, 2 * MA_HEADS, 128), F32)
        out_shape = (a_shape, jax.ShapeDtypeStruct((B, 2, MA_HEADS, MA_DK, MA_DV), F32), nm_shape, nm_shape)
        out_specs = (a_spec, c_spec, nm_spec, nm_spec)
    scratch = [pltpu.VMEM((T, BRANCH_W), BF16)] * 3 + [pltpu.VMEM((T, BRANCH_W), F32)] * 2 + [
        pltpu.VMEM((2, MA_HEADS, MA_DK, MA_DV), F32), pltpu.VMEM((2 * MA_HEADS, MA_DK), F32),
        pltpu.VMEM((2 * MA_HEADS, 128), F32)]
    return pl.pallas_call(
        functools.partial(_mlstm_kernel, T=T, latent=latent),
        out_shape=out_shape, grid=(B,), in_specs=in_specs, out_specs=out_specs, scratch_shapes=scratch,
        compiler_params=_cparams(("arbitrary",)),
        name="mlstm_lat" if latent else "mlstm_ctx",
    )(*args)


HG_SUB = 8


def _hgrn_kernel(*refs, T, latent):
    if latent:
        ff_ref, fb_ref, q_ref, i_ref, g_ref, lbf_ref, lbb_ref, s0_ref, c_ref, of, ob, ST = refs
    else:
        ff_ref, fb_ref, q_ref, i_ref, g_ref, lbf_ref, lbb_ref, c_ref, so_ref, of, ob, ST = refs
    L = HG_CHUNK
    NC = T // L
    NB = L // HG_SUB
    DK = HG_DK

    for d in range(2):
        for h in range(HG_HEADS):
            ST[d, h] = s0_ref[d, h].T if latent else jnp.zeros((HG_DV, DK), F32)

    low = _tri(L, False)
    upp = _tri(L, True)
    row8 = lax.broadcasted_iota(jnp.int32, (HG_SUB, L), 0)
    lane_s = lax.broadcasted_iota(jnp.int32, (HG_SUB, L), 1)

    def chunk(c, d):
        t0 = pl.multiple_of(c * L, L)
        fpre = (ff_ref if d == 0 else fb_ref)[pl.ds(t0, L), :]
        lb = (lbf_ref if d == 0 else lbb_ref)[...]
        f = lb + (1.0 - lb) * jax.nn.sigmoid(fpre)
        logf = jnp.log(f)
        kk = 1.0 - f
        qq = _silu(q_ref[pl.ds(t0, L), :])
        iv = i_ref[pl.ds(t0, L), :].astype(BF16)
        A_all = jnp.dot(low if d == 0 else upp, logf, precision=HI, preferred_element_type=F32)
        last = L - 1 if d == 0 else 0
        for h in range(HG_HEADS):
            hs = slice(h * DK, (h + 1) * DK)
            A, k, q, ivh = A_all[:, hs], kk[:, hs], qq[:, hs], iv[:, hs]
            st = ST[d, h]
            o = lax.dot_general((q * jnp.exp(A)).astype(BF16), st.astype(BF16), _NT, preferred_element_type=F32)
            a_last = A[last:last + 1, :]
            kd = k * jnp.exp(a_last - A)
            rows = []
            for I in range(NB):
                bs = slice(I * HG_SUB, (I + 1) * HG_SUB)
                A_I, q_I = A[bs], q[bs]
                if d == 0:
                    has_off, ref_row, off_mask = I > 0, I * HG_SUB - 1, lane_s < I * HG_SUB
                else:
                    has_off, ref_row, off_mask = I < NB - 1, (I + 1) * HG_SUB, lane_s >= (I + 1) * HG_SUB
                att = jnp.zeros((HG_SUB, L), F32)
                if has_off:
                    R = A[ref_row:ref_row + 1, :]
                    qsc = (q_I * jnp.exp(A_I - R)).astype(BF16)
                    ksc = (k * jnp.exp(jnp.minimum(R - A, 0.0))).astype(BF16)
                    att = jnp.where(off_mask, lax.dot_general(qsc, ksc, _NT, preferred_element_type=F32), 0.0)
                for j in range(HG_SUB):
                    s = I * HG_SUB + j
                    e = jnp.exp(jnp.minimum(A_I - A[s:s + 1, :], 0.0))
                    col = jnp.sum(q_I * k[s:s + 1, :] * e, -1, keepdims=True)
                    keep = (lane_s == s) & ((row8 >= j) if d == 0 else (row8 <= j))
                    att = jnp.where(keep, col, att)
                rows.append(att)
            att = jnp.concatenate(rows, 0)
            o = o + jnp.dot(att.astype(BF16), ivh, preferred_element_type=F32)
            (of if d == 0 else ob)[pl.ds(t0, L), hs] = o
            ST[d, h] = st * jnp.exp(a_last) + lax.dot_general(ivh, kd.astype(BF16), _TN, preferred_element_type=F32)

    def body(i, carry):
        chunk(i, 0)
        chunk(NC - 1 - i, 1)
        return carry

    lax.fori_loop(0, NC, body, 0)

    def epilogue(r, carry):
        t0 = pl.multiple_of(r * 128, 128)
        o = of[pl.ds(t0, 128), :] + ob[pl.ds(t0, 128), :]
        gsil = _silu(g_ref[pl.ds(t0, 128), :])
        outs = []
        for h in range(HG_HEADS):
            oh = o[:, h * HG_DV:(h + 1) * HG_DV]
            outs.append(oh * lax.rsqrt(jnp.mean(oh * oh, -1, keepdims=True) + RMS_EPS))
        c_ref[pl.ds(t0, 128), :] = (jnp.concatenate(outs, -1) * gsil).astype(BF16)
        return carry

    lax.fori_loop(0, T // 128, epilogue, 0)
    if not latent:
        for d in range(2):
            for h in range(HG_HEADS):
                so_ref[d, h] = ST[d, h].T


def _hgrn(proj, lb_l, latent, S0=None):
    T = DEC_SEQ if latent else SEQ
    B = DEC_BATCH if latent else BATCH
    rb0 = N_CTX // DEC_SEQ if latent else 0
    W = BRANCH_W
    full2 = lambda b: (0, 0)
    col = lambda j: pl.BlockSpec((T, W), lambda b: (rb0 + b, j))
    s_spec = pl.BlockSpec((None, 2, HG_HEADS, HG_DK, HG_DV), lambda b: (b, 0, 0, 0, 0))
    in_specs = [col(7), col(8), col(9), col(10), col(11), pl.BlockSpec((1, W), full2), pl.BlockSpec((1, W), full2)]
    args = [proj] * 5 + [lb_l[0][None, :], lb_l[1][None, :]]
    c_shape = jax.ShapeDtypeStruct((B * T, W), BF16)
    c_spec = pl.BlockSpec((T, W), lambda b: (b, 0))
    if latent:
        in_specs.append(s_spec)
        args.append(S0)
        out_shape, out_specs = c_shape, c_spec
    else:
        out_shape = (c_shape, jax.ShapeDtypeStruct((B, 2, HG_HEADS, HG_DK, HG_DV), F32))
        out_specs = (c_spec, s_spec)
    scratch = [pltpu.VMEM((T, W), F32), pltpu.VMEM((T, W), F32), pltpu.VMEM((2, HG_HEADS, HG_DV, HG_DK), F32)]
    return pl.pallas_call(
        functools.partial(_hgrn_kernel, T=T, latent=latent),
        out_shape=out_shape, grid=(B,), in_specs=in_specs, out_specs=out_specs, scratch_shapes=scratch,
        compiler_params=_cparams(("arbitrary",)),
        name="hgrn_lat" if latent else "hgrn_ctx",
    )(*args)


def _split_heads(a, n):
    B_, T_, W_ = a.shape
    return a.reshape(B_, T_, n, W_ // n).transpose(0, 2, 1, 3)


def _merge_heads(a):
    B_, n, T_, d = a.shape
    return a.transpose(0, 2, 1, 3).reshape(B_, T_, n * d)


def _flip_t(a):
    return jnp.flip(a, axis=2)


def _to_chunks(a, L):
    B_, H_, T_ = a.shape[:3]
    return jnp.moveaxis(a.reshape(B_, H_, T_ // L, L, *a.shape[3:]), 2, 0)


def _from_chunks(a):
    a = jnp.moveaxis(a, 0, 2)
    return a.reshape(a.shape[0], a.shape[1], -1, *a.shape[4:])


def _axial_rope(x):
    T_ = x.shape[2]
    t = jnp.arange(T_)
    row = (t // GRID_W).astype(F32)
    col = (t % GRID_W).astype(F32)
    half = x.shape[-1] // 2
    inv = ROPE_BASE ** (-jnp.arange(0, half, 2, dtype=F32) / half)

    def rot(xp, pos):
        ang = pos[:, None] * inv[None, :]
        cos, sin = jnp.cos(ang).astype(x.dtype), jnp.sin(ang).astype(x.dtype)
        x1, x2 = xp[..., :half // 2], xp[..., half // 2:]
        return jnp.concatenate([x1 * cos - x2 * sin, x1 * sin + x2 * cos], -1)

    return jnp.concatenate([rot(x[..., :half], row), rot(x[..., half:], col)], -1)


def _mlstm_chunkwise(q, k, v, ig, lf, C0, n0, m0):
    L = MA_CHUNK
    causal = jnp.tril(jnp.ones((L, L), bool))

    def step(carry, inp):
        C, n, m = carry
        qc, kc, vc, ic, fc = inp
        b = jnp.cumsum(fc, axis=-1)
        dmat = jnp.where(causal, b[..., :, None] - b[..., None, :] + ic[..., None, :], -jnp.inf)
        g = b + m[..., None]
        m_t = jnp.maximum(g, jnp.max(dmat, -1))
        w_inter = jnp.exp(g - m_t)
        s = jnp.einsum('bhtd,bhsd->bhts', qc, kc) * jnp.exp(dmat - m_t[..., None])
        num = w_inter[..., None] * jnp.einsum('bhtd,bhde->bhte', qc, C) + jnp.einsum('bhts,bhse->bhte', s, vc)
        den = w_inter * jnp.einsum('bhtd,bhd->bht', qc, n) + jnp.sum(s, -1)
        h = num / jnp.maximum(jnp.abs(den), jnp.exp(-m_t))[..., None]
        m_new = m_t[..., -1]
        decay_state = jnp.exp(b[..., -1] + m - m_new)
        kw = kc * jnp.exp(b[..., -1:] - b + ic - m_new[..., None])[..., None]
        C_new = decay_state[..., None, None] * C + jnp.einsum('bhsd,bhse->bhde', kw, vc)
        n_new = decay_state[..., None] * n + jnp.sum(kw, -2)
        return (C_new, n_new, m_new), h

    xs = (_to_chunks(q, L), _to_chunks(k, L), _to_chunks(v, L), _to_chunks(ig, L), _to_chunks(lf, L))
    (C, n, m), hs = lax.scan(step, (C0.astype(F32), n0.astype(F32), m0.astype(F32)), xs)
    return _from_chunks(hs), C, n, m


def _hgrn2_chunkwise(q, k, i, logf, S0):
    L = HG_CHUNK
    causal = jnp.tril(jnp.ones((L, L), bool))[:, :, None]

    def step(S, inp):
        qc, kc, ic, lc = inp
        A = jnp.cumsum(lc, axis=2)
        o_inter = jnp.einsum('bhtd,bhde->bhte', qc * jnp.exp(A), S)
        diff = jnp.where(causal, A[:, :, :, None, :] - A[:, :, None, :, :], -jnp.inf)
        att = jnp.einsum('bhtd,bhtsd,bhsd->bhts', qc, jnp.exp(diff), kc)
        o = o_inter + jnp.einsum('bhts,bhse->bhte', att, ic)
        A_last = A[:, :, -1]
        S_new = jnp.exp(A_last)[..., None] * S + jnp.einsum('bhsd,bhse->bhde', kc * jnp.exp(A_last[:, :, None] - A), ic)
        return S_new, o

    xs = (_to_chunks(q, L), _to_chunks(k, L), _to_chunks(i, L), _to_chunks(logf, L))
    S, os_ = lax.scan(step, S0.astype(F32), xs)
    return _from_chunks(os_), S


def _context_attention(q, k, v):
    B_, H_, S_, D_ = q.shape
    nb = S_ // CTX_QBLOCK
    qb = jnp.moveaxis(q.reshape(B_, H_, nb, CTX_QBLOCK, D_), 2, 0)

    def one(qblk):
        s = jnp.einsum('bhqd,bhkd->bhqk', qblk, k).astype(F32)
        p = jax.nn.softmax(s, axis=-1).astype(v.dtype)
        return jnp.einsum('bhqk,bhkd->bhqd', p, v)

    o = lax.map(one, qb)
    return jnp.moveaxis(o, 0, 2).reshape(B_, H_, S_, D_)


def _latent_na(q, k, v, ck, cv, rpb):
    B_, H_, T_, D_ = q.shape
    rows = T_ // GRID_W
    kr = min(NA_KR_MAX, rows)
    nj = GRID_W // NA_QB
    qcol = np.arange(GRID_W)
    c0 = np.clip(qcol - NA_KC // 2, 0, GRID_W - NA_KC)
    cb = np.clip(np.arange(nj) * NA_QB - NA_KC // 2, 0, GRID_W - NA_KB)
    col_idx = cb[:, None] + np.arange(NA_KB)[None, :]
    kcol = col_idx[:, None, :]
    c0b = c0.reshape(nj, NA_QB)[:, :, None]
    col_mask = (kcol >= c0b) & (kcol < c0b + NA_KC)
    dc_idx = np.clip(kcol - qcol.reshape(nj, NA_QB)[:, :, None] + NA_KC - 1, 0, 2 * NA_KC - 2)
    rpb_cols = rpb[:, :, dc_idx]
    kg = k.reshape(B_, H_, rows, GRID_W, D_)
    vg = v.reshape(B_, H_, rows, GRID_W, D_)
    q_rows = jnp.moveaxis(q.reshape(B_, H_, rows, nj, NA_QB, D_), 2, 0)
    n_loc = kr * NA_KB

    def one_row(args):
        r, qr = args
        r0 = jnp.clip(r - kr // 2, 0, rows - kr)
        ks = lax.dynamic_slice_in_dim(kg, r0, kr, axis=2)[:, :, :, col_idx]
        vs = lax.dynamic_slice_in_dim(vg, r0, kr, axis=2)[:, :, :, col_idx]
        dr_idx = r0 + jnp.arange(kr) - r + (NA_KR_MAX - 1)
        bias = jnp.transpose(jnp.take(rpb_cols, dr_idx, axis=1), (0, 2, 3, 1, 4)).astype(F32)
        s_loc = jnp.einsum('bhjqd,bhrjkd->bhjqrk', qr, ks).astype(F32) + bias
        s_loc = jnp.where(col_mask[:, :, None, :], s_loc, -jnp.inf)
        s_ctx = jnp.einsum('bhjqd,bhpd->bhjqp', qr, ck).astype(F32)
        s = jnp.concatenate([s_loc.reshape(B_, H_, nj, NA_QB, n_loc), s_ctx], -1)
        p = jax.nn.softmax(s, axis=-1).astype(v.dtype)
        p_loc = p[..., :n_loc].reshape(B_, H_, nj, NA_QB, kr, NA_KB)
        return jnp.einsum('bhjqrk,bhrjkd->bhjqd', p_loc, vs) + jnp.einsum('bhjqp,bhpd->bhjqd', p[..., n_loc:], cv)

    out = lax.map(one_row, (jnp.arange(rows), q_rows))
    return jnp.moveaxis(out, 0, 2).reshape(B_, H_, T_, D_)


def _mixers(pr, gcol, latent, mC0, mn0, mm0, hS0, ctx_k, ctx_v, fbias_l, lb_l, rpb_l):
    W = BRANCH_W
    aq, ak, av, ao = (pr[..., i * W:(i + 1) * W] for i in range(4))
    bq, bk, bv = (pr[..., (4 + i) * W:(5 + i) * W] for i in range(3))
    cf_f, cf_b, cq, ci, cg = (pr[..., (7 + i) * W:(8 + i) * W] for i in range(5))
    ai_f, af_f, ai_b, af_b = (gcol[..., 4 * i:4 * i + 4] for i in range(4))
    q = _split_heads(aq, MA_HEADS)
    k = _split_heads(ak, MA_HEADS) * (MA_DK ** -0.5)
    v = _split_heads(av, MA_HEADS)
    if latent:
        q, k = _axial_rope(q), _axial_rope(k)
    fb = fbias_l.astype(F32)
    ig_f = jnp.swapaxes(ai_f, 1, 2)
    ig_b = jnp.swapaxes(ai_b, 1, 2)
    lf_f = jax.nn.log_sigmoid(jnp.swapaxes(af_f, 1, 2) + fb[0][:, None])
    lf_b = jax.nn.log_sigmoid(jnp.swapaxes(af_b, 1, 2) + fb[1][:, None])
    hf, Cf, nf, mf = _mlstm_chunkwise(q, k, v, ig_f, lf_f, mC0[:, 0], mn0[:, 0], mm0[:, 0])
    hb, Cb, nb_, mb = _mlstm_chunkwise(_flip_t(q), _flip_t(k), _flip_t(v), _flip_t(ig_b), _flip_t(lf_b),
                                       mC0[:, 1], mn0[:, 1], mm0[:, 1])
    a_out = _merge_heads(jax.nn.sigmoid(_split_heads(ao, MA_HEADS)) * (hf + _flip_t(hb)))
    nq = _split_heads(bq, NA_HEADS) * (NA_DH ** -0.5)
    nk = _split_heads(bk, NA_HEADS)
    nv = _split_heads(bv, NA_HEADS)
    if latent:
        b_heads = _latent_na(nq, nk, nv, ctx_k, ctx_v, rpb_l)
    else:
        b_heads = _context_attention(nq, nk, nv)
        ctx_k, ctx_v = nk, nv
    b_out = _merge_heads(b_heads)
    f_f = lb_l[0] + (1.0 - lb_l[0]) * jax.nn.sigmoid(cf_f)
    f_b = lb_l[1] + (1.0 - lb_l[1]) * jax.nn.sigmoid(cf_b)
    f_f, f_b = _split_heads(f_f, HG_HEADS), _split_heads(f_b, HG_HEADS)
    hq = _split_heads(jax.nn.silu(cq), HG_HEADS)
    hi = _split_heads(ci, HG_HEADS)
    of, Sf = _hgrn2_chunkwise(hq, 1.0 - f_f, hi, jnp.log(f_f), hS0[:, 0])
    ob, Sb = _hgrn2_chunkwise(_flip_t(hq), _flip_t(1.0 - f_b), _flip_t(hi), _flip_t(jnp.log(f_b)), hS0[:, 1])
    o = of + _flip_t(ob)
    o = o * lax.rsqrt(jnp.mean(o * o, -1, keepdims=True) + RMS_EPS)
    c_out = _merge_heads(o) * jax.nn.silu(cg)
    states = (ctx_k, ctx_v, jnp.stack([Cf, Cb], 1), jnp.stack([nf, nb_], 1), jnp.stack([mf, mb], 1), jnp.stack([Sf, Sb], 1))
    return a_out.astype(BF16), b_out.astype(BF16), c_out.astype(BF16), states


def kernel(x_prompt, x_sample, c, cache_na_k, cache_na_v, state_mlstm_C, state_mlstm_n, state_mlstm_m, state_hgrn,
           c_ctx, w_mod, b_mod, w_in, b_in, mlstm_fbias, hgrn_lb_logits, na_rpb, w_branch, w_out, ln_g, ln_b,
           w_rg, w_re, w_e1, w_e3, w_e2):
    lb_cum = jnp.cumsum(jax.nn.softmax(hgrn_lb_logits.astype(F32), axis=1), axis=1)
    lb_all = lb_cum - lb_cum[:, :1]

    cs = jnp.zeros((N_MODROWS, D_MODEL), F32).at[0].set(c_ctx).at[1:1 + DEC_BATCH].set(c)
    mod = _modulation(cs, w_mod, b_mod).reshape(DEPTH, N_MODROWS, 6, 1, D_MODEL)

    x = jnp.concatenate([x_prompt.reshape(N_CTX, D_MODEL), x_sample.reshape(N_LAT, D_MODEL)], 0)
    h = _prep(x, mod[0])

    sts = []
    for l in range(DEPTH):
        w_main = jnp.concatenate([w_in[l][:, :GATE_COL0], w_in[l][:, GATE_COL0 + N_GATES:]], 1).astype(BF16)
        b_main = jnp.concatenate([b_in[l][:GATE_COL0], b_in[l][GATE_COL0 + N_GATES:]])[None, :]
        proj = _inproj(h, w_main, b_main)
        gcol, gt = _gates(h, w_in[l][:, GATE_COL0:GATE_COL0 + N_GATES], b_in[l][GATE_COL0:GATE_COL0 + N_GATES])
        gt3 = gt.reshape(N_GATES, N_TOK // MA_CHUNK, MA_CHUNK).transpose(1, 0, 2)

        a_c, Cc, nc, mc = _mlstm(proj, gcol, gt3, mlstm_fbias[l], False)
        a_l = _mlstm(proj, gcol, gt3, mlstm_fbias[l], True, state_mlstm_C[:, l], state_mlstm_n[:, l], state_mlstm_m[:, l])
        b_c, kc, vc = _ctx_attention(proj)
        b_l = _lat_attention(proj, cache_na_k[:, l], cache_na_v[:, l], _na_bias_table(na_rpb[l]))
        c_c, Sc = _hgrn(proj, lb_all[:, l], False)
        c_l = _hgrn(proj, lb_all[:, l], True, state_hgrn[:, l])
        sts.append((kc, vc, Cc, nc.reshape(BATCH, 2, MA_HEADS, MA_DK), mc[:, :, 0].reshape(BATCH, 2, MA_HEADS), Sc))
        a = jnp.concatenate([a_c, a_l], 0)
        b = jnp.concatenate([b_c, b_l], 0)
        cc = jnp.concatenate([c_c, c_l], 0)

        wr = jnp.zeros((D_MODEL, 128), F32).at[:, :N_GROUPS].set(w_rg[l]).at[:, N_GROUPS:N_GROUPS + N_EXPERTS].set(w_re[l])
        x1, h2, gate, gsel = _merge(a, b, cc, proj, x, mod[l], w_branch[l].astype(BF16), w_out[l].astype(BF16),
                                    ln_g[l, 0][None], ln_b[l, 0][None], wr)
        y2 = _moe(h2, gate, gsel, w_e1[l], w_e3[l], w_e2[l])
        x, h = _final(x1, y2, mod[l], ln_g[l, 1][None], ln_b[l, 1][None], mod[min(l + 1, DEPTH - 1)])

    dt = x_prompt.dtype
    new_na_k = jnp.stack([s[0] for s in sts], 1)
    new_na_v = jnp.stack([s[1] for s in sts], 1)
    new_C = jnp.stack([s[2] for s in sts], 1).astype(dt)
    new_n = jnp.stack([s[3] for s in sts], 1).astype(dt)
    new_m = jnp.stack([s[4] for s in sts], 1).astype(dt)
    new_S = jnp.stack([s[5] for s in sts], 1).astype(dt)
    y_prompt = x[:N_CTX].reshape(BATCH, SEQ, D_MODEL)
    y_sample = x[N_CTX:].reshape(DEC_BATCH, DEC_SEQ, D_MODEL)
    return (y_prompt, y_sample, new_na_k, new_na_v, new_C, new_n, new_m, new_S)
```

```python
import functools

import numpy as np
import jax
import jax.numpy as jnp
from jax import lax
from jax.experimental import pallas as pl
from jax.experimental.pallas import tpu as pltpu

F32 = jnp.float32
BF16 = jnp.bfloat16
HI = lax.Precision.HIGHEST

D_MODEL = 1024
BATCH = 16
SEQ = 256
DEPTH = 2
DEC_BATCH = 4
DEC_SEQ = 1024
PAST_LEN = 256
GRID_W = 64
MA_HEADS = 4
MA_DK = 128
MA_DV = 128
MA_CHUNK = 64
NA_HEADS = 8
NA_DH = 64
NA_KR_MAX = 8
NA_KC = 16
HG_HEADS = 4
HG_DK = 128
HG_DV = 128
HG_CHUNK = 32
BRANCH_W = 512
N_GROUPS = 4
EXP_PER_GROUP = 4
N_EXPERTS = N_GROUPS * EXP_PER_GROUP
D_EXPERT = 512
ROPE_BASE = 10000.0
LN_EPS = 1e-5
RMS_EPS = 1e-6
DEEPNORM_ALPHA = (2 * DEPTH) ** 0.25

N_CTX = BATCH * SEQ
N_LAT = DEC_BATCH * DEC_SEQ
N_TOK = N_CTX + N_LAT
N_MODROWS = 8
GATE_COL0 = 4 * BRANCH_W
N_GATES = 4 * MA_HEADS
N_IN = 9232
P_COLS = N_IN - N_GATES
MOE_TM = 256
MOE_NT = N_TOK // MOE_TM + N_GROUPS
VMEM_LIMIT = 48 * 1024 * 1024

_NT = (((1,), (1,)), ((), ()))
_TN = (((0,), (0,)), ((), ()))


def _cparams(*sem):
    return pltpu.CompilerParams(dimension_semantics=sem, vmem_limit_bytes=VMEM_LIMIT)


def _mod_row(tile, tm):
    return jnp.maximum((tile * tm) // DEC_SEQ - (N_CTX // DEC_SEQ - 1), 0)


def _mod_spec(l, part, tm):
    return pl.BlockSpec((None, None, None, 1, D_MODEL), lambda i: (l, _mod_row(i, tm), part, 0, 0))


def _pair_specs(tm):
    nc = N_CTX // tm
    return (pl.BlockSpec((tm, D_MODEL), lambda i: (jnp.minimum(i, nc - 1), 0)),
            pl.BlockSpec((tm, D_MODEL), lambda i: (jnp.maximum(i - nc, 0), 0)))


def _pair_read(i, tm, c_ref, l_ref):
    return jnp.where(i < N_CTX // tm, c_ref[...], l_ref[...])


def _silu(x):
    return x * jax.nn.sigmoid(x)


def _layer_norm(x, g, b):
    mu = jnp.mean(x, -1, keepdims=True)
    xc = x - mu
    var = jnp.mean(xc * xc, -1, keepdims=True)
    return xc * lax.rsqrt(var + LN_EPS) * g + b


def _log_sigmoid(x):
    return jnp.minimum(x, 0.0) - jnp.log(1.0 + jnp.exp(-jnp.abs(x)))


def _tri(n, upper):
    r = lax.broadcasted_iota(jnp.int32, (n, n), 0)
    c = lax.broadcasted_iota(jnp.int32, (n, n), 1)
    return jnp.where((r <= c) if upper else (r >= c), 1.0, 0.0).astype(F32)


def _mod_kernel(c_ref, w_ref, b_ref, o_ref):
    s = _silu(c_ref[...])
    o_ref[...] = jnp.dot(s.astype(BF16), w_ref[...].astype(BF16), preferred_element_type=F32) + b_ref[...]


def _modulation(cs, w_mod, b_mod):
    tn = 1024
    return pl.pallas_call(
        _mod_kernel,
        out_shape=jax.ShapeDtypeStruct((DEPTH, N_MODROWS, 6 * D_MODEL), F32),
        grid=(DEPTH, 6 * D_MODEL // tn),
        in_specs=[pl.BlockSpec((N_MODROWS, D_MODEL), lambda l, j: (0, 0)),
                  pl.BlockSpec((None, D_MODEL, tn), lambda l, j: (l, 0, j)),
                  pl.BlockSpec((None, 1, tn), lambda l, j: (l, 0, j))],
        out_specs=pl.BlockSpec((None, N_MODROWS, tn), lambda l, j: (l, 0, j)),
        compiler_params=_cparams("arbitrary", "arbitrary"),
        name="modulation",
    )(cs, w_mod, b_mod.reshape(DEPTH, 1, 6 * D_MODEL))


def _prep_kernel(xc_ref, xl_ref, sh_ref, sc_ref, h_ref, *, tm):
    x = _pair_read(pl.program_id(0), tm, xc_ref, xl_ref)
    h_ref[...] = (x * (1.0 + sc_ref[...]) + sh_ref[...]).astype(BF16)


def _prep(xc, xl, mod):
    tm = 1024
    return pl.pallas_call(
        functools.partial(_prep_kernel, tm=tm),
        out_shape=jax.ShapeDtypeStruct((N_TOK, D_MODEL), BF16),
        grid=(N_TOK // tm,),
        in_specs=[*_pair_specs(tm), _mod_spec(0, 0, tm), _mod_spec(0, 1, tm)],
        out_specs=pl.BlockSpec((tm, D_MODEL), lambda i: (i, 0)),
        compiler_params=_cparams("arbitrary"),
        name="prep",
    )(xc, xl, mod, mod)


INPROJ_TN = 512
N_PLAIN_TILES = GATE_COL0 // INPROJ_TN


def _inproj_kernel(h_ref, wa_ref, wb_ref, b_ref, o_ref):
    j = pl.program_id(1)

    @pl.when(j < N_PLAIN_TILES)
    def _():
        o_ref[...] = jnp.dot(h_ref[...], wa_ref[...].astype(BF16), preferred_element_type=F32) + b_ref[...]

    @pl.when(j >= N_PLAIN_TILES)
    def _():
        w = jnp.concatenate([wa_ref[...], wb_ref[...]], 1)[:, N_GATES:N_GATES + INPROJ_TN]
        o_ref[...] = jnp.dot(h_ref[...], w.astype(BF16), preferred_element_type=F32) + b_ref[...]


def _inproj(h, w_in, b_main, l):
    tm, tn = 2048, INPROJ_TN
    return pl.pallas_call(
        _inproj_kernel,
        out_shape=jax.ShapeDtypeStruct((N_TOK, P_COLS), F32),
        grid=(N_TOK // tm, P_COLS // tn),
        in_specs=[pl.BlockSpec((tm, D_MODEL), lambda i, j: (i, 0)),
                  pl.BlockSpec((None, D_MODEL, tn), lambda i, j: (l, 0, j)),
                  pl.BlockSpec((None, D_MODEL, 128), lambda i, j: (l, 0, (j + 1) * (tn // 128))),
                  pl.BlockSpec((1, tn), lambda i, j: (0, j))],
        out_specs=pl.BlockSpec((tm, tn), lambda i, j: (i, j)),
        compiler_params=_cparams("arbitrary", "arbitrary"),
        name="inproj",
    )(h, w_in, w_in, b_main)


def _gates_kernel(h_ref, w_ref, wt_ref, b_ref, bt_ref, gc_ref, gt_ref):
    h = h_ref[...]
    gc_ref[...] = jnp.dot(h, w_ref[...], preferred_element_type=F32) + b_ref[...]
    gt_ref[...] = lax.dot_general(wt_ref[...], h, _NT, preferred_element_type=F32) + bt_ref[...]


def _gates(h, wg, bg):
    tm = 1024
    w = jnp.zeros((D_MODEL, 128), BF16).at[:, :N_GATES].set(wg.astype(BF16))
    b = jnp.zeros((1, 128), F32).at[0, :N_GATES].set(bg)
    return pl.pallas_call(
        _gates_kernel,
        out_shape=(jax.ShapeDtypeStruct((N_TOK, 128), F32), jax.ShapeDtypeStruct((N_GATES, N_TOK), F32)),
        grid=(N_TOK // tm,),
        in_specs=[pl.BlockSpec((tm, D_MODEL), lambda i: (i, 0)),
                  pl.BlockSpec((D_MODEL, 128), lambda i: (0, 0)),
                  pl.BlockSpec((N_GATES, D_MODEL), lambda i: (0, 0)),
                  pl.BlockSpec((1, 128), lambda i: (0, 0)),
                  pl.BlockSpec((N_GATES, 1), lambda i: (0, 0))],
        out_specs=(pl.BlockSpec((tm, 128), lambda i: (i, 0)), pl.BlockSpec((N_GATES, tm), lambda i: (0, i))),
        compiler_params=_cparams("arbitrary"),
        name="gates",
    )(h, w, wg.T.astype(BF16), b, bg.reshape(N_GATES, 1))


HEADS_PER_BLK = 128 // NA_DH
NA_NBLK = NA_HEADS // HEADS_PER_BLK
NA_QSCALE = NA_DH ** -0.5
Q_COL, K_COL, V_COL = 16, 20, 24


def _ctx_attn_kernel(*refs, first):
    q_ref, k_ref, v_ref = refs[:3]
    o_ref, ko_ref, vo_ref = refs[-3:]
    q = q_ref[...] * NA_QSCALE
    k = k_ref[...]
    v = v_ref[...]
    outs = []
    for hh in range(HEADS_PER_BLK):
        sl = slice(hh * NA_DH, (hh + 1) * NA_DH)
        kh, vh = k[:, sl], v[:, sl]
        ko_ref[hh] = kh
        vo_ref[hh] = vh
        s = lax.dot_general(q[:, sl].astype(BF16), kh.astype(BF16), _NT, preferred_element_type=F32)
        e = jnp.exp(s - jnp.max(s, -1, keepdims=True))
        p = e * (1.0 / jnp.sum(e, -1, keepdims=True))
        outs.append(jnp.dot(p.astype(BF16), vh.astype(BF16), preferred_element_type=F32))
    o_ref[...] = jnp.concatenate(outs, -1).astype(BF16)


def _ctx_attention(proj, l, prev_k=None, prev_v=None):
    cb = lambda base: (lambda b, j: (b, base + j))
    kv_shape = jax.ShapeDtypeStruct((BATCH, DEPTH, NA_HEADS, SEQ, NA_DH), F32)
    kv_spec = pl.BlockSpec((None, None, HEADS_PER_BLK, SEQ, NA_DH), lambda b, j: (b, l, j, 0, 0))
    in_specs = [pl.BlockSpec((SEQ, 128), cb(Q_COL)), pl.BlockSpec((SEQ, 128), cb(K_COL)), pl.BlockSpec((SEQ, 128), cb(V_COL))]
    args = [proj, proj, proj]
    aliases = {}
    if prev_k is not None:
        in_specs += [pl.BlockSpec(memory_space=pl.ANY)] * 2
        args += [prev_k, prev_v]
        aliases = {3: 1, 4: 2}
    return pl.pallas_call(
        functools.partial(_ctx_attn_kernel, first=prev_k is None),
        out_shape=(jax.ShapeDtypeStruct((N_TOK, BRANCH_W), BF16), kv_shape, kv_shape),
        grid=(BATCH, NA_NBLK),
        in_specs=in_specs,
        out_specs=(pl.BlockSpec((SEQ, 128), lambda b, j: (b, j)), kv_spec, kv_spec),
        input_output_aliases=aliases,
        compiler_params=_cparams("arbitrary", "arbitrary"),
        name="ctx_attention",
    )(*args)


NA_ROWS = DEC_SEQ // GRID_W
NA_KR = min(NA_KR_MAX, NA_ROWS)
NA_QROWS = 4
NA_QT = NA_ROWS // NA_QROWS
NA_WROWS = NA_KR + NA_QROWS - 1
NA_WKEYS = NA_WROWS * GRID_W


def _na_window_start(t):
    return min(max(t * NA_QROWS - NA_KR // 2, 0), NA_ROWS - NA_WROWS)


def _na_bias_table(rpb):
    c = np.arange(GRID_W)
    c0 = np.clip(c - NA_KC // 2, 0, GRID_W - NA_KC)
    kc = np.arange(GRID_W)
    valid = (kc[None, :] >= c0[:, None]) & (kc[None, :] < c0[:, None] + NA_KC)
    dc = kc[None, :] - c[:, None] + NA_KC - 1
    onehot = (dc[None] == np.arange(2 * NA_KC - 1)[:, None, None]) & valid[None]
    toep = jnp.einsum('hrd,dcx->hrcx', rpb.astype(F32), jnp.asarray(onehot, F32), precision=HI)
    toep = jnp.where(valid[None, None], toep, -jnp.inf)
    ninf = jnp.full((NA_HEADS, GRID_W, GRID_W), -jnp.inf, F32)
    tiles = []
    for t in range(NA_QT):
        w0 = _na_window_start(t)
        qrows = []
        for r in range(t * NA_QROWS, (t + 1) * NA_QROWS):
            r0 = min(max(r - NA_KR // 2, 0), NA_ROWS - NA_KR)
            blocks = []
            for kr in range(w0, w0 + NA_WROWS):
                inside = r0 <= kr < r0 + NA_KR
                blocks.append(toep[:, kr - r + NA_KR_MAX - 1] if inside else ninf)
            qrows.append(jnp.concatenate(blocks, -1))
        tiles.append(jnp.concatenate(qrows, 1))
    return jnp.stack(tiles, 1)


def _lat_attn_kernel(q_ref, k_ref, v_ref, ck_ref, cv_ref, bias_ref, prev_ref, o_ref):
    q = (q_ref[...] * NA_QSCALE).astype(BF16)
    k = k_ref[...].astype(BF16)
    v = v_ref[...].astype(BF16)
    nq = NA_QROWS * GRID_W
    for t in range(NA_QT):
        w0 = _na_window_start(t)
        qs = slice(t * nq, (t + 1) * nq)
        ws = slice(w0 * GRID_W, (w0 + NA_WROWS) * GRID_W)
        outs = []
        for hh in range(HEADS_PER_BLK):
            sl = slice(hh * NA_DH, (hh + 1) * NA_DH)
            qh = q[qs, sl]
            s_loc = lax.dot_general(qh, k[ws, sl], _NT, preferred_element_type=F32) + bias_ref[hh, t]
            s_ctx = lax.dot_general(qh, ck_ref[hh].astype(BF16), _NT, preferred_element_type=F32)
            m = jnp.maximum(jnp.max(s_loc, -1, keepdims=True), jnp.max(s_ctx, -1, keepdims=True))
            e_loc = jnp.exp(s_loc - m)
            e_ctx = jnp.exp(s_ctx - m)
            inv = 1.0 / (jnp.sum(e_loc, -1, keepdims=True) + jnp.sum(e_ctx, -1, keepdims=True))
            acc = (jnp.dot(e_loc.astype(BF16), v[ws, sl], preferred_element_type=F32)
                   + jnp.dot(e_ctx.astype(BF16), cv_ref[hh].astype(BF16), preferred_element_type=F32))
            outs.append(acc * inv)
        o_ref[qs, :] = jnp.concatenate(outs, -1).astype(BF16)


def _lat_attention(proj, ck, cv, bias, l, b_out):
    rb0 = N_CTX // DEC_SEQ
    cb = lambda base: (lambda j, b: (rb0 + b, base + j))
    c_spec = pl.BlockSpec((None, None, HEADS_PER_BLK, PAST_LEN, NA_DH), lambda j, b: (b, l, j, 0, 0))
    return pl.pallas_call(
        _lat_attn_kernel,
        out_shape=jax.ShapeDtypeStruct((N_TOK, BRANCH_W), BF16),
        grid=(NA_NBLK, DEC_BATCH),
        in_specs=[pl.BlockSpec((DEC_SEQ, 128), cb(Q_COL)), pl.BlockSpec((DEC_SEQ, 128), cb(K_COL)),
                  pl.BlockSpec((DEC_SEQ, 128), cb(V_COL)), c_spec, c_spec,
                  pl.BlockSpec((HEADS_PER_BLK, NA_QT, NA_QROWS * GRID_W, NA_WKEYS), lambda j, b: (j, 0, 0, 0)),
                  pl.BlockSpec(memory_space=pl.ANY)],
        out_specs=pl.BlockSpec((DEC_SEQ, 128), lambda j, b: (rb0 + b, j)),
        input_output_aliases={6: 0},
        compiler_params=_cparams("arbitrary", "arbitrary"),
        name="lat_attention",
    )(proj, proj, proj, ck, cv, bias, b_out)


MA_KSCALE = MA_DK ** -0.5


def _rope_tables(T):
    t = np.arange(T)
    half = MA_DK // 2
    inv = ROPE_BASE ** (-jnp.arange(0, half, 2, dtype=F32) / half)
    ang_r = jnp.asarray((t // GRID_W).astype(np.float32))[:, None] * inv[None, :]
    ang_c = jnp.asarray((t % GRID_W).astype(np.float32))[:, None] * inv[None, :]
    cos = jnp.concatenate([jnp.cos(ang_r)] * 2 + [jnp.cos(ang_c)] * 2, -1)
    sin = jnp.concatenate([-jnp.sin(ang_r), jnp.sin(ang_r), -jnp.sin(ang_c), jnp.sin(ang_c)], -1)
    return cos, sin


def _mlstm_kernel(*refs, T, latent):
    if latent:
        (p_ref, gc_ref, gt_ref, fbc_ref, fbr_ref, cos_ref, sin_ref, c0_ref, n0_ref, m0_ref, prev_ref,
         a_ref, qs, ks, vs, hf, hb, Cs, ns, ms) = refs
    else:
        p_ref, gc_ref, gt_ref, fbc_ref, fbr_ref = refs[:5]
        a_ref, co_ref, no_ref, mo_ref, qs, ks, vs, hf, hb, Cs, ns, ms = refs[-12:]
    L = MA_CHUNK
    NC = T // L
    W = BRANCH_W

    lane = lax.broadcasted_iota(jnp.int32, (T, MA_DK), 1)
    lo_half = (lane % (MA_DK // 2)) < (MA_DK // 4)

    def rope(x):
        if not latent:
            return x
        swapped = jnp.where(lo_half, pltpu.roll(x, MA_DK - MA_DK // 4, 1), pltpu.roll(x, MA_DK // 4, 1))
        return x * cos_ref[...] + swapped * sin_ref[...]

    for h in range(MA_HEADS):
        hs = slice(h * MA_DK, (h + 1) * MA_DK)
        qs[:, hs] = rope(p_ref[:, hs]).astype(BF16)
        ks[:, hs] = rope(p_ref[:, W + h * MA_DK:W + (h + 1) * MA_DK] * MA_KSCALE).astype(BF16)
    vs[...] = p_ref[:, 2 * W:3 * W].astype(BF16)

    for d in range(2):
        for h in range(MA_HEADS):
            Cs[d, h] = c0_ref[d, h] if latent else jnp.zeros((MA_DK, MA_DV), F32)
    ns[...] = n0_ref[...] if latent else jnp.zeros_like(ns)
    ms[...] = m0_ref[...] if latent else jnp.zeros_like(ms)

    low = _tri(L, False)
    upp = _tri(L, True)
    rr = lax.broadcasted_iota(jnp.int32, (L, L), 0)
    cc = lax.broadcasted_iota(jnp.int32, (L, L), 1)
    fbc = fbc_ref[...]
    fbr = fbr_ref[...]

    def chunk(c, d):
        t0 = pl.multiple_of(c * L, L)
        gc = gc_ref[pl.ds(t0, L), :]
        gt = gt_ref[c]
        lfc = _log_sigmoid(gc + fbc)
        lfr = _log_sigmoid(gt + fbr)
        if d == 0:
            bcol = jnp.dot(low, lfc, precision=HI, preferred_element_type=F32)
            brow = jnp.dot(lfr, upp, precision=HI, preferred_element_type=F32)
        else:
            bcol = jnp.dot(upp, lfc, precision=HI, preferred_element_type=F32)
            brow = jnp.dot(lfr, low, precision=HI, preferred_element_type=F32)
        mask = (cc <= rr) if d == 0 else (cc >= rr)
        last = L - 1 if d == 0 else 0
        for h in range(MA_HEADS):
            gi = 2 * d * MA_HEADS + h
            gf = gi + MA_HEADS
            sidx = d * MA_HEADS + h
            hs = slice(h * MA_DK, (h + 1) * MA_DK)
            bc = bcol[:, gf:gf + 1]
            br = brow[gf:gf + 1, :]
            ir = gt[gi:gi + 1, :]
            ic = gc[:, gi:gi + 1]
            m = ms[sidx:sidx + 1, 0:1]
            n = ns[sidx:sidx + 1, :]
            C = Cs[d, h]
            q = qs[pl.ds(t0, L), hs]
            k = ks[pl.ds(t0, L), hs]
            v = vs[pl.ds(t0, L), hs]
            dmat = jnp.where(mask, bc - br + ir, -jnp.inf)
            g = bc + m
            m_t = jnp.maximum(g, jnp.max(dmat, -1, keepdims=True))
            w_inter = jnp.exp(g - m_t)
            s = lax.dot_general(q, k, _NT, preferred_element_type=F32) * jnp.exp(dmat - m_t)
            num = (w_inter * jnp.dot(q, C.astype(BF16), preferred_element_type=F32)
                   + jnp.dot(s.astype(BF16), v, preferred_element_type=F32))
            den = w_inter * jnp.sum(q.astype(F32) * n, -1, keepdims=True) + jnp.sum(s, -1, keepdims=True)
            hout = num / jnp.maximum(jnp.abs(den), jnp.exp(-m_t))
            (hf if d == 0 else hb)[pl.ds(t0, L), hs] = hout
            m_new = m_t[last:last + 1, :]
            b_last = bc[last:last + 1, :]
            decay = jnp.exp(b_last + m - m_new)
            kw = k.astype(F32) * jnp.exp(b_last - bc + ic - m_new)
            Cs[d, h] = decay * C + lax.dot_general(kw.astype(BF16), v, _TN, preferred_element_type=F32)
            ns[sidx:sidx + 1, :] = decay * n + jnp.sum(kw, 0, keepdims=True)
            ms[sidx:sidx + 1, :] = jnp.broadcast_to(m_new, (1, 128))

    def body(i, carry):
        chunk(i, 0)
        chunk(NC - 1 - i, 1)
        return carry

    lax.fori_loop(0, NC, body, 0)

    a_ref[...] = (jax.nn.sigmoid(p_ref[:, 3 * W:4 * W]) * (hf[...] + hb[...])).astype(BF16)
    if not latent:
        for d in range(2):
            for h in range(MA_HEADS):
                co_ref[d, h] = Cs[d, h]
        no_ref[...] = ns[...]
        mo_ref[...] = ms[...]


def _mlstm(proj, gcol, gt3, fbias_l, l, latent, C0=None, n0=None, m0=None, a_out=None, prev=None):
    T = DEC_SEQ if latent else SEQ
    B = DEC_BATCH if latent else BATCH
    rb0 = N_CTX // DEC_SEQ if latent else 0
    fb = fbias_l.astype(F32)
    fbc = jnp.zeros((1, 128), F32).at[0, MA_HEADS:2 * MA_HEADS].set(fb[0]).at[0, 3 * MA_HEADS:4 * MA_HEADS].set(fb[1])
    fbr = fbc[0, :N_GATES].reshape(N_GATES, 1)
    full2 = lambda b: (0, 0)
    any_spec = pl.BlockSpec(memory_space=pl.ANY)
    in_specs = [pl.BlockSpec((T, 4 * BRANCH_W), lambda b: (rb0 + b, 0)),
                pl.BlockSpec((T, 128), lambda b: (rb0 + b, 0)),
                pl.BlockSpec((T // MA_CHUNK, N_GATES, MA_CHUNK), lambda b: (rb0 + b, 0, 0)),
                pl.BlockSpec((1, 128), full2), pl.BlockSpec((N_GATES, 1), full2)]
    args = [proj, gcol, gt3, fbc, fbr]
    a_shape = jax.ShapeDtypeStruct((N_TOK, BRANCH_W), BF16)
    a_spec = pl.BlockSpec((T, BRANCH_W), lambda b: (rb0 + b, 0))
    c_spec = pl.BlockSpec((None, None, 2, MA_HEADS, MA_DK, MA_DV), lambda b: (b, l, 0, 0, 0, 0))
    nm_spec = pl.BlockSpec((None, None, 2 * MA_HEADS, 128), lambda b: (b, l, 0, 0))
    aliases = {}
    if latent:
        cos, sin = _rope_tables(T)
        nb = 2 * MA_HEADS
        in_specs += [pl.BlockSpec((T, MA_DK), full2), pl.BlockSpec((T, MA_DK), full2), c_spec, nm_spec, nm_spec, any_spec]
        args += [cos, sin, C0, n0.reshape(B, DEPTH, nb, MA_DK),
                 jnp.broadcast_to(m0.reshape(B, DEPTH, nb, 1), (B, DEPTH, nb, 128)), a_out]
        aliases = {len(args) - 1: 0}
        out_shape, out_specs = a_shape, a_spec
    else:
        nm_shape = jax.ShapeDtypeStruct((B, DEPTH, 2 * MA_HEADS, 128), F32)
        out_shape = (a_shape, jax.ShapeDtypeStruct((B, DEPTH, 2, MA_HEADS, MA_DK, MA_DV), F32), nm_shape, nm_shape)
        out_specs = (a_spec, c_spec, nm_spec, nm_spec)
        if prev is not None:
            in_specs += [any_spec] * 3
            args += list(prev)
            aliases = {len(args) - 3: 1, len(args) - 2: 2, len(args) - 1: 3}
    scratch = [pltpu.VMEM((T, BRANCH_W), BF16)] * 3 + [pltpu.VMEM((T, BRANCH_W), F32)] * 2 + [
        pltpu.VMEM((2, MA_HEADS, MA_DK, MA_DV), F32), pltpu.VMEM((2 * MA_HEADS, MA_DK), F32),
        pltpu.VMEM((2 * MA_HEADS, 128), F32)]
    return pl.pallas_call(
        functools.partial(_mlstm_kernel, T=T, latent=latent),
        out_shape=out_shape, grid=(B,), in_specs=in_specs, out_specs=out_specs, scratch_shapes=scratch,
        input_output_aliases=aliases,
        compiler_params=_cparams("arbitrary"),
        name="mlstm_lat" if latent else "mlstm_ctx",
    )(*args)


HG_SUB = 8


def _hgrn_kernel(*refs, T, latent):
    ff_ref, fb_ref, q_ref, i_ref, g_ref, lbf_ref, lbb_ref = refs[:7]
    if latent:
        s0_ref = refs[7]
        c_ref, of, ob, ST = refs[-4:]
    else:
        c_ref, so_ref, of, ob, ST = refs[-5:]
    L = HG_CHUNK
    NC = T // L
    NB = L // HG_SUB
    DK = HG_DK

    for d in range(2):
        for h in range(HG_HEADS):
            ST[d, h] = s0_ref[d, h].T if latent else jnp.zeros((HG_DV, DK), F32)

    low = _tri(L, False)
    upp = _tri(L, True)
    row8 = lax.broadcasted_iota(jnp.int32, (HG_SUB, L), 0)
    lane_s = lax.broadcasted_iota(jnp.int32, (HG_SUB, L), 1)

    def chunk(c, d):
        t0 = pl.multiple_of(c * L, L)
        fpre = (ff_ref if d == 0 else fb_ref)[pl.ds(t0, L), :]
        lb = (lbf_ref if d == 0 else lbb_ref)[...]
        f = lb + (1.0 - lb) * jax.nn.sigmoid(fpre)
        logf = jnp.log(f)
        kk = 1.0 - f
        qq = _silu(q_ref[pl.ds(t0, L), :])
        iv = i_ref[pl.ds(t0, L), :].astype(BF16)
        A_all = jnp.dot(low if d == 0 else upp, logf, precision=HI, preferred_element_type=F32)
        last = L - 1 if d == 0 else 0
        for h in range(HG_HEADS):
            hs = slice(h * DK, (h + 1) * DK)
            A, k, q, ivh = A_all[:, hs], kk[:, hs], qq[:, hs], iv[:, hs]
            st = ST[d, h]
            o = lax.dot_general((q * jnp.exp(A)).astype(BF16), st.astype(BF16), _NT, preferred_element_type=F32)
            a_last = A[last:last + 1, :]
            kd = k * jnp.exp(a_last - A)
            rows = []
            for I in range(NB):
                bs = slice(I * HG_SUB, (I + 1) * HG_SUB)
                A_I, q_I = A[bs], q[bs]
                if d == 0:
                    has_off, ref_row, off_mask = I > 0, I * HG_SUB - 1, lane_s < I * HG_SUB
                else:
                    has_off, ref_row, off_mask = I < NB - 1, (I + 1) * HG_SUB, lane_s >= (I + 1) * HG_SUB
                att = jnp.zeros((HG_SUB, L), F32)
                if has_off:
                    R = A[ref_row:ref_row + 1, :]
                    qsc = (q_I * jnp.exp(A_I - R)).astype(BF16)
                    ksc = (k * jnp.exp(jnp.minimum(R - A, 0.0))).astype(BF16)
                    att = jnp.where(off_mask, lax.dot_general(qsc, ksc, _NT, preferred_element_type=F32), 0.0)
                for j in range(HG_SUB):
                    s = I * HG_SUB + j
                    e = jnp.exp(jnp.minimum(A_I - A[s:s + 1, :], 0.0))
                    col = jnp.sum(q_I * k[s:s + 1, :] * e, -1, keepdims=True)
                    keep = (lane_s == s) & ((row8 >= j) if d == 0 else (row8 <= j))
                    att = jnp.where(keep, col, att)
                rows.append(att)
            att = jnp.concatenate(rows, 0)
            o = o + jnp.dot(att.astype(BF16), ivh, preferred_element_type=F32)
            (of if d == 0 else ob)[pl.ds(t0, L), hs] = o
            ST[d, h] = st * jnp.exp(a_last) + lax.dot_general(ivh, kd.astype(BF16), _TN, preferred_element_type=F32)

    def body(i, carry):
        chunk(i, 0)
        chunk(NC - 1 - i, 1)
        return carry

    lax.fori_loop(0, NC, body, 0)

    def epilogue(r, carry):
        t0 = pl.multiple_of(r * 128, 128)
        o = of[pl.ds(t0, 128), :] + ob[pl.ds(t0, 128), :]
        gsil = _silu(g_ref[pl.ds(t0, 128), :])
        outs = []
        for h in range(HG_HEADS):
            oh = o[:, h * HG_DV:(h + 1) * HG_DV]
            outs.append(oh * lax.rsqrt(jnp.mean(oh * oh, -1, keepdims=True) + RMS_EPS))
        c_ref[pl.ds(t0, 128), :] = (jnp.concatenate(outs, -1) * gsil).astype(BF16)
        return carry

    lax.fori_loop(0, T // 128, epilogue, 0)
    if not latent:
        for d in range(2):
            for h in range(HG_HEADS):
                so_ref[d, h] = ST[d, h].T


def _hgrn(proj, lb_l, l, latent, S0=None, c_out=None, prev=None):
    T = DEC_SEQ if latent else SEQ
    B = DEC_BATCH if latent else BATCH
    rb0 = N_CTX // DEC_SEQ if latent else 0
    W = BRANCH_W
    full2 = lambda b: (0, 0)
    any_spec = pl.BlockSpec(memory_space=pl.ANY)
    col = lambda j: pl.BlockSpec((T, W), lambda b: (rb0 + b, j))
    s_spec = pl.BlockSpec((None, None, 2, HG_HEADS, HG_DK, HG_DV), lambda b: (b, l, 0, 0, 0, 0))
    in_specs = [col(7), col(8), col(9), col(10), col(11), pl.BlockSpec((1, W), full2), pl.BlockSpec((1, W), full2)]
    args = [proj] * 5 + [lb_l[0][None, :], lb_l[1][None, :]]
    c_shape = jax.ShapeDtypeStruct((N_TOK, W), BF16)
    c_spec = pl.BlockSpec((T, W), lambda b: (rb0 + b, 0))
    aliases = {}
    if latent:
        in_specs += [s_spec, any_spec]
        args += [S0, c_out]
        aliases = {8: 0}
        out_shape, out_specs = c_shape, c_spec
    else:
        out_shape = (c_shape, jax.ShapeDtypeStruct((B, DEPTH, 2, HG_HEADS, HG_DK, HG_DV), F32))
        out_specs = (c_spec, s_spec)
        if prev is not None:
            in_specs.append(any_spec)
            args.append(prev)
            aliases = {7: 1}
    scratch = [pltpu.VMEM((T, W), F32), pltpu.VMEM((T, W), F32), pltpu.VMEM((2, HG_HEADS, HG_DV, HG_DK), F32)]
    return pl.pallas_call(
        functools.partial(_hgrn_kernel, T=T, latent=latent),
        out_shape=out_shape, grid=(B,), in_specs=in_specs, out_specs=out_specs, scratch_shapes=scratch,
        input_output_aliases=aliases,
        compiler_params=_cparams("arbitrary"),
        name="hgrn_lat" if latent else "hgrn_ctx",
    )(*args)


def _merge_kernel(a_ref, b_ref, c_ref, ga_ref, gb_ref, gc_ref, xc_ref, xl_ref, g1_ref, sh2_ref, sc2_ref,
                  wb_ref, wo_ref, lng_ref, lnb_ref, wr_ref, x1_ref, h2_ref, gate_ref, gid_ref, *, tm):
    def br(v_ref, g_ref, k):
        return jax.nn.sigmoid(g_ref[...]) * jnp.dot(v_ref[...], wb_ref[k], preferred_element_type=F32)

    mix = br(a_ref, ga_ref, 0) + br(b_ref, gb_ref, 1) + br(c_ref, gc_ref, 2)
    y = jnp.dot(mix.astype(BF16), wo_ref[...], preferred_element_type=F32)
    x = _pair_read(pl.program_id(0), tm, xc_ref, xl_ref)
    x1 = _layer_norm(DEEPNORM_ALPHA * x + g1_ref[...] * y, lng_ref[...], lnb_ref[...])
    x1_ref[...] = x1
    h2 = x1 * (1.0 + sc2_ref[...]) + sh2_ref[...]
    h2_ref[...] = h2.astype(BF16)
    logits = jnp.dot(h2, wr_ref[...], preferred_element_type=F32, precision=HI)
    lane = lax.broadcasted_iota(jnp.int32, logits.shape, 1)
    neg = -jnp.inf
    lg = jnp.where(lane < N_GROUPS, logits, neg)
    mg = jnp.max(lg, -1, keepdims=True)
    g_sel = jnp.min(jnp.where(lg == mg, lane, 128), -1, keepdims=True)
    p_sel = 1.0 / jnp.sum(jnp.where(lane < N_GROUPS, jnp.exp(lg - mg), 0.0), -1, keepdims=True)
    lo = N_GROUPS + EXP_PER_GROUP * g_sel
    le = jnp.where((lane >= lo) & (lane < lo + EXP_PER_GROUP), logits, neg)
    v1 = jnp.max(le, -1, keepdims=True)
    i1 = jnp.min(jnp.where(le == v1, lane, 128), -1, keepdims=True)
    le2 = jnp.where(lane == i1, neg, le)
    v2 = jnp.max(le2, -1, keepdims=True)
    i2 = jnp.min(jnp.where(le2 == v2, lane, 128), -1, keepdims=True)
    e2 = jnp.exp(v2 - v1)
    w1 = p_sel / (1.0 + e2)
    w2 = p_sel * e2 / (1.0 + e2)
    gate_ref[...] = jnp.where(lane == i1 - lo, w1, jnp.where(lane == i2 - lo, w2, 0.0))
    gid_ref[...] = jnp.broadcast_to(g_sel, logits.shape)


def _merge(a, b, c, proj, xc, xl, mod, wb, wo, lng, lnb, wr, l):
    tm = 512
    tok = lambda i: (i, 0)
    ln_spec = pl.BlockSpec((None, None, 1, D_MODEL), lambda i: (l, 0, 0, 0))
    return pl.pallas_call(
        functools.partial(_merge_kernel, tm=tm),
        out_shape=(jax.ShapeDtypeStruct((N_TOK, D_MODEL), F32), jax.ShapeDtypeStruct((N_TOK, D_MODEL), BF16),
                   jax.ShapeDtypeStruct((N_TOK, 128), F32), jax.ShapeDtypeStruct((N_TOK, 128), jnp.int32)),
        grid=(N_TOK // tm,),
        in_specs=[pl.BlockSpec((tm, BRANCH_W), tok), pl.BlockSpec((tm, BRANCH_W), tok), pl.BlockSpec((tm, BRANCH_W), tok),
                  pl.BlockSpec((tm, D_MODEL), lambda i: (i, 6)), pl.BlockSpec((tm, D_MODEL), lambda i: (i, 7)),
                  pl.BlockSpec((tm, D_MODEL), lambda i: (i, 8)),
                  *_pair_specs(tm), _mod_spec(l, 2, tm), _mod_spec(l, 3, tm), _mod_spec(l, 4, tm),
                  pl.BlockSpec((None, 3, BRANCH_W, D_MODEL), lambda i: (l, 0, 0, 0)),
                  pl.BlockSpec((None, D_MODEL, D_MODEL), lambda i: (l, 0, 0)),
                  ln_spec, ln_spec,
                  pl.BlockSpec((None, D_MODEL, 128), lambda i: (l, 0, 0))],
        out_specs=(pl.BlockSpec((tm, D_MODEL), tok), pl.BlockSpec((tm, D_MODEL), tok),
                   pl.BlockSpec((tm, 128), tok), pl.BlockSpec((tm, 128), tok)),
        compiler_params=_cparams("arbitrary"),
        name="merge",
    )(a, b, c, proj, proj, proj, xc, xl, mod, mod, mod, wb, wo, lng, lnb, wr)


def _moe_up_kernel(gid_ref, x_ref, gate_ref, w1_ref, w3_ref, hid_ref, w1b, w3b):
    f = pl.program_id(0)
    t = pl.program_id(1)
    first = jnp.logical_or(t == 0, gid_ref[t] != gid_ref[jnp.maximum(t - 1, 0)])

    @pl.when(first)
    def _():
        w1b[...] = w1_ref[...].astype(BF16)
        w3b[...] = w3_ref[...].astype(BF16)

    x = x_ref[...]
    a = jnp.dot(x, w1b[...], preferred_element_type=F32)
    b = jnp.dot(x, w3b[...], preferred_element_type=F32)
    gate = gate_ref[...]
    lane = lax.broadcasted_iota(jnp.int32, gate.shape, 1)
    gcol = jnp.sum(jnp.where(lane == f, gate, 0.0), -1, keepdims=True)
    hid_ref[...] = (_silu(a) * b * gcol).astype(BF16)


def _moe_up(gid, xs, gates, w1, w3, l):
    tm = MOE_TM
    npad = MOE_NT * tm
    w_spec = pl.BlockSpec((None, None, D_MODEL, D_EXPERT), lambda f, t, g: (l, EXP_PER_GROUP * g[t] + f, 0, 0))
    grid_spec = pltpu.PrefetchScalarGridSpec(
        num_scalar_prefetch=1,
        grid=(EXP_PER_GROUP, MOE_NT),
        in_specs=[pl.BlockSpec((tm, D_MODEL), lambda f, t, g: (t, 0)),
                  pl.BlockSpec((tm, 128), lambda f, t, g: (t, 0)), w_spec, w_spec],
        out_specs=pl.BlockSpec((tm, D_EXPERT), lambda f, t, g: (t, f)),
        scratch_shapes=[pltpu.VMEM((D_MODEL, D_EXPERT), BF16), pltpu.VMEM((D_MODEL, D_EXPERT), BF16)],
    )
    return pl.pallas_call(
        _moe_up_kernel,
        out_shape=jax.ShapeDtypeStruct((npad, EXP_PER_GROUP * D_EXPERT), BF16),
        grid_spec=grid_spec,
        compiler_params=_cparams("arbitrary", "arbitrary"),
        name="moe_up",
    )(gid, xs, gates, w1, w3)


def _moe_down_kernel(gid_ref, hid_ref, w2_ref, y_ref, w2b):
    t = pl.program_id(0)
    first = jnp.logical_or(t == 0, gid_ref[t] != gid_ref[jnp.maximum(t - 1, 0)])

    @pl.when(first)
    def _():
        w2b[...] = w2_ref[...].astype(BF16)

    y_ref[...] = jnp.dot(hid_ref[...], w2b[...], preferred_element_type=F32)


def _moe_down(gid, hid, w2g, l):
    tm = MOE_TM
    npad = MOE_NT * tm
    hw = EXP_PER_GROUP * D_EXPERT
    grid_spec = pltpu.PrefetchScalarGridSpec(
        num_scalar_prefetch=1,
        grid=(MOE_NT,),
        in_specs=[pl.BlockSpec((tm, hw), lambda t, g: (t, 0)),
                  pl.BlockSpec((None, None, hw, D_MODEL), lambda t, g: (l, g[t], 0, 0))],
        out_specs=pl.BlockSpec((tm, D_MODEL), lambda t, g: (t, 0)),
        scratch_shapes=[pltpu.VMEM((hw, D_MODEL), BF16)],
    )
    return pl.pallas_call(
        _moe_down_kernel,
        out_shape=jax.ShapeDtypeStruct((npad, D_MODEL), F32),
        grid_spec=grid_spec,
        compiler_params=_cparams("arbitrary"),
        name="moe_down",
    )(gid, hid, w2g)


def _moe(h2, gate, gsel, w1, w3, w2g, l):
    tm = MOE_TM
    npad = MOE_NT * tm
    g = gsel[:, 0]
    onehot = (g[:, None] == jnp.arange(N_GROUPS)[None, :]).astype(jnp.int32)
    counts = jnp.sum(onehot, 0)
    rank = jnp.sum((jnp.cumsum(onehot, 0) - onehot) * onehot, 1)
    padded = (counts + tm - 1) // tm * tm
    ends = jnp.cumsum(padded)
    offs = ends - padded
    dest = offs[g] + rank
    src = jnp.full((npad,), -1, jnp.int32).at[dest].set(jnp.arange(N_TOK, dtype=jnp.int32), unique_indices=True)
    valid = (src >= 0).astype(F32)
    src = jnp.maximum(src, 0)
    starts = jnp.arange(MOE_NT, dtype=jnp.int32) * tm
    tile_gid = jnp.minimum(jnp.sum((ends[None, :] <= starts[:, None]).astype(jnp.int32), 1), N_GROUPS - 1)
    take = lambda arr, idx: arr.at[idx].get(mode="promise_in_bounds", unique_indices=False)
    xs = take(h2, src)
    gs = take(gate, src) * valid[:, None]
    hid = _moe_up(tile_gid, xs, gs, w1, w3, l)
    ys = _moe_down(tile_gid, hid, w2g, l)
    return take(ys, dest)


def _final_kernel(*refs, tm, with_h):
    x1_ref, y_ref, g2_ref, lng_ref, lnb_ref = refs[:5]
    x2 = _layer_norm(DEEPNORM_ALPHA * x1_ref[...] + g2_ref[...] * y_ref[...], lng_ref[...], lnb_ref[...])
    i = pl.program_id(0)
    if with_h:
        sh_ref, sc_ref, xc_ref, xl_ref, h_ref = refs[5:]
        h_ref[...] = (x2 * (1.0 + sc_ref[...]) + sh_ref[...]).astype(BF16)
    else:
        xc_ref, xl_ref = refs[5:]

    @pl.when(i < N_CTX // tm)
    def _():
        xc_ref[...] = x2

    @pl.when(i >= N_CTX // tm)
    def _():
        xl_ref[...] = x2


def _final(x1, y, mod, lng, lnb, l):
    tm = 1024
    tok = lambda i: (i, 0)
    with_h = l + 1 < DEPTH
    ln_spec = pl.BlockSpec((None, None, 1, D_MODEL), lambda i: (l, 1, 0, 0))
    half = jax.ShapeDtypeStruct((N_CTX, D_MODEL), F32)
    in_specs = [pl.BlockSpec((tm, D_MODEL), tok), pl.BlockSpec((tm, D_MODEL), tok), _mod_spec(l, 5, tm), ln_spec, ln_spec]
    args = [x1, y, mod, lng, lnb]
    out_shape = [half, half]
    out_specs = list(_pair_specs(tm))
    if with_h:
        in_specs += [_mod_spec(l + 1, 0, tm), _mod_spec(l + 1, 1, tm)]
        args += [mod, mod]
        out_shape.append(jax.ShapeDtypeStruct((N_TOK, D_MODEL), BF16))
        out_specs.append(pl.BlockSpec((tm, D_MODEL), tok))
    return pl.pallas_call(
        functools.partial(_final_kernel, tm=tm, with_h=with_h),
        out_shape=tuple(out_shape), grid=(N_TOK // tm,), in_specs=in_specs, out_specs=tuple(out_specs),
        compiler_params=_cparams("arbitrary"),
        name="final",
    )(*args)


def kernel(x_prompt, x_sample, c, cache_na_k, cache_na_v, state_mlstm_C, state_mlstm_n, state_mlstm_m, state_hgrn,
           c_ctx, w_mod, b_mod, w_in, b_in, mlstm_fbias, hgrn_lb_logits, na_rpb, w_branch, w_out, ln_g, ln_b,
           w_rg, w_re, w_e1, w_e3, w_e2):
    assert N_CTX == N_LAT
    lb_cum = jnp.cumsum(jax.nn.softmax(hgrn_lb_logits.astype(F32), axis=1), axis=1)
    lb_all = lb_cum - lb_cum[:, :1]

    cs = jnp.zeros((N_MODROWS, D_MODEL), F32).at[0].set(c_ctx).at[1:1 + DEC_BATCH].set(c)
    mod = _modulation(cs, w_mod, b_mod).reshape(DEPTH, N_MODROWS, 6, 1, D_MODEL)

    wb = w_branch.astype(BF16)
    wo = w_out.astype(BF16)
    lng = ln_g.reshape(DEPTH, 2, 1, D_MODEL)
    lnb = ln_b.reshape(DEPTH, 2, 1, D_MODEL)
    wr = jnp.concatenate([w_rg, w_re, jnp.zeros((DEPTH, D_MODEL, 128 - N_GROUPS - N_EXPERTS), F32)], -1)
    w2g = w_e2.reshape(DEPTH, N_GROUPS, EXP_PER_GROUP * D_EXPERT, D_MODEL)
    b_main = jnp.concatenate([b_in[:, :GATE_COL0], b_in[:, GATE_COL0 + N_GATES:]], 1)

    xc = x_prompt.reshape(N_CTX, D_MODEL)
    xl = x_sample.reshape(N_LAT, D_MODEL)
    h = _prep(xc, xl, mod)

    kv = (None, None)
    ma_states = None
    hg_state = None
    for l in range(DEPTH):
        proj = _inproj(h, w_in, b_main[l][None, :], l)
        gcol, gt = _gates(h, w_in[l, :, GATE_COL0:GATE_COL0 + N_GATES], b_in[l, GATE_COL0:GATE_COL0 + N_GATES])
        gt3 = gt.reshape(N_GATES, N_TOK // MA_CHUNK, MA_CHUNK).transpose(1, 0, 2)

        a, *ma_states = _mlstm(proj, gcol, gt3, mlstm_fbias[l], l, False, prev=ma_states)
        a = _mlstm(proj, gcol, gt3, mlstm_fbias[l], l, True, state_mlstm_C, state_mlstm_n, state_mlstm_m, a_out=a)
        b, *kv = _ctx_attention(proj, l, *kv)
        b = _lat_attention(proj, cache_na_k, cache_na_v, _na_bias_table(na_rpb[l]), l, b)
        cc, hg_state = _hgrn(proj, lb_all[:, l], l, False, prev=hg_state)
        cc = _hgrn(proj, lb_all[:, l], l, True, state_hgrn, c_out=cc)

        x1, h2, gate, gsel = _merge(a, b, cc, proj, xc, xl, mod, wb, wo, lng, lnb, wr, l)
        y2 = _moe(h2, gate, gsel, w_e1, w_e3, w2g, l)
        outs = _final(x1, y2, mod, lng, lnb, l)
        xc, xl = outs[0], outs[1]
        if l + 1 < DEPTH:
            h = outs[2]

    dt = x_prompt.dtype
    new_C, new_n, new_m = ma_states
    new_n = new_n.reshape(BATCH, DEPTH, 2, MA_HEADS, MA_DK)
    new_m = new_m[:, :, :, 0].reshape(BATCH, DEPTH, 2, MA_HEADS)
    return (xc.reshape(BATCH, SEQ, D_MODEL), xl.reshape(DEC_BATCH, DEC_SEQ, D_MODEL), kv[0], kv[1],
            new_C.astype(dt), new_n.astype(dt), new_m.astype(dt), hg_state.astype(dt))
```

```python
import functools

import numpy as np
import jax
import jax.numpy as jnp
from jax import lax
from jax.experimental import pallas as pl
from jax.experimental.pallas import tpu as pltpu

F32 = jnp.float32
BF16 = jnp.bfloat16
HI = lax.Precision.HIGHEST

D_MODEL = 1024
BATCH = 16
SEQ = 256
DEPTH = 2
DEC_BATCH = 4
DEC_SEQ = 1024
PAST_LEN = 256
GRID_W = 64
MA_HEADS = 4
MA_DK = 128
MA_DV = 128
MA_CHUNK = 64
NA_HEADS = 8
NA_DH = 64
NA_KR_MAX = 8
NA_KC = 16
HG_HEADS = 4
HG_DK = 128
HG_DV = 128
HG_CHUNK = 32
BRANCH_W = 512
N_GROUPS = 4
EXP_PER_GROUP = 4
N_EXPERTS = N_GROUPS * EXP_PER_GROUP
D_EXPERT = 512
ROPE_BASE = 10000.0
LN_EPS = 1e-5
RMS_EPS = 1e-6
DEEPNORM_ALPHA = (2 * DEPTH) ** 0.25

N_CTX = BATCH * SEQ
N_LAT = DEC_BATCH * DEC_SEQ
N_TOK = N_CTX + N_LAT
N_MODROWS = 8
GATE_COL0 = 4 * BRANCH_W
N_GATES = 4 * MA_HEADS
N_IN = 9232
P_COLS = N_IN - N_GATES
MOE_TM = 256
MOE_NT = N_TOK // MOE_TM + N_GROUPS
VMEM_LIMIT = 48 * 1024 * 1024

_NT = (((1,), (1,)), ((), ()))
_TN = (((0,), (0,)), ((), ()))


def _cparams(*sem):
    return pltpu.CompilerParams(dimension_semantics=sem, vmem_limit_bytes=VMEM_LIMIT)


def _mod_row(tile, tm):
    return jnp.maximum((tile * tm) // DEC_SEQ - (N_CTX // DEC_SEQ - 1), 0)


def _mod_spec(l, part, tm):
    return pl.BlockSpec((None, None, None, 1, D_MODEL), lambda i: (l, _mod_row(i, tm), part, 0, 0))


def _pair_specs(tm):
    nc = N_CTX // tm
    return (pl.BlockSpec((tm, D_MODEL), lambda i: (jnp.minimum(i, nc - 1), 0)),
            pl.BlockSpec((tm, D_MODEL), lambda i: (jnp.maximum(i - nc, 0), 0)))


def _pair_read(i, tm, c_ref, l_ref):
    return jnp.where(i < N_CTX // tm, c_ref[...], l_ref[...])


def _silu(x):
    return x * jax.nn.sigmoid(x)


def _layer_norm(x, g, b):
    mu = jnp.mean(x, -1, keepdims=True)
    xc = x - mu
    var = jnp.mean(xc * xc, -1, keepdims=True)
    return xc * lax.rsqrt(var + LN_EPS) * g + b


def _log_sigmoid(x):
    return jnp.minimum(x, 0.0) - jnp.log(1.0 + jnp.exp(-jnp.abs(x)))


def _tri(n, upper):
    r = lax.broadcasted_iota(jnp.int32, (n, n), 0)
    c = lax.broadcasted_iota(jnp.int32, (n, n), 1)
    return jnp.where((r <= c) if upper else (r >= c), 1.0, 0.0).astype(F32)


def _mod_kernel(c_ref, w_ref, b_ref, o_ref):
    s = _silu(c_ref[...])
    o_ref[...] = jnp.dot(s.astype(BF16), w_ref[...].astype(BF16), preferred_element_type=F32) + b_ref[...]


def _modulation(cs, w_mod, b_mod):
    tn = 1024
    return pl.pallas_call(
        _mod_kernel,
        out_shape=jax.ShapeDtypeStruct((DEPTH, N_MODROWS, 6 * D_MODEL), F32),
        grid=(DEPTH, 6 * D_MODEL // tn),
        in_specs=[pl.BlockSpec((N_MODROWS, D_MODEL), lambda l, j: (0, 0)),
                  pl.BlockSpec((None, D_MODEL, tn), lambda l, j: (l, 0, j)),
                  pl.BlockSpec((None, 1, tn), lambda l, j: (l, 0, j))],
        out_specs=pl.BlockSpec((None, N_MODROWS, tn), lambda l, j: (l, 0, j)),
        compiler_params=_cparams("arbitrary", "arbitrary"),
        name="modulation",
    )(cs, w_mod, b_mod.reshape(DEPTH, 1, 6 * D_MODEL))


def _prep_kernel(xc_ref, xl_ref, sh_ref, sc_ref, h_ref, *, tm):
    x = _pair_read(pl.program_id(0), tm, xc_ref, xl_ref)
    h_ref[...] = (x * (1.0 + sc_ref[...]) + sh_ref[...]).astype(BF16)


def _prep(xc, xl, mod):
    tm = 1024
    return pl.pallas_call(
        functools.partial(_prep_kernel, tm=tm),
        out_shape=jax.ShapeDtypeStruct((N_TOK, D_MODEL), BF16),
        grid=(N_TOK // tm,),
        in_specs=[*_pair_specs(tm), _mod_spec(0, 0, tm), _mod_spec(0, 1, tm)],
        out_specs=pl.BlockSpec((tm, D_MODEL), lambda i: (i, 0)),
        compiler_params=_cparams("arbitrary"),
        name="prep",
    )(xc, xl, mod, mod)


INPROJ_TN = 512
N_PLAIN_TILES = GATE_COL0 // INPROJ_TN


def _inproj_kernel(h_ref, wa_ref, wb_ref, b_ref, o_ref):
    j = pl.program_id(1)

    @pl.when(j < N_PLAIN_TILES)
    def _():
        o_ref[...] = jnp.dot(h_ref[...], wa_ref[...].astype(BF16), preferred_element_type=F32) + b_ref[...]

    @pl.when(j >= N_PLAIN_TILES)
    def _():
        w = jnp.concatenate([wa_ref[...], wb_ref[...]], 1)[:, N_GATES:N_GATES + INPROJ_TN]
        o_ref[...] = jnp.dot(h_ref[...], w.astype(BF16), preferred_element_type=F32) + b_ref[...]


def _inproj(h, w_in, b_main, l):
    tm, tn = 2048, INPROJ_TN
    return pl.pallas_call(
        _inproj_kernel,
        out_shape=jax.ShapeDtypeStruct((N_TOK, P_COLS), F32),
        grid=(N_TOK // tm, P_COLS // tn),
        in_specs=[pl.BlockSpec((tm, D_MODEL), lambda i, j: (i, 0)),
                  pl.BlockSpec((None, D_MODEL, tn), lambda i, j: (l, 0, j)),
                  pl.BlockSpec((None, D_MODEL, 128), lambda i, j: (l, 0, (j + 1) * (tn // 128))),
                  pl.BlockSpec((1, tn), lambda i, j: (0, j))],
        out_specs=pl.BlockSpec((tm, tn), lambda i, j: (i, j)),
        compiler_params=_cparams("arbitrary", "arbitrary"),
        name="inproj",
    )(h, w_in, w_in, b_main)


def _gates_kernel(h_ref, w_ref, b_ref, gc_ref, gt_ref):
    g = jnp.dot(h_ref[...], w_ref[...].astype(BF16), preferred_element_type=F32) + b_ref[...]
    gc_ref[...] = g
    gt_ref[...] = g.T[:N_GATES]


def _gates(h, w_in, b_in3, l):
    tm = 1024
    gblk = GATE_COL0 // 128
    return pl.pallas_call(
        _gates_kernel,
        out_shape=(jax.ShapeDtypeStruct((N_TOK, 128), F32), jax.ShapeDtypeStruct((N_GATES, N_TOK), F32)),
        grid=(N_TOK // tm,),
        in_specs=[pl.BlockSpec((tm, D_MODEL), lambda i: (i, 0)),
                  pl.BlockSpec((None, D_MODEL, 128), lambda i: (l, 0, gblk)),
                  pl.BlockSpec((None, 1, 128), lambda i: (l, 0, gblk))],
        out_specs=(pl.BlockSpec((tm, 128), lambda i: (i, 0)), pl.BlockSpec((N_GATES, tm), lambda i: (0, i))),
        compiler_params=_cparams("arbitrary"),
        name="gates",
    )(h, w_in, b_in3)


HEADS_PER_BLK = 128 // NA_DH
NA_NBLK = NA_HEADS // HEADS_PER_BLK
NA_QSCALE = NA_DH ** -0.5
Q_COL, K_COL, V_COL = 16, 20, 24


def _ctx_attn_kernel(*refs, first):
    q_ref, k_ref, v_ref = refs[:3]
    o_ref, ko_ref, vo_ref = refs[-3:]
    q = q_ref[...] * NA_QSCALE
    k = k_ref[...]
    v = v_ref[...]
    outs = []
    for hh in range(HEADS_PER_BLK):
        sl = slice(hh * NA_DH, (hh + 1) * NA_DH)
        kh, vh = k[:, sl], v[:, sl]
        ko_ref[hh] = kh
        vo_ref[hh] = vh
        s = lax.dot_general(q[:, sl].astype(BF16), kh.astype(BF16), _NT, preferred_element_type=F32)
        e = jnp.exp(s - jnp.max(s, -1, keepdims=True))
        p = e * (1.0 / jnp.sum(e, -1, keepdims=True))
        outs.append(jnp.dot(p.astype(BF16), vh.astype(BF16), preferred_element_type=F32))
    o_ref[...] = jnp.concatenate(outs, -1).astype(BF16)


def _ctx_attention(proj, l, prev_k=None, prev_v=None):
    cb = lambda base: (lambda b, j: (b, base + j))
    kv_shape = jax.ShapeDtypeStruct((BATCH, DEPTH, NA_HEADS, SEQ, NA_DH), F32)
    kv_spec = pl.BlockSpec((None, None, HEADS_PER_BLK, SEQ, NA_DH), lambda b, j: (b, l, j, 0, 0))
    in_specs = [pl.BlockSpec((SEQ, 128), cb(Q_COL)), pl.BlockSpec((SEQ, 128), cb(K_COL)), pl.BlockSpec((SEQ, 128), cb(V_COL))]
    args = [proj, proj, proj]
    aliases = {}
    if prev_k is not None:
        in_specs += [pl.BlockSpec(memory_space=pl.ANY)] * 2
        args += [prev_k, prev_v]
        aliases = {3: 1, 4: 2}
    return pl.pallas_call(
        functools.partial(_ctx_attn_kernel, first=prev_k is None),
        out_shape=(jax.ShapeDtypeStruct((N_TOK, BRANCH_W), BF16), kv_shape, kv_shape),
        grid=(BATCH, NA_NBLK),
        in_specs=in_specs,
        out_specs=(pl.BlockSpec((SEQ, 128), lambda b, j: (b, j)), kv_spec, kv_spec),
        input_output_aliases=aliases,
        compiler_params=_cparams("arbitrary", "arbitrary"),
        name="ctx_attention",
    )(*args)


NA_ROWS = DEC_SEQ // GRID_W
NA_KR = min(NA_KR_MAX, NA_ROWS)
NA_QROWS = 4
NA_QT = NA_ROWS // NA_QROWS
NA_WROWS = NA_KR + NA_QROWS - 1
NA_WKEYS = NA_WROWS * GRID_W


def _na_window_start(t):
    return min(max(t * NA_QROWS - NA_KR // 2, 0), NA_ROWS - NA_WROWS)


def _na_bias_table(rpb):
    c = np.arange(GRID_W)
    c0 = np.clip(c - NA_KC // 2, 0, GRID_W - NA_KC)
    kc = np.arange(GRID_W)
    valid = (kc[None, :] >= c0[:, None]) & (kc[None, :] < c0[:, None] + NA_KC)
    dc = kc[None, :] - c[:, None] + NA_KC - 1
    onehot = (dc[None] == np.arange(2 * NA_KC - 1)[:, None, None]) & valid[None]
    toep = jnp.einsum('hrd,dcx->hrcx', rpb.astype(F32), jnp.asarray(onehot, F32), precision=HI)
    toep = jnp.where(valid[None, None], toep, -jnp.inf)
    ninf = jnp.full((NA_HEADS, GRID_W, GRID_W), -jnp.inf, F32)
    tiles = []
    for t in range(NA_QT):
        w0 = _na_window_start(t)
        qrows = []
        for r in range(t * NA_QROWS, (t + 1) * NA_QROWS):
            r0 = min(max(r - NA_KR // 2, 0), NA_ROWS - NA_KR)
            blocks = []
            for kr in range(w0, w0 + NA_WROWS):
                inside = r0 <= kr < r0 + NA_KR
                blocks.append(toep[:, kr - r + NA_KR_MAX - 1] if inside else ninf)
            qrows.append(jnp.concatenate(blocks, -1))
        tiles.append(jnp.concatenate(qrows, 1))
    return jnp.stack(tiles, 1)


def _lat_attn_kernel(q_ref, k_ref, v_ref, ck_ref, cv_ref, bias_ref, prev_ref, o_ref):
    q = (q_ref[...] * NA_QSCALE).astype(BF16)
    k = k_ref[...].astype(BF16)
    v = v_ref[...].astype(BF16)
    nq = NA_QROWS * GRID_W
    for t in range(NA_QT):
        w0 = _na_window_start(t)
        qs = slice(t * nq, (t + 1) * nq)
        ws = slice(w0 * GRID_W, (w0 + NA_WROWS) * GRID_W)
        outs = []
        for hh in range(HEADS_PER_BLK):
            sl = slice(hh * NA_DH, (hh + 1) * NA_DH)
            qh = q[qs, sl]
            s_loc = lax.dot_general(qh, k[ws, sl], _NT, preferred_element_type=F32) + bias_ref[hh, t]
            s_ctx = lax.dot_general(qh, ck_ref[hh].astype(BF16), _NT, preferred_element_type=F32)
            m = jnp.maximum(jnp.max(s_loc, -1, keepdims=True), jnp.max(s_ctx, -1, keepdims=True))
            e_loc = jnp.exp(s_loc - m)
            e_ctx = jnp.exp(s_ctx - m)
            inv = 1.0 / (jnp.sum(e_loc, -1, keepdims=True) + jnp.sum(e_ctx, -1, keepdims=True))
            acc = (jnp.dot(e_loc.astype(BF16), v[ws, sl], preferred_element_type=F32)
                   + jnp.dot(e_ctx.astype(BF16), cv_ref[hh].astype(BF16), preferred_element_type=F32))
            outs.append(acc * inv)
        o_ref[qs, :] = jnp.concatenate(outs, -1).astype(BF16)


def _lat_attention(proj, ck, cv, bias, l, b_out):
    rb0 = N_CTX // DEC_SEQ
    cb = lambda base: (lambda j, b: (rb0 + b, base + j))
    c_spec = pl.BlockSpec((None, None, HEADS_PER_BLK, PAST_LEN, NA_DH), lambda j, b: (b, l, j, 0, 0))
    return pl.pallas_call(
        _lat_attn_kernel,
        out_shape=jax.ShapeDtypeStruct((N_TOK, BRANCH_W), BF16),
        grid=(NA_NBLK, DEC_BATCH),
        in_specs=[pl.BlockSpec((DEC_SEQ, 128), cb(Q_COL)), pl.BlockSpec((DEC_SEQ, 128), cb(K_COL)),
                  pl.BlockSpec((DEC_SEQ, 128), cb(V_COL)), c_spec, c_spec,
                  pl.BlockSpec((HEADS_PER_BLK, NA_QT, NA_QROWS * GRID_W, NA_WKEYS), lambda j, b: (j, 0, 0, 0)),
                  pl.BlockSpec(memory_space=pl.ANY)],
        out_specs=pl.BlockSpec((DEC_SEQ, 128), lambda j, b: (rb0 + b, j)),
        input_output_aliases={6: 0},
        compiler_params=_cparams("arbitrary", "arbitrary"),
        name="lat_attention",
    )(proj, proj, proj, ck, cv, bias, b_out)


MA_KSCALE = MA_DK ** -0.5


def _rope_tables(T):
    t = np.arange(T)
    half = MA_DK // 2
    inv = ROPE_BASE ** (-jnp.arange(0, half, 2, dtype=F32) / half)
    ang_r = jnp.asarray((t // GRID_W).astype(np.float32))[:, None] * inv[None, :]
    ang_c = jnp.asarray((t % GRID_W).astype(np.float32))[:, None] * inv[None, :]
    cos = jnp.concatenate([jnp.cos(ang_r)] * 2 + [jnp.cos(ang_c)] * 2, -1)
    sin = jnp.concatenate([-jnp.sin(ang_r), jnp.sin(ang_r), -jnp.sin(ang_c), jnp.sin(ang_c)], -1)
    return cos, sin


def _mlstm_kernel(*refs, T, latent):
    if latent:
        (p_ref, gc_ref, gt_ref, fbc_ref, fbr_ref, cos_ref, sin_ref, c0_ref, n0_ref, m0_ref, prev_ref,
         a_ref, qs, ks, vT, hfT, hbT, CT, ns, ms, brs, kcs) = refs
    else:
        p_ref, gc_ref, gt_ref, fbc_ref, fbr_ref = refs[:5]
        a_ref, co_ref, no_ref, mo_ref, qs, ks, vT, hfT, hbT, CT, ns, ms, brs, kcs = refs[-14:]
    L = MA_CHUNK
    NC = T // L
    W = BRANCH_W
    PER = 128 // L

    lane = lax.broadcasted_iota(jnp.int32, (T, MA_DK), 1)
    lo_half = (lane % (MA_DK // 2)) < (MA_DK // 4)

    def rope(x):
        if not latent:
            return x
        swapped = jnp.where(lo_half, pltpu.roll(x, MA_DK - MA_DK // 4, 1), pltpu.roll(x, MA_DK // 4, 1))
        return x * cos_ref[...] + swapped * sin_ref[...]

    for h in range(MA_HEADS):
        hs = slice(h * MA_DK, (h + 1) * MA_DK)
        qs[:, hs] = rope(p_ref[:, hs]).astype(BF16)
        ks[:, hs] = rope(p_ref[:, W + h * MA_DK:W + (h + 1) * MA_DK] * MA_KSCALE).astype(BF16)

    def v_block(tb, carry):
        r0 = pl.multiple_of(tb * 128, 128)
        for h in range(MA_HEADS):
            hs = slice(h * MA_DV, (h + 1) * MA_DV)
            blk = p_ref[pl.ds(r0, 128), 2 * W + h * MA_DV:2 * W + (h + 1) * MA_DV].T.astype(BF16)
            for j in range(PER):
                vT[tb * PER + j, h] = blk[:, j * L:(j + 1) * L]
        return carry

    lax.fori_loop(0, T // 128, v_block, 0)

    for d in range(2):
        for h in range(MA_HEADS):
            sidx = d * MA_HEADS + h
            CT[d, h] = c0_ref[d, h].T if latent else jnp.zeros((MA_DV, MA_DK), F32)
            ns[d, h] = n0_ref[sidx:sidx + 1, :] if latent else jnp.zeros((1, MA_DK), F32)
            ms[d, h] = m0_ref[sidx:sidx + 1, :] if latent else jnp.zeros((1, 128), F32)

    low = _tri(L, False)
    upp = _tri(L, True)
    rr = lax.broadcasted_iota(jnp.int32, (L, L), 0)
    cc = lax.broadcasted_iota(jnp.int32, (L, L), 1)
    fbc = fbc_ref[...]
    fbr = fbr_ref[...]

    def gate_sums(c, carry):
        t0 = pl.multiple_of(c * L, L)
        gc = gc_ref[pl.ds(t0, L), :]
        lfc = _log_sigmoid(gc + fbc)
        lfr = _log_sigmoid(gt_ref[c] + fbr)
        ish = pltpu.roll(gc, MA_HEADS, 1)
        brs[0, c] = jnp.dot(lfr, upp, precision=HI, preferred_element_type=F32)
        brs[1, c] = jnp.dot(lfr, low, precision=HI, preferred_element_type=F32)
        kcs[0, pl.ds(t0, L), :] = ish - jnp.dot(low, lfc, precision=HI, preferred_element_type=F32)
        kcs[1, pl.ds(t0, L), :] = ish - jnp.dot(upp, lfc, precision=HI, preferred_element_type=F32)
        return carry

    lax.fori_loop(0, NC, gate_sums, 0)

    def chunk(c, d):
        t0 = pl.multiple_of(c * L, L)
        gt = gt_ref[c]
        brow = brs[d, c]
        kc = kcs[d, pl.ds(t0, L), :]
        mask = (rr <= cc) if d == 0 else (rr >= cc)
        last = L - 1 if d == 0 else 0
        heads = range(MA_HEADS)
        gi0 = 2 * d * MA_HEADS
        gf0 = gi0 + MA_HEADS
        br = jnp.stack([brow[gf0 + h:gf0 + h + 1, :] for h in heads], 0)
        ir = jnp.stack([gt[gi0 + h:gi0 + h + 1, :] for h in heads], 0)
        kcol = jnp.stack([kc[:, gf0 + h:gf0 + h + 1] for h in heads], 0)
        m = ms[d][:, :, 0:1]
        n = ns[d]
        ct = CT[d]
        qc = qs[pl.ds(t0, L), :]
        kc = ks[pl.ds(t0, L), :]
        q = jnp.stack([qc[:, h * MA_DK:(h + 1) * MA_DK] for h in heads], 0)
        k = jnp.stack([kc[:, h * MA_DK:(h + 1) * MA_DK] for h in heads], 0)
        vt = vT[c]
        bnt = (((2,), (2,)), ((0,), (0,)))
        bnn = (((2,), (1,)), ((0,), (0,)))
        dmat = jnp.where(mask, br + kcol, -jnp.inf)
        g = br + m
        m_t = jnp.maximum(g, jnp.max(dmat, 1, keepdims=True))
        w_inter = jnp.exp(g - m_t)
        s = lax.dot_general(k, q, bnt, preferred_element_type=F32) * jnp.exp(dmat - m_t)
        ctn = jnp.concatenate([ct.astype(BF16), jnp.broadcast_to(n, (MA_HEADS, 8, MA_DK)).astype(BF16)], 1)
        cq = lax.dot_general(ctn, q, bnt, preferred_element_type=F32)
        num = w_inter * cq[:, :MA_DV] + lax.dot_general(vt, s.astype(BF16), bnn, preferred_element_type=F32)
        den = w_inter * cq[:, MA_DV:MA_DV + 1] + jnp.sum(s, 1, keepdims=True)
        (hfT if d == 0 else hbT)[c] = num / jnp.maximum(jnp.abs(den), jnp.exp(-m_t))
        m_new = m_t[:, :, last:last + 1]
        b_last = br[:, :, last:last + 1]
        decay = jnp.exp(b_last + m - m_new)
        wk = jnp.exp(b_last - br + ir - m_new)
        wk_hi = wk.astype(BF16)
        wk_lo = (wk - wk_hi.astype(F32)).astype(BF16)
        lhs = jnp.concatenate([(vt.astype(F32) * wk).astype(BF16), wk_hi, wk_lo,
                               jnp.zeros((MA_HEADS, 6, L), BF16)], 1)
        upd = lax.dot_general(lhs, k, bnn, preferred_element_type=F32)
        CT[d] = decay * ct + upd[:, :MA_DV]
        ns[d] = decay * n + upd[:, MA_DV:MA_DV + 1] + upd[:, MA_DV + 1:MA_DV + 2]
        ms[d] = jnp.broadcast_to(m_new, (MA_HEADS, 1, 128))

    def body(i, carry):
        chunk(i, 0)
        chunk(NC - 1 - i, 1)
        return carry

    lax.fori_loop(0, NC, body, 0)

    def out_block(tb, carry):
        r0 = pl.multiple_of(tb * 128, 128)
        hsum = jnp.concatenate([hfT[tb * PER + j] + hbT[tb * PER + j] for j in range(PER)], 2)
        outs = [hsum[h].T for h in range(MA_HEADS)]
        gate = jax.nn.sigmoid(p_ref[pl.ds(r0, 128), 3 * W:4 * W])
        a_ref[pl.ds(r0, 128), :] = (gate * jnp.concatenate(outs, 1)).astype(BF16)
        return carry

    lax.fori_loop(0, T // 128, out_block, 0)
    if not latent:
        for d in range(2):
            for h in range(MA_HEADS):
                sidx = d * MA_HEADS + h
                co_ref[d, h] = CT[d, h].T
                no_ref[sidx:sidx + 1, :] = ns[d, h]
                mo_ref[sidx:sidx + 1, :] = ms[d, h]


def _mlstm(proj, gcol, gt3, fbias_l, l, latent, C0=None, n0=None, m0=None, a_out=None, prev=None):
    T = DEC_SEQ if latent else SEQ
    B = DEC_BATCH if latent else BATCH
    rb0 = N_CTX // DEC_SEQ if latent else 0
    fb = fbias_l.astype(F32)
    fbc = jnp.zeros((1, 128), F32).at[0, MA_HEADS:2 * MA_HEADS].set(fb[0]).at[0, 3 * MA_HEADS:4 * MA_HEADS].set(fb[1])
    fbr = fbc[0, :N_GATES].reshape(N_GATES, 1)
    full2 = lambda b: (0, 0)
    any_spec = pl.BlockSpec(memory_space=pl.ANY)
    in_specs = [pl.BlockSpec((T, 4 * BRANCH_W), lambda b: (rb0 + b, 0)),
                pl.BlockSpec((T, 128), lambda b: (rb0 + b, 0)),
                pl.BlockSpec((T // MA_CHUNK, N_GATES, MA_CHUNK), lambda b: (rb0 + b, 0, 0)),
                pl.BlockSpec((1, 128), full2), pl.BlockSpec((N_GATES, 1), full2)]
    args = [proj, gcol, gt3, fbc, fbr]
    a_shape = jax.ShapeDtypeStruct((N_TOK, BRANCH_W), BF16)
    a_spec = pl.BlockSpec((T, BRANCH_W), lambda b: (rb0 + b, 0))
    c_spec = pl.BlockSpec((None, None, 2, MA_HEADS, MA_DK, MA_DV), lambda b: (b, l, 0, 0, 0, 0))
    nm_spec = pl.BlockSpec((None, None, 2 * MA_HEADS, 128), lambda b: (b, l, 0, 0))
    aliases = {}
    if latent:
        cos, sin = _rope_tables(T)
        nb = 2 * MA_HEADS
        in_specs += [pl.BlockSpec((T, MA_DK), full2), pl.BlockSpec((T, MA_DK), full2), c_spec, nm_spec, nm_spec, any_spec]
        args += [cos, sin, C0, n0.reshape(B, DEPTH, nb, MA_DK),
                 jnp.broadcast_to(m0.reshape(B, DEPTH, nb, 1), (B, DEPTH, nb, 128)), a_out]
        aliases = {len(args) - 1: 0}
        out_shape, out_specs = a_shape, a_spec
    else:
        nm_shape = jax.ShapeDtypeStruct((B, DEPTH, 2 * MA_HEADS, 128), F32)
        out_shape = (a_shape, jax.ShapeDtypeStruct((B, DEPTH, 2, MA_HEADS, MA_DK, MA_DV), F32), nm_shape, nm_shape)
        out_specs = (a_spec, c_spec, nm_spec, nm_spec)
        if prev is not None:
            in_specs += [any_spec] * 3
            args += list(prev)
            aliases = {len(args) - 3: 1, len(args) - 2: 2, len(args) - 1: 3}
    nc = T // MA_CHUNK
    scratch = [pltpu.VMEM((T, BRANCH_W), BF16), pltpu.VMEM((T, BRANCH_W), BF16),
               pltpu.VMEM((nc, MA_HEADS, MA_DV, MA_CHUNK), BF16),
               pltpu.VMEM((nc, MA_HEADS, MA_DV, MA_CHUNK), F32), pltpu.VMEM((nc, MA_HEADS, MA_DV, MA_CHUNK), F32),
               pltpu.VMEM((2, MA_HEADS, MA_DV, MA_DK), F32), pltpu.VMEM((2, MA_HEADS, 1, MA_DK), F32),
               pltpu.VMEM((2, MA_HEADS, 1, 128), F32),
               pltpu.VMEM((2, nc, N_GATES, MA_CHUNK), F32), pltpu.VMEM((2, T, 128), F32)]
    return pl.pallas_call(
        functools.partial(_mlstm_kernel, T=T, latent=latent),
        out_shape=out_shape, grid=(B,), in_specs=in_specs, out_specs=out_specs, scratch_shapes=scratch,
        input_output_aliases=aliases,
        compiler_params=_cparams("arbitrary"),
        name="mlstm_lat" if latent else "mlstm_ctx",
    )(*args)


HG_SUB = 8


def _hgrn_kernel(*refs, T, latent):
    ff_ref, fb_ref, q_ref, i_ref, g_ref, lbf_ref, lbb_ref = refs[:7]
    if latent:
        s0_ref = refs[7]
        c_ref, of, ob, ST = refs[-4:]
    else:
        c_ref, so_ref, of, ob, ST = refs[-5:]
    L = HG_CHUNK
    NC = T // L
    NB = L // HG_SUB
    DK = HG_DK

    for d in range(2):
        for h in range(HG_HEADS):
            ST[d, h] = s0_ref[d, h].T if latent else jnp.zeros((HG_DV, DK), F32)

    low = _tri(L, False)
    upp = _tri(L, True)
    row8 = lax.broadcasted_iota(jnp.int32, (HG_SUB, L), 0)
    lane_s = lax.broadcasted_iota(jnp.int32, (HG_SUB, L), 1)

    def chunk(c, d):
        t0 = pl.multiple_of(c * L, L)
        fpre = (ff_ref if d == 0 else fb_ref)[pl.ds(t0, L), :]
        lb = (lbf_ref if d == 0 else lbb_ref)[...]
        f = lb + (1.0 - lb) * jax.nn.sigmoid(fpre)
        logf = jnp.log(f)
        kk = 1.0 - f
        qq = _silu(q_ref[pl.ds(t0, L), :])
        iv = i_ref[pl.ds(t0, L), :].astype(BF16)
        A_all = jnp.dot(low if d == 0 else upp, logf, precision=HI, preferred_element_type=F32)
        last = L - 1 if d == 0 else 0
        for h in range(HG_HEADS):
            hs = slice(h * DK, (h + 1) * DK)
            A, k, q, ivh = A_all[:, hs], kk[:, hs], qq[:, hs], iv[:, hs]
            st = ST[d, h]
            o = lax.dot_general((q * jnp.exp(A)).astype(BF16), st.astype(BF16), _NT, preferred_element_type=F32)
            a_last = A[last:last + 1, :]
            kd = k * jnp.exp(a_last - A)
            rows = []
            for I in range(NB):
                bs = slice(I * HG_SUB, (I + 1) * HG_SUB)
                A_I, q_I = A[bs], q[bs]
                if d == 0:
                    has_off, ref_row, off_mask = I > 0, I * HG_SUB - 1, lane_s < I * HG_SUB
                else:
                    has_off, ref_row, off_mask = I < NB - 1, (I + 1) * HG_SUB, lane_s >= (I + 1) * HG_SUB
                att = jnp.zeros((HG_SUB, L), F32)
                if has_off:
                    R = A[ref_row:ref_row + 1, :]
                    qsc = (q_I * jnp.exp(A_I - R)).astype(BF16)
                    ksc = (k * jnp.exp(jnp.minimum(R - A, 0.0))).astype(BF16)
                    att = jnp.where(off_mask, lax.dot_general(qsc, ksc, _NT, preferred_element_type=F32), 0.0)
                for j in range(HG_SUB):
                    s = I * HG_SUB + j
                    e = jnp.exp(jnp.minimum(A_I - A[s:s + 1, :], 0.0))
                    col = jnp.sum(q_I * k[s:s + 1, :] * e, -1, keepdims=True)
                    keep = (lane_s == s) & ((row8 >= j) if d == 0 else (row8 <= j))
                    att = jnp.where(keep, col, att)
                rows.append(att)
            att = jnp.concatenate(rows, 0)
            o = o + jnp.dot(att.astype(BF16), ivh, preferred_element_type=F32)
            (of if d == 0 else ob)[pl.ds(t0, L), hs] = o
            ST[d, h] = st * jnp.exp(a_last) + lax.dot_general(ivh, kd.astype(BF16), _TN, preferred_element_type=F32)

    def body(i, carry):
        chunk(i, 0)
        chunk(NC - 1 - i, 1)
        return carry

    lax.fori_loop(0, NC, body, 0)

    def epilogue(r, carry):
        t0 = pl.multiple_of(r * 128, 128)
        o = of[pl.ds(t0, 128), :] + ob[pl.ds(t0, 128), :]
        gsil = _silu(g_ref[pl.ds(t0, 128), :])
        outs = []
        for h in range(HG_HEADS):
            oh = o[:, h * HG_DV:(h + 1) * HG_DV]
            outs.append(oh * lax.rsqrt(jnp.mean(oh * oh, -1, keepdims=True) + RMS_EPS))
        c_ref[pl.ds(t0, 128), :] = (jnp.concatenate(outs, -1) * gsil).astype(BF16)
        return carry

    lax.fori_loop(0, T // 128, epilogue, 0)
    if not latent:
        for d in range(2):
            for h in range(HG_HEADS):
                so_ref[d, h] = ST[d, h].T


def _hgrn(proj, lb_l, l, latent, S0=None, c_out=None, prev=None):
    T = DEC_SEQ if latent else SEQ
    B = DEC_BATCH if latent else BATCH
    rb0 = N_CTX // DEC_SEQ if latent else 0
    W = BRANCH_W
    full2 = lambda b: (0, 0)
    any_spec = pl.BlockSpec(memory_space=pl.ANY)
    col = lambda j: pl.BlockSpec((T, W), lambda b: (rb0 + b, j))
    s_spec = pl.BlockSpec((None, None, 2, HG_HEADS, HG_DK, HG_DV), lambda b: (b, l, 0, 0, 0, 0))
    in_specs = [col(7), col(8), col(9), col(10), col(11), pl.BlockSpec((1, W), full2), pl.BlockSpec((1, W), full2)]
    args = [proj] * 5 + [lb_l[0][None, :], lb_l[1][None, :]]
    c_shape = jax.ShapeDtypeStruct((N_TOK, W), BF16)
    c_spec = pl.BlockSpec((T, W), lambda b: (rb0 + b, 0))
    aliases = {}
    if latent:
        in_specs += [s_spec, any_spec]
        args += [S0, c_out]
        aliases = {8: 0}
        out_shape, out_specs = c_shape, c_spec
    else:
        out_shape = (c_shape, jax.ShapeDtypeStruct((B, DEPTH, 2, HG_HEADS, HG_DK, HG_DV), F32))
        out_specs = (c_spec, s_spec)
        if prev is not None:
            in_specs.append(any_spec)
            args.append(prev)
            aliases = {7: 1}
    scratch = [pltpu.VMEM((T, W), F32), pltpu.VMEM((T, W), F32), pltpu.VMEM((2, HG_HEADS, HG_DV, HG_DK), F32)]
    return pl.pallas_call(
        functools.partial(_hgrn_kernel, T=T, latent=latent),
        out_shape=out_shape, grid=(B,), in_specs=in_specs, out_specs=out_specs, scratch_shapes=scratch,
        input_output_aliases=aliases,
        compiler_params=_cparams("arbitrary"),
        name="hgrn_lat" if latent else "hgrn_ctx",
    )(*args)


def _merge_kernel(a_ref, b_ref, c_ref, ga_ref, gb_ref, gc_ref, xc_ref, xl_ref, g1_ref, sh2_ref, sc2_ref,
                  wb_ref, wo_ref, lng_ref, lnb_ref, wr_ref, x1_ref, h2_ref, gate_ref, gid_ref, *, tm):
    def br(v_ref, g_ref, k):
        return jax.nn.sigmoid(g_ref[...]) * jnp.dot(v_ref[...], wb_ref[k], preferred_element_type=F32)

    mix = br(a_ref, ga_ref, 0) + br(b_ref, gb_ref, 1) + br(c_ref, gc_ref, 2)
    y = jnp.dot(mix.astype(BF16), wo_ref[...], preferred_element_type=F32)
    x = _pair_read(pl.program_id(0), tm, xc_ref, xl_ref)
    x1 = _layer_norm(DEEPNORM_ALPHA * x + g1_ref[...] * y, lng_ref[...], lnb_ref[...])
    x1_ref[...] = x1
    h2 = x1 * (1.0 + sc2_ref[...]) + sh2_ref[...]
    h2_ref[...] = h2.astype(BF16)
    logits = jnp.dot(h2, wr_ref[...], preferred_element_type=F32, precision=HI)
    lane = lax.broadcasted_iota(jnp.int32, logits.shape, 1)
    neg = -jnp.inf
    lg = jnp.where(lane < N_GROUPS, logits, neg)
    mg = jnp.max(lg, -1, keepdims=True)
    g_sel = jnp.min(jnp.where(lg == mg, lane, 128), -1, keepdims=True)
    p_sel = 1.0 / jnp.sum(jnp.where(lane < N_GROUPS, jnp.exp(lg - mg), 0.0), -1, keepdims=True)
    lo = N_GROUPS + EXP_PER_GROUP * g_sel
    le = jnp.where((lane >= lo) & (lane < lo + EXP_PER_GROUP), logits, neg)
    v1 = jnp.max(le, -1, keepdims=True)
    i1 = jnp.min(jnp.where(le == v1, lane, 128), -1, keepdims=True)
    le2 = jnp.where(lane == i1, neg, le)
    v2 = jnp.max(le2, -1, keepdims=True)
    i2 = jnp.min(jnp.where(le2 == v2, lane, 128), -1, keepdims=True)
    e2 = jnp.exp(v2 - v1)
    w1 = p_sel / (1.0 + e2)
    w2 = p_sel * e2 / (1.0 + e2)
    gate_ref[...] = jnp.where(lane == i1 - lo, w1, jnp.where(lane == i2 - lo, w2, 0.0))
    gid_ref[...] = jnp.broadcast_to(g_sel, logits.shape)


def _merge(a, b, c, proj, xc, xl, mod, wb, wo, lng, lnb, wr, l):
    tm = 512
    tok = lambda i: (i, 0)
    ln_spec = pl.BlockSpec((None, None, 1, D_MODEL), lambda i: (l, 0, 0, 0))
    return pl.pallas_call(
        functools.partial(_merge_kernel, tm=tm),
        out_shape=(jax.ShapeDtypeStruct((N_TOK, D_MODEL), F32), jax.ShapeDtypeStruct((N_TOK, D_MODEL), BF16),
                   jax.ShapeDtypeStruct((N_TOK, 128), F32), jax.ShapeDtypeStruct((N_TOK, 128), jnp.int32)),
        grid=(N_TOK // tm,),
        in_specs=[pl.BlockSpec((tm, BRANCH_W), tok), pl.BlockSpec((tm, BRANCH_W), tok), pl.BlockSpec((tm, BRANCH_W), tok),
                  pl.BlockSpec((tm, D_MODEL), lambda i: (i, 6)), pl.BlockSpec((tm, D_MODEL), lambda i: (i, 7)),
                  pl.BlockSpec((tm, D_MODEL), lambda i: (i, 8)),
                  *_pair_specs(tm), _mod_spec(l, 2, tm), _mod_spec(l, 3, tm), _mod_spec(l, 4, tm),
                  pl.BlockSpec((None, 3, BRANCH_W, D_MODEL), lambda i: (l, 0, 0, 0)),
                  pl.BlockSpec((None, D_MODEL, D_MODEL), lambda i: (l, 0, 0)),
                  ln_spec, ln_spec,
                  pl.BlockSpec((None, D_MODEL, 128), lambda i: (l, 0, 0))],
        out_specs=(pl.BlockSpec((tm, D_MODEL), tok), pl.BlockSpec((tm, D_MODEL), tok),
                   pl.BlockSpec((tm, 128), tok), pl.BlockSpec((tm, 128), tok)),
        compiler_params=_cparams("arbitrary"),
        name="merge",
    )(a, b, c, proj, proj, proj, xc, xl, mod, mod, mod, wb, wo, lng, lnb, wr)


def _moe_up_kernel(gid_ref, x_ref, gate_ref, w1_ref, w3_ref, hid_ref, w1b, w3b):
    f = pl.program_id(0)
    t = pl.program_id(1)
    first = jnp.logical_or(t == 0, gid_ref[t] != gid_ref[jnp.maximum(t - 1, 0)])

    @pl.when(first)
    def _():
        w1b[...] = w1_ref[...].astype(BF16)
        w3b[...] = w3_ref[...].astype(BF16)

    x = x_ref[...]
    a = jnp.dot(x, w1b[...], preferred_element_type=F32)
    b = jnp.dot(x, w3b[...], preferred_element_type=F32)
    gate = gate_ref[...]
    lane = lax.broadcasted_iota(jnp.int32, gate.shape, 1)
    gcol = jnp.sum(jnp.where(lane == f, gate, 0.0), -1, keepdims=True)
    hid_ref[...] = (_silu(a) * b * gcol).astype(BF16)


def _moe_up(gid, xs, gates, w1, w3, l):
    tm = MOE_TM
    npad = MOE_NT * tm
    w_spec = pl.BlockSpec((None, None, D_MODEL, D_EXPERT), lambda f, t, g: (l, EXP_PER_GROUP * g[t] + f, 0, 0))
    grid_spec = pltpu.PrefetchScalarGridSpec(
        num_scalar_prefetch=1,
        grid=(EXP_PER_GROUP, MOE_NT),
        in_specs=[pl.BlockSpec((tm, D_MODEL), lambda f, t, g: (t, 0)),
                  pl.BlockSpec((tm, 128), lambda f, t, g: (t, 0)), w_spec, w_spec],
        out_specs=pl.BlockSpec((tm, D_EXPERT), lambda f, t, g: (t, f)),
        scratch_shapes=[pltpu.VMEM((D_MODEL, D_EXPERT), BF16), pltpu.VMEM((D_MODEL, D_EXPERT), BF16)],
    )
    return pl.pallas_call(
        _moe_up_kernel,
        out_shape=jax.ShapeDtypeStruct((npad, EXP_PER_GROUP * D_EXPERT), BF16),
        grid_spec=grid_spec,
        compiler_params=_cparams("arbitrary", "arbitrary"),
        name="moe_up",
    )(gid, xs, gates, w1, w3)


def _moe_down_kernel(gid_ref, hid_ref, w2_ref, y_ref, w2b):
    t = pl.program_id(0)
    first = jnp.logical_or(t == 0, gid_ref[t] != gid_ref[jnp.maximum(t - 1, 0)])

    @pl.when(first)
    def _():
        w2b[...] = w2_ref[...].astype(BF16)

    y_ref[...] = jnp.dot(hid_ref[...], w2b[...], preferred_element_type=F32)


def _moe_down(gid, hid, w2g, l):
    tm = MOE_TM
    npad = MOE_NT * tm
    hw = EXP_PER_GROUP * D_EXPERT
    grid_spec = pltpu.PrefetchScalarGridSpec(
        num_scalar_prefetch=1,
        grid=(MOE_NT,),
        in_specs=[pl.BlockSpec((tm, hw), lambda t, g: (t, 0)),
                  pl.BlockSpec((None, None, hw, D_MODEL), lambda t, g: (l, g[t], 0, 0))],
        out_specs=pl.BlockSpec((tm, D_MODEL), lambda t, g: (t, 0)),
        scratch_shapes=[pltpu.VMEM((hw, D_MODEL), BF16)],
    )
    return pl.pallas_call(
        _moe_down_kernel,
        out_shape=jax.ShapeDtypeStruct((npad, D_MODEL), F32),
        grid_spec=grid_spec,
        compiler_params=_cparams("arbitrary"),
        name="moe_down",
    )(gid, hid, w2g)


def _moe(h2, gate, gsel, w1, w3, w2g, l):
    tm = MOE_TM
    npad = MOE_NT * tm
    g = gsel[:, 0]
    onehot = (g[:, None] == jnp.arange(N_GROUPS)[None, :]).astype(jnp.int32)
    counts = jnp.sum(onehot, 0)
    rank = jnp.sum((jnp.cumsum(onehot, 0) - onehot) * onehot, 1)
    padded = (counts + tm - 1) // tm * tm
    ends = jnp.cumsum(padded)
    offs = ends - padded
    dest = offs[g] + rank
    src = jnp.full((npad,), -1, jnp.int32).at[dest].set(jnp.arange(N_TOK, dtype=jnp.int32), unique_indices=True)
    valid = (src >= 0).astype(F32)
    src = jnp.maximum(src, 0)
    starts = jnp.arange(MOE_NT, dtype=jnp.int32) * tm
    tile_gid = jnp.minimum(jnp.sum((ends[None, :] <= starts[:, None]).astype(jnp.int32), 1), N_GROUPS - 1)
    take = lambda arr, idx: arr.at[idx].get(mode="promise_in_bounds", unique_indices=False)
    xs = take(h2, src)
    gs = take(gate, src) * valid[:, None]
    hid = _moe_up(tile_gid, xs, gs, w1, w3, l)
    ys = _moe_down(tile_gid, hid, w2g, l)
    return take(ys, dest)


def _final_kernel(*refs, tm, with_h):
    x1_ref, y_ref, g2_ref, lng_ref, lnb_ref = refs[:5]
    x2 = _layer_norm(DEEPNORM_ALPHA * x1_ref[...] + g2_ref[...] * y_ref[...], lng_ref[...], lnb_ref[...])
    i = pl.program_id(0)
    if with_h:
        sh_ref, sc_ref, xc_ref, xl_ref, h_ref = refs[5:]
        h_ref[...] = (x2 * (1.0 + sc_ref[...]) + sh_ref[...]).astype(BF16)
    else:
        xc_ref, xl_ref = refs[5:]

    @pl.when(i < N_CTX // tm)
    def _():
        xc_ref[...] = x2

    @pl.when(i >= N_CTX // tm)
    def _():
        xl_ref[...] = x2


def _final(x1, y, mod, lng, lnb, l):
    tm = 1024
    tok = lambda i: (i, 0)
    with_h = l + 1 < DEPTH
    ln_spec = pl.BlockSpec((None, None, 1, D_MODEL), lambda i: (l, 1, 0, 0))
    half = jax.ShapeDtypeStruct((N_CTX, D_MODEL), F32)
    in_specs = [pl.BlockSpec((tm, D_MODEL), tok), pl.BlockSpec((tm, D_MODEL), tok), _mod_spec(l, 5, tm), ln_spec, ln_spec]
    args = [x1, y, mod, lng, lnb]
    out_shape = [half, half]
    out_specs = list(_pair_specs(tm))
    if with_h:
        in_specs += [_mod_spec(l + 1, 0, tm), _mod_spec(l + 1, 1, tm)]
        args += [mod, mod]
        out_shape.append(jax.ShapeDtypeStruct((N_TOK, D_MODEL), BF16))
        out_specs.append(pl.BlockSpec((tm, D_MODEL), tok))
    return pl.pallas_call(
        functools.partial(_final_kernel, tm=tm, with_h=with_h),
        out_shape=tuple(out_shape), grid=(N_TOK // tm,), in_specs=in_specs, out_specs=tuple(out_specs),
        compiler_params=_cparams("arbitrary"),
        name="final",
    )(*args)


def kernel(x_prompt, x_sample, c, cache_na_k, cache_na_v, state_mlstm_C, state_mlstm_n, state_mlstm_m, state_hgrn,
           c_ctx, w_mod, b_mod, w_in, b_in, mlstm_fbias, hgrn_lb_logits, na_rpb, w_branch, w_out, ln_g, ln_b,
           w_rg, w_re, w_e1, w_e3, w_e2):
    assert N_CTX == N_LAT
    lb_cum = jnp.cumsum(jax.nn.softmax(hgrn_lb_logits.astype(F32), axis=1), axis=1)
    lb_all = lb_cum - lb_cum[:, :1]

    cs = jnp.zeros((N_MODROWS, D_MODEL), F32).at[0].set(c_ctx).at[1:1 + DEC_BATCH].set(c)
    mod = _modulation(cs, w_mod, b_mod).reshape(DEPTH, N_MODROWS, 6, 1, D_MODEL)

    wb = w_branch.astype(BF16)
    wo = w_out.astype(BF16)
    lng = ln_g.reshape(DEPTH, 2, 1, D_MODEL)
    lnb = ln_b.reshape(DEPTH, 2, 1, D_MODEL)
    wr = jnp.concatenate([w_rg, w_re, jnp.zeros((DEPTH, D_MODEL, 128 - N_GROUPS - N_EXPERTS), F32)], -1)
    w2g = w_e2.reshape(DEPTH, N_GROUPS, EXP_PER_GROUP * D_EXPERT, D_MODEL)
    b_main = jnp.concatenate([b_in[:, :GATE_COL0], b_in[:, GATE_COL0 + N_GATES:]], 1)

    xc = x_prompt.reshape(N_CTX, D_MODEL)
    xl = x_sample.reshape(N_LAT, D_MODEL)
    h = _prep(xc, xl, mod)

    kv = (None, None)
    ma_states = None
    hg_state = None
    for l in range(DEPTH):
        proj = _inproj(h, w_in, b_main[l][None, :], l)
        gcol, gt = _gates(h, w_in, b_in.reshape(DEPTH, 1, N_IN), l)
        gt3 = gt.reshape(N_GATES, N_TOK // MA_CHUNK, MA_CHUNK).transpose(1, 0, 2)

        a, *ma_states = _mlstm(proj, gcol, gt3, mlstm_fbias[l], l, False, prev=ma_states)
        a = _mlstm(proj, gcol, gt3, mlstm_fbias[l], l, True, state_mlstm_C, state_mlstm_n, state_mlstm_m, a_out=a)
        b, *kv = _ctx_attention(proj, l, *kv)
        b = _lat_attention(proj, cache_na_k, cache_na_v, _na_bias_table(na_rpb[l]), l, b)
        cc, hg_state = _hgrn(proj, lb_all[:, l], l, False, prev=hg_state)
        cc = _hgrn(proj, lb_all[:, l], l, True, state_hgrn, c_out=cc)

        x1, h2, gate, gsel = _merge(a, b, cc, proj, xc, xl, mod, wb, wo, lng, lnb, wr, l)
        y2 = _moe(h2, gate, gsel, w_e1, w_e3, w2g, l)
        outs = _final(x1, y2, mod, lng, lnb, l)
        xc, xl = outs[0], outs[1]
        if l + 1 < DEPTH:
            h = outs[2]

    dt = x_prompt.dtype
    new_C, new_n, new_m = ma_states
    new_n = new_n.reshape(BATCH, DEPTH, 2, MA_HEADS, MA_DK)
    new_m = new_m[:, :, :, 0].reshape(BATCH, DEPTH, 2, MA_HEADS)
    return (xc.reshape(BATCH, SEQ, D_MODEL), xl.reshape(DEC_BATCH, DEC_SEQ, D_MODEL), kv[0], kv[1],
            new_C.astype(dt), new_n.astype(dt), new_m.astype(dt), hg_state.astype(dt))
```

```python
import functools

import numpy as np
import jax
import jax.numpy as jnp
from jax import lax
from jax.experimental import pallas as pl
from jax.experimental.pallas import tpu as pltpu

F32 = jnp.float32
BF16 = jnp.bfloat16
HI = lax.Precision.HIGHEST

D_MODEL = 1024
BATCH = 16
SEQ = 256
DEPTH = 2
DEC_BATCH = 4
DEC_SEQ = 1024
PAST_LEN = 256
GRID_W = 64
MA_HEADS = 4
MA_DK = 128
MA_DV = 128
MA_CHUNK = 64
NA_HEADS = 8
NA_DH = 64
NA_KR_MAX = 8
NA_KC = 16
HG_HEADS = 4
HG_DK = 128
HG_DV = 128
HG_CHUNK = 32
BRANCH_W = 512
N_GROUPS = 4
EXP_PER_GROUP = 4
N_EXPERTS = N_GROUPS * EXP_PER_GROUP
D_EXPERT = 512
ROPE_BASE = 10000.0
LN_EPS = 1e-5
RMS_EPS = 1e-6
DEEPNORM_ALPHA = (2 * DEPTH) ** 0.25

N_CTX = BATCH * SEQ
N_LAT = DEC_BATCH * DEC_SEQ
N_TOK = N_CTX + N_LAT
N_MODROWS = 8
GATE_COL0 = 4 * BRANCH_W
N_GATES = 4 * MA_HEADS
N_IN = 9232
P_COLS = N_IN - N_GATES
MOE_TM = 256
MOE_NT = N_TOK // MOE_TM + N_GROUPS
VMEM_LIMIT = 48 * 1024 * 1024

_NT = (((1,), (1,)), ((), ()))
_TN = (((0,), (0,)), ((), ()))


def _cparams(*sem):
    return pltpu.CompilerParams(dimension_semantics=sem, vmem_limit_bytes=VMEM_LIMIT)


def _mod_row(tile, tm):
    return jnp.maximum((tile * tm) // DEC_SEQ - (N_CTX // DEC_SEQ - 1), 0)


def _mod_spec(l, part, tm):
    return pl.BlockSpec((None, None, None, 1, D_MODEL), lambda i: (l, _mod_row(i, tm), part, 0, 0))


def _pair_specs(tm):
    nc = N_CTX // tm
    return (pl.BlockSpec((tm, D_MODEL), lambda i: (jnp.minimum(i, nc - 1), 0)),
            pl.BlockSpec((tm, D_MODEL), lambda i: (jnp.maximum(i - nc, 0), 0)))


def _pair_read(i, tm, c_ref, l_ref):
    return jnp.where(i < N_CTX // tm, c_ref[...], l_ref[...])


def _silu(x):
    return x * jax.nn.sigmoid(x)


def _layer_norm(x, g, b):
    mu = jnp.mean(x, -1, keepdims=True)
    xc = x - mu
    var = jnp.mean(xc * xc, -1, keepdims=True)
    return xc * lax.rsqrt(var + LN_EPS) * g + b


def _log_sigmoid(x):
    return jnp.minimum(x, 0.0) - jnp.log(1.0 + jnp.exp(-jnp.abs(x)))


def _tri(n, upper):
    r = lax.broadcasted_iota(jnp.int32, (n, n), 0)
    c = lax.broadcasted_iota(jnp.int32, (n, n), 1)
    return jnp.where((r <= c) if upper else (r >= c), 1.0, 0.0).astype(F32)


def _mod_kernel(c_ref, w_ref, b_ref, o_ref):
    s = _silu(c_ref[...])
    o_ref[...] = jnp.dot(s.astype(BF16), w_ref[...].astype(BF16), preferred_element_type=F32) + b_ref[...]


def _modulation(cs, w_mod, b_mod):
    tn = 1024
    return pl.pallas_call(
        _mod_kernel,
        out_shape=jax.ShapeDtypeStruct((DEPTH, N_MODROWS, 6 * D_MODEL), F32),
        grid=(DEPTH, 6 * D_MODEL // tn),
        in_specs=[pl.BlockSpec((N_MODROWS, D_MODEL), lambda l, j: (0, 0)),
                  pl.BlockSpec((None, D_MODEL, tn), lambda l, j: (l, 0, j)),
                  pl.BlockSpec((None, 1, tn), lambda l, j: (l, 0, j))],
        out_specs=pl.BlockSpec((None, N_MODROWS, tn), lambda l, j: (l, 0, j)),
        compiler_params=_cparams("arbitrary", "arbitrary"),
        name="modulation",
    )(cs, w_mod, b_mod.reshape(DEPTH, 1, 6 * D_MODEL))


def _prep_kernel(xc_ref, xl_ref, sh_ref, sc_ref, h_ref, *, tm):
    x = _pair_read(pl.program_id(0), tm, xc_ref, xl_ref)
    h_ref[...] = (x * (1.0 + sc_ref[...]) + sh_ref[...]).astype(BF16)


def _prep(xc, xl, mod):
    tm = 1024
    return pl.pallas_call(
        functools.partial(_prep_kernel, tm=tm),
        out_shape=jax.ShapeDtypeStruct((N_TOK, D_MODEL), BF16),
        grid=(N_TOK // tm,),
        in_specs=[*_pair_specs(tm), _mod_spec(0, 0, tm), _mod_spec(0, 1, tm)],
        out_specs=pl.BlockSpec((tm, D_MODEL), lambda i: (i, 0)),
        compiler_params=_cparams("arbitrary"),
        name="prep",
    )(xc, xl, mod, mod)


INPROJ_TN = 512
N_PLAIN_TILES = GATE_COL0 // INPROJ_TN


def _inproj_kernel(h_ref, wa_ref, wb_ref, b_ref, o_ref):
    j = pl.program_id(1)

    @pl.when(j < N_PLAIN_TILES)
    def _():
        o_ref[...] = lax.dot_general(h_ref[...], wa_ref[...].astype(BF16), _NT, preferred_element_type=F32) + b_ref[...]

    @pl.when(j >= N_PLAIN_TILES)
    def _():
        w = jnp.concatenate([wa_ref[N_GATES:, :], wb_ref[...]], 0)
        o_ref[...] = lax.dot_general(h_ref[...], w.astype(BF16), _NT, preferred_element_type=F32) + b_ref[...]


def _inproj(h, w_t, b_main, l):
    tm, tn = 2048, INPROJ_TN
    return pl.pallas_call(
        _inproj_kernel,
        out_shape=jax.ShapeDtypeStruct((N_TOK, P_COLS), F32),
        grid=(N_TOK // tm, P_COLS // tn),
        in_specs=[pl.BlockSpec((tm, D_MODEL), lambda i, j: (i, 0)),
                  pl.BlockSpec((None, tn, D_MODEL), lambda i, j: (l, j, 0)),
                  pl.BlockSpec((None, N_GATES, D_MODEL), lambda i, j: (l, (j + 1) * (tn // N_GATES), 0)),
                  pl.BlockSpec((1, tn), lambda i, j: (0, j))],
        out_specs=pl.BlockSpec((tm, tn), lambda i, j: (i, j)),
        compiler_params=_cparams("arbitrary", "arbitrary"),
        name="inproj",
    )(h, w_t, w_t, b_main)


def _gates_kernel(h_ref, w_ref, b_ref, gc_ref, gt_ref):
    g = lax.dot_general(h_ref[...], w_ref[...].astype(BF16), _NT, preferred_element_type=F32) + b_ref[...]
    gc_ref[...] = g
    gt_ref[...] = g.T[:N_GATES]


def _gates(h, w_t, b_in3, l):
    tm = 1024
    gblk = GATE_COL0 // 128
    return pl.pallas_call(
        _gates_kernel,
        out_shape=(jax.ShapeDtypeStruct((N_TOK, 128), F32), jax.ShapeDtypeStruct((N_GATES, N_TOK), F32)),
        grid=(N_TOK // tm,),
        in_specs=[pl.BlockSpec((tm, D_MODEL), lambda i: (i, 0)),
                  pl.BlockSpec((None, 128, D_MODEL), lambda i: (l, gblk, 0)),
                  pl.BlockSpec((None, 1, 128), lambda i: (l, 0, gblk))],
        out_specs=(pl.BlockSpec((tm, 128), lambda i: (i, 0)), pl.BlockSpec((N_GATES, tm), lambda i: (0, i))),
        compiler_params=_cparams("arbitrary"),
        name="gates",
    )(h, w_t, b_in3)


HEADS_PER_BLK = 128 // NA_DH
NA_NBLK = NA_HEADS // HEADS_PER_BLK
NA_QSCALE = NA_DH ** -0.5
Q_COL, K_COL, V_COL = 16, 20, 24


def _ctx_attn_kernel(*refs, first):
    q_ref, k_ref, v_ref = refs[:3]
    o_ref, ko_ref, vo_ref = refs[-3:]
    q = q_ref[...] * NA_QSCALE
    k = k_ref[...]
    v = v_ref[...]
    outs = []
    for hh in range(HEADS_PER_BLK):
        sl = slice(hh * NA_DH, (hh + 1) * NA_DH)
        kh, vh = k[:, sl], v[:, sl]
        ko_ref[hh] = kh
        vo_ref[hh] = vh
        s = lax.dot_general(q[:, sl].astype(BF16), kh.astype(BF16), _NT, preferred_element_type=F32)
        e = jnp.exp(s - jnp.max(s, -1, keepdims=True))
        p = e * (1.0 / jnp.sum(e, -1, keepdims=True))
        outs.append(jnp.dot(p.astype(BF16), vh.astype(BF16), preferred_element_type=F32))
    o_ref[...] = jnp.concatenate(outs, -1).astype(BF16)


def _ctx_attention(proj, l, prev_k=None, prev_v=None):
    cb = lambda base: (lambda b, j: (b, base + j))
    kv_shape = jax.ShapeDtypeStruct((BATCH, DEPTH, NA_HEADS, SEQ, NA_DH), F32)
    kv_spec = pl.BlockSpec((None, None, HEADS_PER_BLK, SEQ, NA_DH), lambda b, j: (b, l, j, 0, 0))
    in_specs = [pl.BlockSpec((SEQ, 128), cb(Q_COL)), pl.BlockSpec((SEQ, 128), cb(K_COL)), pl.BlockSpec((SEQ, 128), cb(V_COL))]
    args = [proj, proj, proj]
    aliases = {}
    if prev_k is not None:
        in_specs += [pl.BlockSpec(memory_space=pl.ANY)] * 2
        args += [prev_k, prev_v]
        aliases = {3: 1, 4: 2}
    return pl.pallas_call(
        functools.partial(_ctx_attn_kernel, first=prev_k is None),
        out_shape=(jax.ShapeDtypeStruct((N_TOK, BRANCH_W), BF16), kv_shape, kv_shape),
        grid=(BATCH, NA_NBLK),
        in_specs=in_specs,
        out_specs=(pl.BlockSpec((SEQ, 128), lambda b, j: (b, j)), kv_spec, kv_spec),
        input_output_aliases=aliases,
        compiler_params=_cparams("arbitrary", "arbitrary"),
        name="ctx_attention",
    )(*args)


NA_ROWS = DEC_SEQ // GRID_W
NA_KR = min(NA_KR_MAX, NA_ROWS)
NA_QROWS = 4
NA_QT = NA_ROWS // NA_QROWS
NA_WROWS = NA_KR + NA_QROWS - 1
NA_WKEYS = NA_WROWS * GRID_W


def _na_window_start(t):
    return min(max(t * NA_QROWS - NA_KR // 2, 0), NA_ROWS - NA_WROWS)


def _na_bias_table(rpb):
    c = np.arange(GRID_W)
    c0 = np.clip(c - NA_KC // 2, 0, GRID_W - NA_KC)
    kc = np.arange(GRID_W)
    valid = (kc[None, :] >= c0[:, None]) & (kc[None, :] < c0[:, None] + NA_KC)
    dc = kc[None, :] - c[:, None] + NA_KC - 1
    onehot = (dc[None] == np.arange(2 * NA_KC - 1)[:, None, None]) & valid[None]
    toep = jnp.einsum('hrd,dcx->hrcx', rpb.astype(F32), jnp.asarray(onehot, F32), precision=HI)
    toep = jnp.where(valid[None, None], toep, -jnp.inf)
    ninf = jnp.full((NA_HEADS, GRID_W, GRID_W), -jnp.inf, F32)
    tiles = []
    for t in range(NA_QT):
        w0 = _na_window_start(t)
        qrows = []
        for r in range(t * NA_QROWS, (t + 1) * NA_QROWS):
            r0 = min(max(r - NA_KR // 2, 0), NA_ROWS - NA_KR)
            blocks = []
            for kr in range(w0, w0 + NA_WROWS):
                inside = r0 <= kr < r0 + NA_KR
                blocks.append(toep[:, kr - r + NA_KR_MAX - 1] if inside else ninf)
            qrows.append(jnp.concatenate(blocks, -1))
        tiles.append(jnp.concatenate(qrows, 1))
    return jnp.stack(tiles, 1)


def _lat_attn_kernel(q_ref, k_ref, v_ref, ck_ref, cv_ref, bias_ref, prev_ref, o_ref):
    q = (q_ref[...] * NA_QSCALE).astype(BF16)
    k = k_ref[...].astype(BF16)
    v = v_ref[...].astype(BF16)
    nq = NA_QROWS * GRID_W
    for t in range(NA_QT):
        w0 = _na_window_start(t)
        qs = slice(t * nq, (t + 1) * nq)
        ws = slice(w0 * GRID_W, (w0 + NA_WROWS) * GRID_W)
        outs = []
        for hh in range(HEADS_PER_BLK):
            sl = slice(hh * NA_DH, (hh + 1) * NA_DH)
            qh = q[qs, sl]
            s_loc = lax.dot_general(qh, k[ws, sl], _NT, preferred_element_type=F32) + bias_ref[hh, t]
            s_ctx = lax.dot_general(qh, ck_ref[hh].astype(BF16), _NT, preferred_element_type=F32)
            m = jnp.maximum(jnp.max(s_loc, -1, keepdims=True), jnp.max(s_ctx, -1, keepdims=True))
            e_loc = jnp.exp(s_loc - m)
            e_ctx = jnp.exp(s_ctx - m)
            inv = 1.0 / (jnp.sum(e_loc, -1, keepdims=True) + jnp.sum(e_ctx, -1, keepdims=True))
            acc = (jnp.dot(e_loc.astype(BF16), v[ws, sl], preferred_element_type=F32)
                   + jnp.dot(e_ctx.astype(BF16), cv_ref[hh].astype(BF16), preferred_element_type=F32))
            outs.append(acc * inv)
        o_ref[qs, :] = jnp.concatenate(outs, -1).astype(BF16)


def _lat_attention(proj, ck, cv, bias, l, b_out):
    rb0 = N_CTX // DEC_SEQ
    cb = lambda base: (lambda j, b: (rb0 + b, base + j))
    c_spec = pl.BlockSpec((None, None, HEADS_PER_BLK, PAST_LEN, NA_DH), lambda j, b: (b, l, j, 0, 0))
    return pl.pallas_call(
        _lat_attn_kernel,
        out_shape=jax.ShapeDtypeStruct((N_TOK, BRANCH_W), BF16),
        grid=(NA_NBLK, DEC_BATCH),
        in_specs=[pl.BlockSpec((DEC_SEQ, 128), cb(Q_COL)), pl.BlockSpec((DEC_SEQ, 128), cb(K_COL)),
                  pl.BlockSpec((DEC_SEQ, 128), cb(V_COL)), c_spec, c_spec,
                  pl.BlockSpec((HEADS_PER_BLK, NA_QT, NA_QROWS * GRID_W, NA_WKEYS), lambda j, b: (j, 0, 0, 0)),
                  pl.BlockSpec(memory_space=pl.ANY)],
        out_specs=pl.BlockSpec((DEC_SEQ, 128), lambda j, b: (rb0 + b, j)),
        input_output_aliases={6: 0},
        compiler_params=_cparams("arbitrary", "arbitrary"),
        name="lat_attention",
    )(proj, proj, proj, ck, cv, bias, b_out)


MA_KSCALE = MA_DK ** -0.5


def _rope_tables(T):
    t = np.arange(T)
    half = MA_DK // 2
    inv = ROPE_BASE ** (-jnp.arange(0, half, 2, dtype=F32) / half)
    ang_r = jnp.asarray((t // GRID_W).astype(np.float32))[:, None] * inv[None, :]
    ang_c = jnp.asarray((t % GRID_W).astype(np.float32))[:, None] * inv[None, :]
    cos = jnp.concatenate([jnp.cos(ang_r)] * 2 + [jnp.cos(ang_c)] * 2, -1)
    sin = jnp.concatenate([-jnp.sin(ang_r), jnp.sin(ang_r), -jnp.sin(ang_c), jnp.sin(ang_c)], -1)
    return cos, sin


def _mlstm_kernel(*refs, T, latent):
    if latent:
        (p_ref, gc_ref, gt_ref, fbc_ref, fbr_ref, cos_ref, sin_ref, c0_ref, n0_ref, m0_ref, prev_ref,
         a_ref, qs, ks, vT, hfT, hbT, CT, ns, ms, brs, kcs) = refs
    else:
        p_ref, gc_ref, gt_ref, fbc_ref, fbr_ref = refs[:5]
        a_ref, co_ref, no_ref, mo_ref, qs, ks, vT, hfT, hbT, CT, ns, ms, brs, kcs = refs[-14:]
    L = MA_CHUNK
    NC = T // L
    W = BRANCH_W
    PER = 128 // L

    lane = lax.broadcasted_iota(jnp.int32, (T, MA_DK), 1)
    lo_half = (lane % (MA_DK // 2)) < (MA_DK // 4)

    def rope(x):
        if not latent:
            return x
        swapped = jnp.where(lo_half, pltpu.roll(x, MA_DK - MA_DK // 4, 1), pltpu.roll(x, MA_DK // 4, 1))
        return x * cos_ref[...] + swapped * sin_ref[...]

    for h in range(MA_HEADS):
        hs = slice(h * MA_DK, (h + 1) * MA_DK)
        qs[:, hs] = rope(p_ref[:, hs]).astype(BF16)
        ks[:, hs] = rope(p_ref[:, W + h * MA_DK:W + (h + 1) * MA_DK] * MA_KSCALE).astype(BF16)

    def v_block(tb, carry):
        r0 = pl.multiple_of(tb * 128, 128)
        for h in range(MA_HEADS):
            hs = slice(h * MA_DV, (h + 1) * MA_DV)
            blk = p_ref[pl.ds(r0, 128), 2 * W + h * MA_DV:2 * W + (h + 1) * MA_DV].T.astype(BF16)
            for j in range(PER):
                vT[tb * PER + j, h] = blk[:, j * L:(j + 1) * L]
        return carry

    lax.fori_loop(0, T // 128, v_block, 0)

    for d in range(2):
        for h in range(MA_HEADS):
            sidx = d * MA_HEADS + h
            CT[d, h] = c0_ref[d, h].T if latent else jnp.zeros((MA_DV, MA_DK), F32)
            ns[d, h] = n0_ref[sidx:sidx + 1, :] if latent else jnp.zeros((1, MA_DK), F32)
            ms[d, h] = m0_ref[sidx:sidx + 1, :] if latent else jnp.zeros((1, 128), F32)

    low = _tri(L, False)
    upp = _tri(L, True)
    rr = lax.broadcasted_iota(jnp.int32, (L, L), 0)
    cc = lax.broadcasted_iota(jnp.int32, (L, L), 1)
    fbc = fbc_ref[...]
    fbr = fbr_ref[...]

    def gate_sums(c, carry):
        t0 = pl.multiple_of(c * L, L)
        gc = gc_ref[pl.ds(t0, L), :]
        lfc = _log_sigmoid(gc + fbc)
        lfr = _log_sigmoid(gt_ref[c] + fbr)
        ish = pltpu.roll(gc, MA_HEADS, 1)
        brs[0, c] = jnp.dot(lfr, upp, precision=HI, preferred_element_type=F32)
        brs[1, c] = jnp.dot(lfr, low, precision=HI, preferred_element_type=F32)
        kcs[0, pl.ds(t0, L), :] = ish - jnp.dot(low, lfc, precision=HI, preferred_element_type=F32)
        kcs[1, pl.ds(t0, L), :] = ish - jnp.dot(upp, lfc, precision=HI, preferred_element_type=F32)
        return carry

    lax.fori_loop(0, NC, gate_sums, 0)

    def chunk(c, d):
        t0 = pl.multiple_of(c * L, L)
        gt = gt_ref[c]
        brow = brs[d, c]
        kc = kcs[d, pl.ds(t0, L), :]
        mask = (rr <= cc) if d == 0 else (rr >= cc)
        last = L - 1 if d == 0 else 0
        heads = range(MA_HEADS)
        gi0 = 2 * d * MA_HEADS
        gf0 = gi0 + MA_HEADS
        br = jnp.stack([brow[gf0 + h:gf0 + h + 1, :] for h in heads], 0)
        ir = jnp.stack([gt[gi0 + h:gi0 + h + 1, :] for h in heads], 0)
        kcol = jnp.stack([kc[:, gf0 + h:gf0 + h + 1] for h in heads], 0)
        m = ms[d][:, :, 0:1]
        n = ns[d]
        ct = CT[d]
        qc = qs[pl.ds(t0, L), :]
        kc = ks[pl.ds(t0, L), :]
        q = jnp.stack([qc[:, h * MA_DK:(h + 1) * MA_DK] for h in heads], 0)
        k = jnp.stack([kc[:, h * MA_DK:(h + 1) * MA_DK] for h in heads], 0)
        vt = vT[c]
        bnt = (((2,), (2,)), ((0,), (0,)))
        bnn = (((2,), (1,)), ((0,), (0,)))
        dmat = jnp.where(mask, br + kcol, -jnp.inf)
        g = br + m
        m_t = jnp.maximum(g, jnp.max(dmat, 1, keepdims=True))
        w_inter = jnp.exp(g - m_t)
        s = lax.dot_general(k, q, bnt, preferred_element_type=F32) * jnp.exp(dmat - m_t)
        ctn = jnp.concatenate([ct.astype(BF16), jnp.broadcast_to(n, (MA_HEADS, 8, MA_DK)).astype(BF16)], 1)
        cq = lax.dot_general(ctn, q, bnt, preferred_element_type=F32)
        num = w_inter * cq[:, :MA_DV] + lax.dot_general(vt, s.astype(BF16), bnn, preferred_element_type=F32)
        den = w_inter * cq[:, MA_DV:MA_DV + 1] + jnp.sum(s, 1, keepdims=True)
        (hfT if d == 0 else hbT)[c] = num / jnp.maximum(jnp.abs(den), jnp.exp(-m_t))
        m_new = m_t[:, :, last:last + 1]
        b_last = br[:, :, last:last + 1]
        decay = jnp.exp(b_last + m - m_new)
        wk = jnp.exp(b_last - br + ir - m_new)
        wk_hi = wk.astype(BF16)
        wk_lo = (wk - wk_hi.astype(F32)).astype(BF16)
        lhs = jnp.concatenate([(vt.astype(F32) * wk).astype(BF16), wk_hi, wk_lo,
                               jnp.zeros((MA_HEADS, 6, L), BF16)], 1)
        upd = lax.dot_general(lhs, k, bnn, preferred_element_type=F32)
        CT[d] = decay * ct + upd[:, :MA_DV]
        ns[d] = decay * n + upd[:, MA_DV:MA_DV + 1] + upd[:, MA_DV + 1:MA_DV + 2]
        ms[d] = jnp.broadcast_to(m_new, (MA_HEADS, 1, 128))

    def body(i, carry):
        chunk(i, 0)
        chunk(NC - 1 - i, 1)
        return carry

    lax.fori_loop(0, NC, body, 0)

    def out_block(tb, carry):
        r0 = pl.multiple_of(tb * 128, 128)
        hsum = jnp.concatenate([hfT[tb * PER + j] + hbT[tb * PER + j] for j in range(PER)], 2)
        outs = [hsum[h].T for h in range(MA_HEADS)]
        gate = jax.nn.sigmoid(p_ref[pl.ds(r0, 128), 3 * W:4 * W])
        a_ref[pl.ds(r0, 128), :] = (gate * jnp.concatenate(outs, 1)).astype(BF16)
        return carry

    lax.fori_loop(0, T // 128, out_block, 0)
    if not latent:
        for d in range(2):
            for h in range(MA_HEADS):
                sidx = d * MA_HEADS + h
                co_ref[d, h] = CT[d, h].T
                no_ref[sidx:sidx + 1, :] = ns[d, h]
                mo_ref[sidx:sidx + 1, :] = ms[d, h]


def _mlstm(proj, gcol, gt3, fbias_l, l, latent, C0=None, n0=None, m0=None, a_out=None, prev=None):
    T = DEC_SEQ if latent else SEQ
    B = DEC_BATCH if latent else BATCH
    rb0 = N_CTX // DEC_SEQ if latent else 0
    fb = fbias_l.astype(F32)
    fbc = jnp.zeros((1, 128), F32).at[0, MA_HEADS:2 * MA_HEADS].set(fb[0]).at[0, 3 * MA_HEADS:4 * MA_HEADS].set(fb[1])
    fbr = fbc[0, :N_GATES].reshape(N_GATES, 1)
    full2 = lambda b: (0, 0)
    any_spec = pl.BlockSpec(memory_space=pl.ANY)
    in_specs = [pl.BlockSpec((T, 4 * BRANCH_W), lambda b: (rb0 + b, 0)),
                pl.BlockSpec((T, 128), lambda b: (rb0 + b, 0)),
                pl.BlockSpec((T // MA_CHUNK, N_GATES, MA_CHUNK), lambda b: (rb0 + b, 0, 0)),
                pl.BlockSpec((1, 128), full2), pl.BlockSpec((N_GATES, 1), full2)]
    args = [proj, gcol, gt3, fbc, fbr]
    a_shape = jax.ShapeDtypeStruct((N_TOK, BRANCH_W), BF16)
    a_spec = pl.BlockSpec((T, BRANCH_W), lambda b: (rb0 + b, 0))
    c_spec = pl.BlockSpec((None, None, 2, MA_HEADS, MA_DK, MA_DV), lambda b: (b, l, 0, 0, 0, 0))
    nm_spec = pl.BlockSpec((None, None, 2 * MA_HEADS, 128), lambda b: (b, l, 0, 0))
    aliases = {}
    if latent:
        cos, sin = _rope_tables(T)
        nb = 2 * MA_HEADS
        in_specs += [pl.BlockSpec((T, MA_DK), full2), pl.BlockSpec((T, MA_DK), full2), c_spec, nm_spec, nm_spec, any_spec]
        args += [cos, sin, C0, n0.reshape(B, DEPTH, nb, MA_DK),
                 jnp.broadcast_to(m0.reshape(B, DEPTH, nb, 1), (B, DEPTH, nb, 128)), a_out]
        aliases = {len(args) - 1: 0}
        out_shape, out_specs = a_shape, a_spec
    else:
        nm_shape = jax.ShapeDtypeStruct((B, DEPTH, 2 * MA_HEADS, 128), F32)
        out_shape = (a_shape, jax.ShapeDtypeStruct((B, DEPTH, 2, MA_HEADS, MA_DK, MA_DV), F32), nm_shape, nm_shape)
        out_specs = (a_spec, c_spec, nm_spec, nm_spec)
        if prev is not None:
            in_specs += [any_spec] * 3
            args += list(prev)
            aliases = {len(args) - 3: 1, len(args) - 2: 2, len(args) - 1: 3}
    nc = T // MA_CHUNK
    scratch = [pltpu.VMEM((T, BRANCH_W), BF16), pltpu.VMEM((T, BRANCH_W), BF16),
               pltpu.VMEM((nc, MA_HEADS, MA_DV, MA_CHUNK), BF16),
               pltpu.VMEM((nc, MA_HEADS, MA_DV, MA_CHUNK), F32), pltpu.VMEM((nc, MA_HEADS, MA_DV, MA_CHUNK), F32),
               pltpu.VMEM((2, MA_HEADS, MA_DV, MA_DK), F32), pltpu.VMEM((2, MA_HEADS, 1, MA_DK), F32),
               pltpu.VMEM((2, MA_HEADS, 1, 128), F32),
               pltpu.VMEM((2, nc, N_GATES, MA_CHUNK), F32), pltpu.VMEM((2, T, 128), F32)]
    return pl.pallas_call(
        functools.partial(_mlstm_kernel, T=T, latent=latent),
        out_shape=out_shape, grid=(B,), in_specs=in_specs, out_specs=out_specs, scratch_shapes=scratch,
        input_output_aliases=aliases,
        compiler_params=_cparams("arbitrary"),
        name="mlstm_lat" if latent else "mlstm_ctx",
    )(*args)


HG_SUB = 8


def _hgrn_kernel(*refs, T, latent):
    ff_ref, fb_ref, q_ref, i_ref, g_ref, lbf_ref, lbb_ref = refs[:7]
    if latent:
        s0_ref = refs[7]
        c_ref, of, ob, ST = refs[-4:]
    else:
        c_ref, so_ref, of, ob, ST = refs[-5:]
    L = HG_CHUNK
    NC = T // L
    NB = L // HG_SUB
    DK = HG_DK

    for d in range(2):
        for h in range(HG_HEADS):
            ST[d, h] = s0_ref[d, h].T if latent else jnp.zeros((HG_DV, DK), F32)

    low = _tri(L, False)
    upp = _tri(L, True)
    row8 = lax.broadcasted_iota(jnp.int32, (HG_SUB, L), 0)
    lane_s = lax.broadcasted_iota(jnp.int32, (HG_SUB, L), 1)

    def chunk(c, d):
        t0 = pl.multiple_of(c * L, L)
        fpre = (ff_ref if d == 0 else fb_ref)[pl.ds(t0, L), :]
        lb = (lbf_ref if d == 0 else lbb_ref)[...]
        f = lb + (1.0 - lb) * jax.nn.sigmoid(fpre)
        logf = jnp.log(f)
        kk = 1.0 - f
        qq = _silu(q_ref[pl.ds(t0, L), :])
        iv = i_ref[pl.ds(t0, L), :].astype(BF16)
        A_all = jnp.dot(low if d == 0 else upp, logf, precision=HI, preferred_element_type=F32)
        last = L - 1 if d == 0 else 0
        for h in range(HG_HEADS):
            hs = slice(h * DK, (h + 1) * DK)
            A, k, q, ivh = A_all[:, hs], kk[:, hs], qq[:, hs], iv[:, hs]
            st = ST[d, h]
            o = lax.dot_general((q * jnp.exp(A)).astype(BF16), st.astype(BF16), _NT, preferred_element_type=F32)
            a_last = A[last:last + 1, :]
            kd = k * jnp.exp(a_last - A)
            rows = []
            for I in range(NB):
                bs = slice(I * HG_SUB, (I + 1) * HG_SUB)
                A_I, q_I = A[bs], q[bs]
                if d == 0:
                    has_off, ref_row, off_mask = I > 0, I * HG_SUB - 1, lane_s < I * HG_SUB
                else:
                    has_off, ref_row, off_mask = I < NB - 1, (I + 1) * HG_SUB, lane_s >= (I + 1) * HG_SUB
                att = jnp.zeros((HG_SUB, L), F32)
                if has_off:
                    R = A[ref_row:ref_row + 1, :]
                    qsc = (q_I * jnp.exp(A_I - R)).astype(BF16)
                    ksc = (k * jnp.exp(jnp.minimum(R - A, 0.0))).astype(BF16)
                    att = jnp.where(off_mask, lax.dot_general(qsc, ksc, _NT, preferred_element_type=F32), 0.0)
                for j in range(HG_SUB):
                    s = I * HG_SUB + j
                    e = jnp.exp(jnp.minimum(A_I - A[s:s + 1, :], 0.0))
                    col = jnp.sum(q_I * k[s:s + 1, :] * e, -1, keepdims=True)
                    keep = (lane_s == s) & ((row8 >= j) if d == 0 else (row8 <= j))
                    att = jnp.where(keep, col, att)
                rows.append(att)
            att = jnp.concatenate(rows, 0)
            o = o + jnp.dot(att.astype(BF16), ivh, preferred_element_type=F32)
            (of if d == 0 else ob)[pl.ds(t0, L), hs] = o
            ST[d, h] = st * jnp.exp(a_last) + lax.dot_general(ivh, kd.astype(BF16), _TN, preferred_element_type=F32)

    def body(i, carry):
        chunk(i, 0)
        chunk(NC - 1 - i, 1)
        return carry

    lax.fori_loop(0, NC, body, 0)

    def epilogue(r, carry):
        t0 = pl.multiple_of(r * 128, 128)
        o = of[pl.ds(t0, 128), :] + ob[pl.ds(t0, 128), :]
        gsil = _silu(g_ref[pl.ds(t0, 128), :])
        outs = []
        for h in range(HG_HEADS):
            oh = o[:, h * HG_DV:(h + 1) * HG_DV]
            outs.append(oh * lax.rsqrt(jnp.mean(oh * oh, -1, keepdims=True) + RMS_EPS))
        c_ref[pl.ds(t0, 128), :] = (jnp.concatenate(outs, -1) * gsil).astype(BF16)
        return carry

    lax.fori_loop(0, T // 128, epilogue, 0)
    if not latent:
        for d in range(2):
            for h in range(HG_HEADS):
                so_ref[d, h] = ST[d, h].T


def _hgrn(proj, lb_l, l, latent, S0=None, c_out=None, prev=None):
    T = DEC_SEQ if latent else SEQ
    B = DEC_BATCH if latent else BATCH
    rb0 = N_CTX // DEC_SEQ if latent else 0
    W = BRANCH_W
    full2 = lambda b: (0, 0)
    any_spec = pl.BlockSpec(memory_space=pl.ANY)
    col = lambda j: pl.BlockSpec((T, W), lambda b: (rb0 + b, j))
    s_spec = pl.BlockSpec((None, None, 2, HG_HEADS, HG_DK, HG_DV), lambda b: (b, l, 0, 0, 0, 0))
    in_specs = [col(7), col(8), col(9), col(10), col(11), pl.BlockSpec((1, W), full2), pl.BlockSpec((1, W), full2)]
    args = [proj] * 5 + [lb_l[0][None, :], lb_l[1][None, :]]
    c_shape = jax.ShapeDtypeStruct((N_TOK, W), BF16)
    c_spec = pl.BlockSpec((T, W), lambda b: (rb0 + b, 0))
    aliases = {}
    if latent:
        in_specs += [s_spec, any_spec]
        args += [S0, c_out]
        aliases = {8: 0}
        out_shape, out_specs = c_shape, c_spec
    else:
        out_shape = (c_shape, jax.ShapeDtypeStruct((B, DEPTH, 2, HG_HEADS, HG_DK, HG_DV), F32))
        out_specs = (c_spec, s_spec)
        if prev is not None:
            in_specs.append(any_spec)
            args.append(prev)
            aliases = {7: 1}
    scratch = [pltpu.VMEM((T, W), F32), pltpu.VMEM((T, W), F32), pltpu.VMEM((2, HG_HEADS, HG_DV, HG_DK), F32)]
    return pl.pallas_call(
        functools.partial(_hgrn_kernel, T=T, latent=latent),
        out_shape=out_shape, grid=(B,), in_specs=in_specs, out_specs=out_specs, scratch_shapes=scratch,
        input_output_aliases=aliases,
        compiler_params=_cparams("arbitrary"),
        name="hgrn_lat" if latent else "hgrn_ctx",
    )(*args)


def _merge_kernel(a_ref, b_ref, c_ref, ga_ref, gb_ref, gc_ref, xc_ref, xl_ref, g1_ref, sh2_ref, sc2_ref,
                  wb_ref, wo_ref, lng_ref, lnb_ref, wr_ref, x1_ref, h2_ref, gate_ref, gid_ref, *, tm):
    def br(v_ref, g_ref, k):
        return jax.nn.sigmoid(g_ref[...]) * jnp.dot(v_ref[...], wb_ref[k], preferred_element_type=F32)

    mix = br(a_ref, ga_ref, 0) + br(b_ref, gb_ref, 1) + br(c_ref, gc_ref, 2)
    y = jnp.dot(mix.astype(BF16), wo_ref[...], preferred_element_type=F32)
    x = _pair_read(pl.program_id(0), tm, xc_ref, xl_ref)
    x1 = _layer_norm(DEEPNORM_ALPHA * x + g1_ref[...] * y, lng_ref[...], lnb_ref[...])
    x1_ref[...] = x1
    h2 = x1 * (1.0 + sc2_ref[...]) + sh2_ref[...]
    h2_ref[...] = h2.astype(BF16)
    logits = jnp.dot(h2, wr_ref[...], preferred_element_type=F32, precision=HI)
    lane = lax.broadcasted_iota(jnp.int32, logits.shape, 1)
    neg = -jnp.inf
    lg = jnp.where(lane < N_GROUPS, logits, neg)
    mg = jnp.max(lg, -1, keepdims=True)
    g_sel = jnp.min(jnp.where(lg == mg, lane, 128), -1, keepdims=True)
    p_sel = 1.0 / jnp.sum(jnp.where(lane < N_GROUPS, jnp.exp(lg - mg), 0.0), -1, keepdims=True)
    lo = N_GROUPS + EXP_PER_GROUP * g_sel
    le = jnp.where((lane >= lo) & (lane < lo + EXP_PER_GROUP), logits, neg)
    v1 = jnp.max(le, -1, keepdims=True)
    i1 = jnp.min(jnp.where(le == v1, lane, 128), -1, keepdims=True)
    le2 = jnp.where(lane == i1, neg, le)
    v2 = jnp.max(le2, -1, keepdims=True)
    i2 = jnp.min(jnp.where(le2 == v2, lane, 128), -1, keepdims=True)
    e2 = jnp.exp(v2 - v1)
    w1 = p_sel / (1.0 + e2)
    w2 = p_sel * e2 / (1.0 + e2)
    gate_ref[...] = jnp.where(lane == i1 - lo, w1, jnp.where(lane == i2 - lo, w2, 0.0))
    gid_ref[...] = jnp.broadcast_to(g_sel, logits.shape)


def _merge(a, b, c, proj, xc, xl, mod, wb, wo, lng, lnb, wr, l):
    tm = 512
    tok = lambda i: (i, 0)
    ln_spec = pl.BlockSpec((None, None, 1, D_MODEL), lambda i: (l, 0, 0, 0))
    return pl.pallas_call(
        functools.partial(_merge_kernel, tm=tm),
        out_shape=(jax.ShapeDtypeStruct((N_TOK, D_MODEL), F32), jax.ShapeDtypeStruct((N_TOK, D_MODEL), BF16),
                   jax.ShapeDtypeStruct((N_TOK, 128), F32), jax.ShapeDtypeStruct((N_TOK, 128), jnp.int32)),
        grid=(N_TOK // tm,),
        in_specs=[pl.BlockSpec((tm, BRANCH_W), tok), pl.BlockSpec((tm, BRANCH_W), tok), pl.BlockSpec((tm, BRANCH_W), tok),
                  pl.BlockSpec((tm, D_MODEL), lambda i: (i, 6)), pl.BlockSpec((tm, D_MODEL), lambda i: (i, 7)),
                  pl.BlockSpec((tm, D_MODEL), lambda i: (i, 8)),
                  *_pair_specs(tm), _mod_spec(l, 2, tm), _mod_spec(l, 3, tm), _mod_spec(l, 4, tm),
                  pl.BlockSpec((None, 3, BRANCH_W, D_MODEL), lambda i: (l, 0, 0, 0)),
                  pl.BlockSpec((None, D_MODEL, D_MODEL), lambda i: (l, 0, 0)),
                  ln_spec, ln_spec,
                  pl.BlockSpec((None, D_MODEL, 128), lambda i: (l, 0, 0))],
        out_specs=(pl.BlockSpec((tm, D_MODEL), tok), pl.BlockSpec((tm, D_MODEL), tok),
                   pl.BlockSpec((tm, 128), tok), pl.BlockSpec((tm, 128), tok)),
        compiler_params=_cparams("arbitrary"),
        name="merge",
    )(a, b, c, proj, proj, proj, xc, xl, mod, mod, mod, wb, wo, lng, lnb, wr)


def _moe_up_kernel(gid_ref, x_ref, gate_ref, w1_ref, w3_ref, hid_ref, w1b, w3b):
    f = pl.program_id(0)
    t = pl.program_id(1)
    first = jnp.logical_or(t == 0, gid_ref[t] != gid_ref[jnp.maximum(t - 1, 0)])

    @pl.when(first)
    def _():
        w1b[...] = w1_ref[...].astype(BF16)
        w3b[...] = w3_ref[...].astype(BF16)

    x = x_ref[...]
    a = jnp.dot(x, w1b[...], preferred_element_type=F32)
    b = jnp.dot(x, w3b[...], preferred_element_type=F32)
    gate = gate_ref[...]
    lane = lax.broadcasted_iota(jnp.int32, gate.shape, 1)
    gcol = jnp.sum(jnp.where(lane == f, gate, 0.0), -1, keepdims=True)
    hid_ref[...] = (_silu(a) * b * gcol).astype(BF16)


def _moe_up(gid, xs, gates, w1, w3, l):
    tm = MOE_TM
    npad = MOE_NT * tm
    w_spec = pl.BlockSpec((None, None, D_MODEL, D_EXPERT), lambda f, t, g: (l, EXP_PER_GROUP * g[t] + f, 0, 0))
    grid_spec = pltpu.PrefetchScalarGridSpec(
        num_scalar_prefetch=1,
        grid=(EXP_PER_GROUP, MOE_NT),
        in_specs=[pl.BlockSpec((tm, D_MODEL), lambda f, t, g: (t, 0)),
                  pl.BlockSpec((tm, 128), lambda f, t, g: (t, 0)), w_spec, w_spec],
        out_specs=pl.BlockSpec((tm, D_EXPERT), lambda f, t, g: (t, f)),
        scratch_shapes=[pltpu.VMEM((D_MODEL, D_EXPERT), BF16), pltpu.VMEM((D_MODEL, D_EXPERT), BF16)],
    )
    return pl.pallas_call(
        _moe_up_kernel,
        out_shape=jax.ShapeDtypeStruct((npad, EXP_PER_GROUP * D_EXPERT), BF16),
        grid_spec=grid_spec,
        compiler_params=_cparams("arbitrary", "arbitrary"),
        name="moe_up",
    )(gid, xs, gates, w1, w3)


def _moe_down_kernel(gid_ref, hid_ref, w2_ref, y_ref, w2b):
    t = pl.program_id(0)
    first = jnp.logical_or(t == 0, gid_ref[t] != gid_ref[jnp.maximum(t - 1, 0)])

    @pl.when(first)
    def _():
        w2b[...] = w2_ref[...].astype(BF16)

    y_ref[...] = jnp.dot(hid_ref[...], w2b[...], preferred_element_type=F32)


def _moe_down(gid, hid, w2g, l):
    tm = MOE_TM
    npad = MOE_NT * tm
    hw = EXP_PER_GROUP * D_EXPERT
    grid_spec = pltpu.PrefetchScalarGridSpec(
        num_scalar_prefetch=1,
        grid=(MOE_NT,),
        in_specs=[pl.BlockSpec((tm, hw), lambda t, g: (t, 0)),
                  pl.BlockSpec((None, None, hw, D_MODEL), lambda t, g: (l, g[t], 0, 0))],
        out_specs=pl.BlockSpec((tm, D_MODEL), lambda t, g: (t, 0)),
        scratch_shapes=[pltpu.VMEM((hw, D_MODEL), BF16)],
    )
    return pl.pallas_call(
        _moe_down_kernel,
        out_shape=jax.ShapeDtypeStruct((npad, D_MODEL), F32),
        grid_spec=grid_spec,
        compiler_params=_cparams("arbitrary"),
        name="moe_down",
    )(gid, hid, w2g)


def _moe(h2, gate, gsel, w1, w3, w2g, l):
    tm = MOE_TM
    npad = MOE_NT * tm
    g = gsel[:, 0]
    onehot = (g[:, None] == jnp.arange(N_GROUPS)[None, :]).astype(jnp.int32)
    counts = jnp.sum(onehot, 0)
    rank = jnp.sum((jnp.cumsum(onehot, 0) - onehot) * onehot, 1)
    padded = (counts + tm - 1) // tm * tm
    ends = jnp.cumsum(padded)
    offs = ends - padded
    dest = offs[g] + rank
    src = jnp.full((npad,), -1, jnp.int32).at[dest].set(jnp.arange(N_TOK, dtype=jnp.int32), unique_indices=True)
    valid = (src >= 0).astype(F32)
    src = jnp.maximum(src, 0)
    starts = jnp.arange(MOE_NT, dtype=jnp.int32) * tm
    tile_gid = jnp.minimum(jnp.sum((ends[None, :] <= starts[:, None]).astype(jnp.int32), 1), N_GROUPS - 1)
    take = lambda arr, idx: arr.at[idx].get(mode="promise_in_bounds", unique_indices=False)
    xs = take(h2, src)
    gs = take(gate, src) * valid[:, None]
    hid = _moe_up(tile_gid, xs, gs, w1, w3, l)
    ys = _moe_down(tile_gid, hid, w2g, l)
    return take(ys, dest)


def _final_kernel(*refs, tm, with_h):
    x1_ref, y_ref, g2_ref, lng_ref, lnb_ref = refs[:5]
    x2 = _layer_norm(DEEPNORM_ALPHA * x1_ref[...] + g2_ref[...] * y_ref[...], lng_ref[...], lnb_ref[...])
    i = pl.program_id(0)
    if with_h:
        sh_ref, sc_ref, xc_ref, xl_ref, h_ref = refs[5:]
        h_ref[...] = (x2 * (1.0 + sc_ref[...]) + sh_ref[...]).astype(BF16)
    else:
        xc_ref, xl_ref = refs[5:]

    @pl.when(i < N_CTX // tm)
    def _():
        xc_ref[...] = x2

    @pl.when(i >= N_CTX // tm)
    def _():
        xl_ref[...] = x2


def _final(x1, y, mod, lng, lnb, l):
    tm = 1024
    tok = lambda i: (i, 0)
    with_h = l + 1 < DEPTH
    ln_spec = pl.BlockSpec((None, None, 1, D_MODEL), lambda i: (l, 1, 0, 0))
    half = jax.ShapeDtypeStruct((N_CTX, D_MODEL), F32)
    in_specs = [pl.BlockSpec((tm, D_MODEL), tok), pl.BlockSpec((tm, D_MODEL), tok), _mod_spec(l, 5, tm), ln_spec, ln_spec]
    args = [x1, y, mod, lng, lnb]
    out_shape = [half, half]
    out_specs = list(_pair_specs(tm))
    if with_h:
        in_specs += [_mod_spec(l + 1, 0, tm), _mod_spec(l + 1, 1, tm)]
        args += [mod, mod]
        out_shape.append(jax.ShapeDtypeStruct((N_TOK, D_MODEL), BF16))
        out_specs.append(pl.BlockSpec((tm, D_MODEL), tok))
    return pl.pallas_call(
        functools.partial(_final_kernel, tm=tm, with_h=with_h),
        out_shape=tuple(out_shape), grid=(N_TOK // tm,), in_specs=in_specs, out_specs=tuple(out_specs),
        compiler_params=_cparams("arbitrary"),
        name="final",
    )(*args)


def kernel(x_prompt, x_sample, c, cache_na_k, cache_na_v, state_mlstm_C, state_mlstm_n, state_mlstm_m, state_hgrn,
           c_ctx, w_mod, b_mod, w_in, b_in, mlstm_fbias, hgrn_lb_logits, na_rpb, w_branch, w_out, ln_g, ln_b,
           w_rg, w_re, w_e1, w_e3, w_e2):
    assert N_CTX == N_LAT
    lb_cum = jnp.cumsum(jax.nn.softmax(hgrn_lb_logits.astype(F32), axis=1), axis=1)
    lb_all = lb_cum - lb_cum[:, :1]

    cs = jnp.zeros((N_MODROWS, D_MODEL), F32).at[0].set(c_ctx).at[1:1 + DEC_BATCH].set(c)
    mod = _modulation(cs, w_mod, b_mod).reshape(DEPTH, N_MODROWS, 6, 1, D_MODEL)

    wb = w_branch.astype(BF16)
    wo = w_out.astype(BF16)
    lng = ln_g.reshape(DEPTH, 2, 1, D_MODEL)
    lnb = ln_b.reshape(DEPTH, 2, 1, D_MODEL)
    wr = jnp.concatenate([w_rg, w_re, jnp.zeros((DEPTH, D_MODEL, 128 - N_GROUPS - N_EXPERTS), F32)], -1)
    w2g = w_e2.reshape(DEPTH, N_GROUPS, EXP_PER_GROUP * D_EXPERT, D_MODEL)
    b_main = jnp.concatenate([b_in[:, :GATE_COL0], b_in[:, GATE_COL0 + N_GATES:]], 1)

    xc = x_prompt.reshape(N_CTX, D_MODEL)
    xl = x_sample.reshape(N_LAT, D_MODEL)
    w_t = jnp.swapaxes(w_in, 1, 2)
    h = _prep(xc, xl, mod)

    kv = (None, None)
    ma_states = None
    hg_state = None
    for l in range(DEPTH):
        proj = _inproj(h, w_t, b_main[l][None, :], l)
        gcol, gt = _gates(h, w_t, b_in.reshape(DEPTH, 1, N_IN), l)
        gt3 = gt.reshape(N_GATES, N_TOK // MA_CHUNK, MA_CHUNK).transpose(1, 0, 2)

        a, *ma_states = _mlstm(proj, gcol, gt3, mlstm_fbias[l], l, False, prev=ma_states)
        a = _mlstm(proj, gcol, gt3, mlstm_fbias[l], l, True, state_mlstm_C, state_mlstm_n, state_mlstm_m, a_out=a)
        b, *kv = _ctx_attention(proj, l, *kv)
        b = _lat_attention(proj, cache_na_k, cache_na_v, _na_bias_table(na_rpb[l]), l, b)
        cc, hg_state = _hgrn(proj, lb_all[:, l], l, False, prev=hg_state)
        cc = _hgrn(proj, lb_all[:, l], l, True, state_hgrn, c_out=cc)

        x1, h2, gate, gsel = _merge(a, b, cc, proj, xc, xl, mod, wb, wo, lng, lnb, wr, l)
        y2 = _moe(h2, gate, gsel, w_e1, w_e3, w2g, l)
        outs = _final(x1, y2, mod, lng, lnb, l)
        xc, xl = outs[0], outs[1]
        if l + 1 < DEPTH:
            h = outs[2]

    dt = x_prompt.dtype
    new_C, new_n, new_m = ma_states
    new_n = new_n.reshape(BATCH, DEPTH, 2, MA_HEADS, MA_DK)
    new_m = new_m[:, :, :, 0].reshape(BATCH, DEPTH, 2, MA_HEADS)
    return (xc.reshape(BATCH, SEQ, D_MODEL), xl.reshape(DEC_BATCH, DEC_SEQ, D_MODEL), kv[0], kv[1],
            new_C.astype(dt), new_n.astype(dt), new_m.astype(dt), hg_state.astype(dt))
```

```python
import functools

import numpy as np
import jax
import jax.numpy as jnp
from jax import lax
from jax.experimental import pallas as pl
from jax.experimental.pallas import tpu as pltpu

F32 = jnp.float32
BF16 = jnp.bfloat16
HI = lax.Precision.HIGHEST

D_MODEL = 1024
BATCH = 16
SEQ = 256
DEPTH = 2
DEC_BATCH = 4
DEC_SEQ = 1024
PAST_LEN = 256
GRID_W = 64
MA_HEADS = 4
MA_DK = 128
MA_DV = 128
MA_CHUNK = 64
NA_HEADS = 8
NA_DH = 64
NA_KR_MAX = 8
NA_KC = 16
HG_HEADS = 4
HG_DK = 128
HG_DV = 128
HG_CHUNK = 32
BRANCH_W = 512
N_GROUPS = 4
EXP_PER_GROUP = 4
N_EXPERTS = N_GROUPS * EXP_PER_GROUP
D_EXPERT = 512
ROPE_BASE = 10000.0
LN_EPS = 1e-5
RMS_EPS = 1e-6
DEEPNORM_ALPHA = (2 * DEPTH) ** 0.25

N_CTX = BATCH * SEQ
N_LAT = DEC_BATCH * DEC_SEQ
N_TOK = N_CTX + N_LAT
N_MODROWS = 8
GATE_COL0 = 4 * BRANCH_W
N_GATES = 4 * MA_HEADS
N_IN = 9232
P_COLS = N_IN - N_GATES
MOE_TM = 512
MOE_NT = N_TOK // MOE_TM + N_GROUPS
VMEM_LIMIT = 48 * 1024 * 1024

_NT = (((1,), (1,)), ((), ()))
_TN = (((0,), (0,)), ((), ()))


def _cparams(*sem):
    return pltpu.CompilerParams(dimension_semantics=sem, vmem_limit_bytes=VMEM_LIMIT)


def _mod_row(tile, tm):
    return jnp.maximum((tile * tm) // DEC_SEQ - (N_CTX // DEC_SEQ - 1), 0)


def _mod_spec(l, part, tm):
    return pl.BlockSpec((None, None, None, 1, D_MODEL), lambda i: (l, _mod_row(i, tm), part, 0, 0))


def _pair_specs(tm):
    nc = N_CTX // tm
    return (pl.BlockSpec((tm, D_MODEL), lambda i: (jnp.minimum(i, nc - 1), 0)),
            pl.BlockSpec((tm, D_MODEL), lambda i: (jnp.maximum(i - nc, 0), 0)))


def _pair_read(i, tm, c_ref, l_ref):
    return jnp.where(i < N_CTX // tm, c_ref[...], l_ref[...])


def _silu(x):
    return x * jax.nn.sigmoid(x)


def _layer_norm(x, g, b):
    mu = jnp.mean(x, -1, keepdims=True)
    xc = x - mu
    var = jnp.mean(xc * xc, -1, keepdims=True)
    return xc * lax.rsqrt(var + LN_EPS) * g + b


def _log_sigmoid(x):
    return jnp.minimum(x, 0.0) - jnp.log(1.0 + jnp.exp(-jnp.abs(x)))


def _tri(n, upper):
    r = lax.broadcasted_iota(jnp.int32, (n, n), 0)
    c = lax.broadcasted_iota(jnp.int32, (n, n), 1)
    return jnp.where((r <= c) if upper else (r >= c), 1.0, 0.0).astype(F32)


def _mod_kernel(c_ref, w_ref, b_ref, o_ref):
    s = _silu(c_ref[...])
    o_ref[...] = jnp.dot(s.astype(BF16), w_ref[...].astype(BF16), preferred_element_type=F32) + b_ref[...]


def _modulation(cs, w_mod, b_mod):
    tn = 1024
    return pl.pallas_call(
        _mod_kernel,
        out_shape=jax.ShapeDtypeStruct((DEPTH, N_MODROWS, 6 * D_MODEL), F32),
        grid=(DEPTH, 6 * D_MODEL // tn),
        in_specs=[pl.BlockSpec((N_MODROWS, D_MODEL), lambda l, j: (0, 0)),
                  pl.BlockSpec((None, D_MODEL, tn), lambda l, j: (l, 0, j)),
                  pl.BlockSpec((None, 1, tn), lambda l, j: (l, 0, j))],
        out_specs=pl.BlockSpec((None, N_MODROWS, tn), lambda l, j: (l, 0, j)),
        compiler_params=_cparams("arbitrary", "arbitrary"),
        name="modulation",
    )(cs, w_mod, b_mod.reshape(DEPTH, 1, 6 * D_MODEL))


def _prep_kernel(xc_ref, xl_ref, sh_ref, sc_ref, h_ref, *, tm):
    x = _pair_read(pl.program_id(0), tm, xc_ref, xl_ref)
    h_ref[...] = (x * (1.0 + sc_ref[...]) + sh_ref[...]).astype(BF16)


def _prep(xc, xl, mod):
    tm = 1024
    return pl.pallas_call(
        functools.partial(_prep_kernel, tm=tm),
        out_shape=jax.ShapeDtypeStruct((N_TOK, D_MODEL), BF16),
        grid=(N_TOK // tm,),
        in_specs=[*_pair_specs(tm), _mod_spec(0, 0, tm), _mod_spec(0, 1, tm)],
        out_specs=pl.BlockSpec((tm, D_MODEL), lambda i: (i, 0)),
        compiler_params=_cparams("arbitrary"),
        name="prep",
    )(xc, xl, mod, mod)


INPROJ_TN = 512
N_PLAIN_TILES = GATE_COL0 // INPROJ_TN


F32_TILE0 = 7
N_F32_TILES = 2
P16_COLS = P_COLS - N_F32_TILES * INPROJ_TN


def _inproj_kernel(h_ref, wa_ref, wb_ref, b_ref, o16_ref, o32_ref):
    j = pl.program_id(1)
    is_f32 = jnp.logical_and(j >= F32_TILE0, j < F32_TILE0 + N_F32_TILES)

    def shifted():
        w = jnp.concatenate([wa_ref[N_GATES:, :], wb_ref[...]], 0)
        return lax.dot_general(h_ref[...], w.astype(BF16), _NT, preferred_element_type=F32) + b_ref[...]

    @pl.when(j < N_PLAIN_TILES)
    def _():
        o16_ref[...] = (lax.dot_general(h_ref[...], wa_ref[...].astype(BF16), _NT, preferred_element_type=F32)
                        + b_ref[...]).astype(BF16)

    @pl.when(jnp.logical_and(j >= N_PLAIN_TILES, jnp.logical_not(is_f32)))
    def _():
        o16_ref[...] = shifted().astype(BF16)

    @pl.when(is_f32)
    def _():
        o32_ref[...] = shifted()


def _inproj(h, w_t, b_main, l):
    tm, tn = 2048, INPROJ_TN
    j16 = lambda j: jnp.where(j < F32_TILE0, j, jnp.maximum(j - N_F32_TILES, F32_TILE0 - 1))
    j32 = lambda j: jnp.clip(j - F32_TILE0, 0, N_F32_TILES - 1)
    return pl.pallas_call(
        _inproj_kernel,
        out_shape=(jax.ShapeDtypeStruct((N_TOK, P16_COLS), BF16),
                   jax.ShapeDtypeStruct((N_TOK, N_F32_TILES * tn), F32)),
        grid=(N_TOK // tm, P_COLS // tn),
        in_specs=[pl.BlockSpec((tm, D_MODEL), lambda i, j: (i, 0)),
                  pl.BlockSpec((None, tn, D_MODEL), lambda i, j: (l, j, 0)),
                  pl.BlockSpec((None, N_GATES, D_MODEL), lambda i, j: (l, (j + 1) * (tn // N_GATES), 0)),
                  pl.BlockSpec((1, tn), lambda i, j: (0, j))],
        out_specs=(pl.BlockSpec((tm, tn), lambda i, j: (i, j16(j))), pl.BlockSpec((tm, tn), lambda i, j: (i, j32(j)))),
        compiler_params=_cparams("arbitrary", "arbitrary"),
        name="inproj",
    )(h, w_t, w_t, b_main)


def _gates_kernel(h_ref, w_ref, b_ref, gc_ref, gt_ref):
    g = lax.dot_general(h_ref[...], w_ref[...].astype(BF16), _NT, preferred_element_type=F32) + b_ref[...]
    gc_ref[...] = g
    gt_ref[...] = g.T[:N_GATES]


def _gates(h, w_t, b_in3, l):
    tm = 1024
    gblk = GATE_COL0 // 128
    return pl.pallas_call(
        _gates_kernel,
        out_shape=(jax.ShapeDtypeStruct((N_TOK, 128), F32), jax.ShapeDtypeStruct((N_GATES, N_TOK), F32)),
        grid=(N_TOK // tm,),
        in_specs=[pl.BlockSpec((tm, D_MODEL), lambda i: (i, 0)),
                  pl.BlockSpec((None, 128, D_MODEL), lambda i: (l, gblk, 0)),
                  pl.BlockSpec((None, 1, 128), lambda i: (l, 0, gblk))],
        out_specs=(pl.BlockSpec((tm, 128), lambda i: (i, 0)), pl.BlockSpec((N_GATES, tm), lambda i: (0, i))),
        compiler_params=_cparams("arbitrary"),
        name="gates",
    )(h, w_t, b_in3)


HEADS_PER_BLK = 128 // NA_DH
NA_NBLK = NA_HEADS // HEADS_PER_BLK
NA_QSCALE = NA_DH ** -0.5
Q_COL, K_COL, V_COL = 16, 20, 24
QKV_COL = 4


def _ctx_attn_kernel(*refs):
    q_ref, k_ref, v_ref = refs[:3]
    o_ref, ko_ref, vo_ref = refs[-3:]
    heads = range(NA_HEADS)
    split = lambda x: jnp.stack([x[:, h * NA_DH:(h + 1) * NA_DH] for h in heads], 0)
    q = split(q_ref[...] * NA_QSCALE)
    k = split(k_ref[...])
    v = split(v_ref[...])
    ko_ref[...] = k.astype(F32)
    vo_ref[...] = v.astype(F32)
    s = lax.dot_general(q, k, (((2,), (2,)), ((0,), (0,))), preferred_element_type=F32)
    e = jnp.exp(s - jnp.max(s, -1, keepdims=True))
    p = e * (1.0 / jnp.sum(e, -1, keepdims=True))
    o = lax.dot_general(p.astype(BF16), v, (((2,), (1,)), ((0,), (0,))), preferred_element_type=F32)
    o_ref[...] = jnp.concatenate([o[h] for h in heads], -1).astype(BF16)


def _ctx_attention(p16, l, prev_k=None, prev_v=None):
    kv_shape = jax.ShapeDtypeStruct((BATCH, DEPTH, NA_HEADS, SEQ, NA_DH), F32)
    kv_spec = pl.BlockSpec((None, None, NA_HEADS, SEQ, NA_DH), lambda b: (b, l, 0, 0, 0))
    col = lambda j: pl.BlockSpec((SEQ, BRANCH_W), lambda b: (b, j))
    in_specs = [col(QKV_COL), col(QKV_COL + 1), col(QKV_COL + 2)]
    args = [p16, p16, p16]
    aliases = {}
    if prev_k is not None:
        in_specs += [pl.BlockSpec(memory_space=pl.ANY)] * 2
        args += [prev_k, prev_v]
        aliases = {3: 1, 4: 2}
    return pl.pallas_call(
        _ctx_attn_kernel,
        out_shape=(jax.ShapeDtypeStruct((N_TOK, BRANCH_W), BF16), kv_shape, kv_shape),
        grid=(BATCH,),
        in_specs=in_specs,
        out_specs=(pl.BlockSpec((SEQ, BRANCH_W), lambda b: (b, 0)), kv_spec, kv_spec),
        input_output_aliases=aliases,
        compiler_params=_cparams("arbitrary"),
        name="ctx_attention",
    )(*args)


NA_ROWS = DEC_SEQ // GRID_W
NA_KR = min(NA_KR_MAX, NA_ROWS)
NA_QROWS = 4
NA_QT = NA_ROWS // NA_QROWS
NA_WROWS = NA_KR + NA_QROWS - 1
NA_WKEYS = NA_WROWS * GRID_W


def _na_window_start(t):
    return min(max(t * NA_QROWS - NA_KR // 2, 0), NA_ROWS - NA_WROWS)


def _na_bias_table(rpb):
    c = np.arange(GRID_W)
    c0 = np.clip(c - NA_KC // 2, 0, GRID_W - NA_KC)
    kc = np.arange(GRID_W)
    valid = (kc[None, :] >= c0[:, None]) & (kc[None, :] < c0[:, None] + NA_KC)
    dc = kc[None, :] - c[:, None] + NA_KC - 1
    onehot = (dc[None] == np.arange(2 * NA_KC - 1)[:, None, None]) & valid[None]
    toep = jnp.einsum('hrd,dcx->hrcx', rpb.astype(F32), jnp.asarray(onehot, F32), precision=HI)
    toep = jnp.where(valid[None, None], toep, -jnp.inf)
    ninf = jnp.full((NA_HEADS, GRID_W, GRID_W), -jnp.inf, F32)
    tiles = []
    for t in range(NA_QT):
        w0 = _na_window_start(t)
        qrows = []
        for r in range(t * NA_QROWS, (t + 1) * NA_QROWS):
            r0 = min(max(r - NA_KR // 2, 0), NA_ROWS - NA_KR)
            blocks = []
            for kr in range(w0, w0 + NA_WROWS):
                inside = r0 <= kr < r0 + NA_KR
                blocks.append(toep[:, kr - r + NA_KR_MAX - 1] if inside else ninf)
            qrows.append(jnp.concatenate(blocks, -1))
        tiles.append(jnp.concatenate(qrows, 1))
    return jnp.stack(tiles, 1)


def _lat_attn_kernel(q_ref, k_ref, v_ref, ck_ref, cv_ref, bias_ref, prev_ref, o_ref):
    q = (q_ref[...] * NA_QSCALE).astype(BF16)
    k = k_ref[...].astype(BF16)
    v = v_ref[...].astype(BF16)
    nq = NA_QROWS * GRID_W
    for t in range(NA_QT):
        w0 = _na_window_start(t)
        qs = slice(t * nq, (t + 1) * nq)
        ws = slice(w0 * GRID_W, (w0 + NA_WROWS) * GRID_W)
        outs = []
        for hh in range(HEADS_PER_BLK):
            sl = slice(hh * NA_DH, (hh + 1) * NA_DH)
            qh = q[qs, sl]
            s_loc = lax.dot_general(qh, k[ws, sl], _NT, preferred_element_type=F32) + bias_ref[hh, t]
            s_ctx = lax.dot_general(qh, ck_ref[hh].astype(BF16), _NT, preferred_element_type=F32)
            m = jnp.maximum(jnp.max(s_loc, -1, keepdims=True), jnp.max(s_ctx, -1, keepdims=True))
            e_loc = jnp.exp(s_loc - m)
            e_ctx = jnp.exp(s_ctx - m)
            inv = 1.0 / (jnp.sum(e_loc, -1, keepdims=True) + jnp.sum(e_ctx, -1, keepdims=True))
            acc = (jnp.dot(e_loc.astype(BF16), v[ws, sl], preferred_element_type=F32)
                   + jnp.dot(e_ctx.astype(BF16), cv_ref[hh].astype(BF16), preferred_element_type=F32))
            outs.append(acc * inv)
        o_ref[qs, :] = jnp.concatenate(outs, -1).astype(BF16)


def _lat_attention(proj, ck, cv, bias, l, b_out):
    rb0 = N_CTX // DEC_SEQ
    cb = lambda base: (lambda j, b: (rb0 + b, base + j))
    c_spec = pl.BlockSpec((None, None, HEADS_PER_BLK, PAST_LEN, NA_DH), lambda j, b: (b, l, j, 0, 0))
    return pl.pallas_call(
        _lat_attn_kernel,
        out_shape=jax.ShapeDtypeStruct((N_TOK, BRANCH_W), BF16),
        grid=(NA_NBLK, DEC_BATCH),
        in_specs=[pl.BlockSpec((DEC_SEQ, 128), cb(Q_COL)), pl.BlockSpec((DEC_SEQ, 128), cb(K_COL)),
                  pl.BlockSpec((DEC_SEQ, 128), cb(V_COL)), c_spec, c_spec,
                  pl.BlockSpec((HEADS_PER_BLK, NA_QT, NA_QROWS * GRID_W, NA_WKEYS), lambda j, b: (j, 0, 0, 0)),
                  pl.BlockSpec(memory_space=pl.ANY)],
        out_specs=pl.BlockSpec((DEC_SEQ, 128), lambda j, b: (rb0 + b, j)),
        input_output_aliases={6: 0},
        compiler_params=_cparams("arbitrary", "arbitrary"),
        name="lat_attention",
    )(proj, proj, proj, ck, cv, bias, b_out)


MA_KSCALE = MA_DK ** -0.5


def _rope_tables(T):
    t = np.arange(T)
    half = MA_DK // 2
    inv = ROPE_BASE ** (-jnp.arange(0, half, 2, dtype=F32) / half)
    ang_r = jnp.asarray((t // GRID_W).astype(np.float32))[:, None] * inv[None, :]
    ang_c = jnp.asarray((t % GRID_W).astype(np.float32))[:, None] * inv[None, :]
    cos = jnp.concatenate([jnp.cos(ang_r)] * 2 + [jnp.cos(ang_c)] * 2, -1)
    sin = jnp.concatenate([-jnp.sin(ang_r), jnp.sin(ang_r), -jnp.sin(ang_c), jnp.sin(ang_c)], -1)
    return cos, sin


def _mlstm_kernel(*refs, T, latent):
    if latent:
        (p_ref, gc_ref, gt_ref, fbc_ref, fbr_ref, cos_ref, sin_ref, c0_ref, n0_ref, m0_ref, prev_ref,
         a_ref, qs, ks, vT, hfT, hbT, CT, ns, ms, brs, kcs) = refs
    else:
        p_ref, gc_ref, gt_ref, fbc_ref, fbr_ref = refs[:5]
        a_ref, co_ref, no_ref, mo_ref, qs, ks, vT, hfT, hbT, CT, ns, ms, brs, kcs = refs[-14:]
    L = MA_CHUNK
    NC = T // L
    W = BRANCH_W
    PER = 128 // L

    lane = lax.broadcasted_iota(jnp.int32, (T, MA_DK), 1)
    lo_half = (lane % (MA_DK // 2)) < (MA_DK // 4)

    def rope(x):
        if not latent:
            return x
        swapped = jnp.where(lo_half, pltpu.roll(x, MA_DK - MA_DK // 4, 1), pltpu.roll(x, MA_DK // 4, 1))
        return x * cos_ref[...] + swapped * sin_ref[...]

    for h in range(MA_HEADS):
        hs = slice(h * MA_DK, (h + 1) * MA_DK)
        qs[:, hs] = rope(p_ref[:, hs].astype(F32)).astype(BF16)
        ks[:, hs] = rope(p_ref[:, W + h * MA_DK:W + (h + 1) * MA_DK].astype(F32) * MA_KSCALE).astype(BF16)

    def v_block(tb, carry):
        r0 = pl.multiple_of(tb * 128, 128)
        for h in range(MA_HEADS):
            hs = slice(h * MA_DV, (h + 1) * MA_DV)
            blk = p_ref[pl.ds(r0, 128), 2 * W + h * MA_DV:2 * W + (h + 1) * MA_DV].astype(F32).T.astype(BF16)
            for j in range(PER):
                vT[tb * PER + j, h] = blk[:, j * L:(j + 1) * L]
        return carry

    lax.fori_loop(0, T // 128, v_block, 0)

    for d in range(2):
        for h in range(MA_HEADS):
            sidx = d * MA_HEADS + h
            CT[d, h] = c0_ref[d, h].T if latent else jnp.zeros((MA_DV, MA_DK), F32)
            ns[d, h] = n0_ref[sidx:sidx + 1, :] if latent else jnp.zeros((1, MA_DK), F32)
            ms[d, h] = m0_ref[sidx:sidx + 1, :] if latent else jnp.zeros((1, 128), F32)

    low = _tri(L, False)
    upp = _tri(L, True)
    rr = lax.broadcasted_iota(jnp.int32, (L, L), 0)
    cc = lax.broadcasted_iota(jnp.int32, (L, L), 1)
    fbc = fbc_ref[...]
    fbr = fbr_ref[...]

    def gate_sums(c, carry):
        t0 = pl.multiple_of(c * L, L)
        gc = gc_ref[pl.ds(t0, L), :]
        lfc = _log_sigmoid(gc + fbc)
        lfr = _log_sigmoid(gt_ref[c] + fbr)
        ish = pltpu.roll(gc, MA_HEADS, 1)
        brs[0, c] = jnp.dot(lfr, upp, precision=HI, preferred_element_type=F32)
        brs[1, c] = jnp.dot(lfr, low, precision=HI, preferred_element_type=F32)
        kcs[0, pl.ds(t0, L), :] = ish - jnp.dot(low, lfc, precision=HI, preferred_element_type=F32)
        kcs[1, pl.ds(t0, L), :] = ish - jnp.dot(upp, lfc, precision=HI, preferred_element_type=F32)
        return carry

    lax.fori_loop(0, NC, gate_sums, 0)

    def chunk(c, d):
        t0 = pl.multiple_of(c * L, L)
        gt = gt_ref[c]
        brow = brs[d, c]
        kc = kcs[d, pl.ds(t0, L), :]
        mask = (rr <= cc) if d == 0 else (rr >= cc)
        last = L - 1 if d == 0 else 0
        heads = range(MA_HEADS)
        gi0 = 2 * d * MA_HEADS
        gf0 = gi0 + MA_HEADS
        br = jnp.stack([brow[gf0 + h:gf0 + h + 1, :] for h in heads], 0)
        ir = jnp.stack([gt[gi0 + h:gi0 + h + 1, :] for h in heads], 0)
        kcol = jnp.stack([kc[:, gf0 + h:gf0 + h + 1] for h in heads], 0)
        m = ms[d][:, :, 0:1]
        n = ns[d]
        ct = CT[d]
        qc = qs[pl.ds(t0, L), :]
        kc = ks[pl.ds(t0, L), :]
        q = jnp.stack([qc[:, h * MA_DK:(h + 1) * MA_DK] for h in heads], 0)
        k = jnp.stack([kc[:, h * MA_DK:(h + 1) * MA_DK] for h in heads], 0)
        vt = vT[c]
        bnt = (((2,), (2,)), ((0,), (0,)))
        bnn = (((2,), (1,)), ((0,), (0,)))
        dmat = jnp.where(mask, br + kcol, -jnp.inf)
        g = br + m
        m_t = jnp.maximum(g, jnp.max(dmat, 1, keepdims=True))
        w_inter = jnp.exp(g - m_t)
        s = lax.dot_general(k, q, bnt, preferred_element_type=F32) * jnp.exp(dmat - m_t)
        ctn = jnp.concatenate([ct.astype(BF16), jnp.broadcast_to(n, (MA_HEADS, 8, MA_DK)).astype(BF16)], 1)
        cq = lax.dot_general(ctn, q, bnt, preferred_element_type=F32)
        num = w_inter * cq[:, :MA_DV] + lax.dot_general(vt, s.astype(BF16), bnn, preferred_element_type=F32)
        den = w_inter * cq[:, MA_DV:MA_DV + 1] + jnp.sum(s, 1, keepdims=True)
        (hfT if d == 0 else hbT)[c] = num / jnp.maximum(jnp.abs(den), jnp.exp(-m_t))
        m_new = m_t[:, :, last:last + 1]
        b_last = br[:, :, last:last + 1]
        decay = jnp.exp(b_last + m - m_new)
        wk = jnp.exp(b_last - br + ir - m_new)
        wk_hi = wk.astype(BF16)
        wk_lo = (wk - wk_hi.astype(F32)).astype(BF16)
        lhs = jnp.concatenate([(vt.astype(F32) * wk).astype(BF16), wk_hi, wk_lo,
                               jnp.zeros((MA_HEADS, 6, L), BF16)], 1)
        upd = lax.dot_general(lhs, k, bnn, preferred_element_type=F32)
        CT[d] = decay * ct + upd[:, :MA_DV]
        ns[d] = decay * n + upd[:, MA_DV:MA_DV + 1] + upd[:, MA_DV + 1:MA_DV + 2]
        ms[d] = jnp.broadcast_to(m_new, (MA_HEADS, 1, 128))

    def body(i, carry):
        chunk(i, 0)
        chunk(NC - 1 - i, 1)
        return carry

    lax.fori_loop(0, NC, body, 0)

    def out_block(tb, carry):
        r0 = pl.multiple_of(tb * 128, 128)
        hsum = jnp.concatenate([hfT[tb * PER + j] + hbT[tb * PER + j] for j in range(PER)], 2)
        outs = [hsum[h].T for h in range(MA_HEADS)]
        gate = jax.nn.sigmoid(p_ref[pl.ds(r0, 128), 3 * W:4 * W].astype(F32))
        a_ref[pl.ds(r0, 128), :] = (gate * jnp.concatenate(outs, 1)).astype(BF16)
        return carry

    lax.fori_loop(0, T // 128, out_block, 0)
    if not latent:
        for d in range(2):
            for h in range(MA_HEADS):
                sidx = d * MA_HEADS + h
                co_ref[d, h] = CT[d, h].T
                no_ref[sidx:sidx + 1, :] = ns[d, h]
                mo_ref[sidx:sidx + 1, :] = ms[d, h]


def _mlstm(proj, gcol, gt3, fbias_l, l, latent, C0=None, n0=None, m0=None, a_out=None, prev=None):
    T = DEC_SEQ if latent else SEQ
    B = DEC_BATCH if latent else BATCH
    rb0 = N_CTX // DEC_SEQ if latent else 0
    fb = fbias_l.astype(F32)
    fbc = jnp.zeros((1, 128), F32).at[0, MA_HEADS:2 * MA_HEADS].set(fb[0]).at[0, 3 * MA_HEADS:4 * MA_HEADS].set(fb[1])
    fbr = fbc[0, :N_GATES].reshape(N_GATES, 1)
    full2 = lambda b: (0, 0)
    any_spec = pl.BlockSpec(memory_space=pl.ANY)
    in_specs = [pl.BlockSpec((T, 4 * BRANCH_W), lambda b: (rb0 + b, 0)),
                pl.BlockSpec((T, 128), lambda b: (rb0 + b, 0)),
                pl.BlockSpec((T // MA_CHUNK, N_GATES, MA_CHUNK), lambda b: (rb0 + b, 0, 0)),
                pl.BlockSpec((1, 128), full2), pl.BlockSpec((N_GATES, 1), full2)]
    args = [proj, gcol, gt3, fbc, fbr]
    a_shape = jax.ShapeDtypeStruct((N_TOK, BRANCH_W), BF16)
    a_spec = pl.BlockSpec((T, BRANCH_W), lambda b: (rb0 + b, 0))
    c_spec = pl.BlockSpec((None, None, 2, MA_HEADS, MA_DK, MA_DV), lambda b: (b, l, 0, 0, 0, 0))
    nm_spec = pl.BlockSpec((None, None, 2 * MA_HEADS, 128), lambda b: (b, l, 0, 0))
    aliases = {}
    if latent:
        cos, sin = _rope_tables(T)
        nb = 2 * MA_HEADS
        in_specs += [pl.BlockSpec((T, MA_DK), full2), pl.BlockSpec((T, MA_DK), full2), c_spec, nm_spec, nm_spec, any_spec]
        args += [cos, sin, C0, n0.reshape(B, DEPTH, nb, MA_DK),
                 jnp.broadcast_to(m0.reshape(B, DEPTH, nb, 1), (B, DEPTH, nb, 128)), a_out]
        aliases = {len(args) - 1: 0}
        out_shape, out_specs = a_shape, a_spec
    else:
        nm_shape = jax.ShapeDtypeStruct((B, DEPTH, 2 * MA_HEADS, 128), F32)
        out_shape = (a_shape, jax.ShapeDtypeStruct((B, DEPTH, 2, MA_HEADS, MA_DK, MA_DV), F32), nm_shape, nm_shape)
        out_specs = (a_spec, c_spec, nm_spec, nm_spec)
        if prev is not None:
            in_specs += [any_spec] * 3
            args += list(prev)
            aliases = {len(args) - 3: 1, len(args) - 2: 2, len(args) - 1: 3}
    nc = T // MA_CHUNK
    scratch = [pltpu.VMEM((T, BRANCH_W), BF16), pltpu.VMEM((T, BRANCH_W), BF16),
               pltpu.VMEM((nc, MA_HEADS, MA_DV, MA_CHUNK), BF16),
               pltpu.VMEM((nc, MA_HEADS, MA_DV, MA_CHUNK), F32), pltpu.VMEM((nc, MA_HEADS, MA_DV, MA_CHUNK), F32),
               pltpu.VMEM((2, MA_HEADS, MA_DV, MA_DK), F32), pltpu.VMEM((2, MA_HEADS, 1, MA_DK), F32),
               pltpu.VMEM((2, MA_HEADS, 1, 128), F32),
               pltpu.VMEM((2, nc, N_GATES, MA_CHUNK), F32), pltpu.VMEM((2, T, 128), F32)]
    return pl.pallas_call(
        functools.partial(_mlstm_kernel, T=T, latent=latent),
        out_shape=out_shape, grid=(B,), in_specs=in_specs, out_specs=out_specs, scratch_shapes=scratch,
        input_output_aliases=aliases,
        compiler_params=_cparams("arbitrary"),
        name="mlstm_lat" if latent else "mlstm_ctx",
    )(*args)


HG_SUB = 8


def _hgrn_kernel(*refs, T, latent):
    ff_ref, fb_ref, q_ref, i_ref, g_ref, lbf_ref, lbb_ref = refs[:7]
    if latent:
        s0_ref = refs[7]
        c_ref, of, ob, ST, iT, As, Ks = refs[-7:]
    else:
        c_ref, so_ref, of, ob, ST, iT, As, Ks = refs[-8:]
    L = HG_CHUNK
    NC = T // L
    NB = L // HG_SUB
    DK = HG_DK

    for d in range(2):
        for h in range(HG_HEADS):
            ST[d, h] = s0_ref[d, h].T if latent else jnp.zeros((HG_DV, DK), F32)

    PER = 128 // L

    def i_block(tb, carry):
        r0 = pl.multiple_of(tb * 128, 128)
        for h in range(HG_HEADS):
            blk = i_ref[pl.ds(r0, 128), h * HG_DV:(h + 1) * HG_DV].astype(F32).T.astype(BF16)
            for j in range(PER):
                iT[tb * PER + j, h] = blk[:, j * L:(j + 1) * L]
        return carry

    lax.fori_loop(0, T // 128, i_block, 0)

    low = _tri(L, False)
    upp = _tri(L, True)
    row8 = lax.broadcasted_iota(jnp.int32, (HG_SUB, L), 0)
    lane_s = lax.broadcasted_iota(jnp.int32, (HG_SUB, L), 1)
    heads = range(HG_HEADS)
    bnt = (((2,), (2,)), ((0,), (0,)))
    bnn = (((2,), (1,)), ((0,), (0,)))
    LOG2E = 1.4426950408889634

    def split(x):
        return jnp.stack([x[:, h * DK:(h + 1) * DK] for h in heads], 0)

    def decay_sums(c, carry):
        t0 = pl.multiple_of(c * L, L)
        for d in range(2):
            fpre = (ff_ref if d == 0 else fb_ref)[pl.ds(t0, L), :]
            lb = (lbf_ref if d == 0 else lbb_ref)[...]
            f = lb + (1.0 - lb) * jax.nn.sigmoid(fpre)
            As[d, pl.ds(t0, L), :] = jnp.dot(low if d == 0 else upp, jnp.log(f) * LOG2E, precision=HI,
                                             preferred_element_type=F32)
            Ks[d, pl.ds(t0, L), :] = 1.0 - f
        return carry

    lax.fori_loop(0, NC, decay_sums, 0)

    def chunk(c, d):
        t0 = pl.multiple_of(c * L, L)
        A = split(As[d, pl.ds(t0, L), :])
        k = split(Ks[d, pl.ds(t0, L), :])
        q = split(_silu(q_ref[pl.ds(t0, L), :].astype(F32)))
        iv = split(i_ref[pl.ds(t0, L), :].astype(BF16))
        last = L - 1 if d == 0 else 0
        st = ST[d]
        o = lax.dot_general((q * jnp.exp2(A)).astype(BF16), st.astype(BF16), bnt, preferred_element_type=F32)
        a_last = A[:, last:last + 1, :]
        kd = (k * jnp.exp2(a_last - A)).astype(BF16)
        rows = []
        for I in range(NB):
            lo, hi = I * HG_SUB, (I + 1) * HG_SUB
            A_I, q_I = A[:, lo:hi], q[:, lo:hi]
            att = jnp.zeros((HG_HEADS, HG_SUB, L), F32)
            for j in range(HG_SUB):
                s = lo + j
                col = jnp.sum(q_I * k[:, s:s + 1] * jnp.exp2(A_I - A[:, s:s + 1]), -1, keepdims=True)
                keep = (lane_s == s) & ((row8 >= j) if d == 0 else (row8 <= j))
                att = jnp.where(keep, col, att)
            if d == 0 and I > 0:
                R = A[:, lo - 1:lo]
                ksc = jnp.concatenate([(k[:, :lo] * jnp.exp2(R - A[:, :lo])).astype(BF16),
                                       jnp.zeros((HG_HEADS, L - lo, DK), BF16)], 1)
                att = att + lax.dot_general((q_I * jnp.exp2(A_I - R)).astype(BF16), ksc, bnt, preferred_element_type=F32)
            if d == 1 and I < NB - 1:
                R = A[:, hi:hi + 1]
                ksc = jnp.concatenate([jnp.zeros((HG_HEADS, hi, DK), BF16),
                                       (k[:, hi:] * jnp.exp2(R - A[:, hi:])).astype(BF16)], 1)
                att = att + lax.dot_general((q_I * jnp.exp2(A_I - R)).astype(BF16), ksc, bnt, preferred_element_type=F32)
            rows.append(att)
        att = jnp.concatenate(rows, 1)
        o = o + lax.dot_general(att.astype(BF16), iv, bnn, preferred_element_type=F32)
        for h in heads:
            (of if d == 0 else ob)[pl.ds(t0, L), h * HG_DV:(h + 1) * HG_DV] = o[h]
        ST[d] = st * jnp.exp2(a_last) + lax.dot_general(iT[c], kd, bnn, preferred_element_type=F32)

    def body(i, carry):
        chunk(i, 0)
        chunk(NC - 1 - i, 1)
        return carry

    lax.fori_loop(0, NC, body, 0, unroll=2)

    def epilogue(r, carry):
        t0 = pl.multiple_of(r * 128, 128)
        o = of[pl.ds(t0, 128), :] + ob[pl.ds(t0, 128), :]
        gsil = _silu(g_ref[pl.ds(t0, 128), :].astype(F32))
        outs = []
        for h in range(HG_HEADS):
            oh = o[:, h * HG_DV:(h + 1) * HG_DV]
            outs.append(oh * lax.rsqrt(jnp.mean(oh * oh, -1, keepdims=True) + RMS_EPS))
        c_ref[pl.ds(t0, 128), :] = (jnp.concatenate(outs, -1) * gsil).astype(BF16)
        return carry

    lax.fori_loop(0, T // 128, epilogue, 0)
    if not latent:
        for d in range(2):
            for h in range(HG_HEADS):
                so_ref[d, h] = ST[d, h].T


def _hgrn(p16, p32, lb_l, l, latent, S0=None, c_out=None, prev=None):
    T = DEC_SEQ if latent else SEQ
    B = DEC_BATCH if latent else BATCH
    rb0 = N_CTX // DEC_SEQ if latent else 0
    W = BRANCH_W
    full2 = lambda b: (0, 0)
    any_spec = pl.BlockSpec(memory_space=pl.ANY)
    col = lambda j: pl.BlockSpec((T, W), lambda b: (rb0 + b, j))
    s_spec = pl.BlockSpec((None, None, 2, HG_HEADS, HG_DK, HG_DV), lambda b: (b, l, 0, 0, 0, 0))
    in_specs = [col(0), col(1), col(7), col(8), col(9), pl.BlockSpec((1, W), full2), pl.BlockSpec((1, W), full2)]
    args = [p32, p32, p16, p16, p16, lb_l[0][None, :], lb_l[1][None, :]]
    c_shape = jax.ShapeDtypeStruct((N_TOK, W), BF16)
    c_spec = pl.BlockSpec((T, W), lambda b: (rb0 + b, 0))
    aliases = {}
    if latent:
        in_specs += [s_spec, any_spec]
        args += [S0, c_out]
        aliases = {8: 0}
        out_shape, out_specs = c_shape, c_spec
    else:
        out_shape = (c_shape, jax.ShapeDtypeStruct((B, DEPTH, 2, HG_HEADS, HG_DK, HG_DV), F32))
        out_specs = (c_spec, s_spec)
        if prev is not None:
            in_specs.append(any_spec)
            args.append(prev)
            aliases = {7: 1}
    scratch = [pltpu.VMEM((T, W), F32), pltpu.VMEM((T, W), F32), pltpu.VMEM((2, HG_HEADS, HG_DV, HG_DK), F32),
               pltpu.VMEM((T // HG_CHUNK, HG_HEADS, HG_DV, HG_CHUNK), BF16),
               pltpu.VMEM((2, T, W), F32), pltpu.VMEM((2, T, W), F32)]
    return pl.pallas_call(
        functools.partial(_hgrn_kernel, T=T, latent=latent),
        out_shape=out_shape, grid=(B,), in_specs=in_specs, out_specs=out_specs, scratch_shapes=scratch,
        input_output_aliases=aliases,
        compiler_params=_cparams("arbitrary"),
        name="hgrn_lat" if latent else "hgrn_ctx",
    )(*args)


def _merge_kernel(a_ref, b_ref, c_ref, ga_ref, gb_ref, gc_ref, xc_ref, xl_ref, g1_ref, sh2_ref, sc2_ref,
                  wb_ref, wo_ref, lng_ref, lnb_ref, wr_ref, x1_ref, h2_ref, gate_ref, gid_ref, *, tm):
    def br(v_ref, g_ref, k):
        return jax.nn.sigmoid(g_ref[...].astype(F32)) * jnp.dot(v_ref[...], wb_ref[k], preferred_element_type=F32)

    mix = br(a_ref, ga_ref, 0) + br(b_ref, gb_ref, 1) + br(c_ref, gc_ref, 2)
    y = jnp.dot(mix.astype(BF16), wo_ref[...], preferred_element_type=F32)
    x = _pair_read(pl.program_id(0), tm, xc_ref, xl_ref)
    x1 = _layer_norm(DEEPNORM_ALPHA * x + g1_ref[...] * y, lng_ref[...], lnb_ref[...])
    x1_ref[...] = x1
    h2 = x1 * (1.0 + sc2_ref[...]) + sh2_ref[...]
    h2_ref[...] = h2.astype(BF16)
    logits = jnp.dot(h2, wr_ref[...], preferred_element_type=F32, precision=HI)
    lane = lax.broadcasted_iota(jnp.int32, logits.shape, 1)
    neg = -jnp.inf
    lg = jnp.where(lane < N_GROUPS, logits, neg)
    mg = jnp.max(lg, -1, keepdims=True)
    g_sel = jnp.min(jnp.where(lg == mg, lane, 128), -1, keepdims=True)
    p_sel = 1.0 / jnp.sum(jnp.where(lane < N_GROUPS, jnp.exp(lg - mg), 0.0), -1, keepdims=True)
    lo = N_GROUPS + EXP_PER_GROUP * g_sel
    le = jnp.where((lane >= lo) & (lane < lo + EXP_PER_GROUP), logits, neg)
    v1 = jnp.max(le, -1, keepdims=True)
    i1 = jnp.min(jnp.where(le == v1, lane, 128), -1, keepdims=True)
    le2 = jnp.where(lane == i1, neg, le)
    v2 = jnp.max(le2, -1, keepdims=True)
    i2 = jnp.min(jnp.where(le2 == v2, lane, 128), -1, keepdims=True)
    e2 = jnp.exp(v2 - v1)
    w1 = p_sel / (1.0 + e2)
    w2 = p_sel * e2 / (1.0 + e2)
    gate_ref[...] = jnp.where(lane == i1 - lo, w1, jnp.where(lane == i2 - lo, w2, 0.0))
    gid_ref[...] = jnp.broadcast_to(g_sel, logits.shape)


def _merge(a, b, c, proj, xc, xl, mod, wb, wo, lng, lnb, wr, l):
    tm = 512
    tok = lambda i: (i, 0)
    ln_spec = pl.BlockSpec((None, None, 1, D_MODEL), lambda i: (l, 0, 0, 0))
    return pl.pallas_call(
        functools.partial(_merge_kernel, tm=tm),
        out_shape=(jax.ShapeDtypeStruct((N_TOK, D_MODEL), F32), jax.ShapeDtypeStruct((N_TOK, D_MODEL), BF16),
                   jax.ShapeDtypeStruct((N_TOK, 128), F32), jax.ShapeDtypeStruct((N_TOK, 128), jnp.int32)),
        grid=(N_TOK // tm,),
        in_specs=[pl.BlockSpec((tm, BRANCH_W), tok), pl.BlockSpec((tm, BRANCH_W), tok), pl.BlockSpec((tm, BRANCH_W), tok),
                  pl.BlockSpec((tm, D_MODEL), lambda i: (i, 5)), pl.BlockSpec((tm, D_MODEL), lambda i: (i, 6)),
                  pl.BlockSpec((tm, D_MODEL), lambda i: (i, 7)),
                  *_pair_specs(tm), _mod_spec(l, 2, tm), _mod_spec(l, 3, tm), _mod_spec(l, 4, tm),
                  pl.BlockSpec((None, 3, BRANCH_W, D_MODEL), lambda i: (l, 0, 0, 0)),
                  pl.BlockSpec((None, D_MODEL, D_MODEL), lambda i: (l, 0, 0)),
                  ln_spec, ln_spec,
                  pl.BlockSpec((None, D_MODEL, 128), lambda i: (l, 0, 0))],
        out_specs=(pl.BlockSpec((tm, D_MODEL), tok), pl.BlockSpec((tm, D_MODEL), tok),
                   pl.BlockSpec((tm, 128), tok), pl.BlockSpec((tm, 128), tok)),
        compiler_params=_cparams("arbitrary"),
        name="merge",
    )(a, b, c, proj, proj, proj, xc, xl, mod, mod, mod, wb, wo, lng, lnb, wr)


def _moe_up_kernel(gid_ref, x_ref, gate_ref, w1_ref, w3_ref, hid_ref, w1b, w3b):
    f = pl.program_id(0)
    t = pl.program_id(1)
    first = jnp.logical_or(t == 0, gid_ref[t] != gid_ref[jnp.maximum(t - 1, 0)])

    @pl.when(first)
    def _():
        w1b[...] = w1_ref[...].astype(BF16)
        w3b[...] = w3_ref[...].astype(BF16)

    x = x_ref[...]
    a = jnp.dot(x, w1b[...], preferred_element_type=F32)
    b = jnp.dot(x, w3b[...], preferred_element_type=F32)
    gate = gate_ref[...]
    lane = lax.broadcasted_iota(jnp.int32, gate.shape, 1)
    gcol = jnp.sum(jnp.where(lane == f, gate, 0.0), -1, keepdims=True)
    hid_ref[...] = (_silu(a) * b * gcol).astype(BF16)


def _moe_up(gid, xs, gates, w1, w3, l):
    tm = MOE_TM
    npad = MOE_NT * tm
    w_spec = pl.BlockSpec((None, None, D_MODEL, D_EXPERT), lambda f, t, g: (l, EXP_PER_GROUP * g[t] + f, 0, 0))
    grid_spec = pltpu.PrefetchScalarGridSpec(
        num_scalar_prefetch=1,
        grid=(EXP_PER_GROUP, MOE_NT),
        in_specs=[pl.BlockSpec((tm, D_MODEL), lambda f, t, g: (t, 0)),
                  pl.BlockSpec((tm, 128), lambda f, t, g: (t, 0)), w_spec, w_spec],
        out_specs=pl.BlockSpec((tm, D_EXPERT), lambda f, t, g: (t, f)),
        scratch_shapes=[pltpu.VMEM((D_MODEL, D_EXPERT), BF16), pltpu.VMEM((D_MODEL, D_EXPERT), BF16)],
    )
    return pl.pallas_call(
        _moe_up_kernel,
        out_shape=jax.ShapeDtypeStruct((npad, EXP_PER_GROUP * D_EXPERT), BF16),
        grid_spec=grid_spec,
        compiler_params=_cparams("arbitrary", "arbitrary"),
        name="moe_up",
    )(gid, xs, gates, w1, w3)


def _moe_down_kernel(gid_ref, hid_ref, w2_ref, y_ref, w2b):
    t = pl.program_id(0)
    first = jnp.logical_or(t == 0, gid_ref[t] != gid_ref[jnp.maximum(t - 1, 0)])

    @pl.when(first)
    def _():
        w2b[...] = w2_ref[...].astype(BF16)

    y_ref[...] = jnp.dot(hid_ref[...], w2b[...], preferred_element_type=F32)


def _moe_down(gid, hid, w2g, l):
    tm = MOE_TM
    npad = MOE_NT * tm
    hw = EXP_PER_GROUP * D_EXPERT
    grid_spec = pltpu.PrefetchScalarGridSpec(
        num_scalar_prefetch=1,
        grid=(MOE_NT,),
        in_specs=[pl.BlockSpec((tm, hw), lambda t, g: (t, 0)),
                  pl.BlockSpec((None, None, hw, D_MODEL), lambda t, g: (l, g[t], 0, 0))],
        out_specs=pl.BlockSpec((tm, D_MODEL), lambda t, g: (t, 0)),
        scratch_shapes=[pltpu.VMEM((hw, D_MODEL), BF16)],
    )
    return pl.pallas_call(
        _moe_down_kernel,
        out_shape=jax.ShapeDtypeStruct((npad, D_MODEL), F32),
        grid_spec=grid_spec,
        compiler_params=_cparams("arbitrary"),
        name="moe_down",
    )(gid, hid, w2g)


def _moe(h2, gate, gsel, w1, w3, w2g, l):
    tm = MOE_TM
    npad = MOE_NT * tm
    g = gsel[:, 0]
    onehot = (g[:, None] == jnp.arange(N_GROUPS)[None, :]).astype(jnp.int32)
    counts = jnp.sum(onehot, 0)
    rank = jnp.sum((jnp.cumsum(onehot, 0) - onehot) * onehot, 1)
    padded = (counts + tm - 1) // tm * tm
    ends = jnp.cumsum(padded)
    offs = ends - padded
    dest = offs[g] + rank
    src = jnp.full((npad,), -1, jnp.int32).at[dest].set(jnp.arange(N_TOK, dtype=jnp.int32), unique_indices=True)
    valid = (src >= 0).astype(F32)
    src = jnp.maximum(src, 0)
    starts = jnp.arange(MOE_NT, dtype=jnp.int32) * tm
    tile_gid = jnp.minimum(jnp.sum((ends[None, :] <= starts[:, None]).astype(jnp.int32), 1), N_GROUPS - 1)
    take = lambda arr, idx: arr.at[idx].get(mode="promise_in_bounds", unique_indices=False)
    xs = take(h2, src)
    gs = take(gate, src) * valid[:, None]
    hid = _moe_up(tile_gid, xs, gs, w1, w3, l)
    ys = _moe_down(tile_gid, hid, w2g, l)
    return take(ys, dest)


def _final_kernel(*refs, tm, with_h):
    x1_ref, y_ref, g2_ref, lng_ref, lnb_ref = refs[:5]
    x2 = _layer_norm(DEEPNORM_ALPHA * x1_ref[...] + g2_ref[...] * y_ref[...], lng_ref[...], lnb_ref[...])
    i = pl.program_id(0)
    if with_h:
        sh_ref, sc_ref, xc_ref, xl_ref, h_ref = refs[5:]
        h_ref[...] = (x2 * (1.0 + sc_ref[...]) + sh_ref[...]).astype(BF16)
    else:
        xc_ref, xl_ref = refs[5:]

    @pl.when(i < N_CTX // tm)
    def _():
        xc_ref[...] = x2

    @pl.when(i >= N_CTX // tm)
    def _():
        xl_ref[...] = x2


def _final(x1, y, mod, lng, lnb, l):
    tm = 1024
    tok = lambda i: (i, 0)
    with_h = l + 1 < DEPTH
    ln_spec = pl.BlockSpec((None, None, 1, D_MODEL), lambda i: (l, 1, 0, 0))
    half = jax.ShapeDtypeStruct((N_CTX, D_MODEL), F32)
    in_specs = [pl.BlockSpec((tm, D_MODEL), tok), pl.BlockSpec((tm, D_MODEL), tok), _mod_spec(l, 5, tm), ln_spec, ln_spec]
    args = [x1, y, mod, lng, lnb]
    out_shape = [half, half]
    out_specs = list(_pair_specs(tm))
    if with_h:
        in_specs += [_mod_spec(l + 1, 0, tm), _mod_spec(l + 1, 1, tm)]
        args += [mod, mod]
        out_shape.append(jax.ShapeDtypeStruct((N_TOK, D_MODEL), BF16))
        out_specs.append(pl.BlockSpec((tm, D_MODEL), tok))
    return pl.pallas_call(
        functools.partial(_final_kernel, tm=tm, with_h=with_h),
        out_shape=tuple(out_shape), grid=(N_TOK // tm,), in_specs=in_specs, out_specs=tuple(out_specs),
        compiler_params=_cparams("arbitrary"),
        name="final",
    )(*args)


def kernel(x_prompt, x_sample, c, cache_na_k, cache_na_v, state_mlstm_C, state_mlstm_n, state_mlstm_m, state_hgrn,
           c_ctx, w_mod, b_mod, w_in, b_in, mlstm_fbias, hgrn_lb_logits, na_rpb, w_branch, w_out, ln_g, ln_b,
           w_rg, w_re, w_e1, w_e3, w_e2):
    assert N_CTX == N_LAT
    lb_cum = jnp.cumsum(jax.nn.softmax(hgrn_lb_logits.astype(F32), axis=1), axis=1)
    lb_all = lb_cum - lb_cum[:, :1]

    cs = jnp.zeros((N_MODROWS, D_MODEL), F32).at[0].set(c_ctx).at[1:1 + DEC_BATCH].set(c)
    mod = _modulation(cs, w_mod, b_mod).reshape(DEPTH, N_MODROWS, 6, 1, D_MODEL)

    wb = w_branch.astype(BF16)
    wo = w_out.astype(BF16)
    lng = ln_g.reshape(DEPTH, 2, 1, D_MODEL)
    lnb = ln_b.reshape(DEPTH, 2, 1, D_MODEL)
    wr = jnp.concatenate([w_rg, w_re, jnp.zeros((DEPTH, D_MODEL, 128 - N_GROUPS - N_EXPERTS), F32)], -1)
    w2g = w_e2.reshape(DEPTH, N_GROUPS, EXP_PER_GROUP * D_EXPERT, D_MODEL)
    b_main = jnp.concatenate([b_in[:, :GATE_COL0], b_in[:, GATE_COL0 + N_GATES:]], 1)

    xc = x_prompt.reshape(N_CTX, D_MODEL)
    xl = x_sample.reshape(N_LAT, D_MODEL)
    w_t = jnp.swapaxes(w_in, 1, 2)
    h = _prep(xc, xl, mod)

    kv = (None, None)
    ma_states = None
    hg_state = None
    for l in range(DEPTH):
        p16, p32 = _inproj(h, w_t, b_main[l][None, :], l)
        gcol, gt = _gates(h, w_t, b_in.reshape(DEPTH, 1, N_IN), l)
        gt3 = gt.reshape(N_GATES, N_TOK // MA_CHUNK, MA_CHUNK).transpose(1, 0, 2)

        a, *ma_states = _mlstm(p16, gcol, gt3, mlstm_fbias[l], l, False, prev=ma_states)
        a = _mlstm(p16, gcol, gt3, mlstm_fbias[l], l, True, state_mlstm_C, state_mlstm_n, state_mlstm_m, a_out=a)
        b, *kv = _ctx_attention(p16, l, *kv)
        b = _lat_attention(p16, cache_na_k, cache_na_v, _na_bias_table(na_rpb[l]), l, b)
        cc, hg_state = _hgrn(p16, p32, lb_all[:, l], l, False, prev=hg_state)
        cc = _hgrn(p16, p32, lb_all[:, l], l, True, state_hgrn, c_out=cc)

        x1, h2, gate, gsel = _merge(a, b, cc, p16, xc, xl, mod, wb, wo, lng, lnb, wr, l)
        y2 = _moe(h2, gate, gsel, w_e1, w_e3, w2g, l)
        outs = _final(x1, y2, mod, lng, lnb, l)
        xc, xl = outs[0], outs[1]
        if l + 1 < DEPTH:
            h = outs[2]

    dt = x_prompt.dtype
    new_C, new_n, new_m = ma_states
    new_n = new_n.reshape(BATCH, DEPTH, 2, MA_HEADS, MA_DK)
    new_m = new_m[:, :, :, 0].reshape(BATCH, DEPTH, 2, MA_HEADS)
    return (xc.reshape(BATCH, SEQ, D_MODEL), xl.reshape(DEC_BATCH, DEC_SEQ, D_MODEL), kv[0], kv[1],
            new_C.astype(dt), new_n.astype(dt), new_m.astype(dt), hg_state.astype(dt))
```

```python
import functools

import numpy as np
import jax
import jax.numpy as jnp
from jax import lax
from jax.experimental import pallas as pl
from jax.experimental.pallas import tpu as pltpu

F32 = jnp.float32
BF16 = jnp.bfloat16
HI = lax.Precision.HIGHEST

D_MODEL = 1024
BATCH = 16
SEQ = 256
DEPTH = 2
DEC_BATCH = 4
DEC_SEQ = 1024
PAST_LEN = 256
GRID_W = 64
MA_HEADS = 4
MA_DK = 128
MA_DV = 128
MA_CHUNK = 64
NA_HEADS = 8
NA_DH = 64
NA_KR_MAX = 8
NA_KC = 16
HG_HEADS = 4
HG_DK = 128
HG_DV = 128
HG_CHUNK = 32
BRANCH_W = 512
N_GROUPS = 4
EXP_PER_GROUP = 4
N_EXPERTS = N_GROUPS * EXP_PER_GROUP
D_EXPERT = 512
ROPE_BASE = 10000.0
LN_EPS = 1e-5
RMS_EPS = 1e-6
DEEPNORM_ALPHA = (2 * DEPTH) ** 0.25

N_CTX = BATCH * SEQ
N_LAT = DEC_BATCH * DEC_SEQ
N_TOK = N_CTX + N_LAT
N_MODROWS = 8
GATE_COL0 = 4 * BRANCH_W
N_GATES = 4 * MA_HEADS
N_IN = 9232
P_COLS = N_IN - N_GATES
MOE_TM = 512
MOE_NT = N_TOK // MOE_TM + N_GROUPS
MOE_XW = D_MODEL + 128
ROUTER_ROWS = 32
ROUTER_E0 = 8
VMEM_LIMIT = 48 * 1024 * 1024

_NT = (((1,), (1,)), ((), ()))
_TN = (((0,), (0,)), ((), ()))


def _cparams(*sem):
    return pltpu.CompilerParams(dimension_semantics=sem, vmem_limit_bytes=VMEM_LIMIT)


def _mod_row(tile, tm):
    return jnp.maximum((tile * tm) // DEC_SEQ - (N_CTX // DEC_SEQ - 1), 0)


def _mod_spec(l, part, tm):
    return pl.BlockSpec((None, None, None, 1, D_MODEL), lambda i: (l, _mod_row(i, tm), part, 0, 0))


def _pair_specs(tm):
    nc = N_CTX // tm
    return (pl.BlockSpec((tm, D_MODEL), lambda i: (jnp.minimum(i, nc - 1), 0)),
            pl.BlockSpec((tm, D_MODEL), lambda i: (jnp.maximum(i - nc, 0), 0)))


def _pair_read(i, tm, c_ref, l_ref):
    return jnp.where(i < N_CTX // tm, c_ref[...], l_ref[...])


def _silu(x):
    return x * jax.nn.sigmoid(x)


def _layer_norm(x, g, b):
    mu = jnp.mean(x, -1, keepdims=True)
    xc = x - mu
    var = jnp.mean(xc * xc, -1, keepdims=True)
    return xc * lax.rsqrt(var + LN_EPS) * g + b


def _log_sigmoid(x):
    return jnp.minimum(x, 0.0) - jnp.log(1.0 + jnp.exp(-jnp.abs(x)))


def _tri(n, upper):
    r = lax.broadcasted_iota(jnp.int32, (n, n), 0)
    c = lax.broadcasted_iota(jnp.int32, (n, n), 1)
    return jnp.where((r <= c) if upper else (r >= c), 1.0, 0.0).astype(F32)


def _mod_kernel(c_ref, w_ref, b_ref, o_ref):
    s = _silu(c_ref[...])
    o_ref[...] = jnp.dot(s.astype(BF16), w_ref[...].astype(BF16), preferred_element_type=F32) + b_ref[...]


def _modulation(cs, w_mod, b_mod):
    tn = 1024
    return pl.pallas_call(
        _mod_kernel,
        out_shape=jax.ShapeDtypeStruct((DEPTH, N_MODROWS, 6 * D_MODEL), F32),
        grid=(DEPTH, 6 * D_MODEL // tn),
        in_specs=[pl.BlockSpec((N_MODROWS, D_MODEL), lambda l, j: (0, 0)),
                  pl.BlockSpec((None, D_MODEL, tn), lambda l, j: (l, 0, j)),
                  pl.BlockSpec((None, 1, tn), lambda l, j: (l, 0, j))],
        out_specs=pl.BlockSpec((None, N_MODROWS, tn), lambda l, j: (l, 0, j)),
        compiler_params=_cparams("arbitrary", "arbitrary"),
        name="modulation",
    )(cs, w_mod, b_mod.reshape(DEPTH, 1, 6 * D_MODEL))


def _prep_kernel(xc_ref, xl_ref, sh_ref, sc_ref, h_ref, *, tm):
    x = _pair_read(pl.program_id(0), tm, xc_ref, xl_ref)
    h_ref[...] = (x * (1.0 + sc_ref[...]) + sh_ref[...]).astype(BF16)


def _prep(xc, xl, mod):
    tm = 1024
    return pl.pallas_call(
        functools.partial(_prep_kernel, tm=tm),
        out_shape=jax.ShapeDtypeStruct((N_TOK, D_MODEL), BF16),
        grid=(N_TOK // tm,),
        in_specs=[*_pair_specs(tm), _mod_spec(0, 0, tm), _mod_spec(0, 1, tm)],
        out_specs=pl.BlockSpec((tm, D_MODEL), lambda i: (i, 0)),
        compiler_params=_cparams("arbitrary"),
        name="prep",
    )(xc, xl, mod, mod)


INPROJ_TN = 512
N_PLAIN_TILES = GATE_COL0 // INPROJ_TN


F32_TILE0 = 7
N_F32_TILES = 2
P16_COLS = P_COLS - N_F32_TILES * INPROJ_TN


def _inproj_kernel(h_ref, wa_ref, wb_ref, b_ref, o16_ref, o32_ref):
    j = pl.program_id(1)
    is_f32 = jnp.logical_and(j >= F32_TILE0, j < F32_TILE0 + N_F32_TILES)

    def shifted():
        w = jnp.concatenate([wa_ref[N_GATES:, :], wb_ref[...]], 0)
        return lax.dot_general(h_ref[...], w.astype(BF16), _NT, preferred_element_type=F32) + b_ref[...]

    @pl.when(j < N_PLAIN_TILES)
    def _():
        o16_ref[...] = (lax.dot_general(h_ref[...], wa_ref[...].astype(BF16), _NT, preferred_element_type=F32)
                        + b_ref[...]).astype(BF16)

    @pl.when(jnp.logical_and(j >= N_PLAIN_TILES, jnp.logical_not(is_f32)))
    def _():
        o16_ref[...] = shifted().astype(BF16)

    @pl.when(is_f32)
    def _():
        o32_ref[...] = shifted()


def _inproj(h, w_t, b_main, l):
    tm, tn = 2048, INPROJ_TN
    j16 = lambda j: jnp.where(j < F32_TILE0, j, jnp.maximum(j - N_F32_TILES, F32_TILE0 - 1))
    j32 = lambda j: jnp.clip(j - F32_TILE0, 0, N_F32_TILES - 1)
    return pl.pallas_call(
        _inproj_kernel,
        out_shape=(jax.ShapeDtypeStruct((N_TOK, P16_COLS), BF16),
                   jax.ShapeDtypeStruct((N_TOK, N_F32_TILES * tn), F32)),
        grid=(N_TOK // tm, P_COLS // tn),
        in_specs=[pl.BlockSpec((tm, D_MODEL), lambda i, j: (i, 0)),
                  pl.BlockSpec((None, tn, D_MODEL), lambda i, j: (l, j, 0)),
                  pl.BlockSpec((None, N_GATES, D_MODEL), lambda i, j: (l, (j + 1) * (tn // N_GATES), 0)),
                  pl.BlockSpec((1, tn), lambda i, j: (0, j))],
        out_specs=(pl.BlockSpec((tm, tn), lambda i, j: (i, j16(j))), pl.BlockSpec((tm, tn), lambda i, j: (i, j32(j)))),
        compiler_params=_cparams("arbitrary", "arbitrary"),
        name="inproj",
    )(h, w_t, w_t, b_main)


def _gates_kernel(h_ref, w_ref, b_ref, gc_ref, gt_ref):
    g = lax.dot_general(h_ref[...], w_ref[...].astype(BF16), _NT, preferred_element_type=F32) + b_ref[...]
    gc_ref[...] = g
    gt_ref[...] = g.T[:N_GATES]


def _gates(h, w_t, b_in3, l):
    tm = 1024
    gblk = GATE_COL0 // 128
    return pl.pallas_call(
        _gates_kernel,
        out_shape=(jax.ShapeDtypeStruct((N_TOK, 128), F32), jax.ShapeDtypeStruct((N_GATES, N_TOK), F32)),
        grid=(N_TOK // tm,),
        in_specs=[pl.BlockSpec((tm, D_MODEL), lambda i: (i, 0)),
                  pl.BlockSpec((None, 128, D_MODEL), lambda i: (l, gblk, 0)),
                  pl.BlockSpec((None, 1, 128), lambda i: (l, 0, gblk))],
        out_specs=(pl.BlockSpec((tm, 128), lambda i: (i, 0)), pl.BlockSpec((N_GATES, tm), lambda i: (0, i))),
        compiler_params=_cparams("arbitrary"),
        name="gates",
    )(h, w_t, b_in3)


HEADS_PER_BLK = 128 // NA_DH
NA_NBLK = NA_HEADS // HEADS_PER_BLK
NA_QSCALE = NA_DH ** -0.5
Q_COL, K_COL, V_COL = 16, 20, 24
QKV_COL = 4


def _ctx_attn_kernel(*refs):
    q_ref, k_ref, v_ref = refs[:3]
    o_ref, ko_ref, vo_ref = refs[-3:]
    heads = range(NA_HEADS)
    split = lambda x: jnp.stack([x[:, h * NA_DH:(h + 1) * NA_DH] for h in heads], 0)
    q = split(q_ref[...] * NA_QSCALE)
    k = split(k_ref[...])
    v = split(v_ref[...])
    ko_ref[...] = k.astype(F32)
    vo_ref[...] = v.astype(F32)
    s = lax.dot_general(q, k, (((2,), (2,)), ((0,), (0,))), preferred_element_type=F32)
    e = jnp.exp(s - jnp.max(s, -1, keepdims=True))
    p = e * (1.0 / jnp.sum(e, -1, keepdims=True))
    o = lax.dot_general(p.astype(BF16), v, (((2,), (1,)), ((0,), (0,))), preferred_element_type=F32)
    o_ref[...] = jnp.concatenate([o[h] for h in heads], -1).astype(BF16)


def _ctx_attention(p16, l, prev_k=None, prev_v=None):
    kv_shape = jax.ShapeDtypeStruct((BATCH, DEPTH, NA_HEADS, SEQ, NA_DH), F32)
    kv_spec = pl.BlockSpec((None, None, NA_HEADS, SEQ, NA_DH), lambda b: (b, l, 0, 0, 0))
    col = lambda j: pl.BlockSpec((SEQ, BRANCH_W), lambda b: (b, j))
    in_specs = [col(QKV_COL), col(QKV_COL + 1), col(QKV_COL + 2)]
    args = [p16, p16, p16]
    aliases = {}
    if prev_k is not None:
        in_specs += [pl.BlockSpec(memory_space=pl.ANY)] * 2
        args += [prev_k, prev_v]
        aliases = {3: 1, 4: 2}
    return pl.pallas_call(
        _ctx_attn_kernel,
        out_shape=(jax.ShapeDtypeStruct((N_TOK, BRANCH_W), BF16), kv_shape, kv_shape),
        grid=(BATCH,),
        in_specs=in_specs,
        out_specs=(pl.BlockSpec((SEQ, BRANCH_W), lambda b: (b, 0)), kv_spec, kv_spec),
        input_output_aliases=aliases,
        compiler_params=_cparams("arbitrary"),
        name="ctx_attention",
    )(*args)


NA_ROWS = DEC_SEQ // GRID_W
NA_KR = min(NA_KR_MAX, NA_ROWS)
NA_QROWS = 4
NA_QT = NA_ROWS // NA_QROWS
NA_WROWS = NA_KR + NA_QROWS - 1
NA_WKEYS = NA_WROWS * GRID_W


def _na_window_start(t):
    return min(max(t * NA_QROWS - NA_KR // 2, 0), NA_ROWS - NA_WROWS)


def _na_bias_table(rpb):
    c = np.arange(GRID_W)
    c0 = np.clip(c - NA_KC // 2, 0, GRID_W - NA_KC)
    kc = np.arange(GRID_W)
    valid = (kc[None, :] >= c0[:, None]) & (kc[None, :] < c0[:, None] + NA_KC)
    dc = kc[None, :] - c[:, None] + NA_KC - 1
    onehot = (dc[None] == np.arange(2 * NA_KC - 1)[:, None, None]) & valid[None]
    toep = jnp.einsum('hrd,dcx->hrcx', rpb.astype(F32), jnp.asarray(onehot, F32), precision=HI)
    toep = jnp.where(valid[None, None], toep, -jnp.inf)
    ninf = jnp.full((NA_HEADS, GRID_W, GRID_W), -jnp.inf, F32)
    tiles = []
    for t in range(NA_QT):
        w0 = _na_window_start(t)
        qrows = []
        for r in range(t * NA_QROWS, (t + 1) * NA_QROWS):
            r0 = min(max(r - NA_KR // 2, 0), NA_ROWS - NA_KR)
            blocks = []
            for kr in range(w0, w0 + NA_WROWS):
                inside = r0 <= kr < r0 + NA_KR
                blocks.append(toep[:, kr - r + NA_KR_MAX - 1] if inside else ninf)
            qrows.append(jnp.concatenate(blocks, -1))
        tiles.append(jnp.concatenate(qrows, 1))
    return jnp.stack(tiles, 1)


def _lat_attn_kernel(q_ref, k_ref, v_ref, ck_ref, cv_ref, bias_ref, prev_ref, o_ref):
    q = (q_ref[...] * NA_QSCALE).astype(BF16)
    k = k_ref[...].astype(BF16)
    v = v_ref[...].astype(BF16)
    nq = NA_QROWS * GRID_W
    for t in range(NA_QT):
        w0 = _na_window_start(t)
        qs = slice(t * nq, (t + 1) * nq)
        ws = slice(w0 * GRID_W, (w0 + NA_WROWS) * GRID_W)
        outs = []
        for hh in range(HEADS_PER_BLK):
            sl = slice(hh * NA_DH, (hh + 1) * NA_DH)
            qh = q[qs, sl]
            s_loc = lax.dot_general(qh, k[ws, sl], _NT, preferred_element_type=F32) + bias_ref[hh, t]
            s_ctx = lax.dot_general(qh, ck_ref[hh].astype(BF16), _NT, preferred_element_type=F32)
            m = jnp.maximum(jnp.max(s_loc, -1, keepdims=True), jnp.max(s_ctx, -1, keepdims=True))
            e_loc = jnp.exp(s_loc - m)
            e_ctx = jnp.exp(s_ctx - m)
            inv = 1.0 / (jnp.sum(e_loc, -1, keepdims=True) + jnp.sum(e_ctx, -1, keepdims=True))
            acc = (jnp.dot(e_loc.astype(BF16), v[ws, sl], preferred_element_type=F32)
                   + jnp.dot(e_ctx.astype(BF16), cv_ref[hh].astype(BF16), preferred_element_type=F32))
            outs.append(acc * inv)
        o_ref[qs, :] = jnp.concatenate(outs, -1).astype(BF16)


def _lat_attention(proj, ck, cv, bias, l, b_out):
    rb0 = N_CTX // DEC_SEQ
    cb = lambda base: (lambda j, b: (rb0 + b, base + j))
    c_spec = pl.BlockSpec((None, None, HEADS_PER_BLK, PAST_LEN, NA_DH), lambda j, b: (b, l, j, 0, 0))
    return pl.pallas_call(
        _lat_attn_kernel,
        out_shape=jax.ShapeDtypeStruct((N_TOK, BRANCH_W), BF16),
        grid=(NA_NBLK, DEC_BATCH),
        in_specs=[pl.BlockSpec((DEC_SEQ, 128), cb(Q_COL)), pl.BlockSpec((DEC_SEQ, 128), cb(K_COL)),
                  pl.BlockSpec((DEC_SEQ, 128), cb(V_COL)), c_spec, c_spec,
                  pl.BlockSpec((HEADS_PER_BLK, NA_QT, NA_QROWS * GRID_W, NA_WKEYS), lambda j, b: (j, 0, 0, 0)),
                  pl.BlockSpec(memory_space=pl.ANY)],
        out_specs=pl.BlockSpec((DEC_SEQ, 128), lambda j, b: (rb0 + b, j)),
        input_output_aliases={6: 0},
        compiler_params=_cparams("arbitrary", "arbitrary"),
        name="lat_attention",
    )(proj, proj, proj, ck, cv, bias, b_out)


MA_KSCALE = MA_DK ** -0.5


def _rope_tables(T):
    t = np.arange(T)
    half = MA_DK // 2
    inv = ROPE_BASE ** (-jnp.arange(0, half, 2, dtype=F32) / half)
    ang_r = jnp.asarray((t // GRID_W).astype(np.float32))[:, None] * inv[None, :]
    ang_c = jnp.asarray((t % GRID_W).astype(np.float32))[:, None] * inv[None, :]
    cos = jnp.concatenate([jnp.cos(ang_r)] * 2 + [jnp.cos(ang_c)] * 2, -1)
    sin = jnp.concatenate([-jnp.sin(ang_r), jnp.sin(ang_r), -jnp.sin(ang_c), jnp.sin(ang_c)], -1)
    return cos, sin


def _mlstm_kernel(*refs, T, latent):
    if latent:
        (p_ref, gc_ref, gt_ref, fbc_ref, fbr_ref, cos_ref, sin_ref, c0_ref, n0_ref, m0_ref, prev_ref,
         a_ref, qs, ks, vT, hfT, hbT, CT, ns, ms, brs, kcs) = refs
    else:
        p_ref, gc_ref, gt_ref, fbc_ref, fbr_ref = refs[:5]
        a_ref, co_ref, no_ref, mo_ref, qs, ks, vT, hfT, hbT, CT, ns, ms, brs, kcs = refs[-14:]
    L = MA_CHUNK
    NC = T // L
    W = BRANCH_W
    PER = 128 // L

    lane = lax.broadcasted_iota(jnp.int32, (T, MA_DK), 1)
    lo_half = (lane % (MA_DK // 2)) < (MA_DK // 4)

    def rope(x):
        if not latent:
            return x
        swapped = jnp.where(lo_half, pltpu.roll(x, MA_DK - MA_DK // 4, 1), pltpu.roll(x, MA_DK // 4, 1))
        return x * cos_ref[...] + swapped * sin_ref[...]

    for h in range(MA_HEADS):
        hs = slice(h * MA_DK, (h + 1) * MA_DK)
        qs[:, hs] = rope(p_ref[:, hs].astype(F32)).astype(BF16)
        ks[:, hs] = rope(p_ref[:, W + h * MA_DK:W + (h + 1) * MA_DK].astype(F32) * MA_KSCALE).astype(BF16)

    def v_block(tb, carry):
        r0 = pl.multiple_of(tb * 128, 128)
        for h in range(MA_HEADS):
            hs = slice(h * MA_DV, (h + 1) * MA_DV)
            blk = p_ref[pl.ds(r0, 128), 2 * W + h * MA_DV:2 * W + (h + 1) * MA_DV].astype(F32).T.astype(BF16)
            for j in range(PER):
                vT[tb * PER + j, h] = blk[:, j * L:(j + 1) * L]
        return carry

    lax.fori_loop(0, T // 128, v_block, 0)

    for d in range(2):
        for h in range(MA_HEADS):
            sidx = d * MA_HEADS + h
            CT[d, h] = c0_ref[d, h].T if latent else jnp.zeros((MA_DV, MA_DK), F32)
            ns[d, h] = n0_ref[sidx:sidx + 1, :] if latent else jnp.zeros((1, MA_DK), F32)
            ms[d, h] = m0_ref[sidx:sidx + 1, :] if latent else jnp.zeros((1, 128), F32)

    low = _tri(L, False)
    upp = _tri(L, True)
    rr = lax.broadcasted_iota(jnp.int32, (L, L), 0)
    cc = lax.broadcasted_iota(jnp.int32, (L, L), 1)
    fbc = fbc_ref[...]
    fbr = fbr_ref[...]

    def gate_sums(c, carry):
        t0 = pl.multiple_of(c * L, L)
        gc = gc_ref[pl.ds(t0, L), :]
        lfc = _log_sigmoid(gc + fbc)
        lfr = _log_sigmoid(gt_ref[c] + fbr)
        ish = pltpu.roll(gc, MA_HEADS, 1)
        brs[0, c] = jnp.dot(lfr, upp, precision=HI, preferred_element_type=F32)
        brs[1, c] = jnp.dot(lfr, low, precision=HI, preferred_element_type=F32)
        kcs[0, pl.ds(t0, L), :] = ish - jnp.dot(low, lfc, precision=HI, preferred_element_type=F32)
        kcs[1, pl.ds(t0, L), :] = ish - jnp.dot(upp, lfc, precision=HI, preferred_element_type=F32)
        return carry

    lax.fori_loop(0, NC, gate_sums, 0)

    def chunk(c, d):
        t0 = pl.multiple_of(c * L, L)
        gt = gt_ref[c]
        brow = brs[d, c]
        kc = kcs[d, pl.ds(t0, L), :]
        mask = (rr <= cc) if d == 0 else (rr >= cc)
        last = L - 1 if d == 0 else 0
        heads = range(MA_HEADS)
        gi0 = 2 * d * MA_HEADS
        gf0 = gi0 + MA_HEADS
        br = jnp.stack([brow[gf0 + h:gf0 + h + 1, :] for h in heads], 0)
        ir = jnp.stack([gt[gi0 + h:gi0 + h + 1, :] for h in heads], 0)
        kcol = jnp.stack([kc[:, gf0 + h:gf0 + h + 1] for h in heads], 0)
        m = ms[d][:, :, 0:1]
        n = ns[d]
        ct = CT[d]
        qc = qs[pl.ds(t0, L), :]
        kc = ks[pl.ds(t0, L), :]
        q = jnp.stack([qc[:, h * MA_DK:(h + 1) * MA_DK] for h in heads], 0)
        k = jnp.stack([kc[:, h * MA_DK:(h + 1) * MA_DK] for h in heads], 0)
        vt = vT[c]
        bnt = (((2,), (2,)), ((0,), (0,)))
        bnn = (((2,), (1,)), ((0,), (0,)))
        dmat = jnp.where(mask, br + kcol, -jnp.inf)
        g = br + m
        m_t = jnp.maximum(g, jnp.max(dmat, 1, keepdims=True))
        w_inter = jnp.exp(g - m_t)
        s = lax.dot_general(k, q, bnt, preferred_element_type=F32) * jnp.exp(dmat - m_t)
        ctn = jnp.concatenate([ct.astype(BF16), jnp.broadcast_to(n, (MA_HEADS, 8, MA_DK)).astype(BF16)], 1)
        cq = lax.dot_general(ctn, q, bnt, preferred_element_type=F32)
        num = w_inter * cq[:, :MA_DV] + lax.dot_general(vt, s.astype(BF16), bnn, preferred_element_type=F32)
        den = w_inter * cq[:, MA_DV:MA_DV + 1] + jnp.sum(s, 1, keepdims=True)
        (hfT if d == 0 else hbT)[c] = num / jnp.maximum(jnp.abs(den), jnp.exp(-m_t))
        m_new = m_t[:, :, last:last + 1]
        b_last = br[:, :, last:last + 1]
        decay = jnp.exp(b_last + m - m_new)
        wk = jnp.exp(b_last - br + ir - m_new)
        wk_hi = wk.astype(BF16)
        wk_lo = (wk - wk_hi.astype(F32)).astype(BF16)
        lhs = jnp.concatenate([(vt.astype(F32) * wk).astype(BF16), wk_hi, wk_lo,
                               jnp.zeros((MA_HEADS, 6, L), BF16)], 1)
        upd = lax.dot_general(lhs, k, bnn, preferred_element_type=F32)
        CT[d] = decay * ct + upd[:, :MA_DV]
        ns[d] = decay * n + upd[:, MA_DV:MA_DV + 1] + upd[:, MA_DV + 1:MA_DV + 2]
        ms[d] = jnp.broadcast_to(m_new, (MA_HEADS, 1, 128))

    def body(i, carry):
        chunk(i, 0)
        chunk(NC - 1 - i, 1)
        return carry

    lax.fori_loop(0, NC, body, 0)

    def out_block(tb, carry):
        r0 = pl.multiple_of(tb * 128, 128)
        hsum = jnp.concatenate([hfT[tb * PER + j] + hbT[tb * PER + j] for j in range(PER)], 2)
        outs = [hsum[h].T for h in range(MA_HEADS)]
        gate = jax.nn.sigmoid(p_ref[pl.ds(r0, 128), 3 * W:4 * W].astype(F32))
        a_ref[pl.ds(r0, 128), :] = (gate * jnp.concatenate(outs, 1)).astype(BF16)
        return carry

    lax.fori_loop(0, T // 128, out_block, 0)
    if not latent:
        for d in range(2):
            for h in range(MA_HEADS):
                sidx = d * MA_HEADS + h
                co_ref[d, h] = CT[d, h].T
                no_ref[sidx:sidx + 1, :] = ns[d, h]
                mo_ref[sidx:sidx + 1, :] = ms[d, h]


def _mlstm(proj, gcol, gt3, fbias_l, l, latent, C0=None, n0=None, m0=None, a_out=None, prev=None):
    T = DEC_SEQ if latent else SEQ
    B = DEC_BATCH if latent else BATCH
    rb0 = N_CTX // DEC_SEQ if latent else 0
    fb = fbias_l.astype(F32)
    fbc = jnp.zeros((1, 128), F32).at[0, MA_HEADS:2 * MA_HEADS].set(fb[0]).at[0, 3 * MA_HEADS:4 * MA_HEADS].set(fb[1])
    fbr = fbc[0, :N_GATES].reshape(N_GATES, 1)
    full2 = lambda b: (0, 0)
    any_spec = pl.BlockSpec(memory_space=pl.ANY)
    in_specs = [pl.BlockSpec((T, 4 * BRANCH_W), lambda b: (rb0 + b, 0)),
                pl.BlockSpec((T, 128), lambda b: (rb0 + b, 0)),
                pl.BlockSpec((T // MA_CHUNK, N_GATES, MA_CHUNK), lambda b: (rb0 + b, 0, 0)),
                pl.BlockSpec((1, 128), full2), pl.BlockSpec((N_GATES, 1), full2)]
    args = [proj, gcol, gt3, fbc, fbr]
    a_shape = jax.ShapeDtypeStruct((N_TOK, BRANCH_W), BF16)
    a_spec = pl.BlockSpec((T, BRANCH_W), lambda b: (rb0 + b, 0))
    c_spec = pl.BlockSpec((None, None, 2, MA_HEADS, MA_DK, MA_DV), lambda b: (b, l, 0, 0, 0, 0))
    nm_spec = pl.BlockSpec((None, None, 2 * MA_HEADS, 128), lambda b: (b, l, 0, 0))
    aliases = {}
    if latent:
        cos, sin = _rope_tables(T)
        nb = 2 * MA_HEADS
        in_specs += [pl.BlockSpec((T, MA_DK), full2), pl.BlockSpec((T, MA_DK), full2), c_spec, nm_spec, nm_spec, any_spec]
        args += [cos, sin, C0, n0.reshape(B, DEPTH, nb, MA_DK),
                 jnp.broadcast_to(m0.reshape(B, DEPTH, nb, 1), (B, DEPTH, nb, 128)), a_out]
        aliases = {len(args) - 1: 0}
        out_shape, out_specs = a_shape, a_spec
    else:
        nm_shape = jax.ShapeDtypeStruct((B, DEPTH, 2 * MA_HEADS, 128), F32)
        out_shape = (a_shape, jax.ShapeDtypeStruct((B, DEPTH, 2, MA_HEADS, MA_DK, MA_DV), F32), nm_shape, nm_shape)
        out_specs = (a_spec, c_spec, nm_spec, nm_spec)
        if prev is not None:
            in_specs += [any_spec] * 3
            args += list(prev)
            aliases = {len(args) - 3: 1, len(args) - 2: 2, len(args) - 1: 3}
    nc = T // MA_CHUNK
    scratch = [pltpu.VMEM((T, BRANCH_W), BF16), pltpu.VMEM((T, BRANCH_W), BF16),
               pltpu.VMEM((nc, MA_HEADS, MA_DV, MA_CHUNK), BF16),
               pltpu.VMEM((nc, MA_HEADS, MA_DV, MA_CHUNK), F32), pltpu.VMEM((nc, MA_HEADS, MA_DV, MA_CHUNK), F32),
               pltpu.VMEM((2, MA_HEADS, MA_DV, MA_DK), F32), pltpu.VMEM((2, MA_HEADS, 1, MA_DK), F32),
               pltpu.VMEM((2, MA_HEADS, 1, 128), F32),
               pltpu.VMEM((2, nc, N_GATES, MA_CHUNK), F32), pltpu.VMEM((2, T, 128), F32)]
    return pl.pallas_call(
        functools.partial(_mlstm_kernel, T=T, latent=latent),
        out_shape=out_shape, grid=(B,), in_specs=in_specs, out_specs=out_specs, scratch_shapes=scratch,
        input_output_aliases=aliases,
        compiler_params=_cparams("arbitrary"),
        name="mlstm_lat" if latent else "mlstm_ctx",
    )(*args)


HG_SUB = 8


def _hgrn_kernel(*refs, T, latent):
    ff_ref, fb_ref, q_ref, i_ref, g_ref, lbf_ref, lbb_ref = refs[:7]
    if latent:
        s0_ref = refs[7]
        c_ref, of, ob, ST, iT, As, Ks = refs[-7:]
    else:
        c_ref, so_ref, of, ob, ST, iT, As, Ks = refs[-8:]
    L = HG_CHUNK
    NC = T // L
    NB = L // HG_SUB
    DK = HG_DK

    for d in range(2):
        for h in range(HG_HEADS):
            ST[d, h] = s0_ref[d, h].T if latent else jnp.zeros((HG_DV, DK), F32)

    PER = 128 // L

    def i_block(tb, carry):
        r0 = pl.multiple_of(tb * 128, 128)
        for h in range(HG_HEADS):
            blk = i_ref[pl.ds(r0, 128), h * HG_DV:(h + 1) * HG_DV].astype(F32).T.astype(BF16)
            for j in range(PER):
                iT[tb * PER + j, h] = blk[:, j * L:(j + 1) * L]
        return carry

    lax.fori_loop(0, T // 128, i_block, 0)

    low = _tri(L, False)
    upp = _tri(L, True)
    row8 = lax.broadcasted_iota(jnp.int32, (HG_SUB, L), 0)
    lane_s = lax.broadcasted_iota(jnp.int32, (HG_SUB, L), 1)
    heads = range(HG_HEADS)
    bnt = (((2,), (2,)), ((0,), (0,)))
    bnn = (((2,), (1,)), ((0,), (0,)))
    LOG2E = 1.4426950408889634

    def split(x):
        return jnp.stack([x[:, h * DK:(h + 1) * DK] for h in heads], 0)

    def decay_sums(c, carry):
        t0 = pl.multiple_of(c * L, L)
        for d in range(2):
            fpre = (ff_ref if d == 0 else fb_ref)[pl.ds(t0, L), :]
            lb = (lbf_ref if d == 0 else lbb_ref)[...]
            f = lb + (1.0 - lb) * jax.nn.sigmoid(fpre)
            As[d, pl.ds(t0, L), :] = jnp.dot(low if d == 0 else upp, jnp.log(f) * LOG2E, precision=HI,
                                             preferred_element_type=F32)
            Ks[d, pl.ds(t0, L), :] = 1.0 - f
        return carry

    lax.fori_loop(0, NC, decay_sums, 0)

    def chunk(c, d):
        t0 = pl.multiple_of(c * L, L)
        A = split(As[d, pl.ds(t0, L), :])
        k = split(Ks[d, pl.ds(t0, L), :])
        q = split(_silu(q_ref[pl.ds(t0, L), :].astype(F32)))
        iv = split(i_ref[pl.ds(t0, L), :].astype(BF16))
        last = L - 1 if d == 0 else 0
        st = ST[d]
        o = lax.dot_general((q * jnp.exp2(A)).astype(BF16), st.astype(BF16), bnt, preferred_element_type=F32)
        a_last = A[:, last:last + 1, :]
        kd = (k * jnp.exp2(a_last - A)).astype(BF16)
        rows = []
        for I in range(NB):
            lo, hi = I * HG_SUB, (I + 1) * HG_SUB
            A_I, q_I = A[:, lo:hi], q[:, lo:hi]
            att = jnp.zeros((HG_HEADS, HG_SUB, L), F32)
            for j in range(HG_SUB):
                s = lo + j
                col = jnp.sum(q_I * k[:, s:s + 1] * jnp.exp2(A_I - A[:, s:s + 1]), -1, keepdims=True)
                keep = (lane_s == s) & ((row8 >= j) if d == 0 else (row8 <= j))
                att = jnp.where(keep, col, att)
            if d == 0 and I > 0:
                R = A[:, lo - 1:lo]
                ksc = jnp.concatenate([(k[:, :lo] * jnp.exp2(R - A[:, :lo])).astype(BF16),
                                       jnp.zeros((HG_HEADS, L - lo, DK), BF16)], 1)
                att = att + lax.dot_general((q_I * jnp.exp2(A_I - R)).astype(BF16), ksc, bnt, preferred_element_type=F32)
            if d == 1 and I < NB - 1:
                R = A[:, hi:hi + 1]
                ksc = jnp.concatenate([jnp.zeros((HG_HEADS, hi, DK), BF16),
                                       (k[:, hi:] * jnp.exp2(R - A[:, hi:])).astype(BF16)], 1)
                att = att + lax.dot_general((q_I * jnp.exp2(A_I - R)).astype(BF16), ksc, bnt, preferred_element_type=F32)
            rows.append(att)
        att = jnp.concatenate(rows, 1)
        o = o + lax.dot_general(att.astype(BF16), iv, bnn, preferred_element_type=F32)
        for h in heads:
            (of if d == 0 else ob)[pl.ds(t0, L), h * HG_DV:(h + 1) * HG_DV] = o[h]
        ST[d] = st * jnp.exp2(a_last) + lax.dot_general(iT[c], kd, bnn, preferred_element_type=F32)

    def body(i, carry):
        chunk(i, 0)
        chunk(NC - 1 - i, 1)
        return carry

    lax.fori_loop(0, NC, body, 0, unroll=2)

    def epilogue(r, carry):
        t0 = pl.multiple_of(r * 128, 128)
        o = of[pl.ds(t0, 128), :] + ob[pl.ds(t0, 128), :]
        gsil = _silu(g_ref[pl.ds(t0, 128), :].astype(F32))
        outs = []
        for h in range(HG_HEADS):
            oh = o[:, h * HG_DV:(h + 1) * HG_DV]
            outs.append(oh * lax.rsqrt(jnp.mean(oh * oh, -1, keepdims=True) + RMS_EPS))
        c_ref[pl.ds(t0, 128), :] = (jnp.concatenate(outs, -1) * gsil).astype(BF16)
        return carry

    lax.fori_loop(0, T // 128, epilogue, 0)
    if not latent:
        for d in range(2):
            for h in range(HG_HEADS):
                so_ref[d, h] = ST[d, h].T


def _hgrn(p16, p32, lb_l, l, latent, S0=None, c_out=None, prev=None):
    T = DEC_SEQ if latent else SEQ
    B = DEC_BATCH if latent else BATCH
    rb0 = N_CTX // DEC_SEQ if latent else 0
    W = BRANCH_W
    full2 = lambda b: (0, 0)
    any_spec = pl.BlockSpec(memory_space=pl.ANY)
    col = lambda j: pl.BlockSpec((T, W), lambda b: (rb0 + b, j))
    s_spec = pl.BlockSpec((None, None, 2, HG_HEADS, HG_DK, HG_DV), lambda b: (b, l, 0, 0, 0, 0))
    in_specs = [col(0), col(1), col(7), col(8), col(9), pl.BlockSpec((1, W), full2), pl.BlockSpec((1, W), full2)]
    args = [p32, p32, p16, p16, p16, lb_l[0][None, :], lb_l[1][None, :]]
    c_shape = jax.ShapeDtypeStruct((N_TOK, W), BF16)
    c_spec = pl.BlockSpec((T, W), lambda b: (rb0 + b, 0))
    aliases = {}
    if latent:
        in_specs += [s_spec, any_spec]
        args += [S0, c_out]
        aliases = {8: 0}
        out_shape, out_specs = c_shape, c_spec
    else:
        out_shape = (c_shape, jax.ShapeDtypeStruct((B, DEPTH, 2, HG_HEADS, HG_DK, HG_DV), F32))
        out_specs = (c_spec, s_spec)
        if prev is not None:
            in_specs.append(any_spec)
            args.append(prev)
            aliases = {7: 1}
    scratch = [pltpu.VMEM((T, W), F32), pltpu.VMEM((T, W), F32), pltpu.VMEM((2, HG_HEADS, HG_DV, HG_DK), F32),
               pltpu.VMEM((T // HG_CHUNK, HG_HEADS, HG_DV, HG_CHUNK), BF16),
               pltpu.VMEM((2, T, W), F32), pltpu.VMEM((2, T, W), F32)]
    return pl.pallas_call(
        functools.partial(_hgrn_kernel, T=T, latent=latent),
        out_shape=out_shape, grid=(B,), in_specs=in_specs, out_specs=out_specs, scratch_shapes=scratch,
        input_output_aliases=aliases,
        compiler_params=_cparams("arbitrary"),
        name="hgrn_lat" if latent else "hgrn_ctx",
    )(*args)


def _merge_kernel(a_ref, b_ref, c_ref, ga_ref, gb_ref, gc_ref, xc_ref, xl_ref, g1_ref, sh2_ref, sc2_ref,
                  wb_ref, wo_ref, lng_ref, lnb_ref, wr_ref, x1_ref, h2_ref, *, tm):
    def br(v_ref, g_ref, k):
        return jax.nn.sigmoid(g_ref[...].astype(F32)) * jnp.dot(v_ref[...], wb_ref[k], preferred_element_type=F32)

    mix = br(a_ref, ga_ref, 0) + br(b_ref, gb_ref, 1) + br(c_ref, gc_ref, 2)
    y = jnp.dot(mix.astype(BF16), wo_ref[...], preferred_element_type=F32)
    x = _pair_read(pl.program_id(0), tm, xc_ref, xl_ref)
    x1 = _layer_norm(DEEPNORM_ALPHA * x + g1_ref[...] * y, lng_ref[...], lnb_ref[...])
    x1_ref[...] = x1
    h2 = x1 * (1.0 + sc2_ref[...]) + sh2_ref[...]
    h2_ref[:, :D_MODEL] = h2.astype(BF16)
    lt = lax.dot_general(wr_ref[...], h2, _NT, preferred_element_type=F32, precision=HI)
    r = lax.broadcasted_iota(jnp.int32, lt.shape, 0)
    neg = -jnp.inf
    lg = jnp.where(r < N_GROUPS, lt, neg)
    mg = jnp.max(lg, 0, keepdims=True)
    g_sel = jnp.min(jnp.where(lg == mg, r, ROUTER_ROWS), 0, keepdims=True)
    p_sel = 1.0 / jnp.sum(jnp.where(r < N_GROUPS, jnp.exp(lg - mg), 0.0), 0, keepdims=True)
    lo = ROUTER_E0 + EXP_PER_GROUP * g_sel
    le = jnp.where((r >= lo) & (r < lo + EXP_PER_GROUP), lt, neg)
    v1 = jnp.max(le, 0, keepdims=True)
    i1 = jnp.min(jnp.where(le == v1, r, ROUTER_ROWS), 0, keepdims=True)
    le2 = jnp.where(r == i1, neg, le)
    v2 = jnp.max(le2, 0, keepdims=True)
    i2 = jnp.min(jnp.where(le2 == v2, r, ROUTER_ROWS), 0, keepdims=True)
    e2 = jnp.exp(v2 - v1)
    w1 = p_sel / (1.0 + e2)
    w2 = p_sel * e2 / (1.0 + e2)
    w1_hi = w1.astype(BF16).astype(F32)
    w2_hi = w2.astype(BF16).astype(F32)
    j1, j2 = i1 - lo, i2 - lo
    packed = jnp.where(r == j1, w1_hi, jnp.where(r == j2, w2_hi, jnp.where(
        r == j1 + EXP_PER_GROUP, w1 - w1_hi, jnp.where(r == j2 + EXP_PER_GROUP, w2 - w2_hi, jnp.where(
            r == 2 * EXP_PER_GROUP, g_sel.astype(F32), 0.0)))))
    packed = jnp.concatenate([packed, jnp.zeros((128 - ROUTER_ROWS, packed.shape[1]), F32)], 0)
    h2_ref[:, D_MODEL:] = packed.T.astype(BF16)


def _merge(a, b, c, p16, xc, xl, mod, wb, wo, lng, lnb, wr, l):
    tm = 512
    tok = lambda i: (i, 0)
    ln_spec = pl.BlockSpec((None, None, 1, D_MODEL), lambda i: (l, 0, 0, 0))
    return pl.pallas_call(
        functools.partial(_merge_kernel, tm=tm),
        out_shape=(jax.ShapeDtypeStruct((N_TOK, D_MODEL), F32), jax.ShapeDtypeStruct((N_TOK, MOE_XW), BF16)),
        grid=(N_TOK // tm,),
        in_specs=[pl.BlockSpec((tm, BRANCH_W), tok), pl.BlockSpec((tm, BRANCH_W), tok), pl.BlockSpec((tm, BRANCH_W), tok),
                  pl.BlockSpec((tm, D_MODEL), lambda i: (i, 5)), pl.BlockSpec((tm, D_MODEL), lambda i: (i, 6)),
                  pl.BlockSpec((tm, D_MODEL), lambda i: (i, 7)),
                  *_pair_specs(tm), _mod_spec(l, 2, tm), _mod_spec(l, 3, tm), _mod_spec(l, 4, tm),
                  pl.BlockSpec((None, 3, BRANCH_W, D_MODEL), lambda i: (l, 0, 0, 0)),
                  pl.BlockSpec((None, D_MODEL, D_MODEL), lambda i: (l, 0, 0)),
                  ln_spec, ln_spec,
                  pl.BlockSpec((None, ROUTER_ROWS, D_MODEL), lambda i: (l, 0, 0))],
        out_specs=(pl.BlockSpec((tm, D_MODEL), tok), pl.BlockSpec((tm, MOE_XW), tok)),
        compiler_params=_cparams("arbitrary"),
        name="merge",
    )(a, b, c, p16, p16, p16, xc, xl, mod, mod, mod, wb, wo, lng, lnb, wr)


def _moe_up_kernel(gid_ref, nused_ref, x_ref, w1_ref, w3_ref, hid_ref, w1b, w3b):
    f = pl.program_id(0)
    t = pl.program_id(1)

    @pl.when(t < nused_ref[0])
    def _():
        first = jnp.logical_or(t == 0, gid_ref[t] != gid_ref[jnp.maximum(t - 1, 0)])

        @pl.when(first)
        def _():
            w1b[...] = w1_ref[...].astype(BF16)
            w3b[...] = w3_ref[...].astype(BF16)

        x = x_ref[:, :D_MODEL]
        a = jnp.dot(x, w1b[...], preferred_element_type=F32)
        b = jnp.dot(x, w3b[...], preferred_element_type=F32)
        rec = x_ref[:, D_MODEL:].astype(F32)
        lane = lax.broadcasted_iota(jnp.int32, rec.shape, 1)
        gcol = jnp.sum(jnp.where(jnp.logical_or(lane == f, lane == f + EXP_PER_GROUP), rec, 0.0), -1, keepdims=True)
        hid_ref[...] = (_silu(a) * b * gcol).astype(BF16)


def _moe_tile(t, n):
    return jnp.minimum(t, n[0] - 1)


def _moe_up(gid, nused, xs, w1, w3, l):
    tm = MOE_TM
    npad = MOE_NT * tm
    w_spec = pl.BlockSpec((None, None, D_MODEL, D_EXPERT),
                          lambda f, t, g, n: (l, EXP_PER_GROUP * g[_moe_tile(t, n)] + f, 0, 0))
    grid_spec = pltpu.PrefetchScalarGridSpec(
        num_scalar_prefetch=2,
        grid=(EXP_PER_GROUP, MOE_NT),
        in_specs=[pl.BlockSpec((tm, MOE_XW), lambda f, t, g, n: (_moe_tile(t, n), 0)), w_spec, w_spec],
        out_specs=pl.BlockSpec((tm, D_EXPERT), lambda f, t, g, n: (_moe_tile(t, n), f)),
        scratch_shapes=[pltpu.VMEM((D_MODEL, D_EXPERT), BF16), pltpu.VMEM((D_MODEL, D_EXPERT), BF16)],
    )
    return pl.pallas_call(
        _moe_up_kernel,
        out_shape=jax.ShapeDtypeStruct((npad, EXP_PER_GROUP * D_EXPERT), BF16),
        grid_spec=grid_spec,
        compiler_params=_cparams("arbitrary", "arbitrary"),
        name="moe_up",
    )(gid, nused, xs, w1, w3)


def _moe_down_kernel(gid_ref, nused_ref, hid_ref, w2_ref, y_ref, w2b):
    t = pl.program_id(0)

    @pl.when(t < nused_ref[0])
    def _():
        first = jnp.logical_or(t == 0, gid_ref[t] != gid_ref[jnp.maximum(t - 1, 0)])

        @pl.when(first)
        def _():
            w2b[...] = w2_ref[...].astype(BF16)

        y_ref[...] = jnp.dot(hid_ref[...], w2b[...], preferred_element_type=F32)


def _moe_down(gid, nused, hid, w2g, l):
    tm = MOE_TM
    npad = MOE_NT * tm
    hw = EXP_PER_GROUP * D_EXPERT
    grid_spec = pltpu.PrefetchScalarGridSpec(
        num_scalar_prefetch=2,
        grid=(MOE_NT,),
        in_specs=[pl.BlockSpec((tm, hw), lambda t, g, n: (_moe_tile(t, n), 0)),
                  pl.BlockSpec((None, None, hw, D_MODEL), lambda t, g, n: (l, g[_moe_tile(t, n)], 0, 0))],
        out_specs=pl.BlockSpec((tm, D_MODEL), lambda t, g, n: (_moe_tile(t, n), 0)),
        scratch_shapes=[pltpu.VMEM((hw, D_MODEL), BF16)],
    )
    return pl.pallas_call(
        _moe_down_kernel,
        out_shape=jax.ShapeDtypeStruct((npad, D_MODEL), F32),
        grid_spec=grid_spec,
        compiler_params=_cparams("arbitrary"),
        name="moe_down",
    )(gid, nused, hid, w2g)


def _moe(h2x, w1, w3, w2g, l):
    tm = MOE_TM
    npad = MOE_NT * tm
    g = h2x[:, D_MODEL + 2 * EXP_PER_GROUP].astype(jnp.int32)
    onehot = (g[:, None] == jnp.arange(N_GROUPS)[None, :]).astype(jnp.int32)
    counts = jnp.sum(onehot, 0)
    rank = jnp.sum((jnp.cumsum(onehot, 0) - onehot) * onehot, 1)
    padded = (counts + tm - 1) // tm * tm
    ends = jnp.cumsum(padded)
    offs = ends - padded
    dest = offs[g] + rank
    src = jnp.zeros((npad,), jnp.int32).at[dest].set(jnp.arange(N_TOK, dtype=jnp.int32), unique_indices=True)
    starts = jnp.arange(MOE_NT, dtype=jnp.int32) * tm
    tile_gid = jnp.minimum(jnp.sum((ends[None, :] <= starts[:, None]).astype(jnp.int32), 1), N_GROUPS - 1)
    nused = (ends[-1:] // tm).astype(jnp.int32)
    take = lambda arr, idx: arr.at[idx].get(mode="promise_in_bounds", unique_indices=False)
    hid = _moe_up(tile_gid, nused, take(h2x, src), w1, w3, l)
    ys = _moe_down(tile_gid, nused, hid, w2g, l)
    return take(ys, dest)


def _final_kernel(*refs, tm, with_h):
    x1_ref, y_ref, g2_ref, lng_ref, lnb_ref = refs[:5]
    x2 = _layer_norm(DEEPNORM_ALPHA * x1_ref[...] + g2_ref[...] * y_ref[...], lng_ref[...], lnb_ref[...])
    i = pl.program_id(0)
    if with_h:
        sh_ref, sc_ref, xc_ref, xl_ref, h_ref = refs[5:]
        h_ref[...] = (x2 * (1.0 + sc_ref[...]) + sh_ref[...]).astype(BF16)
    else:
        xc_ref, xl_ref = refs[5:]

    @pl.when(i < N_CTX // tm)
    def _():
        xc_ref[...] = x2

    @pl.when(i >= N_CTX // tm)
    def _():
        xl_ref[...] = x2


def _final(x1, y, mod, lng, lnb, l):
    tm = 1024
    tok = lambda i: (i, 0)
    with_h = l + 1 < DEPTH
    ln_spec = pl.BlockSpec((None, None, 1, D_MODEL), lambda i: (l, 1, 0, 0))
    half = jax.ShapeDtypeStruct((N_CTX, D_MODEL), F32)
    in_specs = [pl.BlockSpec((tm, D_MODEL), tok), pl.BlockSpec((tm, D_MODEL), tok), _mod_spec(l, 5, tm), ln_spec, ln_spec]
    args = [x1, y, mod, lng, lnb]
    out_shape = [half, half]
    out_specs = list(_pair_specs(tm))
    if with_h:
        in_specs += [_mod_spec(l + 1, 0, tm), _mod_spec(l + 1, 1, tm)]
        args += [mod, mod]
        out_shape.append(jax.ShapeDtypeStruct((N_TOK, D_MODEL), BF16))
        out_specs.append(pl.BlockSpec((tm, D_MODEL), tok))
    return pl.pallas_call(
        functools.partial(_final_kernel, tm=tm, with_h=with_h),
        out_shape=tuple(out_shape), grid=(N_TOK // tm,), in_specs=in_specs, out_specs=tuple(out_specs),
        compiler_params=_cparams("arbitrary"),
        name="final",
    )(*args)


def kernel(x_prompt, x_sample, c, cache_na_k, cache_na_v, state_mlstm_C, state_mlstm_n, state_mlstm_m, state_hgrn,
           c_ctx, w_mod, b_mod, w_in, b_in, mlstm_fbias, hgrn_lb_logits, na_rpb, w_branch, w_out, ln_g, ln_b,
           w_rg, w_re, w_e1, w_e3, w_e2):
    assert N_CTX == N_LAT
    lb_cum = jnp.cumsum(jax.nn.softmax(hgrn_lb_logits.astype(F32), axis=1), axis=1)
    lb_all = lb_cum - lb_cum[:, :1]

    cs = jnp.zeros((N_MODROWS, D_MODEL), F32).at[0].set(c_ctx).at[1:1 + DEC_BATCH].set(c)
    mod = _modulation(cs, w_mod, b_mod).reshape(DEPTH, N_MODROWS, 6, 1, D_MODEL)

    wb = w_branch.astype(BF16)
    wo = w_out.astype(BF16)
    lng = ln_g.reshape(DEPTH, 2, 1, D_MODEL)
    lnb = ln_b.reshape(DEPTH, 2, 1, D_MODEL)
    wr = jnp.zeros((DEPTH, ROUTER_ROWS, D_MODEL), F32)
    wr = wr.at[:, :N_GROUPS].set(jnp.swapaxes(w_rg, 1, 2)).at[:, ROUTER_E0:ROUTER_E0 + N_EXPERTS].set(jnp.swapaxes(w_re, 1, 2))
    w2g = w_e2.reshape(DEPTH, N_GROUPS, EXP_PER_GROUP * D_EXPERT, D_MODEL)
    b_main = jnp.concatenate([b_in[:, :GATE_COL0], b_in[:, GATE_COL0 + N_GATES:]], 1)

    xc = x_prompt.reshape(N_CTX, D_MODEL)
    xl = x_sample.reshape(N_LAT, D_MODEL)
    w_t = jnp.swapaxes(w_in, 1, 2)
    h = _prep(xc, xl, mod)

    kv = (None, None)
    ma_states = None
    hg_state = None
    for l in range(DEPTH):
        p16, p32 = _inproj(h, w_t, b_main[l][None, :], l)
        gcol, gt = _gates(h, w_t, b_in.reshape(DEPTH, 1, N_IN), l)
        gt3 = gt.reshape(N_GATES, N_TOK // MA_CHUNK, MA_CHUNK).transpose(1, 0, 2)

        a, *ma_states = _mlstm(p16, gcol, gt3, mlstm_fbias[l], l, False, prev=ma_states)
        a = _mlstm(p16, gcol, gt3, mlstm_fbias[l], l, True, state_mlstm_C, state_mlstm_n, state_mlstm_m, a_out=a)
        b, *kv = _ctx_attention(p16, l, *kv)
        b = _lat_attention(p16, cache_na_k, cache_na_v, _na_bias_table(na_rpb[l]), l, b)
        cc, hg_state = _hgrn(p16, p32, lb_all[:, l], l, False, prev=hg_state)
        cc = _hgrn(p16, p32, lb_all[:, l], l, True, state_hgrn, c_out=cc)

        x1, h2x = _merge(a, b, cc, p16, xc, xl, mod, wb, wo, lng, lnb, wr, l)
        y2 = _moe(h2x, w_e1, w_e3, w2g, l)
        outs = _final(x1, y2, mod, lng, lnb, l)
        xc, xl = outs[0], outs[1]
        if l + 1 < DEPTH:
            h = outs[2]

    dt = x_prompt.dtype
    new_C, new_n, new_m = ma_states
    new_n = new_n.reshape(BATCH, DEPTH, 2, MA_HEADS, MA_DK)
    new_m = new_m[:, :, :, 0].reshape(BATCH, DEPTH, 2, MA_HEADS)
    return (xc.reshape(BATCH, SEQ, D_MODEL), xl.reshape(DEC_BATCH, DEC_SEQ, D_MODEL), kv[0], kv[1],
            new_C.astype(dt), new_n.astype(dt), new_m.astype(dt), hg_state.astype(dt))
```

```python
import functools

import numpy as np
import jax
import jax.numpy as jnp
from jax import lax
from jax.experimental import pallas as pl
from jax.experimental.pallas import tpu as pltpu

F32 = jnp.float32
BF16 = jnp.bfloat16
HI = lax.Precision.HIGHEST

D_MODEL = 1024
BATCH = 16
SEQ = 256
DEPTH = 2
DEC_BATCH = 4
DEC_SEQ = 1024
PAST_LEN = 256
GRID_W = 64
MA_HEADS = 4
MA_DK = 128
MA_DV = 128
MA_CHUNK = 64
NA_HEADS = 8
NA_DH = 64
NA_KR_MAX = 8
NA_KC = 16
HG_HEADS = 4
HG_DK = 128
HG_DV = 128
HG_CHUNK = 32
BRANCH_W = 512
N_GROUPS = 4
EXP_PER_GROUP = 4
N_EXPERTS = N_GROUPS * EXP_PER_GROUP
D_EXPERT = 512
ROPE_BASE = 10000.0
LN_EPS = 1e-5
RMS_EPS = 1e-6
DEEPNORM_ALPHA = (2 * DEPTH) ** 0.25

N_CTX = BATCH * SEQ
N_LAT = DEC_BATCH * DEC_SEQ
N_TOK = N_CTX + N_LAT
N_MODROWS = 8
GATE_COL0 = 4 * BRANCH_W
N_GATES = 4 * MA_HEADS
N_IN = 9232
P_COLS = N_IN - N_GATES
MOE_TM = 512
MOE_NT = N_TOK // MOE_TM + N_GROUPS
MOE_FE = 2
MOE_XW = D_MODEL + 128
ROUTER_ROWS = 32
ROUTER_E0 = 8
VMEM_LIMIT = 48 * 1024 * 1024

_NT = (((1,), (1,)), ((), ()))
_TN = (((0,), (0,)), ((), ()))


def _cparams(*sem):
    return pltpu.CompilerParams(dimension_semantics=sem, vmem_limit_bytes=VMEM_LIMIT)


def _mod_row(tile, tm):
    return jnp.maximum((tile * tm) // DEC_SEQ - (N_CTX // DEC_SEQ - 1), 0)


def _mod_spec(l, part, tm):
    return pl.BlockSpec((None, None, None, 1, D_MODEL), lambda i: (l, _mod_row(i, tm), part, 0, 0))


def _pair_specs(tm):
    nc = N_CTX // tm
    return (pl.BlockSpec((tm, D_MODEL), lambda i: (jnp.minimum(i, nc - 1), 0)),
            pl.BlockSpec((tm, D_MODEL), lambda i: (jnp.maximum(i - nc, 0), 0)))


def _pair_read(i, tm, c_ref, l_ref):
    return jnp.where(i < N_CTX // tm, c_ref[...], l_ref[...])


def _silu(x):
    return x * jax.nn.sigmoid(x)


def _layer_norm(x, g, b):
    mu = jnp.mean(x, -1, keepdims=True)
    xc = x - mu
    var = jnp.mean(xc * xc, -1, keepdims=True)
    return xc * lax.rsqrt(var + LN_EPS) * g + b


def _log_sigmoid(x):
    return jnp.minimum(x, 0.0) - jnp.log(1.0 + jnp.exp(-jnp.abs(x)))


def _tri(n, upper):
    r = lax.broadcasted_iota(jnp.int32, (n, n), 0)
    c = lax.broadcasted_iota(jnp.int32, (n, n), 1)
    return jnp.where((r <= c) if upper else (r >= c), 1.0, 0.0).astype(F32)


def _mod_kernel(c_ref, w_ref, b_ref, o_ref):
    s = _silu(c_ref[...])
    o_ref[...] = jnp.dot(s.astype(BF16), w_ref[...].astype(BF16), preferred_element_type=F32) + b_ref[...]


def _modulation(cs, w_mod, b_mod):
    tn = 1024
    return pl.pallas_call(
        _mod_kernel,
        out_shape=jax.ShapeDtypeStruct((DEPTH, N_MODROWS, 6 * D_MODEL), F32),
        grid=(DEPTH, 6 * D_MODEL // tn),
        in_specs=[pl.BlockSpec((N_MODROWS, D_MODEL), lambda l, j: (0, 0)),
                  pl.BlockSpec((None, D_MODEL, tn), lambda l, j: (l, 0, j)),
                  pl.BlockSpec((None, 1, tn), lambda l, j: (l, 0, j))],
        out_specs=pl.BlockSpec((None, N_MODROWS, tn), lambda l, j: (l, 0, j)),
        compiler_params=_cparams("arbitrary", "arbitrary"),
        name="modulation",
    )(cs, w_mod, b_mod.reshape(DEPTH, 1, 6 * D_MODEL))


def _prep_kernel(xc_ref, xl_ref, sh_ref, sc_ref, h_ref, *, tm):
    x = _pair_read(pl.program_id(0), tm, xc_ref, xl_ref)
    h_ref[...] = (x * (1.0 + sc_ref[...]) + sh_ref[...]).astype(BF16)


def _prep(xc, xl, mod):
    tm = 1024
    return pl.pallas_call(
        functools.partial(_prep_kernel, tm=tm),
        out_shape=jax.ShapeDtypeStruct((N_TOK, D_MODEL), BF16),
        grid=(N_TOK // tm,),
        in_specs=[*_pair_specs(tm), _mod_spec(0, 0, tm), _mod_spec(0, 1, tm)],
        out_specs=pl.BlockSpec((tm, D_MODEL), lambda i: (i, 0)),
        compiler_params=_cparams("arbitrary"),
        name="prep",
    )(xc, xl, mod, mod)


INPROJ_TN = 512
N_PLAIN_TILES = GATE_COL0 // INPROJ_TN


F32_TILE0 = 7
N_F32_TILES = 2
P16_COLS = P_COLS - N_F32_TILES * INPROJ_TN


def _inproj_kernel(h_ref, wa_ref, wb_ref, b_ref, o_ref, *, src_tile):
    j = src_tile(pl.program_id(1))

    @pl.when(j < N_PLAIN_TILES)
    def _():
        o_ref[...] = (lax.dot_general(h_ref[...], wa_ref[...].astype(BF16), _NT, preferred_element_type=F32)
                      + b_ref[...]).astype(o_ref.dtype)

    @pl.when(j >= N_PLAIN_TILES)
    def _():
        w = jnp.concatenate([wa_ref[N_GATES:, :], wb_ref[...]], 0)
        o_ref[...] = (lax.dot_general(h_ref[...], w.astype(BF16), _NT, preferred_element_type=F32)
                      + b_ref[...]).astype(o_ref.dtype)


def _inproj_call(h, w_t, b_main, l, tm, n_tiles, src_tile, dtype, name):
    tn = INPROJ_TN
    return pl.pallas_call(
        functools.partial(_inproj_kernel, src_tile=src_tile),
        out_shape=jax.ShapeDtypeStruct((N_TOK, n_tiles * tn), dtype),
        grid=(N_TOK // tm, n_tiles),
        in_specs=[pl.BlockSpec((tm, D_MODEL), lambda i, j: (i, 0)),
                  pl.BlockSpec((None, tn, D_MODEL), lambda i, j: (l, src_tile(j), 0)),
                  pl.BlockSpec((None, N_GATES, D_MODEL), lambda i, j: (l, (src_tile(j) + 1) * (tn // N_GATES), 0)),
                  pl.BlockSpec((1, tn), lambda i, j: (0, src_tile(j)))],
        out_specs=pl.BlockSpec((tm, tn), lambda i, j: (i, j)),
        compiler_params=_cparams("arbitrary", "arbitrary"),
        name=name,
    )(h, w_t, w_t, b_main)


def _inproj(h, w_t, b_main, l):
    skip_f32 = lambda j: jnp.where(j < F32_TILE0, j, j + N_F32_TILES)
    p16 = _inproj_call(h, w_t, b_main, l, 4096, P16_COLS // INPROJ_TN, skip_f32, BF16, "inproj")
    p32 = _inproj_call(h, w_t, b_main, l, 2048, N_F32_TILES, lambda j: j + F32_TILE0, F32, "inproj_f32")
    return p16, p32


def _gates_kernel(h_ref, w_ref, b_ref, gc_ref, gt_ref):
    g = lax.dot_general(h_ref[...], w_ref[...].astype(BF16), _NT, preferred_element_type=F32) + b_ref[...]
    gc_ref[...] = g
    gt_ref[...] = g.T[:N_GATES]


def _gates(h, w_t, b_in3, l):
    tm = 1024
    gblk = GATE_COL0 // 128
    return pl.pallas_call(
        _gates_kernel,
        out_shape=(jax.ShapeDtypeStruct((N_TOK, 128), F32), jax.ShapeDtypeStruct((N_GATES, N_TOK), F32)),
        grid=(N_TOK // tm,),
        in_specs=[pl.BlockSpec((tm, D_MODEL), lambda i: (i, 0)),
                  pl.BlockSpec((None, 128, D_MODEL), lambda i: (l, gblk, 0)),
                  pl.BlockSpec((None, 1, 128), lambda i: (l, 0, gblk))],
        out_specs=(pl.BlockSpec((tm, 128), lambda i: (i, 0)), pl.BlockSpec((N_GATES, tm), lambda i: (0, i))),
        compiler_params=_cparams("arbitrary"),
        name="gates",
    )(h, w_t, b_in3)


HEADS_PER_BLK = 128 // NA_DH
NA_NBLK = NA_HEADS // HEADS_PER_BLK
NA_QSCALE = NA_DH ** -0.5
Q_COL, K_COL, V_COL = 16, 20, 24
QKV_COL = 4


def _ctx_attn_kernel(*refs):
    q_ref, k_ref, v_ref = refs[:3]
    o_ref, ko_ref, vo_ref = refs[-3:]
    heads = range(NA_HEADS)
    split = lambda x: jnp.stack([x[:, h * NA_DH:(h + 1) * NA_DH] for h in heads], 0)
    q = split(q_ref[...] * NA_QSCALE)
    k = split(k_ref[...])
    v = split(v_ref[...])
    ko_ref[...] = k.astype(F32)
    vo_ref[...] = v.astype(F32)
    s = lax.dot_general(q, k, (((2,), (2,)), ((0,), (0,))), preferred_element_type=F32)
    e = jnp.exp(s - jnp.max(s, -1, keepdims=True))
    p = e * (1.0 / jnp.sum(e, -1, keepdims=True))
    o = lax.dot_general(p.astype(BF16), v, (((2,), (1,)), ((0,), (0,))), preferred_element_type=F32)
    o_ref[...] = jnp.concatenate([o[h] for h in heads], -1).astype(BF16)


def _ctx_attention(p16, l, prev_k=None, prev_v=None):
    kv_shape = jax.ShapeDtypeStruct((BATCH, DEPTH, NA_HEADS, SEQ, NA_DH), F32)
    kv_spec = pl.BlockSpec((None, None, NA_HEADS, SEQ, NA_DH), lambda b: (b, l, 0, 0, 0))
    col = lambda j: pl.BlockSpec((SEQ, BRANCH_W), lambda b: (b, j))
    in_specs = [col(QKV_COL), col(QKV_COL + 1), col(QKV_COL + 2)]
    args = [p16, p16, p16]
    aliases = {}
    if prev_k is not None:
        in_specs += [pl.BlockSpec(memory_space=pl.ANY)] * 2
        args += [prev_k, prev_v]
        aliases = {3: 1, 4: 2}
    return pl.pallas_call(
        _ctx_attn_kernel,
        out_shape=(jax.ShapeDtypeStruct((N_TOK, BRANCH_W), BF16), kv_shape, kv_shape),
        grid=(BATCH,),
        in_specs=in_specs,
        out_specs=(pl.BlockSpec((SEQ, BRANCH_W), lambda b: (b, 0)), kv_spec, kv_spec),
        input_output_aliases=aliases,
        compiler_params=_cparams("arbitrary"),
        name="ctx_attention",
    )(*args)


NA_ROWS = DEC_SEQ // GRID_W
NA_KR = min(NA_KR_MAX, NA_ROWS)
NA_QROWS = 4
NA_QT = NA_ROWS // NA_QROWS
NA_WROWS = NA_KR + NA_QROWS - 1
NA_WKEYS = NA_WROWS * GRID_W


def _na_window_start(t):
    return min(max(t * NA_QROWS - NA_KR // 2, 0), NA_ROWS - NA_WROWS)


def _na_bias_table(rpb):
    c = np.arange(GRID_W)
    c0 = np.clip(c - NA_KC // 2, 0, GRID_W - NA_KC)
    kc = np.arange(GRID_W)
    valid = (kc[None, :] >= c0[:, None]) & (kc[None, :] < c0[:, None] + NA_KC)
    dc = kc[None, :] - c[:, None] + NA_KC - 1
    onehot = (dc[None] == np.arange(2 * NA_KC - 1)[:, None, None]) & valid[None]
    toep = jnp.einsum('hrd,dcx->hrcx', rpb.astype(F32), jnp.asarray(onehot, F32), precision=HI)
    toep = jnp.where(valid[None, None], toep, -jnp.inf)
    ninf = jnp.full((NA_HEADS, GRID_W, GRID_W), -jnp.inf, F32)
    tiles = []
    for t in range(NA_QT):
        w0 = _na_window_start(t)
        qrows = []
        for r in range(t * NA_QROWS, (t + 1) * NA_QROWS):
            r0 = min(max(r - NA_KR // 2, 0), NA_ROWS - NA_KR)
            blocks = []
            for kr in range(w0, w0 + NA_WROWS):
                inside = r0 <= kr < r0 + NA_KR
                blocks.append(toep[:, kr - r + NA_KR_MAX - 1] if inside else ninf)
            qrows.append(jnp.concatenate(blocks, -1))
        tiles.append(jnp.concatenate(qrows, 1))
    return jnp.stack(tiles, 1)


def _lat_attn_kernel(q_ref, k_ref, v_ref, ck_ref, cv_ref, bias_ref, prev_ref, o_ref):
    q = (q_ref[...] * NA_QSCALE).astype(BF16)
    k = k_ref[...].astype(BF16)
    v = v_ref[...].astype(BF16)
    nq = NA_QROWS * GRID_W
    for t in range(NA_QT):
        w0 = _na_window_start(t)
        qs = slice(t * nq, (t + 1) * nq)
        ws = slice(w0 * GRID_W, (w0 + NA_WROWS) * GRID_W)
        outs = []
        for hh in range(HEADS_PER_BLK):
            sl = slice(hh * NA_DH, (hh + 1) * NA_DH)
            qh = q[qs, sl]
            s_loc = lax.dot_general(qh, k[ws, sl], _NT, preferred_element_type=F32) + bias_ref[hh, t]
            s_ctx = lax.dot_general(qh, ck_ref[hh].astype(BF16), _NT, preferred_element_type=F32)
            m = jnp.maximum(jnp.max(s_loc, -1, keepdims=True), jnp.max(s_ctx, -1, keepdims=True))
            e_loc = jnp.exp(s_loc - m)
            e_ctx = jnp.exp(s_ctx - m)
            inv = 1.0 / (jnp.sum(e_loc, -1, keepdims=True) + jnp.sum(e_ctx, -1, keepdims=True))
            acc = (jnp.dot(e_loc.astype(BF16), v[ws, sl], preferred_element_type=F32)
                   + jnp.dot(e_ctx.astype(BF16), cv_ref[hh].astype(BF16), preferred_element_type=F32))
            outs.append(acc * inv)
        o_ref[qs, :] = jnp.concatenate(outs, -1).astype(BF16)


def _lat_attention(proj, ck, cv, bias, l, b_out):
    rb0 = N_CTX // DEC_SEQ
    cb = lambda base: (lambda j, b: (rb0 + b, base + j))
    c_spec = pl.BlockSpec((None, None, HEADS_PER_BLK, PAST_LEN, NA_DH), lambda j, b: (b, l, j, 0, 0))
    return pl.pallas_call(
        _lat_attn_kernel,
        out_shape=jax.ShapeDtypeStruct((N_TOK, BRANCH_W), BF16),
        grid=(NA_NBLK, DEC_BATCH),
        in_specs=[pl.BlockSpec((DEC_SEQ, 128), cb(Q_COL)), pl.BlockSpec((DEC_SEQ, 128), cb(K_COL)),
                  pl.BlockSpec((DEC_SEQ, 128), cb(V_COL)), c_spec, c_spec,
                  pl.BlockSpec((HEADS_PER_BLK, NA_QT, NA_QROWS * GRID_W, NA_WKEYS), lambda j, b: (j, 0, 0, 0)),
                  pl.BlockSpec(memory_space=pl.ANY)],
        out_specs=pl.BlockSpec((DEC_SEQ, 128), lambda j, b: (rb0 + b, j)),
        input_output_aliases={6: 0},
        compiler_params=_cparams("arbitrary", "arbitrary"),
        name="lat_attention",
    )(proj, proj, proj, ck, cv, bias, b_out)


MA_KSCALE = MA_DK ** -0.5


def _rope_tables(T):
    t = np.arange(T)
    half = MA_DK // 2
    inv = ROPE_BASE ** (-jnp.arange(0, half, 2, dtype=F32) / half)
    ang_r = jnp.asarray((t // GRID_W).astype(np.float32))[:, None] * inv[None, :]
    ang_c = jnp.asarray((t % GRID_W).astype(np.float32))[:, None] * inv[None, :]
    cos = jnp.concatenate([jnp.cos(ang_r)] * 2 + [jnp.cos(ang_c)] * 2, -1)
    sin = jnp.concatenate([-jnp.sin(ang_r), jnp.sin(ang_r), -jnp.sin(ang_c), jnp.sin(ang_c)], -1)
    return cos, sin


def _mlstm_kernel(*refs, T, latent):
    if latent:
        (p_ref, gc_ref, gt_ref, fbc_ref, fbr_ref, cos_ref, sin_ref, c0_ref, n0_ref, m0_ref, prev_ref,
         a_ref, qs, ks, vT, hfT, hbT, CT, ns, ms, brs, kcs) = refs
    else:
        p_ref, gc_ref, gt_ref, fbc_ref, fbr_ref = refs[:5]
        a_ref, co_ref, no_ref, mo_ref, qs, ks, vT, hfT, hbT, CT, ns, ms, brs, kcs = refs[-14:]
    L = MA_CHUNK
    NC = T // L
    W = BRANCH_W
    PER = 128 // L

    lane = lax.broadcasted_iota(jnp.int32, (T, MA_DK), 1)
    lo_half = (lane % (MA_DK // 2)) < (MA_DK // 4)

    def rope(x):
        if not latent:
            return x
        swapped = jnp.where(lo_half, pltpu.roll(x, MA_DK - MA_DK // 4, 1), pltpu.roll(x, MA_DK // 4, 1))
        return x * cos_ref[...] + swapped * sin_ref[...]

    for h in range(MA_HEADS):
        hs = slice(h * MA_DK, (h + 1) * MA_DK)
        qs[:, hs] = rope(p_ref[:, hs].astype(F32)).astype(BF16)
        ks[:, hs] = rope(p_ref[:, W + h * MA_DK:W + (h + 1) * MA_DK].astype(F32) * MA_KSCALE).astype(BF16)

    def v_block(tb, carry):
        r0 = pl.multiple_of(tb * 128, 128)
        for h in range(MA_HEADS):
            hs = slice(h * MA_DV, (h + 1) * MA_DV)
            blk = p_ref[pl.ds(r0, 128), 2 * W + h * MA_DV:2 * W + (h + 1) * MA_DV].astype(F32).T.astype(BF16)
            for j in range(PER):
                vT[tb * PER + j, h] = blk[:, j * L:(j + 1) * L]
        return carry

    lax.fori_loop(0, T // 128, v_block, 0)

    for d in range(2):
        for h in range(MA_HEADS):
            sidx = d * MA_HEADS + h
            CT[d, h] = c0_ref[d, h].T if latent else jnp.zeros((MA_DV, MA_DK), F32)
            ns[d, h] = n0_ref[sidx:sidx + 1, :] if latent else jnp.zeros((1, MA_DK), F32)
            ms[d, h] = m0_ref[sidx:sidx + 1, :] if latent else jnp.zeros((1, 128), F32)

    low = _tri(L, False)
    upp = _tri(L, True)
    rr = lax.broadcasted_iota(jnp.int32, (L, L), 0)
    cc = lax.broadcasted_iota(jnp.int32, (L, L), 1)
    fbc = fbc_ref[...]
    fbr = fbr_ref[...]

    def gate_sums(c, carry):
        t0 = pl.multiple_of(c * L, L)
        gc = gc_ref[pl.ds(t0, L), :]
        lfc = _log_sigmoid(gc + fbc)
        lfr = _log_sigmoid(gt_ref[c] + fbr)
        ish = pltpu.roll(gc, MA_HEADS, 1)
        brs[0, c] = jnp.dot(lfr, upp, precision=HI, preferred_element_type=F32)
        brs[1, c] = jnp.dot(lfr, low, precision=HI, preferred_element_type=F32)
        kcs[0, pl.ds(t0, L), :] = ish - jnp.dot(low, lfc, precision=HI, preferred_element_type=F32)
        kcs[1, pl.ds(t0, L), :] = ish - jnp.dot(upp, lfc, precision=HI, preferred_element_type=F32)
        return carry

    lax.fori_loop(0, NC, gate_sums, 0)

    def chunk(c, d):
        t0 = pl.multiple_of(c * L, L)
        gt = gt_ref[c]
        brow = brs[d, c]
        kc = kcs[d, pl.ds(t0, L), :]
        mask = (rr <= cc) if d == 0 else (rr >= cc)
        last = L - 1 if d == 0 else 0
        heads = range(MA_HEADS)
        gi0 = 2 * d * MA_HEADS
        gf0 = gi0 + MA_HEADS
        br = jnp.stack([brow[gf0 + h:gf0 + h + 1, :] for h in heads], 0)
        ir = jnp.stack([gt[gi0 + h:gi0 + h + 1, :] for h in heads], 0)
        kcol = jnp.stack([kc[:, gf0 + h:gf0 + h + 1] for h in heads], 0)
        m = ms[d][:, :, 0:1]
        n = ns[d]
        ct = CT[d]
        qc = qs[pl.ds(t0, L), :]
        kc = ks[pl.ds(t0, L), :]
        q = jnp.stack([qc[:, h * MA_DK:(h + 1) * MA_DK] for h in heads], 0)
        k = jnp.stack([kc[:, h * MA_DK:(h + 1) * MA_DK] for h in heads], 0)
        vt = vT[c]
        bnt = (((2,), (2,)), ((0,), (0,)))
        bnn = (((2,), (1,)), ((0,), (0,)))
        dmat = jnp.where(mask, br + kcol, -jnp.inf)
        g = br + m
        m_t = jnp.maximum(g, jnp.max(dmat, 1, keepdims=True))
        w_inter = jnp.exp(g - m_t)
        s = lax.dot_general(k, q, bnt, preferred_element_type=F32) * jnp.exp(dmat - m_t)
        ctn = jnp.concatenate([ct.astype(BF16), jnp.broadcast_to(n, (MA_HEADS, 8, MA_DK)).astype(BF16)], 1)
        cq = lax.dot_general(ctn, q, bnt, preferred_element_type=F32)
        num = w_inter * cq[:, :MA_DV] + lax.dot_general(vt, s.astype(BF16), bnn, preferred_element_type=F32)
        den = w_inter * cq[:, MA_DV:MA_DV + 1] + jnp.sum(s, 1, keepdims=True)
        (hfT if d == 0 else hbT)[c] = num / jnp.maximum(jnp.abs(den), jnp.exp(-m_t))
        m_new = m_t[:, :, last:last + 1]
        b_last = br[:, :, last:last + 1]
        decay = jnp.exp(b_last + m - m_new)
        wk = jnp.exp(b_last - br + ir - m_new)
        wk_hi = wk.astype(BF16)
        wk_lo = (wk - wk_hi.astype(F32)).astype(BF16)
        lhs = jnp.concatenate([(vt.astype(F32) * wk).astype(BF16), wk_hi, wk_lo,
                               jnp.zeros((MA_HEADS, 6, L), BF16)], 1)
        upd = lax.dot_general(lhs, k, bnn, preferred_element_type=F32)
        CT[d] = decay * ct + upd[:, :MA_DV]
        ns[d] = decay * n + upd[:, MA_DV:MA_DV + 1] + upd[:, MA_DV + 1:MA_DV + 2]
        ms[d] = jnp.broadcast_to(m_new, (MA_HEADS, 1, 128))

    def body(i, carry):
        chunk(i, 0)
        chunk(NC - 1 - i, 1)
        return carry

    lax.fori_loop(0, NC, body, 0)

    def out_block(tb, carry):
        r0 = pl.multiple_of(tb * 128, 128)
        hsum = jnp.concatenate([hfT[tb * PER + j] + hbT[tb * PER + j] for j in range(PER)], 2)
        outs = [hsum[h].T for h in range(MA_HEADS)]
        gate = jax.nn.sigmoid(p_ref[pl.ds(r0, 128), 3 * W:4 * W].astype(F32))
        a_ref[pl.ds(r0, 128), :] = (gate * jnp.concatenate(outs, 1)).astype(BF16)
        return carry

    lax.fori_loop(0, T // 128, out_block, 0)
    if not latent:
        for d in range(2):
            for h in range(MA_HEADS):
                sidx = d * MA_HEADS + h
                co_ref[d, h] = CT[d, h].T
                no_ref[sidx:sidx + 1, :] = ns[d, h]
                mo_ref[sidx:sidx + 1, :] = ms[d, h]


def _mlstm(proj, gcol, gt3, fbias_l, l, latent, C0=None, n0=None, m0=None, a_out=None, prev=None):
    T = DEC_SEQ if latent else SEQ
    B = DEC_BATCH if latent else BATCH
    rb0 = N_CTX // DEC_SEQ if latent else 0
    fb = fbias_l.astype(F32)
    fbc = jnp.zeros((1, 128), F32).at[0, MA_HEADS:2 * MA_HEADS].set(fb[0]).at[0, 3 * MA_HEADS:4 * MA_HEADS].set(fb[1])
    fbr = fbc[0, :N_GATES].reshape(N_GATES, 1)
    full2 = lambda b: (0, 0)
    any_spec = pl.BlockSpec(memory_space=pl.ANY)
    in_specs = [pl.BlockSpec((T, 4 * BRANCH_W), lambda b: (rb0 + b, 0)),
                pl.BlockSpec((T, 128), lambda b: (rb0 + b, 0)),
                pl.BlockSpec((T // MA_CHUNK, N_GATES, MA_CHUNK), lambda b: (rb0 + b, 0, 0)),
                pl.BlockSpec((1, 128), full2), pl.BlockSpec((N_GATES, 1), full2)]
    args = [proj, gcol, gt3, fbc, fbr]
    a_shape = jax.ShapeDtypeStruct((N_TOK, BRANCH_W), BF16)
    a_spec = pl.BlockSpec((T, BRANCH_W), lambda b: (rb0 + b, 0))
    c_spec = pl.BlockSpec((None, None, 2, MA_HEADS, MA_DK, MA_DV), lambda b: (b, l, 0, 0, 0, 0))
    nm_spec = pl.BlockSpec((None, None, 2 * MA_HEADS, 128), lambda b: (b, l, 0, 0))
    aliases = {}
    if latent:
        cos, sin = _rope_tables(T)
        nb = 2 * MA_HEADS
        in_specs += [pl.BlockSpec((T, MA_DK), full2), pl.BlockSpec((T, MA_DK), full2), c_spec, nm_spec, nm_spec, any_spec]
        args += [cos, sin, C0, n0.reshape(B, DEPTH, nb, MA_DK),
                 jnp.broadcast_to(m0.reshape(B, DEPTH, nb, 1), (B, DEPTH, nb, 128)), a_out]
        aliases = {len(args) - 1: 0}
        out_shape, out_specs = a_shape, a_spec
    else:
        nm_shape = jax.ShapeDtypeStruct((B, DEPTH, 2 * MA_HEADS, 128), F32)
        out_shape = (a_shape, jax.ShapeDtypeStruct((B, DEPTH, 2, MA_HEADS, MA_DK, MA_DV), F32), nm_shape, nm_shape)
        out_specs = (a_spec, c_spec, nm_spec, nm_spec)
        if prev is not None:
            in_specs += [any_spec] * 3
            args += list(prev)
            aliases = {len(args) - 3: 1, len(args) - 2: 2, len(args) - 1: 3}
    nc = T // MA_CHUNK
    scratch = [pltpu.VMEM((T, BRANCH_W), BF16), pltpu.VMEM((T, BRANCH_W), BF16),
               pltpu.VMEM((nc, MA_HEADS, MA_DV, MA_CHUNK), BF16),
               pltpu.VMEM((nc, MA_HEADS, MA_DV, MA_CHUNK), F32), pltpu.VMEM((nc, MA_HEADS, MA_DV, MA_CHUNK), F32),
               pltpu.VMEM((2, MA_HEADS, MA_DV, MA_DK), F32), pltpu.VMEM((2, MA_HEADS, 1, MA_DK), F32),
               pltpu.VMEM((2, MA_HEADS, 1, 128), F32),
               pltpu.VMEM((2, nc, N_GATES, MA_CHUNK), F32), pltpu.VMEM((2, T, 128), F32)]
    return pl.pallas_call(
        functools.partial(_mlstm_kernel, T=T, latent=latent),
        out_shape=out_shape, grid=(B,), in_specs=in_specs, out_specs=out_specs, scratch_shapes=scratch,
        input_output_aliases=aliases,
        compiler_params=_cparams("arbitrary"),
        name="mlstm_lat" if latent else "mlstm_ctx",
    )(*args)


HG_SUB = 8


def _hgrn_kernel(*refs, T, latent):
    ff_ref, fb_ref, q_ref, i_ref, g_ref, lbf_ref, lbb_ref = refs[:7]
    if latent:
        s0_ref = refs[7]
        c_ref, of, ob, ST, iT, As, Ks = refs[-7:]
    else:
        c_ref, so_ref, of, ob, ST, iT, As, Ks = refs[-8:]
    L = HG_CHUNK
    NC = T // L
    NB = L // HG_SUB
    DK = HG_DK

    for d in range(2):
        for h in range(HG_HEADS):
            ST[d, h] = s0_ref[d, h].T if latent else jnp.zeros((HG_DV, DK), F32)

    PER = 128 // L

    def i_block(tb, carry):
        r0 = pl.multiple_of(tb * 128, 128)
        for h in range(HG_HEADS):
            blk = i_ref[pl.ds(r0, 128), h * HG_DV:(h + 1) * HG_DV].astype(F32).T.astype(BF16)
            for j in range(PER):
                iT[tb * PER + j, h] = blk[:, j * L:(j + 1) * L]
        return carry

    lax.fori_loop(0, T // 128, i_block, 0)

    low = _tri(L, False)
    upp = _tri(L, True)
    row8 = lax.broadcasted_iota(jnp.int32, (HG_SUB, L), 0)
    lane_s = lax.broadcasted_iota(jnp.int32, (HG_SUB, L), 1)
    heads = range(HG_HEADS)
    bnt = (((2,), (2,)), ((0,), (0,)))
    bnn = (((2,), (1,)), ((0,), (0,)))
    LOG2E = 1.4426950408889634

    def split(x):
        return jnp.stack([x[:, h * DK:(h + 1) * DK] for h in heads], 0)

    def decay_sums(c, carry):
        t0 = pl.multiple_of(c * L, L)
        for d in range(2):
            fpre = (ff_ref if d == 0 else fb_ref)[pl.ds(t0, L), :]
            lb = (lbf_ref if d == 0 else lbb_ref)[...]
            f = lb + (1.0 - lb) * jax.nn.sigmoid(fpre)
            As[d, pl.ds(t0, L), :] = jnp.dot(low if d == 0 else upp, jnp.log(f) * LOG2E, precision=HI,
                                             preferred_element_type=F32)
            Ks[d, pl.ds(t0, L), :] = 1.0 - f
        return carry

    lax.fori_loop(0, NC, decay_sums, 0)

    def chunk(c, d):
        t0 = pl.multiple_of(c * L, L)
        A = split(As[d, pl.ds(t0, L), :])
        k = split(Ks[d, pl.ds(t0, L), :])
        q = split(_silu(q_ref[pl.ds(t0, L), :].astype(F32)))
        iv = split(i_ref[pl.ds(t0, L), :].astype(BF16))
        last = L - 1 if d == 0 else 0
        st = ST[d]
        o = lax.dot_general((q * jnp.exp2(A)).astype(BF16), st.astype(BF16), bnt, preferred_element_type=F32)
        a_last = A[:, last:last + 1, :]
        kd = (k * jnp.exp2(a_last - A)).astype(BF16)
        rows = []
        for I in range(NB):
            lo, hi = I * HG_SUB, (I + 1) * HG_SUB
            A_I, q_I = A[:, lo:hi], q[:, lo:hi]
            att = jnp.zeros((HG_HEADS, HG_SUB, L), F32)
            for j in range(HG_SUB):
                s = lo + j
                col = jnp.sum(q_I * k[:, s:s + 1] * jnp.exp2(A_I - A[:, s:s + 1]), -1, keepdims=True)
                keep = (lane_s == s) & ((row8 >= j) if d == 0 else (row8 <= j))
                att = jnp.where(keep, col, att)
            if d == 0 and I > 0:
                R = A[:, lo - 1:lo]
                ksc = jnp.concatenate([(k[:, :lo] * jnp.exp2(R - A[:, :lo])).astype(BF16),
                                       jnp.zeros((HG_HEADS, L - lo, DK), BF16)], 1)
                att = att + lax.dot_general((q_I * jnp.exp2(A_I - R)).astype(BF16), ksc, bnt, preferred_element_type=F32)
            if d == 1 and I < NB - 1:
                R = A[:, hi:hi + 1]
                ksc = jnp.concatenate([jnp.zeros((HG_HEADS, hi, DK), BF16),
                                       (k[:, hi:] * jnp.exp2(R - A[:, hi:])).astype(BF16)], 1)
                att = att + lax.dot_general((q_I * jnp.exp2(A_I - R)).astype(BF16), ksc, bnt, preferred_element_type=F32)
            rows.append(att)
        att = jnp.concatenate(rows, 1)
        o = o + lax.dot_general(att.astype(BF16), iv, bnn, preferred_element_type=F32)
        for h in heads:
            (of if d == 0 else ob)[pl.ds(t0, L), h * HG_DV:(h + 1) * HG_DV] = o[h]
        ST[d] = st * jnp.exp2(a_last) + lax.dot_general(iT[c], kd, bnn, preferred_element_type=F32)

    def body(i, carry):
        chunk(i, 0)
        chunk(NC - 1 - i, 1)
        return carry

    lax.fori_loop(0, NC, body, 0, unroll=2)

    def epilogue(r, carry):
        t0 = pl.multiple_of(r * 128, 128)
        o = of[pl.ds(t0, 128), :] + ob[pl.ds(t0, 128), :]
        gsil = _silu(g_ref[pl.ds(t0, 128), :].astype(F32))
        outs = []
        for h in range(HG_HEADS):
            oh = o[:, h * HG_DV:(h + 1) * HG_DV]
            outs.append(oh * lax.rsqrt(jnp.mean(oh * oh, -1, keepdims=True) + RMS_EPS))
        c_ref[pl.ds(t0, 128), :] = (jnp.concatenate(outs, -1) * gsil).astype(BF16)
        return carry

    lax.fori_loop(0, T // 128, epilogue, 0)
    if not latent:
        for d in range(2):
            for h in range(HG_HEADS):
                so_ref[d, h] = ST[d, h].T


def _hgrn(p16, p32, lb_l, l, latent, S0=None, c_out=None, prev=None):
    T = DEC_SEQ if latent else SEQ
    B = DEC_BATCH if latent else BATCH
    rb0 = N_CTX // DEC_SEQ if latent else 0
    W = BRANCH_W
    full2 = lambda b: (0, 0)
    any_spec = pl.BlockSpec(memory_space=pl.ANY)
    col = lambda j: pl.BlockSpec((T, W), lambda b: (rb0 + b, j))
    s_spec = pl.BlockSpec((None, None, 2, HG_HEADS, HG_DK, HG_DV), lambda b: (b, l, 0, 0, 0, 0))
    in_specs = [col(0), col(1), col(7), col(8), col(9), pl.BlockSpec((1, W), full2), pl.BlockSpec((1, W), full2)]
    args = [p32, p32, p16, p16, p16, lb_l[0][None, :], lb_l[1][None, :]]
    c_shape = jax.ShapeDtypeStruct((N_TOK, W), BF16)
    c_spec = pl.BlockSpec((T, W), lambda b: (rb0 + b, 0))
    aliases = {}
    if latent:
        in_specs += [s_spec, any_spec]
        args += [S0, c_out]
        aliases = {8: 0}
        out_shape, out_specs = c_shape, c_spec
    else:
        out_shape = (c_shape, jax.ShapeDtypeStruct((B, DEPTH, 2, HG_HEADS, HG_DK, HG_DV), F32))
        out_specs = (c_spec, s_spec)
        if prev is not None:
            in_specs.append(any_spec)
            args.append(prev)
            aliases = {7: 1}
    scratch = [pltpu.VMEM((T, W), F32), pltpu.VMEM((T, W), F32), pltpu.VMEM((2, HG_HEADS, HG_DV, HG_DK), F32),
               pltpu.VMEM((T // HG_CHUNK, HG_HEADS, HG_DV, HG_CHUNK), BF16),
               pltpu.VMEM((2, T, W), F32), pltpu.VMEM((2, T, W), F32)]
    return pl.pallas_call(
        functools.partial(_hgrn_kernel, T=T, latent=latent),
        out_shape=out_shape, grid=(B,), in_specs=in_specs, out_specs=out_specs, scratch_shapes=scratch,
        input_output_aliases=aliases,
        compiler_params=_cparams("arbitrary"),
        name="hgrn_lat" if latent else "hgrn_ctx",
    )(*args)


def _merge_kernel(a_ref, b_ref, c_ref, ga_ref, gb_ref, gc_ref, xc_ref, xl_ref, g1_ref, sh2_ref, sc2_ref,
                  wb_ref, wo_ref, lng_ref, lnb_ref, wr_ref, x1_ref, h2_ref, *, tm):
    def br(v_ref, g_ref, k):
        return jax.nn.sigmoid(g_ref[...].astype(F32)) * jnp.dot(v_ref[...], wb_ref[k], preferred_element_type=F32)

    mix = br(a_ref, ga_ref, 0) + br(b_ref, gb_ref, 1) + br(c_ref, gc_ref, 2)
    y = jnp.dot(mix.astype(BF16), wo_ref[...], preferred_element_type=F32)
    x = _pair_read(pl.program_id(0), tm, xc_ref, xl_ref)
    x1 = _layer_norm(DEEPNORM_ALPHA * x + g1_ref[...] * y, lng_ref[...], lnb_ref[...])
    x1_ref[...] = x1
    h2 = x1 * (1.0 + sc2_ref[...]) + sh2_ref[...]
    h2_ref[:, :D_MODEL] = h2.astype(BF16)
    lt = lax.dot_general(wr_ref[...], h2, _NT, preferred_element_type=F32, precision=HI)
    r = lax.broadcasted_iota(jnp.int32, lt.shape, 0)
    neg = -jnp.inf
    lg = jnp.where(r < N_GROUPS, lt, neg)
    mg = jnp.max(lg, 0, keepdims=True)
    g_sel = jnp.min(jnp.where(lg == mg, r, ROUTER_ROWS), 0, keepdims=True)
    p_sel = 1.0 / jnp.sum(jnp.where(r < N_GROUPS, jnp.exp(lg - mg), 0.0), 0, keepdims=True)
    lo = ROUTER_E0 + EXP_PER_GROUP * g_sel
    le = jnp.where((r >= lo) & (r < lo + EXP_PER_GROUP), lt, neg)
    v1 = jnp.max(le, 0, keepdims=True)
    i1 = jnp.min(jnp.where(le == v1, r, ROUTER_ROWS), 0, keepdims=True)
    le2 = jnp.where(r == i1, neg, le)
    v2 = jnp.max(le2, 0, keepdims=True)
    i2 = jnp.min(jnp.where(le2 == v2, r, ROUTER_ROWS), 0, keepdims=True)
    e2 = jnp.exp(v2 - v1)
    w1 = p_sel / (1.0 + e2)
    w2 = p_sel * e2 / (1.0 + e2)
    w1_hi = w1.astype(BF16).astype(F32)
    w2_hi = w2.astype(BF16).astype(F32)
    j1, j2 = i1 - lo, i2 - lo
    packed = jnp.where(r == j1, w1_hi, jnp.where(r == j2, w2_hi, jnp.where(
        r == j1 + EXP_PER_GROUP, w1 - w1_hi, jnp.where(r == j2 + EXP_PER_GROUP, w2 - w2_hi, jnp.where(
            r == 2 * EXP_PER_GROUP, g_sel.astype(F32), 0.0)))))
    packed = jnp.concatenate([packed, jnp.zeros((128 - ROUTER_ROWS, packed.shape[1]), F32)], 0)
    h2_ref[:, D_MODEL:] = packed.T.astype(BF16)


def _merge(a, b, c, p16, xc, xl, mod, wb, wo, lng, lnb, wr, l):
    tm = 512
    tok = lambda i: (i, 0)
    ln_spec = pl.BlockSpec((None, None, 1, D_MODEL), lambda i: (l, 0, 0, 0))
    return pl.pallas_call(
        functools.partial(_merge_kernel, tm=tm),
        out_shape=(jax.ShapeDtypeStruct((N_TOK, D_MODEL), F32), jax.ShapeDtypeStruct((N_TOK, MOE_XW), BF16)),
        grid=(N_TOK // tm,),
        in_specs=[pl.BlockSpec((tm, BRANCH_W), tok), pl.BlockSpec((tm, BRANCH_W), tok), pl.BlockSpec((tm, BRANCH_W), tok),
                  pl.BlockSpec((tm, D_MODEL), lambda i: (i, 5)), pl.BlockSpec((tm, D_MODEL), lambda i: (i, 6)),
                  pl.BlockSpec((tm, D_MODEL), lambda i: (i, 7)),
                  *_pair_specs(tm), _mod_spec(l, 2, tm), _mod_spec(l, 3, tm), _mod_spec(l, 4, tm),
                  pl.BlockSpec((None, 3, BRANCH_W, D_MODEL), lambda i: (l, 0, 0, 0)),
                  pl.BlockSpec((None, D_MODEL, D_MODEL), lambda i: (l, 0, 0)),
                  ln_spec, ln_spec,
                  pl.BlockSpec((None, ROUTER_ROWS, D_MODEL), lambda i: (l, 0, 0))],
        out_specs=(pl.BlockSpec((tm, D_MODEL), tok), pl.BlockSpec((tm, MOE_XW), tok)),
        compiler_params=_cparams("arbitrary"),
        name="merge",
    )(a, b, c, p16, p16, p16, xc, xl, mod, mod, mod, wb, wo, lng, lnb, wr)


def _moe_up_kernel(gid_ref, nused_ref, x_ref, w1_ref, w3_ref, hid_ref, w1b, w3b):
    f = pl.program_id(0)
    t = pl.program_id(1)

    @pl.when(t < nused_ref[0])
    def _():
        first = jnp.logical_or(t == 0, gid_ref[t] != gid_ref[jnp.maximum(t - 1, 0)])

        @pl.when(first)
        def _():
            w1b[...] = w1_ref[...].astype(BF16)
            w3b[...] = w3_ref[...].astype(BF16)

        x = x_ref[:, :D_MODEL]
        rec = x_ref[:, D_MODEL:].astype(F32)
        lane = lax.broadcasted_iota(jnp.int32, rec.shape, 1)
        for j in range(MOE_FE):
            e = f * MOE_FE + j
            a = jnp.dot(x, w1b[j], preferred_element_type=F32)
            b = jnp.dot(x, w3b[j], preferred_element_type=F32)
            gcol = jnp.sum(jnp.where(jnp.logical_or(lane == e, lane == e + EXP_PER_GROUP), rec, 0.0), -1, keepdims=True)
            hid_ref[:, j * D_EXPERT:(j + 1) * D_EXPERT] = (_silu(a) * b * gcol).astype(BF16)


def _moe_tile(t, n):
    return jnp.minimum(t, n[0] - 1)


def _moe_up(gid, nused, xs, w1, w3, l):
    tm = MOE_TM
    npad = MOE_NT * tm
    nf = EXP_PER_GROUP // MOE_FE
    w_spec = pl.BlockSpec((None, MOE_FE, D_MODEL, D_EXPERT),
                          lambda f, t, g, n: (l, nf * g[_moe_tile(t, n)] + f, 0, 0))
    grid_spec = pltpu.PrefetchScalarGridSpec(
        num_scalar_prefetch=2,
        grid=(nf, MOE_NT),
        in_specs=[pl.BlockSpec((tm, MOE_XW), lambda f, t, g, n: (_moe_tile(t, n), 0)), w_spec, w_spec],
        out_specs=pl.BlockSpec((tm, MOE_FE * D_EXPERT), lambda f, t, g, n: (_moe_tile(t, n), f)),
        scratch_shapes=[pltpu.VMEM((MOE_FE, D_MODEL, D_EXPERT), BF16), pltpu.VMEM((MOE_FE, D_MODEL, D_EXPERT), BF16)],
    )
    return pl.pallas_call(
        _moe_up_kernel,
        out_shape=jax.ShapeDtypeStruct((npad, EXP_PER_GROUP * D_EXPERT), BF16),
        grid_spec=grid_spec,
        compiler_params=_cparams("arbitrary", "arbitrary"),
        name="moe_up",
    )(gid, nused, xs, w1, w3)


def _moe_down_kernel(gid_ref, nused_ref, hid_ref, w2_ref, y_ref, w2b):
    t = pl.program_id(0)

    @pl.when(t < nused_ref[0])
    def _():
        first = jnp.logical_or(t == 0, gid_ref[t] != gid_ref[jnp.maximum(t - 1, 0)])

        @pl.when(first)
        def _():
            w2b[...] = w2_ref[...].astype(BF16)

        y_ref[...] = jnp.dot(hid_ref[...], w2b[...], preferred_element_type=F32)


def _moe_down(gid, nused, hid, w2g, l):
    tm = MOE_TM
    npad = MOE_NT * tm
    hw = EXP_PER_GROUP * D_EXPERT
    grid_spec = pltpu.PrefetchScalarGridSpec(
        num_scalar_prefetch=2,
        grid=(MOE_NT,),
        in_specs=[pl.BlockSpec((tm, hw), lambda t, g, n: (_moe_tile(t, n), 0)),
                  pl.BlockSpec((None, None, hw, D_MODEL), lambda t, g, n: (l, g[_moe_tile(t, n)], 0, 0))],
        out_specs=pl.BlockSpec((tm, D_MODEL), lambda t, g, n: (_moe_tile(t, n), 0)),
        scratch_shapes=[pltpu.VMEM((hw, D_MODEL), BF16)],
    )
    return pl.pallas_call(
        _moe_down_kernel,
        out_shape=jax.ShapeDtypeStruct((npad, D_MODEL), F32),
        grid_spec=grid_spec,
        compiler_params=_cparams("arbitrary"),
        name="moe_down",
    )(gid, nused, hid, w2g)


def _moe(h2x, w1, w3, w2g, l):
    tm = MOE_TM
    npad = MOE_NT * tm
    g = h2x[:, D_MODEL + 2 * EXP_PER_GROUP].astype(jnp.int32)
    onehot = (g[:, None] == jnp.arange(N_GROUPS)[None, :]).astype(jnp.int32)
    counts = jnp.sum(onehot, 0)
    rank = jnp.sum((jnp.cumsum(onehot, 0) - onehot) * onehot, 1)
    padded = (counts + tm - 1) // tm * tm
    ends = jnp.cumsum(padded)
    offs = ends - padded
    dest = offs[g] + rank
    src = jnp.zeros((npad,), jnp.int32).at[dest].set(jnp.arange(N_TOK, dtype=jnp.int32), unique_indices=True)
    starts = jnp.arange(MOE_NT, dtype=jnp.int32) * tm
    tile_gid = jnp.minimum(jnp.sum((ends[None, :] <= starts[:, None]).astype(jnp.int32), 1), N_GROUPS - 1)
    nused = (ends[-1:] // tm).astype(jnp.int32)
    take = lambda arr, idx: arr.at[idx].get(mode="promise_in_bounds", unique_indices=False)
    hid = _moe_up(tile_gid, nused, take(h2x, src), w1, w3, l)
    ys = _moe_down(tile_gid, nused, hid, w2g, l)
    return take(ys, dest)


def _final_kernel(*refs, tm, with_h):
    x1_ref, y_ref, g2_ref, lng_ref, lnb_ref = refs[:5]
    x2 = _layer_norm(DEEPNORM_ALPHA * x1_ref[...] + g2_ref[...] * y_ref[...], lng_ref[...], lnb_ref[...])
    i = pl.program_id(0)
    if with_h:
        sh_ref, sc_ref, xc_ref, xl_ref, h_ref = refs[5:]
        h_ref[...] = (x2 * (1.0 + sc_ref[...]) + sh_ref[...]).astype(BF16)
    else:
        xc_ref, xl_ref = refs[5:]

    @pl.when(i < N_CTX // tm)
    def _():
        xc_ref[...] = x2

    @pl.when(i >= N_CTX // tm)
    def _():
        xl_ref[...] = x2


def _final(x1, y, mod, lng, lnb, l):
    tm = 1024
    tok = lambda i: (i, 0)
    with_h = l + 1 < DEPTH
    ln_spec = pl.BlockSpec((None, None, 1, D_MODEL), lambda i: (l, 1, 0, 0))
    half = jax.ShapeDtypeStruct((N_CTX, D_MODEL), F32)
    in_specs = [pl.BlockSpec((tm, D_MODEL), tok), pl.BlockSpec((tm, D_MODEL), tok), _mod_spec(l, 5, tm), ln_spec, ln_spec]
    args = [x1, y, mod, lng, lnb]
    out_shape = [half, half]
    out_specs = list(_pair_specs(tm))
    if with_h:
        in_specs += [_mod_spec(l + 1, 0, tm), _mod_spec(l + 1, 1, tm)]
        args += [mod, mod]
        out_shape.append(jax.ShapeDtypeStruct((N_TOK, D_MODEL), BF16))
        out_specs.append(pl.BlockSpec((tm, D_MODEL), tok))
    return pl.pallas_call(
        functools.partial(_final_kernel, tm=tm, with_h=with_h),
        out_shape=tuple(out_shape), grid=(N_TOK // tm,), in_specs=in_specs, out_specs=tuple(out_specs),
        compiler_params=_cparams("arbitrary"),
        name="final",
    )(*args)


def kernel(x_prompt, x_sample, c, cache_na_k, cache_na_v, state_mlstm_C, state_mlstm_n, state_mlstm_m, state_hgrn,
           c_ctx, w_mod, b_mod, w_in, b_in, mlstm_fbias, hgrn_lb_logits, na_rpb, w_branch, w_out, ln_g, ln_b,
           w_rg, w_re, w_e1, w_e3, w_e2):
    assert N_CTX == N_LAT
    lb_cum = jnp.cumsum(jax.nn.softmax(hgrn_lb_logits.astype(F32), axis=1), axis=1)
    lb_all = lb_cum - lb_cum[:, :1]

    cs = jnp.zeros((N_MODROWS, D_MODEL), F32).at[0].set(c_ctx).at[1:1 + DEC_BATCH].set(c)
    mod = _modulation(cs, w_mod, b_mod).reshape(DEPTH, N_MODROWS, 6, 1, D_MODEL)

    wb = w_branch.astype(BF16)
    wo = w_out.astype(BF16)
    lng = ln_g.reshape(DEPTH, 2, 1, D_MODEL)
    lnb = ln_b.reshape(DEPTH, 2, 1, D_MODEL)
    wr = jnp.zeros((DEPTH, ROUTER_ROWS, D_MODEL), F32)
    wr = wr.at[:, :N_GROUPS].set(jnp.swapaxes(w_rg, 1, 2)).at[:, ROUTER_E0:ROUTER_E0 + N_EXPERTS].set(jnp.swapaxes(w_re, 1, 2))
    w2g = w_e2.reshape(DEPTH, N_GROUPS, EXP_PER_GROUP * D_EXPERT, D_MODEL)
    b_main = jnp.concatenate([b_in[:, :GATE_COL0], b_in[:, GATE_COL0 + N_GATES:]], 1)

    xc = x_prompt.reshape(N_CTX, D_MODEL)
    xl = x_sample.reshape(N_LAT, D_MODEL)
    w_t = jnp.swapaxes(w_in, 1, 2)
    h = _prep(xc, xl, mod)

    kv = (None, None)
    ma_states = None
    hg_state = None
    for l in range(DEPTH):
        p16, p32 = _inproj(h, w_t, b_main[l][None, :], l)
        gcol, gt = _gates(h, w_t, b_in.reshape(DEPTH, 1, N_IN), l)
        gt3 = gt.reshape(N_GATES, N_TOK // MA_CHUNK, MA_CHUNK).transpose(1, 0, 2)

        a, *ma_states = _mlstm(p16, gcol, gt3, mlstm_fbias[l], l, False, prev=ma_states)
        a = _mlstm(p16, gcol, gt3, mlstm_fbias[l], l, True, state_mlstm_C, state_mlstm_n, state_mlstm_m, a_out=a)
        b, *kv = _ctx_attention(p16, l, *kv)
        b = _lat_attention(p16, cache_na_k, cache_na_v, _na_bias_table(na_rpb[l]), l, b)
        cc, hg_state = _hgrn(p16, p32, lb_all[:, l], l, False, prev=hg_state)
        cc = _hgrn(p16, p32, lb_all[:, l], l, True, state_hgrn, c_out=cc)

        x1, h2x = _merge(a, b, cc, p16, xc, xl, mod, wb, wo, lng, lnb, wr, l)
        y2 = _moe(h2x, w_e1, w_e3, w2g, l)
        outs = _final(x1, y2, mod, lng, lnb, l)
        xc, xl = outs[0], outs[1]
        if l + 1 < DEPTH:
            h = outs[2]

    dt = x_prompt.dtype
    new_C, new_n, new_m = ma_states
    new_n = new_n.reshape(BATCH, DEPTH, 2, MA_HEADS, MA_DK)
    new_m = new_m[:, :, :, 0].reshape(BATCH, DEPTH, 2, MA_HEADS)
    return (xc.reshape(BATCH, SEQ, D_MODEL), xl.reshape(DEC_BATCH, DEC_SEQ, D_MODEL), kv[0], kv[1],
            new_C.astype(dt), new_n.astype(dt), new_m.astype(dt), hg_state.astype(dt))
```

```python
import functools

import numpy as np
import jax
import jax.numpy as jnp
from jax import lax
from jax.experimental import pallas as pl
from jax.experimental.pallas import tpu as pltpu

F32 = jnp.float32
BF16 = jnp.bfloat16
HI = lax.Precision.HIGHEST

D_MODEL = 1024
BATCH = 16
SEQ = 256
DEPTH = 2
DEC_BATCH = 4
DEC_SEQ = 1024
PAST_LEN = 256
GRID_W = 64
MA_HEADS = 4
MA_DK = 128
MA_DV = 128
MA_CHUNK = 64
NA_HEADS = 8
NA_DH = 64
NA_KR_MAX = 8
NA_KC = 16
HG_HEADS = 4
HG_DK = 128
HG_DV = 128
HG_CHUNK = 32
BRANCH_W = 512
N_GROUPS = 4
EXP_PER_GROUP = 4
N_EXPERTS = N_GROUPS * EXP_PER_GROUP
D_EXPERT = 512
ROPE_BASE = 10000.0
LN_EPS = 1e-5
RMS_EPS = 1e-6
DEEPNORM_ALPHA = (2 * DEPTH) ** 0.25

N_CTX = BATCH * SEQ
N_LAT = DEC_BATCH * DEC_SEQ
N_TOK = N_CTX + N_LAT
N_MODROWS = 8
GATE_COL0 = 4 * BRANCH_W
N_GATES = 4 * MA_HEADS
N_IN = 9232
P_COLS = N_IN - N_GATES
MOE_TM = 512
MOE_NT = N_TOK // MOE_TM + N_GROUPS
MOE_FE = 2
MOE_XW = D_MODEL + 128
ROUTER_ROWS = 32
ROUTER_E0 = 8
VMEM_LIMIT = 48 * 1024 * 1024

_NT = (((1,), (1,)), ((), ()))
_TN = (((0,), (0,)), ((), ()))


def _cparams(*sem):
    return pltpu.CompilerParams(dimension_semantics=sem, vmem_limit_bytes=VMEM_LIMIT)


def _mod_row(tile, tm):
    return jnp.maximum((tile * tm) // DEC_SEQ - (N_CTX // DEC_SEQ - 1), 0)


def _mod_spec(l, part, tm):
    return pl.BlockSpec((None, None, None, 1, D_MODEL), lambda i: (l, _mod_row(i, tm), part, 0, 0))


def _pair_specs(tm):
    nc = N_CTX // tm
    return (pl.BlockSpec((tm, D_MODEL), lambda i: (jnp.minimum(i, nc - 1), 0)),
            pl.BlockSpec((tm, D_MODEL), lambda i: (jnp.maximum(i - nc, 0), 0)))


def _pair_read(i, tm, c_ref, l_ref):
    return jnp.where(i < N_CTX // tm, c_ref[...], l_ref[...])


def _silu(x):
    return x * jax.nn.sigmoid(x)


def _layer_norm(x, g, b):
    mu = jnp.mean(x, -1, keepdims=True)
    xc = x - mu
    var = jnp.mean(xc * xc, -1, keepdims=True)
    return xc * lax.rsqrt(var + LN_EPS) * g + b


def _log_sigmoid(x):
    return jnp.minimum(x, 0.0) - jnp.log(1.0 + jnp.exp(-jnp.abs(x)))


def _tri(n, upper):
    r = lax.broadcasted_iota(jnp.int32, (n, n), 0)
    c = lax.broadcasted_iota(jnp.int32, (n, n), 1)
    return jnp.where((r <= c) if upper else (r >= c), 1.0, 0.0).astype(F32)


def _mod_kernel(c_ref, w_ref, b_ref, o_ref):
    s = _silu(c_ref[...])
    o_ref[...] = jnp.dot(s.astype(BF16), w_ref[...].astype(BF16), preferred_element_type=F32) + b_ref[...]


def _modulation(cs, w_mod, b_mod):
    tn = 1024
    return pl.pallas_call(
        _mod_kernel,
        out_shape=jax.ShapeDtypeStruct((DEPTH, N_MODROWS, 6 * D_MODEL), F32),
        grid=(DEPTH, 6 * D_MODEL // tn),
        in_specs=[pl.BlockSpec((N_MODROWS, D_MODEL), lambda l, j: (0, 0)),
                  pl.BlockSpec((None, D_MODEL, tn), lambda l, j: (l, 0, j)),
                  pl.BlockSpec((None, 1, tn), lambda l, j: (l, 0, j))],
        out_specs=pl.BlockSpec((None, N_MODROWS, tn), lambda l, j: (l, 0, j)),
        compiler_params=_cparams("arbitrary", "arbitrary"),
        name="modulation",
    )(cs, w_mod, b_mod.reshape(DEPTH, 1, 6 * D_MODEL))


def _prep_kernel(xc_ref, xl_ref, sh_ref, sc_ref, h_ref, *, tm):
    x = _pair_read(pl.program_id(0), tm, xc_ref, xl_ref)
    h_ref[...] = (x * (1.0 + sc_ref[...]) + sh_ref[...]).astype(BF16)


def _prep(xc, xl, mod):
    tm = 1024
    return pl.pallas_call(
        functools.partial(_prep_kernel, tm=tm),
        out_shape=jax.ShapeDtypeStruct((N_TOK, D_MODEL), BF16),
        grid=(N_TOK // tm,),
        in_specs=[*_pair_specs(tm), _mod_spec(0, 0, tm), _mod_spec(0, 1, tm)],
        out_specs=pl.BlockSpec((tm, D_MODEL), lambda i: (i, 0)),
        compiler_params=_cparams("arbitrary"),
        name="prep",
    )(xc, xl, mod, mod)


INPROJ_TN = 512
N_PLAIN_TILES = GATE_COL0 // INPROJ_TN


F32_TILE0 = 7
N_F32_TILES = 2
P16_COLS = P_COLS - N_F32_TILES * INPROJ_TN


def _inproj_kernel(h_ref, wa_ref, wb_ref, b_ref, o_ref, *, src_tile):
    j = src_tile(pl.program_id(1))

    @pl.when(j < N_PLAIN_TILES)
    def _():
        o_ref[...] = (lax.dot_general(h_ref[...], wa_ref[...].astype(BF16), _NT, preferred_element_type=F32)
                      + b_ref[...]).astype(o_ref.dtype)

    @pl.when(j >= N_PLAIN_TILES)
    def _():
        w = jnp.concatenate([wa_ref[N_GATES:, :], wb_ref[...]], 0)
        o_ref[...] = (lax.dot_general(h_ref[...], w.astype(BF16), _NT, preferred_element_type=F32)
                      + b_ref[...]).astype(o_ref.dtype)


def _inproj_call(h, w_t, b_main, l, tm, n_tiles, src_tile, dtype, name):
    tn = INPROJ_TN
    return pl.pallas_call(
        functools.partial(_inproj_kernel, src_tile=src_tile),
        out_shape=jax.ShapeDtypeStruct((N_TOK, n_tiles * tn), dtype),
        grid=(N_TOK // tm, n_tiles),
        in_specs=[pl.BlockSpec((tm, D_MODEL), lambda i, j: (i, 0)),
                  pl.BlockSpec((None, tn, D_MODEL), lambda i, j: (l, src_tile(j), 0)),
                  pl.BlockSpec((None, N_GATES, D_MODEL), lambda i, j: (l, (src_tile(j) + 1) * (tn // N_GATES), 0)),
                  pl.BlockSpec((1, tn), lambda i, j: (0, src_tile(j)))],
        out_specs=pl.BlockSpec((tm, tn), lambda i, j: (i, j)),
        compiler_params=_cparams("arbitrary", "arbitrary"),
        name=name,
    )(h, w_t, w_t, b_main)


def _inproj(h, w_t, b_main, l):
    skip_f32 = lambda j: jnp.where(j < F32_TILE0, j, j + N_F32_TILES)
    p16 = _inproj_call(h, w_t, b_main, l, 4096, P16_COLS // INPROJ_TN, skip_f32, BF16, "inproj")
    p32 = _inproj_call(h, w_t, b_main, l, 2048, N_F32_TILES, lambda j: j + F32_TILE0, F32, "inproj_f32")
    return p16, p32


def _gates_kernel(h_ref, w_ref, b_ref, gc_ref, gt_ref):
    g = lax.dot_general(h_ref[...], w_ref[...].astype(BF16), _NT, preferred_element_type=F32) + b_ref[...]
    gc_ref[...] = g
    gt_ref[...] = g.T[:N_GATES]


def _gates(h, w_t, b_in3, l):
    tm = 1024
    gblk = GATE_COL0 // 128
    return pl.pallas_call(
        _gates_kernel,
        out_shape=(jax.ShapeDtypeStruct((N_TOK, 128), F32), jax.ShapeDtypeStruct((N_GATES, N_TOK), F32)),
        grid=(N_TOK // tm,),
        in_specs=[pl.BlockSpec((tm, D_MODEL), lambda i: (i, 0)),
                  pl.BlockSpec((None, 128, D_MODEL), lambda i: (l, gblk, 0)),
                  pl.BlockSpec((None, 1, 128), lambda i: (l, 0, gblk))],
        out_specs=(pl.BlockSpec((tm, 128), lambda i: (i, 0)), pl.BlockSpec((N_GATES, tm), lambda i: (0, i))),
        compiler_params=_cparams("arbitrary"),
        name="gates",
    )(h, w_t, b_in3)


HEADS_PER_BLK = 128 // NA_DH
NA_NBLK = NA_HEADS // HEADS_PER_BLK
NA_QSCALE = NA_DH ** -0.5
Q_COL, K_COL, V_COL = 16, 20, 24
QKV_COL = 4


def _ctx_attn_kernel(*refs):
    q_ref, k_ref, v_ref = refs[:3]
    o_ref, ko_ref, vo_ref = refs[-3:]
    heads = range(NA_HEADS)
    split = lambda x: jnp.stack([x[:, h * NA_DH:(h + 1) * NA_DH] for h in heads], 0)
    q = split(q_ref[...] * NA_QSCALE)
    k = split(k_ref[...])
    v = split(v_ref[...])
    ko_ref[...] = k.astype(F32)
    vo_ref[...] = v.astype(F32)
    s = lax.dot_general(q, k, (((2,), (2,)), ((0,), (0,))), preferred_element_type=F32)
    e = jnp.exp(s - jnp.max(s, -1, keepdims=True))
    p = e * (1.0 / jnp.sum(e, -1, keepdims=True))
    o = lax.dot_general(p.astype(BF16), v, (((2,), (1,)), ((0,), (0,))), preferred_element_type=F32)
    o_ref[...] = jnp.concatenate([o[h] for h in heads], -1).astype(BF16)


def _ctx_attention(p16, l, prev_k=None, prev_v=None):
    kv_shape = jax.ShapeDtypeStruct((BATCH, DEPTH, NA_HEADS, SEQ, NA_DH), F32)
    kv_spec = pl.BlockSpec((None, None, NA_HEADS, SEQ, NA_DH), lambda b: (b, l, 0, 0, 0))
    col = lambda j: pl.BlockSpec((SEQ, BRANCH_W), lambda b: (b, j))
    in_specs = [col(QKV_COL), col(QKV_COL + 1), col(QKV_COL + 2)]
    args = [p16, p16, p16]
    aliases = {}
    if prev_k is not None:
        in_specs += [pl.BlockSpec(memory_space=pl.ANY)] * 2
        args += [prev_k, prev_v]
        aliases = {3: 1, 4: 2}
    return pl.pallas_call(
        _ctx_attn_kernel,
        out_shape=(jax.ShapeDtypeStruct((N_TOK, BRANCH_W), BF16), kv_shape, kv_shape),
        grid=(BATCH,),
        in_specs=in_specs,
        out_specs=(pl.BlockSpec((SEQ, BRANCH_W), lambda b: (b, 0)), kv_spec, kv_spec),
        input_output_aliases=aliases,
        compiler_params=_cparams("arbitrary"),
        name="ctx_attention",
    )(*args)


NA_ROWS = DEC_SEQ // GRID_W
NA_KR = min(NA_KR_MAX, NA_ROWS)
NA_QROWS = 4
NA_QT = NA_ROWS // NA_QROWS
NA_WROWS = NA_KR + NA_QROWS - 1
NA_WKEYS = NA_WROWS * GRID_W


def _na_window_start(t):
    return min(max(t * NA_QROWS - NA_KR // 2, 0), NA_ROWS - NA_WROWS)


def _na_bias_table(rpb):
    c = np.arange(GRID_W)
    c0 = np.clip(c - NA_KC // 2, 0, GRID_W - NA_KC)
    kc = np.arange(GRID_W)
    valid = (kc[None, :] >= c0[:, None]) & (kc[None, :] < c0[:, None] + NA_KC)
    dc = kc[None, :] - c[:, None] + NA_KC - 1
    onehot = (dc[None] == np.arange(2 * NA_KC - 1)[:, None, None]) & valid[None]
    toep = jnp.einsum('hrd,dcx->hrcx', rpb.astype(F32), jnp.asarray(onehot, F32), precision=HI)
    toep = jnp.where(valid[None, None], toep, -jnp.inf)
    ninf = jnp.full((NA_HEADS, GRID_W, GRID_W), -jnp.inf, F32)
    tiles = []
    for t in range(NA_QT):
        w0 = _na_window_start(t)
        qrows = []
        for r in range(t * NA_QROWS, (t + 1) * NA_QROWS):
            r0 = min(max(r - NA_KR // 2, 0), NA_ROWS - NA_KR)
            blocks = []
            for kr in range(w0, w0 + NA_WROWS):
                inside = r0 <= kr < r0 + NA_KR
                blocks.append(toep[:, kr - r + NA_KR_MAX - 1] if inside else ninf)
            qrows.append(jnp.concatenate(blocks, -1))
        tiles.append(jnp.concatenate(qrows, 1))
    return jnp.stack(tiles, 1)


def _lat_attn_kernel(q_ref, k_ref, v_ref, ck_ref, cv_ref, bias_ref, prev_ref, o_ref):
    q = (q_ref[...] * NA_QSCALE).astype(BF16)
    k = k_ref[...].astype(BF16)
    v = v_ref[...].astype(BF16)
    nq = NA_QROWS * GRID_W
    for t in range(NA_QT):
        w0 = _na_window_start(t)
        qs = slice(t * nq, (t + 1) * nq)
        ws = slice(w0 * GRID_W, (w0 + NA_WROWS) * GRID_W)
        outs = []
        for hh in range(HEADS_PER_BLK):
            sl = slice(hh * NA_DH, (hh + 1) * NA_DH)
            qh = q[qs, sl]
            s_loc = lax.dot_general(qh, k[ws, sl], _NT, preferred_element_type=F32) + bias_ref[hh, t]
            s_ctx = lax.dot_general(qh, ck_ref[hh].astype(BF16), _NT, preferred_element_type=F32)
            m = jnp.maximum(jnp.max(s_loc, -1, keepdims=True), jnp.max(s_ctx, -1, keepdims=True))
            e_loc = jnp.exp(s_loc - m)
            e_ctx = jnp.exp(s_ctx - m)
            inv = 1.0 / (jnp.sum(e_loc, -1, keepdims=True) + jnp.sum(e_ctx, -1, keepdims=True))
            acc = (jnp.dot(e_loc.astype(BF16), v[ws, sl], preferred_element_type=F32)
                   + jnp.dot(e_ctx.astype(BF16), cv_ref[hh].astype(BF16), preferred_element_type=F32))
            outs.append(acc * inv)
        o_ref[qs, :] = jnp.concatenate(outs, -1).astype(BF16)


def _lat_attention(proj, ck, cv, bias, l, b_out):
    rb0 = N_CTX // DEC_SEQ
    cb = lambda base: (lambda j, b: (rb0 + b, base + j))
    c_spec = pl.BlockSpec((None, None, HEADS_PER_BLK, PAST_LEN, NA_DH), lambda j, b: (b, l, j, 0, 0))
    return pl.pallas_call(
        _lat_attn_kernel,
        out_shape=jax.ShapeDtypeStruct((N_TOK, BRANCH_W), BF16),
        grid=(NA_NBLK, DEC_BATCH),
        in_specs=[pl.BlockSpec((DEC_SEQ, 128), cb(Q_COL)), pl.BlockSpec((DEC_SEQ, 128), cb(K_COL)),
                  pl.BlockSpec((DEC_SEQ, 128), cb(V_COL)), c_spec, c_spec,
                  pl.BlockSpec((HEADS_PER_BLK, NA_QT, NA_QROWS * GRID_W, NA_WKEYS), lambda j, b: (j, 0, 0, 0)),
                  pl.BlockSpec(memory_space=pl.ANY)],
        out_specs=pl.BlockSpec((DEC_SEQ, 128), lambda j, b: (rb0 + b, j)),
        input_output_aliases={6: 0},
        compiler_params=_cparams("arbitrary", "arbitrary"),
        name="lat_attention",
    )(proj, proj, proj, ck, cv, bias, b_out)


MA_KSCALE = MA_DK ** -0.5


def _rope_tables(T):
    t = np.arange(T)
    half = MA_DK // 2
    inv = ROPE_BASE ** (-jnp.arange(0, half, 2, dtype=F32) / half)
    ang_r = jnp.asarray((t // GRID_W).astype(np.float32))[:, None] * inv[None, :]
    ang_c = jnp.asarray((t % GRID_W).astype(np.float32))[:, None] * inv[None, :]
    cos = jnp.concatenate([jnp.cos(ang_r)] * 2 + [jnp.cos(ang_c)] * 2, -1)
    sin = jnp.concatenate([-jnp.sin(ang_r), jnp.sin(ang_r), -jnp.sin(ang_c), jnp.sin(ang_c)], -1)
    return cos, sin


def _mlstm_kernel(*refs, T, latent):
    if latent:
        (p_ref, gc_ref, gt_ref, fbc_ref, fbr_ref, cos_ref, sin_ref, c0_ref, n0_ref, m0_ref, prev_ref,
         a_ref, qs, ks, vT, hfT, hbT, CT, ns, ms, brs, kcs) = refs
    else:
        p_ref, gc_ref, gt_ref, fbc_ref, fbr_ref = refs[:5]
        a_ref, co_ref, no_ref, mo_ref, qs, ks, vT, hfT, hbT, CT, ns, ms, brs, kcs = refs[-14:]
    L = MA_CHUNK
    NC = T // L
    W = BRANCH_W
    PER = 128 // L

    lane = lax.broadcasted_iota(jnp.int32, (T, MA_DK), 1)
    lo_half = (lane % (MA_DK // 2)) < (MA_DK // 4)

    def rope(x):
        if not latent:
            return x
        swapped = jnp.where(lo_half, pltpu.roll(x, MA_DK - MA_DK // 4, 1), pltpu.roll(x, MA_DK // 4, 1))
        return x * cos_ref[...] + swapped * sin_ref[...]

    for h in range(MA_HEADS):
        hs = slice(h * MA_DK, (h + 1) * MA_DK)
        qs[:, hs] = rope(p_ref[:, hs].astype(F32)).astype(BF16)
        ks[:, hs] = rope(p_ref[:, W + h * MA_DK:W + (h + 1) * MA_DK].astype(F32) * MA_KSCALE).astype(BF16)

    def v_block(tb, carry):
        r0 = pl.multiple_of(tb * 128, 128)
        for h in range(MA_HEADS):
            hs = slice(h * MA_DV, (h + 1) * MA_DV)
            blk = p_ref[pl.ds(r0, 128), 2 * W + h * MA_DV:2 * W + (h + 1) * MA_DV].astype(F32).T.astype(BF16)
            for j in range(PER):
                vT[tb * PER + j, h] = blk[:, j * L:(j + 1) * L]
        return carry

    lax.fori_loop(0, T // 128, v_block, 0)

    for d in range(2):
        for h in range(MA_HEADS):
            sidx = d * MA_HEADS + h
            CT[d, h] = c0_ref[d, h].T if latent else jnp.zeros((MA_DV, MA_DK), F32)
            ns[d, h] = n0_ref[sidx:sidx + 1, :] if latent else jnp.zeros((1, MA_DK), F32)
            ms[d, h] = m0_ref[sidx:sidx + 1, :] if latent else jnp.zeros((1, 128), F32)

    low = _tri(L, False)
    upp = _tri(L, True)
    rr = lax.broadcasted_iota(jnp.int32, (L, L), 0)
    cc = lax.broadcasted_iota(jnp.int32, (L, L), 1)
    fbc = fbc_ref[...]
    fbr = fbr_ref[...]

    def gate_sums(c, carry):
        t0 = pl.multiple_of(c * L, L)
        gc = gc_ref[pl.ds(t0, L), :]
        lfc = _log_sigmoid(gc + fbc)
        lfr = _log_sigmoid(gt_ref[c] + fbr)
        ish = pltpu.roll(gc, MA_HEADS, 1)
        brs[0, c] = jnp.dot(lfr, upp, precision=HI, preferred_element_type=F32)
        brs[1, c] = jnp.dot(lfr, low, precision=HI, preferred_element_type=F32)
        kcs[0, pl.ds(t0, L), :] = ish - jnp.dot(low, lfc, precision=HI, preferred_element_type=F32)
        kcs[1, pl.ds(t0, L), :] = ish - jnp.dot(upp, lfc, precision=HI, preferred_element_type=F32)
        return carry

    lax.fori_loop(0, NC, gate_sums, 0, unroll=4)

    def chunk(c, d):
        t0 = pl.multiple_of(c * L, L)
        gt = gt_ref[c]
        brow = brs[d, c]
        kc = kcs[d, pl.ds(t0, L), :]
        mask = (rr <= cc) if d == 0 else (rr >= cc)
        last = L - 1 if d == 0 else 0
        heads = range(MA_HEADS)
        gi0 = 2 * d * MA_HEADS
        gf0 = gi0 + MA_HEADS
        br = jnp.stack([brow[gf0 + h:gf0 + h + 1, :] for h in heads], 0)
        ir = jnp.stack([gt[gi0 + h:gi0 + h + 1, :] for h in heads], 0)
        kcol = jnp.stack([kc[:, gf0 + h:gf0 + h + 1] for h in heads], 0)
        m = ms[d][:, :, 0:1]
        n = ns[d]
        ct = CT[d]
        qc = qs[pl.ds(t0, L), :]
        kc = ks[pl.ds(t0, L), :]
        q = jnp.stack([qc[:, h * MA_DK:(h + 1) * MA_DK] for h in heads], 0)
        k = jnp.stack([kc[:, h * MA_DK:(h + 1) * MA_DK] for h in heads], 0)
        vt = vT[c]
        bnt = (((2,), (2,)), ((0,), (0,)))
        bnn = (((2,), (1,)), ((0,), (0,)))
        dmat = jnp.where(mask, br + kcol, -jnp.inf)
        g = br + m
        m_t = jnp.maximum(g, jnp.max(dmat, 1, keepdims=True))
        w_inter = jnp.exp(g - m_t)
        s = lax.dot_general(k, q, bnt, preferred_element_type=F32) * jnp.exp(dmat - m_t)
        ctn = jnp.concatenate([ct.astype(BF16), jnp.broadcast_to(n, (MA_HEADS, 8, MA_DK)).astype(BF16)], 1)
        cq = lax.dot_general(ctn, q, bnt, preferred_element_type=F32)
        num = w_inter * cq[:, :MA_DV] + lax.dot_general(vt, s.astype(BF16), bnn, preferred_element_type=F32)
        den = w_inter * cq[:, MA_DV:MA_DV + 1] + jnp.sum(s, 1, keepdims=True)
        (hfT if d == 0 else hbT)[c] = num / jnp.maximum(jnp.abs(den), jnp.exp(-m_t))
        m_new = m_t[:, :, last:last + 1]
        b_last = br[:, :, last:last + 1]
        decay = jnp.exp(b_last + m - m_new)
        wk = jnp.exp(b_last - br + ir - m_new)
        wk_hi = wk.astype(BF16)
        wk_lo = (wk - wk_hi.astype(F32)).astype(BF16)
        lhs = jnp.concatenate([(vt.astype(F32) * wk).astype(BF16), wk_hi, wk_lo,
                               jnp.zeros((MA_HEADS, 6, L), BF16)], 1)
        upd = lax.dot_general(lhs, k, bnn, preferred_element_type=F32)
        CT[d] = decay * ct + upd[:, :MA_DV]
        ns[d] = decay * n + upd[:, MA_DV:MA_DV + 1] + upd[:, MA_DV + 1:MA_DV + 2]
        ms[d] = jnp.broadcast_to(m_new, (MA_HEADS, 1, 128))

    def body(i, carry):
        chunk(i, 0)
        chunk(NC - 1 - i, 1)
        return carry

    lax.fori_loop(0, NC, body, 0, unroll=2)

    def out_block(tb, carry):
        r0 = pl.multiple_of(tb * 128, 128)
        hsum = jnp.concatenate([hfT[tb * PER + j] + hbT[tb * PER + j] for j in range(PER)], 2)
        outs = [hsum[h].T for h in range(MA_HEADS)]
        gate = jax.nn.sigmoid(p_ref[pl.ds(r0, 128), 3 * W:4 * W].astype(F32))
        a_ref[pl.ds(r0, 128), :] = (gate * jnp.concatenate(outs, 1)).astype(BF16)
        return carry

    lax.fori_loop(0, T // 128, out_block, 0)
    if not latent:
        for d in range(2):
            for h in range(MA_HEADS):
                sidx = d * MA_HEADS + h
                co_ref[d, h] = CT[d, h].T
                no_ref[sidx:sidx + 1, :] = ns[d, h]
                mo_ref[sidx:sidx + 1, :] = ms[d, h]


def _mlstm(proj, gcol, gt3, fbias_l, l, latent, C0=None, n0=None, m0=None, a_out=None, prev=None):
    T = DEC_SEQ if latent else SEQ
    B = DEC_BATCH if latent else BATCH
    rb0 = N_CTX // DEC_SEQ if latent else 0
    fb = fbias_l.astype(F32)
    fbc = jnp.zeros((1, 128), F32).at[0, MA_HEADS:2 * MA_HEADS].set(fb[0]).at[0, 3 * MA_HEADS:4 * MA_HEADS].set(fb[1])
    fbr = fbc[0, :N_GATES].reshape(N_GATES, 1)
    full2 = lambda b: (0, 0)
    any_spec = pl.BlockSpec(memory_space=pl.ANY)
    in_specs = [pl.BlockSpec((T, 4 * BRANCH_W), lambda b: (rb0 + b, 0)),
                pl.BlockSpec((T, 128), lambda b: (rb0 + b, 0)),
                pl.BlockSpec((T // MA_CHUNK, N_GATES, MA_CHUNK), lambda b: (rb0 + b, 0, 0)),
                pl.BlockSpec((1, 128), full2), pl.BlockSpec((N_GATES, 1), full2)]
    args = [proj, gcol, gt3, fbc, fbr]
    a_shape = jax.ShapeDtypeStruct((N_TOK, BRANCH_W), BF16)
    a_spec = pl.BlockSpec((T, BRANCH_W), lambda b: (rb0 + b, 0))
    c_spec = pl.BlockSpec((None, None, 2, MA_HEADS, MA_DK, MA_DV), lambda b: (b, l, 0, 0, 0, 0))
    nm_spec = pl.BlockSpec((None, None, 2 * MA_HEADS, 128), lambda b: (b, l, 0, 0))
    aliases = {}
    if latent:
        cos, sin = _rope_tables(T)
        nb = 2 * MA_HEADS
        in_specs += [pl.BlockSpec((T, MA_DK), full2), pl.BlockSpec((T, MA_DK), full2), c_spec, nm_spec, nm_spec, any_spec]
        args += [cos, sin, C0, n0.reshape(B, DEPTH, nb, MA_DK),
                 jnp.broadcast_to(m0.reshape(B, DEPTH, nb, 1), (B, DEPTH, nb, 128)), a_out]
        aliases = {len(args) - 1: 0}
        out_shape, out_specs = a_shape, a_spec
    else:
        nm_shape = jax.ShapeDtypeStruct((B, DEPTH, 2 * MA_HEADS, 128), F32)
        out_shape = (a_shape, jax.ShapeDtypeStruct((B, DEPTH, 2, MA_HEADS, MA_DK, MA_DV), F32), nm_shape, nm_shape)
        out_specs = (a_spec, c_spec, nm_spec, nm_spec)
        if prev is not None:
            in_specs += [any_spec] * 3
            args += list(prev)
            aliases = {len(args) - 3: 1, len(args) - 2: 2, len(args) - 1: 3}
    nc = T // MA_CHUNK
    scratch = [pltpu.VMEM((T, BRANCH_W), BF16), pltpu.VMEM((T, BRANCH_W), BF16),
               pltpu.VMEM((nc, MA_HEADS, MA_DV, MA_CHUNK), BF16),
               pltpu.VMEM((nc, MA_HEADS, MA_DV, MA_CHUNK), F32), pltpu.VMEM((nc, MA_HEADS, MA_DV, MA_CHUNK), F32),
               pltpu.VMEM((2, MA_HEADS, MA_DV, MA_DK), F32), pltpu.VMEM((2, MA_HEADS, 1, MA_DK), F32),
               pltpu.VMEM((2, MA_HEADS, 1, 128), F32),
               pltpu.VMEM((2, nc, N_GATES, MA_CHUNK), F32), pltpu.VMEM((2, T, 128), F32)]
    return pl.pallas_call(
        functools.partial(_mlstm_kernel, T=T, latent=latent),
        out_shape=out_shape, grid=(B,), in_specs=in_specs, out_specs=out_specs, scratch_shapes=scratch,
        input_output_aliases=aliases,
        compiler_params=_cparams("arbitrary"),
        name="mlstm_lat" if latent else "mlstm_ctx",
    )(*args)


HG_SUB = 8


def _hgrn_kernel(*refs, T, latent):
    ff_ref, fb_ref, q_ref, i_ref, g_ref, lbf_ref, lbb_ref = refs[:7]
    if latent:
        s0_ref = refs[7]
        c_ref, of, ob, ST, iT, As, Bs = refs[-7:]
    else:
        c_ref, so_ref, of, ob, ST, iT, As, Bs = refs[-8:]
    L = HG_CHUNK
    NC = T // L
    NB = L // HG_SUB
    DK = HG_DK

    for d in range(2):
        for h in range(HG_HEADS):
            ST[d * HG_HEADS + h] = s0_ref[d, h].T if latent else jnp.zeros((HG_DV, DK), F32)

    PER = 128 // L

    def i_block(tb, carry):
        r0 = pl.multiple_of(tb * 128, 128)
        for h in range(HG_HEADS):
            blk = i_ref[pl.ds(r0, 128), h * HG_DV:(h + 1) * HG_DV].astype(F32).T.astype(BF16)
            for j in range(PER):
                iT[tb * PER + j, h] = blk[:, j * L:(j + 1) * L]
        return carry

    lax.fori_loop(0, T // 128, i_block, 0)

    low = _tri(L, False)
    upp = _tri(L, True)
    row8 = lax.broadcasted_iota(jnp.int32, (HG_SUB, L), 0)
    lane_s = lax.broadcasted_iota(jnp.int32, (HG_SUB, L), 1)
    heads = range(HG_HEADS)
    bnt = (((2,), (2,)), ((0,), (0,)))
    bnn = (((2,), (1,)), ((0,), (0,)))
    LOG2E = 1.4426950408889634

    def split(x):
        return jnp.stack([x[:, h * DK:(h + 1) * DK] for h in heads], 0)

    def decay_sums(c, carry):
        t0 = pl.multiple_of(c * L, L)
        for d in range(2):
            fpre = (ff_ref if d == 0 else fb_ref)[pl.ds(t0, L), :]
            lb = (lbf_ref if d == 0 else lbb_ref)[...]
            f = lb + (1.0 - lb) * jax.nn.sigmoid(fpre)
            a = jnp.dot(low if d == 0 else upp, jnp.log(f) * LOG2E, precision=HI, preferred_element_type=F32)
            As[d, pl.ds(t0, L), :] = a
            Bs[d, pl.ds(t0, L), :] = a - jnp.log(1.0 - f) * LOG2E
        return carry

    lax.fori_loop(0, NC, decay_sums, 0, unroll=4)

    def chunk_pair(cf, cb):
        tf = pl.multiple_of(cf * L, L)
        tb = pl.multiple_of(cb * L, L)
        both = lambda fn: jnp.concatenate([fn(0, tf), fn(1, tb)], 0)
        A = both(lambda d, t: split(As[d, pl.ds(t, L), :]))
        B = both(lambda d, t: split(Bs[d, pl.ds(t, L), :]))
        q = both(lambda d, t: split(_silu(q_ref[pl.ds(t, L), :].astype(F32))))
        iv = both(lambda d, t: split(i_ref[pl.ds(t, L), :].astype(BF16)))
        ivT = jnp.concatenate([iT[cf], iT[cb]], 0)
        H = HG_HEADS
        st = ST[...]
        o = lax.dot_general((q * jnp.exp2(A)).astype(BF16), st.astype(BF16), bnt, preferred_element_type=F32)
        a_last = jnp.concatenate([A[:H, L - 1:L], A[H:, 0:1]], 0)
        kd = jnp.exp2(a_last - B).astype(BF16)
        rows = []
        for I in range(NB):
            lo, hi = I * HG_SUB, (I + 1) * HG_SUB
            A_I, q_I = A[:, lo:hi], q[:, lo:hi]
            att_f = jnp.zeros((H, HG_SUB, L), F32)
            att_b = jnp.zeros((H, HG_SUB, L), F32)
            for j in range(HG_SUB):
                s = lo + j
                col = jnp.sum(q_I * jnp.exp2(A_I - B[:, s:s + 1]), -1, keepdims=True)
                att_f = jnp.where((lane_s == s) & (row8 >= j), col[:H], att_f)
                att_b = jnp.where((lane_s == s) & (row8 <= j), col[H:], att_b)
            rf, rb = max(lo - 1, 0), min(hi, L - 1)
            R = jnp.concatenate([A[:H, rf:rf + 1], A[H:, rb:rb + 1]], 0)
            zeros = lambda n: jnp.zeros((H, n, DK), BF16)
            ksc_f = jnp.concatenate([jnp.exp2(R[:H] - B[:H, :lo]).astype(BF16), zeros(L - lo)], 1) if I > 0 else zeros(L)
            ksc_b = jnp.concatenate([zeros(hi), jnp.exp2(R[H:] - B[H:, hi:]).astype(BF16)], 1) if I < NB - 1 else zeros(L)
            ksc = jnp.concatenate([ksc_f, ksc_b], 0)
            off = lax.dot_general((q_I * jnp.exp2(A_I - R)).astype(BF16), ksc, bnt, preferred_element_type=F32)
            rows.append(jnp.concatenate([att_f, att_b], 0) + off)
        att = jnp.concatenate(rows, 1)
        o = o + lax.dot_general(att.astype(BF16), iv, bnn, preferred_element_type=F32)
        for h in heads:
            of[pl.ds(tf, L), h * HG_DV:(h + 1) * HG_DV] = o[h]
            ob[pl.ds(tb, L), h * HG_DV:(h + 1) * HG_DV] = o[H + h]
        ST[...] = st * jnp.exp2(a_last) + lax.dot_general(ivT, kd, bnn, preferred_element_type=F32)

    def body(i, carry):
        chunk_pair(i, NC - 1 - i)
        return carry

    lax.fori_loop(0, NC, body, 0, unroll=4)

    def epilogue(r, carry):
        t0 = pl.multiple_of(r * 128, 128)
        o = of[pl.ds(t0, 128), :] + ob[pl.ds(t0, 128), :]
        gsil = _silu(g_ref[pl.ds(t0, 128), :].astype(F32))
        outs = []
        for h in range(HG_HEADS):
            oh = o[:, h * HG_DV:(h + 1) * HG_DV]
            outs.append(oh * lax.rsqrt(jnp.mean(oh * oh, -1, keepdims=True) + RMS_EPS))
        c_ref[pl.ds(t0, 128), :] = (jnp.concatenate(outs, -1) * gsil).astype(BF16)
        return carry

    lax.fori_loop(0, T // 128, epilogue, 0)
    if not latent:
        for d in range(2):
            for h in range(HG_HEADS):
                so_ref[d, h] = ST[d * HG_HEADS + h].T


def _hgrn(p16, p32, lb_l, l, latent, S0=None, c_out=None, prev=None):
    T = DEC_SEQ if latent else SEQ
    B = DEC_BATCH if latent else BATCH
    rb0 = N_CTX // DEC_SEQ if latent else 0
    W = BRANCH_W
    full2 = lambda b: (0, 0)
    any_spec = pl.BlockSpec(memory_space=pl.ANY)
    col = lambda j: pl.BlockSpec((T, W), lambda b: (rb0 + b, j))
    s_spec = pl.BlockSpec((None, None, 2, HG_HEADS, HG_DK, HG_DV), lambda b: (b, l, 0, 0, 0, 0))
    in_specs = [col(0), col(1), col(7), col(8), col(9), pl.BlockSpec((1, W), full2), pl.BlockSpec((1, W), full2)]
    args = [p32, p32, p16, p16, p16, lb_l[0][None, :], lb_l[1][None, :]]
    c_shape = jax.ShapeDtypeStruct((N_TOK, W), BF16)
    c_spec = pl.BlockSpec((T, W), lambda b: (rb0 + b, 0))
    aliases = {}
    if latent:
        in_specs += [s_spec, any_spec]
        args += [S0, c_out]
        aliases = {8: 0}
        out_shape, out_specs = c_shape, c_spec
    else:
        out_shape = (c_shape, jax.ShapeDtypeStruct((B, DEPTH, 2, HG_HEADS, HG_DK, HG_DV), F32))
        out_specs = (c_spec, s_spec)
        if prev is not None:
            in_specs.append(any_spec)
            args.append(prev)
            aliases = {7: 1}
    scratch = [pltpu.VMEM((T, W), F32), pltpu.VMEM((T, W), F32), pltpu.VMEM((2 * HG_HEADS, HG_DV, HG_DK), F32),
               pltpu.VMEM((T // HG_CHUNK, HG_HEADS, HG_DV, HG_CHUNK), BF16),
               pltpu.VMEM((2, T, W), F32), pltpu.VMEM((2, T, W), F32)]
    return pl.pallas_call(
        functools.partial(_hgrn_kernel, T=T, latent=latent),
        out_shape=out_shape, grid=(B,), in_specs=in_specs, out_specs=out_specs, scratch_shapes=scratch,
        input_output_aliases=aliases,
        compiler_params=_cparams("arbitrary"),
        name="hgrn_lat" if latent else "hgrn_ctx",
    )(*args)


def _merge_kernel(a_ref, b_ref, c_ref, ga_ref, gb_ref, gc_ref, xc_ref, xl_ref, g1_ref, sh2_ref, sc2_ref,
                  wb_ref, wo_ref, lng_ref, lnb_ref, wr_ref, x1_ref, h2_ref, *, tm):
    def br(v_ref, g_ref, k):
        return jax.nn.sigmoid(g_ref[...].astype(F32)) * jnp.dot(v_ref[...], wb_ref[k], preferred_element_type=F32)

    mix = br(a_ref, ga_ref, 0) + br(b_ref, gb_ref, 1) + br(c_ref, gc_ref, 2)
    y = jnp.dot(mix.astype(BF16), wo_ref[...], preferred_element_type=F32)
    x = _pair_read(pl.program_id(0), tm, xc_ref, xl_ref)
    x1 = _layer_norm(DEEPNORM_ALPHA * x + g1_ref[...] * y, lng_ref[...], lnb_ref[...])
    x1_ref[...] = x1
    h2 = x1 * (1.0 + sc2_ref[...]) + sh2_ref[...]
    h2_ref[:, :D_MODEL] = h2.astype(BF16)
    lt = lax.dot_general(wr_ref[...], h2, _NT, preferred_element_type=F32, precision=HI)
    r = lax.broadcasted_iota(jnp.int32, lt.shape, 0)
    neg = -jnp.inf
    lg = jnp.where(r < N_GROUPS, lt, neg)
    mg = jnp.max(lg, 0, keepdims=True)
    g_sel = jnp.min(jnp.where(lg == mg, r, ROUTER_ROWS), 0, keepdims=True)
    p_sel = 1.0 / jnp.sum(jnp.where(r < N_GROUPS, jnp.exp(lg - mg), 0.0), 0, keepdims=True)
    lo = ROUTER_E0 + EXP_PER_GROUP * g_sel
    le = jnp.where((r >= lo) & (r < lo + EXP_PER_GROUP), lt, neg)
    v1 = jnp.max(le, 0, keepdims=True)
    i1 = jnp.min(jnp.where(le == v1, r, ROUTER_ROWS), 0, keepdims=True)
    le2 = jnp.where(r == i1, neg, le)
    v2 = jnp.max(le2, 0, keepdims=True)
    i2 = jnp.min(jnp.where(le2 == v2, r, ROUTER_ROWS), 0, keepdims=True)
    e2 = jnp.exp(v2 - v1)
    w1 = p_sel / (1.0 + e2)
    w2 = p_sel * e2 / (1.0 + e2)
    w1_hi = w1.astype(BF16).astype(F32)
    w2_hi = w2.astype(BF16).astype(F32)
    j1, j2 = i1 - lo, i2 - lo
    packed = jnp.where(r == j1, w1_hi, jnp.where(r == j2, w2_hi, jnp.where(
        r == j1 + EXP_PER_GROUP, w1 - w1_hi, jnp.where(r == j2 + EXP_PER_GROUP, w2 - w2_hi, jnp.where(
            r == 2 * EXP_PER_GROUP, g_sel.astype(F32), 0.0)))))
    packed = jnp.concatenate([packed, jnp.zeros((128 - ROUTER_ROWS, packed.shape[1]), F32)], 0)
    h2_ref[:, D_MODEL:] = packed.T.astype(BF16)


def _merge(a, b, c, p16, xc, xl, mod, wb, wo, lng, lnb, wr, l):
    tm = 512
    tok = lambda i: (i, 0)
    ln_spec = pl.BlockSpec((None, None, 1, D_MODEL), lambda i: (l, 0, 0, 0))
    return pl.pallas_call(
        functools.partial(_merge_kernel, tm=tm),
        out_shape=(jax.ShapeDtypeStruct((N_TOK, D_MODEL), F32), jax.ShapeDtypeStruct((N_TOK, MOE_XW), BF16)),
        grid=(N_TOK // tm,),
        in_specs=[pl.BlockSpec((tm, BRANCH_W), tok), pl.BlockSpec((tm, BRANCH_W), tok), pl.BlockSpec((tm, BRANCH_W), tok),
                  pl.BlockSpec((tm, D_MODEL), lambda i: (i, 5)), pl.BlockSpec((tm, D_MODEL), lambda i: (i, 6)),
                  pl.BlockSpec((tm, D_MODEL), lambda i: (i, 7)),
                  *_pair_specs(tm), _mod_spec(l, 2, tm), _mod_spec(l, 3, tm), _mod_spec(l, 4, tm),
                  pl.BlockSpec((None, 3, BRANCH_W, D_MODEL), lambda i: (l, 0, 0, 0)),
                  pl.BlockSpec((None, D_MODEL, D_MODEL), lambda i: (l, 0, 0)),
                  ln_spec, ln_spec,
                  pl.BlockSpec((None, ROUTER_ROWS, D_MODEL), lambda i: (l, 0, 0))],
        out_specs=(pl.BlockSpec((tm, D_MODEL), tok), pl.BlockSpec((tm, MOE_XW), tok)),
        compiler_params=_cparams("arbitrary"),
        name="merge",
    )(a, b, c, p16, p16, p16, xc, xl, mod, mod, mod, wb, wo, lng, lnb, wr)


def _moe_up_kernel(gid_ref, nused_ref, x_ref, w1_ref, w3_ref, hid_ref, w1b, w3b):
    f = pl.program_id(0)
    t = pl.program_id(1)

    @pl.when(t < nused_ref[0])
    def _():
        first = jnp.logical_or(t == 0, gid_ref[t] != gid_ref[jnp.maximum(t - 1, 0)])

        @pl.when(first)
        def _():
            w1b[...] = w1_ref[...].astype(BF16)
            w3b[...] = w3_ref[...].astype(BF16)

        x = x_ref[:, :D_MODEL]
        rec = x_ref[:, D_MODEL:].astype(F32)
        lane = lax.broadcasted_iota(jnp.int32, rec.shape, 1)
        for j in range(MOE_FE):
            e = f * MOE_FE + j
            a = jnp.dot(x, w1b[j], preferred_element_type=F32)
            b = jnp.dot(x, w3b[j], preferred_element_type=F32)
            gcol = jnp.sum(jnp.where(jnp.logical_or(lane == e, lane == e + EXP_PER_GROUP), rec, 0.0), -1, keepdims=True)
            hid_ref[:, j * D_EXPERT:(j + 1) * D_EXPERT] = (_silu(a) * b * gcol).astype(BF16)


def _moe_tile(t, n):
    return jnp.minimum(t, n[0] - 1)


def _moe_up(gid, nused, xs, w1, w3, l):
    tm = MOE_TM
    npad = MOE_NT * tm
    nf = EXP_PER_GROUP // MOE_FE
    w_spec = pl.BlockSpec((None, MOE_FE, D_MODEL, D_EXPERT),
                          lambda f, t, g, n: (l, nf * g[_moe_tile(t, n)] + f, 0, 0))
    grid_spec = pltpu.PrefetchScalarGridSpec(
        num_scalar_prefetch=2,
        grid=(nf, MOE_NT),
        in_specs=[pl.BlockSpec((tm, MOE_XW), lambda f, t, g, n: (_moe_tile(t, n), 0)), w_spec, w_spec],
        out_specs=pl.BlockSpec((tm, MOE_FE * D_EXPERT), lambda f, t, g, n: (_moe_tile(t, n), f)),
        scratch_shapes=[pltpu.VMEM((MOE_FE, D_MODEL, D_EXPERT), BF16), pltpu.VMEM((MOE_FE, D_MODEL, D_EXPERT), BF16)],
    )
    return pl.pallas_call(
        _moe_up_kernel,
        out_shape=jax.ShapeDtypeStruct((npad, EXP_PER_GROUP * D_EXPERT), BF16),
        grid_spec=grid_spec,
        compiler_params=_cparams("arbitrary", "arbitrary"),
        name="moe_up",
    )(gid, nused, xs, w1, w3)


def _moe_down_kernel(gid_ref, nused_ref, hid_ref, w2_ref, y_ref, w2b):
    t = pl.program_id(0)

    @pl.when(t < nused_ref[0])
    def _():
        first = jnp.logical_or(t == 0, gid_ref[t] != gid_ref[jnp.maximum(t - 1, 0)])

        @pl.when(first)
        def _():
            w2b[...] = w2_ref[...].astype(BF16)

        y_ref[...] = jnp.dot(hid_ref[...], w2b[...], preferred_element_type=F32)


def _moe_down(gid, nused, hid, w2g, l):
    tm = MOE_TM
    npad = MOE_NT * tm
    hw = EXP_PER_GROUP * D_EXPERT
    grid_spec = pltpu.PrefetchScalarGridSpec(
        num_scalar_prefetch=2,
        grid=(MOE_NT,),
        in_specs=[pl.BlockSpec((tm, hw), lambda t, g, n: (_moe_tile(t, n), 0)),
                  pl.BlockSpec((None, None, hw, D_MODEL), lambda t, g, n: (l, g[_moe_tile(t, n)], 0, 0))],
        out_specs=pl.BlockSpec((tm, D_MODEL), lambda t, g, n: (_moe_tile(t, n), 0)),
        scratch_shapes=[pltpu.VMEM((hw, D_MODEL), BF16)],
    )
    return pl.pallas_call(
        _moe_down_kernel,
        out_shape=jax.ShapeDtypeStruct((npad, D_MODEL), F32),
        grid_spec=grid_spec,
        compiler_params=_cparams("arbitrary"),
        name="moe_down",
    )(gid, nused, hid, w2g)


def _moe(h2x, w1, w3, w2g, l):
    tm = MOE_TM
    npad = MOE_NT * tm
    g = h2x[:, D_MODEL + 2 * EXP_PER_GROUP].astype(jnp.int32)
    onehot = (g[:, None] == jnp.arange(N_GROUPS)[None, :]).astype(jnp.int32)
    counts = jnp.sum(onehot, 0)
    rank = jnp.sum((jnp.cumsum(onehot, 0) - onehot) * onehot, 1)
    padded = (counts + tm - 1) // tm * tm
    ends = jnp.cumsum(padded)
    offs = ends - padded
    dest = offs[g] + rank
    src = jnp.zeros((npad,), jnp.int32).at[dest].set(jnp.arange(N_TOK, dtype=jnp.int32), unique_indices=True)
    starts = jnp.arange(MOE_NT, dtype=jnp.int32) * tm
    tile_gid = jnp.minimum(jnp.sum((ends[None, :] <= starts[:, None]).astype(jnp.int32), 1), N_GROUPS - 1)
    nused = (ends[-1:] // tm).astype(jnp.int32)
    take = lambda arr, idx: arr.at[idx].get(mode="promise_in_bounds", unique_indices=False)
    hid = _moe_up(tile_gid, nused, take(h2x, src), w1, w3, l)
    ys = _moe_down(tile_gid, nused, hid, w2g, l)
    return take(ys, dest)


def _final_kernel(*refs, tm, with_h):
    x1_ref, y_ref, g2_ref, lng_ref, lnb_ref = refs[:5]
    x2 = _layer_norm(DEEPNORM_ALPHA * x1_ref[...] + g2_ref[...] * y_ref[...], lng_ref[...], lnb_ref[...])
    i = pl.program_id(0)
    if with_h:
        sh_ref, sc_ref, xc_ref, xl_ref, h_ref = refs[5:]
        h_ref[...] = (x2 * (1.0 + sc_ref[...]) + sh_ref[...]).astype(BF16)
    else:
        xc_ref, xl_ref = refs[5:]

    @pl.when(i < N_CTX // tm)
    def _():
        xc_ref[...] = x2

    @pl.when(i >= N_CTX // tm)
    def _():
        xl_ref[...] = x2


def _final(x1, y, mod, lng, lnb, l):
    tm = 1024
    tok = lambda i: (i, 0)
    with_h = l + 1 < DEPTH
    ln_spec = pl.BlockSpec((None, None, 1, D_MODEL), lambda i: (l, 1, 0, 0))
    half = jax.ShapeDtypeStruct((N_CTX, D_MODEL), F32)
    in_specs = [pl.BlockSpec((tm, D_MODEL), tok), pl.BlockSpec((tm, D_MODEL), tok), _mod_spec(l, 5, tm), ln_spec, ln_spec]
    args = [x1, y, mod, lng, lnb]
    out_shape = [half, half]
    out_specs = list(_pair_specs(tm))
    if with_h:
        in_specs += [_mod_spec(l + 1, 0, tm), _mod_spec(l + 1, 1, tm)]
        args += [mod, mod]
        out_shape.append(jax.ShapeDtypeStruct((N_TOK, D_MODEL), BF16))
        out_specs.append(pl.BlockSpec((tm, D_MODEL), tok))
    return pl.pallas_call(
        functools.partial(_final_kernel, tm=tm, with_h=with_h),
        out_shape=tuple(out_shape), grid=(N_TOK // tm,), in_specs=in_specs, out_specs=tuple(out_specs),
        compiler_params=_cparams("arbitrary"),
        name="final",
    )(*args)


def kernel(x_prompt, x_sample, c, cache_na_k, cache_na_v, state_mlstm_C, state_mlstm_n, state_mlstm_m, state_hgrn,
           c_ctx, w_mod, b_mod, w_in, b_in, mlstm_fbias, hgrn_lb_logits, na_rpb, w_branch, w_out, ln_g, ln_b,
           w_rg, w_re, w_e1, w_e3, w_e2):
    assert N_CTX == N_LAT
    lb_cum = jnp.cumsum(jax.nn.softmax(hgrn_lb_logits.astype(F32), axis=1), axis=1)
    lb_all = lb_cum - lb_cum[:, :1]

    cs = jnp.zeros((N_MODROWS, D_MODEL), F32).at[0].set(c_ctx).at[1:1 + DEC_BATCH].set(c)
    mod = _modulation(cs, w_mod, b_mod).reshape(DEPTH, N_MODROWS, 6, 1, D_MODEL)

    wb = w_branch.astype(BF16)
    wo = w_out.astype(BF16)
    lng = ln_g.reshape(DEPTH, 2, 1, D_MODEL)
    lnb = ln_b.reshape(DEPTH, 2, 1, D_MODEL)
    wr = jnp.zeros((DEPTH, ROUTER_ROWS, D_MODEL), F32)
    wr = wr.at[:, :N_GROUPS].set(jnp.swapaxes(w_rg, 1, 2)).at[:, ROUTER_E0:ROUTER_E0 + N_EXPERTS].set(jnp.swapaxes(w_re, 1, 2))
    w2g = w_e2.reshape(DEPTH, N_GROUPS, EXP_PER_GROUP * D_EXPERT, D_MODEL)
    b_main = jnp.concatenate([b_in[:, :GATE_COL0], b_in[:, GATE_COL0 + N_GATES:]], 1)

    xc = x_prompt.reshape(N_CTX, D_MODEL)
    xl = x_sample.reshape(N_LAT, D_MODEL)
    w_t = jnp.swapaxes(w_in, 1, 2)
    h = _prep(xc, xl, mod)

    kv = (None, None)
    ma_states = None
    hg_state = None
    for l in range(DEPTH):
        p16, p32 = _inproj(h, w_t, b_main[l][None, :], l)
        gcol, gt = _gates(h, w_t, b_in.reshape(DEPTH, 1, N_IN), l)
        gt3 = gt.reshape(N_GATES, N_TOK // MA_CHUNK, MA_CHUNK).transpose(1, 0, 2)

        a, *ma_states = _mlstm(p16, gcol, gt3, mlstm_fbias[l], l, False, prev=ma_states)
        a = _mlstm(p16, gcol, gt3, mlstm_fbias[l], l, True, state_mlstm_C, state_mlstm_n, state_mlstm_m, a_out=a)
        b, *kv = _ctx_attention(p16, l, *kv)
        b = _lat_attention(p16, cache_na_k, cache_na_v, _na_bias_table(na_rpb[l]), l, b)
        cc, hg_state = _hgrn(p16, p32, lb_all[:, l], l, False, prev=hg_state)
        cc = _hgrn(p16, p32, lb_all[:, l], l, True, state_hgrn, c_out=cc)

        x1, h2x = _merge(a, b, cc, p16, xc, xl, mod, wb, wo, lng, lnb, wr, l)
        y2 = _moe(h2x, w_e1, w_e3, w2g, l)
        outs = _final(x1, y2, mod, lng, lnb, l)
        xc, xl = outs[0], outs[1]
        if l + 1 < DEPTH:
            h = outs[2]

    dt = x_prompt.dtype
    new_C, new_n, new_m = ma_states
    new_n = new_n.reshape(BATCH, DEPTH, 2, MA_HEADS, MA_DK)
    new_m = new_m[:, :, :, 0].reshape(BATCH, DEPTH, 2, MA_HEADS)
    return (xc.reshape(BATCH, SEQ, D_MODEL), xl.reshape(DEC_BATCH, DEC_SEQ, D_MODEL), kv[0], kv[1],
            new_C.astype(dt), new_n.astype(dt), new_m.astype(dt), hg_state.astype(dt))
```

```python
import functools

import numpy as np
import jax
import jax.numpy as jnp
from jax import lax
from jax.experimental import pallas as pl
from jax.experimental.pallas import tpu as pltpu

F32 = jnp.float32
BF16 = jnp.bfloat16
HI = lax.Precision.HIGHEST

D_MODEL = 1024
BATCH = 16
SEQ = 256
DEPTH = 2
DEC_BATCH = 4
DEC_SEQ = 1024
PAST_LEN = 256
GRID_W = 64
MA_HEADS = 4
MA_DK = 128
MA_DV = 128
MA_CHUNK = 64
NA_HEADS = 8
NA_DH = 64
NA_KR_MAX = 8
NA_KC = 16
HG_HEADS = 4
HG_DK = 128
HG_DV = 128
HG_CHUNK = 32
BRANCH_W = 512
N_GROUPS = 4
EXP_PER_GROUP = 4
N_EXPERTS = N_GROUPS * EXP_PER_GROUP
D_EXPERT = 512
ROPE_BASE = 10000.0
LN_EPS = 1e-5
RMS_EPS = 1e-6
DEEPNORM_ALPHA = (2 * DEPTH) ** 0.25

N_CTX = BATCH * SEQ
N_LAT = DEC_BATCH * DEC_SEQ
N_TOK = N_CTX + N_LAT
N_MODROWS = 8
GATE_COL0 = 4 * BRANCH_W
N_GATES = 4 * MA_HEADS
N_IN = 9232
P_COLS = N_IN - N_GATES
MOE_TM = 512
MOE_NT = N_TOK // MOE_TM + N_GROUPS
MOE_FE = 2
MOE_XW = D_MODEL + 128
ROUTER_ROWS = 32
ROUTER_E0 = 8
VMEM_LIMIT = 48 * 1024 * 1024

_NT = (((1,), (1,)), ((), ()))
_TN = (((0,), (0,)), ((), ()))


def _cparams(*sem):
    return pltpu.CompilerParams(dimension_semantics=sem, vmem_limit_bytes=VMEM_LIMIT)


def _mod_row(tile, tm):
    return jnp.maximum((tile * tm) // DEC_SEQ - (N_CTX // DEC_SEQ - 1), 0)


def _mod_spec(l, part, tm):
    return pl.BlockSpec((None, None, None, 1, D_MODEL), lambda i: (l, _mod_row(i, tm), part, 0, 0))


def _pair_specs(tm):
    nc = N_CTX // tm
    return (pl.BlockSpec((tm, D_MODEL), lambda i: (jnp.minimum(i, nc - 1), 0)),
            pl.BlockSpec((tm, D_MODEL), lambda i: (jnp.maximum(i - nc, 0), 0)))


def _pair_read(i, tm, c_ref, l_ref):
    return jnp.where(i < N_CTX // tm, c_ref[...], l_ref[...])


def _silu(x):
    return x * jax.nn.sigmoid(x)


def _layer_norm(x, g, b):
    mu = jnp.mean(x, -1, keepdims=True)
    xc = x - mu
    var = jnp.mean(xc * xc, -1, keepdims=True)
    return xc * lax.rsqrt(var + LN_EPS) * g + b


def _log_sigmoid(x):
    return jnp.minimum(x, 0.0) - jnp.log(1.0 + jnp.exp(-jnp.abs(x)))


def _tri(n, upper):
    r = lax.broadcasted_iota(jnp.int32, (n, n), 0)
    c = lax.broadcasted_iota(jnp.int32, (n, n), 1)
    return jnp.where((r <= c) if upper else (r >= c), 1.0, 0.0).astype(F32)


def _mod_kernel(c_ref, w_ref, b_ref, o_ref):
    s = _silu(c_ref[...])
    o_ref[...] = jnp.dot(s.astype(BF16), w_ref[...].astype(BF16), preferred_element_type=F32) + b_ref[...]


def _modulation(cs, w_mod, b_mod):
    tn = 1024
    return pl.pallas_call(
        _mod_kernel,
        out_shape=jax.ShapeDtypeStruct((DEPTH, N_MODROWS, 6 * D_MODEL), F32),
        grid=(DEPTH, 6 * D_MODEL // tn),
        in_specs=[pl.BlockSpec((N_MODROWS, D_MODEL), lambda l, j: (0, 0)),
                  pl.BlockSpec((None, D_MODEL, tn), lambda l, j: (l, 0, j)),
                  pl.BlockSpec((None, 1, tn), lambda l, j: (l, 0, j))],
        out_specs=pl.BlockSpec((None, N_MODROWS, tn), lambda l, j: (l, 0, j)),
        compiler_params=_cparams("arbitrary", "arbitrary"),
        name="modulation",
    )(cs, w_mod, b_mod.reshape(DEPTH, 1, 6 * D_MODEL))


def _prep_kernel(xc_ref, xl_ref, sh_ref, sc_ref, h_ref, *, tm):
    x = _pair_read(pl.program_id(0), tm, xc_ref, xl_ref)
    h_ref[...] = (x * (1.0 + sc_ref[...]) + sh_ref[...]).astype(BF16)


def _prep(xc, xl, mod):
    tm = 1024
    return pl.pallas_call(
        functools.partial(_prep_kernel, tm=tm),
        out_shape=jax.ShapeDtypeStruct((N_TOK, D_MODEL), BF16),
        grid=(N_TOK // tm,),
        in_specs=[*_pair_specs(tm), _mod_spec(0, 0, tm), _mod_spec(0, 1, tm)],
        out_specs=pl.BlockSpec((tm, D_MODEL), lambda i: (i, 0)),
        compiler_params=_cparams("arbitrary"),
        name="prep",
    )(xc, xl, mod, mod)


INPROJ_TN = 512
N_PLAIN_TILES = GATE_COL0 // INPROJ_TN


F32_TILE0 = 7
N_F32_TILES = 2
MERGE_GATE_COL0 = P_COLS - 3 * D_MODEL
P16_COLS = MERGE_GATE_COL0 - N_F32_TILES * INPROJ_TN


def _inproj_kernel(h_ref, wa_ref, wb_ref, b_ref, o_ref, *, src_tile):
    j = src_tile(pl.program_id(1))

    @pl.when(j < N_PLAIN_TILES)
    def _():
        o_ref[...] = (lax.dot_general(h_ref[...], wa_ref[...].astype(BF16), _NT, preferred_element_type=F32)
                      + b_ref[...]).astype(o_ref.dtype)

    @pl.when(j >= N_PLAIN_TILES)
    def _():
        w = jnp.concatenate([wa_ref[N_GATES:, :], wb_ref[...]], 0)
        o_ref[...] = (lax.dot_general(h_ref[...], w.astype(BF16), _NT, preferred_element_type=F32)
                      + b_ref[...]).astype(o_ref.dtype)


def _inproj_call(h, w_t, b_main, l, tm, n_tiles, src_tile, dtype, name):
    tn = INPROJ_TN
    return pl.pallas_call(
        functools.partial(_inproj_kernel, src_tile=src_tile),
        out_shape=jax.ShapeDtypeStruct((N_TOK, n_tiles * tn), dtype),
        grid=(N_TOK // tm, n_tiles),
        in_specs=[pl.BlockSpec((tm, D_MODEL), lambda i, j: (i, 0)),
                  pl.BlockSpec((None, tn, D_MODEL), lambda i, j: (l, src_tile(j), 0)),
                  pl.BlockSpec((None, N_GATES, D_MODEL), lambda i, j: (l, (src_tile(j) + 1) * (tn // N_GATES), 0)),
                  pl.BlockSpec((1, tn), lambda i, j: (0, src_tile(j)))],
        out_specs=pl.BlockSpec((tm, tn), lambda i, j: (i, j)),
        compiler_params=_cparams("arbitrary", "arbitrary"),
        name=name,
    )(h, w_t, w_t, b_main)


def _inproj(h, w_t, b_main, l):
    skip_f32 = lambda j: jnp.where(j < F32_TILE0, j, j + N_F32_TILES)
    p16 = _inproj_call(h, w_t, b_main, l, 4096, P16_COLS // INPROJ_TN, skip_f32, BF16, "inproj")
    p32 = _inproj_call(h, w_t, b_main, l, 2048, N_F32_TILES, lambda j: j + F32_TILE0, F32, "inproj_f32")
    return p16, p32


def _gates_kernel(h_ref, w_ref, b_ref, gc_ref, gt_ref):
    g = lax.dot_general(h_ref[...], w_ref[...].astype(BF16), _NT, preferred_element_type=F32) + b_ref[...]
    gc_ref[...] = g
    gt_ref[...] = g.T[:N_GATES]


def _gates(h, w_t, b_in3, l):
    tm = 1024
    gblk = GATE_COL0 // 128
    return pl.pallas_call(
        _gates_kernel,
        out_shape=(jax.ShapeDtypeStruct((N_TOK, 128), F32), jax.ShapeDtypeStruct((N_GATES, N_TOK), F32)),
        grid=(N_TOK // tm,),
        in_specs=[pl.BlockSpec((tm, D_MODEL), lambda i: (i, 0)),
                  pl.BlockSpec((None, 128, D_MODEL), lambda i: (l, gblk, 0)),
                  pl.BlockSpec((None, 1, 128), lambda i: (l, 0, gblk))],
        out_specs=(pl.BlockSpec((tm, 128), lambda i: (i, 0)), pl.BlockSpec((N_GATES, tm), lambda i: (0, i))),
        compiler_params=_cparams("arbitrary"),
        name="gates",
    )(h, w_t, b_in3)


HEADS_PER_BLK = 128 // NA_DH
NA_NBLK = NA_HEADS // HEADS_PER_BLK
NA_QSCALE = NA_DH ** -0.5
Q_COL, K_COL, V_COL = 16, 20, 24
QKV_COL = 4


def _ctx_attn_kernel(*refs):
    q_ref, k_ref, v_ref = refs[:3]
    o_ref, ko_ref, vo_ref = refs[-3:]
    heads = range(NA_HEADS)
    split = lambda x: jnp.stack([x[:, h * NA_DH:(h + 1) * NA_DH] for h in heads], 0)
    q = split(q_ref[...] * NA_QSCALE)
    k = split(k_ref[...])
    v = split(v_ref[...])
    ko_ref[...] = k.astype(F32)
    vo_ref[...] = v.astype(F32)
    s = lax.dot_general(q, k, (((2,), (2,)), ((0,), (0,))), preferred_element_type=F32)
    e = jnp.exp(s - jnp.max(s, -1, keepdims=True))
    p = e * (1.0 / jnp.sum(e, -1, keepdims=True))
    o = lax.dot_general(p.astype(BF16), v, (((2,), (1,)), ((0,), (0,))), preferred_element_type=F32)
    o_ref[...] = jnp.concatenate([o[h] for h in heads], -1).astype(BF16)


def _ctx_attention(p16, l, prev_k=None, prev_v=None):
    kv_shape = jax.ShapeDtypeStruct((BATCH, DEPTH, NA_HEADS, SEQ, NA_DH), F32)
    kv_spec = pl.BlockSpec((None, None, NA_HEADS, SEQ, NA_DH), lambda b: (b, l, 0, 0, 0))
    col = lambda j: pl.BlockSpec((SEQ, BRANCH_W), lambda b: (b, j))
    in_specs = [col(QKV_COL), col(QKV_COL + 1), col(QKV_COL + 2)]
    args = [p16, p16, p16]
    aliases = {}
    if prev_k is not None:
        in_specs += [pl.BlockSpec(memory_space=pl.ANY)] * 2
        args += [prev_k, prev_v]
        aliases = {3: 1, 4: 2}
    return pl.pallas_call(
        _ctx_attn_kernel,
        out_shape=(jax.ShapeDtypeStruct((N_TOK, BRANCH_W), BF16), kv_shape, kv_shape),
        grid=(BATCH,),
        in_specs=in_specs,
        out_specs=(pl.BlockSpec((SEQ, BRANCH_W), lambda b: (b, 0)), kv_spec, kv_spec),
        input_output_aliases=aliases,
        compiler_params=_cparams("arbitrary"),
        name="ctx_attention",
    )(*args)


NA_ROWS = DEC_SEQ // GRID_W
NA_KR = min(NA_KR_MAX, NA_ROWS)
NA_QROWS = 4
NA_QT = NA_ROWS // NA_QROWS
NA_WROWS = NA_KR + NA_QROWS - 1
NA_WKEYS = NA_WROWS * GRID_W


def _na_window_start(t):
    return min(max(t * NA_QROWS - NA_KR // 2, 0), NA_ROWS - NA_WROWS)


def _na_bias_table(rpb):
    c = np.arange(GRID_W)
    c0 = np.clip(c - NA_KC // 2, 0, GRID_W - NA_KC)
    kc = np.arange(GRID_W)
    valid = (kc[None, :] >= c0[:, None]) & (kc[None, :] < c0[:, None] + NA_KC)
    dc = kc[None, :] - c[:, None] + NA_KC - 1
    onehot = (dc[None] == np.arange(2 * NA_KC - 1)[:, None, None]) & valid[None]
    toep = jnp.einsum('hrd,dcx->hrcx', rpb.astype(F32), jnp.asarray(onehot, F32), precision=HI)
    toep = jnp.where(valid[None, None], toep, -jnp.inf)
    pad = jnp.full((NA_HEADS, NA_WROWS, GRID_W, GRID_W), -jnp.inf, F32)
    toep = jnp.concatenate([pad, toep, pad], 1)
    rows, row_ok = [], np.zeros((NA_ROWS, NA_WROWS), bool)
    for r in range(NA_ROWS):
        w0 = _na_window_start(r // NA_QROWS)
        r0 = min(max(r - NA_KR // 2, 0), NA_ROWS - NA_KR)
        start = w0 - r + NA_KR_MAX - 1 + NA_WROWS
        rows.append(toep[:, start:start + NA_WROWS])
        row_ok[r] = (np.arange(w0, w0 + NA_WROWS) >= r0) & (np.arange(w0, w0 + NA_WROWS) < r0 + NA_KR)
    tab = jnp.stack(rows, 1) + jnp.asarray(np.where(row_ok, 0.0, -np.inf), F32)[None, :, :, None, None]
    tab = tab.transpose(0, 1, 3, 2, 4)
    return tab.reshape(NA_HEADS, NA_QT, NA_QROWS * GRID_W, NA_WKEYS)


def _lat_attn_kernel(q_ref, k_ref, v_ref, ck_ref, cv_ref, bias_ref, prev_ref, o_ref):
    q = (q_ref[...] * NA_QSCALE).astype(BF16)
    k = k_ref[...].astype(BF16)
    v = v_ref[...].astype(BF16)
    nq = NA_QROWS * GRID_W
    for t in range(NA_QT):
        w0 = _na_window_start(t)
        qs = slice(t * nq, (t + 1) * nq)
        ws = slice(w0 * GRID_W, (w0 + NA_WROWS) * GRID_W)
        outs = []
        for hh in range(HEADS_PER_BLK):
            sl = slice(hh * NA_DH, (hh + 1) * NA_DH)
            qh = q[qs, sl]
            s_loc = lax.dot_general(qh, k[ws, sl], _NT, preferred_element_type=F32) + bias_ref[hh, t]
            s_ctx = lax.dot_general(qh, ck_ref[hh].astype(BF16), _NT, preferred_element_type=F32)
            m = jnp.maximum(jnp.max(s_loc, -1, keepdims=True), jnp.max(s_ctx, -1, keepdims=True))
            e_loc = jnp.exp(s_loc - m)
            e_ctx = jnp.exp(s_ctx - m)
            inv = 1.0 / (jnp.sum(e_loc, -1, keepdims=True) + jnp.sum(e_ctx, -1, keepdims=True))
            acc = (jnp.dot(e_loc.astype(BF16), v[ws, sl], preferred_element_type=F32)
                   + jnp.dot(e_ctx.astype(BF16), cv_ref[hh].astype(BF16), preferred_element_type=F32))
            outs.append(acc * inv)
        o_ref[qs, :] = jnp.concatenate(outs, -1).astype(BF16)


def _lat_attention(proj, ck, cv, bias, l, b_out):
    rb0 = N_CTX // DEC_SEQ
    cb = lambda base: (lambda j, b: (rb0 + b, base + j))
    c_spec = pl.BlockSpec((None, None, HEADS_PER_BLK, PAST_LEN, NA_DH), lambda j, b: (b, l, j, 0, 0))
    return pl.pallas_call(
        _lat_attn_kernel,
        out_shape=jax.ShapeDtypeStruct((N_TOK, BRANCH_W), BF16),
        grid=(NA_NBLK, DEC_BATCH),
        in_specs=[pl.BlockSpec((DEC_SEQ, 128), cb(Q_COL)), pl.BlockSpec((DEC_SEQ, 128), cb(K_COL)),
                  pl.BlockSpec((DEC_SEQ, 128), cb(V_COL)), c_spec, c_spec,
                  pl.BlockSpec((HEADS_PER_BLK, NA_QT, NA_QROWS * GRID_W, NA_WKEYS), lambda j, b: (j, 0, 0, 0)),
                  pl.BlockSpec(memory_space=pl.ANY)],
        out_specs=pl.BlockSpec((DEC_SEQ, 128), lambda j, b: (rb0 + b, j)),
        input_output_aliases={6: 0},
        compiler_params=_cparams("arbitrary", "arbitrary"),
        name="lat_attention",
    )(proj, proj, proj, ck, cv, bias, b_out)


MA_KSCALE = MA_DK ** -0.5


def _rope_tables(T):
    t = np.arange(T)
    half = MA_DK // 2
    inv = ROPE_BASE ** (-jnp.arange(0, half, 2, dtype=F32) / half)
    ang_r = jnp.asarray((t // GRID_W).astype(np.float32))[:, None] * inv[None, :]
    ang_c = jnp.asarray((t % GRID_W).astype(np.float32))[:, None] * inv[None, :]
    cos = jnp.concatenate([jnp.cos(ang_r)] * 2 + [jnp.cos(ang_c)] * 2, -1)
    sin = jnp.concatenate([-jnp.sin(ang_r), jnp.sin(ang_r), -jnp.sin(ang_c), jnp.sin(ang_c)], -1)
    return cos, sin


def _mlstm_kernel(*refs, T, latent):
    if latent:
        (p_ref, gc_ref, gt_ref, fbc_ref, fbr_ref, cos_ref, sin_ref, c0_ref, n0_ref, m0_ref, prev_ref,
         a_ref, qs, ks, vT, hfT, hbT, CT, ns, ms, brs, kcs) = refs
    else:
        p_ref, gc_ref, gt_ref, fbc_ref, fbr_ref = refs[:5]
        a_ref, co_ref, no_ref, mo_ref, qs, ks, vT, hfT, hbT, CT, ns, ms, brs, kcs = refs[-14:]
    L = MA_CHUNK
    NC = T // L
    W = BRANCH_W
    PER = 128 // L

    lane = lax.broadcasted_iota(jnp.int32, (T, MA_DK), 1)
    lo_half = (lane % (MA_DK // 2)) < (MA_DK // 4)

    def rope(x):
        if not latent:
            return x
        swapped = jnp.where(lo_half, pltpu.roll(x, MA_DK - MA_DK // 4, 1), pltpu.roll(x, MA_DK // 4, 1))
        return x * cos_ref[...] + swapped * sin_ref[...]

    for h in range(MA_HEADS):
        hs = slice(h * MA_DK, (h + 1) * MA_DK)
        qs[:, hs] = rope(p_ref[:, hs].astype(F32)).astype(BF16)
        ks[:, hs] = rope(p_ref[:, W + h * MA_DK:W + (h + 1) * MA_DK].astype(F32) * MA_KSCALE).astype(BF16)

    def v_block(tb, carry):
        r0 = pl.multiple_of(tb * 128, 128)
        for h in range(MA_HEADS):
            hs = slice(h * MA_DV, (h + 1) * MA_DV)
            blk = p_ref[pl.ds(r0, 128), 2 * W + h * MA_DV:2 * W + (h + 1) * MA_DV].astype(F32).T.astype(BF16)
            for j in range(PER):
                vT[tb * PER + j, h] = blk[:, j * L:(j + 1) * L]
        return carry

    lax.fori_loop(0, T // 128, v_block, 0)

    for d in range(2):
        for h in range(MA_HEADS):
            sidx = d * MA_HEADS + h
            CT[d, h] = c0_ref[d, h].T if latent else jnp.zeros((MA_DV, MA_DK), F32)
            ns[d, h] = n0_ref[sidx:sidx + 1, :] if latent else jnp.zeros((1, MA_DK), F32)
            ms[d, h] = m0_ref[sidx:sidx + 1, :] if latent else jnp.zeros((1, 128), F32)

    low = _tri(L, False)
    upp = _tri(L, True)
    rr = lax.broadcasted_iota(jnp.int32, (L, L), 0)
    cc = lax.broadcasted_iota(jnp.int32, (L, L), 1)
    fbc = fbc_ref[...]
    fbr = fbr_ref[...]

    def gate_sums(c, carry):
        t0 = pl.multiple_of(c * L, L)
        gc = gc_ref[pl.ds(t0, L), :]
        lfc = _log_sigmoid(gc + fbc)
        lfr = _log_sigmoid(gt_ref[c] + fbr)
        ish = pltpu.roll(gc, MA_HEADS, 1)
        brs[0, c] = jnp.dot(lfr, upp, precision=HI, preferred_element_type=F32)
        brs[1, c] = jnp.dot(lfr, low, precision=HI, preferred_element_type=F32)
        kcs[0, pl.ds(t0, L), :] = ish - jnp.dot(low, lfc, precision=HI, preferred_element_type=F32)
        kcs[1, pl.ds(t0, L), :] = ish - jnp.dot(upp, lfc, precision=HI, preferred_element_type=F32)
        return carry

    lax.fori_loop(0, NC, gate_sums, 0, unroll=4)

    def chunk(c, d):
        t0 = pl.multiple_of(c * L, L)
        gt = gt_ref[c]
        brow = brs[d, c]
        kc = kcs[d, pl.ds(t0, L), :]
        mask = (rr <= cc) if d == 0 else (rr >= cc)
        last = L - 1 if d == 0 else 0
        heads = range(MA_HEADS)
        gi0 = 2 * d * MA_HEADS
        gf0 = gi0 + MA_HEADS
        br = jnp.stack([brow[gf0 + h:gf0 + h + 1, :] for h in heads], 0)
        ir = jnp.stack([gt[gi0 + h:gi0 + h + 1, :] for h in heads], 0)
        kcol = jnp.stack([kc[:, gf0 + h:gf0 + h + 1] for h in heads], 0)
        m = ms[d][:, :, 0:1]
        n = ns[d]
        ct = CT[d]
        qc = qs[pl.ds(t0, L), :]
        kc = ks[pl.ds(t0, L), :]
        q = jnp.stack([qc[:, h * MA_DK:(h + 1) * MA_DK] for h in heads], 0)
        k = jnp.stack([kc[:, h * MA_DK:(h + 1) * MA_DK] for h in heads], 0)
        vt = vT[c]
        bnt = (((2,), (2,)), ((0,), (0,)))
        bnn = (((2,), (1,)), ((0,), (0,)))
        dmat = jnp.where(mask, br + kcol, -jnp.inf)
        g = br + m
        m_t = jnp.maximum(g, jnp.max(dmat, 1, keepdims=True))
        w_inter = jnp.exp(g - m_t)
        s = lax.dot_general(k, q, bnt, preferred_element_type=F32) * jnp.exp(dmat - m_t)
        ctn = jnp.concatenate([ct.astype(BF16), jnp.broadcast_to(n, (MA_HEADS, 8, MA_DK)).astype(BF16)], 1)
        cq = lax.dot_general(ctn, q, bnt, preferred_element_type=F32)
        num = w_inter * cq[:, :MA_DV] + lax.dot_general(vt, s.astype(BF16), bnn, preferred_element_type=F32)
        den = w_inter * cq[:, MA_DV:MA_DV + 1] + jnp.sum(s, 1, keepdims=True)
        (hfT if d == 0 else hbT)[c] = num / jnp.maximum(jnp.abs(den), jnp.exp(-m_t))
        m_new = m_t[:, :, last:last + 1]
        b_last = br[:, :, last:last + 1]
        decay = jnp.exp(b_last + m - m_new)
        wk = jnp.exp(b_last - br + ir - m_new)
        wk_hi = wk.astype(BF16)
        wk_lo = (wk - wk_hi.astype(F32)).astype(BF16)
        lhs = jnp.concatenate([(vt.astype(F32) * wk).astype(BF16), wk_hi, wk_lo,
                               jnp.zeros((MA_HEADS, 6, L), BF16)], 1)
        upd = lax.dot_general(lhs, k, bnn, preferred_element_type=F32)
        CT[d] = decay * ct + upd[:, :MA_DV]
        ns[d] = decay * n + upd[:, MA_DV:MA_DV + 1] + upd[:, MA_DV + 1:MA_DV + 2]
        ms[d] = jnp.broadcast_to(m_new, (MA_HEADS, 1, 128))

    def body(i, carry):
        chunk(i, 0)
        chunk(NC - 1 - i, 1)
        return carry

    lax.fori_loop(0, NC, body, 0, unroll=2)

    def out_block(tb, carry):
        r0 = pl.multiple_of(tb * 128, 128)
        hsum = jnp.concatenate([hfT[tb * PER + j] + hbT[tb * PER + j] for j in range(PER)], 2)
        outs = [hsum[h].T for h in range(MA_HEADS)]
        gate = jax.nn.sigmoid(p_ref[pl.ds(r0, 128), 3 * W:4 * W].astype(F32))
        a_ref[pl.ds(r0, 128), :] = (gate * jnp.concatenate(outs, 1)).astype(BF16)
        return carry

    lax.fori_loop(0, T // 128, out_block, 0)
    if not latent:
        for d in range(2):
            for h in range(MA_HEADS):
                sidx = d * MA_HEADS + h
                co_ref[d, h] = CT[d, h].T
                no_ref[sidx:sidx + 1, :] = ns[d, h]
                mo_ref[sidx:sidx + 1, :] = ms[d, h]


def _mlstm(proj, gcol, gt3, fbias_l, l, latent, C0=None, n0=None, m0=None, a_out=None, prev=None):
    T = DEC_SEQ if latent else SEQ
    B = DEC_BATCH if latent else BATCH
    rb0 = N_CTX // DEC_SEQ if latent else 0
    fb = fbias_l.astype(F32)
    fbc = jnp.zeros((1, 128), F32).at[0, MA_HEADS:2 * MA_HEADS].set(fb[0]).at[0, 3 * MA_HEADS:4 * MA_HEADS].set(fb[1])
    fbr = fbc[0, :N_GATES].reshape(N_GATES, 1)
    full2 = lambda b: (0, 0)
    any_spec = pl.BlockSpec(memory_space=pl.ANY)
    in_specs = [pl.BlockSpec((T, 4 * BRANCH_W), lambda b: (rb0 + b, 0)),
                pl.BlockSpec((T, 128), lambda b: (rb0 + b, 0)),
                pl.BlockSpec((T // MA_CHUNK, N_GATES, MA_CHUNK), lambda b: (rb0 + b, 0, 0)),
                pl.BlockSpec((1, 128), full2), pl.BlockSpec((N_GATES, 1), full2)]
    args = [proj, gcol, gt3, fbc, fbr]
    a_shape = jax.ShapeDtypeStruct((N_TOK, BRANCH_W), BF16)
    a_spec = pl.BlockSpec((T, BRANCH_W), lambda b: (rb0 + b, 0))
    c_spec = pl.BlockSpec((None, None, 2, MA_HEADS, MA_DK, MA_DV), lambda b: (b, l, 0, 0, 0, 0))
    nm_spec = pl.BlockSpec((None, None, 2 * MA_HEADS, 128), lambda b: (b, l, 0, 0))
    aliases = {}
    if latent:
        cos, sin = _rope_tables(T)
        nb = 2 * MA_HEADS
        in_specs += [pl.BlockSpec((T, MA_DK), full2), pl.BlockSpec((T, MA_DK), full2), c_spec, nm_spec, nm_spec, any_spec]
        args += [cos, sin, C0, n0.reshape(B, DEPTH, nb, MA_DK),
                 jnp.broadcast_to(m0.reshape(B, DEPTH, nb, 1), (B, DEPTH, nb, 128)), a_out]
        aliases = {len(args) - 1: 0}
        out_shape, out_specs = a_shape, a_spec
    else:
        nm_shape = jax.ShapeDtypeStruct((B, DEPTH, 2 * MA_HEADS, 128), F32)
        out_shape = (a_shape, jax.ShapeDtypeStruct((B, DEPTH, 2, MA_HEADS, MA_DK, MA_DV), F32), nm_shape, nm_shape)
        out_specs = (a_spec, c_spec, nm_spec, nm_spec)
        if prev is not None:
            in_specs += [any_spec] * 3
            args += list(prev)
            aliases = {len(args) - 3: 1, len(args) - 2: 2, len(args) - 1: 3}
    nc = T // MA_CHUNK
    scratch = [pltpu.VMEM((T, BRANCH_W), BF16), pltpu.VMEM((T, BRANCH_W), BF16),
               pltpu.VMEM((nc, MA_HEADS, MA_DV, MA_CHUNK), BF16),
               pltpu.VMEM((nc, MA_HEADS, MA_DV, MA_CHUNK), F32), pltpu.VMEM((nc, MA_HEADS, MA_DV, MA_CHUNK), F32),
               pltpu.VMEM((2, MA_HEADS, MA_DV, MA_DK), F32), pltpu.VMEM((2, MA_HEADS, 1, MA_DK), F32),
               pltpu.VMEM((2, MA_HEADS, 1, 128), F32),
               pltpu.VMEM((2, nc, N_GATES, MA_CHUNK), F32), pltpu.VMEM((2, T, 128), F32)]
    return pl.pallas_call(
        functools.partial(_mlstm_kernel, T=T, latent=latent),
        out_shape=out_shape, grid=(B,), in_specs=in_specs, out_specs=out_specs, scratch_shapes=scratch,
        input_output_aliases=aliases,
        compiler_params=_cparams("arbitrary"),
        name="mlstm_lat" if latent else "mlstm_ctx",
    )(*args)


HG_SUB = 8


def _hgrn_kernel(*refs, T, latent):
    ff_ref, fb_ref, q_ref, i_ref, g_ref, lbf_ref, lbb_ref = refs[:7]
    if latent:
        s0_ref = refs[7]
        c_ref, of, ob, ST, iT, As, Bs = refs[-7:]
    else:
        c_ref, so_ref, of, ob, ST, iT, As, Bs = refs[-8:]
    L = HG_CHUNK
    NC = T // L
    NB = L // HG_SUB
    DK = HG_DK

    for d in range(2):
        for h in range(HG_HEADS):
            ST[d * HG_HEADS + h] = s0_ref[d, h].T if latent else jnp.zeros((HG_DV, DK), F32)

    PER = 128 // L

    def i_block(tb, carry):
        r0 = pl.multiple_of(tb * 128, 128)
        for h in range(HG_HEADS):
            blk = i_ref[pl.ds(r0, 128), h * HG_DV:(h + 1) * HG_DV].astype(F32).T.astype(BF16)
            for j in range(PER):
                iT[tb * PER + j, h] = blk[:, j * L:(j + 1) * L]
        return carry

    lax.fori_loop(0, T // 128, i_block, 0)

    low = _tri(L, False)
    upp = _tri(L, True)
    row8 = lax.broadcasted_iota(jnp.int32, (HG_SUB, L), 0)
    lane_s = lax.broadcasted_iota(jnp.int32, (HG_SUB, L), 1)
    heads = range(HG_HEADS)
    bnt = (((2,), (2,)), ((0,), (0,)))
    bnn = (((2,), (1,)), ((0,), (0,)))
    LOG2E = 1.4426950408889634

    def split(x):
        return jnp.stack([x[:, h * DK:(h + 1) * DK] for h in heads], 0)

    def decay_sums(c, carry):
        t0 = pl.multiple_of(c * L, L)
        for d in range(2):
            fpre = (ff_ref if d == 0 else fb_ref)[pl.ds(t0, L), :]
            lb = (lbf_ref if d == 0 else lbb_ref)[...]
            f = lb + (1.0 - lb) * jax.nn.sigmoid(fpre)
            a = jnp.dot(low if d == 0 else upp, jnp.log(f) * LOG2E, precision=HI, preferred_element_type=F32)
            As[d, pl.ds(t0, L), :] = a
            Bs[d, pl.ds(t0, L), :] = a - jnp.log(1.0 - f) * LOG2E
        return carry

    lax.fori_loop(0, NC, decay_sums, 0, unroll=4)

    def chunk_pair(cf, cb):
        tf = pl.multiple_of(cf * L, L)
        tb = pl.multiple_of(cb * L, L)
        both = lambda fn: jnp.concatenate([fn(0, tf), fn(1, tb)], 0)
        A = both(lambda d, t: split(As[d, pl.ds(t, L), :]))
        B = both(lambda d, t: split(Bs[d, pl.ds(t, L), :]))
        q = both(lambda d, t: split(_silu(q_ref[pl.ds(t, L), :].astype(F32))))
        iv = both(lambda d, t: split(i_ref[pl.ds(t, L), :].astype(BF16)))
        ivT = jnp.concatenate([iT[cf], iT[cb]], 0)
        H = HG_HEADS
        st = ST[...]
        o = lax.dot_general((q * jnp.exp2(A)).astype(BF16), st.astype(BF16), bnt, preferred_element_type=F32)
        a_last = jnp.concatenate([A[:H, L - 1:L], A[H:, 0:1]], 0)
        kd = jnp.exp2(a_last - B).astype(BF16)
        rows = []
        for I in range(NB):
            lo, hi = I * HG_SUB, (I + 1) * HG_SUB
            A_I, q_I = A[:, lo:hi], q[:, lo:hi]
            att_f = jnp.zeros((H, HG_SUB, L), F32)
            att_b = jnp.zeros((H, HG_SUB, L), F32)
            for j in range(HG_SUB):
                s = lo + j
                col = jnp.sum(q_I * jnp.exp2(A_I - B[:, s:s + 1]), -1, keepdims=True)
                att_f = jnp.where((lane_s == s) & (row8 >= j), col[:H], att_f)
                att_b = jnp.where((lane_s == s) & (row8 <= j), col[H:], att_b)
            rf, rb = max(lo - 1, 0), min(hi, L - 1)
            R = jnp.concatenate([A[:H, rf:rf + 1], A[H:, rb:rb + 1]], 0)
            zeros = lambda n: jnp.zeros((H, n, DK), BF16)
            ksc_f = jnp.concatenate([jnp.exp2(R[:H] - B[:H, :lo]).astype(BF16), zeros(L - lo)], 1) if I > 0 else zeros(L)
            ksc_b = jnp.concatenate([zeros(hi), jnp.exp2(R[H:] - B[H:, hi:]).astype(BF16)], 1) if I < NB - 1 else zeros(L)
            ksc = jnp.concatenate([ksc_f, ksc_b], 0)
            off = lax.dot_general((q_I * jnp.exp2(A_I - R)).astype(BF16), ksc, bnt, preferred_element_type=F32)
            rows.append(jnp.concatenate([att_f, att_b], 0) + off)
        att = jnp.concatenate(rows, 1)
        o = o + lax.dot_general(att.astype(BF16), iv, bnn, preferred_element_type=F32)
        for h in heads:
            of[pl.ds(tf, L), h * HG_DV:(h + 1) * HG_DV] = o[h]
            ob[pl.ds(tb, L), h * HG_DV:(h + 1) * HG_DV] = o[H + h]
        ST[...] = st * jnp.exp2(a_last) + lax.dot_general(ivT, kd, bnn, preferred_element_type=F32)

    def body(i, carry):
        chunk_pair(i, NC - 1 - i)
        return carry

    lax.fori_loop(0, NC, body, 0, unroll=4)

    def epilogue(r, carry):
        t0 = pl.multiple_of(r * 128, 128)
        o = of[pl.ds(t0, 128), :] + ob[pl.ds(t0, 128), :]
        gsil = _silu(g_ref[pl.ds(t0, 128), :].astype(F32))
        outs = []
        for h in range(HG_HEADS):
            oh = o[:, h * HG_DV:(h + 1) * HG_DV]
            outs.append(oh * lax.rsqrt(jnp.mean(oh * oh, -1, keepdims=True) + RMS_EPS))
        c_ref[pl.ds(t0, 128), :] = (jnp.concatenate(outs, -1) * gsil).astype(BF16)
        return carry

    lax.fori_loop(0, T // 128, epilogue, 0)
    if not latent:
        for d in range(2):
            for h in range(HG_HEADS):
                so_ref[d, h] = ST[d * HG_HEADS + h].T


def _hgrn(p16, p32, lb_l, l, latent, S0=None, c_out=None, prev=None):
    T = DEC_SEQ if latent else SEQ
    B = DEC_BATCH if latent else BATCH
    rb0 = N_CTX // DEC_SEQ if latent else 0
    W = BRANCH_W
    full2 = lambda b: (0, 0)
    any_spec = pl.BlockSpec(memory_space=pl.ANY)
    col = lambda j: pl.BlockSpec((T, W), lambda b: (rb0 + b, j))
    s_spec = pl.BlockSpec((None, None, 2, HG_HEADS, HG_DK, HG_DV), lambda b: (b, l, 0, 0, 0, 0))
    in_specs = [col(0), col(1), col(7), col(8), col(9), pl.BlockSpec((1, W), full2), pl.BlockSpec((1, W), full2)]
    args = [p32, p32, p16, p16, p16, lb_l[0][None, :], lb_l[1][None, :]]
    c_shape = jax.ShapeDtypeStruct((N_TOK, W), BF16)
    c_spec = pl.BlockSpec((T, W), lambda b: (rb0 + b, 0))
    aliases = {}
    if latent:
        in_specs += [s_spec, any_spec]
        args += [S0, c_out]
        aliases = {8: 0}
        out_shape, out_specs = c_shape, c_spec
    else:
        out_shape = (c_shape, jax.ShapeDtypeStruct((B, DEPTH, 2, HG_HEADS, HG_DK, HG_DV), F32))
        out_specs = (c_spec, s_spec)
        if prev is not None:
            in_specs.append(any_spec)
            args.append(prev)
            aliases = {7: 1}
    scratch = [pltpu.VMEM((T, W), F32), pltpu.VMEM((T, W), F32), pltpu.VMEM((2 * HG_HEADS, HG_DV, HG_DK), F32),
               pltpu.VMEM((T // HG_CHUNK, HG_HEADS, HG_DV, HG_CHUNK), BF16),
               pltpu.VMEM((2, T, W), F32), pltpu.VMEM((2, T, W), F32)]
    return pl.pallas_call(
        functools.partial(_hgrn_kernel, T=T, latent=latent),
        out_shape=out_shape, grid=(B,), in_specs=in_specs, out_specs=out_specs, scratch_shapes=scratch,
        input_output_aliases=aliases,
        compiler_params=_cparams("arbitrary"),
        name="hgrn_lat" if latent else "hgrn_ctx",
    )(*args)


def _merge_kernel(a_ref, b_ref, c_ref, h_ref, wg_ref, bg_ref, xc_ref, xl_ref, g1_ref, sh2_ref, sc2_ref,
                  wb_ref, wo_ref, lng_ref, lnb_ref, wr_ref, x1_ref, h2_ref, *, tm):
    h = h_ref[...]

    def br(v_ref, k):
        cols = slice(k * D_MODEL, (k + 1) * D_MODEL)
        g = lax.dot_general(h, wg_ref[cols, :], _NT, preferred_element_type=F32) + bg_ref[:, cols]
        return jax.nn.sigmoid(g) * jnp.dot(v_ref[...], wb_ref[k], preferred_element_type=F32)

    mix = br(a_ref, 0) + br(b_ref, 1) + br(c_ref, 2)
    y = jnp.dot(mix.astype(BF16), wo_ref[...], preferred_element_type=F32)
    x = _pair_read(pl.program_id(0), tm, xc_ref, xl_ref)
    x1 = _layer_norm(DEEPNORM_ALPHA * x + g1_ref[...] * y, lng_ref[...], lnb_ref[...])
    x1_ref[...] = x1
    h2 = x1 * (1.0 + sc2_ref[...]) + sh2_ref[...]
    h2_ref[:, :D_MODEL] = h2.astype(BF16)
    lt = lax.dot_general(wr_ref[...], h2, _NT, preferred_element_type=F32, precision=HI)
    r = lax.broadcasted_iota(jnp.int32, lt.shape, 0)
    neg = -jnp.inf
    lg = jnp.where(r < N_GROUPS, lt, neg)
    mg = jnp.max(lg, 0, keepdims=True)
    g_sel = jnp.min(jnp.where(lg == mg, r, ROUTER_ROWS), 0, keepdims=True)
    p_sel = 1.0 / jnp.sum(jnp.where(r < N_GROUPS, jnp.exp(lg - mg), 0.0), 0, keepdims=True)
    lo = ROUTER_E0 + EXP_PER_GROUP * g_sel
    le = jnp.where((r >= lo) & (r < lo + EXP_PER_GROUP), lt, neg)
    v1 = jnp.max(le, 0, keepdims=True)
    i1 = jnp.min(jnp.where(le == v1, r, ROUTER_ROWS), 0, keepdims=True)
    le2 = jnp.where(r == i1, neg, le)
    v2 = jnp.max(le2, 0, keepdims=True)
    i2 = jnp.min(jnp.where(le2 == v2, r, ROUTER_ROWS), 0, keepdims=True)
    e2 = jnp.exp(v2 - v1)
    w1 = p_sel / (1.0 + e2)
    w2 = p_sel * e2 / (1.0 + e2)
    w1_hi = w1.astype(BF16).astype(F32)
    w2_hi = w2.astype(BF16).astype(F32)
    j1, j2 = i1 - lo, i2 - lo
    packed = jnp.where(r == j1, w1_hi, jnp.where(r == j2, w2_hi, jnp.where(
        r == j1 + EXP_PER_GROUP, w1 - w1_hi, jnp.where(r == j2 + EXP_PER_GROUP, w2 - w2_hi, jnp.where(
            r == 2 * EXP_PER_GROUP, g_sel.astype(F32), 0.0)))))
    packed = jnp.concatenate([packed, jnp.zeros((128 - ROUTER_ROWS, packed.shape[1]), F32)], 0)
    h2_ref[:, D_MODEL:] = packed.T.astype(BF16)


def _merge(a, b, c, h, wg, bg, xc, xl, mod, wb, wo, lng, lnb, wr, l):
    tm = 512
    tok = lambda i: (i, 0)
    ln_spec = pl.BlockSpec((None, None, 1, D_MODEL), lambda i: (l, 0, 0, 0))
    return pl.pallas_call(
        functools.partial(_merge_kernel, tm=tm),
        out_shape=(jax.ShapeDtypeStruct((N_TOK, D_MODEL), F32), jax.ShapeDtypeStruct((N_TOK, MOE_XW), BF16)),
        grid=(N_TOK // tm,),
        in_specs=[pl.BlockSpec((tm, BRANCH_W), tok), pl.BlockSpec((tm, BRANCH_W), tok), pl.BlockSpec((tm, BRANCH_W), tok),
                  pl.BlockSpec((tm, D_MODEL), tok),
                  pl.BlockSpec((None, 3 * D_MODEL, D_MODEL), lambda i: (l, 0, 0)),
                  pl.BlockSpec((None, 1, 3 * D_MODEL), lambda i: (l, 0, 0)),
                  *_pair_specs(tm), _mod_spec(l, 2, tm), _mod_spec(l, 3, tm), _mod_spec(l, 4, tm),
                  pl.BlockSpec((None, 3, BRANCH_W, D_MODEL), lambda i: (l, 0, 0, 0)),
                  pl.BlockSpec((None, D_MODEL, D_MODEL), lambda i: (l, 0, 0)),
                  ln_spec, ln_spec,
                  pl.BlockSpec((None, ROUTER_ROWS, D_MODEL), lambda i: (l, 0, 0))],
        out_specs=(pl.BlockSpec((tm, D_MODEL), tok), pl.BlockSpec((tm, MOE_XW), tok)),
        compiler_params=_cparams("arbitrary"),
        name="merge",
    )(a, b, c, h, wg, bg, xc, xl, mod, mod, mod, wb, wo, lng, lnb, wr)


def _moe_up_kernel(gid_ref, nused_ref, x_ref, w1_ref, w3_ref, hid_ref, w1b, w3b):
    f = pl.program_id(0)
    t = pl.program_id(1)

    @pl.when(t < nused_ref[0])
    def _():
        first = jnp.logical_or(t == 0, gid_ref[t] != gid_ref[jnp.maximum(t - 1, 0)])

        @pl.when(first)
        def _():
            w1b[...] = w1_ref[...].astype(BF16)
            w3b[...] = w3_ref[...].astype(BF16)

        x = x_ref[:, :D_MODEL]
        rec = x_ref[:, D_MODEL:].astype(F32)
        lane = lax.broadcasted_iota(jnp.int32, rec.shape, 1)
        for j in range(MOE_FE):
            e = f * MOE_FE + j
            a = jnp.dot(x, w1b[j], preferred_element_type=F32)
            b = jnp.dot(x, w3b[j], preferred_element_type=F32)
            gcol = jnp.sum(jnp.where(jnp.logical_or(lane == e, lane == e + EXP_PER_GROUP), rec, 0.0), -1, keepdims=True)
            hid_ref[:, j * D_EXPERT:(j + 1) * D_EXPERT] = (_silu(a) * b * gcol).astype(BF16)


def _moe_tile(t, n):
    return jnp.minimum(t, n[0] - 1)


def _moe_up(gid, nused, xs, w1, w3, l):
    tm = MOE_TM
    npad = MOE_NT * tm
    nf = EXP_PER_GROUP // MOE_FE
    w_spec = pl.BlockSpec((None, MOE_FE, D_MODEL, D_EXPERT),
                          lambda f, t, g, n: (l, nf * g[_moe_tile(t, n)] + f, 0, 0))
    grid_spec = pltpu.PrefetchScalarGridSpec(
        num_scalar_prefetch=2,
        grid=(nf, MOE_NT),
        in_specs=[pl.BlockSpec((tm, MOE_XW), lambda f, t, g, n: (_moe_tile(t, n), 0)), w_spec, w_spec],
        out_specs=pl.BlockSpec((tm, MOE_FE * D_EXPERT), lambda f, t, g, n: (_moe_tile(t, n), f)),
        scratch_shapes=[pltpu.VMEM((MOE_FE, D_MODEL, D_EXPERT), BF16), pltpu.VMEM((MOE_FE, D_MODEL, D_EXPERT), BF16)],
    )
    return pl.pallas_call(
        _moe_up_kernel,
        out_shape=jax.ShapeDtypeStruct((npad, EXP_PER_GROUP * D_EXPERT), BF16),
        grid_spec=grid_spec,
        compiler_params=_cparams("arbitrary", "arbitrary"),
        name="moe_up",
    )(gid, nused, xs, w1, w3)


def _moe_down_kernel(gid_ref, nused_ref, hid_ref, w2_ref, y_ref, w2b):
    t = pl.program_id(0)

    @pl.when(t < nused_ref[0])
    def _():
        first = jnp.logical_or(t == 0, gid_ref[t] != gid_ref[jnp.maximum(t - 1, 0)])

        @pl.when(first)
        def _():
            w2b[...] = w2_ref[...].astype(BF16)

        y_ref[...] = jnp.dot(hid_ref[...], w2b[...], preferred_element_type=F32)


def _moe_down(gid, nused, hid, w2g, l):
    tm = MOE_TM
    npad = MOE_NT * tm
    hw = EXP_PER_GROUP * D_EXPERT
    grid_spec = pltpu.PrefetchScalarGridSpec(
        num_scalar_prefetch=2,
        grid=(MOE_NT,),
        in_specs=[pl.BlockSpec((tm, hw), lambda t, g, n: (_moe_tile(t, n), 0)),
                  pl.BlockSpec((None, None, hw, D_MODEL), lambda t, g, n: (l, g[_moe_tile(t, n)], 0, 0))],
        out_specs=pl.BlockSpec((tm, D_MODEL), lambda t, g, n: (_moe_tile(t, n), 0)),
        scratch_shapes=[pltpu.VMEM((hw, D_MODEL), BF16)],
    )
    return pl.pallas_call(
        _moe_down_kernel,
        out_shape=jax.ShapeDtypeStruct((npad, D_MODEL), F32),
        grid_spec=grid_spec,
        compiler_params=_cparams("arbitrary"),
        name="moe_down",
    )(gid, nused, hid, w2g)


def _moe(h2x, w1, w3, w2g, l):
    tm = MOE_TM
    npad = MOE_NT * tm
    g = h2x[:, D_MODEL + 2 * EXP_PER_GROUP].astype(jnp.int32)
    onehot = (g[:, None] == jnp.arange(N_GROUPS)[None, :]).astype(jnp.int32)
    counts = jnp.sum(onehot, 0)
    rank = jnp.sum((jnp.cumsum(onehot, 0) - onehot) * onehot, 1)
    padded = (counts + tm - 1) // tm * tm
    ends = jnp.cumsum(padded)
    offs = ends - padded
    dest = offs[g] + rank
    src = jnp.zeros((npad,), jnp.int32).at[dest].set(jnp.arange(N_TOK, dtype=jnp.int32), unique_indices=True)
    starts = jnp.arange(MOE_NT, dtype=jnp.int32) * tm
    tile_gid = jnp.minimum(jnp.sum((ends[None, :] <= starts[:, None]).astype(jnp.int32), 1), N_GROUPS - 1)
    nused = (ends[-1:] // tm).astype(jnp.int32)
    take = lambda arr, idx: arr.at[idx].get(mode="promise_in_bounds", unique_indices=False)
    hid = _moe_up(tile_gid, nused, take(h2x, src), w1, w3, l)
    ys = _moe_down(tile_gid, nused, hid, w2g, l)
    return take(ys, dest)


def _final_kernel(*refs, tm, with_h):
    x1_ref, y_ref, g2_ref, lng_ref, lnb_ref = refs[:5]
    x2 = _layer_norm(DEEPNORM_ALPHA * x1_ref[...] + g2_ref[...] * y_ref[...], lng_ref[...], lnb_ref[...])
    i = pl.program_id(0)
    if with_h:
        sh_ref, sc_ref, xc_ref, xl_ref, h_ref = refs[5:]
        h_ref[...] = (x2 * (1.0 + sc_ref[...]) + sh_ref[...]).astype(BF16)
    else:
        xc_ref, xl_ref = refs[5:]

    @pl.when(i < N_CTX // tm)
    def _():
        xc_ref[...] = x2

    @pl.when(i >= N_CTX // tm)
    def _():
        xl_ref[...] = x2


def _final(x1, y, mod, lng, lnb, l):
    tm = 1024
    tok = lambda i: (i, 0)
    with_h = l + 1 < DEPTH
    ln_spec = pl.BlockSpec((None, None, 1, D_MODEL), lambda i: (l, 1, 0, 0))
    half = jax.ShapeDtypeStruct((N_CTX, D_MODEL), F32)
    in_specs = [pl.BlockSpec((tm, D_MODEL), tok), pl.BlockSpec((tm, D_MODEL), tok), _mod_spec(l, 5, tm), ln_spec, ln_spec]
    args = [x1, y, mod, lng, lnb]
    out_shape = [half, half]
    out_specs = list(_pair_specs(tm))
    if with_h:
        in_specs += [_mod_spec(l + 1, 0, tm), _mod_spec(l + 1, 1, tm)]
        args += [mod, mod]
        out_shape.append(jax.ShapeDtypeStruct((N_TOK, D_MODEL), BF16))
        out_specs.append(pl.BlockSpec((tm, D_MODEL), tok))
    return pl.pallas_call(
        functools.partial(_final_kernel, tm=tm, with_h=with_h),
        out_shape=tuple(out_shape), grid=(N_TOK // tm,), in_specs=in_specs, out_specs=tuple(out_specs),
        compiler_params=_cparams("arbitrary"),
        name="final",
    )(*args)


def kernel(x_prompt, x_sample, c, cache_na_k, cache_na_v, state_mlstm_C, state_mlstm_n, state_mlstm_m, state_hgrn,
           c_ctx, w_mod, b_mod, w_in, b_in, mlstm_fbias, hgrn_lb_logits, na_rpb, w_branch, w_out, ln_g, ln_b,
           w_rg, w_re, w_e1, w_e3, w_e2):
    assert N_CTX == N_LAT
    lb_cum = jnp.cumsum(jax.nn.softmax(hgrn_lb_logits.astype(F32), axis=1), axis=1)
    lb_all = lb_cum - lb_cum[:, :1]

    cs = jnp.zeros((N_MODROWS, D_MODEL), F32).at[0].set(c_ctx).at[1:1 + DEC_BATCH].set(c)
    mod = _modulation(cs, w_mod, b_mod).reshape(DEPTH, N_MODROWS, 6, 1, D_MODEL)

    wb = w_branch.astype(BF16)
    wo = w_out.astype(BF16)
    lng = ln_g.reshape(DEPTH, 2, 1, D_MODEL)
    lnb = ln_b.reshape(DEPTH, 2, 1, D_MODEL)
    wr = jnp.zeros((DEPTH, ROUTER_ROWS, D_MODEL), F32)
    wr = wr.at[:, :N_GROUPS].set(jnp.swapaxes(w_rg, 1, 2)).at[:, ROUTER_E0:ROUTER_E0 + N_EXPERTS].set(jnp.swapaxes(w_re, 1, 2))
    w2g = w_e2.reshape(DEPTH, N_GROUPS, EXP_PER_GROUP * D_EXPERT, D_MODEL)
    b_main = jnp.concatenate([b_in[:, :GATE_COL0], b_in[:, GATE_COL0 + N_GATES:]], 1)

    xc = x_prompt.reshape(N_CTX, D_MODEL)
    xl = x_sample.reshape(N_LAT, D_MODEL)
    w_t = jnp.swapaxes(w_in, 1, 2)
    g0 = MERGE_GATE_COL0 + N_GATES
    wg = w_t[:, g0:, :].astype(BF16)
    bg = b_in[:, g0:].reshape(DEPTH, 1, 3 * D_MODEL)
    h = _prep(xc, xl, mod)

    kv = (None, None)
    ma_states = None
    hg_state = None
    for l in range(DEPTH):
        p16, p32 = _inproj(h, w_t, b_main[l][None, :], l)
        gcol, gt = _gates(h, w_t, b_in.reshape(DEPTH, 1, N_IN), l)
        gt3 = gt.reshape(N_GATES, N_TOK // MA_CHUNK, MA_CHUNK).transpose(1, 0, 2)

        a, *ma_states = _mlstm(p16, gcol, gt3, mlstm_fbias[l], l, False, prev=ma_states)
        a = _mlstm(p16, gcol, gt3, mlstm_fbias[l], l, True, state_mlstm_C, state_mlstm_n, state_mlstm_m, a_out=a)
        b, *kv = _ctx_attention(p16, l, *kv)
        b = _lat_attention(p16, cache_na_k, cache_na_v, _na_bias_table(na_rpb[l]), l, b)
        cc, hg_state = _hgrn(p16, p32, lb_all[:, l], l, False, prev=hg_state)
        cc = _hgrn(p16, p32, lb_all[:, l], l, True, state_hgrn, c_out=cc)

        x1, h2x = _merge(a, b, cc, h, wg, bg, xc, xl, mod, wb, wo, lng, lnb, wr, l)
        y2 = _moe(h2x, w_e1, w_e3, w2g, l)
        outs = _final(x1, y2, mod, lng, lnb, l)
        xc, xl = outs[0], outs[1]
        if l + 1 < DEPTH:
            h = outs[2]

    dt = x_prompt.dtype
    new_C, new_n, new_m = ma_states
    new_n = new_n.reshape(BATCH, DEPTH, 2, MA_HEADS, MA_DK)
    new_m = new_m[:, :, :, 0].reshape(BATCH, DEPTH, 2, MA_HEADS)
    return (xc.reshape(BATCH, SEQ, D_MODEL), xl.reshape(DEC_BATCH, DEC_SEQ, D_MODEL), kv[0], kv[1],
            new_C.astype(dt), new_n.astype(dt), new_m.astype(dt), hg_state.astype(dt))
```

```python
import functools

import numpy as np
import jax
import jax.numpy as jnp
from jax import lax
from jax.experimental import pallas as pl
from jax.experimental.pallas import tpu as pltpu

F32 = jnp.float32
BF16 = jnp.bfloat16
HI = lax.Precision.HIGHEST

D_MODEL = 1024
BATCH = 16
SEQ = 256
DEPTH = 2
DEC_BATCH = 4
DEC_SEQ = 1024
PAST_LEN = 256
GRID_W = 64
MA_HEADS = 4
MA_DK = 128
MA_DV = 128
MA_CHUNK = 64
NA_HEADS = 8
NA_DH = 64
NA_KR_MAX = 8
NA_KC = 16
HG_HEADS = 4
HG_DK = 128
HG_DV = 128
HG_CHUNK = 32
BRANCH_W = 512
N_GROUPS = 4
EXP_PER_GROUP = 4
N_EXPERTS = N_GROUPS * EXP_PER_GROUP
D_EXPERT = 512
ROPE_BASE = 10000.0
LN_EPS = 1e-5
RMS_EPS = 1e-6
DEEPNORM_ALPHA = (2 * DEPTH) ** 0.25

N_CTX = BATCH * SEQ
N_LAT = DEC_BATCH * DEC_SEQ
N_TOK = N_CTX + N_LAT
N_MODROWS = 8
GATE_COL0 = 4 * BRANCH_W
N_GATES = 4 * MA_HEADS
N_IN = 9232
P_COLS = N_IN - N_GATES
MOE_TM = 512
MOE_NT = N_TOK // MOE_TM + N_GROUPS
MOE_FE = 4
MOE_XW = D_MODEL + 128
ROUTER_ROWS = 32
ROUTER_E0 = 8
VMEM_LIMIT = 56 * 1024 * 1024

_NT = (((1,), (1,)), ((), ()))
_TN = (((0,), (0,)), ((), ()))


def _cparams(*sem):
    return pltpu.CompilerParams(dimension_semantics=sem, vmem_limit_bytes=VMEM_LIMIT)


def _mod_row(tile, tm):
    return jnp.maximum((tile * tm) // DEC_SEQ - (N_CTX // DEC_SEQ - 1), 0)


def _mod_spec(l, part, tm):
    return pl.BlockSpec((None, None, None, 1, D_MODEL), lambda i: (l, _mod_row(i, tm), part, 0, 0))


def _pair_specs(tm):
    nc = N_CTX // tm
    return (pl.BlockSpec((tm, D_MODEL), lambda i: (jnp.minimum(i, nc - 1), 0)),
            pl.BlockSpec((tm, D_MODEL), lambda i: (jnp.maximum(i - nc, 0), 0)))


def _pair_read(i, tm, c_ref, l_ref):
    return jnp.where(i < N_CTX // tm, c_ref[...], l_ref[...])


def _silu(x):
    return x * jax.nn.sigmoid(x)


def _layer_norm(x, g, b):
    mu = jnp.mean(x, -1, keepdims=True)
    xc = x - mu
    var = jnp.mean(xc * xc, -1, keepdims=True)
    return xc * lax.rsqrt(var + LN_EPS) * g + b


def _log_sigmoid(x):
    return jnp.minimum(x, 0.0) - jnp.log(1.0 + jnp.exp(-jnp.abs(x)))


def _tri(n, upper):
    r = lax.broadcasted_iota(jnp.int32, (n, n), 0)
    c = lax.broadcasted_iota(jnp.int32, (n, n), 1)
    return jnp.where((r <= c) if upper else (r >= c), 1.0, 0.0).astype(F32)


def _mod_kernel(c_ref, w_ref, b_ref, o_ref):
    s = _silu(c_ref[...])
    o_ref[...] = jnp.dot(s.astype(BF16), w_ref[...].astype(BF16), preferred_element_type=F32) + b_ref[...]


def _modulation(cs, w_mod, b_mod):
    tn = 1024
    return pl.pallas_call(
        _mod_kernel,
        out_shape=jax.ShapeDtypeStruct((DEPTH, N_MODROWS, 6 * D_MODEL), F32),
        grid=(DEPTH, 6 * D_MODEL // tn),
        in_specs=[pl.BlockSpec((N_MODROWS, D_MODEL), lambda l, j: (0, 0)),
                  pl.BlockSpec((None, D_MODEL, tn), lambda l, j: (l, 0, j)),
                  pl.BlockSpec((None, 1, tn), lambda l, j: (l, 0, j))],
        out_specs=pl.BlockSpec((None, N_MODROWS, tn), lambda l, j: (l, 0, j)),
        compiler_params=_cparams("arbitrary", "arbitrary"),
        name="modulation",
    )(cs, w_mod, b_mod.reshape(DEPTH, 1, 6 * D_MODEL))


def _prep_kernel(xc_ref, xl_ref, sh_ref, sc_ref, h_ref, *, tm):
    x = _pair_read(pl.program_id(0), tm, xc_ref, xl_ref)
    h_ref[...] = (x * (1.0 + sc_ref[...]) + sh_ref[...]).astype(BF16)


def _prep(xc, xl, mod):
    tm = 1024
    return pl.pallas_call(
        functools.partial(_prep_kernel, tm=tm),
        out_shape=jax.ShapeDtypeStruct((N_TOK, D_MODEL), BF16),
        grid=(N_TOK // tm,),
        in_specs=[*_pair_specs(tm), _mod_spec(0, 0, tm), _mod_spec(0, 1, tm)],
        out_specs=pl.BlockSpec((tm, D_MODEL), lambda i: (i, 0)),
        compiler_params=_cparams("arbitrary"),
        name="prep",
    )(xc, xl, mod, mod)


INPROJ_TN = 512
N_PLAIN_TILES = GATE_COL0 // INPROJ_TN


F32_TILE0 = 7
N_F32_TILES = 2
P16_COLS = P_COLS - N_F32_TILES * INPROJ_TN


def _inproj_kernel(h_ref, wa_ref, wb_ref, b_ref, o_ref, *, src_tile):
    j = src_tile(pl.program_id(1))

    @pl.when(j < N_PLAIN_TILES)
    def _():
        o_ref[...] = (lax.dot_general(h_ref[...], wa_ref[...].astype(BF16), _NT, preferred_element_type=F32)
                      + b_ref[...]).astype(o_ref.dtype)

    @pl.when(j >= N_PLAIN_TILES)
    def _():
        w = jnp.concatenate([wa_ref[N_GATES:, :], wb_ref[...]], 0)
        o_ref[...] = (lax.dot_general(h_ref[...], w.astype(BF16), _NT, preferred_element_type=F32)
                      + b_ref[...]).astype(o_ref.dtype)


def _inproj_call(h, w_t, b_main, l, tm, n_tiles, src_tile, dtype, name):
    tn = INPROJ_TN
    return pl.pallas_call(
        functools.partial(_inproj_kernel, src_tile=src_tile),
        out_shape=jax.ShapeDtypeStruct((N_TOK, n_tiles * tn), dtype),
        grid=(N_TOK // tm, n_tiles),
        in_specs=[pl.BlockSpec((tm, D_MODEL), lambda i, j: (i, 0)),
                  pl.BlockSpec((None, tn, D_MODEL), lambda i, j: (l, src_tile(j), 0)),
                  pl.BlockSpec((None, N_GATES, D_MODEL), lambda i, j: (l, (src_tile(j) + 1) * (tn // N_GATES), 0)),
                  pl.BlockSpec((1, tn), lambda i, j: (0, src_tile(j)))],
        out_specs=pl.BlockSpec((tm, tn), lambda i, j: (i, j)),
        compiler_params=_cparams("arbitrary", "arbitrary"),
        name=name,
    )(h, w_t, w_t, b_main)


def _inproj(h, w_t, b_main, l):
    skip_f32 = lambda j: jnp.where(j < F32_TILE0, j, j + N_F32_TILES)
    p16 = _inproj_call(h, w_t, b_main, l, 4096, P16_COLS // INPROJ_TN, skip_f32, BF16, "inproj")
    p32 = _inproj_call(h, w_t, b_main, l, 2048, N_F32_TILES, lambda j: j + F32_TILE0, F32, "inproj_f32")
    return p16, p32


def _gates_kernel(h_ref, w_ref, b_ref, gc_ref, gt_ref):
    g = lax.dot_general(h_ref[...], w_ref[...].astype(BF16), _NT, preferred_element_type=F32) + b_ref[...]
    gc_ref[...] = g
    gt_ref[...] = g.T[:N_GATES]


def _gates(h, w_t, b_in3, l):
    tm = 1024
    gblk = GATE_COL0 // 128
    return pl.pallas_call(
        _gates_kernel,
        out_shape=(jax.ShapeDtypeStruct((N_TOK, 128), F32), jax.ShapeDtypeStruct((N_GATES, N_TOK), F32)),
        grid=(N_TOK // tm,),
        in_specs=[pl.BlockSpec((tm, D_MODEL), lambda i: (i, 0)),
                  pl.BlockSpec((None, 128, D_MODEL), lambda i: (l, gblk, 0)),
                  pl.BlockSpec((None, 1, 128), lambda i: (l, 0, gblk))],
        out_specs=(pl.BlockSpec((tm, 128), lambda i: (i, 0)), pl.BlockSpec((N_GATES, tm), lambda i: (0, i))),
        compiler_params=_cparams("arbitrary"),
        name="gates",
    )(h, w_t, b_in3)


HEADS_PER_BLK = 128 // NA_DH
NA_NBLK = NA_HEADS // HEADS_PER_BLK
NA_QSCALE = NA_DH ** -0.5
Q_COL, K_COL, V_COL = 16, 20, 24
QKV_COL = 4


def _ctx_attn_kernel(*refs):
    q_ref, k_ref, v_ref = refs[:3]
    o_ref, ko_ref, vo_ref = refs[-3:]
    heads = range(NA_HEADS)
    split = lambda x: jnp.stack([x[:, h * NA_DH:(h + 1) * NA_DH] for h in heads], 0)
    q = split(q_ref[...] * NA_QSCALE)
    k = split(k_ref[...])
    v = split(v_ref[...])
    ko_ref[...] = k.astype(F32)
    vo_ref[...] = v.astype(F32)
    s = lax.dot_general(q, k, (((2,), (2,)), ((0,), (0,))), preferred_element_type=F32)
    e = jnp.exp(s - jnp.max(s, -1, keepdims=True))
    p = e * (1.0 / jnp.sum(e, -1, keepdims=True))
    o = lax.dot_general(p.astype(BF16), v, (((2,), (1,)), ((0,), (0,))), preferred_element_type=F32)
    o_ref[...] = jnp.concatenate([o[h] for h in heads], -1).astype(BF16)


def _ctx_attention(p16, l, prev_k=None, prev_v=None):
    kv_shape = jax.ShapeDtypeStruct((BATCH, DEPTH, NA_HEADS, SEQ, NA_DH), F32)
    kv_spec = pl.BlockSpec((None, None, NA_HEADS, SEQ, NA_DH), lambda b: (b, l, 0, 0, 0))
    col = lambda j: pl.BlockSpec((SEQ, BRANCH_W), lambda b: (b, j))
    in_specs = [col(QKV_COL), col(QKV_COL + 1), col(QKV_COL + 2)]
    args = [p16, p16, p16]
    aliases = {}
    if prev_k is not None:
        in_specs += [pl.BlockSpec(memory_space=pl.ANY)] * 2
        args += [prev_k, prev_v]
        aliases = {3: 1, 4: 2}
    return pl.pallas_call(
        _ctx_attn_kernel,
        out_shape=(jax.ShapeDtypeStruct((N_TOK, BRANCH_W), BF16), kv_shape, kv_shape),
        grid=(BATCH,),
        in_specs=in_specs,
        out_specs=(pl.BlockSpec((SEQ, BRANCH_W), lambda b: (b, 0)), kv_spec, kv_spec),
        input_output_aliases=aliases,
        compiler_params=_cparams("arbitrary"),
        name="ctx_attention",
    )(*args)


NA_ROWS = DEC_SEQ // GRID_W
NA_KR = min(NA_KR_MAX, NA_ROWS)
NA_QROWS = 4
NA_QT = NA_ROWS // NA_QROWS
NA_WROWS = NA_KR + NA_QROWS - 1
NA_WKEYS = NA_WROWS * GRID_W


def _na_window_start(t):
    return min(max(t * NA_QROWS - NA_KR // 2, 0), NA_ROWS - NA_WROWS)


def _na_bias_table(rpb):
    c = np.arange(GRID_W)
    c0 = np.clip(c - NA_KC // 2, 0, GRID_W - NA_KC)
    kc = np.arange(GRID_W)
    valid = (kc[None, :] >= c0[:, None]) & (kc[None, :] < c0[:, None] + NA_KC)
    dc = kc[None, :] - c[:, None] + NA_KC - 1
    onehot = (dc[None] == np.arange(2 * NA_KC - 1)[:, None, None]) & valid[None]
    toep = jnp.einsum('hrd,dcx->hrcx', rpb.astype(F32), jnp.asarray(onehot, F32), precision=HI)
    toep = jnp.where(valid[None, None], toep, -jnp.inf)
    ninf = jnp.full((NA_HEADS, GRID_W, GRID_W), -jnp.inf, F32)
    tiles = []
    for t in range(NA_QT):
        w0 = _na_window_start(t)
        qrows = []
        for r in range(t * NA_QROWS, (t + 1) * NA_QROWS):
            r0 = min(max(r - NA_KR // 2, 0), NA_ROWS - NA_KR)
            blocks = []
            for kr in range(w0, w0 + NA_WROWS):
                inside = r0 <= kr < r0 + NA_KR
                blocks.append(toep[:, kr - r + NA_KR_MAX - 1] if inside else ninf)
            qrows.append(jnp.concatenate(blocks, -1))
        tiles.append(jnp.concatenate(qrows, 1))
    return jnp.stack(tiles, 1)


def _lat_attn_kernel(q_ref, k_ref, v_ref, ck_ref, cv_ref, bias_ref, prev_ref, o_ref):
    q = (q_ref[...] * NA_QSCALE).astype(BF16)
    k = k_ref[...].astype(BF16)
    v = v_ref[...].astype(BF16)
    nq = NA_QROWS * GRID_W
    for t in range(NA_QT):
        w0 = _na_window_start(t)
        qs = slice(t * nq, (t + 1) * nq)
        ws = slice(w0 * GRID_W, (w0 + NA_WROWS) * GRID_W)
        outs = []
        for hh in range(HEADS_PER_BLK):
            sl = slice(hh * NA_DH, (hh + 1) * NA_DH)
            qh = q[qs, sl]
            s_loc = lax.dot_general(qh, k[ws, sl], _NT, preferred_element_type=F32) + bias_ref[hh, t]
            s_ctx = lax.dot_general(qh, ck_ref[hh].astype(BF16), _NT, preferred_element_type=F32)
            m = jnp.maximum(jnp.max(s_loc, -1, keepdims=True), jnp.max(s_ctx, -1, keepdims=True))
            e_loc = jnp.exp(s_loc - m)
            e_ctx = jnp.exp(s_ctx - m)
            inv = 1.0 / (jnp.sum(e_loc, -1, keepdims=True) + jnp.sum(e_ctx, -1, keepdims=True))
            acc = (jnp.dot(e_loc.astype(BF16), v[ws, sl], preferred_element_type=F32)
                   + jnp.dot(e_ctx.astype(BF16), cv_ref[hh].astype(BF16), preferred_element_type=F32))
            outs.append(acc * inv)
        o_ref[qs, :] = jnp.concatenate(outs, -1).astype(BF16)


def _lat_attention(proj, ck, cv, bias, l, b_out):
    rb0 = N_CTX // DEC_SEQ
    cb = lambda base: (lambda j, b: (rb0 + b, base + j))
    c_spec = pl.BlockSpec((None, None, HEADS_PER_BLK, PAST_LEN, NA_DH), lambda j, b: (b, l, j, 0, 0))
    return pl.pallas_call(
        _lat_attn_kernel,
        out_shape=jax.ShapeDtypeStruct((N_TOK, BRANCH_W), BF16),
        grid=(NA_NBLK, DEC_BATCH),
        in_specs=[pl.BlockSpec((DEC_SEQ, 128), cb(Q_COL)), pl.BlockSpec((DEC_SEQ, 128), cb(K_COL)),
                  pl.BlockSpec((DEC_SEQ, 128), cb(V_COL)), c_spec, c_spec,
                  pl.BlockSpec((HEADS_PER_BLK, NA_QT, NA_QROWS * GRID_W, NA_WKEYS), lambda j, b: (j, 0, 0, 0)),
                  pl.BlockSpec(memory_space=pl.ANY)],
        out_specs=pl.BlockSpec((DEC_SEQ, 128), lambda j, b: (rb0 + b, j)),
        input_output_aliases={6: 0},
        compiler_params=_cparams("arbitrary", "arbitrary"),
        name="lat_attention",
    )(proj, proj, proj, ck, cv, bias, b_out)


MA_KSCALE = MA_DK ** -0.5


def _rope_tables(T):
    t = np.arange(T)
    half = MA_DK // 2
    inv = ROPE_BASE ** (-jnp.arange(0, half, 2, dtype=F32) / half)
    ang_r = jnp.asarray((t // GRID_W).astype(np.float32))[:, None] * inv[None, :]
    ang_c = jnp.asarray((t % GRID_W).astype(np.float32))[:, None] * inv[None, :]
    cos = jnp.concatenate([jnp.cos(ang_r)] * 2 + [jnp.cos(ang_c)] * 2, -1)
    sin = jnp.concatenate([-jnp.sin(ang_r), jnp.sin(ang_r), -jnp.sin(ang_c), jnp.sin(ang_c)], -1)
    return cos, sin


def _mlstm_kernel(*refs, T, latent):
    if latent:
        (p_ref, gc_ref, gt_ref, fbc_ref, fbr_ref, cos_ref, sin_ref, c0_ref, n0_ref, m0_ref, prev_ref,
         a_ref, qs, ks, vT, hfT, hbT, CT, ns, ms, brs, kcs) = refs
    else:
        p_ref, gc_ref, gt_ref, fbc_ref, fbr_ref = refs[:5]
        a_ref, co_ref, no_ref, mo_ref, qs, ks, vT, hfT, hbT, CT, ns, ms, brs, kcs = refs[-14:]
    L = MA_CHUNK
    NC = T // L
    W = BRANCH_W
    PER = 128 // L

    lane = lax.broadcasted_iota(jnp.int32, (T, MA_DK), 1)
    lo_half = (lane % (MA_DK // 2)) < (MA_DK // 4)

    def rope(x):
        if not latent:
            return x
        swapped = jnp.where(lo_half, pltpu.roll(x, MA_DK - MA_DK // 4, 1), pltpu.roll(x, MA_DK // 4, 1))
        return x * cos_ref[...] + swapped * sin_ref[...]

    for h in range(MA_HEADS):
        hs = slice(h * MA_DK, (h + 1) * MA_DK)
        qs[:, hs] = rope(p_ref[:, hs].astype(F32)).astype(BF16)
        ks[:, hs] = rope(p_ref[:, W + h * MA_DK:W + (h + 1) * MA_DK].astype(F32) * MA_KSCALE).astype(BF16)

    def v_block(tb, carry):
        r0 = pl.multiple_of(tb * 128, 128)
        for h in range(MA_HEADS):
            hs = slice(h * MA_DV, (h + 1) * MA_DV)
            blk = p_ref[pl.ds(r0, 128), 2 * W + h * MA_DV:2 * W + (h + 1) * MA_DV].astype(F32).T.astype(BF16)
            for j in range(PER):
                vT[tb * PER + j, h] = blk[:, j * L:(j + 1) * L]
        return carry

    lax.fori_loop(0, T // 128, v_block, 0)

    for d in range(2):
        for h in range(MA_HEADS):
            sidx = d * MA_HEADS + h
            CT[sidx] = c0_ref[d, h].T if latent else jnp.zeros((MA_DV, MA_DK), F32)
            ns[sidx] = n0_ref[sidx:sidx + 1, :] if latent else jnp.zeros((1, MA_DK), F32)
            ms[sidx] = m0_ref[sidx:sidx + 1, :] if latent else jnp.zeros((1, 128), F32)

    low = _tri(L, False)
    upp = _tri(L, True)
    rr = lax.broadcasted_iota(jnp.int32, (L, L), 0)
    cc = lax.broadcasted_iota(jnp.int32, (L, L), 1)
    fbc = fbc_ref[...]
    fbr = fbr_ref[...]

    def gate_sums(c, carry):
        t0 = pl.multiple_of(c * L, L)
        gc = gc_ref[pl.ds(t0, L), :]
        lfc = _log_sigmoid(gc + fbc)
        lfr = _log_sigmoid(gt_ref[c] + fbr)
        ish = pltpu.roll(gc, MA_HEADS, 1)
        brs[0, c] = jnp.dot(lfr, upp, precision=HI, preferred_element_type=F32)
        brs[1, c] = jnp.dot(lfr, low, precision=HI, preferred_element_type=F32)
        kcs[0, pl.ds(t0, L), :] = ish - jnp.dot(low, lfc, precision=HI, preferred_element_type=F32)
        kcs[1, pl.ds(t0, L), :] = ish - jnp.dot(upp, lfc, precision=HI, preferred_element_type=F32)
        return carry

    lax.fori_loop(0, NC, gate_sums, 0, unroll=4)

    def chunk_pair(cf, cb):
        H = MA_HEADS
        G = 2 * H
        heads = range(H)
        cs, ts = (cf, cb), (pl.multiple_of(cf * L, L), pl.multiple_of(cb * L, L))
        rows_of = lambda x, r0: [x[r0 + h:r0 + h + 1, :] for h in heads]
        cols_of = lambda x, c0: [x[:, c0 + h:c0 + h + 1] for h in heads]
        br = jnp.stack(sum([rows_of(brs[d, cs[d]], 2 * d * H + H) for d in range(2)], []), 0)
        ir = jnp.stack(sum([rows_of(gt_ref[cs[d]], 2 * d * H) for d in range(2)], []), 0)
        kcol = jnp.stack(sum([cols_of(kcs[d, pl.ds(ts[d], L), :], 2 * d * H + H) for d in range(2)], []), 0)
        split = lambda x: [x[:, h * MA_DK:(h + 1) * MA_DK] for h in heads]
        q = jnp.stack(split(qs[pl.ds(ts[0], L), :]) + split(qs[pl.ds(ts[1], L), :]), 0)
        k = jnp.stack(split(ks[pl.ds(ts[0], L), :]) + split(ks[pl.ds(ts[1], L), :]), 0)
        vt = jnp.concatenate([vT[cf], vT[cb]], 0)
        m = ms[...][:, :, 0:1]
        n = ns[...]
        ct = CT[...]
        bnt = (((2,), (2,)), ((0,), (0,)))
        bnn = (((2,), (1,)), ((0,), (0,)))
        pre = br + kcol
        dmat = jnp.concatenate([jnp.where(rr <= cc, pre[:H], -jnp.inf), jnp.where(rr >= cc, pre[H:], -jnp.inf)], 0)
        g = br + m
        m_t = jnp.maximum(g, jnp.max(dmat, 1, keepdims=True))
        w_inter = jnp.exp(g - m_t)
        s = lax.dot_general(k, q, bnt, preferred_element_type=F32) * jnp.exp(dmat - m_t)
        ctn = jnp.concatenate([ct.astype(BF16), jnp.broadcast_to(n, (G, 8, MA_DK)).astype(BF16)], 1)
        cq = lax.dot_general(ctn, q, bnt, preferred_element_type=F32)
        num = w_inter * cq[:, :MA_DV] + lax.dot_general(vt, s.astype(BF16), bnn, preferred_element_type=F32)
        den = w_inter * cq[:, MA_DV:MA_DV + 1] + jnp.sum(s, 1, keepdims=True)
        hout = num / jnp.maximum(jnp.abs(den), jnp.exp(-m_t))
        hfT[cf] = hout[:H]
        hbT[cb] = hout[H:]
        last = lambda x: jnp.concatenate([x[:H, :, L - 1:L], x[H:, :, 0:1]], 0)
        m_new = last(m_t)
        b_last = last(br)
        decay = jnp.exp(b_last + m - m_new)
        wk = jnp.exp(b_last - br + ir - m_new)
        wk_hi = wk.astype(BF16)
        wk_lo = (wk - wk_hi.astype(F32)).astype(BF16)
        lhs = jnp.concatenate([(vt.astype(F32) * wk).astype(BF16), wk_hi, wk_lo, jnp.zeros((G, 6, L), BF16)], 1)
        upd = lax.dot_general(lhs, k, bnn, preferred_element_type=F32)
        CT[...] = decay * ct + upd[:, :MA_DV]
        ns[...] = decay * n + upd[:, MA_DV:MA_DV + 1] + upd[:, MA_DV + 1:MA_DV + 2]
        ms[...] = jnp.broadcast_to(m_new, (G, 1, 128))

    def body(i, carry):
        chunk_pair(i, NC - 1 - i)
        return carry

    lax.fori_loop(0, NC, body, 0, unroll=2)

    def out_block(tb, carry):
        r0 = pl.multiple_of(tb * 128, 128)
        hsum = jnp.concatenate([hfT[tb * PER + j] + hbT[tb * PER + j] for j in range(PER)], 2)
        outs = [hsum[h].T for h in range(MA_HEADS)]
        gate = jax.nn.sigmoid(p_ref[pl.ds(r0, 128), 3 * W:4 * W].astype(F32))
        a_ref[pl.ds(r0, 128), :] = (gate * jnp.concatenate(outs, 1)).astype(BF16)
        return carry

    lax.fori_loop(0, T // 128, out_block, 0)
    if not latent:
        for d in range(2):
            for h in range(MA_HEADS):
                sidx = d * MA_HEADS + h
                co_ref[d, h] = CT[sidx].T
                no_ref[sidx:sidx + 1, :] = ns[sidx]
                mo_ref[sidx:sidx + 1, :] = ms[sidx]


def _mlstm(proj, gcol, gt3, fbias_l, l, latent, C0=None, n0=None, m0=None, a_out=None, prev=None):
    T = DEC_SEQ if latent else SEQ
    B = DEC_BATCH if latent else BATCH
    rb0 = N_CTX // DEC_SEQ if latent else 0
    fb = fbias_l.astype(F32)
    fbc = jnp.zeros((1, 128), F32).at[0, MA_HEADS:2 * MA_HEADS].set(fb[0]).at[0, 3 * MA_HEADS:4 * MA_HEADS].set(fb[1])
    fbr = fbc[0, :N_GATES].reshape(N_GATES, 1)
    full2 = lambda b: (0, 0)
    any_spec = pl.BlockSpec(memory_space=pl.ANY)
    in_specs = [pl.BlockSpec((T, 4 * BRANCH_W), lambda b: (rb0 + b, 0)),
                pl.BlockSpec((T, 128), lambda b: (rb0 + b, 0)),
                pl.BlockSpec((T // MA_CHUNK, N_GATES, MA_CHUNK), lambda b: (rb0 + b, 0, 0)),
                pl.BlockSpec((1, 128), full2), pl.BlockSpec((N_GATES, 1), full2)]
    args = [proj, gcol, gt3, fbc, fbr]
    a_shape = jax.ShapeDtypeStruct((N_TOK, BRANCH_W), BF16)
    a_spec = pl.BlockSpec((T, BRANCH_W), lambda b: (rb0 + b, 0))
    c_spec = pl.BlockSpec((None, None, 2, MA_HEADS, MA_DK, MA_DV), lambda b: (b, l, 0, 0, 0, 0))
    nm_spec = pl.BlockSpec((None, None, 2 * MA_HEADS, 128), lambda b: (b, l, 0, 0))
    aliases = {}
    if latent:
        cos, sin = _rope_tables(T)
        nb = 2 * MA_HEADS
        in_specs += [pl.BlockSpec((T, MA_DK), full2), pl.BlockSpec((T, MA_DK), full2), c_spec, nm_spec, nm_spec, any_spec]
        args += [cos, sin, C0, n0.reshape(B, DEPTH, nb, MA_DK),
                 jnp.broadcast_to(m0.reshape(B, DEPTH, nb, 1), (B, DEPTH, nb, 128)), a_out]
        aliases = {len(args) - 1: 0}
        out_shape, out_specs = a_shape, a_spec
    else:
        nm_shape = jax.ShapeDtypeStruct((B, DEPTH, 2 * MA_HEADS, 128), F32)
        out_shape = (a_shape, jax.ShapeDtypeStruct((B, DEPTH, 2, MA_HEADS, MA_DK, MA_DV), F32), nm_shape, nm_shape)
        out_specs = (a_spec, c_spec, nm_spec, nm_spec)
        if prev is not None:
            in_specs += [any_spec] * 3
            args += list(prev)
            aliases = {len(args) - 3: 1, len(args) - 2: 2, len(args) - 1: 3}
    nc = T // MA_CHUNK
    scratch = [pltpu.VMEM((T, BRANCH_W), BF16), pltpu.VMEM((T, BRANCH_W), BF16),
               pltpu.VMEM((nc, MA_HEADS, MA_DV, MA_CHUNK), BF16),
               pltpu.VMEM((nc, MA_HEADS, MA_DV, MA_CHUNK), F32), pltpu.VMEM((nc, MA_HEADS, MA_DV, MA_CHUNK), F32),
               pltpu.VMEM((2 * MA_HEADS, MA_DV, MA_DK), F32), pltpu.VMEM((2 * MA_HEADS, 1, MA_DK), F32),
               pltpu.VMEM((2 * MA_HEADS, 1, 128), F32),
               pltpu.VMEM((2, nc, N_GATES, MA_CHUNK), F32), pltpu.VMEM((2, T, 128), F32)]
    return pl.pallas_call(
        functools.partial(_mlstm_kernel, T=T, latent=latent),
        out_shape=out_shape, grid=(B,), in_specs=in_specs, out_specs=out_specs, scratch_shapes=scratch,
        input_output_aliases=aliases,
        compiler_params=_cparams("arbitrary"),
        name="mlstm_lat" if latent else "mlstm_ctx",
    )(*args)


HG_SUB = 8


def _hgrn_kernel(*refs, T, latent):
    ff_ref, fb_ref, q_ref, i_ref, g_ref, lbf_ref, lbb_ref = refs[:7]
    if latent:
        s0_ref = refs[7]
        c_ref, of, ob, ST, iT, As, Bs = refs[-7:]
    else:
        c_ref, so_ref, of, ob, ST, iT, As, Bs = refs[-8:]
    L = HG_CHUNK
    NC = T // L
    NB = L // HG_SUB
    DK = HG_DK

    for d in range(2):
        for h in range(HG_HEADS):
            ST[d * HG_HEADS + h] = s0_ref[d, h].T if latent else jnp.zeros((HG_DV, DK), F32)

    PER = 128 // L

    def i_block(tb, carry):
        r0 = pl.multiple_of(tb * 128, 128)
        for h in range(HG_HEADS):
            blk = i_ref[pl.ds(r0, 128), h * HG_DV:(h + 1) * HG_DV].astype(F32).T.astype(BF16)
            for j in range(PER):
                iT[tb * PER + j, h] = blk[:, j * L:(j + 1) * L]
        return carry

    lax.fori_loop(0, T // 128, i_block, 0)

    low = _tri(L, False)
    upp = _tri(L, True)
    row8 = lax.broadcasted_iota(jnp.int32, (HG_SUB, L), 0)
    lane_s = lax.broadcasted_iota(jnp.int32, (HG_SUB, L), 1)
    heads = range(HG_HEADS)
    bnt = (((2,), (2,)), ((0,), (0,)))
    bnn = (((2,), (1,)), ((0,), (0,)))
    LOG2E = 1.4426950408889634

    def split(x):
        return jnp.stack([x[:, h * DK:(h + 1) * DK] for h in heads], 0)

    def decay_sums(c, carry):
        t0 = pl.multiple_of(c * L, L)
        for d in range(2):
            fpre = (ff_ref if d == 0 else fb_ref)[pl.ds(t0, L), :]
            lb = (lbf_ref if d == 0 else lbb_ref)[...]
            f = lb + (1.0 - lb) * jax.nn.sigmoid(fpre)
            a = jnp.dot(low if d == 0 else upp, jnp.log(f) * LOG2E, precision=HI, preferred_element_type=F32)
            As[d, pl.ds(t0, L), :] = a
            Bs[d, pl.ds(t0, L), :] = a - jnp.log(1.0 - f) * LOG2E
        return carry

    lax.fori_loop(0, NC, decay_sums, 0, unroll=4)

    def chunk_pair(cf, cb):
        tf = pl.multiple_of(cf * L, L)
        tb = pl.multiple_of(cb * L, L)
        both = lambda fn: jnp.concatenate([fn(0, tf), fn(1, tb)], 0)
        A = both(lambda d, t: split(As[d, pl.ds(t, L), :]))
        B = both(lambda d, t: split(Bs[d, pl.ds(t, L), :]))
        q = both(lambda d, t: split(_silu(q_ref[pl.ds(t, L), :].astype(F32))))
        iv = both(lambda d, t: split(i_ref[pl.ds(t, L), :].astype(BF16)))
        ivT = jnp.concatenate([iT[cf], iT[cb]], 0)
        H = HG_HEADS
        st = ST[...]
        o = lax.dot_general((q * jnp.exp2(A)).astype(BF16), st.astype(BF16), bnt, preferred_element_type=F32)
        a_last = jnp.concatenate([A[:H, L - 1:L], A[H:, 0:1]], 0)
        kd = jnp.exp2(a_last - B).astype(BF16)
        rows = []
        for I in range(NB):
            lo, hi = I * HG_SUB, (I + 1) * HG_SUB
            A_I, q_I = A[:, lo:hi], q[:, lo:hi]
            att_f = jnp.zeros((H, HG_SUB, L), F32)
            att_b = jnp.zeros((H, HG_SUB, L), F32)
            for j in range(HG_SUB):
                s = lo + j
                col = jnp.sum(q_I * jnp.exp2(A_I - B[:, s:s + 1]), -1, keepdims=True)
                att_f = jnp.where((lane_s == s) & (row8 >= j), col[:H], att_f)
                att_b = jnp.where((lane_s == s) & (row8 <= j), col[H:], att_b)
            rf, rb = max(lo - 1, 0), min(hi, L - 1)
            R = jnp.concatenate([A[:H, rf:rf + 1], A[H:, rb:rb + 1]], 0)
            zeros = lambda n: jnp.zeros((H, n, DK), BF16)
            ksc_f = jnp.concatenate([jnp.exp2(R[:H] - B[:H, :lo]).astype(BF16), zeros(L - lo)], 1) if I > 0 else zeros(L)
            ksc_b = jnp.concatenate([zeros(hi), jnp.exp2(R[H:] - B[H:, hi:]).astype(BF16)], 1) if I < NB - 1 else zeros(L)
            ksc = jnp.concatenate([ksc_f, ksc_b], 0)
            off = lax.dot_general((q_I * jnp.exp2(A_I - R)).astype(BF16), ksc, bnt, preferred_element_type=F32)
            rows.append(jnp.concatenate([att_f, att_b], 0) + off)
        att = jnp.concatenate(rows, 1)
        o = o + lax.dot_general(att.astype(BF16), iv, bnn, preferred_element_type=F32)
        for h in heads:
            of[pl.ds(tf, L), h * HG_DV:(h + 1) * HG_DV] = o[h]
            ob[pl.ds(tb, L), h * HG_DV:(h + 1) * HG_DV] = o[H + h]
        ST[...] = st * jnp.exp2(a_last) + lax.dot_general(ivT, kd, bnn, preferred_element_type=F32)

    def body(i, carry):
        chunk_pair(i, NC - 1 - i)
        return carry

    lax.fori_loop(0, NC, body, 0, unroll=4)

    def epilogue(r, carry):
        t0 = pl.multiple_of(r * 128, 128)
        o = of[pl.ds(t0, 128), :] + ob[pl.ds(t0, 128), :]
        gsil = _silu(g_ref[pl.ds(t0, 128), :].astype(F32))
        outs = []
        for h in range(HG_HEADS):
            oh = o[:, h * HG_DV:(h + 1) * HG_DV]
            outs.append(oh * lax.rsqrt(jnp.mean(oh * oh, -1, keepdims=True) + RMS_EPS))
        c_ref[pl.ds(t0, 128), :] = (jnp.concatenate(outs, -1) * gsil).astype(BF16)
        return carry

    lax.fori_loop(0, T // 128, epilogue, 0)
    if not latent:
        for d in range(2):
            for h in range(HG_HEADS):
                so_ref[d, h] = ST[d * HG_HEADS + h].T


def _hgrn(p16, p32, lb_l, l, latent, S0=None, c_out=None, prev=None):
    T = DEC_SEQ if latent else SEQ
    B = DEC_BATCH if latent else BATCH
    rb0 = N_CTX // DEC_SEQ if latent else 0
    W = BRANCH_W
    full2 = lambda b: (0, 0)
    any_spec = pl.BlockSpec(memory_space=pl.ANY)
    col = lambda j: pl.BlockSpec((T, W), lambda b: (rb0 + b, j))
    s_spec = pl.BlockSpec((None, None, 2, HG_HEADS, HG_DK, HG_DV), lambda b: (b, l, 0, 0, 0, 0))
    in_specs = [col(0), col(1), col(7), col(8), col(9), pl.BlockSpec((1, W), full2), pl.BlockSpec((1, W), full2)]
    args = [p32, p32, p16, p16, p16, lb_l[0][None, :], lb_l[1][None, :]]
    c_shape = jax.ShapeDtypeStruct((N_TOK, W), BF16)
    c_spec = pl.BlockSpec((T, W), lambda b: (rb0 + b, 0))
    aliases = {}
    if latent:
        in_specs += [s_spec, any_spec]
        args += [S0, c_out]
        aliases = {8: 0}
        out_shape, out_specs = c_shape, c_spec
    else:
        out_shape = (c_shape, jax.ShapeDtypeStruct((B, DEPTH, 2, HG_HEADS, HG_DK, HG_DV), F32))
        out_specs = (c_spec, s_spec)
        if prev is not None:
            in_specs.append(any_spec)
            args.append(prev)
            aliases = {7: 1}
    scratch = [pltpu.VMEM((T, W), F32), pltpu.VMEM((T, W), F32), pltpu.VMEM((2 * HG_HEADS, HG_DV, HG_DK), F32),
               pltpu.VMEM((T // HG_CHUNK, HG_HEADS, HG_DV, HG_CHUNK), BF16),
               pltpu.VMEM((2, T, W), F32), pltpu.VMEM((2, T, W), F32)]
    return pl.pallas_call(
        functools.partial(_hgrn_kernel, T=T, latent=latent),
        out_shape=out_shape, grid=(B,), in_specs=in_specs, out_specs=out_specs, scratch_shapes=scratch,
        input_output_aliases=aliases,
        compiler_params=_cparams("arbitrary"),
        name="hgrn_lat" if latent else "hgrn_ctx",
    )(*args)


def _merge_kernel(a_ref, b_ref, c_ref, ga_ref, gb_ref, gc_ref, xc_ref, xl_ref, g1_ref, sh2_ref, sc2_ref,
                  wb_ref, wo_ref, lng_ref, lnb_ref, wr_ref, x1_ref, h2_ref, *, tm):
    def br(v_ref, g_ref, k):
        return jax.nn.sigmoid(g_ref[...].astype(F32)) * jnp.dot(v_ref[...], wb_ref[k], preferred_element_type=F32)

    mix = br(a_ref, ga_ref, 0) + br(b_ref, gb_ref, 1) + br(c_ref, gc_ref, 2)
    y = jnp.dot(mix.astype(BF16), wo_ref[...], preferred_element_type=F32)
    x = _pair_read(pl.program_id(0), tm, xc_ref, xl_ref)
    x1 = _layer_norm(DEEPNORM_ALPHA * x + g1_ref[...] * y, lng_ref[...], lnb_ref[...])
    x1_ref[...] = x1
    h2 = x1 * (1.0 + sc2_ref[...]) + sh2_ref[...]
    h2_ref[:, :D_MODEL] = h2.astype(BF16)
    lt = lax.dot_general(wr_ref[...], h2, _NT, preferred_element_type=F32, precision=HI)
    r = lax.broadcasted_iota(jnp.int32, lt.shape, 0)
    neg = -jnp.inf
    lg = jnp.where(r < N_GROUPS, lt, neg)
    mg = jnp.max(lg, 0, keepdims=True)
    g_sel = jnp.min(jnp.where(lg == mg, r, ROUTER_ROWS), 0, keepdims=True)
    p_sel = 1.0 / jnp.sum(jnp.where(r < N_GROUPS, jnp.exp(lg - mg), 0.0), 0, keepdims=True)
    lo = ROUTER_E0 + EXP_PER_GROUP * g_sel
    le = jnp.where((r >= lo) & (r < lo + EXP_PER_GROUP), lt, neg)
    v1 = jnp.max(le, 0, keepdims=True)
    i1 = jnp.min(jnp.where(le == v1, r, ROUTER_ROWS), 0, keepdims=True)
    le2 = jnp.where(r == i1, neg, le)
    v2 = jnp.max(le2, 0, keepdims=True)
    i2 = jnp.min(jnp.where(le2 == v2, r, ROUTER_ROWS), 0, keepdims=True)
    e2 = jnp.exp(v2 - v1)
    w1 = p_sel / (1.0 + e2)
    w2 = p_sel * e2 / (1.0 + e2)
    w1_hi = w1.astype(BF16).astype(F32)
    w2_hi = w2.astype(BF16).astype(F32)
    j1, j2 = i1 - lo, i2 - lo
    packed = jnp.where(r == j1, w1_hi, jnp.where(r == j2, w2_hi, jnp.where(
        r == j1 + EXP_PER_GROUP, w1 - w1_hi, jnp.where(r == j2 + EXP_PER_GROUP, w2 - w2_hi, jnp.where(
            r == 2 * EXP_PER_GROUP, g_sel.astype(F32), 0.0)))))
    packed = jnp.concatenate([packed, jnp.zeros((128 - ROUTER_ROWS, packed.shape[1]), F32)], 0)
    h2_ref[:, D_MODEL:] = packed.T.astype(BF16)


def _merge(a, b, c, p16, xc, xl, mod, wb, wo, lng, lnb, wr, l):
    tm = 512
    tok = lambda i: (i, 0)
    ln_spec = pl.BlockSpec((None, None, 1, D_MODEL), lambda i: (l, 0, 0, 0))
    return pl.pallas_call(
        functools.partial(_merge_kernel, tm=tm),
        out_shape=(jax.ShapeDtypeStruct((N_TOK, D_MODEL), F32), jax.ShapeDtypeStruct((N_TOK, MOE_XW), BF16)),
        grid=(N_TOK // tm,),
        in_specs=[pl.BlockSpec((tm, BRANCH_W), tok), pl.BlockSpec((tm, BRANCH_W), tok), pl.BlockSpec((tm, BRANCH_W), tok),
                  pl.BlockSpec((tm, D_MODEL), lambda i: (i, 5)), pl.BlockSpec((tm, D_MODEL), lambda i: (i, 6)),
                  pl.BlockSpec((tm, D_MODEL), lambda i: (i, 7)),
                  *_pair_specs(tm), _mod_spec(l, 2, tm), _mod_spec(l, 3, tm), _mod_spec(l, 4, tm),
                  pl.BlockSpec((None, 3, BRANCH_W, D_MODEL), lambda i: (l, 0, 0, 0)),
                  pl.BlockSpec((None, D_MODEL, D_MODEL), lambda i: (l, 0, 0)),
                  ln_spec, ln_spec,
                  pl.BlockSpec((None, ROUTER_ROWS, D_MODEL), lambda i: (l, 0, 0))],
        out_specs=(pl.BlockSpec((tm, D_MODEL), tok), pl.BlockSpec((tm, MOE_XW), tok)),
        compiler_params=_cparams("arbitrary"),
        name="merge",
    )(a, b, c, p16, p16, p16, xc, xl, mod, mod, mod, wb, wo, lng, lnb, wr)


def _moe_up_kernel(gid_ref, nused_ref, x_ref, w1_ref, w3_ref, hid_ref, w1b, w3b):
    f = pl.program_id(0)
    t = pl.program_id(1)

    @pl.when(t < nused_ref[0])
    def _():
        first = jnp.logical_or(t == 0, gid_ref[t] != gid_ref[jnp.maximum(t - 1, 0)])

        @pl.when(first)
        def _():
            w1b[...] = w1_ref[...].astype(BF16)
            w3b[...] = w3_ref[...].astype(BF16)

        x = x_ref[:, :D_MODEL]
        rec = x_ref[:, D_MODEL:].astype(F32)
        lane = lax.broadcasted_iota(jnp.int32, rec.shape, 1)
        for j in range(MOE_FE):
            e = f * MOE_FE + j
            a = jnp.dot(x, w1b[j], preferred_element_type=F32)
            b = jnp.dot(x, w3b[j], preferred_element_type=F32)
            gcol = jnp.sum(jnp.where(jnp.logical_or(lane == e, lane == e + EXP_PER_GROUP), rec, 0.0), -1, keepdims=True)
            hid_ref[:, j * D_EXPERT:(j + 1) * D_EXPERT] = (_silu(a) * b * gcol).astype(BF16)


def _moe_tile(t, n):
    return jnp.minimum(t, n[0] - 1)


def _moe_up(gid, nused, xs, w1, w3, l):
    tm = MOE_TM
    npad = MOE_NT * tm
    nf = EXP_PER_GROUP // MOE_FE
    w_spec = pl.BlockSpec((None, MOE_FE, D_MODEL, D_EXPERT),
                          lambda f, t, g, n: (l, nf * g[_moe_tile(t, n)] + f, 0, 0))
    grid_spec = pltpu.PrefetchScalarGridSpec(
        num_scalar_prefetch=2,
        grid=(nf, MOE_NT),
        in_specs=[pl.BlockSpec((tm, MOE_XW), lambda f, t, g, n: (_moe_tile(t, n), 0)), w_spec, w_spec],
        out_specs=pl.BlockSpec((tm, MOE_FE * D_EXPERT), lambda f, t, g, n: (_moe_tile(t, n), f)),
        scratch_shapes=[pltpu.VMEM((MOE_FE, D_MODEL, D_EXPERT), BF16), pltpu.VMEM((MOE_FE, D_MODEL, D_EXPERT), BF16)],
    )
    return pl.pallas_call(
        _moe_up_kernel,
        out_shape=jax.ShapeDtypeStruct((npad, EXP_PER_GROUP * D_EXPERT), BF16),
        grid_spec=grid_spec,
        compiler_params=_cparams("arbitrary", "arbitrary"),
        name="moe_up",
    )(gid, nused, xs, w1, w3)


def _moe_down_kernel(gid_ref, nused_ref, hid_ref, w2_ref, y_ref, w2b):
    t = pl.program_id(0)

    @pl.when(t < nused_ref[0])
    def _():
        first = jnp.logical_or(t == 0, gid_ref[t] != gid_ref[jnp.maximum(t - 1, 0)])

        @pl.when(first)
        def _():
            w2b[...] = w2_ref[...].astype(BF16)

        y_ref[...] = jnp.dot(hid_ref[...], w2b[...], preferred_element_type=F32)


def _moe_down(gid, nused, hid, w2g, l):
    tm = MOE_TM
    npad = MOE_NT * tm
    hw = EXP_PER_GROUP * D_EXPERT
    grid_spec = pltpu.PrefetchScalarGridSpec(
        num_scalar_prefetch=2,
        grid=(MOE_NT,),
        in_specs=[pl.BlockSpec((tm, hw), lambda t, g, n: (_moe_tile(t, n), 0)),
                  pl.BlockSpec((None, None, hw, D_MODEL), lambda t, g, n: (l, g[_moe_tile(t, n)], 0, 0))],
        out_specs=pl.BlockSpec((tm, D_MODEL), lambda t, g, n: (_moe_tile(t, n), 0)),
        scratch_shapes=[pltpu.VMEM((hw, D_MODEL), BF16)],
    )
    return pl.pallas_call(
        _moe_down_kernel,
        out_shape=jax.ShapeDtypeStruct((npad, D_MODEL), F32),
        grid_spec=grid_spec,
        compiler_params=_cparams("arbitrary"),
        name="moe_down",
    )(gid, nused, hid, w2g)


def _moe(h2x, w1, w3, w2g, l):
    tm = MOE_TM
    npad = MOE_NT * tm
    g = h2x[:, D_MODEL + 2 * EXP_PER_GROUP].astype(jnp.int32)
    onehot = (g[:, None] == jnp.arange(N_GROUPS)[None, :]).astype(jnp.int32)
    counts = jnp.sum(onehot, 0)
    rank = jnp.sum((jnp.cumsum(onehot, 0) - onehot) * onehot, 1)
    padded = (counts + tm - 1) // tm * tm
    ends = jnp.cumsum(padded)
    offs = ends - padded
    dest = offs[g] + rank
    src = jnp.zeros((npad,), jnp.int32).at[dest].set(jnp.arange(N_TOK, dtype=jnp.int32), unique_indices=True)
    starts = jnp.arange(MOE_NT, dtype=jnp.int32) * tm
    tile_gid = jnp.minimum(jnp.sum((ends[None, :] <= starts[:, None]).astype(jnp.int32), 1), N_GROUPS - 1)
    nused = (ends[-1:] // tm).astype(jnp.int32)
    take = lambda arr, idx: arr.at[idx].get(mode="promise_in_bounds", unique_indices=False)
    hid = _moe_up(tile_gid, nused, take(h2x, src), w1, w3, l)
    ys = _moe_down(tile_gid, nused, hid, w2g, l)
    return take(ys, dest)


def _final_kernel(*refs, tm, with_h):
    x1_ref, y_ref, g2_ref, lng_ref, lnb_ref = refs[:5]
    x2 = _layer_norm(DEEPNORM_ALPHA * x1_ref[...] + g2_ref[...] * y_ref[...], lng_ref[...], lnb_ref[...])
    i = pl.program_id(0)
    if with_h:
        sh_ref, sc_ref, xc_ref, xl_ref, h_ref = refs[5:]
        h_ref[...] = (x2 * (1.0 + sc_ref[...]) + sh_ref[...]).astype(BF16)
    else:
        xc_ref, xl_ref = refs[5:]

    @pl.when(i < N_CTX // tm)
    def _():
        xc_ref[...] = x2

    @pl.when(i >= N_CTX // tm)
    def _():
        xl_ref[...] = x2


def _final(x1, y, mod, lng, lnb, l):
    tm = 1024
    tok = lambda i: (i, 0)
    with_h = l + 1 < DEPTH
    ln_spec = pl.BlockSpec((None, None, 1, D_MODEL), lambda i: (l, 1, 0, 0))
    half = jax.ShapeDtypeStruct((N_CTX, D_MODEL), F32)
    in_specs = [pl.BlockSpec((tm, D_MODEL), tok), pl.BlockSpec((tm, D_MODEL), tok), _mod_spec(l, 5, tm), ln_spec, ln_spec]
    args = [x1, y, mod, lng, lnb]
    out_shape = [half, half]
    out_specs = list(_pair_specs(tm))
    if with_h:
        in_specs += [_mod_spec(l + 1, 0, tm), _mod_spec(l + 1, 1, tm)]
        args += [mod, mod]
        out_shape.append(jax.ShapeDtypeStruct((N_TOK, D_MODEL), BF16))
        out_specs.append(pl.BlockSpec((tm, D_MODEL), tok))
    return pl.pallas_call(
        functools.partial(_final_kernel, tm=tm, with_h=with_h),
        out_shape=tuple(out_shape), grid=(N_TOK // tm,), in_specs=in_specs, out_specs=tuple(out_specs),
        compiler_params=_cparams("arbitrary"),
        name="final",
    )(*args)


def kernel(x_prompt, x_sample, c, cache_na_k, cache_na_v, state_mlstm_C, state_mlstm_n, state_mlstm_m, state_hgrn,
           c_ctx, w_mod, b_mod, w_in, b_in, mlstm_fbias, hgrn_lb_logits, na_rpb, w_branch, w_out, ln_g, ln_b,
           w_rg, w_re, w_e1, w_e3, w_e2):
    assert N_CTX == N_LAT
    lb_cum = jnp.cumsum(jax.nn.softmax(hgrn_lb_logits.astype(F32), axis=1), axis=1)
    lb_all = lb_cum - lb_cum[:, :1]

    cs = jnp.zeros((N_MODROWS, D_MODEL), F32).at[0].set(c_ctx).at[1:1 + DEC_BATCH].set(c)
    mod = _modulation(cs, w_mod, b_mod).reshape(DEPTH, N_MODROWS, 6, 1, D_MODEL)

    wb = w_branch.astype(BF16)
    wo = w_out.astype(BF16)
    lng = ln_g.reshape(DEPTH, 2, 1, D_MODEL)
    lnb = ln_b.reshape(DEPTH, 2, 1, D_MODEL)
    wr = jnp.zeros((DEPTH, ROUTER_ROWS, D_MODEL), F32)
    wr = wr.at[:, :N_GROUPS].set(jnp.swapaxes(w_rg, 1, 2)).at[:, ROUTER_E0:ROUTER_E0 + N_EXPERTS].set(jnp.swapaxes(w_re, 1, 2))
    w2g = w_e2.reshape(DEPTH, N_GROUPS, EXP_PER_GROUP * D_EXPERT, D_MODEL)
    b_main = jnp.concatenate([b_in[:, :GATE_COL0], b_in[:, GATE_COL0 + N_GATES:]], 1)

    xc = x_prompt.reshape(N_CTX, D_MODEL)
    xl = x_sample.reshape(N_LAT, D_MODEL)
    w_t = jnp.swapaxes(w_in, 1, 2)
    h = _prep(xc, xl, mod)

    kv = (None, None)
    ma_states = None
    hg_state = None
    for l in range(DEPTH):
        p16, p32 = _inproj(h, w_t, b_main[l][None, :], l)
        gcol, gt = _gates(h, w_t, b_in.reshape(DEPTH, 1, N_IN), l)
        gt3 = gt.reshape(N_GATES, N_TOK // MA_CHUNK, MA_CHUNK).transpose(1, 0, 2)

        a, *ma_states = _mlstm(p16, gcol, gt3, mlstm_fbias[l], l, False, prev=ma_states)
        a = _mlstm(p16, gcol, gt3, mlstm_fbias[l], l, True, state_mlstm_C, state_mlstm_n, state_mlstm_m, a_out=a)
        b, *kv = _ctx_attention(p16, l, *kv)
        b = _lat_attention(p16, cache_na_k, cache_na_v, _na_bias_table(na_rpb[l]), l, b)
        cc, hg_state = _hgrn(p16, p32, lb_all[:, l], l, False, prev=hg_state)
        cc = _hgrn(p16, p32, lb_all[:, l], l, True, state_hgrn, c_out=cc)

        x1, h2x = _merge(a, b, cc, p16, xc, xl, mod, wb, wo, lng, lnb, wr, l)
        y2 = _moe(h2x, w_e1, w_e3, w2g, l)
        outs = _final(x1, y2, mod, lng, lnb, l)
        xc, xl = outs[0], outs[1]
        if l + 1 < DEPTH:
            h = outs[2]

    dt = x_prompt.dtype
    new_C, new_n, new_m = ma_states
    new_n = new_n.reshape(BATCH, DEPTH, 2, MA_HEADS, MA_DK)
    new_m = new_m[:, :, :, 0].reshape(BATCH, DEPTH, 2, MA_HEADS)
    return (xc.reshape(BATCH, SEQ, D_MODEL), xl.reshape(DEC_BATCH, DEC_SEQ, D_MODEL), kv[0], kv[1],
            new_C.astype(dt), new_n.astype(dt), new_m.astype(dt), hg_state.astype(dt))
```

```python
import functools

import numpy as np
import jax
import jax.numpy as jnp
from jax import lax
from jax.experimental import pallas as pl
from jax.experimental.pallas import tpu as pltpu

F32 = jnp.float32
BF16 = jnp.bfloat16
HI = lax.Precision.HIGHEST

D_MODEL = 1024
BATCH = 16
SEQ = 256
DEPTH = 2
DEC_BATCH = 4
DEC_SEQ = 1024
PAST_LEN = 256
GRID_W = 64
MA_HEADS = 4
MA_DK = 128
MA_DV = 128
MA_CHUNK = 64
NA_HEADS = 8
NA_DH = 64
NA_KR_MAX = 8
NA_KC = 16
HG_HEADS = 4
HG_DK = 128
HG_DV = 128
HG_CHUNK = 32
BRANCH_W = 512
N_GROUPS = 4
EXP_PER_GROUP = 4
N_EXPERTS = N_GROUPS * EXP_PER_GROUP
D_EXPERT = 512
ROPE_BASE = 10000.0
LN_EPS = 1e-5
RMS_EPS = 1e-6
DEEPNORM_ALPHA = (2 * DEPTH) ** 0.25

N_CTX = BATCH * SEQ
N_LAT = DEC_BATCH * DEC_SEQ
N_TOK = N_CTX + N_LAT
N_MODROWS = 8
GATE_COL0 = 4 * BRANCH_W
N_GATES = 4 * MA_HEADS
N_IN = 9232
P_COLS = N_IN - N_GATES
MOE_TM = 512
MOE_NT = N_TOK // MOE_TM + N_GROUPS
MOE_FE = 4
MOE_XW = D_MODEL + 128
ROUTER_ROWS = 32
ROUTER_E0 = 8
VMEM_LIMIT = 56 * 1024 * 1024

_NT = (((1,), (1,)), ((), ()))
_TN = (((0,), (0,)), ((), ()))


def _cparams(*sem):
    return pltpu.CompilerParams(dimension_semantics=sem, vmem_limit_bytes=VMEM_LIMIT)


def _mod_row(tile, tm):
    return jnp.maximum((tile * tm) // DEC_SEQ - (N_CTX // DEC_SEQ - 1), 0)


def _mod_spec(l, part, tm):
    return pl.BlockSpec((None, None, None, 1, D_MODEL), lambda i: (l, _mod_row(i, tm), part, 0, 0))


def _pair_specs(tm):
    nc = N_CTX // tm
    return (pl.BlockSpec((tm, D_MODEL), lambda i: (jnp.minimum(i, nc - 1), 0)),
            pl.BlockSpec((tm, D_MODEL), lambda i: (jnp.maximum(i - nc, 0), 0)))


def _pair_read(i, tm, c_ref, l_ref):
    return jnp.where(i < N_CTX // tm, c_ref[...], l_ref[...])


def _silu(x):
    return x * jax.nn.sigmoid(x)


def _layer_norm(x, g, b):
    mu = jnp.mean(x, -1, keepdims=True)
    xc = x - mu
    var = jnp.mean(xc * xc, -1, keepdims=True)
    return xc * lax.rsqrt(var + LN_EPS) * g + b


def _log_sigmoid(x):
    return jnp.minimum(x, 0.0) - jnp.log(1.0 + jnp.exp(-jnp.abs(x)))


def _tri(n, upper):
    r = lax.broadcasted_iota(jnp.int32, (n, n), 0)
    c = lax.broadcasted_iota(jnp.int32, (n, n), 1)
    return jnp.where((r <= c) if upper else (r >= c), 1.0, 0.0).astype(F32)


def _mod_kernel(c_ref, w_ref, b_ref, o_ref):
    s = _silu(c_ref[...])
    o_ref[...] = jnp.dot(s.astype(BF16), w_ref[...].astype(BF16), preferred_element_type=F32) + b_ref[...]


def _modulation(cs, w_mod, b_mod):
    tn = 1024
    return pl.pallas_call(
        _mod_kernel,
        out_shape=jax.ShapeDtypeStruct((DEPTH, N_MODROWS, 6 * D_MODEL), F32),
        grid=(DEPTH, 6 * D_MODEL // tn),
        in_specs=[pl.BlockSpec((N_MODROWS, D_MODEL), lambda l, j: (0, 0)),
                  pl.BlockSpec((None, D_MODEL, tn), lambda l, j: (l, 0, j)),
                  pl.BlockSpec((None, 1, tn), lambda l, j: (l, 0, j))],
        out_specs=pl.BlockSpec((None, N_MODROWS, tn), lambda l, j: (l, 0, j)),
        compiler_params=_cparams("arbitrary", "arbitrary"),
        name="modulation",
    )(cs, w_mod, b_mod.reshape(DEPTH, 1, 6 * D_MODEL))


def _prep_kernel(xc_ref, xl_ref, sh_ref, sc_ref, h_ref, *, tm):
    x = _pair_read(pl.program_id(0), tm, xc_ref, xl_ref)
    h_ref[...] = (x * (1.0 + sc_ref[...]) + sh_ref[...]).astype(BF16)


def _prep(xc, xl, mod):
    tm = 1024
    return pl.pallas_call(
        functools.partial(_prep_kernel, tm=tm),
        out_shape=jax.ShapeDtypeStruct((N_TOK, D_MODEL), BF16),
        grid=(N_TOK // tm,),
        in_specs=[*_pair_specs(tm), _mod_spec(0, 0, tm), _mod_spec(0, 1, tm)],
        out_specs=pl.BlockSpec((tm, D_MODEL), lambda i: (i, 0)),
        compiler_params=_cparams("arbitrary"),
        name="prep",
    )(xc, xl, mod, mod)


INPROJ_TN = 512
N_PLAIN_TILES = GATE_COL0 // INPROJ_TN


F32_TILE0 = 7
N_F32_TILES = 2
P16_COLS = P_COLS - N_F32_TILES * INPROJ_TN


def _inproj_kernel(h_ref, wa_ref, wb_ref, b_ref, o_ref, *, src_tile):
    j = src_tile(pl.program_id(1))

    @pl.when(j < N_PLAIN_TILES)
    def _():
        o_ref[...] = (lax.dot_general(h_ref[...], wa_ref[...].astype(BF16), _NT, preferred_element_type=F32)
                      + b_ref[...]).astype(o_ref.dtype)

    @pl.when(j >= N_PLAIN_TILES)
    def _():
        w = jnp.concatenate([wa_ref[N_GATES:, :], wb_ref[...]], 0)
        o_ref[...] = (lax.dot_general(h_ref[...], w.astype(BF16), _NT, preferred_element_type=F32)
                      + b_ref[...]).astype(o_ref.dtype)


def _inproj_call(h, w_t, b_main, l, tm, n_tiles, src_tile, dtype, name):
    tn = INPROJ_TN
    return pl.pallas_call(
        functools.partial(_inproj_kernel, src_tile=src_tile),
        out_shape=jax.ShapeDtypeStruct((N_TOK, n_tiles * tn), dtype),
        grid=(N_TOK // tm, n_tiles),
        in_specs=[pl.BlockSpec((tm, D_MODEL), lambda i, j: (i, 0)),
                  pl.BlockSpec((None, tn, D_MODEL), lambda i, j: (l, src_tile(j), 0)),
                  pl.BlockSpec((None, N_GATES, D_MODEL), lambda i, j: (l, (src_tile(j) + 1) * (tn // N_GATES), 0)),
                  pl.BlockSpec((1, tn), lambda i, j: (0, src_tile(j)))],
        out_specs=pl.BlockSpec((tm, tn), lambda i, j: (i, j)),
        compiler_params=_cparams("arbitrary", "arbitrary"),
        name=name,
    )(h, w_t, w_t, b_main)


def _inproj(h, w_t, b_main, l):
    skip_f32 = lambda j: jnp.where(j < F32_TILE0, j, j + N_F32_TILES)
    p16 = _inproj_call(h, w_t, b_main, l, 4096, P16_COLS // INPROJ_TN, skip_f32, BF16, "inproj")
    p32 = _inproj_call(h, w_t, b_main, l, 4096, N_F32_TILES, lambda j: j + F32_TILE0, F32, "inproj_f32")
    return p16, p32


def _gates_kernel(h_ref, w_ref, b_ref, gc_ref, gt_ref):
    g = lax.dot_general(h_ref[...], w_ref[...].astype(BF16), _NT, preferred_element_type=F32) + b_ref[...]
    gc_ref[...] = g
    gt_ref[...] = g.T[:N_GATES]


def _gates(h, w_t, b_in3, l):
    tm = 1024
    gblk = GATE_COL0 // 128
    return pl.pallas_call(
        _gates_kernel,
        out_shape=(jax.ShapeDtypeStruct((N_TOK, 128), F32), jax.ShapeDtypeStruct((N_GATES, N_TOK), F32)),
        grid=(N_TOK // tm,),
        in_specs=[pl.BlockSpec((tm, D_MODEL), lambda i: (i, 0)),
                  pl.BlockSpec((None, 128, D_MODEL), lambda i: (l, gblk, 0)),
                  pl.BlockSpec((None, 1, 128), lambda i: (l, 0, gblk))],
        out_specs=(pl.BlockSpec((tm, 128), lambda i: (i, 0)), pl.BlockSpec((N_GATES, tm), lambda i: (0, i))),
        compiler_params=_cparams("arbitrary"),
        name="gates",
    )(h, w_t, b_in3)


HEADS_PER_BLK = 128 // NA_DH
NA_NBLK = NA_HEADS // HEADS_PER_BLK
NA_QSCALE = NA_DH ** -0.5
Q_COL, K_COL, V_COL = 16, 20, 24
QKV_COL = 4


def _ctx_attn_kernel(*refs):
    q_ref, k_ref, v_ref = refs[:3]
    o_ref, ko_ref, vo_ref = refs[-3:]
    heads = range(NA_HEADS)
    split = lambda x: jnp.stack([x[:, h * NA_DH:(h + 1) * NA_DH] for h in heads], 0)
    q = split(q_ref[...] * NA_QSCALE)
    k = split(k_ref[...])
    v = split(v_ref[...])
    ko_ref[...] = k.astype(F32)
    vo_ref[...] = v.astype(F32)
    s = lax.dot_general(q, k, (((2,), (2,)), ((0,), (0,))), preferred_element_type=F32)
    e = jnp.exp(s - jnp.max(s, -1, keepdims=True))
    p = e * (1.0 / jnp.sum(e, -1, keepdims=True))
    o = lax.dot_general(p.astype(BF16), v, (((2,), (1,)), ((0,), (0,))), preferred_element_type=F32)
    o_ref[...] = jnp.concatenate([o[h] for h in heads], -1).astype(BF16)


def _ctx_attention(p16, l, prev_k=None, prev_v=None):
    kv_shape = jax.ShapeDtypeStruct((BATCH, DEPTH, NA_HEADS, SEQ, NA_DH), F32)
    kv_spec = pl.BlockSpec((None, None, NA_HEADS, SEQ, NA_DH), lambda b: (b, l, 0, 0, 0))
    col = lambda j: pl.BlockSpec((SEQ, BRANCH_W), lambda b: (b, j))
    in_specs = [col(QKV_COL), col(QKV_COL + 1), col(QKV_COL + 2)]
    args = [p16, p16, p16]
    aliases = {}
    if prev_k is not None:
        in_specs += [pl.BlockSpec(memory_space=pl.ANY)] * 2
        args += [prev_k, prev_v]
        aliases = {3: 1, 4: 2}
    return pl.pallas_call(
        _ctx_attn_kernel,
        out_shape=(jax.ShapeDtypeStruct((N_TOK, BRANCH_W), BF16), kv_shape, kv_shape),
        grid=(BATCH,),
        in_specs=in_specs,
        out_specs=(pl.BlockSpec((SEQ, BRANCH_W), lambda b: (b, 0)), kv_spec, kv_spec),
        input_output_aliases=aliases,
        compiler_params=_cparams("arbitrary"),
        name="ctx_attention",
    )(*args)


NA_ROWS = DEC_SEQ // GRID_W
NA_KR = min(NA_KR_MAX, NA_ROWS)
NA_QROWS = 4
NA_QT = NA_ROWS // NA_QROWS
NA_WROWS = NA_KR + NA_QROWS - 1
NA_WKEYS = NA_WROWS * GRID_W


def _na_window_start(t):
    return min(max(t * NA_QROWS - NA_KR // 2, 0), NA_ROWS - NA_WROWS)


def _na_bias_table(rpb):
    c = np.arange(GRID_W)
    c0 = np.clip(c - NA_KC // 2, 0, GRID_W - NA_KC)
    kc = np.arange(GRID_W)
    valid = (kc[None, :] >= c0[:, None]) & (kc[None, :] < c0[:, None] + NA_KC)
    dc = kc[None, :] - c[:, None] + NA_KC - 1
    onehot = (dc[None] == np.arange(2 * NA_KC - 1)[:, None, None]) & valid[None]
    toep = jnp.einsum('hrd,dcx->hrcx', rpb.astype(F32), jnp.asarray(onehot, F32), precision=HI)
    toep = jnp.where(valid[None, None], toep, -jnp.inf)
    ninf = jnp.full((NA_HEADS, GRID_W, GRID_W), -jnp.inf, F32)
    tiles = []
    for t in range(NA_QT):
        w0 = _na_window_start(t)
        qrows = []
        for r in range(t * NA_QROWS, (t + 1) * NA_QROWS):
            r0 = min(max(r - NA_KR // 2, 0), NA_ROWS - NA_KR)
            blocks = []
            for kr in range(w0, w0 + NA_WROWS):
                inside = r0 <= kr < r0 + NA_KR
                blocks.append(toep[:, kr - r + NA_KR_MAX - 1] if inside else ninf)
            qrows.append(jnp.concatenate(blocks, -1))
        tiles.append(jnp.concatenate(qrows, 1))
    return jnp.stack(tiles, 1)


def _lat_attn_kernel(q_ref, k_ref, v_ref, ck_ref, cv_ref, bias_ref, prev_ref, o_ref):
    heads = range(HEADS_PER_BLK)
    split = lambda x: jnp.stack([x[:, h * NA_DH:(h + 1) * NA_DH] for h in heads], 0)
    bnt = (((2,), (2,)), ((0,), (0,)))
    bnn = (((2,), (1,)), ((0,), (0,)))
    q = (q_ref[...] * NA_QSCALE).astype(BF16)
    k = k_ref[...].astype(BF16)
    v = v_ref[...].astype(BF16)
    ck = ck_ref[...].astype(BF16)
    cv = cv_ref[...].astype(BF16)
    nq = NA_QROWS * GRID_W
    for t in range(NA_QT):
        w0 = _na_window_start(t)
        qs = slice(t * nq, (t + 1) * nq)
        ws = slice(w0 * GRID_W, (w0 + NA_WROWS) * GRID_W)
        qh = split(q[qs])
        s_loc = lax.dot_general(qh, split(k[ws]), bnt, preferred_element_type=F32) + bias_ref[:, t]
        s_ctx = lax.dot_general(qh, ck, bnt, preferred_element_type=F32)
        m = jnp.maximum(jnp.max(s_loc, -1, keepdims=True), jnp.max(s_ctx, -1, keepdims=True))
        e_loc = jnp.exp(s_loc - m)
        e_ctx = jnp.exp(s_ctx - m)
        inv = 1.0 / (jnp.sum(e_loc, -1, keepdims=True) + jnp.sum(e_ctx, -1, keepdims=True))
        acc = (lax.dot_general(e_loc.astype(BF16), split(v[ws]), bnn, preferred_element_type=F32)
               + lax.dot_general(e_ctx.astype(BF16), cv, bnn, preferred_element_type=F32)) * inv
        o_ref[qs, :] = jnp.concatenate([acc[h] for h in heads], -1).astype(BF16)


def _lat_attention(proj, ck, cv, bias, l, b_out):
    rb0 = N_CTX // DEC_SEQ
    cb = lambda base: (lambda j, b: (rb0 + b, base + j))
    c_spec = pl.BlockSpec((None, None, HEADS_PER_BLK, PAST_LEN, NA_DH), lambda j, b: (b, l, j, 0, 0))
    return pl.pallas_call(
        _lat_attn_kernel,
        out_shape=jax.ShapeDtypeStruct((N_TOK, BRANCH_W), BF16),
        grid=(NA_NBLK, DEC_BATCH),
        in_specs=[pl.BlockSpec((DEC_SEQ, 128), cb(Q_COL)), pl.BlockSpec((DEC_SEQ, 128), cb(K_COL)),
                  pl.BlockSpec((DEC_SEQ, 128), cb(V_COL)), c_spec, c_spec,
                  pl.BlockSpec((None, HEADS_PER_BLK, NA_QT, NA_QROWS * GRID_W, NA_WKEYS), lambda j, b: (l, j, 0, 0, 0)),
                  pl.BlockSpec(memory_space=pl.ANY)],
        out_specs=pl.BlockSpec((DEC_SEQ, 128), lambda j, b: (rb0 + b, j)),
        input_output_aliases={6: 0},
        compiler_params=_cparams("arbitrary", "arbitrary"),
        name="lat_attention",
    )(proj, proj, proj, ck, cv, bias, b_out)


MA_KSCALE = MA_DK ** -0.5


def _rope_tables(T):
    t = np.arange(T)
    half = MA_DK // 2
    inv = ROPE_BASE ** (-jnp.arange(0, half, 2, dtype=F32) / half)
    ang_r = jnp.asarray((t // GRID_W).astype(np.float32))[:, None] * inv[None, :]
    ang_c = jnp.asarray((t % GRID_W).astype(np.float32))[:, None] * inv[None, :]
    cos = jnp.concatenate([jnp.cos(ang_r)] * 2 + [jnp.cos(ang_c)] * 2, -1)
    sin = jnp.concatenate([-jnp.sin(ang_r), jnp.sin(ang_r), -jnp.sin(ang_c), jnp.sin(ang_c)], -1)
    return cos, sin


def _mlstm_kernel(*refs, T, latent):
    if latent:
        (p_ref, gc_ref, gt_ref, fbc_ref, fbr_ref, cos_ref, sin_ref, c0_ref, n0_ref, m0_ref, prev_ref,
         a_ref, qs, ks, vT, hfT, hbT, CT, ns, ms, brs, kcs) = refs
    else:
        p_ref, gc_ref, gt_ref, fbc_ref, fbr_ref = refs[:5]
        a_ref, co_ref, no_ref, mo_ref, qs, ks, vT, hfT, hbT, CT, ns, ms, brs, kcs = refs[-14:]
    L = MA_CHUNK
    NC = T // L
    W = BRANCH_W
    PER = 128 // L

    lane = lax.broadcasted_iota(jnp.int32, (T, MA_DK), 1)
    lo_half = (lane % (MA_DK // 2)) < (MA_DK // 4)

    def rope(x):
        if not latent:
            return x
        swapped = jnp.where(lo_half, pltpu.roll(x, MA_DK - MA_DK // 4, 1), pltpu.roll(x, MA_DK // 4, 1))
        return x * cos_ref[...] + swapped * sin_ref[...]

    for h in range(MA_HEADS):
        hs = slice(h * MA_DK, (h + 1) * MA_DK)
        qs[:, hs] = rope(p_ref[:, hs].astype(F32)).astype(BF16)
        ks[:, hs] = rope(p_ref[:, W + h * MA_DK:W + (h + 1) * MA_DK].astype(F32) * MA_KSCALE).astype(BF16)

    def v_block(tb, carry):
        r0 = pl.multiple_of(tb * 128, 128)
        for h in range(MA_HEADS):
            hs = slice(h * MA_DV, (h + 1) * MA_DV)
            blk = p_ref[pl.ds(r0, 128), 2 * W + h * MA_DV:2 * W + (h + 1) * MA_DV].astype(F32).T.astype(BF16)
            for j in range(PER):
                vT[tb * PER + j, h] = blk[:, j * L:(j + 1) * L]
        return carry

    lax.fori_loop(0, T // 128, v_block, 0)

    for d in range(2):
        for h in range(MA_HEADS):
            sidx = d * MA_HEADS + h
            CT[sidx] = c0_ref[d, h].T if latent else jnp.zeros((MA_DV, MA_DK), F32)
            ns[sidx] = n0_ref[sidx:sidx + 1, :] if latent else jnp.zeros((1, MA_DK), F32)
            ms[sidx] = m0_ref[sidx:sidx + 1, :] if latent else jnp.zeros((1, 128), F32)

    low = _tri(L, False)
    upp = _tri(L, True)
    rr = lax.broadcasted_iota(jnp.int32, (L, L), 0)
    cc = lax.broadcasted_iota(jnp.int32, (L, L), 1)
    fbc = fbc_ref[...]
    fbr = fbr_ref[...]

    def gate_sums(c, carry):
        t0 = pl.multiple_of(c * L, L)
        gc = gc_ref[pl.ds(t0, L), :]
        lfc = _log_sigmoid(gc + fbc)
        lfr = _log_sigmoid(gt_ref[c] + fbr)
        ish = pltpu.roll(gc, MA_HEADS, 1)
        brs[0, c] = jnp.dot(lfr, upp, precision=HI, preferred_element_type=F32)
        brs[1, c] = jnp.dot(lfr, low, precision=HI, preferred_element_type=F32)
        kcs[0, pl.ds(t0, L), :] = ish - jnp.dot(low, lfc, precision=HI, preferred_element_type=F32)
        kcs[1, pl.ds(t0, L), :] = ish - jnp.dot(upp, lfc, precision=HI, preferred_element_type=F32)
        return carry

    lax.fori_loop(0, NC, gate_sums, 0, unroll=4)

    def chunk_pair(cf, cb):
        H = MA_HEADS
        G = 2 * H
        heads = range(H)
        cs, ts = (cf, cb), (pl.multiple_of(cf * L, L), pl.multiple_of(cb * L, L))
        rows_of = lambda x, r0: [x[r0 + h:r0 + h + 1, :] for h in heads]
        cols_of = lambda x, c0: [x[:, c0 + h:c0 + h + 1] for h in heads]
        br = jnp.stack(sum([rows_of(brs[d, cs[d]], 2 * d * H + H) for d in range(2)], []), 0)
        ir = jnp.stack(sum([rows_of(gt_ref[cs[d]], 2 * d * H) for d in range(2)], []), 0)
        kcol = jnp.stack(sum([cols_of(kcs[d, pl.ds(ts[d], L), :], 2 * d * H + H) for d in range(2)], []), 0)
        split = lambda x: [x[:, h * MA_DK:(h + 1) * MA_DK] for h in heads]
        q = jnp.stack(split(qs[pl.ds(ts[0], L), :]) + split(qs[pl.ds(ts[1], L), :]), 0)
        k = jnp.stack(split(ks[pl.ds(ts[0], L), :]) + split(ks[pl.ds(ts[1], L), :]), 0)
        vt = jnp.concatenate([vT[cf], vT[cb]], 0)
        m = ms[...][:, :, 0:1]
        n = ns[...]
        ct = CT[...]
        bnt = (((2,), (2,)), ((0,), (0,)))
        bnn = (((2,), (1,)), ((0,), (0,)))
        pre = br + kcol
        dmat = jnp.concatenate([jnp.where(rr <= cc, pre[:H], -jnp.inf), jnp.where(rr >= cc, pre[H:], -jnp.inf)], 0)
        g = br + m
        m_t = jnp.maximum(g, jnp.max(dmat, 1, keepdims=True))
        w_inter = jnp.exp(g - m_t)
        s = lax.dot_general(k, q, bnt, preferred_element_type=F32) * jnp.exp(dmat - m_t)
        ctn = jnp.concatenate([ct.astype(BF16), jnp.broadcast_to(n, (G, 8, MA_DK)).astype(BF16)], 1)
        cq = lax.dot_general(ctn, q, bnt, preferred_element_type=F32)
        num = w_inter * cq[:, :MA_DV] + lax.dot_general(vt, s.astype(BF16), bnn, preferred_element_type=F32)
        den = w_inter * cq[:, MA_DV:MA_DV + 1] + jnp.sum(s, 1, keepdims=True)
        hout = num / jnp.maximum(jnp.abs(den), jnp.exp(-m_t))
        hfT[cf] = hout[:H]
        hbT[cb] = hout[H:]
        last = lambda x: jnp.concatenate([x[:H, :, L - 1:L], x[H:, :, 0:1]], 0)
        m_new = last(m_t)
        b_last = last(br)
        decay = jnp.exp(b_last + m - m_new)
        wk = jnp.exp(b_last - br + ir - m_new)
        wk_hi = wk.astype(BF16)
        wk_lo = (wk - wk_hi.astype(F32)).astype(BF16)
        lhs = jnp.concatenate([(vt.astype(F32) * wk).astype(BF16), wk_hi, wk_lo, jnp.zeros((G, 6, L), BF16)], 1)
        upd = lax.dot_general(lhs, k, bnn, preferred_element_type=F32)
        CT[...] = decay * ct + upd[:, :MA_DV]
        ns[...] = decay * n + upd[:, MA_DV:MA_DV + 1] + upd[:, MA_DV + 1:MA_DV + 2]
        ms[...] = jnp.broadcast_to(m_new, (G, 1, 128))

    def body(i, carry):
        chunk_pair(i, NC - 1 - i)
        return carry

    lax.fori_loop(0, NC, body, 0, unroll=2)

    def out_block(tb, carry):
        r0 = pl.multiple_of(tb * 128, 128)
        hsum = jnp.concatenate([hfT[tb * PER + j] + hbT[tb * PER + j] for j in range(PER)], 2)
        outs = [hsum[h].T for h in range(MA_HEADS)]
        gate = jax.nn.sigmoid(p_ref[pl.ds(r0, 128), 3 * W:4 * W].astype(F32))
        a_ref[pl.ds(r0, 128), :] = (gate * jnp.concatenate(outs, 1)).astype(BF16)
        return carry

    lax.fori_loop(0, T // 128, out_block, 0)
    if not latent:
        for d in range(2):
            for h in range(MA_HEADS):
                sidx = d * MA_HEADS + h
                co_ref[d, h] = CT[sidx].T
                no_ref[sidx:sidx + 1, :] = ns[sidx]
                mo_ref[sidx:sidx + 1, :] = ms[sidx]


def _mlstm(proj, gcol, gt3, fbias_l, l, latent, C0=None, n0=None, m0=None, a_out=None, prev=None):
    T = DEC_SEQ if latent else SEQ
    B = DEC_BATCH if latent else BATCH
    rb0 = N_CTX // DEC_SEQ if latent else 0
    fb = fbias_l.astype(F32)
    fbc = jnp.zeros((1, 128), F32).at[0, MA_HEADS:2 * MA_HEADS].set(fb[0]).at[0, 3 * MA_HEADS:4 * MA_HEADS].set(fb[1])
    fbr = fbc[0, :N_GATES].reshape(N_GATES, 1)
    full2 = lambda b: (0, 0)
    any_spec = pl.BlockSpec(memory_space=pl.ANY)
    in_specs = [pl.BlockSpec((T, 4 * BRANCH_W), lambda b: (rb0 + b, 0)),
                pl.BlockSpec((T, 128), lambda b: (rb0 + b, 0)),
                pl.BlockSpec((T // MA_CHUNK, N_GATES, MA_CHUNK), lambda b: (rb0 + b, 0, 0)),
                pl.BlockSpec((1, 128), full2), pl.BlockSpec((N_GATES, 1), full2)]
    args = [proj, gcol, gt3, fbc, fbr]
    a_shape = jax.ShapeDtypeStruct((N_TOK, BRANCH_W), BF16)
    a_spec = pl.BlockSpec((T, BRANCH_W), lambda b: (rb0 + b, 0))
    c_spec = pl.BlockSpec((None, None, 2, MA_HEADS, MA_DK, MA_DV), lambda b: (b, l, 0, 0, 0, 0))
    nm_spec = pl.BlockSpec((None, None, 2 * MA_HEADS, 128), lambda b: (b, l, 0, 0))
    aliases = {}
    if latent:
        cos, sin = _rope_tables(T)
        nb = 2 * MA_HEADS
        in_specs += [pl.BlockSpec((T, MA_DK), full2), pl.BlockSpec((T, MA_DK), full2), c_spec, nm_spec, nm_spec, any_spec]
        args += [cos, sin, C0, n0.reshape(B, DEPTH, nb, MA_DK),
                 jnp.broadcast_to(m0.reshape(B, DEPTH, nb, 1), (B, DEPTH, nb, 128)), a_out]
        aliases = {len(args) - 1: 0}
        out_shape, out_specs = a_shape, a_spec
    else:
        nm_shape = jax.ShapeDtypeStruct((B, DEPTH, 2 * MA_HEADS, 128), F32)
        out_shape = (a_shape, jax.ShapeDtypeStruct((B, DEPTH, 2, MA_HEADS, MA_DK, MA_DV), F32), nm_shape, nm_shape)
        out_specs = (a_spec, c_spec, nm_spec, nm_spec)
        if prev is not None:
            in_specs += [any_spec] * 3
            args += list(prev)
            aliases = {len(args) - 3: 1, len(args) - 2: 2, len(args) - 1: 3}
    nc = T // MA_CHUNK
    scratch = [pltpu.VMEM((T, BRANCH_W), BF16), pltpu.VMEM((T, BRANCH_W), BF16),
               pltpu.VMEM((nc, MA_HEADS, MA_DV, MA_CHUNK), BF16),
               pltpu.VMEM((nc, MA_HEADS, MA_DV, MA_CHUNK), F32), pltpu.VMEM((nc, MA_HEADS, MA_DV, MA_CHUNK), F32),
               pltpu.VMEM((2 * MA_HEADS, MA_DV, MA_DK), F32), pltpu.VMEM((2 * MA_HEADS, 1, MA_DK), F32),
               pltpu.VMEM((2 * MA_HEADS, 1, 128), F32),
               pltpu.VMEM((2, nc, N_GATES, MA_CHUNK), F32), pltpu.VMEM((2, T, 128), F32)]
    return pl.pallas_call(
        functools.partial(_mlstm_kernel, T=T, latent=latent),
        out_shape=out_shape, grid=(B,), in_specs=in_specs, out_specs=out_specs, scratch_shapes=scratch,
        input_output_aliases=aliases,
        compiler_params=_cparams("arbitrary"),
        name="mlstm_lat" if latent else "mlstm_ctx",
    )(*args)


HG_SUB = 8


def _hgrn_kernel(*refs, T, latent):
    ff_ref, fb_ref, q_ref, i_ref, g_ref, lbf_ref, lbb_ref = refs[:7]
    if latent:
        s0_ref = refs[7]
        c_ref, of, ob, ST, iT, As, Bs = refs[-7:]
    else:
        c_ref, so_ref, of, ob, ST, iT, As, Bs = refs[-8:]
    L = HG_CHUNK
    NC = T // L
    NB = L // HG_SUB
    DK = HG_DK

    for d in range(2):
        for h in range(HG_HEADS):
            ST[d * HG_HEADS + h] = s0_ref[d, h].T if latent else jnp.zeros((HG_DV, DK), F32)

    PER = 128 // L

    def i_block(tb, carry):
        r0 = pl.multiple_of(tb * 128, 128)
        for h in range(HG_HEADS):
            blk = i_ref[pl.ds(r0, 128), h * HG_DV:(h + 1) * HG_DV].astype(F32).T.astype(BF16)
            for j in range(PER):
                iT[tb * PER + j, h] = blk[:, j * L:(j + 1) * L]
        return carry

    lax.fori_loop(0, T // 128, i_block, 0)

    low = _tri(L, False)
    upp = _tri(L, True)
    row8 = lax.broadcasted_iota(jnp.int32, (HG_SUB, L), 0)
    lane_s = lax.broadcasted_iota(jnp.int32, (HG_SUB, L), 1)
    heads = range(HG_HEADS)
    bnt = (((2,), (2,)), ((0,), (0,)))
    bnn = (((2,), (1,)), ((0,), (0,)))
    LOG2E = 1.4426950408889634

    def split(x):
        return jnp.stack([x[:, h * DK:(h + 1) * DK] for h in heads], 0)

    def decay_sums(c, carry):
        t0 = pl.multiple_of(c * L, L)
        for d in range(2):
            fpre = (ff_ref if d == 0 else fb_ref)[pl.ds(t0, L), :]
            lb = (lbf_ref if d == 0 else lbb_ref)[...]
            f = lb + (1.0 - lb) * jax.nn.sigmoid(fpre)
            a = jnp.dot(low if d == 0 else upp, jnp.log(f) * LOG2E, precision=HI, preferred_element_type=F32)
            As[d, pl.ds(t0, L), :] = a
            Bs[d, pl.ds(t0, L), :] = a - jnp.log(1.0 - f) * LOG2E
        return carry

    lax.fori_loop(0, NC, decay_sums, 0, unroll=4)

    def chunk_pair(cf, cb):
        tf = pl.multiple_of(cf * L, L)
        tb = pl.multiple_of(cb * L, L)
        both = lambda fn: jnp.concatenate([fn(0, tf), fn(1, tb)], 0)
        A = both(lambda d, t: split(As[d, pl.ds(t, L), :]))
        B = both(lambda d, t: split(Bs[d, pl.ds(t, L), :]))
        q = both(lambda d, t: split(_silu(q_ref[pl.ds(t, L), :].astype(F32))))
        iv = both(lambda d, t: split(i_ref[pl.ds(t, L), :].astype(BF16)))
        ivT = jnp.concatenate([iT[cf], iT[cb]], 0)
        H = HG_HEADS
        st = ST[...]
        o = lax.dot_general((q * jnp.exp2(A)).astype(BF16), st.astype(BF16), bnt, preferred_element_type=F32)
        a_last = jnp.concatenate([A[:H, L - 1:L], A[H:, 0:1]], 0)
        kd = jnp.exp2(a_last - B).astype(BF16)
        rows = []
        for I in range(NB):
            lo, hi = I * HG_SUB, (I + 1) * HG_SUB
            A_I, q_I = A[:, lo:hi], q[:, lo:hi]
            att_f = jnp.zeros((H, HG_SUB, L), F32)
            att_b = jnp.zeros((H, HG_SUB, L), F32)
            for j in range(HG_SUB):
                s = lo + j
                col = jnp.sum(q_I * jnp.exp2(A_I - B[:, s:s + 1]), -1, keepdims=True)
                att_f = jnp.where((lane_s == s) & (row8 >= j), col[:H], att_f)
                att_b = jnp.where((lane_s == s) & (row8 <= j), col[H:], att_b)
            rf, rb = max(lo - 1, 0), min(hi, L - 1)
            R = jnp.concatenate([A[:H, rf:rf + 1], A[H:, rb:rb + 1]], 0)
            zeros = lambda n: jnp.zeros((H, n, DK), BF16)
            ksc_f = jnp.concatenate([jnp.exp2(R[:H] - B[:H, :lo]).astype(BF16), zeros(L - lo)], 1) if I > 0 else zeros(L)
            ksc_b = jnp.concatenate([zeros(hi), jnp.exp2(R[H:] - B[H:, hi:]).astype(BF16)], 1) if I < NB - 1 else zeros(L)
            ksc = jnp.concatenate([ksc_f, ksc_b], 0)
            off = lax.dot_general((q_I * jnp.exp2(A_I - R)).astype(BF16), ksc, bnt, preferred_element_type=F32)
            rows.append(jnp.concatenate([att_f, att_b], 0) + off)
        att = jnp.concatenate(rows, 1)
        o = o + lax.dot_general(att.astype(BF16), iv, bnn, preferred_element_type=F32)
        for h in heads:
            of[pl.ds(tf, L), h * HG_DV:(h + 1) * HG_DV] = o[h]
            ob[pl.ds(tb, L), h * HG_DV:(h + 1) * HG_DV] = o[H + h]
        ST[...] = st * jnp.exp2(a_last) + lax.dot_general(ivT, kd, bnn, preferred_element_type=F32)

    def body(i, carry):
        chunk_pair(i, NC - 1 - i)
        return carry

    lax.fori_loop(0, NC, body, 0, unroll=4)

    def epilogue(r, carry):
        t0 = pl.multiple_of(r * 128, 128)
        o = of[pl.ds(t0, 128), :] + ob[pl.ds(t0, 128), :]
        gsil = _silu(g_ref[pl.ds(t0, 128), :].astype(F32))
        outs = []
        for h in range(HG_HEADS):
            oh = o[:, h * HG_DV:(h + 1) * HG_DV]
            outs.append(oh * lax.rsqrt(jnp.mean(oh * oh, -1, keepdims=True) + RMS_EPS))
        c_ref[pl.ds(t0, 128), :] = (jnp.concatenate(outs, -1) * gsil).astype(BF16)
        return carry

    lax.fori_loop(0, T // 128, epilogue, 0)
    if not latent:
        for d in range(2):
            for h in range(HG_HEADS):
                so_ref[d, h] = ST[d * HG_HEADS + h].T


def _hgrn(p16, p32, lb_l, l, latent, S0=None, c_out=None, prev=None):
    T = DEC_SEQ if latent else SEQ
    B = DEC_BATCH if latent else BATCH
    rb0 = N_CTX // DEC_SEQ if latent else 0
    W = BRANCH_W
    full2 = lambda b: (0, 0)
    any_spec = pl.BlockSpec(memory_space=pl.ANY)
    col = lambda j: pl.BlockSpec((T, W), lambda b: (rb0 + b, j))
    s_spec = pl.BlockSpec((None, None, 2, HG_HEADS, HG_DK, HG_DV), lambda b: (b, l, 0, 0, 0, 0))
    in_specs = [col(0), col(1), col(7), col(8), col(9), pl.BlockSpec((1, W), full2), pl.BlockSpec((1, W), full2)]
    args = [p32, p32, p16, p16, p16, lb_l[0][None, :], lb_l[1][None, :]]
    c_shape = jax.ShapeDtypeStruct((N_TOK, W), BF16)
    c_spec = pl.BlockSpec((T, W), lambda b: (rb0 + b, 0))
    aliases = {}
    if latent:
        in_specs += [s_spec, any_spec]
        args += [S0, c_out]
        aliases = {8: 0}
        out_shape, out_specs = c_shape, c_spec
    else:
        out_shape = (c_shape, jax.ShapeDtypeStruct((B, DEPTH, 2, HG_HEADS, HG_DK, HG_DV), F32))
        out_specs = (c_spec, s_spec)
        if prev is not None:
            in_specs.append(any_spec)
            args.append(prev)
            aliases = {7: 1}
    scratch = [pltpu.VMEM((T, W), F32), pltpu.VMEM((T, W), F32), pltpu.VMEM((2 * HG_HEADS, HG_DV, HG_DK), F32),
               pltpu.VMEM((T // HG_CHUNK, HG_HEADS, HG_DV, HG_CHUNK), BF16),
               pltpu.VMEM((2, T, W), F32), pltpu.VMEM((2, T, W), F32)]
    return pl.pallas_call(
        functools.partial(_hgrn_kernel, T=T, latent=latent),
        out_shape=out_shape, grid=(B,), in_specs=in_specs, out_specs=out_specs, scratch_shapes=scratch,
        input_output_aliases=aliases,
        compiler_params=_cparams("arbitrary"),
        name="hgrn_lat" if latent else "hgrn_ctx",
    )(*args)


def _merge_kernel(a_ref, b_ref, c_ref, ga_ref, gb_ref, gc_ref, xc_ref, xl_ref, g1_ref, sh2_ref, sc2_ref,
                  wb_ref, wo_ref, lng_ref, lnb_ref, wr_ref, x1_ref, h2_ref, *, tm):
    def br(v_ref, g_ref, k):
        return jax.nn.sigmoid(g_ref[...].astype(F32)) * jnp.dot(v_ref[...], wb_ref[k], preferred_element_type=F32)

    mix = br(a_ref, ga_ref, 0) + br(b_ref, gb_ref, 1) + br(c_ref, gc_ref, 2)
    y = jnp.dot(mix.astype(BF16), wo_ref[...], preferred_element_type=F32)
    x = _pair_read(pl.program_id(0), tm, xc_ref, xl_ref)
    x1 = _layer_norm(DEEPNORM_ALPHA * x + g1_ref[...] * y, lng_ref[...], lnb_ref[...])
    x1_ref[...] = x1
    h2 = x1 * (1.0 + sc2_ref[...]) + sh2_ref[...]
    h2_ref[:, :D_MODEL] = h2.astype(BF16)
    lt = lax.dot_general(wr_ref[...], h2, _NT, preferred_element_type=F32, precision=HI)
    r = lax.broadcasted_iota(jnp.int32, lt.shape, 0)
    neg = -jnp.inf
    lg = jnp.where(r < N_GROUPS, lt, neg)
    mg = jnp.max(lg, 0, keepdims=True)
    g_sel = jnp.min(jnp.where(lg == mg, r, ROUTER_ROWS), 0, keepdims=True)
    p_sel = 1.0 / jnp.sum(jnp.where(r < N_GROUPS, jnp.exp(lg - mg), 0.0), 0, keepdims=True)
    lo = ROUTER_E0 + EXP_PER_GROUP * g_sel
    le = jnp.where((r >= lo) & (r < lo + EXP_PER_GROUP), lt, neg)
    v1 = jnp.max(le, 0, keepdims=True)
    i1 = jnp.min(jnp.where(le == v1, r, ROUTER_ROWS), 0, keepdims=True)
    le2 = jnp.where(r == i1, neg, le)
    v2 = jnp.max(le2, 0, keepdims=True)
    i2 = jnp.min(jnp.where(le2 == v2, r, ROUTER_ROWS), 0, keepdims=True)
    e2 = jnp.exp(v2 - v1)
    w1 = p_sel / (1.0 + e2)
    w2 = p_sel * e2 / (1.0 + e2)
    w1_hi = w1.astype(BF16).astype(F32)
    w2_hi = w2.astype(BF16).astype(F32)
    j1, j2 = i1 - lo, i2 - lo
    packed = jnp.where(r == j1, w1_hi, jnp.where(r == j2, w2_hi, jnp.where(
        r == j1 + EXP_PER_GROUP, w1 - w1_hi, jnp.where(r == j2 + EXP_PER_GROUP, w2 - w2_hi, jnp.where(
            r == 2 * EXP_PER_GROUP, g_sel.astype(F32), 0.0)))))
    packed = jnp.concatenate([packed, jnp.zeros((128 - ROUTER_ROWS, packed.shape[1]), F32)], 0)
    h2_ref[:, D_MODEL:] = packed.T.astype(BF16)


def _merge(a, b, c, p16, xc, xl, mod, wb, wo, lng, lnb, wr, l):
    tm = 512
    tok = lambda i: (i, 0)
    ln_spec = pl.BlockSpec((None, None, 1, D_MODEL), lambda i: (l, 0, 0, 0))
    return pl.pallas_call(
        functools.partial(_merge_kernel, tm=tm),
        out_shape=(jax.ShapeDtypeStruct((N_TOK, D_MODEL), F32), jax.ShapeDtypeStruct((N_TOK, MOE_XW), BF16)),
        grid=(N_TOK // tm,),
        in_specs=[pl.BlockSpec((tm, BRANCH_W), tok), pl.BlockSpec((tm, BRANCH_W), tok), pl.BlockSpec((tm, BRANCH_W), tok),
                  pl.BlockSpec((tm, D_MODEL), lambda i: (i, 5)), pl.BlockSpec((tm, D_MODEL), lambda i: (i, 6)),
                  pl.BlockSpec((tm, D_MODEL), lambda i: (i, 7)),
                  *_pair_specs(tm), _mod_spec(l, 2, tm), _mod_spec(l, 3, tm), _mod_spec(l, 4, tm),
                  pl.BlockSpec((None, 3, BRANCH_W, D_MODEL), lambda i: (l, 0, 0, 0)),
                  pl.BlockSpec((None, D_MODEL, D_MODEL), lambda i: (l, 0, 0)),
                  ln_spec, ln_spec,
                  pl.BlockSpec((None, ROUTER_ROWS, D_MODEL), lambda i: (l, 0, 0))],
        out_specs=(pl.BlockSpec((tm, D_MODEL), tok), pl.BlockSpec((tm, MOE_XW), tok)),
        compiler_params=_cparams("arbitrary"),
        name="merge",
    )(a, b, c, p16, p16, p16, xc, xl, mod, mod, mod, wb, wo, lng, lnb, wr)


def _moe_up_kernel(gid_ref, nused_ref, x_ref, w1_ref, w3_ref, hid_ref, w1b, w3b):
    f = pl.program_id(0)
    t = pl.program_id(1)

    @pl.when(t < nused_ref[0])
    def _():
        first = jnp.logical_or(t == 0, gid_ref[t] != gid_ref[jnp.maximum(t - 1, 0)])

        @pl.when(first)
        def _():
            w1b[...] = w1_ref[...].astype(BF16)
            w3b[...] = w3_ref[...].astype(BF16)

        x = x_ref[:, :D_MODEL]
        rec = x_ref[:, D_MODEL:].astype(F32)
        lane = lax.broadcasted_iota(jnp.int32, rec.shape, 1)
        for j in range(MOE_FE):
            e = f * MOE_FE + j
            a = jnp.dot(x, w1b[j], preferred_element_type=F32)
            b = jnp.dot(x, w3b[j], preferred_element_type=F32)
            gcol = jnp.sum(jnp.where(jnp.logical_or(lane == e, lane == e + EXP_PER_GROUP), rec, 0.0), -1, keepdims=True)
            hid_ref[:, j * D_EXPERT:(j + 1) * D_EXPERT] = (_silu(a) * b * gcol).astype(BF16)


def _moe_tile(t, n):
    return jnp.minimum(t, n[0] - 1)


def _moe_up(gid, nused, xs, w1, w3, l):
    tm = MOE_TM
    npad = MOE_NT * tm
    nf = EXP_PER_GROUP // MOE_FE
    w_spec = pl.BlockSpec((None, MOE_FE, D_MODEL, D_EXPERT),
                          lambda f, t, g, n: (l, nf * g[_moe_tile(t, n)] + f, 0, 0))
    grid_spec = pltpu.PrefetchScalarGridSpec(
        num_scalar_prefetch=2,
        grid=(nf, MOE_NT),
        in_specs=[pl.BlockSpec((tm, MOE_XW), lambda f, t, g, n: (_moe_tile(t, n), 0)), w_spec, w_spec],
        out_specs=pl.BlockSpec((tm, MOE_FE * D_EXPERT), lambda f, t, g, n: (_moe_tile(t, n), f)),
        scratch_shapes=[pltpu.VMEM((MOE_FE, D_MODEL, D_EXPERT), BF16), pltpu.VMEM((MOE_FE, D_MODEL, D_EXPERT), BF16)],
    )
    return pl.pallas_call(
        _moe_up_kernel,
        out_shape=jax.ShapeDtypeStruct((npad, EXP_PER_GROUP * D_EXPERT), BF16),
        grid_spec=grid_spec,
        compiler_params=_cparams("arbitrary", "arbitrary"),
        name="moe_up",
    )(gid, nused, xs, w1, w3)


def _moe_down_kernel(gid_ref, nused_ref, hid_ref, w2_ref, y_ref, w2b):
    t = pl.program_id(0)

    @pl.when(t < nused_ref[0])
    def _():
        first = jnp.logical_or(t == 0, gid_ref[t] != gid_ref[jnp.maximum(t - 1, 0)])

        @pl.when(first)
        def _():
            w2b[...] = w2_ref[...].astype(BF16)

        y_ref[...] = jnp.dot(hid_ref[...], w2b[...], preferred_element_type=F32)


def _moe_down(gid, nused, hid, w2g, l):
    tm = MOE_TM
    npad = MOE_NT * tm
    hw = EXP_PER_GROUP * D_EXPERT
    grid_spec = pltpu.PrefetchScalarGridSpec(
        num_scalar_prefetch=2,
        grid=(MOE_NT,),
        in_specs=[pl.BlockSpec((tm, hw), lambda t, g, n: (_moe_tile(t, n), 0)),
                  pl.BlockSpec((None, None, hw, D_MODEL), lambda t, g, n: (l, g[_moe_tile(t, n)], 0, 0))],
        out_specs=pl.BlockSpec((tm, D_MODEL), lambda t, g, n: (_moe_tile(t, n), 0)),
        scratch_shapes=[pltpu.VMEM((hw, D_MODEL), BF16)],
    )
    return pl.pallas_call(
        _moe_down_kernel,
        out_shape=jax.ShapeDtypeStruct((npad, D_MODEL), F32),
        grid_spec=grid_spec,
        compiler_params=_cparams("arbitrary"),
        name="moe_down",
    )(gid, nused, hid, w2g)


def _moe(h2x, w1, w3, w2g, l):
    tm = MOE_TM
    npad = MOE_NT * tm
    g = h2x[:, D_MODEL + 2 * EXP_PER_GROUP].astype(jnp.int32)
    onehot = (g[:, None] == jnp.arange(N_GROUPS)[None, :]).astype(jnp.int32)
    counts = jnp.sum(onehot, 0)
    rank = jnp.sum((jnp.cumsum(onehot, 0) - onehot) * onehot, 1)
    padded = (counts + tm - 1) // tm * tm
    ends = jnp.cumsum(padded)
    offs = ends - padded
    dest = offs[g] + rank
    src = jnp.zeros((npad,), jnp.int32).at[dest].set(jnp.arange(N_TOK, dtype=jnp.int32), unique_indices=True)
    starts = jnp.arange(MOE_NT, dtype=jnp.int32) * tm
    tile_gid = jnp.minimum(jnp.sum((ends[None, :] <= starts[:, None]).astype(jnp.int32), 1), N_GROUPS - 1)
    nused = (ends[-1:] // tm).astype(jnp.int32)
    take = lambda arr, idx: arr.at[idx].get(mode="promise_in_bounds", unique_indices=False)
    hid = _moe_up(tile_gid, nused, take(h2x, src), w1, w3, l)
    ys = _moe_down(tile_gid, nused, hid, w2g, l)
    return take(ys, dest)


def _final_kernel(*refs, tm, with_h):
    x1_ref, y_ref, g2_ref, lng_ref, lnb_ref = refs[:5]
    x2 = _layer_norm(DEEPNORM_ALPHA * x1_ref[...] + g2_ref[...] * y_ref[...], lng_ref[...], lnb_ref[...])
    i = pl.program_id(0)
    if with_h:
        sh_ref, sc_ref, xc_ref, xl_ref, h_ref = refs[5:]
        h_ref[...] = (x2 * (1.0 + sc_ref[...]) + sh_ref[...]).astype(BF16)
    else:
        xc_ref, xl_ref = refs[5:]

    @pl.when(i < N_CTX // tm)
    def _():
        xc_ref[...] = x2

    @pl.when(i >= N_CTX // tm)
    def _():
        xl_ref[...] = x2


def _final(x1, y, mod, lng, lnb, l):
    tm = 1024
    tok = lambda i: (i, 0)
    with_h = l + 1 < DEPTH
    ln_spec = pl.BlockSpec((None, None, 1, D_MODEL), lambda i: (l, 1, 0, 0))
    half = jax.ShapeDtypeStruct((N_CTX, D_MODEL), F32)
    in_specs = [pl.BlockSpec((tm, D_MODEL), tok), pl.BlockSpec((tm, D_MODEL), tok), _mod_spec(l, 5, tm), ln_spec, ln_spec]
    args = [x1, y, mod, lng, lnb]
    out_shape = [half, half]
    out_specs = list(_pair_specs(tm))
    if with_h:
        in_specs += [_mod_spec(l + 1, 0, tm), _mod_spec(l + 1, 1, tm)]
        args += [mod, mod]
        out_shape.append(jax.ShapeDtypeStruct((N_TOK, D_MODEL), BF16))
        out_specs.append(pl.BlockSpec((tm, D_MODEL), tok))
    return pl.pallas_call(
        functools.partial(_final_kernel, tm=tm, with_h=with_h),
        out_shape=tuple(out_shape), grid=(N_TOK // tm,), in_specs=in_specs, out_specs=tuple(out_specs),
        compiler_params=_cparams("arbitrary"),
        name="final",
    )(*args)


def kernel(x_prompt, x_sample, c, cache_na_k, cache_na_v, state_mlstm_C, state_mlstm_n, state_mlstm_m, state_hgrn,
           c_ctx, w_mod, b_mod, w_in, b_in, mlstm_fbias, hgrn_lb_logits, na_rpb, w_branch, w_out, ln_g, ln_b,
           w_rg, w_re, w_e1, w_e3, w_e2):
    assert N_CTX == N_LAT
    lb_cum = jnp.cumsum(jax.nn.softmax(hgrn_lb_logits.astype(F32), axis=1), axis=1)
    lb_all = lb_cum - lb_cum[:, :1]

    cs = jnp.zeros((N_MODROWS, D_MODEL), F32).at[0].set(c_ctx).at[1:1 + DEC_BATCH].set(c)
    mod = _modulation(cs, w_mod, b_mod).reshape(DEPTH, N_MODROWS, 6, 1, D_MODEL)

    wb = w_branch.astype(BF16)
    wo = w_out.astype(BF16)
    lng = ln_g.reshape(DEPTH, 2, 1, D_MODEL)
    lnb = ln_b.reshape(DEPTH, 2, 1, D_MODEL)
    wr = jnp.zeros((DEPTH, ROUTER_ROWS, D_MODEL), F32)
    wr = wr.at[:, :N_GROUPS].set(jnp.swapaxes(w_rg, 1, 2)).at[:, ROUTER_E0:ROUTER_E0 + N_EXPERTS].set(jnp.swapaxes(w_re, 1, 2))
    w2g = w_e2.reshape(DEPTH, N_GROUPS, EXP_PER_GROUP * D_EXPERT, D_MODEL)
    b_main = jnp.concatenate([b_in[:, :GATE_COL0], b_in[:, GATE_COL0 + N_GATES:]], 1)

    xc = x_prompt.reshape(N_CTX, D_MODEL)
    xl = x_sample.reshape(N_LAT, D_MODEL)
    w_t = jnp.swapaxes(w_in, 1, 2)
    na_bias = jax.vmap(_na_bias_table)(na_rpb)
    h = _prep(xc, xl, mod)

    kv = (None, None)
    ma_states = None
    hg_state = None
    for l in range(DEPTH):
        p16, p32 = _inproj(h, w_t, b_main[l][None, :], l)
        gcol, gt = _gates(h, w_t, b_in.reshape(DEPTH, 1, N_IN), l)
        gt3 = gt.reshape(N_GATES, N_TOK // MA_CHUNK, MA_CHUNK).transpose(1, 0, 2)

        a, *ma_states = _mlstm(p16, gcol, gt3, mlstm_fbias[l], l, False, prev=ma_states)
        a = _mlstm(p16, gcol, gt3, mlstm_fbias[l], l, True, state_mlstm_C, state_mlstm_n, state_mlstm_m, a_out=a)
        b, *kv = _ctx_attention(p16, l, *kv)
        b = _lat_attention(p16, cache_na_k, cache_na_v, na_bias, l, b)
        cc, hg_state = _hgrn(p16, p32, lb_all[:, l], l, False, prev=hg_state)
        cc = _hgrn(p16, p32, lb_all[:, l], l, True, state_hgrn, c_out=cc)

        x1, h2x = _merge(a, b, cc, p16, xc, xl, mod, wb, wo, lng, lnb, wr, l)
        y2 = _moe(h2x, w_e1, w_e3, w2g, l)
        outs = _final(x1, y2, mod, lng, lnb, l)
        xc, xl = outs[0], outs[1]
        if l + 1 < DEPTH:
            h = outs[2]

    dt = x_prompt.dtype
    new_C, new_n, new_m = ma_states
    new_n = new_n.reshape(BATCH, DEPTH, 2, MA_HEADS, MA_DK)
    new_m = new_m[:, :, :, 0].reshape(BATCH, DEPTH, 2, MA_HEADS)
    return (xc.reshape(BATCH, SEQ, D_MODEL), xl.reshape(DEC_BATCH, DEC_SEQ, D_MODEL), kv[0], kv[1],
            new_C.astype(dt), new_n.astype(dt), new_m.astype(dt), hg_state.astype(dt))
```

```python
import functools

import numpy as np
import jax
import jax.numpy as jnp
from jax import lax
from jax.experimental import pallas as pl
from jax.experimental.pallas import tpu as pltpu

F32 = jnp.float32
BF16 = jnp.bfloat16
HI = lax.Precision.HIGHEST

D_MODEL = 1024
BATCH = 16
SEQ = 256
DEPTH = 2
DEC_BATCH = 4
DEC_SEQ = 1024
PAST_LEN = 256
GRID_W = 64
MA_HEADS = 4
MA_DK = 128
MA_DV = 128
MA_CHUNK = 64
NA_HEADS = 8
NA_DH = 64
NA_KR_MAX = 8
NA_KC = 16
HG_HEADS = 4
HG_DK = 128
HG_DV = 128
HG_CHUNK = 32
BRANCH_W = 512
N_GROUPS = 4
EXP_PER_GROUP = 4
N_EXPERTS = N_GROUPS * EXP_PER_GROUP
D_EXPERT = 512
ROPE_BASE = 10000.0
LN_EPS = 1e-5
RMS_EPS = 1e-6
DEEPNORM_ALPHA = (2 * DEPTH) ** 0.25

N_CTX = BATCH * SEQ
N_LAT = DEC_BATCH * DEC_SEQ
N_TOK = N_CTX + N_LAT
N_MODROWS = 8
GATE_COL0 = 4 * BRANCH_W
N_GATES = 4 * MA_HEADS
N_IN = 9232
P_COLS = N_IN - N_GATES
MOE_TM = 512
MOE_NT = N_TOK // MOE_TM + N_GROUPS
MOE_FE = 4
MOE_XW = D_MODEL + 128
MOE_XDT = F32
ROUTER_ROWS = 32
ROUTER_E0 = 8
VMEM_LIMIT = 56 * 1024 * 1024

_NT = (((1,), (1,)), ((), ()))
_TN = (((0,), (0,)), ((), ()))


def _cparams(*sem):
    return pltpu.CompilerParams(dimension_semantics=sem, vmem_limit_bytes=VMEM_LIMIT)


def _mod_row(tile, tm):
    return jnp.maximum((tile * tm) // DEC_SEQ - (N_CTX // DEC_SEQ - 1), 0)


def _mod_spec(l, part, tm):
    return pl.BlockSpec((None, None, None, 1, D_MODEL), lambda i: (l, _mod_row(i, tm), part, 0, 0))


def _pair_specs(tm):
    nc = N_CTX // tm
    return (pl.BlockSpec((tm, D_MODEL), lambda i: (jnp.minimum(i, nc - 1), 0)),
            pl.BlockSpec((tm, D_MODEL), lambda i: (jnp.maximum(i - nc, 0), 0)))


def _pair_read(i, tm, c_ref, l_ref):
    return jnp.where(i < N_CTX // tm, c_ref[...], l_ref[...])


def _silu(x):
    return x * jax.nn.sigmoid(x)


def _layer_norm(x, g, b):
    mu = jnp.mean(x, -1, keepdims=True)
    xc = x - mu
    var = jnp.mean(xc * xc, -1, keepdims=True)
    return xc * lax.rsqrt(var + LN_EPS) * g + b


def _log_sigmoid(x):
    return jnp.minimum(x, 0.0) - jnp.log(1.0 + jnp.exp(-jnp.abs(x)))


def _tri(n, upper):
    r = lax.broadcasted_iota(jnp.int32, (n, n), 0)
    c = lax.broadcasted_iota(jnp.int32, (n, n), 1)
    return jnp.where((r <= c) if upper else (r >= c), 1.0, 0.0).astype(F32)


def _mod_kernel(c_ref, w_ref, b_ref, o_ref):
    s = _silu(c_ref[...])
    o_ref[...] = jnp.dot(s.astype(BF16), w_ref[...].astype(BF16), preferred_element_type=F32) + b_ref[...]


def _modulation(cs, w_mod, b_mod):
    tn = 1024
    return pl.pallas_call(
        _mod_kernel,
        out_shape=jax.ShapeDtypeStruct((DEPTH, N_MODROWS, 6 * D_MODEL), F32),
        grid=(DEPTH, 6 * D_MODEL // tn),
        in_specs=[pl.BlockSpec((N_MODROWS, D_MODEL), lambda l, j: (0, 0)),
                  pl.BlockSpec((None, D_MODEL, tn), lambda l, j: (l, 0, j)),
                  pl.BlockSpec((None, 1, tn), lambda l, j: (l, 0, j))],
        out_specs=pl.BlockSpec((None, N_MODROWS, tn), lambda l, j: (l, 0, j)),
        compiler_params=_cparams("arbitrary", "arbitrary"),
        name="modulation",
    )(cs, w_mod, b_mod.reshape(DEPTH, 1, 6 * D_MODEL))


def _prep_kernel(xc_ref, xl_ref, sh_ref, sc_ref, h_ref, *, tm):
    x = _pair_read(pl.program_id(0), tm, xc_ref, xl_ref)
    h_ref[...] = (x * (1.0 + sc_ref[...]) + sh_ref[...]).astype(BF16)


def _prep(xc, xl, mod):
    tm = 1024
    return pl.pallas_call(
        functools.partial(_prep_kernel, tm=tm),
        out_shape=jax.ShapeDtypeStruct((N_TOK, D_MODEL), BF16),
        grid=(N_TOK // tm,),
        in_specs=[*_pair_specs(tm), _mod_spec(0, 0, tm), _mod_spec(0, 1, tm)],
        out_specs=pl.BlockSpec((tm, D_MODEL), lambda i: (i, 0)),
        compiler_params=_cparams("arbitrary"),
        name="prep",
    )(xc, xl, mod, mod)


INPROJ_TN = 512
N_PLAIN_TILES = GATE_COL0 // INPROJ_TN


F32_TILE0 = 7
N_F32_TILES = 2
P16_COLS = P_COLS - N_F32_TILES * INPROJ_TN


def _inproj_kernel(h_ref, wa_ref, wb_ref, b_ref, o_ref, *, src_tile):
    j = src_tile(pl.program_id(1))

    @pl.when(j < N_PLAIN_TILES)
    def _():
        o_ref[...] = (lax.dot_general(h_ref[...], wa_ref[...].astype(BF16), _NT, preferred_element_type=F32)
                      + b_ref[...]).astype(o_ref.dtype)

    @pl.when(j >= N_PLAIN_TILES)
    def _():
        w = jnp.concatenate([wa_ref[N_GATES:, :], wb_ref[...]], 0)
        o_ref[...] = (lax.dot_general(h_ref[...], w.astype(BF16), _NT, preferred_element_type=F32)
                      + b_ref[...]).astype(o_ref.dtype)


def _inproj_call(h, w_t, b_main, l, tm, n_tiles, src_tile, dtype, name):
    tn = INPROJ_TN
    return pl.pallas_call(
        functools.partial(_inproj_kernel, src_tile=src_tile),
        out_shape=jax.ShapeDtypeStruct((N_TOK, n_tiles * tn), dtype),
        grid=(N_TOK // tm, n_tiles),
        in_specs=[pl.BlockSpec((tm, D_MODEL), lambda i, j: (i, 0)),
                  pl.BlockSpec((None, tn, D_MODEL), lambda i, j: (l, src_tile(j), 0)),
                  pl.BlockSpec((None, N_GATES, D_MODEL), lambda i, j: (l, (src_tile(j) + 1) * (tn // N_GATES), 0)),
                  pl.BlockSpec((1, tn), lambda i, j: (0, src_tile(j)))],
        out_specs=pl.BlockSpec((tm, tn), lambda i, j: (i, j)),
        compiler_params=_cparams("arbitrary", "arbitrary"),
        name=name,
    )(h, w_t, w_t, b_main)


def _inproj(h, w_t, b_main, l):
    skip_f32 = lambda j: jnp.where(j < F32_TILE0, j, j + N_F32_TILES)
    p16 = _inproj_call(h, w_t, b_main, l, 4096, P16_COLS // INPROJ_TN, skip_f32, BF16, "inproj")
    p32 = _inproj_call(h, w_t, b_main, l, 4096, N_F32_TILES, lambda j: j + F32_TILE0, F32, "inproj_f32")
    return p16, p32


def _gates_kernel(h_ref, w_ref, b_ref, gc_ref, gt_ref):
    g = lax.dot_general(h_ref[...], w_ref[...].astype(BF16), _NT, preferred_element_type=F32) + b_ref[...]
    gc_ref[...] = g
    gt_ref[...] = g.T[:N_GATES]


def _gates(h, w_t, b_in3, l):
    tm = 1024
    gblk = GATE_COL0 // 128
    return pl.pallas_call(
        _gates_kernel,
        out_shape=(jax.ShapeDtypeStruct((N_TOK, 128), F32), jax.ShapeDtypeStruct((N_GATES, N_TOK), F32)),
        grid=(N_TOK // tm,),
        in_specs=[pl.BlockSpec((tm, D_MODEL), lambda i: (i, 0)),
                  pl.BlockSpec((None, 128, D_MODEL), lambda i: (l, gblk, 0)),
                  pl.BlockSpec((None, 1, 128), lambda i: (l, 0, gblk))],
        out_specs=(pl.BlockSpec((tm, 128), lambda i: (i, 0)), pl.BlockSpec((N_GATES, tm), lambda i: (0, i))),
        compiler_params=_cparams("arbitrary"),
        name="gates",
    )(h, w_t, b_in3)


HEADS_PER_BLK = 128 // NA_DH
NA_NBLK = NA_HEADS // HEADS_PER_BLK
NA_QSCALE = NA_DH ** -0.5
Q_COL, K_COL, V_COL = 16, 20, 24
QKV_COL = 4


def _ctx_attn_kernel(*refs):
    q_ref, k_ref, v_ref = refs[:3]
    o_ref, ko_ref, vo_ref = refs[-3:]
    heads = range(NA_HEADS)
    split = lambda x: jnp.stack([x[:, h * NA_DH:(h + 1) * NA_DH] for h in heads], 0)
    q = split(q_ref[...] * NA_QSCALE)
    k = split(k_ref[...])
    v = split(v_ref[...])
    ko_ref[...] = k.astype(F32)
    vo_ref[...] = v.astype(F32)
    s = lax.dot_general(q, k, (((2,), (2,)), ((0,), (0,))), preferred_element_type=F32)
    e = jnp.exp(s - jnp.max(s, -1, keepdims=True))
    p = e * (1.0 / jnp.sum(e, -1, keepdims=True))
    o = lax.dot_general(p.astype(BF16), v, (((2,), (1,)), ((0,), (0,))), preferred_element_type=F32)
    o_ref[...] = jnp.concatenate([o[h] for h in heads], -1).astype(BF16)


def _ctx_attention(p16, l, prev_k=None, prev_v=None):
    kv_shape = jax.ShapeDtypeStruct((BATCH, DEPTH, NA_HEADS, SEQ, NA_DH), F32)
    kv_spec = pl.BlockSpec((None, None, NA_HEADS, SEQ, NA_DH), lambda b: (b, l, 0, 0, 0))
    col = lambda j: pl.BlockSpec((SEQ, BRANCH_W), lambda b: (b, j))
    in_specs = [col(QKV_COL), col(QKV_COL + 1), col(QKV_COL + 2)]
    args = [p16, p16, p16]
    aliases = {}
    if prev_k is not None:
        in_specs += [pl.BlockSpec(memory_space=pl.ANY)] * 2
        args += [prev_k, prev_v]
        aliases = {3: 1, 4: 2}
    return pl.pallas_call(
        _ctx_attn_kernel,
        out_shape=(jax.ShapeDtypeStruct((N_TOK, BRANCH_W), BF16), kv_shape, kv_shape),
        grid=(BATCH,),
        in_specs=in_specs,
        out_specs=(pl.BlockSpec((SEQ, BRANCH_W), lambda b: (b, 0)), kv_spec, kv_spec),
        input_output_aliases=aliases,
        compiler_params=_cparams("arbitrary"),
        name="ctx_attention",
    )(*args)


NA_ROWS = DEC_SEQ // GRID_W
NA_KR = min(NA_KR_MAX, NA_ROWS)
NA_QROWS = 4
NA_QT = NA_ROWS // NA_QROWS
NA_WROWS = NA_KR + NA_QROWS - 1
NA_WKEYS = NA_WROWS * GRID_W


def _na_window_start(t):
    return min(max(t * NA_QROWS - NA_KR // 2, 0), NA_ROWS - NA_WROWS)


def _na_bias_table(rpb):
    c = np.arange(GRID_W)
    c0 = np.clip(c - NA_KC // 2, 0, GRID_W - NA_KC)
    kc = np.arange(GRID_W)
    valid = (kc[None, :] >= c0[:, None]) & (kc[None, :] < c0[:, None] + NA_KC)
    dc = kc[None, :] - c[:, None] + NA_KC - 1
    onehot = (dc[None] == np.arange(2 * NA_KC - 1)[:, None, None]) & valid[None]
    toep = jnp.einsum('hrd,dcx->hrcx', rpb.astype(F32), jnp.asarray(onehot, F32), precision=HI)
    toep = jnp.where(valid[None, None], toep, -jnp.inf)
    ninf = jnp.full((NA_HEADS, GRID_W, GRID_W), -jnp.inf, F32)
    tiles = []
    for t in range(NA_QT):
        w0 = _na_window_start(t)
        qrows = []
        for r in range(t * NA_QROWS, (t + 1) * NA_QROWS):
            r0 = min(max(r - NA_KR // 2, 0), NA_ROWS - NA_KR)
            blocks = []
            for kr in range(w0, w0 + NA_WROWS):
                inside = r0 <= kr < r0 + NA_KR
                blocks.append(toep[:, kr - r + NA_KR_MAX - 1] if inside else ninf)
            qrows.append(jnp.concatenate(blocks, -1))
        tiles.append(jnp.concatenate(qrows, 1))
    return jnp.stack(tiles, 1)


def _lat_attn_kernel(q_ref, k_ref, v_ref, ck_ref, cv_ref, bias_ref, prev_ref, o_ref):
    heads = range(HEADS_PER_BLK)
    split = lambda x: jnp.stack([x[:, h * NA_DH:(h + 1) * NA_DH] for h in heads], 0)
    bnt = (((2,), (2,)), ((0,), (0,)))
    bnn = (((2,), (1,)), ((0,), (0,)))
    q = (q_ref[...] * NA_QSCALE).astype(BF16)
    k = k_ref[...].astype(BF16)
    v = v_ref[...].astype(BF16)
    ck = ck_ref[...].astype(BF16)
    cv = cv_ref[...].astype(BF16)
    nq = NA_QROWS * GRID_W
    for t in range(NA_QT):
        w0 = _na_window_start(t)
        qs = slice(t * nq, (t + 1) * nq)
        ws = slice(w0 * GRID_W, (w0 + NA_WROWS) * GRID_W)
        qh = split(q[qs])
        s_loc = lax.dot_general(qh, split(k[ws]), bnt, preferred_element_type=F32) + bias_ref[:, t]
        s_ctx = lax.dot_general(qh, ck, bnt, preferred_element_type=F32)
        m = jnp.maximum(jnp.max(s_loc, -1, keepdims=True), jnp.max(s_ctx, -1, keepdims=True))
        e_loc = jnp.exp(s_loc - m)
        e_ctx = jnp.exp(s_ctx - m)
        inv = 1.0 / (jnp.sum(e_loc, -1, keepdims=True) + jnp.sum(e_ctx, -1, keepdims=True))
        acc = (lax.dot_general(e_loc.astype(BF16), split(v[ws]), bnn, preferred_element_type=F32)
               + lax.dot_general(e_ctx.astype(BF16), cv, bnn, preferred_element_type=F32)) * inv
        o_ref[qs, :] = jnp.concatenate([acc[h] for h in heads], -1).astype(BF16)


def _lat_attention(proj, ck, cv, bias, l, b_out):
    rb0 = N_CTX // DEC_SEQ
    cb = lambda base: (lambda j, b: (rb0 + b, base + j))
    c_spec = pl.BlockSpec((None, None, HEADS_PER_BLK, PAST_LEN, NA_DH), lambda j, b: (b, l, j, 0, 0))
    return pl.pallas_call(
        _lat_attn_kernel,
        out_shape=jax.ShapeDtypeStruct((N_TOK, BRANCH_W), BF16),
        grid=(NA_NBLK, DEC_BATCH),
        in_specs=[pl.BlockSpec((DEC_SEQ, 128), cb(Q_COL)), pl.BlockSpec((DEC_SEQ, 128), cb(K_COL)),
                  pl.BlockSpec((DEC_SEQ, 128), cb(V_COL)), c_spec, c_spec,
                  pl.BlockSpec((None, HEADS_PER_BLK, NA_QT, NA_QROWS * GRID_W, NA_WKEYS), lambda j, b: (l, j, 0, 0, 0)),
                  pl.BlockSpec(memory_space=pl.ANY)],
        out_specs=pl.BlockSpec((DEC_SEQ, 128), lambda j, b: (rb0 + b, j)),
        input_output_aliases={6: 0},
        compiler_params=_cparams("arbitrary", "arbitrary"),
        name="lat_attention",
    )(proj, proj, proj, ck, cv, bias, b_out)


MA_KSCALE = MA_DK ** -0.5


def _rope_tables(T):
    t = np.arange(T)
    half = MA_DK // 2
    inv = ROPE_BASE ** (-jnp.arange(0, half, 2, dtype=F32) / half)
    ang_r = jnp.asarray((t // GRID_W).astype(np.float32))[:, None] * inv[None, :]
    ang_c = jnp.asarray((t % GRID_W).astype(np.float32))[:, None] * inv[None, :]
    cos = jnp.concatenate([jnp.cos(ang_r)] * 2 + [jnp.cos(ang_c)] * 2, -1)
    sin = jnp.concatenate([-jnp.sin(ang_r), jnp.sin(ang_r), -jnp.sin(ang_c), jnp.sin(ang_c)], -1)
    return cos, sin


def _mlstm_kernel(*refs, T, latent):
    if latent:
        (p_ref, gc_ref, gt_ref, fbc_ref, fbr_ref, cos_ref, sin_ref, c0_ref, n0_ref, m0_ref, prev_ref,
         a_ref, qs, ks, vT, hfT, hbT, CT, ns, ms, brs, kcs) = refs
    else:
        p_ref, gc_ref, gt_ref, fbc_ref, fbr_ref = refs[:5]
        a_ref, co_ref, no_ref, mo_ref, qs, ks, vT, hfT, hbT, CT, ns, ms, brs, kcs = refs[-14:]
    L = MA_CHUNK
    NC = T // L
    W = BRANCH_W
    PER = 128 // L

    lane = lax.broadcasted_iota(jnp.int32, (T, MA_DK), 1)
    lo_half = (lane % (MA_DK // 2)) < (MA_DK // 4)

    def rope(x):
        if not latent:
            return x
        swapped = jnp.where(lo_half, pltpu.roll(x, MA_DK - MA_DK // 4, 1), pltpu.roll(x, MA_DK // 4, 1))
        return x * cos_ref[...] + swapped * sin_ref[...]

    for h in range(MA_HEADS):
        hs = slice(h * MA_DK, (h + 1) * MA_DK)
        qs[:, hs] = rope(p_ref[:, hs].astype(F32)).astype(BF16)
        ks[:, hs] = rope(p_ref[:, W + h * MA_DK:W + (h + 1) * MA_DK].astype(F32) * MA_KSCALE).astype(BF16)

    def v_block(tb, carry):
        r0 = pl.multiple_of(tb * 128, 128)
        for h in range(MA_HEADS):
            hs = slice(h * MA_DV, (h + 1) * MA_DV)
            blk = p_ref[pl.ds(r0, 128), 2 * W + h * MA_DV:2 * W + (h + 1) * MA_DV].astype(F32).T.astype(BF16)
            for j in range(PER):
                vT[tb * PER + j, h] = blk[:, j * L:(j + 1) * L]
        return carry

    lax.fori_loop(0, T // 128, v_block, 0)

    for d in range(2):
        for h in range(MA_HEADS):
            sidx = d * MA_HEADS + h
            CT[sidx] = c0_ref[d, h].T if latent else jnp.zeros((MA_DV, MA_DK), F32)
            ns[sidx] = n0_ref[sidx:sidx + 1, :] if latent else jnp.zeros((1, MA_DK), F32)
            ms[sidx] = m0_ref[sidx:sidx + 1, :] if latent else jnp.zeros((1, 128), F32)

    low = _tri(L, False)
    upp = _tri(L, True)
    rr = lax.broadcasted_iota(jnp.int32, (L, L), 0)
    cc = lax.broadcasted_iota(jnp.int32, (L, L), 1)
    fbc = fbc_ref[...]
    fbr = fbr_ref[...]

    def gate_sums(c, carry):
        t0 = pl.multiple_of(c * L, L)
        gc = gc_ref[pl.ds(t0, L), :]
        lfc = _log_sigmoid(gc + fbc)
        lfr = _log_sigmoid(gt_ref[c] + fbr)
        ish = pltpu.roll(gc, MA_HEADS, 1)
        brs[0, c] = jnp.dot(lfr, upp, precision=HI, preferred_element_type=F32)
        brs[1, c] = jnp.dot(lfr, low, precision=HI, preferred_element_type=F32)
        kcs[0, pl.ds(t0, L), :] = ish - jnp.dot(low, lfc, precision=HI, preferred_element_type=F32)
        kcs[1, pl.ds(t0, L), :] = ish - jnp.dot(upp, lfc, precision=HI, preferred_element_type=F32)
        return carry

    lax.fori_loop(0, NC, gate_sums, 0, unroll=4)

    def chunk_pair(cf, cb):
        H = MA_HEADS
        G = 2 * H
        heads = range(H)
        cs, ts = (cf, cb), (pl.multiple_of(cf * L, L), pl.multiple_of(cb * L, L))
        rows_of = lambda x, r0: [x[r0 + h:r0 + h + 1, :] for h in heads]
        cols_of = lambda x, c0: [x[:, c0 + h:c0 + h + 1] for h in heads]
        br = jnp.stack(sum([rows_of(brs[d, cs[d]], 2 * d * H + H) for d in range(2)], []), 0)
        ir = jnp.stack(sum([rows_of(gt_ref[cs[d]], 2 * d * H) for d in range(2)], []), 0)
        kcol = jnp.stack(sum([cols_of(kcs[d, pl.ds(ts[d], L), :], 2 * d * H + H) for d in range(2)], []), 0)
        split = lambda x: [x[:, h * MA_DK:(h + 1) * MA_DK] for h in heads]
        q = jnp.stack(split(qs[pl.ds(ts[0], L), :]) + split(qs[pl.ds(ts[1], L), :]), 0)
        k = jnp.stack(split(ks[pl.ds(ts[0], L), :]) + split(ks[pl.ds(ts[1], L), :]), 0)
        vt = jnp.concatenate([vT[cf], vT[cb]], 0)
        m = ms[...][:, :, 0:1]
        n = ns[...]
        ct = CT[...]
        bnt = (((2,), (2,)), ((0,), (0,)))
        bnn = (((2,), (1,)), ((0,), (0,)))
        pre = br + kcol
        dmat = jnp.concatenate([jnp.where(rr <= cc, pre[:H], -jnp.inf), jnp.where(rr >= cc, pre[H:], -jnp.inf)], 0)
        g = br + m
        m_t = jnp.maximum(g, jnp.max(dmat, 1, keepdims=True))
        w_inter = jnp.exp(g - m_t)
        s = lax.dot_general(k, q, bnt, preferred_element_type=F32) * jnp.exp(dmat - m_t)
        ctn = jnp.concatenate([ct.astype(BF16), jnp.broadcast_to(n, (G, 8, MA_DK)).astype(BF16)], 1)
        cq = lax.dot_general(ctn, q, bnt, preferred_element_type=F32)
        num = w_inter * cq[:, :MA_DV] + lax.dot_general(vt, s.astype(BF16), bnn, preferred_element_type=F32)
        den = w_inter * cq[:, MA_DV:MA_DV + 1] + jnp.sum(s, 1, keepdims=True)
        hout = num / jnp.maximum(jnp.abs(den), jnp.exp(-m_t))
        hfT[cf] = hout[:H]
        hbT[cb] = hout[H:]
        last = lambda x: jnp.concatenate([x[:H, :, L - 1:L], x[H:, :, 0:1]], 0)
        m_new = last(m_t)
        b_last = last(br)
        decay = jnp.exp(b_last + m - m_new)
        wk = jnp.exp(b_last - br + ir - m_new)
        wk_hi = wk.astype(BF16)
        wk_lo = (wk - wk_hi.astype(F32)).astype(BF16)
        lhs = jnp.concatenate([(vt.astype(F32) * wk).astype(BF16), wk_hi, wk_lo, jnp.zeros((G, 6, L), BF16)], 1)
        upd = lax.dot_general(lhs, k, bnn, preferred_element_type=F32)
        CT[...] = decay * ct + upd[:, :MA_DV]
        ns[...] = decay * n + upd[:, MA_DV:MA_DV + 1] + upd[:, MA_DV + 1:MA_DV + 2]
        ms[...] = jnp.broadcast_to(m_new, (G, 1, 128))

    def body(i, carry):
        chunk_pair(i, NC - 1 - i)
        return carry

    lax.fori_loop(0, NC, body, 0, unroll=2)

    def out_block(tb, carry):
        r0 = pl.multiple_of(tb * 128, 128)
        hsum = jnp.concatenate([hfT[tb * PER + j] + hbT[tb * PER + j] for j in range(PER)], 2)
        outs = [hsum[h].T for h in range(MA_HEADS)]
        gate = jax.nn.sigmoid(p_ref[pl.ds(r0, 128), 3 * W:4 * W].astype(F32))
        a_ref[pl.ds(r0, 128), :] = (gate * jnp.concatenate(outs, 1)).astype(BF16)
        return carry

    lax.fori_loop(0, T // 128, out_block, 0)
    if not latent:
        for d in range(2):
            for h in range(MA_HEADS):
                sidx = d * MA_HEADS + h
                co_ref[d, h] = CT[sidx].T
                no_ref[sidx:sidx + 1, :] = ns[sidx]
                mo_ref[sidx:sidx + 1, :] = ms[sidx]


def _mlstm(proj, gcol, gt3, fbias_l, l, latent, C0=None, n0=None, m0=None, a_out=None, prev=None):
    T = DEC_SEQ if latent else SEQ
    B = DEC_BATCH if latent else BATCH
    rb0 = N_CTX // DEC_SEQ if latent else 0
    fb = fbias_l.astype(F32)
    fbc = jnp.zeros((1, 128), F32).at[0, MA_HEADS:2 * MA_HEADS].set(fb[0]).at[0, 3 * MA_HEADS:4 * MA_HEADS].set(fb[1])
    fbr = fbc[0, :N_GATES].reshape(N_GATES, 1)
    full2 = lambda b: (0, 0)
    any_spec = pl.BlockSpec(memory_space=pl.ANY)
    in_specs = [pl.BlockSpec((T, 4 * BRANCH_W), lambda b: (rb0 + b, 0)),
                pl.BlockSpec((T, 128), lambda b: (rb0 + b, 0)),
                pl.BlockSpec((T // MA_CHUNK, N_GATES, MA_CHUNK), lambda b: (rb0 + b, 0, 0)),
                pl.BlockSpec((1, 128), full2), pl.BlockSpec((N_GATES, 1), full2)]
    args = [proj, gcol, gt3, fbc, fbr]
    a_shape = jax.ShapeDtypeStruct((N_TOK, BRANCH_W), BF16)
    a_spec = pl.BlockSpec((T, BRANCH_W), lambda b: (rb0 + b, 0))
    c_spec = pl.BlockSpec((None, None, 2, MA_HEADS, MA_DK, MA_DV), lambda b: (b, l, 0, 0, 0, 0))
    nm_spec = pl.BlockSpec((None, None, 2 * MA_HEADS, 128), lambda b: (b, l, 0, 0))
    aliases = {}
    if latent:
        cos, sin = _rope_tables(T)
        nb = 2 * MA_HEADS
        in_specs += [pl.BlockSpec((T, MA_DK), full2), pl.BlockSpec((T, MA_DK), full2), c_spec, nm_spec, nm_spec, any_spec]
        args += [cos, sin, C0, n0.reshape(B, DEPTH, nb, MA_DK),
                 jnp.broadcast_to(m0.reshape(B, DEPTH, nb, 1), (B, DEPTH, nb, 128)), a_out]
        aliases = {len(args) - 1: 0}
        out_shape, out_specs = a_shape, a_spec
    else:
        nm_shape = jax.ShapeDtypeStruct((B, DEPTH, 2 * MA_HEADS, 128), F32)
        out_shape = (a_shape, jax.ShapeDtypeStruct((B, DEPTH, 2, MA_HEADS, MA_DK, MA_DV), F32), nm_shape, nm_shape)
        out_specs = (a_spec, c_spec, nm_spec, nm_spec)
        if prev is not None:
            in_specs += [any_spec] * 3
            args += list(prev)
            aliases = {len(args) - 3: 1, len(args) - 2: 2, len(args) - 1: 3}
    nc = T // MA_CHUNK
    scratch = [pltpu.VMEM((T, BRANCH_W), BF16), pltpu.VMEM((T, BRANCH_W), BF16),
               pltpu.VMEM((nc, MA_HEADS, MA_DV, MA_CHUNK), BF16),
               pltpu.VMEM((nc, MA_HEADS, MA_DV, MA_CHUNK), F32), pltpu.VMEM((nc, MA_HEADS, MA_DV, MA_CHUNK), F32),
               pltpu.VMEM((2 * MA_HEADS, MA_DV, MA_DK), F32), pltpu.VMEM((2 * MA_HEADS, 1, MA_DK), F32),
               pltpu.VMEM((2 * MA_HEADS, 1, 128), F32),
               pltpu.VMEM((2, nc, N_GATES, MA_CHUNK), F32), pltpu.VMEM((2, T, 128), F32)]
    return pl.pallas_call(
        functools.partial(_mlstm_kernel, T=T, latent=latent),
        out_shape=out_shape, grid=(B,), in_specs=in_specs, out_specs=out_specs, scratch_shapes=scratch,
        input_output_aliases=aliases,
        compiler_params=_cparams("arbitrary"),
        name="mlstm_lat" if latent else "mlstm_ctx",
    )(*args)


HG_SUB = 8


def _hgrn_kernel(*refs, T, latent):
    ff_ref, fb_ref, q_ref, i_ref, g_ref, lbf_ref, lbb_ref = refs[:7]
    if latent:
        s0_ref = refs[7]
        c_ref, of, ob, ST, iT, As, Bs = refs[-7:]
    else:
        c_ref, so_ref, of, ob, ST, iT, As, Bs = refs[-8:]
    L = HG_CHUNK
    NC = T // L
    NB = L // HG_SUB
    DK = HG_DK

    for d in range(2):
        for h in range(HG_HEADS):
            ST[d * HG_HEADS + h] = s0_ref[d, h].T if latent else jnp.zeros((HG_DV, DK), F32)

    PER = 128 // L

    def i_block(tb, carry):
        r0 = pl.multiple_of(tb * 128, 128)
        for h in range(HG_HEADS):
            blk = i_ref[pl.ds(r0, 128), h * HG_DV:(h + 1) * HG_DV].astype(F32).T.astype(BF16)
            for j in range(PER):
                iT[tb * PER + j, h] = blk[:, j * L:(j + 1) * L]
        return carry

    lax.fori_loop(0, T // 128, i_block, 0)

    low = _tri(L, False)
    upp = _tri(L, True)
    row8 = lax.broadcasted_iota(jnp.int32, (HG_SUB, L), 0)
    lane_s = lax.broadcasted_iota(jnp.int32, (HG_SUB, L), 1)
    heads = range(HG_HEADS)
    bnt = (((2,), (2,)), ((0,), (0,)))
    bnn = (((2,), (1,)), ((0,), (0,)))
    LOG2E = 1.4426950408889634

    def split(x):
        return jnp.stack([x[:, h * DK:(h + 1) * DK] for h in heads], 0)

    def decay_sums(c, carry):
        t0 = pl.multiple_of(c * L, L)
        for d in range(2):
            fpre = (ff_ref if d == 0 else fb_ref)[pl.ds(t0, L), :]
            lb = (lbf_ref if d == 0 else lbb_ref)[...]
            f = lb + (1.0 - lb) * jax.nn.sigmoid(fpre)
            a = jnp.dot(low if d == 0 else upp, jnp.log(f) * LOG2E, precision=HI, preferred_element_type=F32)
            As[d, pl.ds(t0, L), :] = a
            Bs[d, pl.ds(t0, L), :] = a - jnp.log(1.0 - f) * LOG2E
        return carry

    lax.fori_loop(0, NC, decay_sums, 0, unroll=4)

    def chunk_pair(cf, cb):
        tf = pl.multiple_of(cf * L, L)
        tb = pl.multiple_of(cb * L, L)
        both = lambda fn: jnp.concatenate([fn(0, tf), fn(1, tb)], 0)
        A = both(lambda d, t: split(As[d, pl.ds(t, L), :]))
        B = both(lambda d, t: split(Bs[d, pl.ds(t, L), :]))
        q = both(lambda d, t: split(_silu(q_ref[pl.ds(t, L), :].astype(F32))))
        iv = both(lambda d, t: split(i_ref[pl.ds(t, L), :].astype(BF16)))
        ivT = jnp.concatenate([iT[cf], iT[cb]], 0)
        H = HG_HEADS
        st = ST[...]
        o = lax.dot_general((q * jnp.exp2(A)).astype(BF16), st.astype(BF16), bnt, preferred_element_type=F32)
        a_last = jnp.concatenate([A[:H, L - 1:L], A[H:, 0:1]], 0)
        kd = jnp.exp2(a_last - B).astype(BF16)
        rows = []
        for I in range(NB):
            lo, hi = I * HG_SUB, (I + 1) * HG_SUB
            A_I, q_I = A[:, lo:hi], q[:, lo:hi]
            att_f = jnp.zeros((H, HG_SUB, L), F32)
            att_b = jnp.zeros((H, HG_SUB, L), F32)
            for j in range(HG_SUB):
                s = lo + j
                col = jnp.sum(q_I * jnp.exp2(A_I - B[:, s:s + 1]), -1, keepdims=True)
                att_f = jnp.where((lane_s == s) & (row8 >= j), col[:H], att_f)
                att_b = jnp.where((lane_s == s) & (row8 <= j), col[H:], att_b)
            rf, rb = max(lo - 1, 0), min(hi, L - 1)
            R = jnp.concatenate([A[:H, rf:rf + 1], A[H:, rb:rb + 1]], 0)
            zeros = lambda n: jnp.zeros((H, n, DK), BF16)
            ksc_f = jnp.concatenate([jnp.exp2(R[:H] - B[:H, :lo]).astype(BF16), zeros(L - lo)], 1) if I > 0 else zeros(L)
            ksc_b = jnp.concatenate([zeros(hi), jnp.exp2(R[H:] - B[H:, hi:]).astype(BF16)], 1) if I < NB - 1 else zeros(L)
            ksc = jnp.concatenate([ksc_f, ksc_b], 0)
            off = lax.dot_general((q_I * jnp.exp2(A_I - R)).astype(BF16), ksc, bnt, preferred_element_type=F32)
            rows.append(jnp.concatenate([att_f, att_b], 0) + off)
        att = jnp.concatenate(rows, 1)
        o = o + lax.dot_general(att.astype(BF16), iv, bnn, preferred_element_type=F32)
        for h in heads:
            of[pl.ds(tf, L), h * HG_DV:(h + 1) * HG_DV] = o[h]
            ob[pl.ds(tb, L), h * HG_DV:(h + 1) * HG_DV] = o[H + h]
        ST[...] = st * jnp.exp2(a_last) + lax.dot_general(ivT, kd, bnn, preferred_element_type=F32)

    def body(i, carry):
        chunk_pair(i, NC - 1 - i)
        return carry

    lax.fori_loop(0, NC, body, 0, unroll=4)

    def epilogue(r, carry):
        t0 = pl.multiple_of(r * 128, 128)
        o = of[pl.ds(t0, 128), :] + ob[pl.ds(t0, 128), :]
        gsil = _silu(g_ref[pl.ds(t0, 128), :].astype(F32))
        outs = []
        for h in range(HG_HEADS):
            oh = o[:, h * HG_DV:(h + 1) * HG_DV]
            outs.append(oh * lax.rsqrt(jnp.mean(oh * oh, -1, keepdims=True) + RMS_EPS))
        c_ref[pl.ds(t0, 128), :] = (jnp.concatenate(outs, -1) * gsil).astype(BF16)
        return carry

    lax.fori_loop(0, T // 128, epilogue, 0)
    if not latent:
        for d in range(2):
            for h in range(HG_HEADS):
                so_ref[d, h] = ST[d * HG_HEADS + h].T


def _hgrn(p16, p32, lb_l, l, latent, S0=None, c_out=None, prev=None):
    T = DEC_SEQ if latent else SEQ
    B = DEC_BATCH if latent else BATCH
    rb0 = N_CTX // DEC_SEQ if latent else 0
    W = BRANCH_W
    full2 = lambda b: (0, 0)
    any_spec = pl.BlockSpec(memory_space=pl.ANY)
    col = lambda j: pl.BlockSpec((T, W), lambda b: (rb0 + b, j))
    s_spec = pl.BlockSpec((None, None, 2, HG_HEADS, HG_DK, HG_DV), lambda b: (b, l, 0, 0, 0, 0))
    in_specs = [col(0), col(1), col(7), col(8), col(9), pl.BlockSpec((1, W), full2), pl.BlockSpec((1, W), full2)]
    args = [p32, p32, p16, p16, p16, lb_l[0][None, :], lb_l[1][None, :]]
    c_shape = jax.ShapeDtypeStruct((N_TOK, W), BF16)
    c_spec = pl.BlockSpec((T, W), lambda b: (rb0 + b, 0))
    aliases = {}
    if latent:
        in_specs += [s_spec, any_spec]
        args += [S0, c_out]
        aliases = {8: 0}
        out_shape, out_specs = c_shape, c_spec
    else:
        out_shape = (c_shape, jax.ShapeDtypeStruct((B, DEPTH, 2, HG_HEADS, HG_DK, HG_DV), F32))
        out_specs = (c_spec, s_spec)
        if prev is not None:
            in_specs.append(any_spec)
            args.append(prev)
            aliases = {7: 1}
    scratch = [pltpu.VMEM((T, W), F32), pltpu.VMEM((T, W), F32), pltpu.VMEM((2 * HG_HEADS, HG_DV, HG_DK), F32),
               pltpu.VMEM((T // HG_CHUNK, HG_HEADS, HG_DV, HG_CHUNK), BF16),
               pltpu.VMEM((2, T, W), F32), pltpu.VMEM((2, T, W), F32)]
    return pl.pallas_call(
        functools.partial(_hgrn_kernel, T=T, latent=latent),
        out_shape=out_shape, grid=(B,), in_specs=in_specs, out_specs=out_specs, scratch_shapes=scratch,
        input_output_aliases=aliases,
        compiler_params=_cparams("arbitrary"),
        name="hgrn_lat" if latent else "hgrn_ctx",
    )(*args)


def _merge_kernel(a_ref, b_ref, c_ref, ga_ref, gb_ref, gc_ref, xc_ref, xl_ref, g1_ref, sh2_ref, sc2_ref,
                  wb_ref, wo_ref, lng_ref, lnb_ref, wr_ref, x1_ref, h2_ref, *, tm):
    def br(v_ref, g_ref, k):
        return jax.nn.sigmoid(g_ref[...].astype(F32)) * jnp.dot(v_ref[...], wb_ref[k], preferred_element_type=F32)

    mix = br(a_ref, ga_ref, 0) + br(b_ref, gb_ref, 1) + br(c_ref, gc_ref, 2)
    y = jnp.dot(mix.astype(BF16), wo_ref[...], preferred_element_type=F32)
    x = _pair_read(pl.program_id(0), tm, xc_ref, xl_ref)
    x1 = _layer_norm(DEEPNORM_ALPHA * x + g1_ref[...] * y, lng_ref[...], lnb_ref[...])
    x1_ref[...] = x1
    h2 = x1 * (1.0 + sc2_ref[...]) + sh2_ref[...]
    h2_ref[:, :D_MODEL] = h2.astype(h2_ref.dtype)
    lt = lax.dot_general(wr_ref[...], h2, _NT, preferred_element_type=F32, precision=HI)
    r = lax.broadcasted_iota(jnp.int32, lt.shape, 0)
    neg = -jnp.inf
    lg = jnp.where(r < N_GROUPS, lt, neg)
    mg = jnp.max(lg, 0, keepdims=True)
    g_sel = jnp.min(jnp.where(lg == mg, r, ROUTER_ROWS), 0, keepdims=True)
    p_sel = 1.0 / jnp.sum(jnp.where(r < N_GROUPS, jnp.exp(lg - mg), 0.0), 0, keepdims=True)
    lo = ROUTER_E0 + EXP_PER_GROUP * g_sel
    le = jnp.where((r >= lo) & (r < lo + EXP_PER_GROUP), lt, neg)
    v1 = jnp.max(le, 0, keepdims=True)
    i1 = jnp.min(jnp.where(le == v1, r, ROUTER_ROWS), 0, keepdims=True)
    le2 = jnp.where(r == i1, neg, le)
    v2 = jnp.max(le2, 0, keepdims=True)
    i2 = jnp.min(jnp.where(le2 == v2, r, ROUTER_ROWS), 0, keepdims=True)
    e2 = jnp.exp(v2 - v1)
    w1 = p_sel / (1.0 + e2)
    w2 = p_sel * e2 / (1.0 + e2)
    w1_hi = w1.astype(BF16).astype(F32)
    w2_hi = w2.astype(BF16).astype(F32)
    j1, j2 = i1 - lo, i2 - lo
    packed = jnp.where(r == j1, w1_hi, jnp.where(r == j2, w2_hi, jnp.where(
        r == j1 + EXP_PER_GROUP, w1 - w1_hi, jnp.where(r == j2 + EXP_PER_GROUP, w2 - w2_hi, jnp.where(
            r == 2 * EXP_PER_GROUP, g_sel.astype(F32), 0.0)))))
    packed = jnp.concatenate([packed, jnp.zeros((128 - ROUTER_ROWS, packed.shape[1]), F32)], 0)
    h2_ref[:, D_MODEL:] = packed.T.astype(h2_ref.dtype)


def _merge(a, b, c, p16, xc, xl, mod, wb, wo, lng, lnb, wr, l):
    tm = 512
    tok = lambda i: (i, 0)
    ln_spec = pl.BlockSpec((None, None, 1, D_MODEL), lambda i: (l, 0, 0, 0))
    return pl.pallas_call(
        functools.partial(_merge_kernel, tm=tm),
        out_shape=(jax.ShapeDtypeStruct((N_TOK, D_MODEL), F32), jax.ShapeDtypeStruct((N_TOK, MOE_XW), MOE_XDT)),
        grid=(N_TOK // tm,),
        in_specs=[pl.BlockSpec((tm, BRANCH_W), tok), pl.BlockSpec((tm, BRANCH_W), tok), pl.BlockSpec((tm, BRANCH_W), tok),
                  pl.BlockSpec((tm, D_MODEL), lambda i: (i, 5)), pl.BlockSpec((tm, D_MODEL), lambda i: (i, 6)),
                  pl.BlockSpec((tm, D_MODEL), lambda i: (i, 7)),
                  *_pair_specs(tm), _mod_spec(l, 2, tm), _mod_spec(l, 3, tm), _mod_spec(l, 4, tm),
                  pl.BlockSpec((None, 3, BRANCH_W, D_MODEL), lambda i: (l, 0, 0, 0)),
                  pl.BlockSpec((None, D_MODEL, D_MODEL), lambda i: (l, 0, 0)),
                  ln_spec, ln_spec,
                  pl.BlockSpec((None, ROUTER_ROWS, D_MODEL), lambda i: (l, 0, 0))],
        out_specs=(pl.BlockSpec((tm, D_MODEL), tok), pl.BlockSpec((tm, MOE_XW), tok)),
        compiler_params=_cparams("arbitrary"),
        name="merge",
    )(a, b, c, p16, p16, p16, xc, xl, mod, mod, mod, wb, wo, lng, lnb, wr)


def _moe_up_kernel(gid_ref, nused_ref, x_ref, w1_ref, w3_ref, hid_ref, w1b, w3b):
    f = pl.program_id(0)
    t = pl.program_id(1)

    @pl.when(t < nused_ref[0])
    def _():
        first = jnp.logical_or(t == 0, gid_ref[t] != gid_ref[jnp.maximum(t - 1, 0)])

        @pl.when(first)
        def _():
            w1b[...] = w1_ref[...].astype(BF16)
            w3b[...] = w3_ref[...].astype(BF16)

        x = x_ref[:, :D_MODEL].astype(BF16)
        rec = x_ref[:, D_MODEL:].astype(F32)
        lane = lax.broadcasted_iota(jnp.int32, rec.shape, 1)
        for j in range(MOE_FE):
            e = f * MOE_FE + j
            a = jnp.dot(x, w1b[j], preferred_element_type=F32)
            b = jnp.dot(x, w3b[j], preferred_element_type=F32)
            gcol = jnp.sum(jnp.where(jnp.logical_or(lane == e, lane == e + EXP_PER_GROUP), rec, 0.0), -1, keepdims=True)
            hid_ref[:, j * D_EXPERT:(j + 1) * D_EXPERT] = (_silu(a) * b * gcol).astype(BF16)


def _moe_tile(t, n):
    return jnp.minimum(t, n[0] - 1)


def _moe_up(gid, nused, xs, w1, w3, l):
    tm = MOE_TM
    npad = MOE_NT * tm
    nf = EXP_PER_GROUP // MOE_FE
    w_spec = pl.BlockSpec((None, MOE_FE, D_MODEL, D_EXPERT),
                          lambda f, t, g, n: (l, nf * g[_moe_tile(t, n)] + f, 0, 0))
    grid_spec = pltpu.PrefetchScalarGridSpec(
        num_scalar_prefetch=2,
        grid=(nf, MOE_NT),
        in_specs=[pl.BlockSpec((tm, MOE_XW), lambda f, t, g, n: (_moe_tile(t, n), 0)), w_spec, w_spec],
        out_specs=pl.BlockSpec((tm, MOE_FE * D_EXPERT), lambda f, t, g, n: (_moe_tile(t, n), f)),
        scratch_shapes=[pltpu.VMEM((MOE_FE, D_MODEL, D_EXPERT), BF16), pltpu.VMEM((MOE_FE, D_MODEL, D_EXPERT), BF16)],
    )
    return pl.pallas_call(
        _moe_up_kernel,
        out_shape=jax.ShapeDtypeStruct((npad, EXP_PER_GROUP * D_EXPERT), BF16),
        grid_spec=grid_spec,
        compiler_params=_cparams("arbitrary", "arbitrary"),
        name="moe_up",
    )(gid, nused, xs, w1, w3)


def _moe_down_kernel(gid_ref, nused_ref, hid_ref, w2_ref, y_ref, w2b):
    t = pl.program_id(0)

    @pl.when(t < nused_ref[0])
    def _():
        first = jnp.logical_or(t == 0, gid_ref[t] != gid_ref[jnp.maximum(t - 1, 0)])

        @pl.when(first)
        def _():
            w2b[...] = w2_ref[...].astype(BF16)

        y_ref[...] = jnp.dot(hid_ref[...], w2b[...], preferred_element_type=F32)


def _moe_down(gid, nused, hid, w2g, l):
    tm = MOE_TM
    npad = MOE_NT * tm
    hw = EXP_PER_GROUP * D_EXPERT
    grid_spec = pltpu.PrefetchScalarGridSpec(
        num_scalar_prefetch=2,
        grid=(MOE_NT,),
        in_specs=[pl.BlockSpec((tm, hw), lambda t, g, n: (_moe_tile(t, n), 0)),
                  pl.BlockSpec((None, None, hw, D_MODEL), lambda t, g, n: (l, g[_moe_tile(t, n)], 0, 0))],
        out_specs=pl.BlockSpec((tm, D_MODEL), lambda t, g, n: (_moe_tile(t, n), 0)),
        scratch_shapes=[pltpu.VMEM((hw, D_MODEL), BF16)],
    )
    return pl.pallas_call(
        _moe_down_kernel,
        out_shape=jax.ShapeDtypeStruct((npad, D_MODEL), F32),
        grid_spec=grid_spec,
        compiler_params=_cparams("arbitrary"),
        name="moe_down",
    )(gid, nused, hid, w2g)


def _moe(h2x, w1, w3, w2g, l):
    tm = MOE_TM
    npad = MOE_NT * tm
    g = h2x[:, D_MODEL + 2 * EXP_PER_GROUP].astype(jnp.int32)
    onehot = (g[:, None] == jnp.arange(N_GROUPS)[None, :]).astype(jnp.int32)
    counts = jnp.sum(onehot, 0)
    rank = jnp.sum((jnp.cumsum(onehot, 0) - onehot) * onehot, 1)
    padded = (counts + tm - 1) // tm * tm
    ends = jnp.cumsum(padded)
    offs = ends - padded
    dest = offs[g] + rank
    src = jnp.zeros((npad,), jnp.int32).at[dest].set(jnp.arange(N_TOK, dtype=jnp.int32), unique_indices=True)
    starts = jnp.arange(MOE_NT, dtype=jnp.int32) * tm
    tile_gid = jnp.minimum(jnp.sum((ends[None, :] <= starts[:, None]).astype(jnp.int32), 1), N_GROUPS - 1)
    nused = (ends[-1:] // tm).astype(jnp.int32)
    take = lambda arr, idx: arr.at[idx].get(mode="promise_in_bounds", unique_indices=False)
    hid = _moe_up(tile_gid, nused, take(h2x, src), w1, w3, l)
    ys = _moe_down(tile_gid, nused, hid, w2g, l)
    return take(ys, dest)


def _final_kernel(*refs, tm, with_h):
    x1_ref, y_ref, g2_ref, lng_ref, lnb_ref = refs[:5]
    x2 = _layer_norm(DEEPNORM_ALPHA * x1_ref[...] + g2_ref[...] * y_ref[...], lng_ref[...], lnb_ref[...])
    i = pl.program_id(0)
    if with_h:
        sh_ref, sc_ref, xc_ref, xl_ref, h_ref = refs[5:]
        h_ref[...] = (x2 * (1.0 + sc_ref[...]) + sh_ref[...]).astype(BF16)
    else:
        xc_ref, xl_ref = refs[5:]

    @pl.when(i < N_CTX // tm)
    def _():
        xc_ref[...] = x2

    @pl.when(i >= N_CTX // tm)
    def _():
        xl_ref[...] = x2


def _final(x1, y, mod, lng, lnb, l):
    tm = 1024
    tok = lambda i: (i, 0)
    with_h = l + 1 < DEPTH
    ln_spec = pl.BlockSpec((None, None, 1, D_MODEL), lambda i: (l, 1, 0, 0))
    half = jax.ShapeDtypeStruct((N_CTX, D_MODEL), F32)
    in_specs = [pl.BlockSpec((tm, D_MODEL), tok), pl.BlockSpec((tm, D_MODEL), tok), _mod_spec(l, 5, tm), ln_spec, ln_spec]
    args = [x1, y, mod, lng, lnb]
    out_shape = [half, half]
    out_specs = list(_pair_specs(tm))
    if with_h:
        in_specs += [_mod_spec(l + 1, 0, tm), _mod_spec(l + 1, 1, tm)]
        args += [mod, mod]
        out_shape.append(jax.ShapeDtypeStruct((N_TOK, D_MODEL), BF16))
        out_specs.append(pl.BlockSpec((tm, D_MODEL), tok))
    return pl.pallas_call(
        functools.partial(_final_kernel, tm=tm, with_h=with_h),
        out_shape=tuple(out_shape), grid=(N_TOK // tm,), in_specs=in_specs, out_specs=tuple(out_specs),
        compiler_params=_cparams("arbitrary"),
        name="final",
    )(*args)


def kernel(x_prompt, x_sample, c, cache_na_k, cache_na_v, state_mlstm_C, state_mlstm_n, state_mlstm_m, state_hgrn,
           c_ctx, w_mod, b_mod, w_in, b_in, mlstm_fbias, hgrn_lb_logits, na_rpb, w_branch, w_out, ln_g, ln_b,
           w_rg, w_re, w_e1, w_e3, w_e2):
    assert N_CTX == N_LAT
    lb_cum = jnp.cumsum(jax.nn.softmax(hgrn_lb_logits.astype(F32), axis=1), axis=1)
    lb_all = lb_cum - lb_cum[:, :1]

    cs = jnp.zeros((N_MODROWS, D_MODEL), F32).at[0].set(c_ctx).at[1:1 + DEC_BATCH].set(c)
    mod = _modulation(cs, w_mod, b_mod).reshape(DEPTH, N_MODROWS, 6, 1, D_MODEL)

    wb = w_branch.astype(BF16)
    wo = w_out.astype(BF16)
    lng = ln_g.reshape(DEPTH, 2, 1, D_MODEL)
    lnb = ln_b.reshape(DEPTH, 2, 1, D_MODEL)
    wr = jnp.zeros((DEPTH, ROUTER_ROWS, D_MODEL), F32)
    wr = wr.at[:, :N_GROUPS].set(jnp.swapaxes(w_rg, 1, 2)).at[:, ROUTER_E0:ROUTER_E0 + N_EXPERTS].set(jnp.swapaxes(w_re, 1, 2))
    w2g = w_e2.reshape(DEPTH, N_GROUPS, EXP_PER_GROUP * D_EXPERT, D_MODEL)
    b_main = jnp.concatenate([b_in[:, :GATE_COL0], b_in[:, GATE_COL0 + N_GATES:]], 1)

    xc = x_prompt.reshape(N_CTX, D_MODEL)
    xl = x_sample.reshape(N_LAT, D_MODEL)
    w_t = jnp.swapaxes(w_in, 1, 2)
    na_bias = jax.vmap(_na_bias_table)(na_rpb)
    h = _prep(xc, xl, mod)

    kv = (None, None)
    ma_states = None
    hg_state = None
    for l in range(DEPTH):
        p16, p32 = _inproj(h, w_t, b_main[l][None, :], l)
        gcol, gt = _gates(h, w_t, b_in.reshape(DEPTH, 1, N_IN), l)
        gt3 = gt.reshape(N_GATES, N_TOK // MA_CHUNK, MA_CHUNK).transpose(1, 0, 2)

        a, *ma_states = _mlstm(p16, gcol, gt3, mlstm_fbias[l], l, False, prev=ma_states)
        a = _mlstm(p16, gcol, gt3, mlstm_fbias[l], l, True, state_mlstm_C, state_mlstm_n, state_mlstm_m, a_out=a)
        b, *kv = _ctx_attention(p16, l, *kv)
        b = _lat_attention(p16, cache_na_k, cache_na_v, na_bias, l, b)
        cc, hg_state = _hgrn(p16, p32, lb_all[:, l], l, False, prev=hg_state)
        cc = _hgrn(p16, p32, lb_all[:, l], l, True, state_hgrn, c_out=cc)

        x1, h2x = _merge(a, b, cc, p16, xc, xl, mod, wb, wo, lng, lnb, wr, l)
        y2 = _moe(h2x, w_e1, w_e3, w2g, l)
        outs = _final(x1, y2, mod, lng, lnb, l)
        xc, xl = outs[0], outs[1]
        if l + 1 < DEPTH:
            h = outs[2]

    dt = x_prompt.dtype
    new_C, new_n, new_m = ma_states
    new_n = new_n.reshape(BATCH, DEPTH, 2, MA_HEADS, MA_DK)
    new_m = new_m[:, :, :, 0].reshape(BATCH, DEPTH, 2, MA_HEADS)
    return (xc.reshape(BATCH, SEQ, D_MODEL), xl.reshape(DEC_BATCH, DEC_SEQ, D_MODEL), kv[0], kv[1],
            new_C.astype(dt), new_n.astype(dt), new_m.astype(dt), hg_state.astype(dt))
```

```python
import functools

import numpy as np
import jax
import jax.numpy as jnp
from jax import lax
from jax.experimental import pallas as pl
from jax.experimental.pallas import tpu as pltpu

F32 = jnp.float32
BF16 = jnp.bfloat16
HI = lax.Precision.HIGHEST

D_MODEL = 1024
BATCH = 16
SEQ = 256
DEPTH = 2
DEC_BATCH = 4
DEC_SEQ = 1024
PAST_LEN = 256
GRID_W = 64
MA_HEADS = 4
MA_DK = 128
MA_DV = 128
MA_CHUNK = 64
NA_HEADS = 8
NA_DH = 64
NA_KR_MAX = 8
NA_KC = 16
HG_HEADS = 4
HG_DK = 128
HG_DV = 128
HG_CHUNK = 32
BRANCH_W = 512
N_GROUPS = 4
EXP_PER_GROUP = 4
N_EXPERTS = N_GROUPS * EXP_PER_GROUP
D_EXPERT = 512
ROPE_BASE = 10000.0
LN_EPS = 1e-5
RMS_EPS = 1e-6
DEEPNORM_ALPHA = (2 * DEPTH) ** 0.25

N_CTX = BATCH * SEQ
N_LAT = DEC_BATCH * DEC_SEQ
N_TOK = N_CTX + N_LAT
N_MODROWS = 8
GATE_COL0 = 4 * BRANCH_W
N_GATES = 4 * MA_HEADS
N_IN = 9232
P_COLS = N_IN - N_GATES
MOE_TM = 512
MOE_NT = N_TOK // MOE_TM + N_GROUPS
MOE_FE = 4
MOE_XW = D_MODEL + 128
MOE_XDT = F32
ROUTER_ROWS = 32
ROUTER_E0 = 8
VMEM_LIMIT = 56 * 1024 * 1024

_NT = (((1,), (1,)), ((), ()))
_TN = (((0,), (0,)), ((), ()))


def _cparams(*sem):
    return pltpu.CompilerParams(dimension_semantics=sem, vmem_limit_bytes=VMEM_LIMIT)


def _mod_row(tile, tm):
    return jnp.maximum((tile * tm) // DEC_SEQ - (N_CTX // DEC_SEQ - 1), 0)


def _mod_spec(l, part, tm):
    return pl.BlockSpec((None, None, None, 1, D_MODEL), lambda i: (l, _mod_row(i, tm), part, 0, 0))


def _pair_specs(tm):
    nc = N_CTX // tm
    return (pl.BlockSpec((tm, D_MODEL), lambda i: (jnp.minimum(i, nc - 1), 0)),
            pl.BlockSpec((tm, D_MODEL), lambda i: (jnp.maximum(i - nc, 0), 0)))


def _pair_read(i, tm, c_ref, l_ref):
    return jnp.where(i < N_CTX // tm, c_ref[...], l_ref[...])


def _silu(x):
    return x * jax.nn.sigmoid(x)


def _layer_norm(x, g, b):
    mu = jnp.mean(x, -1, keepdims=True)
    xc = x - mu
    var = jnp.mean(xc * xc, -1, keepdims=True)
    return xc * lax.rsqrt(var + LN_EPS) * g + b


def _log_sigmoid(x):
    return jnp.minimum(x, 0.0) - jnp.log(1.0 + jnp.exp(-jnp.abs(x)))


def _tri(n, upper):
    r = lax.broadcasted_iota(jnp.int32, (n, n), 0)
    c = lax.broadcasted_iota(jnp.int32, (n, n), 1)
    return jnp.where((r <= c) if upper else (r >= c), 1.0, 0.0).astype(F32)


def _mod_kernel(c_ref, w_ref, b_ref, o_ref):
    s = _silu(c_ref[...])
    o_ref[...] = jnp.dot(s.astype(BF16), w_ref[...].astype(BF16), preferred_element_type=F32) + b_ref[...]


def _modulation(cs, w_mod, b_mod):
    tn = 1024
    return pl.pallas_call(
        _mod_kernel,
        out_shape=jax.ShapeDtypeStruct((DEPTH, N_MODROWS, 6 * D_MODEL), F32),
        grid=(DEPTH, 6 * D_MODEL // tn),
        in_specs=[pl.BlockSpec((N_MODROWS, D_MODEL), lambda l, j: (0, 0)),
                  pl.BlockSpec((None, D_MODEL, tn), lambda l, j: (l, 0, j)),
                  pl.BlockSpec((None, 1, tn), lambda l, j: (l, 0, j))],
        out_specs=pl.BlockSpec((None, N_MODROWS, tn), lambda l, j: (l, 0, j)),
        compiler_params=_cparams("arbitrary", "arbitrary"),
        name="modulation",
    )(cs, w_mod, b_mod.reshape(DEPTH, 1, 6 * D_MODEL))


def _prep_kernel(xc_ref, xl_ref, sh_ref, sc_ref, h_ref, *, tm):
    x = _pair_read(pl.program_id(0), tm, xc_ref, xl_ref)
    h_ref[...] = (x * (1.0 + sc_ref[...]) + sh_ref[...]).astype(BF16)


def _prep(xc, xl, mod):
    tm = 1024
    return pl.pallas_call(
        functools.partial(_prep_kernel, tm=tm),
        out_shape=jax.ShapeDtypeStruct((N_TOK, D_MODEL), BF16),
        grid=(N_TOK // tm,),
        in_specs=[*_pair_specs(tm), _mod_spec(0, 0, tm), _mod_spec(0, 1, tm)],
        out_specs=pl.BlockSpec((tm, D_MODEL), lambda i: (i, 0)),
        compiler_params=_cparams("arbitrary"),
        name="prep",
    )(xc, xl, mod, mod)


INPROJ_TN = 512
N_PLAIN_TILES = GATE_COL0 // INPROJ_TN


F32_TILE0 = 7
N_F32_TILES = 2
P16_COLS = P_COLS - N_F32_TILES * INPROJ_TN


def _inproj_kernel(h_ref, wa_ref, wb_ref, b_ref, o_ref, *, src_tile):
    j = src_tile(pl.program_id(1))

    @pl.when(j < N_PLAIN_TILES)
    def _():
        o_ref[...] = (lax.dot_general(h_ref[...], wa_ref[...].astype(BF16), _NT, preferred_element_type=F32)
                      + b_ref[...]).astype(o_ref.dtype)

    @pl.when(j >= N_PLAIN_TILES)
    def _():
        w = jnp.concatenate([wa_ref[N_GATES:, :], wb_ref[...]], 0)
        o_ref[...] = (lax.dot_general(h_ref[...], w.astype(BF16), _NT, preferred_element_type=F32)
                      + b_ref[...]).astype(o_ref.dtype)


def _inproj_call(h, w_t, b_main, l, tm, n_tiles, src_tile, dtype, name):
    tn = INPROJ_TN
    return pl.pallas_call(
        functools.partial(_inproj_kernel, src_tile=src_tile),
        out_shape=jax.ShapeDtypeStruct((N_TOK, n_tiles * tn), dtype),
        grid=(N_TOK // tm, n_tiles),
        in_specs=[pl.BlockSpec((tm, D_MODEL), lambda i, j: (i, 0)),
                  pl.BlockSpec((None, tn, D_MODEL), lambda i, j: (l, src_tile(j), 0)),
                  pl.BlockSpec((None, N_GATES, D_MODEL), lambda i, j: (l, (src_tile(j) + 1) * (tn // N_GATES), 0)),
                  pl.BlockSpec((1, tn), lambda i, j: (0, src_tile(j)))],
        out_specs=pl.BlockSpec((tm, tn), lambda i, j: (i, j)),
        compiler_params=_cparams("arbitrary", "arbitrary"),
        name=name,
    )(h, w_t, w_t, b_main)


def _inproj(h, w_t, b_main, l):
    skip_f32 = lambda j: jnp.where(j < F32_TILE0, j, j + N_F32_TILES)
    p16 = _inproj_call(h, w_t, b_main, l, 4096, P16_COLS // INPROJ_TN, skip_f32, BF16, "inproj")
    p32 = _inproj_call(h, w_t, b_main, l, 4096, N_F32_TILES, lambda j: j + F32_TILE0, F32, "inproj_f32")
    return p16, p32


def _gates_kernel(h_ref, w_ref, b_ref, gc_ref, gt_ref):
    g = lax.dot_general(h_ref[...], w_ref[...].astype(BF16), _NT, preferred_element_type=F32) + b_ref[...]
    gc_ref[...] = g
    gt_ref[...] = g.T[:N_GATES]


def _gates(h, w_t, b_in3, l):
    tm = 1024
    gblk = GATE_COL0 // 128
    return pl.pallas_call(
        _gates_kernel,
        out_shape=(jax.ShapeDtypeStruct((N_TOK, 128), F32), jax.ShapeDtypeStruct((N_GATES, N_TOK), F32)),
        grid=(N_TOK // tm,),
        in_specs=[pl.BlockSpec((tm, D_MODEL), lambda i: (i, 0)),
                  pl.BlockSpec((None, 128, D_MODEL), lambda i: (l, gblk, 0)),
                  pl.BlockSpec((None, 1, 128), lambda i: (l, 0, gblk))],
        out_specs=(pl.BlockSpec((tm, 128), lambda i: (i, 0)), pl.BlockSpec((N_GATES, tm), lambda i: (0, i))),
        compiler_params=_cparams("arbitrary"),
        name="gates",
    )(h, w_t, b_in3)


HEADS_PER_BLK = 128 // NA_DH
NA_NBLK = NA_HEADS // HEADS_PER_BLK
NA_QSCALE = NA_DH ** -0.5
Q_COL, K_COL, V_COL = 16, 20, 24
QKV_COL = 4


def _ctx_attn_kernel(*refs):
    q_ref, k_ref, v_ref = refs[:3]
    o_ref, ko_ref, vo_ref = refs[-3:]
    heads = range(NA_HEADS)
    split = lambda x: jnp.stack([x[:, h * NA_DH:(h + 1) * NA_DH] for h in heads], 0)
    q = split(q_ref[...] * NA_QSCALE)
    k = split(k_ref[...])
    v = split(v_ref[...])
    ko_ref[...] = k.astype(F32)
    vo_ref[...] = v.astype(F32)
    s = lax.dot_general(q, k, (((2,), (2,)), ((0,), (0,))), preferred_element_type=F32)
    e = jnp.exp(s - jnp.max(s, -1, keepdims=True))
    p = e * (1.0 / jnp.sum(e, -1, keepdims=True))
    o = lax.dot_general(p.astype(BF16), v, (((2,), (1,)), ((0,), (0,))), preferred_element_type=F32)
    o_ref[...] = jnp.concatenate([o[h] for h in heads], -1).astype(BF16)


def _ctx_attention(p16, l, prev_k=None, prev_v=None):
    kv_shape = jax.ShapeDtypeStruct((BATCH, DEPTH, NA_HEADS, SEQ, NA_DH), F32)
    kv_spec = pl.BlockSpec((None, None, NA_HEADS, SEQ, NA_DH), lambda b: (b, l, 0, 0, 0))
    col = lambda j: pl.BlockSpec((SEQ, BRANCH_W), lambda b: (b, j))
    in_specs = [col(QKV_COL), col(QKV_COL + 1), col(QKV_COL + 2)]
    args = [p16, p16, p16]
    aliases = {}
    if prev_k is not None:
        in_specs += [pl.BlockSpec(memory_space=pl.ANY)] * 2
        args += [prev_k, prev_v]
        aliases = {3: 1, 4: 2}
    return pl.pallas_call(
        _ctx_attn_kernel,
        out_shape=(jax.ShapeDtypeStruct((N_TOK, BRANCH_W), BF16), kv_shape, kv_shape),
        grid=(BATCH,),
        in_specs=in_specs,
        out_specs=(pl.BlockSpec((SEQ, BRANCH_W), lambda b: (b, 0)), kv_spec, kv_spec),
        input_output_aliases=aliases,
        compiler_params=_cparams("arbitrary"),
        name="ctx_attention",
    )(*args)


NA_ROWS = DEC_SEQ // GRID_W
NA_KR = min(NA_KR_MAX, NA_ROWS)
NA_QROWS = 4
NA_QT = NA_ROWS // NA_QROWS
NA_WROWS = NA_KR + NA_QROWS - 1
NA_WKEYS = NA_WROWS * GRID_W


def _na_window_start(t):
    return min(max(t * NA_QROWS - NA_KR // 2, 0), NA_ROWS - NA_WROWS)


def _na_bias_table(rpb):
    c = np.arange(GRID_W)
    c0 = np.clip(c - NA_KC // 2, 0, GRID_W - NA_KC)
    kc = np.arange(GRID_W)
    valid = (kc[None, :] >= c0[:, None]) & (kc[None, :] < c0[:, None] + NA_KC)
    dc = kc[None, :] - c[:, None] + NA_KC - 1
    onehot = (dc[None] == np.arange(2 * NA_KC - 1)[:, None, None]) & valid[None]
    toep = jnp.einsum('hrd,dcx->hrcx', rpb.astype(F32), jnp.asarray(onehot, F32), precision=HI)
    toep = jnp.where(valid[None, None], toep, -jnp.inf)
    ninf = jnp.full((NA_HEADS, GRID_W, GRID_W), -jnp.inf, F32)
    tiles = []
    for t in range(NA_QT):
        w0 = _na_window_start(t)
        qrows = []
        for r in range(t * NA_QROWS, (t + 1) * NA_QROWS):
            r0 = min(max(r - NA_KR // 2, 0), NA_ROWS - NA_KR)
            blocks = []
            for kr in range(w0, w0 + NA_WROWS):
                inside = r0 <= kr < r0 + NA_KR
                blocks.append(toep[:, kr - r + NA_KR_MAX - 1] if inside else ninf)
            qrows.append(jnp.concatenate(blocks, -1))
        tiles.append(jnp.concatenate(qrows, 1))
    return jnp.stack(tiles, 1)


def _lat_attn_kernel(q_ref, k_ref, v_ref, ck_ref, cv_ref, bias_ref, prev_ref, o_ref):
    heads = range(HEADS_PER_BLK)
    split = lambda x: jnp.stack([x[:, h * NA_DH:(h + 1) * NA_DH] for h in heads], 0)
    bnt = (((2,), (2,)), ((0,), (0,)))
    bnn = (((2,), (1,)), ((0,), (0,)))
    q = (q_ref[...] * NA_QSCALE).astype(BF16)
    k = k_ref[...].astype(BF16)
    v = v_ref[...].astype(BF16)
    ck = ck_ref[...].astype(BF16)
    cv = cv_ref[...].astype(BF16)
    nq = NA_QROWS * GRID_W
    for t in range(NA_QT):
        w0 = _na_window_start(t)
        qs = slice(t * nq, (t + 1) * nq)
        ws = slice(w0 * GRID_W, (w0 + NA_WROWS) * GRID_W)
        qh = split(q[qs])
        s_loc = lax.dot_general(qh, split(k[ws]), bnt, preferred_element_type=F32) + bias_ref[:, t]
        s_ctx = lax.dot_general(qh, ck, bnt, preferred_element_type=F32)
        m = jnp.maximum(jnp.max(s_loc, -1, keepdims=True), jnp.max(s_ctx, -1, keepdims=True))
        e_loc = jnp.exp(s_loc - m)
        e_ctx = jnp.exp(s_ctx - m)
        inv = 1.0 / (jnp.sum(e_loc, -1, keepdims=True) + jnp.sum(e_ctx, -1, keepdims=True))
        acc = (lax.dot_general(e_loc.astype(BF16), split(v[ws]), bnn, preferred_element_type=F32)
               + lax.dot_general(e_ctx.astype(BF16), cv, bnn, preferred_element_type=F32)) * inv
        o_ref[qs, :] = jnp.concatenate([acc[h] for h in heads], -1).astype(BF16)


def _lat_attention(proj, ck, cv, bias, l, b_out):
    rb0 = N_CTX // DEC_SEQ
    cb = lambda base: (lambda j, b: (rb0 + b, base + j))
    c_spec = pl.BlockSpec((None, None, HEADS_PER_BLK, PAST_LEN, NA_DH), lambda j, b: (b, l, j, 0, 0))
    return pl.pallas_call(
        _lat_attn_kernel,
        out_shape=jax.ShapeDtypeStruct((N_TOK, BRANCH_W), BF16),
        grid=(NA_NBLK, DEC_BATCH),
        in_specs=[pl.BlockSpec((DEC_SEQ, 128), cb(Q_COL)), pl.BlockSpec((DEC_SEQ, 128), cb(K_COL)),
                  pl.BlockSpec((DEC_SEQ, 128), cb(V_COL)), c_spec, c_spec,
                  pl.BlockSpec((None, HEADS_PER_BLK, NA_QT, NA_QROWS * GRID_W, NA_WKEYS), lambda j, b: (l, j, 0, 0, 0)),
                  pl.BlockSpec(memory_space=pl.ANY)],
        out_specs=pl.BlockSpec((DEC_SEQ, 128), lambda j, b: (rb0 + b, j)),
        input_output_aliases={6: 0},
        compiler_params=_cparams("arbitrary", "arbitrary"),
        name="lat_attention",
    )(proj, proj, proj, ck, cv, bias, b_out)


MA_KSCALE = MA_DK ** -0.5


def _rope_tables(T):
    t = np.arange(T)
    half = MA_DK // 2
    inv = ROPE_BASE ** (-jnp.arange(0, half, 2, dtype=F32) / half)
    ang_r = jnp.asarray((t // GRID_W).astype(np.float32))[:, None] * inv[None, :]
    ang_c = jnp.asarray((t % GRID_W).astype(np.float32))[:, None] * inv[None, :]
    cos = jnp.concatenate([jnp.cos(ang_r)] * 2 + [jnp.cos(ang_c)] * 2, -1)
    sin = jnp.concatenate([-jnp.sin(ang_r), jnp.sin(ang_r), -jnp.sin(ang_c), jnp.sin(ang_c)], -1)
    return cos, sin


def _mlstm_kernel(*refs, T, latent):
    if latent:
        (p_ref, gc_ref, gt_ref, fbc_ref, fbr_ref, cos_ref, sin_ref, c0_ref, n0_ref, m0_ref, prev_ref,
         a_ref, qs, ks, vT, hfT, hbT, CT, ns, ms, brs, kcs) = refs
    else:
        p_ref, gc_ref, gt_ref, fbc_ref, fbr_ref = refs[:5]
        a_ref, co_ref, no_ref, mo_ref, qs, ks, vT, hfT, hbT, CT, ns, ms, brs, kcs = refs[-14:]
    L = MA_CHUNK
    NC = T // L
    W = BRANCH_W
    PER = 128 // L

    lane = lax.broadcasted_iota(jnp.int32, (T, MA_DK), 1)
    lo_half = (lane % (MA_DK // 2)) < (MA_DK // 4)

    def rope(x):
        if not latent:
            return x
        swapped = jnp.where(lo_half, pltpu.roll(x, MA_DK - MA_DK // 4, 1), pltpu.roll(x, MA_DK // 4, 1))
        return x * cos_ref[...] + swapped * sin_ref[...]

    for h in range(MA_HEADS):
        hs = slice(h * MA_DK, (h + 1) * MA_DK)
        qs[:, hs] = rope(p_ref[:, hs].astype(F32)).astype(BF16)
        ks[:, hs] = rope(p_ref[:, W + h * MA_DK:W + (h + 1) * MA_DK].astype(F32) * MA_KSCALE).astype(BF16)

    def v_block(tb, carry):
        r0 = pl.multiple_of(tb * 128, 128)
        for h in range(MA_HEADS):
            hs = slice(h * MA_DV, (h + 1) * MA_DV)
            blk = p_ref[pl.ds(r0, 128), 2 * W + h * MA_DV:2 * W + (h + 1) * MA_DV].astype(F32).T.astype(BF16)
            for j in range(PER):
                vT[tb * PER + j, h] = blk[:, j * L:(j + 1) * L]
        return carry

    lax.fori_loop(0, T // 128, v_block, 0)

    for d in range(2):
        for h in range(MA_HEADS):
            sidx = d * MA_HEADS + h
            CT[sidx] = c0_ref[d, h].T if latent else jnp.zeros((MA_DV, MA_DK), F32)
            ns[sidx] = n0_ref[sidx:sidx + 1, :] if latent else jnp.zeros((1, MA_DK), F32)
            ms[sidx] = m0_ref[sidx:sidx + 1, :] if latent else jnp.zeros((1, 128), F32)

    low = _tri(L, False)
    upp = _tri(L, True)
    rr = lax.broadcasted_iota(jnp.int32, (L, L), 0)
    cc = lax.broadcasted_iota(jnp.int32, (L, L), 1)
    fbc = fbc_ref[...]
    fbr = fbr_ref[...]

    def gate_sums(c, carry):
        t0 = pl.multiple_of(c * L, L)
        gc = gc_ref[pl.ds(t0, L), :]
        lfc = _log_sigmoid(gc + fbc)
        lfr = _log_sigmoid(gt_ref[c] + fbr)
        ish = pltpu.roll(gc, MA_HEADS, 1)
        brs[0, c] = jnp.dot(lfr, upp, precision=HI, preferred_element_type=F32)
        brs[1, c] = jnp.dot(lfr, low, precision=HI, preferred_element_type=F32)
        kcs[0, pl.ds(t0, L), :] = ish - jnp.dot(low, lfc, precision=HI, preferred_element_type=F32)
        kcs[1, pl.ds(t0, L), :] = ish - jnp.dot(upp, lfc, precision=HI, preferred_element_type=F32)
        return carry

    lax.fori_loop(0, NC, gate_sums, 0, unroll=4)

    def chunk_pair(cf, cb):
        H = MA_HEADS
        G = 2 * H
        heads = range(H)
        cs, ts = (cf, cb), (pl.multiple_of(cf * L, L), pl.multiple_of(cb * L, L))
        rows_of = lambda x, r0: [x[r0 + h:r0 + h + 1, :] for h in heads]
        cols_of = lambda x, c0: [x[:, c0 + h:c0 + h + 1] for h in heads]
        br = jnp.stack(sum([rows_of(brs[d, cs[d]], 2 * d * H + H) for d in range(2)], []), 0)
        ir = jnp.stack(sum([rows_of(gt_ref[cs[d]], 2 * d * H) for d in range(2)], []), 0)
        kcol = jnp.stack(sum([cols_of(kcs[d, pl.ds(ts[d], L), :], 2 * d * H + H) for d in range(2)], []), 0)
        split = lambda x: [x[:, h * MA_DK:(h + 1) * MA_DK] for h in heads]
        q = jnp.stack(split(qs[pl.ds(ts[0], L), :]) + split(qs[pl.ds(ts[1], L), :]), 0)
        k = jnp.stack(split(ks[pl.ds(ts[0], L), :]) + split(ks[pl.ds(ts[1], L), :]), 0)
        vt = jnp.concatenate([vT[cf], vT[cb]], 0)
        m = ms[...][:, :, 0:1]
        n = ns[...]
        ct = CT[...]
        bnt = (((2,), (2,)), ((0,), (0,)))
        bnn = (((2,), (1,)), ((0,), (0,)))
        pre = br + kcol
        dmat = jnp.concatenate([jnp.where(rr <= cc, pre[:H], -jnp.inf), jnp.where(rr >= cc, pre[H:], -jnp.inf)], 0)
        g = br + m
        m_t = jnp.maximum(g, jnp.max(dmat, 1, keepdims=True))
        w_inter = jnp.exp(g - m_t)
        s = lax.dot_general(k, q, bnt, preferred_element_type=F32) * jnp.exp(dmat - m_t)
        ctn = jnp.concatenate([ct.astype(BF16), jnp.broadcast_to(n, (G, 8, MA_DK)).astype(BF16)], 1)
        cq = lax.dot_general(ctn, q, bnt, preferred_element_type=F32)
        num = w_inter * cq[:, :MA_DV] + lax.dot_general(vt, s.astype(BF16), bnn, preferred_element_type=F32)
        den = w_inter * cq[:, MA_DV:MA_DV + 1] + jnp.sum(s, 1, keepdims=True)
        hout = num / jnp.maximum(jnp.abs(den), jnp.exp(-m_t))
        hfT[cf] = hout[:H]
        hbT[cb] = hout[H:]
        last = lambda x: jnp.concatenate([x[:H, :, L - 1:L], x[H:, :, 0:1]], 0)
        m_new = last(m_t)
        b_last = last(br)
        decay = jnp.exp(b_last + m - m_new)
        wk = jnp.exp(b_last - br + ir - m_new)
        wk_hi = wk.astype(BF16)
        wk_lo = (wk - wk_hi.astype(F32)).astype(BF16)
        lhs = jnp.concatenate([(vt.astype(F32) * wk).astype(BF16), wk_hi, wk_lo, jnp.zeros((G, 6, L), BF16)], 1)
        upd = lax.dot_general(lhs, k, bnn, preferred_element_type=F32)
        CT[...] = decay * ct + upd[:, :MA_DV]
        ns[...] = decay * n + upd[:, MA_DV:MA_DV + 1] + upd[:, MA_DV + 1:MA_DV + 2]
        ms[...] = jnp.broadcast_to(m_new, (G, 1, 128))

    def body(i, carry):
        chunk_pair(i, NC - 1 - i)
        return carry

    lax.fori_loop(0, NC, body, 0, unroll=2)

    def out_block(tb, carry):
        r0 = pl.multiple_of(tb * 128, 128)
        hsum = jnp.concatenate([hfT[tb * PER + j] + hbT[tb * PER + j] for j in range(PER)], 2)
        outs = [hsum[h].T for h in range(MA_HEADS)]
        gate = jax.nn.sigmoid(p_ref[pl.ds(r0, 128), 3 * W:4 * W].astype(F32))
        a_ref[pl.ds(r0, 128), :] = (gate * jnp.concatenate(outs, 1)).astype(BF16)
        return carry

    lax.fori_loop(0, T // 128, out_block, 0)
    if not latent:
        for d in range(2):
            for h in range(MA_HEADS):
                sidx = d * MA_HEADS + h
                co_ref[d, h] = CT[sidx].T
                no_ref[sidx:sidx + 1, :] = ns[sidx]
                mo_ref[sidx:sidx + 1, :] = ms[sidx]


def _mlstm(proj, gcol, gt3, fbias_l, l, latent, C0=None, n0=None, m0=None, a_out=None, prev=None):
    T = DEC_SEQ if latent else SEQ
    B = DEC_BATCH if latent else BATCH
    rb0 = N_CTX // DEC_SEQ if latent else 0
    fb = fbias_l.astype(F32)
    fbc = jnp.zeros((1, 128), F32).at[0, MA_HEADS:2 * MA_HEADS].set(fb[0]).at[0, 3 * MA_HEADS:4 * MA_HEADS].set(fb[1])
    fbr = fbc[0, :N_GATES].reshape(N_GATES, 1)
    full2 = lambda b: (0, 0)
    any_spec = pl.BlockSpec(memory_space=pl.ANY)
    in_specs = [pl.BlockSpec((T, 4 * BRANCH_W), lambda b: (rb0 + b, 0)),
                pl.BlockSpec((T, 128), lambda b: (rb0 + b, 0)),
                pl.BlockSpec((T // MA_CHUNK, N_GATES, MA_CHUNK), lambda b: (rb0 + b, 0, 0)),
                pl.BlockSpec((1, 128), full2), pl.BlockSpec((N_GATES, 1), full2)]
    args = [proj, gcol, gt3, fbc, fbr]
    a_shape = jax.ShapeDtypeStruct((N_TOK, BRANCH_W), BF16)
    a_spec = pl.BlockSpec((T, BRANCH_W), lambda b: (rb0 + b, 0))
    c_spec = pl.BlockSpec((None, None, 2, MA_HEADS, MA_DK, MA_DV), lambda b: (b, l, 0, 0, 0, 0))
    nm_spec = pl.BlockSpec((None, None, 2 * MA_HEADS, 128), lambda b: (b, l, 0, 0))
    aliases = {}
    if latent:
        cos, sin = _rope_tables(T)
        nb = 2 * MA_HEADS
        in_specs += [pl.BlockSpec((T, MA_DK), full2), pl.BlockSpec((T, MA_DK), full2), c_spec, nm_spec, nm_spec, any_spec]
        args += [cos, sin, C0, n0.reshape(B, DEPTH, nb, MA_DK),
                 jnp.broadcast_to(m0.reshape(B, DEPTH, nb, 1), (B, DEPTH, nb, 128)), a_out]
        aliases = {len(args) - 1: 0}
        out_shape, out_specs = a_shape, a_spec
    else:
        nm_shape = jax.ShapeDtypeStruct((B, DEPTH, 2 * MA_HEADS, 128), F32)
        out_shape = (a_shape, jax.ShapeDtypeStruct((B, DEPTH, 2, MA_HEADS, MA_DK, MA_DV), F32), nm_shape, nm_shape)
        out_specs = (a_spec, c_spec, nm_spec, nm_spec)
        if prev is not None:
            in_specs += [any_spec] * 3
            args += list(prev)
            aliases = {len(args) - 3: 1, len(args) - 2: 2, len(args) - 1: 3}
    nc = T // MA_CHUNK
    scratch = [pltpu.VMEM((T, BRANCH_W), BF16), pltpu.VMEM((T, BRANCH_W), BF16),
               pltpu.VMEM((nc, MA_HEADS, MA_DV, MA_CHUNK), BF16),
               pltpu.VMEM((nc, MA_HEADS, MA_DV, MA_CHUNK), F32), pltpu.VMEM((nc, MA_HEADS, MA_DV, MA_CHUNK), F32),
               pltpu.VMEM((2 * MA_HEADS, MA_DV, MA_DK), F32), pltpu.VMEM((2 * MA_HEADS, 1, MA_DK), F32),
               pltpu.VMEM((2 * MA_HEADS, 1, 128), F32),
               pltpu.VMEM((2, nc, N_GATES, MA_CHUNK), F32), pltpu.VMEM((2, T, 128), F32)]
    return pl.pallas_call(
        functools.partial(_mlstm_kernel, T=T, latent=latent),
        out_shape=out_shape, grid=(B,), in_specs=in_specs, out_specs=out_specs, scratch_shapes=scratch,
        input_output_aliases=aliases,
        compiler_params=_cparams("arbitrary"),
        name="mlstm_lat" if latent else "mlstm_ctx",
    )(*args)


HG_SUB = 8


def _hgrn_kernel(*refs, T, latent):
    ff_ref, fb_ref, q_ref, i_ref, g_ref, lbf_ref, lbb_ref = refs[:7]
    if latent:
        s0_ref = refs[7]
        c_ref, of, ob, ST, iT, As, Bs = refs[-7:]
    else:
        c_ref, so_ref, of, ob, ST, iT, As, Bs = refs[-8:]
    L = HG_CHUNK
    NC = T // L
    NB = L // HG_SUB
    DK = HG_DK

    for d in range(2):
        for h in range(HG_HEADS):
            ST[d * HG_HEADS + h] = s0_ref[d, h].T if latent else jnp.zeros((HG_DV, DK), F32)

    PER = 128 // L

    def i_block(tb, carry):
        r0 = pl.multiple_of(tb * 128, 128)
        for h in range(HG_HEADS):
            blk = i_ref[pl.ds(r0, 128), h * HG_DV:(h + 1) * HG_DV].astype(F32).T.astype(BF16)
            for j in range(PER):
                iT[tb * PER + j, h] = blk[:, j * L:(j + 1) * L]
        return carry

    lax.fori_loop(0, T // 128, i_block, 0)

    low = _tri(L, False)
    upp = _tri(L, True)
    row8 = lax.broadcasted_iota(jnp.int32, (HG_SUB, L), 0)
    lane_s = lax.broadcasted_iota(jnp.int32, (HG_SUB, L), 1)
    heads = range(HG_HEADS)
    bnt = (((2,), (2,)), ((0,), (0,)))
    bnn = (((2,), (1,)), ((0,), (0,)))
    LOG2E = 1.4426950408889634

    def split(x):
        return jnp.stack([x[:, h * DK:(h + 1) * DK] for h in heads], 0)

    def decay_sums(c, carry):
        t0 = pl.multiple_of(c * L, L)
        for d in range(2):
            fpre = (ff_ref if d == 0 else fb_ref)[pl.ds(t0, L), :]
            lb = (lbf_ref if d == 0 else lbb_ref)[...]
            f = lb + (1.0 - lb) * jax.nn.sigmoid(fpre)
            a = jnp.dot(low if d == 0 else upp, jnp.log(f) * LOG2E, precision=HI, preferred_element_type=F32)
            As[d, pl.ds(t0, L), :] = a
            Bs[d, pl.ds(t0, L), :] = a - jnp.log(1.0 - f) * LOG2E
        return carry

    lax.fori_loop(0, NC, decay_sums, 0, unroll=4)

    def chunk_pair(cf, cb):
        tf = pl.multiple_of(cf * L, L)
        tb = pl.multiple_of(cb * L, L)
        both = lambda fn: jnp.concatenate([fn(0, tf), fn(1, tb)], 0)
        A = both(lambda d, t: split(As[d, pl.ds(t, L), :]))
        B = both(lambda d, t: split(Bs[d, pl.ds(t, L), :]))
        q = both(lambda d, t: split(_silu(q_ref[pl.ds(t, L), :].astype(F32))))
        iv = both(lambda d, t: split(i_ref[pl.ds(t, L), :].astype(BF16)))
        ivT = jnp.concatenate([iT[cf], iT[cb]], 0)
        H = HG_HEADS
        st = ST[...]
        o = lax.dot_general((q * jnp.exp2(A)).astype(BF16), st.astype(BF16), bnt, preferred_element_type=F32)
        a_last = jnp.concatenate([A[:H, L - 1:L], A[H:, 0:1]], 0)
        kd = jnp.exp2(a_last - B).astype(BF16)
        rows = []
        for I in range(NB):
            lo, hi = I * HG_SUB, (I + 1) * HG_SUB
            A_I, q_I = A[:, lo:hi], q[:, lo:hi]
            att_f = jnp.zeros((H, HG_SUB, L), F32)
            att_b = jnp.zeros((H, HG_SUB, L), F32)
            for j in range(HG_SUB):
                s = lo + j
                col = jnp.sum(q_I * jnp.exp2(A_I - B[:, s:s + 1]), -1, keepdims=True)
                att_f = jnp.where((lane_s == s) & (row8 >= j), col[:H], att_f)
                att_b = jnp.where((lane_s == s) & (row8 <= j), col[H:], att_b)
            rf, rb = max(lo - 1, 0), min(hi, L - 1)
            R = jnp.concatenate([A[:H, rf:rf + 1], A[H:, rb:rb + 1]], 0)
            zeros = lambda n: jnp.zeros((H, n, DK), BF16)
            ksc_f = jnp.concatenate([jnp.exp2(R[:H] - B[:H, :lo]).astype(BF16), zeros(L - lo)], 1) if I > 0 else zeros(L)
            ksc_b = jnp.concatenate([zeros(hi), jnp.exp2(R[H:] - B[H:, hi:]).astype(BF16)], 1) if I < NB - 1 else zeros(L)
            ksc = jnp.concatenate([ksc_f, ksc_b], 0)
            off = lax.dot_general((q_I * jnp.exp2(A_I - R)).astype(BF16), ksc, bnt, preferred_element_type=F32)
            rows.append(jnp.concatenate([att_f, att_b], 0) + off)
        att = jnp.concatenate(rows, 1)
        o = o + lax.dot_general(att.astype(BF16), iv, bnn, preferred_element_type=F32)
        for h in heads:
            of[pl.ds(tf, L), h * HG_DV:(h + 1) * HG_DV] = o[h]
            ob[pl.ds(tb, L), h * HG_DV:(h + 1) * HG_DV] = o[H + h]
        ST[...] = st * jnp.exp2(a_last) + lax.dot_general(ivT, kd, bnn, preferred_element_type=F32)

    def body(i, carry):
        chunk_pair(i, NC - 1 - i)
        return carry

    lax.fori_loop(0, NC, body, 0, unroll=4)

    def epilogue(r, carry):
        t0 = pl.multiple_of(r * 128, 128)
        o = of[pl.ds(t0, 128), :] + ob[pl.ds(t0, 128), :]
        gsil = _silu(g_ref[pl.ds(t0, 128), :].astype(F32))
        outs = []
        for h in range(HG_HEADS):
            oh = o[:, h * HG_DV:(h + 1) * HG_DV]
            outs.append(oh * lax.rsqrt(jnp.mean(oh * oh, -1, keepdims=True) + RMS_EPS))
        c_ref[pl.ds(t0, 128), :] = (jnp.concatenate(outs, -1) * gsil).astype(BF16)
        return carry

    lax.fori_loop(0, T // 128, epilogue, 0)
    if not latent:
        for d in range(2):
            for h in range(HG_HEADS):
                so_ref[d, h] = ST[d * HG_HEADS + h].T


def _hgrn(p16, p32, lb_l, l, latent, S0=None, c_out=None, prev=None):
    T = DEC_SEQ if latent else SEQ
    B = DEC_BATCH if latent else BATCH
    rb0 = N_CTX // DEC_SEQ if latent else 0
    W = BRANCH_W
    full2 = lambda b: (0, 0)
    any_spec = pl.BlockSpec(memory_space=pl.ANY)
    col = lambda j: pl.BlockSpec((T, W), lambda b: (rb0 + b, j))
    s_spec = pl.BlockSpec((None, None, 2, HG_HEADS, HG_DK, HG_DV), lambda b: (b, l, 0, 0, 0, 0))
    in_specs = [col(0), col(1), col(7), col(8), col(9), pl.BlockSpec((1, W), full2), pl.BlockSpec((1, W), full2)]
    args = [p32, p32, p16, p16, p16, lb_l[0][None, :], lb_l[1][None, :]]
    c_shape = jax.ShapeDtypeStruct((N_TOK, W), BF16)
    c_spec = pl.BlockSpec((T, W), lambda b: (rb0 + b, 0))
    aliases = {}
    if latent:
        in_specs += [s_spec, any_spec]
        args += [S0, c_out]
        aliases = {8: 0}
        out_shape, out_specs = c_shape, c_spec
    else:
        out_shape = (c_shape, jax.ShapeDtypeStruct((B, DEPTH, 2, HG_HEADS, HG_DK, HG_DV), F32))
        out_specs = (c_spec, s_spec)
        if prev is not None:
            in_specs.append(any_spec)
            args.append(prev)
            aliases = {7: 1}
    scratch = [pltpu.VMEM((T, W), F32), pltpu.VMEM((T, W), F32), pltpu.VMEM((2 * HG_HEADS, HG_DV, HG_DK), F32),
               pltpu.VMEM((T // HG_CHUNK, HG_HEADS, HG_DV, HG_CHUNK), BF16),
               pltpu.VMEM((2, T, W), F32), pltpu.VMEM((2, T, W), F32)]
    return pl.pallas_call(
        functools.partial(_hgrn_kernel, T=T, latent=latent),
        out_shape=out_shape, grid=(B,), in_specs=in_specs, out_specs=out_specs, scratch_shapes=scratch,
        input_output_aliases=aliases,
        compiler_params=_cparams("arbitrary"),
        name="hgrn_lat" if latent else "hgrn_ctx",
    )(*args)


def _merge_kernel(a_ref, b_ref, c_ref, ga_ref, gb_ref, gc_ref, xc_ref, xl_ref, g1_ref, sh2_ref, sc2_ref,
                  wb_ref, wo_ref, lng_ref, lnb_ref, wr_ref, x1_ref, h2_ref, *, tm):
    def br(v_ref, g_ref, k):
        return jax.nn.sigmoid(g_ref[...].astype(F32)) * jnp.dot(v_ref[...], wb_ref[k], preferred_element_type=F32)

    mix = br(a_ref, ga_ref, 0) + br(b_ref, gb_ref, 1) + br(c_ref, gc_ref, 2)
    y = jnp.dot(mix.astype(BF16), wo_ref[...], preferred_element_type=F32)
    x = _pair_read(pl.program_id(0), tm, xc_ref, xl_ref)
    x1 = _layer_norm(DEEPNORM_ALPHA * x + g1_ref[...] * y, lng_ref[...], lnb_ref[...])
    x1_ref[...] = x1
    h2 = x1 * (1.0 + sc2_ref[...]) + sh2_ref[...]
    h2_ref[:, :D_MODEL] = h2.astype(h2_ref.dtype)
    lt = lax.dot_general(wr_ref[...], h2, _NT, preferred_element_type=F32, precision=HI)
    r = lax.broadcasted_iota(jnp.int32, lt.shape, 0)
    neg = -jnp.inf
    lg = jnp.where(r < N_GROUPS, lt, neg)
    mg = jnp.max(lg, 0, keepdims=True)
    g_sel = jnp.min(jnp.where(lg == mg, r, ROUTER_ROWS), 0, keepdims=True)
    p_sel = 1.0 / jnp.sum(jnp.where(r < N_GROUPS, jnp.exp(lg - mg), 0.0), 0, keepdims=True)
    lo = ROUTER_E0 + EXP_PER_GROUP * g_sel
    le = jnp.where((r >= lo) & (r < lo + EXP_PER_GROUP), lt, neg)
    v1 = jnp.max(le, 0, keepdims=True)
    i1 = jnp.min(jnp.where(le == v1, r, ROUTER_ROWS), 0, keepdims=True)
    le2 = jnp.where(r == i1, neg, le)
    v2 = jnp.max(le2, 0, keepdims=True)
    i2 = jnp.min(jnp.where(le2 == v2, r, ROUTER_ROWS), 0, keepdims=True)
    e2 = jnp.exp(v2 - v1)
    w1 = p_sel / (1.0 + e2)
    w2 = p_sel * e2 / (1.0 + e2)
    w1_hi = w1.astype(BF16).astype(F32)
    w2_hi = w2.astype(BF16).astype(F32)
    j1, j2 = i1 - lo, i2 - lo
    packed = jnp.where(r == j1, w1_hi, jnp.where(r == j2, w2_hi, jnp.where(
        r == j1 + EXP_PER_GROUP, w1 - w1_hi, jnp.where(r == j2 + EXP_PER_GROUP, w2 - w2_hi, jnp.where(
            r == 2 * EXP_PER_GROUP, g_sel.astype(F32), 0.0)))))
    packed = jnp.concatenate([packed, jnp.zeros((128 - ROUTER_ROWS, packed.shape[1]), F32)], 0)
    h2_ref[:, D_MODEL:] = packed.T.astype(h2_ref.dtype)


def _merge(a, b, c, p16, xc, xl, mod, wb, wo, lng, lnb, wr, l):
    tm = 512
    tok = lambda i: (i, 0)
    ln_spec = pl.BlockSpec((None, None, 1, D_MODEL), lambda i: (l, 0, 0, 0))
    return pl.pallas_call(
        functools.partial(_merge_kernel, tm=tm),
        out_shape=(jax.ShapeDtypeStruct((N_TOK, D_MODEL), F32), jax.ShapeDtypeStruct((N_TOK, MOE_XW), MOE_XDT)),
        grid=(N_TOK // tm,),
        in_specs=[pl.BlockSpec((tm, BRANCH_W), tok), pl.BlockSpec((tm, BRANCH_W), tok), pl.BlockSpec((tm, BRANCH_W), tok),
                  pl.BlockSpec((tm, D_MODEL), lambda i: (i, 5)), pl.BlockSpec((tm, D_MODEL), lambda i: (i, 6)),
                  pl.BlockSpec((tm, D_MODEL), lambda i: (i, 7)),
                  *_pair_specs(tm), _mod_spec(l, 2, tm), _mod_spec(l, 3, tm), _mod_spec(l, 4, tm),
                  pl.BlockSpec((None, 3, BRANCH_W, D_MODEL), lambda i: (l, 0, 0, 0)),
                  pl.BlockSpec((None, D_MODEL, D_MODEL), lambda i: (l, 0, 0)),
                  ln_spec, ln_spec,
                  pl.BlockSpec((None, ROUTER_ROWS, D_MODEL), lambda i: (l, 0, 0))],
        out_specs=(pl.BlockSpec((tm, D_MODEL), tok), pl.BlockSpec((tm, MOE_XW), tok)),
        compiler_params=_cparams("arbitrary"),
        name="merge",
    )(a, b, c, p16, p16, p16, xc, xl, mod, mod, mod, wb, wo, lng, lnb, wr)


def _moe_up_kernel(gid_ref, nused_ref, x_ref, w1_ref, w3_ref, hid_ref, w1b, w3b):
    f = pl.program_id(0)
    t = pl.program_id(1)

    @pl.when(t < nused_ref[0])
    def _():
        first = jnp.logical_or(t == 0, gid_ref[t] != gid_ref[jnp.maximum(t - 1, 0)])

        @pl.when(first)
        def _():
            w1b[...] = w1_ref[...].astype(BF16)
            w3b[...] = w3_ref[...].astype(BF16)

        x = x_ref[:, :D_MODEL].astype(BF16)
        rec = x_ref[:, D_MODEL:].astype(F32)
        lane = lax.broadcasted_iota(jnp.int32, rec.shape, 1)
        for j in range(MOE_FE):
            e = f * MOE_FE + j
            a = jnp.dot(x, w1b[j], preferred_element_type=F32)
            b = jnp.dot(x, w3b[j], preferred_element_type=F32)
            gcol = jnp.sum(jnp.where(jnp.logical_or(lane == e, lane == e + EXP_PER_GROUP), rec, 0.0), -1, keepdims=True)
            hid_ref[:, j * D_EXPERT:(j + 1) * D_EXPERT] = (_silu(a) * b * gcol).astype(BF16)


def _moe_tile(t, n):
    return jnp.minimum(t, n[0] - 1)


def _moe_up(gid, nused, xs, w1, w3, l):
    tm = MOE_TM
    npad = MOE_NT * tm
    nf = EXP_PER_GROUP // MOE_FE
    w_spec = pl.BlockSpec((None, MOE_FE, D_MODEL, D_EXPERT),
                          lambda f, t, g, n: (l, nf * g[_moe_tile(t, n)] + f, 0, 0))
    grid_spec = pltpu.PrefetchScalarGridSpec(
        num_scalar_prefetch=2,
        grid=(nf, MOE_NT),
        in_specs=[pl.BlockSpec((tm, MOE_XW), lambda f, t, g, n: (_moe_tile(t, n), 0)), w_spec, w_spec],
        out_specs=pl.BlockSpec((tm, MOE_FE * D_EXPERT), lambda f, t, g, n: (_moe_tile(t, n), f)),
        scratch_shapes=[pltpu.VMEM((MOE_FE, D_MODEL, D_EXPERT), BF16), pltpu.VMEM((MOE_FE, D_MODEL, D_EXPERT), BF16)],
    )
    return pl.pallas_call(
        _moe_up_kernel,
        out_shape=jax.ShapeDtypeStruct((npad, EXP_PER_GROUP * D_EXPERT), BF16),
        grid_spec=grid_spec,
        compiler_params=_cparams("arbitrary", "arbitrary"),
        name="moe_up",
    )(gid, nused, xs, w1, w3)


def _moe_down_kernel(gid_ref, nused_ref, hid_ref, w2_ref, y_ref, w2b):
    t = pl.program_id(0)

    @pl.when(t < nused_ref[0])
    def _():
        first = jnp.logical_or(t == 0, gid_ref[t] != gid_ref[jnp.maximum(t - 1, 0)])

        @pl.when(first)
        def _():
            w2b[...] = w2_ref[...].astype(BF16)

        y_ref[...] = jnp.dot(hid_ref[...], w2b[...], preferred_element_type=F32)


def _moe_down(gid, nused, hid, w2g, l):
    tm = MOE_TM
    npad = MOE_NT * tm
    hw = EXP_PER_GROUP * D_EXPERT
    grid_spec = pltpu.PrefetchScalarGridSpec(
        num_scalar_prefetch=2,
        grid=(MOE_NT,),
        in_specs=[pl.BlockSpec((tm, hw), lambda t, g, n: (_moe_tile(t, n), 0)),
                  pl.BlockSpec((None, None, hw, D_MODEL), lambda t, g, n: (l, g[_moe_tile(t, n)], 0, 0))],
        out_specs=pl.BlockSpec((tm, D_MODEL), lambda t, g, n: (_moe_tile(t, n), 0)),
        scratch_shapes=[pltpu.VMEM((hw, D_MODEL), BF16)],
    )
    return pl.pallas_call(
        _moe_down_kernel,
        out_shape=jax.ShapeDtypeStruct((npad, D_MODEL), F32),
        grid_spec=grid_spec,
        compiler_params=_cparams("arbitrary"),
        name="moe_down",
    )(gid, nused, hid, w2g)


def _moe(h2x, w1, w3, w2g, l):
    tm = MOE_TM
    npad = MOE_NT * tm
    g = h2x[:, D_MODEL + 2 * EXP_PER_GROUP].astype(jnp.int32)
    onehot = (g[:, None] == jnp.arange(N_GROUPS)[None, :]).astype(jnp.int32)
    counts = jnp.sum(onehot, 0)
    rank = jnp.sum((jnp.cumsum(onehot, 0) - onehot) * onehot, 1)
    padded = (counts + tm - 1) // tm * tm
    ends = jnp.cumsum(padded)
    offs = ends - padded
    dest = offs[g] + rank
    src = (jnp.arange(npad, dtype=jnp.int32) % N_TOK).at[dest].set(jnp.arange(N_TOK, dtype=jnp.int32),
                                                                   unique_indices=True)
    starts = jnp.arange(MOE_NT, dtype=jnp.int32) * tm
    tile_gid = jnp.minimum(jnp.sum((ends[None, :] <= starts[:, None]).astype(jnp.int32), 1), N_GROUPS - 1)
    nused = (ends[-1:] // tm).astype(jnp.int32)
    take = lambda arr, idx: arr.at[idx].get(mode="promise_in_bounds", unique_indices=False)
    hid = _moe_up(tile_gid, nused, take(h2x, src), w1, w3, l)
    ys = _moe_down(tile_gid, nused, hid, w2g, l)
    return take(ys, dest)


def _final_kernel(*refs, tm, with_h):
    x1_ref, y_ref, g2_ref, lng_ref, lnb_ref = refs[:5]
    x2 = _layer_norm(DEEPNORM_ALPHA * x1_ref[...] + g2_ref[...] * y_ref[...], lng_ref[...], lnb_ref[...])
    i = pl.program_id(0)
    if with_h:
        sh_ref, sc_ref, xc_ref, xl_ref, h_ref = refs[5:]
        h_ref[...] = (x2 * (1.0 + sc_ref[...]) + sh_ref[...]).astype(BF16)
    else:
        xc_ref, xl_ref = refs[5:]

    @pl.when(i < N_CTX // tm)
    def _():
        xc_ref[...] = x2

    @pl.when(i >= N_CTX // tm)
    def _():
        xl_ref[...] = x2


def _final(x1, y, mod, lng, lnb, l):
    tm = 1024
    tok = lambda i: (i, 0)
    with_h = l + 1 < DEPTH
    ln_spec = pl.BlockSpec((None, None, 1, D_MODEL), lambda i: (l, 1, 0, 0))
    half = jax.ShapeDtypeStruct((N_CTX, D_MODEL), F32)
    in_specs = [pl.BlockSpec((tm, D_MODEL), tok), pl.BlockSpec((tm, D_MODEL), tok), _mod_spec(l, 5, tm), ln_spec, ln_spec]
    args = [x1, y, mod, lng, lnb]
    out_shape = [half, half]
    out_specs = list(_pair_specs(tm))
    if with_h:
        in_specs += [_mod_spec(l + 1, 0, tm), _mod_spec(l + 1, 1, tm)]
        args += [mod, mod]
        out_shape.append(jax.ShapeDtypeStruct((N_TOK, D_MODEL), BF16))
        out_specs.append(pl.BlockSpec((tm, D_MODEL), tok))
    return pl.pallas_call(
        functools.partial(_final_kernel, tm=tm, with_h=with_h),
        out_shape=tuple(out_shape), grid=(N_TOK // tm,), in_specs=in_specs, out_specs=tuple(out_specs),
        compiler_params=_cparams("arbitrary"),
        name="final",
    )(*args)


def kernel(x_prompt, x_sample, c, cache_na_k, cache_na_v, state_mlstm_C, state_mlstm_n, state_mlstm_m, state_hgrn,
           c_ctx, w_mod, b_mod, w_in, b_in, mlstm_fbias, hgrn_lb_logits, na_rpb, w_branch, w_out, ln_g, ln_b,
           w_rg, w_re, w_e1, w_e3, w_e2):
    assert N_CTX == N_LAT
    lb_cum = jnp.cumsum(jax.nn.softmax(hgrn_lb_logits.astype(F32), axis=1), axis=1)
    lb_all = lb_cum - lb_cum[:, :1]

    cs = jnp.zeros((N_MODROWS, D_MODEL), F32).at[0].set(c_ctx).at[1:1 + DEC_BATCH].set(c)
    mod = _modulation(cs, w_mod, b_mod).reshape(DEPTH, N_MODROWS, 6, 1, D_MODEL)

    wb = w_branch.astype(BF16)
    wo = w_out.astype(BF16)
    lng = ln_g.reshape(DEPTH, 2, 1, D_MODEL)
    lnb = ln_b.reshape(DEPTH, 2, 1, D_MODEL)
    wr = jnp.zeros((DEPTH, ROUTER_ROWS, D_MODEL), F32)
    wr = wr.at[:, :N_GROUPS].set(jnp.swapaxes(w_rg, 1, 2)).at[:, ROUTER_E0:ROUTER_E0 + N_EXPERTS].set(jnp.swapaxes(w_re, 1, 2))
    w2g = w_e2.reshape(DEPTH, N_GROUPS, EXP_PER_GROUP * D_EXPERT, D_MODEL)
    b_main = jnp.concatenate([b_in[:, :GATE_COL0], b_in[:, GATE_COL0 + N_GATES:]], 1)

    xc = x_prompt.reshape(N_CTX, D_MODEL)
    xl = x_sample.reshape(N_LAT, D_MODEL)
    w_t = jnp.swapaxes(w_in, 1, 2)
    na_bias = jax.vmap(_na_bias_table)(na_rpb)
    h = _prep(xc, xl, mod)

    kv = (None, None)
    ma_states = None
    hg_state = None
    for l in range(DEPTH):
        p16, p32 = _inproj(h, w_t, b_main[l][None, :], l)
        gcol, gt = _gates(h, w_t, b_in.reshape(DEPTH, 1, N_IN), l)
        gt3 = gt.reshape(N_GATES, N_TOK // MA_CHUNK, MA_CHUNK).transpose(1, 0, 2)

        a, *ma_states = _mlstm(p16, gcol, gt3, mlstm_fbias[l], l, False, prev=ma_states)
        a = _mlstm(p16, gcol, gt3, mlstm_fbias[l], l, True, state_mlstm_C, state_mlstm_n, state_mlstm_m, a_out=a)
        b, *kv = _ctx_attention(p16, l, *kv)
        b = _lat_attention(p16, cache_na_k, cache_na_v, na_bias, l, b)
        cc, hg_state = _hgrn(p16, p32, lb_all[:, l], l, False, prev=hg_state)
        cc = _hgrn(p16, p32, lb_all[:, l], l, True, state_hgrn, c_out=cc)

        x1, h2x = _merge(a, b, cc, p16, xc, xl, mod, wb, wo, lng, lnb, wr, l)
        y2 = _moe(h2x, w_e1, w_e3, w2g, l)
        outs = _final(x1, y2, mod, lng, lnb, l)
        xc, xl = outs[0], outs[1]
        if l + 1 < DEPTH:
            h = outs[2]

    dt = x_prompt.dtype
    new_C, new_n, new_m = ma_states
    new_n = new_n.reshape(BATCH, DEPTH, 2, MA_HEADS, MA_DK)
    new_m = new_m[:, :, :, 0].reshape(BATCH, DEPTH, 2, MA_HEADS)
    return (xc.reshape(BATCH, SEQ, D_MODEL), xl.reshape(DEC_BATCH, DEC_SEQ, D_MODEL), kv[0], kv[1],
            new_C.astype(dt), new_n.astype(dt), new_m.astype(dt), hg_state.astype(dt))
```

```python
import functools

import numpy as np
import jax
import jax.numpy as jnp
from jax import lax
from jax.experimental import pallas as pl
from jax.experimental.pallas import tpu as pltpu

F32 = jnp.float32
BF16 = jnp.bfloat16
HI = lax.Precision.HIGHEST

D_MODEL = 1024
BATCH = 16
SEQ = 256
DEPTH = 2
DEC_BATCH = 4
DEC_SEQ = 1024
PAST_LEN = 256
GRID_W = 64
MA_HEADS = 4
MA_DK = 128
MA_DV = 128
MA_CHUNK = 64
NA_HEADS = 8
NA_DH = 64
NA_KR_MAX = 8
NA_KC = 16
HG_HEADS = 4
HG_DK = 128
HG_DV = 128
HG_CHUNK = 32
BRANCH_W = 512
N_GROUPS = 4
EXP_PER_GROUP = 4
N_EXPERTS = N_GROUPS * EXP_PER_GROUP
D_EXPERT = 512
ROPE_BASE = 10000.0
LN_EPS = 1e-5
RMS_EPS = 1e-6
DEEPNORM_ALPHA = (2 * DEPTH) ** 0.25

N_CTX = BATCH * SEQ
N_LAT = DEC_BATCH * DEC_SEQ
N_TOK = N_CTX + N_LAT
N_MODROWS = 8
GATE_COL0 = 4 * BRANCH_W
N_GATES = 4 * MA_HEADS
N_IN = 9232
P_COLS = N_IN - N_GATES
MOE_TM = 512
MOE_NT = N_TOK // MOE_TM + N_GROUPS
MOE_FE = 4
MOE_XW = D_MODEL + 128
MOE_XDT = F32
ROUTER_ROWS = 32
ROUTER_E0 = 8
VMEM_LIMIT = 56 * 1024 * 1024

_NT = (((1,), (1,)), ((), ()))
_TN = (((0,), (0,)), ((), ()))


def _cparams(*sem):
    return pltpu.CompilerParams(dimension_semantics=sem, vmem_limit_bytes=VMEM_LIMIT)


def _mod_row(tile, tm):
    return jnp.maximum((tile * tm) // DEC_SEQ - (N_CTX // DEC_SEQ - 1), 0)


def _mod_spec(l, part, tm):
    return pl.BlockSpec((None, None, None, 1, D_MODEL), lambda i: (l, _mod_row(i, tm), part, 0, 0))


def _pair_specs(tm):
    nc = N_CTX // tm
    return (pl.BlockSpec((tm, D_MODEL), lambda i: (jnp.minimum(i, nc - 1), 0)),
            pl.BlockSpec((tm, D_MODEL), lambda i: (jnp.maximum(i - nc, 0), 0)))


def _pair_read(i, tm, c_ref, l_ref):
    return jnp.where(i < N_CTX // tm, c_ref[...], l_ref[...])


def _silu(x):
    return x * jax.nn.sigmoid(x)


def _layer_norm(x, g, b):
    mu = jnp.mean(x, -1, keepdims=True)
    xc = x - mu
    var = jnp.mean(xc * xc, -1, keepdims=True)
    return xc * lax.rsqrt(var + LN_EPS) * g + b


def _log_sigmoid(x):
    return jnp.minimum(x, 0.0) - jnp.log(1.0 + jnp.exp(-jnp.abs(x)))


def _tri(n, upper):
    r = lax.broadcasted_iota(jnp.int32, (n, n), 0)
    c = lax.broadcasted_iota(jnp.int32, (n, n), 1)
    return jnp.where((r <= c) if upper else (r >= c), 1.0, 0.0).astype(F32)


def _mod_kernel(c_ref, w_ref, b_ref, o_ref):
    s = _silu(c_ref[...])
    o_ref[...] = jnp.dot(s.astype(BF16), w_ref[...].astype(BF16), preferred_element_type=F32) + b_ref[...]


def _modulation(cs, w_mod, b_mod):
    tn = 1024
    return pl.pallas_call(
        _mod_kernel,
        out_shape=jax.ShapeDtypeStruct((DEPTH, N_MODROWS, 6 * D_MODEL), F32),
        grid=(DEPTH, 6 * D_MODEL // tn),
        in_specs=[pl.BlockSpec((N_MODROWS, D_MODEL), lambda l, j: (0, 0)),
                  pl.BlockSpec((None, D_MODEL, tn), lambda l, j: (l, 0, j)),
                  pl.BlockSpec((None, 1, tn), lambda l, j: (l, 0, j))],
        out_specs=pl.BlockSpec((None, N_MODROWS, tn), lambda l, j: (l, 0, j)),
        compiler_params=_cparams("arbitrary", "arbitrary"),
        name="modulation",
    )(cs, w_mod, b_mod.reshape(DEPTH, 1, 6 * D_MODEL))


def _prep_kernel(xc_ref, xl_ref, sh_ref, sc_ref, h_ref, *, tm):
    x = _pair_read(pl.program_id(0), tm, xc_ref, xl_ref)
    h_ref[...] = (x * (1.0 + sc_ref[...]) + sh_ref[...]).astype(BF16)


def _prep(xc, xl, mod):
    tm = 1024
    return pl.pallas_call(
        functools.partial(_prep_kernel, tm=tm),
        out_shape=jax.ShapeDtypeStruct((N_TOK, D_MODEL), BF16),
        grid=(N_TOK // tm,),
        in_specs=[*_pair_specs(tm), _mod_spec(0, 0, tm), _mod_spec(0, 1, tm)],
        out_specs=pl.BlockSpec((tm, D_MODEL), lambda i: (i, 0)),
        compiler_params=_cparams("arbitrary"),
        name="prep",
    )(xc, xl, mod, mod)


INPROJ_TN = 512
N_PLAIN_TILES = GATE_COL0 // INPROJ_TN


F32_TILE0 = 7
N_F32_TILES = 2
P16_COLS = P_COLS - N_F32_TILES * INPROJ_TN


def _inproj_kernel(h_ref, wa_ref, wb_ref, b_ref, o_ref, *, src_tile):
    j = src_tile(pl.program_id(1))

    @pl.when(j < N_PLAIN_TILES)
    def _():
        o_ref[...] = (lax.dot_general(h_ref[...], wa_ref[...].astype(BF16), _NT, preferred_element_type=F32)
                      + b_ref[...]).astype(o_ref.dtype)

    @pl.when(j >= N_PLAIN_TILES)
    def _():
        w = jnp.concatenate([wa_ref[N_GATES:, :], wb_ref[...]], 0)
        o_ref[...] = (lax.dot_general(h_ref[...], w.astype(BF16), _NT, preferred_element_type=F32)
                      + b_ref[...]).astype(o_ref.dtype)


def _inproj_call(h, w_t, b_main, l, tm, n_tiles, src_tile, dtype, name):
    tn = INPROJ_TN
    return pl.pallas_call(
        functools.partial(_inproj_kernel, src_tile=src_tile),
        out_shape=jax.ShapeDtypeStruct((N_TOK, n_tiles * tn), dtype),
        grid=(N_TOK // tm, n_tiles),
        in_specs=[pl.BlockSpec((tm, D_MODEL), lambda i, j: (i, 0)),
                  pl.BlockSpec((None, tn, D_MODEL), lambda i, j: (l, src_tile(j), 0)),
                  pl.BlockSpec((None, N_GATES, D_MODEL), lambda i, j: (l, (src_tile(j) + 1) * (tn // N_GATES), 0)),
                  pl.BlockSpec((1, tn), lambda i, j: (0, src_tile(j)))],
        out_specs=pl.BlockSpec((tm, tn), lambda i, j: (i, j)),
        compiler_params=_cparams("arbitrary", "arbitrary"),
        name=name,
    )(h, w_t, w_t, b_main)


def _inproj(h, w_t, b_main, l):
    skip_f32 = lambda j: jnp.where(j < F32_TILE0, j, j + N_F32_TILES)
    p16 = _inproj_call(h, w_t, b_main, l, 4096, P16_COLS // INPROJ_TN, skip_f32, BF16, "inproj")
    p32 = _inproj_call(h, w_t, b_main, l, 4096, N_F32_TILES, lambda j: j + F32_TILE0, F32, "inproj_f32")
    return p16, p32


def _gates_kernel(h_ref, w_ref, b_ref, gc_ref, gt_ref):
    g = lax.dot_general(h_ref[...], w_ref[...].astype(BF16), _NT, preferred_element_type=F32) + b_ref[...]
    gc_ref[...] = g
    gt_ref[...] = g.T[:N_GATES]


def _gates(h, w_t, b_in3, l):
    tm = 1024
    gblk = GATE_COL0 // 128
    return pl.pallas_call(
        _gates_kernel,
        out_shape=(jax.ShapeDtypeStruct((N_TOK, 128), F32), jax.ShapeDtypeStruct((N_GATES, N_TOK), F32)),
        grid=(N_TOK // tm,),
        in_specs=[pl.BlockSpec((tm, D_MODEL), lambda i: (i, 0)),
                  pl.BlockSpec((None, 128, D_MODEL), lambda i: (l, gblk, 0)),
                  pl.BlockSpec((None, 1, 128), lambda i: (l, 0, gblk))],
        out_specs=(pl.BlockSpec((tm, 128), lambda i: (i, 0)), pl.BlockSpec((N_GATES, tm), lambda i: (0, i))),
        compiler_params=_cparams("arbitrary"),
        name="gates",
    )(h, w_t, b_in3)


HEADS_PER_BLK = 128 // NA_DH
NA_NBLK = NA_HEADS // HEADS_PER_BLK
NA_QSCALE = NA_DH ** -0.5
Q_COL, K_COL, V_COL = 16, 20, 24
QKV_COL = 4


def _ctx_attn_kernel(*refs):
    q_ref, k_ref, v_ref = refs[:3]
    o_ref, ko_ref, vo_ref = refs[-3:]
    heads = range(NA_HEADS)
    split = lambda x: jnp.stack([x[:, h * NA_DH:(h + 1) * NA_DH] for h in heads], 0)
    q = split(q_ref[...] * NA_QSCALE)
    k = split(k_ref[...])
    v = split(v_ref[...])
    ko_ref[...] = k.astype(F32)
    vo_ref[...] = v.astype(F32)
    s = lax.dot_general(q, k, (((2,), (2,)), ((0,), (0,))), preferred_element_type=F32)
    e = jnp.exp(s - jnp.max(s, -1, keepdims=True))
    p = e * (1.0 / jnp.sum(e, -1, keepdims=True))
    o = lax.dot_general(p.astype(BF16), v, (((2,), (1,)), ((0,), (0,))), preferred_element_type=F32)
    o_ref[...] = jnp.concatenate([o[h] for h in heads], -1).astype(BF16)


def _ctx_attention(p16, l, prev_k=None, prev_v=None):
    kv_shape = jax.ShapeDtypeStruct((BATCH, DEPTH, NA_HEADS, SEQ, NA_DH), F32)
    kv_spec = pl.BlockSpec((None, None, NA_HEADS, SEQ, NA_DH), lambda b: (b, l, 0, 0, 0))
    col = lambda j: pl.BlockSpec((SEQ, BRANCH_W), lambda b: (b, j))
    in_specs = [col(QKV_COL), col(QKV_COL + 1), col(QKV_COL + 2)]
    args = [p16, p16, p16]
    aliases = {}
    if prev_k is not None:
        in_specs += [pl.BlockSpec(memory_space=pl.ANY)] * 2
        args += [prev_k, prev_v]
        aliases = {3: 1, 4: 2}
    return pl.pallas_call(
        _ctx_attn_kernel,
        out_shape=(jax.ShapeDtypeStruct((N_TOK, BRANCH_W), BF16), kv_shape, kv_shape),
        grid=(BATCH,),
        in_specs=in_specs,
        out_specs=(pl.BlockSpec((SEQ, BRANCH_W), lambda b: (b, 0)), kv_spec, kv_spec),
        input_output_aliases=aliases,
        compiler_params=_cparams("arbitrary"),
        name="ctx_attention",
    )(*args)


NA_ROWS = DEC_SEQ // GRID_W
NA_KR = min(NA_KR_MAX, NA_ROWS)
NA_QROWS = 4
NA_QT = NA_ROWS // NA_QROWS
NA_WROWS = NA_KR + NA_QROWS - 1
NA_WKEYS = NA_WROWS * GRID_W


def _na_window_start(t):
    return min(max(t * NA_QROWS - NA_KR // 2, 0), NA_ROWS - NA_WROWS)


def _na_bias_table(rpb):
    c = np.arange(GRID_W)
    c0 = np.clip(c - NA_KC // 2, 0, GRID_W - NA_KC)
    kc = np.arange(GRID_W)
    valid = (kc[None, :] >= c0[:, None]) & (kc[None, :] < c0[:, None] + NA_KC)
    dc = kc[None, :] - c[:, None] + NA_KC - 1
    onehot = (dc[None] == np.arange(2 * NA_KC - 1)[:, None, None]) & valid[None]
    toep = jnp.einsum('hrd,dcx->hrcx', rpb.astype(F32), jnp.asarray(onehot, F32), precision=HI)
    toep = jnp.where(valid[None, None], toep, -jnp.inf)
    ninf = jnp.full((NA_HEADS, GRID_W, GRID_W), -jnp.inf, F32)
    tiles = []
    for t in range(NA_QT):
        w0 = _na_window_start(t)
        qrows = []
        for r in range(t * NA_QROWS, (t + 1) * NA_QROWS):
            r0 = min(max(r - NA_KR // 2, 0), NA_ROWS - NA_KR)
            blocks = []
            for kr in range(w0, w0 + NA_WROWS):
                inside = r0 <= kr < r0 + NA_KR
                blocks.append(toep[:, kr - r + NA_KR_MAX - 1] if inside else ninf)
            qrows.append(jnp.concatenate(blocks, -1))
        tiles.append(jnp.concatenate(qrows, 1))
    return jnp.stack(tiles, 1).astype(BF16)


def _lat_attn_kernel(q_ref, k_ref, v_ref, ck_ref, cv_ref, bias_ref, prev_ref, o_ref):
    heads = range(HEADS_PER_BLK)
    split = lambda x: jnp.stack([x[:, h * NA_DH:(h + 1) * NA_DH] for h in heads], 0)
    bnt = (((2,), (2,)), ((0,), (0,)))
    bnn = (((2,), (1,)), ((0,), (0,)))
    q = (q_ref[...] * NA_QSCALE).astype(BF16)
    k = k_ref[...].astype(BF16)
    v = v_ref[...].astype(BF16)
    ck = ck_ref[...].astype(BF16)
    cv = cv_ref[...].astype(BF16)
    nq = NA_QROWS * GRID_W
    for t in range(NA_QT):
        w0 = _na_window_start(t)
        qs = slice(t * nq, (t + 1) * nq)
        ws = slice(w0 * GRID_W, (w0 + NA_WROWS) * GRID_W)
        qh = split(q[qs])
        s_loc = lax.dot_general(qh, split(k[ws]), bnt, preferred_element_type=F32) + bias_ref[:, t]
        s_ctx = lax.dot_general(qh, ck, bnt, preferred_element_type=F32)
        m = jnp.maximum(jnp.max(s_loc, -1, keepdims=True), jnp.max(s_ctx, -1, keepdims=True))
        e_loc = jnp.exp(s_loc - m)
        e_ctx = jnp.exp(s_ctx - m)
        inv = 1.0 / (jnp.sum(e_loc, -1, keepdims=True) + jnp.sum(e_ctx, -1, keepdims=True))
        acc = (lax.dot_general(e_loc.astype(BF16), split(v[ws]), bnn, preferred_element_type=F32)
               + lax.dot_general(e_ctx.astype(BF16), cv, bnn, preferred_element_type=F32)) * inv
        o_ref[qs, :] = jnp.concatenate([acc[h] for h in heads], -1).astype(BF16)


def _lat_attention(proj, ck, cv, bias, l, b_out):
    rb0 = N_CTX // DEC_SEQ
    cb = lambda base: (lambda j, b: (rb0 + b, base + j))
    c_spec = pl.BlockSpec((None, None, HEADS_PER_BLK, PAST_LEN, NA_DH), lambda j, b: (b, l, j, 0, 0))
    return pl.pallas_call(
        _lat_attn_kernel,
        out_shape=jax.ShapeDtypeStruct((N_TOK, BRANCH_W), BF16),
        grid=(NA_NBLK, DEC_BATCH),
        in_specs=[pl.BlockSpec((DEC_SEQ, 128), cb(Q_COL)), pl.BlockSpec((DEC_SEQ, 128), cb(K_COL)),
                  pl.BlockSpec((DEC_SEQ, 128), cb(V_COL)), c_spec, c_spec,
                  pl.BlockSpec((None, HEADS_PER_BLK, NA_QT, NA_QROWS * GRID_W, NA_WKEYS), lambda j, b: (l, j, 0, 0, 0)),
                  pl.BlockSpec(memory_space=pl.ANY)],
        out_specs=pl.BlockSpec((DEC_SEQ, 128), lambda j, b: (rb0 + b, j)),
        input_output_aliases={6: 0},
        compiler_params=_cparams("arbitrary", "arbitrary"),
        name="lat_attention",
    )(proj, proj, proj, ck, cv, bias, b_out)


MA_KSCALE = MA_DK ** -0.5


def _rope_tables(T):
    t = np.arange(T)
    half = MA_DK // 2
    inv = ROPE_BASE ** (-jnp.arange(0, half, 2, dtype=F32) / half)
    ang_r = jnp.asarray((t // GRID_W).astype(np.float32))[:, None] * inv[None, :]
    ang_c = jnp.asarray((t % GRID_W).astype(np.float32))[:, None] * inv[None, :]
    cos = jnp.concatenate([jnp.cos(ang_r)] * 2 + [jnp.cos(ang_c)] * 2, -1)
    sin = jnp.concatenate([-jnp.sin(ang_r), jnp.sin(ang_r), -jnp.sin(ang_c), jnp.sin(ang_c)], -1)
    return cos, sin


def _mlstm_kernel(*refs, T, latent):
    if latent:
        (p_ref, gc_ref, gt_ref, fbc_ref, fbr_ref, cos_ref, sin_ref, c0_ref, n0_ref, m0_ref, prev_ref,
         a_ref, qs, ks, vT, hfT, hbT, CT, ns, ms, brs, kcs) = refs
    else:
        p_ref, gc_ref, gt_ref, fbc_ref, fbr_ref = refs[:5]
        a_ref, co_ref, no_ref, mo_ref, qs, ks, vT, hfT, hbT, CT, ns, ms, brs, kcs = refs[-14:]
    L = MA_CHUNK
    NC = T // L
    W = BRANCH_W
    PER = 128 // L

    lane = lax.broadcasted_iota(jnp.int32, (T, MA_DK), 1)
    lo_half = (lane % (MA_DK // 2)) < (MA_DK // 4)

    def rope(x):
        if not latent:
            return x
        swapped = jnp.where(lo_half, pltpu.roll(x, MA_DK - MA_DK // 4, 1), pltpu.roll(x, MA_DK // 4, 1))
        return x * cos_ref[...] + swapped * sin_ref[...]

    for h in range(MA_HEADS):
        hs = slice(h * MA_DK, (h + 1) * MA_DK)
        qs[:, hs] = rope(p_ref[:, hs].astype(F32)).astype(BF16)
        ks[:, hs] = rope(p_ref[:, W + h * MA_DK:W + (h + 1) * MA_DK].astype(F32) * MA_KSCALE).astype(BF16)

    def v_block(tb, carry):
        r0 = pl.multiple_of(tb * 128, 128)
        for h in range(MA_HEADS):
            hs = slice(h * MA_DV, (h + 1) * MA_DV)
            blk = p_ref[pl.ds(r0, 128), 2 * W + h * MA_DV:2 * W + (h + 1) * MA_DV].astype(F32).T.astype(BF16)
            for j in range(PER):
                vT[tb * PER + j, h] = blk[:, j * L:(j + 1) * L]
        return carry

    lax.fori_loop(0, T // 128, v_block, 0)

    for d in range(2):
        for h in range(MA_HEADS):
            sidx = d * MA_HEADS + h
            CT[sidx] = c0_ref[d, h].T if latent else jnp.zeros((MA_DV, MA_DK), F32)
            ns[sidx] = n0_ref[sidx:sidx + 1, :] if latent else jnp.zeros((1, MA_DK), F32)
            ms[sidx] = m0_ref[sidx:sidx + 1, :] if latent else jnp.zeros((1, 128), F32)

    low = _tri(L, False)
    upp = _tri(L, True)
    rr = lax.broadcasted_iota(jnp.int32, (L, L), 0)
    cc = lax.broadcasted_iota(jnp.int32, (L, L), 1)
    fbc = fbc_ref[...]
    fbr = fbr_ref[...]

    def gate_sums(c, carry):
        t0 = pl.multiple_of(c * L, L)
        gc = gc_ref[pl.ds(t0, L), :]
        lfc = _log_sigmoid(gc + fbc)
        lfr = _log_sigmoid(gt_ref[c] + fbr)
        ish = pltpu.roll(gc, MA_HEADS, 1)
        brs[0, c] = jnp.dot(lfr, upp, precision=HI, preferred_element_type=F32)
        brs[1, c] = jnp.dot(lfr, low, precision=HI, preferred_element_type=F32)
        kcs[0, pl.ds(t0, L), :] = ish - jnp.dot(low, lfc, precision=HI, preferred_element_type=F32)
        kcs[1, pl.ds(t0, L), :] = ish - jnp.dot(upp, lfc, precision=HI, preferred_element_type=F32)
        return carry

    lax.fori_loop(0, NC, gate_sums, 0, unroll=4)

    def chunk_pair(cf, cb):
        H = MA_HEADS
        G = 2 * H
        heads = range(H)
        cs, ts = (cf, cb), (pl.multiple_of(cf * L, L), pl.multiple_of(cb * L, L))
        rows_of = lambda x, r0: [x[r0 + h:r0 + h + 1, :] for h in heads]
        cols_of = lambda x, c0: [x[:, c0 + h:c0 + h + 1] for h in heads]
        br = jnp.stack(sum([rows_of(brs[d, cs[d]], 2 * d * H + H) for d in range(2)], []), 0)
        ir = jnp.stack(sum([rows_of(gt_ref[cs[d]], 2 * d * H) for d in range(2)], []), 0)
        kcol = jnp.stack(sum([cols_of(kcs[d, pl.ds(ts[d], L), :], 2 * d * H + H) for d in range(2)], []), 0)
        split = lambda x: [x[:, h * MA_DK:(h + 1) * MA_DK] for h in heads]
        q = jnp.stack(split(qs[pl.ds(ts[0], L), :]) + split(qs[pl.ds(ts[1], L), :]), 0)
        k = jnp.stack(split(ks[pl.ds(ts[0], L), :]) + split(ks[pl.ds(ts[1], L), :]), 0)
        vt = jnp.concatenate([vT[cf], vT[cb]], 0)
        m = ms[...][:, :, 0:1]
        n = ns[...]
        ct = CT[...]
        bnt = (((2,), (2,)), ((0,), (0,)))
        bnn = (((2,), (1,)), ((0,), (0,)))
        pre = br + kcol
        dmat = jnp.concatenate([jnp.where(rr <= cc, pre[:H], -jnp.inf), jnp.where(rr >= cc, pre[H:], -jnp.inf)], 0)
        g = br + m
        m_t = jnp.maximum(g, jnp.max(dmat, 1, keepdims=True))
        w_inter = jnp.exp(g - m_t)
        s = lax.dot_general(k, q, bnt, preferred_element_type=F32) * jnp.exp(dmat - m_t)
        ctn = jnp.concatenate([ct.astype(BF16), jnp.broadcast_to(n, (G, 8, MA_DK)).astype(BF16)], 1)
        cq = lax.dot_general(ctn, q, bnt, preferred_element_type=F32)
        num = w_inter * cq[:, :MA_DV] + lax.dot_general(vt, s.astype(BF16), bnn, preferred_element_type=F32)
        den = w_inter * cq[:, MA_DV:MA_DV + 1] + jnp.sum(s, 1, keepdims=True)
        hout = num / jnp.maximum(jnp.abs(den), jnp.exp(-m_t))
        hfT[cf] = hout[:H]
        hbT[cb] = hout[H:]
        last = lambda x: jnp.concatenate([x[:H, :, L - 1:L], x[H:, :, 0:1]], 0)
        m_new = last(m_t)
        b_last = last(br)
        decay = jnp.exp(b_last + m - m_new)
        wk = jnp.exp(b_last - br + ir - m_new)
        wk_hi = wk.astype(BF16)
        wk_lo = (wk - wk_hi.astype(F32)).astype(BF16)
        lhs = jnp.concatenate([(vt.astype(F32) * wk).astype(BF16), wk_hi, wk_lo, jnp.zeros((G, 6, L), BF16)], 1)
        upd = lax.dot_general(lhs, k, bnn, preferred_element_type=F32)
        CT[...] = decay * ct + upd[:, :MA_DV]
        ns[...] = decay * n + upd[:, MA_DV:MA_DV + 1] + upd[:, MA_DV + 1:MA_DV + 2]
        ms[...] = jnp.broadcast_to(m_new, (G, 1, 128))

    def body(i, carry):
        chunk_pair(i, NC - 1 - i)
        return carry

    lax.fori_loop(0, NC, body, 0, unroll=4)

    def out_block(tb, carry):
        r0 = pl.multiple_of(tb * 128, 128)
        hsum = jnp.concatenate([hfT[tb * PER + j] + hbT[tb * PER + j] for j in range(PER)], 2)
        outs = [hsum[h].T for h in range(MA_HEADS)]
        gate = jax.nn.sigmoid(p_ref[pl.ds(r0, 128), 3 * W:4 * W].astype(F32))
        a_ref[pl.ds(r0, 128), :] = (gate * jnp.concatenate(outs, 1)).astype(BF16)
        return carry

    lax.fori_loop(0, T // 128, out_block, 0)
    if not latent:
        for d in range(2):
            for h in range(MA_HEADS):
                sidx = d * MA_HEADS + h
                co_ref[d, h] = CT[sidx].T
                no_ref[sidx:sidx + 1, :] = ns[sidx]
                mo_ref[sidx:sidx + 1, :] = ms[sidx]


def _mlstm(proj, gcol, gt3, fbias_l, l, latent, C0=None, n0=None, m0=None, a_out=None, prev=None):
    T = DEC_SEQ if latent else SEQ
    B = DEC_BATCH if latent else BATCH
    rb0 = N_CTX // DEC_SEQ if latent else 0
    fb = fbias_l.astype(F32)
    fbc = jnp.zeros((1, 128), F32).at[0, MA_HEADS:2 * MA_HEADS].set(fb[0]).at[0, 3 * MA_HEADS:4 * MA_HEADS].set(fb[1])
    fbr = fbc[0, :N_GATES].reshape(N_GATES, 1)
    full2 = lambda b: (0, 0)
    any_spec = pl.BlockSpec(memory_space=pl.ANY)
    in_specs = [pl.BlockSpec((T, 4 * BRANCH_W), lambda b: (rb0 + b, 0)),
                pl.BlockSpec((T, 128), lambda b: (rb0 + b, 0)),
                pl.BlockSpec((T // MA_CHUNK, N_GATES, MA_CHUNK), lambda b: (rb0 + b, 0, 0)),
                pl.BlockSpec((1, 128), full2), pl.BlockSpec((N_GATES, 1), full2)]
    args = [proj, gcol, gt3, fbc, fbr]
    a_shape = jax.ShapeDtypeStruct((N_TOK, BRANCH_W), BF16)
    a_spec = pl.BlockSpec((T, BRANCH_W), lambda b: (rb0 + b, 0))
    c_spec = pl.BlockSpec((None, None, 2, MA_HEADS, MA_DK, MA_DV), lambda b: (b, l, 0, 0, 0, 0))
    nm_spec = pl.BlockSpec((None, None, 2 * MA_HEADS, 128), lambda b: (b, l, 0, 0))
    aliases = {}
    if latent:
        cos, sin = _rope_tables(T)
        nb = 2 * MA_HEADS
        in_specs += [pl.BlockSpec((T, MA_DK), full2), pl.BlockSpec((T, MA_DK), full2), c_spec, nm_spec, nm_spec, any_spec]
        args += [cos, sin, C0, n0.reshape(B, DEPTH, nb, MA_DK),
                 jnp.broadcast_to(m0.reshape(B, DEPTH, nb, 1), (B, DEPTH, nb, 128)), a_out]
        aliases = {len(args) - 1: 0}
        out_shape, out_specs = a_shape, a_spec
    else:
        nm_shape = jax.ShapeDtypeStruct((B, DEPTH, 2 * MA_HEADS, 128), F32)
        out_shape = (a_shape, jax.ShapeDtypeStruct((B, DEPTH, 2, MA_HEADS, MA_DK, MA_DV), F32), nm_shape, nm_shape)
        out_specs = (a_spec, c_spec, nm_spec, nm_spec)
        if prev is not None:
            in_specs += [any_spec] * 3
            args += list(prev)
            aliases = {len(args) - 3: 1, len(args) - 2: 2, len(args) - 1: 3}
    nc = T // MA_CHUNK
    scratch = [pltpu.VMEM((T, BRANCH_W), BF16), pltpu.VMEM((T, BRANCH_W), BF16),
               pltpu.VMEM((nc, MA_HEADS, MA_DV, MA_CHUNK), BF16),
               pltpu.VMEM((nc, MA_HEADS, MA_DV, MA_CHUNK), F32), pltpu.VMEM((nc, MA_HEADS, MA_DV, MA_CHUNK), F32),
               pltpu.VMEM((2 * MA_HEADS, MA_DV, MA_DK), F32), pltpu.VMEM((2 * MA_HEADS, 1, MA_DK), F32),
               pltpu.VMEM((2 * MA_HEADS, 1, 128), F32),
               pltpu.VMEM((2, nc, N_GATES, MA_CHUNK), F32), pltpu.VMEM((2, T, 128), F32)]
    return pl.pallas_call(
        functools.partial(_mlstm_kernel, T=T, latent=latent),
        out_shape=out_shape, grid=(B,), in_specs=in_specs, out_specs=out_specs, scratch_shapes=scratch,
        input_output_aliases=aliases,
        compiler_params=_cparams("arbitrary"),
        name="mlstm_lat" if latent else "mlstm_ctx",
    )(*args)


HG_SUB = 8


def _hgrn_kernel(*refs, T, latent):
    ff_ref, fb_ref, q_ref, i_ref, g_ref, lbf_ref, lbb_ref = refs[:7]
    if latent:
        s0_ref = refs[7]
        c_ref, of, ob, ST, iT, As, Bs = refs[-7:]
    else:
        c_ref, so_ref, of, ob, ST, iT, As, Bs = refs[-8:]
    L = HG_CHUNK
    NC = T // L
    NB = L // HG_SUB
    DK = HG_DK

    for d in range(2):
        for h in range(HG_HEADS):
            ST[d * HG_HEADS + h] = s0_ref[d, h].T if latent else jnp.zeros((HG_DV, DK), F32)

    PER = 128 // L

    def i_block(tb, carry):
        r0 = pl.multiple_of(tb * 128, 128)
        for h in range(HG_HEADS):
            blk = i_ref[pl.ds(r0, 128), h * HG_DV:(h + 1) * HG_DV].astype(F32).T.astype(BF16)
            for j in range(PER):
                iT[tb * PER + j, h] = blk[:, j * L:(j + 1) * L]
        return carry

    lax.fori_loop(0, T // 128, i_block, 0)

    low = _tri(L, False)
    upp = _tri(L, True)
    row8 = lax.broadcasted_iota(jnp.int32, (HG_SUB, L), 0)
    lane_s = lax.broadcasted_iota(jnp.int32, (HG_SUB, L), 1)
    heads = range(HG_HEADS)
    bnt = (((2,), (2,)), ((0,), (0,)))
    bnn = (((2,), (1,)), ((0,), (0,)))
    LOG2E = 1.4426950408889634

    def split(x):
        return jnp.stack([x[:, h * DK:(h + 1) * DK] for h in heads], 0)

    def decay_sums(c, carry):
        t0 = pl.multiple_of(c * L, L)
        for d in range(2):
            fpre = (ff_ref if d == 0 else fb_ref)[pl.ds(t0, L), :]
            lb = (lbf_ref if d == 0 else lbb_ref)[...]
            f = lb + (1.0 - lb) * jax.nn.sigmoid(fpre)
            a = jnp.dot(low if d == 0 else upp, jnp.log(f) * LOG2E, precision=HI, preferred_element_type=F32)
            As[d, pl.ds(t0, L), :] = a
            Bs[d, pl.ds(t0, L), :] = a - jnp.log(1.0 - f) * LOG2E
        return carry

    lax.fori_loop(0, NC, decay_sums, 0, unroll=4)

    def chunk_pair(cf, cb):
        tf = pl.multiple_of(cf * L, L)
        tb = pl.multiple_of(cb * L, L)
        both = lambda fn: jnp.concatenate([fn(0, tf), fn(1, tb)], 0)
        A = both(lambda d, t: split(As[d, pl.ds(t, L), :]))
        B = both(lambda d, t: split(Bs[d, pl.ds(t, L), :]))
        q = both(lambda d, t: split(_silu(q_ref[pl.ds(t, L), :].astype(F32))))
        iv = both(lambda d, t: split(i_ref[pl.ds(t, L), :].astype(BF16)))
        ivT = jnp.concatenate([iT[cf], iT[cb]], 0)
        H = HG_HEADS
        st = ST[...]
        o = lax.dot_general((q * jnp.exp2(A)).astype(BF16), st.astype(BF16), bnt, preferred_element_type=F32)
        a_last = jnp.concatenate([A[:H, L - 1:L], A[H:, 0:1]], 0)
        kd = jnp.exp2(a_last - B).astype(BF16)
        rows = []
        for I in range(NB):
            lo, hi = I * HG_SUB, (I + 1) * HG_SUB
            A_I, q_I = A[:, lo:hi], q[:, lo:hi]
            att_f = jnp.zeros((H, HG_SUB, L), F32)
            att_b = jnp.zeros((H, HG_SUB, L), F32)
            for j in range(HG_SUB):
                s = lo + j
                col = jnp.sum(q_I * jnp.exp2(A_I - B[:, s:s + 1]), -1, keepdims=True)
                att_f = jnp.where((lane_s == s) & (row8 >= j), col[:H], att_f)
                att_b = jnp.where((lane_s == s) & (row8 <= j), col[H:], att_b)
            rf, rb = max(lo - 1, 0), min(hi, L - 1)
            R = jnp.concatenate([A[:H, rf:rf + 1], A[H:, rb:rb + 1]], 0)
            zeros = lambda n: jnp.zeros((H, n, DK), BF16)
            ksc_f = jnp.concatenate([jnp.exp2(R[:H] - B[:H, :lo]).astype(BF16), zeros(L - lo)], 1) if I > 0 else zeros(L)
            ksc_b = jnp.concatenate([zeros(hi), jnp.exp2(R[H:] - B[H:, hi:]).astype(BF16)], 1) if I < NB - 1 else zeros(L)
            ksc = jnp.concatenate([ksc_f, ksc_b], 0)
            off = lax.dot_general((q_I * jnp.exp2(A_I - R)).astype(BF16), ksc, bnt, preferred_element_type=F32)
            rows.append(jnp.concatenate([att_f, att_b], 0) + off)
        att = jnp.concatenate(rows, 1)
        o = o + lax.dot_general(att.astype(BF16), iv, bnn, preferred_element_type=F32)
        for h in heads:
            of[pl.ds(tf, L), h * HG_DV:(h + 1) * HG_DV] = o[h]
            ob[pl.ds(tb, L), h * HG_DV:(h + 1) * HG_DV] = o[H + h]
        ST[...] = st * jnp.exp2(a_last) + lax.dot_general(ivT, kd, bnn, preferred_element_type=F32)

    def body(i, carry):
        chunk_pair(i, NC - 1 - i)
        return carry

    lax.fori_loop(0, NC, body, 0, unroll=8)

    def epilogue(r, carry):
        t0 = pl.multiple_of(r * 128, 128)
        o = of[pl.ds(t0, 128), :] + ob[pl.ds(t0, 128), :]
        gsil = _silu(g_ref[pl.ds(t0, 128), :].astype(F32))
        outs = []
        for h in range(HG_HEADS):
            oh = o[:, h * HG_DV:(h + 1) * HG_DV]
            outs.append(oh * lax.rsqrt(jnp.mean(oh * oh, -1, keepdims=True) + RMS_EPS))
        c_ref[pl.ds(t0, 128), :] = (jnp.concatenate(outs, -1) * gsil).astype(BF16)
        return carry

    lax.fori_loop(0, T // 128, epilogue, 0)
    if not latent:
        for d in range(2):
            for h in range(HG_HEADS):
                so_ref[d, h] = ST[d * HG_HEADS + h].T


def _hgrn(p16, p32, lb_l, l, latent, S0=None, c_out=None, prev=None):
    T = DEC_SEQ if latent else SEQ
    B = DEC_BATCH if latent else BATCH
    rb0 = N_CTX // DEC_SEQ if latent else 0
    W = BRANCH_W
    full2 = lambda b: (0, 0)
    any_spec = pl.BlockSpec(memory_space=pl.ANY)
    col = lambda j: pl.BlockSpec((T, W), lambda b: (rb0 + b, j))
    s_spec = pl.BlockSpec((None, None, 2, HG_HEADS, HG_DK, HG_DV), lambda b: (b, l, 0, 0, 0, 0))
    in_specs = [col(0), col(1), col(7), col(8), col(9), pl.BlockSpec((1, W), full2), pl.BlockSpec((1, W), full2)]
    args = [p32, p32, p16, p16, p16, lb_l[0][None, :], lb_l[1][None, :]]
    c_shape = jax.ShapeDtypeStruct((N_TOK, W), BF16)
    c_spec = pl.BlockSpec((T, W), lambda b: (rb0 + b, 0))
    aliases = {}
    if latent:
        in_specs += [s_spec, any_spec]
        args += [S0, c_out]
        aliases = {8: 0}
        out_shape, out_specs = c_shape, c_spec
    else:
        out_shape = (c_shape, jax.ShapeDtypeStruct((B, DEPTH, 2, HG_HEADS, HG_DK, HG_DV), F32))
        out_specs = (c_spec, s_spec)
        if prev is not None:
            in_specs.append(any_spec)
            args.append(prev)
            aliases = {7: 1}
    scratch = [pltpu.VMEM((T, W), F32), pltpu.VMEM((T, W), F32), pltpu.VMEM((2 * HG_HEADS, HG_DV, HG_DK), F32),
               pltpu.VMEM((T // HG_CHUNK, HG_HEADS, HG_DV, HG_CHUNK), BF16),
               pltpu.VMEM((2, T, W), F32), pltpu.VMEM((2, T, W), F32)]
    return pl.pallas_call(
        functools.partial(_hgrn_kernel, T=T, latent=latent),
        out_shape=out_shape, grid=(B,), in_specs=in_specs, out_specs=out_specs, scratch_shapes=scratch,
        input_output_aliases=aliases,
        compiler_params=_cparams("arbitrary"),
        name="hgrn_lat" if latent else "hgrn_ctx",
    )(*args)


def _merge_kernel(a_ref, b_ref, c_ref, ga_ref, gb_ref, gc_ref, xc_ref, xl_ref, g1_ref, sh2_ref, sc2_ref,
                  wb_ref, wo_ref, lng_ref, lnb_ref, wr_ref, x1_ref, h2_ref, *, tm):
    def br(v_ref, g_ref, k):
        return jax.nn.sigmoid(g_ref[...].astype(F32)) * jnp.dot(v_ref[...], wb_ref[k], preferred_element_type=F32)

    mix = br(a_ref, ga_ref, 0) + br(b_ref, gb_ref, 1) + br(c_ref, gc_ref, 2)
    y = jnp.dot(mix.astype(BF16), wo_ref[...], preferred_element_type=F32)
    x = _pair_read(pl.program_id(0), tm, xc_ref, xl_ref)
    x1 = _layer_norm(DEEPNORM_ALPHA * x + g1_ref[...] * y, lng_ref[...], lnb_ref[...])
    x1_ref[...] = x1
    h2 = x1 * (1.0 + sc2_ref[...]) + sh2_ref[...]
    h2_ref[:, :D_MODEL] = h2.astype(h2_ref.dtype)
    lt = lax.dot_general(wr_ref[...], h2, _NT, preferred_element_type=F32, precision=HI)
    r = lax.broadcasted_iota(jnp.int32, lt.shape, 0)
    neg = -jnp.inf
    lg = jnp.where(r < N_GROUPS, lt, neg)
    mg = jnp.max(lg, 0, keepdims=True)
    g_sel = jnp.min(jnp.where(lg == mg, r, ROUTER_ROWS), 0, keepdims=True)
    p_sel = 1.0 / jnp.sum(jnp.where(r < N_GROUPS, jnp.exp(lg - mg), 0.0), 0, keepdims=True)
    lo = ROUTER_E0 + EXP_PER_GROUP * g_sel
    le = jnp.where((r >= lo) & (r < lo + EXP_PER_GROUP), lt, neg)
    v1 = jnp.max(le, 0, keepdims=True)
    i1 = jnp.min(jnp.where(le == v1, r, ROUTER_ROWS), 0, keepdims=True)
    le2 = jnp.where(r == i1, neg, le)
    v2 = jnp.max(le2, 0, keepdims=True)
    i2 = jnp.min(jnp.where(le2 == v2, r, ROUTER_ROWS), 0, keepdims=True)
    e2 = jnp.exp(v2 - v1)
    w1 = p_sel / (1.0 + e2)
    w2 = p_sel * e2 / (1.0 + e2)
    w1_hi = w1.astype(BF16).astype(F32)
    w2_hi = w2.astype(BF16).astype(F32)
    j1, j2 = i1 - lo, i2 - lo
    packed = jnp.where(r == j1, w1_hi, jnp.where(r == j2, w2_hi, jnp.where(
        r == j1 + EXP_PER_GROUP, w1 - w1_hi, jnp.where(r == j2 + EXP_PER_GROUP, w2 - w2_hi, jnp.where(
            r == 2 * EXP_PER_GROUP, g_sel.astype(F32), 0.0)))))
    packed = jnp.concatenate([packed, jnp.zeros((128 - ROUTER_ROWS, packed.shape[1]), F32)], 0)
    h2_ref[:, D_MODEL:] = packed.T.astype(h2_ref.dtype)


def _merge(a, b, c, p16, xc, xl, mod, wb, wo, lng, lnb, wr, l):
    tm = 512
    tok = lambda i: (i, 0)
    ln_spec = pl.BlockSpec((None, None, 1, D_MODEL), lambda i: (l, 0, 0, 0))
    return pl.pallas_call(
        functools.partial(_merge_kernel, tm=tm),
        out_shape=(jax.ShapeDtypeStruct((N_TOK, D_MODEL), F32), jax.ShapeDtypeStruct((N_TOK, MOE_XW), MOE_XDT)),
        grid=(N_TOK // tm,),
        in_specs=[pl.BlockSpec((tm, BRANCH_W), tok), pl.BlockSpec((tm, BRANCH_W), tok), pl.BlockSpec((tm, BRANCH_W), tok),
                  pl.BlockSpec((tm, D_MODEL), lambda i: (i, 5)), pl.BlockSpec((tm, D_MODEL), lambda i: (i, 6)),
                  pl.BlockSpec((tm, D_MODEL), lambda i: (i, 7)),
                  *_pair_specs(tm), _mod_spec(l, 2, tm), _mod_spec(l, 3, tm), _mod_spec(l, 4, tm),
                  pl.BlockSpec((None, 3, BRANCH_W, D_MODEL), lambda i: (l, 0, 0, 0)),
                  pl.BlockSpec((None, D_MODEL, D_MODEL), lambda i: (l, 0, 0)),
                  ln_spec, ln_spec,
                  pl.BlockSpec((None, ROUTER_ROWS, D_MODEL), lambda i: (l, 0, 0))],
        out_specs=(pl.BlockSpec((tm, D_MODEL), tok), pl.BlockSpec((tm, MOE_XW), tok)),
        compiler_params=_cparams("arbitrary"),
        name="merge",
    )(a, b, c, p16, p16, p16, xc, xl, mod, mod, mod, wb, wo, lng, lnb, wr)


def _moe_up_kernel(gid_ref, nused_ref, x_ref, w1_ref, w3_ref, hid_ref, w1b, w3b):
    f = pl.program_id(0)
    t = pl.program_id(1)

    @pl.when(t < nused_ref[0])
    def _():
        first = jnp.logical_or(t == 0, gid_ref[t] != gid_ref[jnp.maximum(t - 1, 0)])

        @pl.when(first)
        def _():
            w1b[...] = w1_ref[...].astype(BF16)
            w3b[...] = w3_ref[...].astype(BF16)

        x = x_ref[:, :D_MODEL].astype(BF16)
        rec = x_ref[:, D_MODEL:].astype(F32)
        lane = lax.broadcasted_iota(jnp.int32, rec.shape, 1)
        for j in range(MOE_FE):
            e = f * MOE_FE + j
            a = jnp.dot(x, w1b[j], preferred_element_type=F32)
            b = jnp.dot(x, w3b[j], preferred_element_type=F32)
            gcol = jnp.sum(jnp.where(jnp.logical_or(lane == e, lane == e + EXP_PER_GROUP), rec, 0.0), -1, keepdims=True)
            hid_ref[:, j * D_EXPERT:(j + 1) * D_EXPERT] = (_silu(a) * b * gcol).astype(BF16)


def _moe_tile(t, n):
    return jnp.minimum(t, n[0] - 1)


def _moe_up(gid, nused, xs, w1, w3, l):
    tm = MOE_TM
    npad = MOE_NT * tm
    nf = EXP_PER_GROUP // MOE_FE
    w_spec = pl.BlockSpec((None, MOE_FE, D_MODEL, D_EXPERT),
                          lambda f, t, g, n: (l, nf * g[_moe_tile(t, n)] + f, 0, 0))
    grid_spec = pltpu.PrefetchScalarGridSpec(
        num_scalar_prefetch=2,
        grid=(nf, MOE_NT),
        in_specs=[pl.BlockSpec((tm, MOE_XW), lambda f, t, g, n: (_moe_tile(t, n), 0)), w_spec, w_spec],
        out_specs=pl.BlockSpec((tm, MOE_FE * D_EXPERT), lambda f, t, g, n: (_moe_tile(t, n), f)),
        scratch_shapes=[pltpu.VMEM((MOE_FE, D_MODEL, D_EXPERT), BF16), pltpu.VMEM((MOE_FE, D_MODEL, D_EXPERT), BF16)],
    )
    return pl.pallas_call(
        _moe_up_kernel,
        out_shape=jax.ShapeDtypeStruct((npad, EXP_PER_GROUP * D_EXPERT), BF16),
        grid_spec=grid_spec,
        compiler_params=_cparams("arbitrary", "arbitrary"),
        name="moe_up",
    )(gid, nused, xs, w1, w3)


def _moe_down_kernel(gid_ref, nused_ref, hid_ref, w2_ref, y_ref, w2b):
    t = pl.program_id(0)

    @pl.when(t < nused_ref[0])
    def _():
        first = jnp.logical_or(t == 0, gid_ref[t] != gid_ref[jnp.maximum(t - 1, 0)])

        @pl.when(first)
        def _():
            w2b[...] = w2_ref[...].astype(BF16)

        y_ref[...] = jnp.dot(hid_ref[...], w2b[...], preferred_element_type=F32)


def _moe_down(gid, nused, hid, w2g, l):
    tm = MOE_TM
    npad = MOE_NT * tm
    hw = EXP_PER_GROUP * D_EXPERT
    grid_spec = pltpu.PrefetchScalarGridSpec(
        num_scalar_prefetch=2,
        grid=(MOE_NT,),
        in_specs=[pl.BlockSpec((tm, hw), lambda t, g, n: (_moe_tile(t, n), 0)),
                  pl.BlockSpec((None, None, hw, D_MODEL), lambda t, g, n: (l, g[_moe_tile(t, n)], 0, 0))],
        out_specs=pl.BlockSpec((tm, D_MODEL), lambda t, g, n: (_moe_tile(t, n), 0)),
        scratch_shapes=[pltpu.VMEM((hw, D_MODEL), BF16)],
    )
    return pl.pallas_call(
        _moe_down_kernel,
        out_shape=jax.ShapeDtypeStruct((npad, D_MODEL), F32),
        grid_spec=grid_spec,
        compiler_params=_cparams("arbitrary"),
        name="moe_down",
    )(gid, nused, hid, w2g)


def _moe(h2x, w1, w3, w2g, l):
    tm = MOE_TM
    npad = MOE_NT * tm
    g = h2x[:, D_MODEL + 2 * EXP_PER_GROUP].astype(jnp.int32)
    onehot = (g[:, None] == jnp.arange(N_GROUPS)[None, :]).astype(jnp.int32)
    counts = jnp.sum(onehot, 0)
    rank = jnp.sum((jnp.cumsum(onehot, 0) - onehot) * onehot, 1)
    padded = (counts + tm - 1) // tm * tm
    ends = jnp.cumsum(padded)
    offs = ends - padded
    dest = offs[g] + rank
    src = (jnp.arange(npad, dtype=jnp.int32) % N_TOK).at[dest].set(jnp.arange(N_TOK, dtype=jnp.int32),
                                                                   unique_indices=True)
    starts = jnp.arange(MOE_NT, dtype=jnp.int32) * tm
    tile_gid = jnp.minimum(jnp.sum((ends[None, :] <= starts[:, None]).astype(jnp.int32), 1), N_GROUPS - 1)
    nused = (ends[-1:] // tm).astype(jnp.int32)
    take = lambda arr, idx: arr.at[idx].get(mode="promise_in_bounds", unique_indices=False)
    hid = _moe_up(tile_gid, nused, take(h2x, src), w1, w3, l)
    ys = _moe_down(tile_gid, nused, hid, w2g, l)
    return take(ys, dest)


def _final_kernel(*refs, tm, with_h):
    x1_ref, y_ref, g2_ref, lng_ref, lnb_ref = refs[:5]
    x2 = _layer_norm(DEEPNORM_ALPHA * x1_ref[...] + g2_ref[...] * y_ref[...], lng_ref[...], lnb_ref[...])
    i = pl.program_id(0)
    if with_h:
        sh_ref, sc_ref, xc_ref, xl_ref, h_ref = refs[5:]
        h_ref[...] = (x2 * (1.0 + sc_ref[...]) + sh_ref[...]).astype(BF16)
    else:
        xc_ref, xl_ref = refs[5:]

    @pl.when(i < N_CTX // tm)
    def _():
        xc_ref[...] = x2

    @pl.when(i >= N_CTX // tm)
    def _():
        xl_ref[...] = x2


def _final(x1, y, mod, lng, lnb, l):
    tm = 1024
    tok = lambda i: (i, 0)
    with_h = l + 1 < DEPTH
    ln_spec = pl.BlockSpec((None, None, 1, D_MODEL), lambda i: (l, 1, 0, 0))
    half = jax.ShapeDtypeStruct((N_CTX, D_MODEL), F32)
    in_specs = [pl.BlockSpec((tm, D_MODEL), tok), pl.BlockSpec((tm, D_MODEL), tok), _mod_spec(l, 5, tm), ln_spec, ln_spec]
    args = [x1, y, mod, lng, lnb]
    out_shape = [half, half]
    out_specs = list(_pair_specs(tm))
    if with_h:
        in_specs += [_mod_spec(l + 1, 0, tm), _mod_spec(l + 1, 1, tm)]
        args += [mod, mod]
        out_shape.append(jax.ShapeDtypeStruct((N_TOK, D_MODEL), BF16))
        out_specs.append(pl.BlockSpec((tm, D_MODEL), tok))
    return pl.pallas_call(
        functools.partial(_final_kernel, tm=tm, with_h=with_h),
        out_shape=tuple(out_shape), grid=(N_TOK // tm,), in_specs=in_specs, out_specs=tuple(out_specs),
        compiler_params=_cparams("arbitrary"),
        name="final",
    )(*args)


def kernel(x_prompt, x_sample, c, cache_na_k, cache_na_v, state_mlstm_C, state_mlstm_n, state_mlstm_m, state_hgrn,
           c_ctx, w_mod, b_mod, w_in, b_in, mlstm_fbias, hgrn_lb_logits, na_rpb, w_branch, w_out, ln_g, ln_b,
           w_rg, w_re, w_e1, w_e3, w_e2):
    assert N_CTX == N_LAT
    lb_cum = jnp.cumsum(jax.nn.softmax(hgrn_lb_logits.astype(F32), axis=1), axis=1)
    lb_all = lb_cum - lb_cum[:, :1]

    cs = jnp.zeros((N_MODROWS, D_MODEL), F32).at[0].set(c_ctx).at[1:1 + DEC_BATCH].set(c)
    mod = _modulation(cs, w_mod, b_mod).reshape(DEPTH, N_MODROWS, 6, 1, D_MODEL)

    wb = w_branch.astype(BF16)
    wo = w_out.astype(BF16)
    lng = ln_g.reshape(DEPTH, 2, 1, D_MODEL)
    lnb = ln_b.reshape(DEPTH, 2, 1, D_MODEL)
    wr = jnp.zeros((DEPTH, ROUTER_ROWS, D_MODEL), F32)
    wr = wr.at[:, :N_GROUPS].set(jnp.swapaxes(w_rg, 1, 2)).at[:, ROUTER_E0:ROUTER_E0 + N_EXPERTS].set(jnp.swapaxes(w_re, 1, 2))
    w2g = w_e2.reshape(DEPTH, N_GROUPS, EXP_PER_GROUP * D_EXPERT, D_MODEL)
    b_main = jnp.concatenate([b_in[:, :GATE_COL0], b_in[:, GATE_COL0 + N_GATES:]], 1)

    xc = x_prompt.reshape(N_CTX, D_MODEL)
    xl = x_sample.reshape(N_LAT, D_MODEL)
    w_t = jnp.swapaxes(w_in, 1, 2)
    na_bias = jax.vmap(_na_bias_table)(na_rpb)
    h = _prep(xc, xl, mod)

    kv = (None, None)
    ma_states = None
    hg_state = None
    for l in range(DEPTH):
        p16, p32 = _inproj(h, w_t, b_main[l][None, :], l)
        gcol, gt = _gates(h, w_t, b_in.reshape(DEPTH, 1, N_IN), l)
        gt3 = gt.reshape(N_GATES, N_TOK // MA_CHUNK, MA_CHUNK).transpose(1, 0, 2)

        a, *ma_states = _mlstm(p16, gcol, gt3, mlstm_fbias[l], l, False, prev=ma_states)
        a = _mlstm(p16, gcol, gt3, mlstm_fbias[l], l, True, state_mlstm_C, state_mlstm_n, state_mlstm_m, a_out=a)
        b, *kv = _ctx_attention(p16, l, *kv)
        b = _lat_attention(p16, cache_na_k, cache_na_v, na_bias, l, b)
        cc, hg_state = _hgrn(p16, p32, lb_all[:, l], l, False, prev=hg_state)
        cc = _hgrn(p16, p32, lb_all[:, l], l, True, state_hgrn, c_out=cc)

        x1, h2x = _merge(a, b, cc, p16, xc, xl, mod, wb, wo, lng, lnb, wr, l)
        y2 = _moe(h2x, w_e1, w_e3, w2g, l)
        outs = _final(x1, y2, mod, lng, lnb, l)
        xc, xl = outs[0], outs[1]
        if l + 1 < DEPTH:
            h = outs[2]

    dt = x_prompt.dtype
    new_C, new_n, new_m = ma_states
    new_n = new_n.reshape(BATCH, DEPTH, 2, MA_HEADS, MA_DK)
    new_m = new_m[:, :, :, 0].reshape(BATCH, DEPTH, 2, MA_HEADS)
    return (xc.reshape(BATCH, SEQ, D_MODEL), xl.reshape(DEC_BATCH, DEC_SEQ, D_MODEL), kv[0], kv[1],
            new_C.astype(dt), new_n.astype(dt), new_m.astype(dt), hg_state.astype(dt))
```

```python
import functools

import numpy as np
import jax
import jax.numpy as jnp
from jax import lax
from jax.experimental import pallas as pl
from jax.experimental.pallas import tpu as pltpu

F32 = jnp.float32
BF16 = jnp.bfloat16
HI = lax.Precision.HIGHEST

D_MODEL = 1024
BATCH = 16
SEQ = 256
DEPTH = 2
DEC_BATCH = 4
DEC_SEQ = 1024
PAST_LEN = 256
GRID_W = 64
MA_HEADS = 4
MA_DK = 128
MA_DV = 128
MA_CHUNK = 64
NA_HEADS = 8
NA_DH = 64
NA_KR_MAX = 8
NA_KC = 16
HG_HEADS = 4
HG_DK = 128
HG_DV = 128
HG_CHUNK = 32
BRANCH_W = 512
N_GROUPS = 4
EXP_PER_GROUP = 4
N_EXPERTS = N_GROUPS * EXP_PER_GROUP
D_EXPERT = 512
ROPE_BASE = 10000.0
LN_EPS = 1e-5
RMS_EPS = 1e-6
DEEPNORM_ALPHA = (2 * DEPTH) ** 0.25

N_CTX = BATCH * SEQ
N_LAT = DEC_BATCH * DEC_SEQ
N_TOK = N_CTX + N_LAT
N_MODROWS = 8
GATE_COL0 = 4 * BRANCH_W
N_GATES = 4 * MA_HEADS
N_IN = 9232
P_COLS = N_IN - N_GATES
MOE_TM = 512
MOE_NT = N_TOK // MOE_TM + N_GROUPS
MOE_FE = 4
MOE_XW = D_MODEL + 128
MOE_XDT = F32
ROUTER_ROWS = 32
ROUTER_E0 = 8
VMEM_LIMIT = 56 * 1024 * 1024

_NT = (((1,), (1,)), ((), ()))
_TN = (((0,), (0,)), ((), ()))


def _cparams(*sem):
    return pltpu.CompilerParams(dimension_semantics=sem, vmem_limit_bytes=VMEM_LIMIT)


def _mod_row(tile, tm):
    return jnp.maximum((tile * tm) // DEC_SEQ - (N_CTX // DEC_SEQ - 1), 0)


def _mod_spec(l, part, tm):
    return pl.BlockSpec((None, None, None, 1, D_MODEL), lambda i: (l, _mod_row(i, tm), part, 0, 0))


def _pair_specs(tm):
    nc = N_CTX // tm
    return (pl.BlockSpec((tm, D_MODEL), lambda i: (jnp.minimum(i, nc - 1), 0)),
            pl.BlockSpec((tm, D_MODEL), lambda i: (jnp.maximum(i - nc, 0), 0)))


def _pair_read(i, tm, c_ref, l_ref):
    return jnp.where(i < N_CTX // tm, c_ref[...], l_ref[...])


def _silu(x):
    return x * jax.nn.sigmoid(x)


def _layer_norm(x, g, b):
    mu = jnp.mean(x, -1, keepdims=True)
    xc = x - mu
    var = jnp.mean(xc * xc, -1, keepdims=True)
    return xc * lax.rsqrt(var + LN_EPS) * g + b


def _log_sigmoid(x):
    return jnp.minimum(x, 0.0) - jnp.log(1.0 + jnp.exp(-jnp.abs(x)))


def _tri(n, upper):
    r = lax.broadcasted_iota(jnp.int32, (n, n), 0)
    c = lax.broadcasted_iota(jnp.int32, (n, n), 1)
    return jnp.where((r <= c) if upper else (r >= c), 1.0, 0.0).astype(F32)


def _dot3(a, b, split_b):
    x = b if split_b else a
    hi = x.astype(BF16)
    r = x - hi.astype(F32)
    mid = r.astype(BF16)
    lo = (r - mid.astype(F32)).astype(BF16)
    one = (a if split_b else b).astype(BF16)
    dot = (lambda p: jnp.dot(one, p, preferred_element_type=F32)) if split_b else (
        lambda p: jnp.dot(p, one, preferred_element_type=F32))
    return dot(hi) + dot(mid) + dot(lo)


def _mod_kernel(c_ref, w_ref, b_ref, o_ref):
    s = _silu(c_ref[...])
    o_ref[...] = jnp.dot(s.astype(BF16), w_ref[...].astype(BF16), preferred_element_type=F32) + b_ref[...]


def _modulation(cs, w_mod, b_mod):
    tn = 1024
    return pl.pallas_call(
        _mod_kernel,
        out_shape=jax.ShapeDtypeStruct((DEPTH, N_MODROWS, 6 * D_MODEL), F32),
        grid=(DEPTH, 6 * D_MODEL // tn),
        in_specs=[pl.BlockSpec((N_MODROWS, D_MODEL), lambda l, j: (0, 0)),
                  pl.BlockSpec((None, D_MODEL, tn), lambda l, j: (l, 0, j)),
                  pl.BlockSpec((None, 1, tn), lambda l, j: (l, 0, j))],
        out_specs=pl.BlockSpec((None, N_MODROWS, tn), lambda l, j: (l, 0, j)),
        compiler_params=_cparams("arbitrary", "arbitrary"),
        name="modulation",
    )(cs, w_mod, b_mod.reshape(DEPTH, 1, 6 * D_MODEL))


def _prep_kernel(xc_ref, xl_ref, sh_ref, sc_ref, h_ref, *, tm):
    x = _pair_read(pl.program_id(0), tm, xc_ref, xl_ref)
    h_ref[...] = (x * (1.0 + sc_ref[...]) + sh_ref[...]).astype(BF16)


def _prep(xc, xl, mod):
    tm = 1024
    return pl.pallas_call(
        functools.partial(_prep_kernel, tm=tm),
        out_shape=jax.ShapeDtypeStruct((N_TOK, D_MODEL), BF16),
        grid=(N_TOK // tm,),
        in_specs=[*_pair_specs(tm), _mod_spec(0, 0, tm), _mod_spec(0, 1, tm)],
        out_specs=pl.BlockSpec((tm, D_MODEL), lambda i: (i, 0)),
        compiler_params=_cparams("arbitrary"),
        name="prep",
    )(xc, xl, mod, mod)


INPROJ_TN = 512
N_PLAIN_TILES = GATE_COL0 // INPROJ_TN


F32_TILE0 = 7
N_F32_TILES = 2
P16_COLS = P_COLS - N_F32_TILES * INPROJ_TN


def _inproj_kernel(h_ref, wa_ref, wb_ref, b_ref, o_ref, *, src_tile):
    j = src_tile(pl.program_id(1))

    @pl.when(j < N_PLAIN_TILES)
    def _():
        o_ref[...] = (lax.dot_general(h_ref[...], wa_ref[...].astype(BF16), _NT, preferred_element_type=F32)
                      + b_ref[...]).astype(o_ref.dtype)

    @pl.when(j >= N_PLAIN_TILES)
    def _():
        w = jnp.concatenate([wa_ref[N_GATES:, :], wb_ref[...]], 0)
        o_ref[...] = (lax.dot_general(h_ref[...], w.astype(BF16), _NT, preferred_element_type=F32)
                      + b_ref[...]).astype(o_ref.dtype)


def _inproj_call(h, w_t, b_main, l, tm, n_tiles, src_tile, dtype, name):
    tn = INPROJ_TN
    return pl.pallas_call(
        functools.partial(_inproj_kernel, src_tile=src_tile),
        out_shape=jax.ShapeDtypeStruct((N_TOK, n_tiles * tn), dtype),
        grid=(N_TOK // tm, n_tiles),
        in_specs=[pl.BlockSpec((tm, D_MODEL), lambda i, j: (i, 0)),
                  pl.BlockSpec((None, tn, D_MODEL), lambda i, j: (l, src_tile(j), 0)),
                  pl.BlockSpec((None, N_GATES, D_MODEL), lambda i, j: (l, (src_tile(j) + 1) * (tn // N_GATES), 0)),
                  pl.BlockSpec((1, tn), lambda i, j: (0, src_tile(j)))],
        out_specs=pl.BlockSpec((tm, tn), lambda i, j: (i, j)),
        compiler_params=_cparams("arbitrary", "arbitrary"),
        name=name,
    )(h, w_t, w_t, b_main)


def _inproj(h, w_t, b_main, l):
    skip_f32 = lambda j: jnp.where(j < F32_TILE0, j, j + N_F32_TILES)
    p16 = _inproj_call(h, w_t, b_main, l, 4096, P16_COLS // INPROJ_TN, skip_f32, BF16, "inproj")
    p32 = _inproj_call(h, w_t, b_main, l, 4096, N_F32_TILES, lambda j: j + F32_TILE0, F32, "inproj_f32")
    return p16, p32


def _gates_kernel(h_ref, w_ref, b_ref, gc_ref, gt_ref):
    g = lax.dot_general(h_ref[...], w_ref[...].astype(BF16), _NT, preferred_element_type=F32) + b_ref[...]
    gc_ref[...] = g
    gt_ref[...] = g.T[:N_GATES]


def _gates(h, w_t, b_in3, l):
    tm = 1024
    gblk = GATE_COL0 // 128
    return pl.pallas_call(
        _gates_kernel,
        out_shape=(jax.ShapeDtypeStruct((N_TOK, 128), F32), jax.ShapeDtypeStruct((N_GATES, N_TOK), F32)),
        grid=(N_TOK // tm,),
        in_specs=[pl.BlockSpec((tm, D_MODEL), lambda i: (i, 0)),
                  pl.BlockSpec((None, 128, D_MODEL), lambda i: (l, gblk, 0)),
                  pl.BlockSpec((None, 1, 128), lambda i: (l, 0, gblk))],
        out_specs=(pl.BlockSpec((tm, 128), lambda i: (i, 0)), pl.BlockSpec((N_GATES, tm), lambda i: (0, i))),
        compiler_params=_cparams("arbitrary"),
        name="gates",
    )(h, w_t, b_in3)


HEADS_PER_BLK = 128 // NA_DH
NA_NBLK = NA_HEADS // HEADS_PER_BLK
NA_QSCALE = NA_DH ** -0.5
Q_COL, K_COL, V_COL = 16, 20, 24
QKV_COL = 4


def _ctx_attn_kernel(*refs):
    q_ref, k_ref, v_ref = refs[:3]
    o_ref, ko_ref, vo_ref = refs[-3:]
    heads = range(NA_HEADS)
    split = lambda x: jnp.stack([x[:, h * NA_DH:(h + 1) * NA_DH] for h in heads], 0)
    q = split(q_ref[...] * NA_QSCALE)
    k = split(k_ref[...])
    v = split(v_ref[...])
    ko_ref[...] = k.astype(F32)
    vo_ref[...] = v.astype(F32)
    s = lax.dot_general(q, k, (((2,), (2,)), ((0,), (0,))), preferred_element_type=F32)
    e = jnp.exp(s - jnp.max(s, -1, keepdims=True))
    p = e * (1.0 / jnp.sum(e, -1, keepdims=True))
    o = lax.dot_general(p.astype(BF16), v, (((2,), (1,)), ((0,), (0,))), preferred_element_type=F32)
    o_ref[...] = jnp.concatenate([o[h] for h in heads], -1).astype(BF16)


def _ctx_attention(p16, l, prev_k=None, prev_v=None):
    kv_shape = jax.ShapeDtypeStruct((BATCH, DEPTH, NA_HEADS, SEQ, NA_DH), F32)
    kv_spec = pl.BlockSpec((None, None, NA_HEADS, SEQ, NA_DH), lambda b: (b, l, 0, 0, 0))
    col = lambda j: pl.BlockSpec((SEQ, BRANCH_W), lambda b: (b, j))
    in_specs = [col(QKV_COL), col(QKV_COL + 1), col(QKV_COL + 2)]
    args = [p16, p16, p16]
    aliases = {}
    if prev_k is not None:
        in_specs += [pl.BlockSpec(memory_space=pl.ANY)] * 2
        args += [prev_k, prev_v]
        aliases = {3: 1, 4: 2}
    return pl.pallas_call(
        _ctx_attn_kernel,
        out_shape=(jax.ShapeDtypeStruct((N_TOK, BRANCH_W), BF16), kv_shape, kv_shape),
        grid=(BATCH,),
        in_specs=in_specs,
        out_specs=(pl.BlockSpec((SEQ, BRANCH_W), lambda b: (b, 0)), kv_spec, kv_spec),
        input_output_aliases=aliases,
        compiler_params=_cparams("arbitrary"),
        name="ctx_attention",
    )(*args)


NA_ROWS = DEC_SEQ // GRID_W
NA_KR = min(NA_KR_MAX, NA_ROWS)
NA_QROWS = 4
NA_QT = NA_ROWS // NA_QROWS
NA_WROWS = NA_KR + NA_QROWS - 1
NA_WKEYS = NA_WROWS * GRID_W


def _na_window_start(t):
    return min(max(t * NA_QROWS - NA_KR // 2, 0), NA_ROWS - NA_WROWS)


def _na_bias_table(rpb):
    c = np.arange(GRID_W)
    c0 = np.clip(c - NA_KC // 2, 0, GRID_W - NA_KC)
    kc = np.arange(GRID_W)
    valid = (kc[None, :] >= c0[:, None]) & (kc[None, :] < c0[:, None] + NA_KC)
    dc = kc[None, :] - c[:, None] + NA_KC - 1
    onehot = (dc[None] == np.arange(2 * NA_KC - 1)[:, None, None]) & valid[None]
    toep = jnp.einsum('hrd,dcx->hrcx', rpb.astype(F32), jnp.asarray(onehot, F32), precision=HI)
    toep = jnp.where(valid[None, None], toep, -jnp.inf)
    ninf = jnp.full((NA_HEADS, GRID_W, GRID_W), -jnp.inf, F32)
    tiles = []
    for t in range(NA_QT):
        w0 = _na_window_start(t)
        qrows = []
        for r in range(t * NA_QROWS, (t + 1) * NA_QROWS):
            r0 = min(max(r - NA_KR // 2, 0), NA_ROWS - NA_KR)
            blocks = []
            for kr in range(w0, w0 + NA_WROWS):
                inside = r0 <= kr < r0 + NA_KR
                blocks.append(toep[:, kr - r + NA_KR_MAX - 1] if inside else ninf)
            qrows.append(jnp.concatenate(blocks, -1))
        tiles.append(jnp.concatenate(qrows, 1))
    return jnp.stack(tiles, 1).astype(BF16)


def _lat_attn_kernel(q_ref, k_ref, v_ref, ck_ref, cv_ref, bias_ref, prev_ref, o_ref):
    heads = range(HEADS_PER_BLK)
    split = lambda x: jnp.stack([x[:, h * NA_DH:(h + 1) * NA_DH] for h in heads], 0)
    bnt = (((2,), (2,)), ((0,), (0,)))
    bnn = (((2,), (1,)), ((0,), (0,)))
    q = (q_ref[...] * NA_QSCALE).astype(BF16)
    k = k_ref[...].astype(BF16)
    v = v_ref[...].astype(BF16)
    ck = ck_ref[...].astype(BF16)
    cv = cv_ref[...].astype(BF16)
    nq = NA_QROWS * GRID_W
    for t in range(NA_QT):
        w0 = _na_window_start(t)
        qs = slice(t * nq, (t + 1) * nq)
        ws = slice(w0 * GRID_W, (w0 + NA_WROWS) * GRID_W)
        qh = split(q[qs])
        s_loc = lax.dot_general(qh, split(k[ws]), bnt, preferred_element_type=F32) + bias_ref[:, t]
        s_ctx = lax.dot_general(qh, ck, bnt, preferred_element_type=F32)
        m = jnp.maximum(jnp.max(s_loc, -1, keepdims=True), jnp.max(s_ctx, -1, keepdims=True))
        e_loc = jnp.exp(s_loc - m)
        e_ctx = jnp.exp(s_ctx - m)
        inv = 1.0 / (jnp.sum(e_loc, -1, keepdims=True) + jnp.sum(e_ctx, -1, keepdims=True))
        acc = (lax.dot_general(e_loc.astype(BF16), split(v[ws]), bnn, preferred_element_type=F32)
               + lax.dot_general(e_ctx.astype(BF16), cv, bnn, preferred_element_type=F32)) * inv
        o_ref[qs, :] = jnp.concatenate([acc[h] for h in heads], -1).astype(BF16)


def _lat_attention(proj, ck, cv, bias, l, b_out):
    rb0 = N_CTX // DEC_SEQ
    cb = lambda base: (lambda j, b: (rb0 + b, base + j))
    c_spec = pl.BlockSpec((None, None, HEADS_PER_BLK, PAST_LEN, NA_DH), lambda j, b: (b, l, j, 0, 0))
    return pl.pallas_call(
        _lat_attn_kernel,
        out_shape=jax.ShapeDtypeStruct((N_TOK, BRANCH_W), BF16),
        grid=(NA_NBLK, DEC_BATCH),
        in_specs=[pl.BlockSpec((DEC_SEQ, 128), cb(Q_COL)), pl.BlockSpec((DEC_SEQ, 128), cb(K_COL)),
                  pl.BlockSpec((DEC_SEQ, 128), cb(V_COL)), c_spec, c_spec,
                  pl.BlockSpec((None, HEADS_PER_BLK, NA_QT, NA_QROWS * GRID_W, NA_WKEYS), lambda j, b: (l, j, 0, 0, 0)),
                  pl.BlockSpec(memory_space=pl.ANY)],
        out_specs=pl.BlockSpec((DEC_SEQ, 128), lambda j, b: (rb0 + b, j)),
        input_output_aliases={6: 0},
        compiler_params=_cparams("arbitrary", "arbitrary"),
        name="lat_attention",
    )(proj, proj, proj, ck, cv, bias, b_out)


MA_KSCALE = MA_DK ** -0.5


def _rope_tables(T):
    t = np.arange(T)
    half = MA_DK // 2
    inv = ROPE_BASE ** (-jnp.arange(0, half, 2, dtype=F32) / half)
    ang_r = jnp.asarray((t // GRID_W).astype(np.float32))[:, None] * inv[None, :]
    ang_c = jnp.asarray((t % GRID_W).astype(np.float32))[:, None] * inv[None, :]
    cos = jnp.concatenate([jnp.cos(ang_r)] * 2 + [jnp.cos(ang_c)] * 2, -1)
    sin = jnp.concatenate([-jnp.sin(ang_r), jnp.sin(ang_r), -jnp.sin(ang_c), jnp.sin(ang_c)], -1)
    return cos, sin


def _mlstm_kernel(*refs, T, latent):
    if latent:
        (p_ref, gc_ref, gt_ref, fbc_ref, fbr_ref, cos_ref, sin_ref, c0_ref, n0_ref, m0_ref, prev_ref,
         a_ref, qs, ks, vT, hfT, hbT, CT, ns, ms, brs, kcs) = refs
    else:
        p_ref, gc_ref, gt_ref, fbc_ref, fbr_ref = refs[:5]
        a_ref, co_ref, no_ref, mo_ref, qs, ks, vT, hfT, hbT, CT, ns, ms, brs, kcs = refs[-14:]
    L = MA_CHUNK
    NC = T // L
    W = BRANCH_W
    PER = 128 // L

    lane = lax.broadcasted_iota(jnp.int32, (T, MA_DK), 1)
    lo_half = (lane % (MA_DK // 2)) < (MA_DK // 4)

    def rope(x):
        if not latent:
            return x
        swapped = jnp.where(lo_half, pltpu.roll(x, MA_DK - MA_DK // 4, 1), pltpu.roll(x, MA_DK // 4, 1))
        return x * cos_ref[...] + swapped * sin_ref[...]

    for h in range(MA_HEADS):
        hs = slice(h * MA_DK, (h + 1) * MA_DK)
        qs[:, hs] = rope(p_ref[:, hs].astype(F32)).astype(BF16)
        ks[:, hs] = rope(p_ref[:, W + h * MA_DK:W + (h + 1) * MA_DK].astype(F32) * MA_KSCALE).astype(BF16)

    def v_block(tb, carry):
        r0 = pl.multiple_of(tb * 128, 128)
        for h in range(MA_HEADS):
            hs = slice(h * MA_DV, (h + 1) * MA_DV)
            blk = p_ref[pl.ds(r0, 128), 2 * W + h * MA_DV:2 * W + (h + 1) * MA_DV].astype(F32).T.astype(BF16)
            for j in range(PER):
                vT[tb * PER + j, h] = blk[:, j * L:(j + 1) * L]
        return carry

    lax.fori_loop(0, T // 128, v_block, 0)

    for d in range(2):
        for h in range(MA_HEADS):
            sidx = d * MA_HEADS + h
            CT[sidx] = c0_ref[d, h].T if latent else jnp.zeros((MA_DV, MA_DK), F32)
            ns[sidx] = n0_ref[sidx:sidx + 1, :] if latent else jnp.zeros((1, MA_DK), F32)
            ms[sidx] = m0_ref[sidx:sidx + 1, :] if latent else jnp.zeros((1, 128), F32)

    low = _tri(L, False)
    upp = _tri(L, True)
    rr = lax.broadcasted_iota(jnp.int32, (L, L), 0)
    cc = lax.broadcasted_iota(jnp.int32, (L, L), 1)
    tri_cols = jnp.concatenate([upp, low], 1)
    tri_rows = jnp.concatenate([low, upp], 0)
    fbc = fbc_ref[...]
    fbr = fbr_ref[...]

    def gate_sums(c, carry):
        t0 = pl.multiple_of(c * L, L)
        gc = gc_ref[pl.ds(t0, L), :]
        lfc = _log_sigmoid(gc + fbc)
        lfr = _log_sigmoid(gt_ref[c] + fbr)
        ish = pltpu.roll(gc, MA_HEADS, 1)
        brow = _dot3(lfr, tri_cols, False)
        bcol = _dot3(tri_rows, lfc, True)
        brs[0, c] = brow[:, :L]
        brs[1, c] = brow[:, L:]
        kcs[0, pl.ds(t0, L), :] = ish - bcol[:L]
        kcs[1, pl.ds(t0, L), :] = ish - bcol[L:]
        return carry

    lax.fori_loop(0, NC, gate_sums, 0, unroll=4)

    def chunk_pair(cf, cb):
        H = MA_HEADS
        G = 2 * H
        heads = range(H)
        cs, ts = (cf, cb), (pl.multiple_of(cf * L, L), pl.multiple_of(cb * L, L))
        rows_of = lambda x, r0: [x[r0 + h:r0 + h + 1, :] for h in heads]
        cols_of = lambda x, c0: [x[:, c0 + h:c0 + h + 1] for h in heads]
        br = jnp.stack(sum([rows_of(brs[d, cs[d]], 2 * d * H + H) for d in range(2)], []), 0)
        ir = jnp.stack(sum([rows_of(gt_ref[cs[d]], 2 * d * H) for d in range(2)], []), 0)
        kcol = jnp.stack(sum([cols_of(kcs[d, pl.ds(ts[d], L), :], 2 * d * H + H) for d in range(2)], []), 0)
        split = lambda x: [x[:, h * MA_DK:(h + 1) * MA_DK] for h in heads]
        q = jnp.stack(split(qs[pl.ds(ts[0], L), :]) + split(qs[pl.ds(ts[1], L), :]), 0)
        k = jnp.stack(split(ks[pl.ds(ts[0], L), :]) + split(ks[pl.ds(ts[1], L), :]), 0)
        vt = jnp.concatenate([vT[cf], vT[cb]], 0)
        m = ms[...][:, :, 0:1]
        n = ns[...]
        ct = CT[...]
        bnt = (((2,), (2,)), ((0,), (0,)))
        bnn = (((2,), (1,)), ((0,), (0,)))
        pre = br + kcol
        dmat = jnp.concatenate([jnp.where(rr <= cc, pre[:H], -jnp.inf), jnp.where(rr >= cc, pre[H:], -jnp.inf)], 0)
        g = br + m
        m_t = jnp.maximum(g, jnp.max(dmat, 1, keepdims=True))
        w_inter = jnp.exp(g - m_t)
        s = lax.dot_general(k, q, bnt, preferred_element_type=F32) * jnp.exp(dmat - m_t)
        ctn = jnp.concatenate([ct.astype(BF16), jnp.broadcast_to(n, (G, 8, MA_DK)).astype(BF16)], 1)
        cq = lax.dot_general(ctn, q, bnt, preferred_element_type=F32)
        num = w_inter * cq[:, :MA_DV] + lax.dot_general(vt, s.astype(BF16), bnn, preferred_element_type=F32)
        den = w_inter * cq[:, MA_DV:MA_DV + 1] + jnp.sum(s, 1, keepdims=True)
        hout = num / jnp.maximum(jnp.abs(den), jnp.exp(-m_t))
        hfT[cf] = hout[:H]
        hbT[cb] = hout[H:]
        last = lambda x: jnp.concatenate([x[:H, :, L - 1:L], x[H:, :, 0:1]], 0)
        m_new = last(m_t)
        b_last = last(br)
        decay = jnp.exp(b_last + m - m_new)
        wk = jnp.exp(b_last - br + ir - m_new)
        wk_hi = wk.astype(BF16)
        wk_lo = (wk - wk_hi.astype(F32)).astype(BF16)
        lhs = jnp.concatenate([(vt.astype(F32) * wk).astype(BF16), wk_hi, wk_lo, jnp.zeros((G, 6, L), BF16)], 1)
        upd = lax.dot_general(lhs, k, bnn, preferred_element_type=F32)
        CT[...] = decay * ct + upd[:, :MA_DV]
        ns[...] = decay * n + upd[:, MA_DV:MA_DV + 1] + upd[:, MA_DV + 1:MA_DV + 2]
        ms[...] = jnp.broadcast_to(m_new, (G, 1, 128))

    def body(i, carry):
        chunk_pair(i, NC - 1 - i)
        return carry

    lax.fori_loop(0, NC, body, 0, unroll=4)

    def out_block(tb, carry):
        r0 = pl.multiple_of(tb * 128, 128)
        hsum = jnp.concatenate([hfT[tb * PER + j] + hbT[tb * PER + j] for j in range(PER)], 2)
        outs = [hsum[h].T for h in range(MA_HEADS)]
        gate = jax.nn.sigmoid(p_ref[pl.ds(r0, 128), 3 * W:4 * W].astype(F32))
        a_ref[pl.ds(r0, 128), :] = (gate * jnp.concatenate(outs, 1)).astype(BF16)
        return carry

    lax.fori_loop(0, T // 128, out_block, 0)
    if not latent:
        for d in range(2):
            for h in range(MA_HEADS):
                sidx = d * MA_HEADS + h
                co_ref[d, h] = CT[sidx].T
                no_ref[sidx:sidx + 1, :] = ns[sidx]
                mo_ref[sidx:sidx + 1, :] = ms[sidx]


def _mlstm(proj, gcol, gt3, fbias_l, l, latent, C0=None, n0=None, m0=None, a_out=None, prev=None):
    T = DEC_SEQ if latent else SEQ
    B = DEC_BATCH if latent else BATCH
    rb0 = N_CTX // DEC_SEQ if latent else 0
    fb = fbias_l.astype(F32)
    fbc = jnp.zeros((1, 128), F32).at[0, MA_HEADS:2 * MA_HEADS].set(fb[0]).at[0, 3 * MA_HEADS:4 * MA_HEADS].set(fb[1])
    fbr = fbc[0, :N_GATES].reshape(N_GATES, 1)
    full2 = lambda b: (0, 0)
    any_spec = pl.BlockSpec(memory_space=pl.ANY)
    in_specs = [pl.BlockSpec((T, 4 * BRANCH_W), lambda b: (rb0 + b, 0)),
                pl.BlockSpec((T, 128), lambda b: (rb0 + b, 0)),
                pl.BlockSpec((T // MA_CHUNK, N_GATES, MA_CHUNK), lambda b: (rb0 + b, 0, 0)),
                pl.BlockSpec((1, 128), full2), pl.BlockSpec((N_GATES, 1), full2)]
    args = [proj, gcol, gt3, fbc, fbr]
    a_shape = jax.ShapeDtypeStruct((N_TOK, BRANCH_W), BF16)
    a_spec = pl.BlockSpec((T, BRANCH_W), lambda b: (rb0 + b, 0))
    c_spec = pl.BlockSpec((None, None, 2, MA_HEADS, MA_DK, MA_DV), lambda b: (b, l, 0, 0, 0, 0))
    nm_spec = pl.BlockSpec((None, None, 2 * MA_HEADS, 128), lambda b: (b, l, 0, 0))
    aliases = {}
    if latent:
        cos, sin = _rope_tables(T)
        nb = 2 * MA_HEADS
        in_specs += [pl.BlockSpec((T, MA_DK), full2), pl.BlockSpec((T, MA_DK), full2), c_spec, nm_spec, nm_spec, any_spec]
        args += [cos, sin, C0, n0.reshape(B, DEPTH, nb, MA_DK),
                 jnp.broadcast_to(m0.reshape(B, DEPTH, nb, 1), (B, DEPTH, nb, 128)), a_out]
        aliases = {len(args) - 1: 0}
        out_shape, out_specs = a_shape, a_spec
    else:
        nm_shape = jax.ShapeDtypeStruct((B, DEPTH, 2 * MA_HEADS, 128), F32)
        out_shape = (a_shape, jax.ShapeDtypeStruct((B, DEPTH, 2, MA_HEADS, MA_DK, MA_DV), F32), nm_shape, nm_shape)
        out_specs = (a_spec, c_spec, nm_spec, nm_spec)
        if prev is not None:
            in_specs += [any_spec] * 3
            args += list(prev)
            aliases = {len(args) - 3: 1, len(args) - 2: 2, len(args) - 1: 3}
    nc = T // MA_CHUNK
    scratch = [pltpu.VMEM((T, BRANCH_W), BF16), pltpu.VMEM((T, BRANCH_W), BF16),
               pltpu.VMEM((nc, MA_HEADS, MA_DV, MA_CHUNK), BF16),
               pltpu.VMEM((nc, MA_HEADS, MA_DV, MA_CHUNK), F32), pltpu.VMEM((nc, MA_HEADS, MA_DV, MA_CHUNK), F32),
               pltpu.VMEM((2 * MA_HEADS, MA_DV, MA_DK), F32), pltpu.VMEM((2 * MA_HEADS, 1, MA_DK), F32),
               pltpu.VMEM((2 * MA_HEADS, 1, 128), F32),
               pltpu.VMEM((2, nc, N_GATES, MA_CHUNK), F32), pltpu.VMEM((2, T, 128), F32)]
    return pl.pallas_call(
        functools.partial(_mlstm_kernel, T=T, latent=latent),
        out_shape=out_shape, grid=(B,), in_specs=in_specs, out_specs=out_specs, scratch_shapes=scratch,
        input_output_aliases=aliases,
        compiler_params=_cparams("arbitrary"),
        name="mlstm_lat" if latent else "mlstm_ctx",
    )(*args)


HG_SUB = 8


def _hgrn_kernel(*refs, T, latent):
    ff_ref, fb_ref, q_ref, i_ref, g_ref, lbf_ref, lbb_ref = refs[:7]
    if latent:
        s0_ref = refs[7]
        c_ref, of, ob, ST, iT, As, Bs = refs[-7:]
    else:
        c_ref, so_ref, of, ob, ST, iT, As, Bs = refs[-8:]
    L = HG_CHUNK
    NC = T // L
    NB = L // HG_SUB
    DK = HG_DK

    for d in range(2):
        for h in range(HG_HEADS):
            ST[d * HG_HEADS + h] = s0_ref[d, h].T if latent else jnp.zeros((HG_DV, DK), F32)

    PER = 128 // L

    def i_block(tb, carry):
        r0 = pl.multiple_of(tb * 128, 128)
        for h in range(HG_HEADS):
            blk = i_ref[pl.ds(r0, 128), h * HG_DV:(h + 1) * HG_DV].astype(F32).T.astype(BF16)
            for j in range(PER):
                iT[tb * PER + j, h] = blk[:, j * L:(j + 1) * L]
        return carry

    lax.fori_loop(0, T // 128, i_block, 0)

    low = _tri(L, False)
    upp = _tri(L, True)
    row8 = lax.broadcasted_iota(jnp.int32, (HG_SUB, L), 0)
    lane_s = lax.broadcasted_iota(jnp.int32, (HG_SUB, L), 1)
    heads = range(HG_HEADS)
    bnt = (((2,), (2,)), ((0,), (0,)))
    bnn = (((2,), (1,)), ((0,), (0,)))
    LOG2E = 1.4426950408889634

    def split(x):
        return jnp.stack([x[:, h * DK:(h + 1) * DK] for h in heads], 0)

    def decay_sums(c, carry):
        t0 = pl.multiple_of(c * L, L)
        for d in range(2):
            fpre = (ff_ref if d == 0 else fb_ref)[pl.ds(t0, L), :]
            lb = (lbf_ref if d == 0 else lbb_ref)[...]
            f = lb + (1.0 - lb) * jax.nn.sigmoid(fpre)
            a = _dot3(low if d == 0 else upp, jnp.log(f) * LOG2E, True)
            As[d, pl.ds(t0, L), :] = a
            Bs[d, pl.ds(t0, L), :] = a - jnp.log(1.0 - f) * LOG2E
        return carry

    lax.fori_loop(0, NC, decay_sums, 0, unroll=4)

    def chunk_pair(cf, cb):
        tf = pl.multiple_of(cf * L, L)
        tb = pl.multiple_of(cb * L, L)
        both = lambda fn: jnp.concatenate([fn(0, tf), fn(1, tb)], 0)
        A = both(lambda d, t: split(As[d, pl.ds(t, L), :]))
        B = both(lambda d, t: split(Bs[d, pl.ds(t, L), :]))
        q = both(lambda d, t: split(_silu(q_ref[pl.ds(t, L), :].astype(F32))))
        iv = both(lambda d, t: split(i_ref[pl.ds(t, L), :].astype(BF16)))
        ivT = jnp.concatenate([iT[cf], iT[cb]], 0)
        H = HG_HEADS
        st = ST[...]
        o = lax.dot_general((q * jnp.exp2(A)).astype(BF16), st.astype(BF16), bnt, preferred_element_type=F32)
        a_last = jnp.concatenate([A[:H, L - 1:L], A[H:, 0:1]], 0)
        kd = jnp.exp2(a_last - B).astype(BF16)
        rows = []
        for I in range(NB):
            lo, hi = I * HG_SUB, (I + 1) * HG_SUB
            A_I, q_I = A[:, lo:hi], q[:, lo:hi]
            att_f = jnp.zeros((H, HG_SUB, L), F32)
            att_b = jnp.zeros((H, HG_SUB, L), F32)
            for j in range(HG_SUB):
                s = lo + j
                col = jnp.sum(q_I * jnp.exp2(A_I - B[:, s:s + 1]), -1, keepdims=True)
                att_f = jnp.where((lane_s == s) & (row8 >= j), col[:H], att_f)
                att_b = jnp.where((lane_s == s) & (row8 <= j), col[H:], att_b)
            rf, rb = max(lo - 1, 0), min(hi, L - 1)
            R = jnp.concatenate([A[:H, rf:rf + 1], A[H:, rb:rb + 1]], 0)
            zeros = lambda n: jnp.zeros((H, n, DK), BF16)
            ksc_f = jnp.concatenate([jnp.exp2(R[:H] - B[:H, :lo]).astype(BF16), zeros(L - lo)], 1) if I > 0 else zeros(L)
            ksc_b = jnp.concatenate([zeros(hi), jnp.exp2(R[H:] - B[H:, hi:]).astype(BF16)], 1) if I < NB - 1 else zeros(L)
            ksc = jnp.concatenate([ksc_f, ksc_b], 0)
            off = lax.dot_general((q_I * jnp.exp2(A_I - R)).astype(BF16), ksc, bnt, preferred_element_type=F32)
            rows.append(jnp.concatenate([att_f, att_b], 0) + off)
        att = jnp.concatenate(rows, 1)
        o = o + lax.dot_general(att.astype(BF16), iv, bnn, preferred_element_type=F32)
        for h in heads:
            of[pl.ds(tf, L), h * HG_DV:(h + 1) * HG_DV] = o[h]
            ob[pl.ds(tb, L), h * HG_DV:(h + 1) * HG_DV] = o[H + h]
        ST[...] = st * jnp.exp2(a_last) + lax.dot_general(ivT, kd, bnn, preferred_element_type=F32)

    def body(i, carry):
        chunk_pair(i, NC - 1 - i)
        return carry

    lax.fori_loop(0, NC, body, 0, unroll=8)

    def epilogue(r, carry):
        t0 = pl.multiple_of(r * 128, 128)
        o = of[pl.ds(t0, 128), :] + ob[pl.ds(t0, 128), :]
        gsil = _silu(g_ref[pl.ds(t0, 128), :].astype(F32))
        outs = []
        for h in range(HG_HEADS):
            oh = o[:, h * HG_DV:(h + 1) * HG_DV]
            outs.append(oh * lax.rsqrt(jnp.mean(oh * oh, -1, keepdims=True) + RMS_EPS))
        c_ref[pl.ds(t0, 128), :] = (jnp.concatenate(outs, -1) * gsil).astype(BF16)
        return carry

    lax.fori_loop(0, T // 128, epilogue, 0)
    if not latent:
        for d in range(2):
            for h in range(HG_HEADS):
                so_ref[d, h] = ST[d * HG_HEADS + h].T


def _hgrn(p16, p32, lb_l, l, latent, S0=None, c_out=None, prev=None):
    T = DEC_SEQ if latent else SEQ
    B = DEC_BATCH if latent else BATCH
    rb0 = N_CTX // DEC_SEQ if latent else 0
    W = BRANCH_W
    full2 = lambda b: (0, 0)
    any_spec = pl.BlockSpec(memory_space=pl.ANY)
    col = lambda j: pl.BlockSpec((T, W), lambda b: (rb0 + b, j))
    s_spec = pl.BlockSpec((None, None, 2, HG_HEADS, HG_DK, HG_DV), lambda b: (b, l, 0, 0, 0, 0))
    in_specs = [col(0), col(1), col(7), col(8), col(9), pl.BlockSpec((1, W), full2), pl.BlockSpec((1, W), full2)]
    args = [p32, p32, p16, p16, p16, lb_l[0][None, :], lb_l[1][None, :]]
    c_shape = jax.ShapeDtypeStruct((N_TOK, W), BF16)
    c_spec = pl.BlockSpec((T, W), lambda b: (rb0 + b, 0))
    aliases = {}
    if latent:
        in_specs += [s_spec, any_spec]
        args += [S0, c_out]
        aliases = {8: 0}
        out_shape, out_specs = c_shape, c_spec
    else:
        out_shape = (c_shape, jax.ShapeDtypeStruct((B, DEPTH, 2, HG_HEADS, HG_DK, HG_DV), F32))
        out_specs = (c_spec, s_spec)
        if prev is not None:
            in_specs.append(any_spec)
            args.append(prev)
            aliases = {7: 1}
    scratch = [pltpu.VMEM((T, W), F32), pltpu.VMEM((T, W), F32), pltpu.VMEM((2 * HG_HEADS, HG_DV, HG_DK), F32),
               pltpu.VMEM((T // HG_CHUNK, HG_HEADS, HG_DV, HG_CHUNK), BF16),
               pltpu.VMEM((2, T, W), F32), pltpu.VMEM((2, T, W), F32)]
    return pl.pallas_call(
        functools.partial(_hgrn_kernel, T=T, latent=latent),
        out_shape=out_shape, grid=(B,), in_specs=in_specs, out_specs=out_specs, scratch_shapes=scratch,
        input_output_aliases=aliases,
        compiler_params=_cparams("arbitrary"),
        name="hgrn_lat" if latent else "hgrn_ctx",
    )(*args)


def _merge_kernel(a_ref, b_ref, c_ref, ga_ref, gb_ref, gc_ref, xc_ref, xl_ref, g1_ref, sh2_ref, sc2_ref,
                  wb_ref, wo_ref, lng_ref, lnb_ref, wr_ref, x1_ref, h2_ref, *, tm):
    def br(v_ref, g_ref, k):
        return jax.nn.sigmoid(g_ref[...].astype(F32)) * jnp.dot(v_ref[...], wb_ref[k], preferred_element_type=F32)

    mix = br(a_ref, ga_ref, 0) + br(b_ref, gb_ref, 1) + br(c_ref, gc_ref, 2)
    y = jnp.dot(mix.astype(BF16), wo_ref[...], preferred_element_type=F32)
    x = _pair_read(pl.program_id(0), tm, xc_ref, xl_ref)
    x1 = _layer_norm(DEEPNORM_ALPHA * x + g1_ref[...] * y, lng_ref[...], lnb_ref[...])
    x1_ref[...] = x1
    h2 = x1 * (1.0 + sc2_ref[...]) + sh2_ref[...]
    h2_ref[:, :D_MODEL] = h2.astype(h2_ref.dtype)
    lt = lax.dot_general(wr_ref[...], h2, _NT, preferred_element_type=F32, precision=HI)
    r = lax.broadcasted_iota(jnp.int32, lt.shape, 0)
    neg = -jnp.inf
    lg = jnp.where(r < N_GROUPS, lt, neg)
    mg = jnp.max(lg, 0, keepdims=True)
    g_sel = jnp.min(jnp.where(lg == mg, r, ROUTER_ROWS), 0, keepdims=True)
    p_sel = 1.0 / jnp.sum(jnp.where(r < N_GROUPS, jnp.exp(lg - mg), 0.0), 0, keepdims=True)
    lo = ROUTER_E0 + EXP_PER_GROUP * g_sel
    le = jnp.where((r >= lo) & (r < lo + EXP_PER_GROUP), lt, neg)
    v1 = jnp.max(le, 0, keepdims=True)
    i1 = jnp.min(jnp.where(le == v1, r, ROUTER_ROWS), 0, keepdims=True)
    le2 = jnp.where(r == i1, neg, le)
    v2 = jnp.max(le2, 0, keepdims=True)
    i2 = jnp.min(jnp.where(le2 == v2, r, ROUTER_ROWS), 0, keepdims=True)
    e2 = jnp.exp(v2 - v1)
    w1 = p_sel / (1.0 + e2)
    w2 = p_sel * e2 / (1.0 + e2)
    w1_hi = w1.astype(BF16).astype(F32)
    w2_hi = w2.astype(BF16).astype(F32)
    j1, j2 = i1 - lo, i2 - lo
    packed = jnp.where(r == j1, w1_hi, jnp.where(r == j2, w2_hi, jnp.where(
        r == j1 + EXP_PER_GROUP, w1 - w1_hi, jnp.where(r == j2 + EXP_PER_GROUP, w2 - w2_hi, jnp.where(
            r == 2 * EXP_PER_GROUP, g_sel.astype(F32), 0.0)))))
    packed = jnp.concatenate([packed, jnp.zeros((128 - ROUTER_ROWS, packed.shape[1]), F32)], 0)
    h2_ref[:, D_MODEL:] = packed.T.astype(h2_ref.dtype)


def _merge(a, b, c, p16, xc, xl, mod, wb, wo, lng, lnb, wr, l):
    tm = 512
    tok = lambda i: (i, 0)
    ln_spec = pl.BlockSpec((None, None, 1, D_MODEL), lambda i: (l, 0, 0, 0))
    return pl.pallas_call(
        functools.partial(_merge_kernel, tm=tm),
        out_shape=(jax.ShapeDtypeStruct((N_TOK, D_MODEL), F32), jax.ShapeDtypeStruct((N_TOK, MOE_XW), MOE_XDT)),
        grid=(N_TOK // tm,),
        in_specs=[pl.BlockSpec((tm, BRANCH_W), tok), pl.BlockSpec((tm, BRANCH_W), tok), pl.BlockSpec((tm, BRANCH_W), tok),
                  pl.BlockSpec((tm, D_MODEL), lambda i: (i, 5)), pl.BlockSpec((tm, D_MODEL), lambda i: (i, 6)),
                  pl.BlockSpec((tm, D_MODEL), lambda i: (i, 7)),
                  *_pair_specs(tm), _mod_spec(l, 2, tm), _mod_spec(l, 3, tm), _mod_spec(l, 4, tm),
                  pl.BlockSpec((None, 3, BRANCH_W, D_MODEL), lambda i: (l, 0, 0, 0)),
                  pl.BlockSpec((None, D_MODEL, D_MODEL), lambda i: (l, 0, 0)),
                  ln_spec, ln_spec,
                  pl.BlockSpec((None, ROUTER_ROWS, D_MODEL), lambda i: (l, 0, 0))],
        out_specs=(pl.BlockSpec((tm, D_MODEL), tok), pl.BlockSpec((tm, MOE_XW), tok)),
        compiler_params=_cparams("arbitrary"),
        name="merge",
    )(a, b, c, p16, p16, p16, xc, xl, mod, mod, mod, wb, wo, lng, lnb, wr)


def _moe_up_kernel(gid_ref, nused_ref, x_ref, w1_ref, w3_ref, hid_ref, w1b, w3b):
    f = pl.program_id(0)
    t = pl.program_id(1)

    @pl.when(t < nused_ref[0])
    def _():
        first = jnp.logical_or(t == 0, gid_ref[t] != gid_ref[jnp.maximum(t - 1, 0)])

        @pl.when(first)
        def _():
            w1b[...] = w1_ref[...].astype(BF16)
            w3b[...] = w3_ref[...].astype(BF16)

        x = x_ref[:, :D_MODEL].astype(BF16)
        rec = x_ref[:, D_MODEL:].astype(F32)
        lane = lax.broadcasted_iota(jnp.int32, rec.shape, 1)
        for j in range(MOE_FE):
            e = f * MOE_FE + j
            a = jnp.dot(x, w1b[j], preferred_element_type=F32)
            b = jnp.dot(x, w3b[j], preferred_element_type=F32)
            gcol = jnp.sum(jnp.where(jnp.logical_or(lane == e, lane == e + EXP_PER_GROUP), rec, 0.0), -1, keepdims=True)
            hid_ref[:, j * D_EXPERT:(j + 1) * D_EXPERT] = (_silu(a) * b * gcol).astype(BF16)


def _moe_tile(t, n):
    return jnp.minimum(t, n[0] - 1)


def _moe_up(gid, nused, xs, w1, w3, l):
    tm = MOE_TM
    npad = MOE_NT * tm
    nf = EXP_PER_GROUP // MOE_FE
    w_spec = pl.BlockSpec((None, MOE_FE, D_MODEL, D_EXPERT),
                          lambda f, t, g, n: (l, nf * g[_moe_tile(t, n)] + f, 0, 0))
    grid_spec = pltpu.PrefetchScalarGridSpec(
        num_scalar_prefetch=2,
        grid=(nf, MOE_NT),
        in_specs=[pl.BlockSpec((tm, MOE_XW), lambda f, t, g, n: (_moe_tile(t, n), 0)), w_spec, w_spec],
        out_specs=pl.BlockSpec((tm, MOE_FE * D_EXPERT), lambda f, t, g, n: (_moe_tile(t, n), f)),
        scratch_shapes=[pltpu.VMEM((MOE_FE, D_MODEL, D_EXPERT), BF16), pltpu.VMEM((MOE_FE, D_MODEL, D_EXPERT), BF16)],
    )
    return pl.pallas_call(
        _moe_up_kernel,
        out_shape=jax.ShapeDtypeStruct((npad, EXP_PER_GROUP * D_EXPERT), BF16),
        grid_spec=grid_spec,
        compiler_params=_cparams("arbitrary", "arbitrary"),
        name="moe_up",
    )(gid, nused, xs, w1, w3)


def _moe_down_kernel(gid_ref, nused_ref, hid_ref, w2_ref, y_ref, w2b):
    t = pl.program_id(0)

    @pl.when(t < nused_ref[0])
    def _():
        first = jnp.logical_or(t == 0, gid_ref[t] != gid_ref[jnp.maximum(t - 1, 0)])

        @pl.when(first)
        def _():
            w2b[...] = w2_ref[...].astype(BF16)

        y_ref[...] = jnp.dot(hid_ref[...], w2b[...], preferred_element_type=F32)


def _moe_down(gid, nused, hid, w2g, l):
    tm = MOE_TM
    npad = MOE_NT * tm
    hw = EXP_PER_GROUP * D_EXPERT
    grid_spec = pltpu.PrefetchScalarGridSpec(
        num_scalar_prefetch=2,
        grid=(MOE_NT,),
        in_specs=[pl.BlockSpec((tm, hw), lambda t, g, n: (_moe_tile(t, n), 0)),
                  pl.BlockSpec((None, None, hw, D_MODEL), lambda t, g, n: (l, g[_moe_tile(t, n)], 0, 0))],
        out_specs=pl.BlockSpec((tm, D_MODEL), lambda t, g, n: (_moe_tile(t, n), 0)),
        scratch_shapes=[pltpu.VMEM((hw, D_MODEL), BF16)],
    )
    return pl.pallas_call(
        _moe_down_kernel,
        out_shape=jax.ShapeDtypeStruct((npad, D_MODEL), F32),
        grid_spec=grid_spec,
        compiler_params=_cparams("arbitrary"),
        name="moe_down",
    )(gid, nused, hid, w2g)


def _moe(h2x, w1, w3, w2g, l):
    tm = MOE_TM
    npad = MOE_NT * tm
    g = h2x[:, D_MODEL + 2 * EXP_PER_GROUP].astype(jnp.int32)
    onehot = (g[:, None] == jnp.arange(N_GROUPS)[None, :]).astype(jnp.int32)
    counts = jnp.sum(onehot, 0)
    rank = jnp.sum((jnp.cumsum(onehot, 0) - onehot) * onehot, 1)
    padded = (counts + tm - 1) // tm * tm
    ends = jnp.cumsum(padded)
    offs = ends - padded
    dest = offs[g] + rank
    src = (jnp.arange(npad, dtype=jnp.int32) % N_TOK).at[dest].set(jnp.arange(N_TOK, dtype=jnp.int32),
                                                                   unique_indices=True)
    starts = jnp.arange(MOE_NT, dtype=jnp.int32) * tm
    tile_gid = jnp.minimum(jnp.sum((ends[None, :] <= starts[:, None]).astype(jnp.int32), 1), N_GROUPS - 1)
    nused = (ends[-1:] // tm).astype(jnp.int32)
    take = lambda arr, idx: arr.at[idx].get(mode="promise_in_bounds", unique_indices=False)
    hid = _moe_up(tile_gid, nused, take(h2x, src), w1, w3, l)
    ys = _moe_down(tile_gid, nused, hid, w2g, l)
    return take(ys, dest)


def _final_kernel(*refs, tm, with_h):
    x1_ref, y_ref, g2_ref, lng_ref, lnb_ref = refs[:5]
    x2 = _layer_norm(DEEPNORM_ALPHA * x1_ref[...] + g2_ref[...] * y_ref[...], lng_ref[...], lnb_ref[...])
    i = pl.program_id(0)
    if with_h:
        sh_ref, sc_ref, xc_ref, xl_ref, h_ref = refs[5:]
        h_ref[...] = (x2 * (1.0 + sc_ref[...]) + sh_ref[...]).astype(BF16)
    else:
        xc_ref, xl_ref = refs[5:]

    @pl.when(i < N_CTX // tm)
    def _():
        xc_ref[...] = x2

    @pl.when(i >= N_CTX // tm)
    def _():
        xl_ref[...] = x2


def _final(x1, y, mod, lng, lnb, l):
    tm = 1024
    tok = lambda i: (i, 0)
    with_h = l + 1 < DEPTH
    ln_spec = pl.BlockSpec((None, None, 1, D_MODEL), lambda i: (l, 1, 0, 0))
    half = jax.ShapeDtypeStruct((N_CTX, D_MODEL), F32)
    in_specs = [pl.BlockSpec((tm, D_MODEL), tok), pl.BlockSpec((tm, D_MODEL), tok), _mod_spec(l, 5, tm), ln_spec, ln_spec]
    args = [x1, y, mod, lng, lnb]
    out_shape = [half, half]
    out_specs = list(_pair_specs(tm))
    if with_h:
        in_specs += [_mod_spec(l + 1, 0, tm), _mod_spec(l + 1, 1, tm)]
        args += [mod, mod]
        out_shape.append(jax.ShapeDtypeStruct((N_TOK, D_MODEL), BF16))
        out_specs.append(pl.BlockSpec((tm, D_MODEL), tok))
    return pl.pallas_call(
        functools.partial(_final_kernel, tm=tm, with_h=with_h),
        out_shape=tuple(out_shape), grid=(N_TOK // tm,), in_specs=in_specs, out_specs=tuple(out_specs),
        compiler_params=_cparams("arbitrary"),
        name="final",
    )(*args)


def kernel(x_prompt, x_sample, c, cache_na_k, cache_na_v, state_mlstm_C, state_mlstm_n, state_mlstm_m, state_hgrn,
           c_ctx, w_mod, b_mod, w_in, b_in, mlstm_fbias, hgrn_lb_logits, na_rpb, w_branch, w_out, ln_g, ln_b,
           w_rg, w_re, w_e1, w_e3, w_e2):
    assert N_CTX == N_LAT
    lb_cum = jnp.cumsum(jax.nn.softmax(hgrn_lb_logits.astype(F32), axis=1), axis=1)
    lb_all = lb_cum - lb_cum[:, :1]

    cs = jnp.zeros((N_MODROWS, D_MODEL), F32).at[0].set(c_ctx).at[1:1 + DEC_BATCH].set(c)
    mod = _modulation(cs, w_mod, b_mod).reshape(DEPTH, N_MODROWS, 6, 1, D_MODEL)

    wb = w_branch.astype(BF16)
    wo = w_out.astype(BF16)
    lng = ln_g.reshape(DEPTH, 2, 1, D_MODEL)
    lnb = ln_b.reshape(DEPTH, 2, 1, D_MODEL)
    wr = jnp.zeros((DEPTH, ROUTER_ROWS, D_MODEL), F32)
    wr = wr.at[:, :N_GROUPS].set(jnp.swapaxes(w_rg, 1, 2)).at[:, ROUTER_E0:ROUTER_E0 + N_EXPERTS].set(jnp.swapaxes(w_re, 1, 2))
    w2g = w_e2.reshape(DEPTH, N_GROUPS, EXP_PER_GROUP * D_EXPERT, D_MODEL)
    b_main = jnp.concatenate([b_in[:, :GATE_COL0], b_in[:, GATE_COL0 + N_GATES:]], 1)

    xc = x_prompt.reshape(N_CTX, D_MODEL)
    xl = x_sample.reshape(N_LAT, D_MODEL)
    w_t = jnp.swapaxes(w_in, 1, 2)
    na_bias = jax.vmap(_na_bias_table)(na_rpb)
    h = _prep(xc, xl, mod)

    kv = (None, None)
    ma_states = None
    hg_state = None
    for l in range(DEPTH):
        p16, p32 = _inproj(h, w_t, b_main[l][None, :], l)
        gcol, gt = _gates(h, w_t, b_in.reshape(DEPTH, 1, N_IN), l)
        gt3 = gt.reshape(N_GATES, N_TOK // MA_CHUNK, MA_CHUNK).transpose(1, 0, 2)

        a, *ma_states = _mlstm(p16, gcol, gt3, mlstm_fbias[l], l, False, prev=ma_states)
        a = _mlstm(p16, gcol, gt3, mlstm_fbias[l], l, True, state_mlstm_C, state_mlstm_n, state_mlstm_m, a_out=a)
        b, *kv = _ctx_attention(p16, l, *kv)
        b = _lat_attention(p16, cache_na_k, cache_na_v, na_bias, l, b)
        cc, hg_state = _hgrn(p16, p32, lb_all[:, l], l, False, prev=hg_state)
        cc = _hgrn(p16, p32, lb_all[:, l], l, True, state_hgrn, c_out=cc)

        x1, h2x = _merge(a, b, cc, p16, xc, xl, mod, wb, wo, lng, lnb, wr, l)
        y2 = _moe(h2x, w_e1, w_e3, w2g, l)
        outs = _final(x1, y2, mod, lng, lnb, l)
        xc, xl = outs[0], outs[1]
        if l + 1 < DEPTH:
            h = outs[2]

    dt = x_prompt.dtype
    new_C, new_n, new_m = ma_states
    new_n = new_n.reshape(BATCH, DEPTH, 2, MA_HEADS, MA_DK)
    new_m = new_m[:, :, :, 0].reshape(BATCH, DEPTH, 2, MA_HEADS)
    return (xc.reshape(BATCH, SEQ, D_MODEL), xl.reshape(DEC_BATCH, DEC_SEQ, D_MODEL), kv[0], kv[1],
            new_C.astype(dt), new_n.astype(dt), new_m.astype(dt), hg_state.astype(dt))
```

```python
import functools

import numpy as np
import jax
import jax.numpy as jnp
from jax import lax
from jax.experimental import pallas as pl
from jax.experimental.pallas import tpu as pltpu

F32 = jnp.float32
BF16 = jnp.bfloat16
HI = lax.Precision.HIGHEST

D_MODEL = 1024
BATCH = 16
SEQ = 256
DEPTH = 2
DEC_BATCH = 4
DEC_SEQ = 1024
PAST_LEN = 256
GRID_W = 64
MA_HEADS = 4
MA_DK = 128
MA_DV = 128
MA_CHUNK = 64
NA_HEADS = 8
NA_DH = 64
NA_KR_MAX = 8
NA_KC = 16
HG_HEADS = 4
HG_DK = 128
HG_DV = 128
HG_CHUNK = 32
BRANCH_W = 512
N_GROUPS = 4
EXP_PER_GROUP = 4
N_EXPERTS = N_GROUPS * EXP_PER_GROUP
D_EXPERT = 512
ROPE_BASE = 10000.0
LN_EPS = 1e-5
RMS_EPS = 1e-6
DEEPNORM_ALPHA = (2 * DEPTH) ** 0.25

N_CTX = BATCH * SEQ
N_LAT = DEC_BATCH * DEC_SEQ
N_TOK = N_CTX + N_LAT
N_MODROWS = 8
GATE_COL0 = 4 * BRANCH_W
N_GATES = 4 * MA_HEADS
N_IN = 9232
P_COLS = N_IN - N_GATES
MOE_TM = 512
MOE_NT = N_TOK // MOE_TM + N_GROUPS
MOE_FE = 4
MOE_XW = D_MODEL + 128
MOE_XDT = F32
ROUTER_ROWS = 32
ROUTER_E0 = 8
VMEM_LIMIT = 56 * 1024 * 1024

_NT = (((1,), (1,)), ((), ()))


def _cparams(*sem):
    return pltpu.CompilerParams(dimension_semantics=sem, vmem_limit_bytes=VMEM_LIMIT)


def _mod_row(tile, tm):
    return jnp.maximum((tile * tm) // DEC_SEQ - (N_CTX // DEC_SEQ - 1), 0)


def _mod_spec(l, part, tm):
    return pl.BlockSpec((None, None, None, 1, D_MODEL), lambda i: (l, _mod_row(i, tm), part, 0, 0))


def _pair_specs(tm):
    nc = N_CTX // tm
    return (pl.BlockSpec((tm, D_MODEL), lambda i: (jnp.minimum(i, nc - 1), 0)),
            pl.BlockSpec((tm, D_MODEL), lambda i: (jnp.maximum(i - nc, 0), 0)))


def _pair_read(i, tm, c_ref, l_ref):
    return jnp.where(i < N_CTX // tm, c_ref[...], l_ref[...])


def _silu(x):
    return x * jax.nn.sigmoid(x)


def _layer_norm(x, g, b):
    mu = jnp.mean(x, -1, keepdims=True)
    xc = x - mu
    var = jnp.mean(xc * xc, -1, keepdims=True)
    return xc * lax.rsqrt(var + LN_EPS) * g + b


def _log_sigmoid(x):
    return jnp.minimum(x, 0.0) - jnp.log(1.0 + jnp.exp(-jnp.abs(x)))


def _tri(n, upper):
    r = lax.broadcasted_iota(jnp.int32, (n, n), 0)
    c = lax.broadcasted_iota(jnp.int32, (n, n), 1)
    return jnp.where((r <= c) if upper else (r >= c), 1.0, 0.0).astype(F32)


def _dot3(a, b, split_b):
    x = b if split_b else a
    hi = x.astype(BF16)
    r = x - hi.astype(F32)
    mid = r.astype(BF16)
    lo = (r - mid.astype(F32)).astype(BF16)
    one = (a if split_b else b).astype(BF16)
    dot = (lambda p: jnp.dot(one, p, preferred_element_type=F32)) if split_b else (
        lambda p: jnp.dot(p, one, preferred_element_type=F32))
    return dot(hi) + dot(mid) + dot(lo)


def _mod_kernel(c_ref, w_ref, b_ref, o_ref):
    s = _silu(c_ref[...])
    o_ref[...] = jnp.dot(s.astype(BF16), w_ref[...].astype(BF16), preferred_element_type=F32) + b_ref[...]


def _modulation(cs, w_mod, b_mod):
    tn = 1024
    return pl.pallas_call(
        _mod_kernel,
        out_shape=jax.ShapeDtypeStruct((DEPTH, N_MODROWS, 6 * D_MODEL), F32),
        grid=(DEPTH, 6 * D_MODEL // tn),
        in_specs=[pl.BlockSpec((N_MODROWS, D_MODEL), lambda l, j: (0, 0)),
                  pl.BlockSpec((None, D_MODEL, tn), lambda l, j: (l, 0, j)),
                  pl.BlockSpec((None, 1, tn), lambda l, j: (l, 0, j))],
        out_specs=pl.BlockSpec((None, N_MODROWS, tn), lambda l, j: (l, 0, j)),
        compiler_params=_cparams("arbitrary", "arbitrary"),
        name="modulation",
    )(cs, w_mod, b_mod.reshape(DEPTH, 1, 6 * D_MODEL))


def _prep_kernel(xc_ref, xl_ref, sh_ref, sc_ref, h_ref, *, tm):
    x = _pair_read(pl.program_id(0), tm, xc_ref, xl_ref)
    h_ref[...] = (x * (1.0 + sc_ref[...]) + sh_ref[...]).astype(BF16)


def _prep(xc, xl, mod):
    tm = 1024
    return pl.pallas_call(
        functools.partial(_prep_kernel, tm=tm),
        out_shape=jax.ShapeDtypeStruct((N_TOK, D_MODEL), BF16),
        grid=(N_TOK // tm,),
        in_specs=[*_pair_specs(tm), _mod_spec(0, 0, tm), _mod_spec(0, 1, tm)],
        out_specs=pl.BlockSpec((tm, D_MODEL), lambda i: (i, 0)),
        compiler_params=_cparams("arbitrary"),
        name="prep",
    )(xc, xl, mod, mod)


INPROJ_TN = 512
N_PLAIN_TILES = GATE_COL0 // INPROJ_TN


F32_TILE0 = 7
N_F32_TILES = 2
P16_COLS = P_COLS - N_F32_TILES * INPROJ_TN


def _inproj_kernel(h_ref, wa_ref, wb_ref, b_ref, o_ref, *, src_tile):
    j = src_tile(pl.program_id(1))

    @pl.when(j < N_PLAIN_TILES)
    def _():
        o_ref[...] = (lax.dot_general(h_ref[...], wa_ref[...].astype(BF16), _NT, preferred_element_type=F32)
                      + b_ref[...]).astype(o_ref.dtype)

    @pl.when(j >= N_PLAIN_TILES)
    def _():
        w = jnp.concatenate([wa_ref[N_GATES:, :], wb_ref[...]], 0)
        o_ref[...] = (lax.dot_general(h_ref[...], w.astype(BF16), _NT, preferred_element_type=F32)
                      + b_ref[...]).astype(o_ref.dtype)


def _inproj_call(h, w_t, b_main, l, tm, n_tiles, src_tile, dtype, name):
    tn = INPROJ_TN
    return pl.pallas_call(
        functools.partial(_inproj_kernel, src_tile=src_tile),
        out_shape=jax.ShapeDtypeStruct((N_TOK, n_tiles * tn), dtype),
        grid=(N_TOK // tm, n_tiles),
        in_specs=[pl.BlockSpec((tm, D_MODEL), lambda i, j: (i, 0)),
                  pl.BlockSpec((None, tn, D_MODEL), lambda i, j: (l, src_tile(j), 0)),
                  pl.BlockSpec((None, N_GATES, D_MODEL), lambda i, j: (l, (src_tile(j) + 1) * (tn // N_GATES), 0)),
                  pl.BlockSpec((1, tn), lambda i, j: (0, src_tile(j)))],
        out_specs=pl.BlockSpec((tm, tn), lambda i, j: (i, j)),
        compiler_params=_cparams("arbitrary", "arbitrary"),
        name=name,
    )(h, w_t, w_t, b_main)


def _inproj(h, w_t, b_main, l):
    skip_f32 = lambda j: jnp.where(j < F32_TILE0, j, j + N_F32_TILES)
    p16 = _inproj_call(h, w_t, b_main, l, 4096, P16_COLS // INPROJ_TN, skip_f32, BF16, "inproj")
    p32 = _inproj_call(h, w_t, b_main, l, 4096, N_F32_TILES, lambda j: j + F32_TILE0, F32, "inproj_f32")
    return p16, p32


def _gates_kernel(h_ref, w_ref, b_ref, gc_ref, gt_ref):
    g = lax.dot_general(h_ref[...], w_ref[...].astype(BF16), _NT, preferred_element_type=F32) + b_ref[...]
    gc_ref[...] = g
    gt_ref[...] = g.T[:N_GATES]


def _gates(h, w_t, b_in3, l):
    tm = 1024
    gblk = GATE_COL0 // 128
    return pl.pallas_call(
        _gates_kernel,
        out_shape=(jax.ShapeDtypeStruct((N_TOK, 128), F32), jax.ShapeDtypeStruct((N_GATES, N_TOK), F32)),
        grid=(N_TOK // tm,),
        in_specs=[pl.BlockSpec((tm, D_MODEL), lambda i: (i, 0)),
                  pl.BlockSpec((None, 128, D_MODEL), lambda i: (l, gblk, 0)),
                  pl.BlockSpec((None, 1, 128), lambda i: (l, 0, gblk))],
        out_specs=(pl.BlockSpec((tm, 128), lambda i: (i, 0)), pl.BlockSpec((N_GATES, tm), lambda i: (0, i))),
        compiler_params=_cparams("arbitrary"),
        name="gates",
    )(h, w_t, b_in3)


HEADS_PER_BLK = 128 // NA_DH
NA_NBLK = NA_HEADS // HEADS_PER_BLK
NA_QSCALE = NA_DH ** -0.5
Q_COL, K_COL, V_COL = 16, 20, 24
QKV_COL = 4


def _ctx_attn_kernel(*refs):
    q_ref, k_ref, v_ref = refs[:3]
    o_ref, ko_ref, vo_ref = refs[-3:]
    heads = range(NA_HEADS)
    split = lambda x: jnp.stack([x[:, h * NA_DH:(h + 1) * NA_DH] for h in heads], 0)
    q = split(q_ref[...] * NA_QSCALE)
    k = split(k_ref[...])
    v = split(v_ref[...])
    ko_ref[...] = k.astype(F32)
    vo_ref[...] = v.astype(F32)
    s = lax.dot_general(q, k, (((2,), (2,)), ((0,), (0,))), preferred_element_type=F32)
    e = jnp.exp(s - jnp.max(s, -1, keepdims=True))
    p = e * (1.0 / jnp.sum(e, -1, keepdims=True))
    o = lax.dot_general(p.astype(BF16), v, (((2,), (1,)), ((0,), (0,))), preferred_element_type=F32)
    o_ref[...] = jnp.concatenate([o[h] for h in heads], -1).astype(BF16)


def _ctx_attention(p16, l, prev_k=None, prev_v=None):
    kv_shape = jax.ShapeDtypeStruct((BATCH, DEPTH, NA_HEADS, SEQ, NA_DH), F32)
    kv_spec = pl.BlockSpec((None, None, NA_HEADS, SEQ, NA_DH), lambda b: (b, l, 0, 0, 0))
    col = lambda j: pl.BlockSpec((SEQ, BRANCH_W), lambda b: (b, j))
    in_specs = [col(QKV_COL), col(QKV_COL + 1), col(QKV_COL + 2)]
    args = [p16, p16, p16]
    aliases = {}
    if prev_k is not None:
        in_specs += [pl.BlockSpec(memory_space=pl.ANY)] * 2
        args += [prev_k, prev_v]
        aliases = {3: 1, 4: 2}
    return pl.pallas_call(
        _ctx_attn_kernel,
        out_shape=(jax.ShapeDtypeStruct((N_TOK, BRANCH_W), BF16), kv_shape, kv_shape),
        grid=(BATCH,),
        in_specs=in_specs,
        out_specs=(pl.BlockSpec((SEQ, BRANCH_W), lambda b: (b, 0)), kv_spec, kv_spec),
        input_output_aliases=aliases,
        compiler_params=_cparams("arbitrary"),
        name="ctx_attention",
    )(*args)


NA_ROWS = DEC_SEQ // GRID_W
NA_KR = min(NA_KR_MAX, NA_ROWS)
NA_QROWS = 4
NA_QT = NA_ROWS // NA_QROWS
NA_WROWS = NA_KR + NA_QROWS - 1
NA_WKEYS = NA_WROWS * GRID_W


def _na_window_start(t):
    return min(max(t * NA_QROWS - NA_KR // 2, 0), NA_ROWS - NA_WROWS)


def _na_bias_table(rpb):
    c = np.arange(GRID_W)
    c0 = np.clip(c - NA_KC // 2, 0, GRID_W - NA_KC)
    kc = np.arange(GRID_W)
    valid = (kc[None, :] >= c0[:, None]) & (kc[None, :] < c0[:, None] + NA_KC)
    dc = kc[None, :] - c[:, None] + NA_KC - 1
    onehot = (dc[None] == np.arange(2 * NA_KC - 1)[:, None, None]) & valid[None]
    toep = jnp.einsum('hrd,dcx->hrcx', rpb.astype(F32), jnp.asarray(onehot, F32), precision=HI)
    toep = jnp.where(valid[None, None], toep, -jnp.inf)
    ninf = jnp.full((NA_HEADS, GRID_W, GRID_W), -jnp.inf, F32)
    tiles = []
    for t in range(NA_QT):
        w0 = _na_window_start(t)
        qrows = []
        for r in range(t * NA_QROWS, (t + 1) * NA_QROWS):
            r0 = min(max(r - NA_KR // 2, 0), NA_ROWS - NA_KR)
            blocks = []
            for kr in range(w0, w0 + NA_WROWS):
                inside = r0 <= kr < r0 + NA_KR
                blocks.append(toep[:, kr - r + NA_KR_MAX - 1] if inside else ninf)
            qrows.append(jnp.concatenate(blocks, -1))
        tiles.append(jnp.concatenate(qrows, 1))
    return jnp.stack(tiles, 1).astype(BF16)


def _lat_attn_kernel(q_ref, k_ref, v_ref, ck_ref, cv_ref, bias_ref, prev_ref, o_ref):
    heads = range(HEADS_PER_BLK)
    split = lambda x: jnp.stack([x[:, h * NA_DH:(h + 1) * NA_DH] for h in heads], 0)
    bnt = (((2,), (2,)), ((0,), (0,)))
    bnn = (((2,), (1,)), ((0,), (0,)))
    q = (q_ref[...] * NA_QSCALE).astype(BF16)
    k = k_ref[...].astype(BF16)
    v = v_ref[...].astype(BF16)
    ck = ck_ref[...].astype(BF16)
    cv = cv_ref[...].astype(BF16)
    nq = NA_QROWS * GRID_W
    for t in range(NA_QT):
        w0 = _na_window_start(t)
        qs = slice(t * nq, (t + 1) * nq)
        ws = slice(w0 * GRID_W, (w0 + NA_WROWS) * GRID_W)
        qh = split(q[qs])
        s_loc = lax.dot_general(qh, split(k[ws]), bnt, preferred_element_type=F32) + bias_ref[:, t]
        s_ctx = lax.dot_general(qh, ck, bnt, preferred_element_type=F32)
        m = jnp.maximum(jnp.max(s_loc, -1, keepdims=True), jnp.max(s_ctx, -1, keepdims=True))
        e_loc = jnp.exp(s_loc - m)
        e_ctx = jnp.exp(s_ctx - m)
        inv = 1.0 / (jnp.sum(e_loc, -1, keepdims=True) + jnp.sum(e_ctx, -1, keepdims=True))
        acc = (lax.dot_general(e_loc.astype(BF16), split(v[ws]), bnn, preferred_element_type=F32)
               + lax.dot_general(e_ctx.astype(BF16), cv, bnn, preferred_element_type=F32)) * inv
        o_ref[qs, :] = jnp.concatenate([acc[h] for h in heads], -1).astype(BF16)


def _lat_attention(proj, ck, cv, bias, l, b_out):
    rb0 = N_CTX // DEC_SEQ
    cb = lambda base: (lambda j, b: (rb0 + b, base + j))
    c_spec = pl.BlockSpec((None, None, HEADS_PER_BLK, PAST_LEN, NA_DH), lambda j, b: (b, l, j, 0, 0))
    return pl.pallas_call(
        _lat_attn_kernel,
        out_shape=jax.ShapeDtypeStruct((N_TOK, BRANCH_W), BF16),
        grid=(NA_NBLK, DEC_BATCH),
        in_specs=[pl.BlockSpec((DEC_SEQ, 128), cb(Q_COL)), pl.BlockSpec((DEC_SEQ, 128), cb(K_COL)),
                  pl.BlockSpec((DEC_SEQ, 128), cb(V_COL)), c_spec, c_spec,
                  pl.BlockSpec((None, HEADS_PER_BLK, NA_QT, NA_QROWS * GRID_W, NA_WKEYS), lambda j, b: (l, j, 0, 0, 0)),
                  pl.BlockSpec(memory_space=pl.ANY)],
        out_specs=pl.BlockSpec((DEC_SEQ, 128), lambda j, b: (rb0 + b, j)),
        input_output_aliases={6: 0},
        compiler_params=_cparams("arbitrary", "arbitrary"),
        name="lat_attention",
    )(proj, proj, proj, ck, cv, bias, b_out)


MA_KSCALE = MA_DK ** -0.5


def _rope_tables(T):
    t = np.arange(T)
    half = MA_DK // 2
    inv = ROPE_BASE ** (-jnp.arange(0, half, 2, dtype=F32) / half)
    ang_r = jnp.asarray((t // GRID_W).astype(np.float32))[:, None] * inv[None, :]
    ang_c = jnp.asarray((t % GRID_W).astype(np.float32))[:, None] * inv[None, :]
    cos = jnp.concatenate([jnp.cos(ang_r)] * 2 + [jnp.cos(ang_c)] * 2, -1)
    sin = jnp.concatenate([-jnp.sin(ang_r), jnp.sin(ang_r), -jnp.sin(ang_c), jnp.sin(ang_c)], -1)
    return cos, sin


def _mlstm_kernel(*refs, T, latent):
    if latent:
        (p_ref, gc_ref, gt_ref, fbc_ref, fbr_ref, cos_ref, sin_ref, c0_ref, n0_ref, m0_ref, prev_ref,
         a_ref, qs, ks, vT, hfT, hbT, CT, ns, ms, brs, kcs) = refs
    else:
        p_ref, gc_ref, gt_ref, fbc_ref, fbr_ref = refs[:5]
        a_ref, co_ref, no_ref, mo_ref, qs, ks, vT, hfT, hbT, CT, ns, ms, brs, kcs = refs[-14:]
    L = MA_CHUNK
    NC = T // L
    W = BRANCH_W
    PER = 128 // L

    lane = lax.broadcasted_iota(jnp.int32, (T, MA_DK), 1)
    lo_half = (lane % (MA_DK // 2)) < (MA_DK // 4)

    def rope(x):
        if not latent:
            return x
        swapped = jnp.where(lo_half, pltpu.roll(x, MA_DK - MA_DK // 4, 1), pltpu.roll(x, MA_DK // 4, 1))
        return x * cos_ref[...] + swapped * sin_ref[...]

    for h in range(MA_HEADS):
        hs = slice(h * MA_DK, (h + 1) * MA_DK)
        qs[:, hs] = rope(p_ref[:, hs].astype(F32)).astype(BF16)
        ks[:, hs] = rope(p_ref[:, W + h * MA_DK:W + (h + 1) * MA_DK].astype(F32) * MA_KSCALE).astype(BF16)

    def v_block(tb, carry):
        r0 = pl.multiple_of(tb * 128, 128)
        for h in range(MA_HEADS):
            hs = slice(h * MA_DV, (h + 1) * MA_DV)
            blk = p_ref[pl.ds(r0, 128), 2 * W + h * MA_DV:2 * W + (h + 1) * MA_DV].astype(F32).T.astype(BF16)
            for j in range(PER):
                vT[tb * PER + j, h] = blk[:, j * L:(j + 1) * L]
        return carry

    lax.fori_loop(0, T // 128, v_block, 0)

    for d in range(2):
        for h in range(MA_HEADS):
            sidx = d * MA_HEADS + h
            CT[sidx] = c0_ref[d, h].T if latent else jnp.zeros((MA_DV, MA_DK), F32)
            ns[sidx] = n0_ref[sidx:sidx + 1, :] if latent else jnp.zeros((1, MA_DK), F32)
            ms[sidx] = m0_ref[sidx:sidx + 1, :] if latent else jnp.zeros((1, 128), F32)

    low = _tri(L, False)
    upp = _tri(L, True)
    rr = lax.broadcasted_iota(jnp.int32, (L, L), 0)
    cc = lax.broadcasted_iota(jnp.int32, (L, L), 1)
    tri_cols = jnp.concatenate([upp, low], 1)
    tri_rows = jnp.concatenate([low, upp], 0)
    fbc = fbc_ref[...]
    fbr = fbr_ref[...]

    def gate_sums(c, carry):
        t0 = pl.multiple_of(c * L, L)
        gc = gc_ref[pl.ds(t0, L), :]
        lfc = _log_sigmoid(gc + fbc)
        lfr = _log_sigmoid(gt_ref[c] + fbr)
        ish = pltpu.roll(gc, MA_HEADS, 1)
        brow = _dot3(lfr, tri_cols, False)
        bcol = _dot3(tri_rows, lfc, True)
        brs[0, c] = brow[:, :L]
        brs[1, c] = brow[:, L:]
        kcs[0, pl.ds(t0, L), :] = ish - bcol[:L]
        kcs[1, pl.ds(t0, L), :] = ish - bcol[L:]
        return carry

    lax.fori_loop(0, NC, gate_sums, 0, unroll=4)

    def chunk_pair(cf, cb):
        H = MA_HEADS
        G = 2 * H
        heads = range(H)
        cs, ts = (cf, cb), (pl.multiple_of(cf * L, L), pl.multiple_of(cb * L, L))
        rows_of = lambda x, r0: [x[r0 + h:r0 + h + 1, :] for h in heads]
        cols_of = lambda x, c0: [x[:, c0 + h:c0 + h + 1] for h in heads]
        br = jnp.stack(sum([rows_of(brs[d, cs[d]], 2 * d * H + H) for d in range(2)], []), 0)
        ir = jnp.stack(sum([rows_of(gt_ref[cs[d]], 2 * d * H) for d in range(2)], []), 0)
        kcol = jnp.stack(sum([cols_of(kcs[d, pl.ds(ts[d], L), :], 2 * d * H + H) for d in range(2)], []), 0)
        split = lambda x: [x[:, h * MA_DK:(h + 1) * MA_DK] for h in heads]
        q = jnp.stack(split(qs[pl.ds(ts[0], L), :]) + split(qs[pl.ds(ts[1], L), :]), 0)
        k = jnp.stack(split(ks[pl.ds(ts[0], L), :]) + split(ks[pl.ds(ts[1], L), :]), 0)
        vt = jnp.concatenate([vT[cf], vT[cb]], 0)
        m = ms[...][:, :, 0:1]
        n = ns[...]
        ct = CT[...]
        bnt = (((2,), (2,)), ((0,), (0,)))
        bnn = (((2,), (1,)), ((0,), (0,)))
        pre = br + kcol
        dmat = jnp.concatenate([jnp.where(rr <= cc, pre[:H], -jnp.inf), jnp.where(rr >= cc, pre[H:], -jnp.inf)], 0)
        g = br + m
        m_t = jnp.maximum(g, jnp.max(dmat, 1, keepdims=True))
        w_inter = jnp.exp(g - m_t)
        s = lax.dot_general(k, q, bnt, preferred_element_type=F32) * jnp.exp(dmat - m_t)
        ctn = jnp.concatenate([ct.astype(BF16), jnp.broadcast_to(n, (G, 8, MA_DK)).astype(BF16)], 1)
        cq = lax.dot_general(ctn, q, bnt, preferred_element_type=F32)
        num = w_inter * cq[:, :MA_DV] + lax.dot_general(vt, s.astype(BF16), bnn, preferred_element_type=F32)
        den = w_inter * cq[:, MA_DV:MA_DV + 1] + jnp.sum(s, 1, keepdims=True)
        hout = num / jnp.maximum(jnp.abs(den), jnp.exp(-m_t))
        hfT[cf] = hout[:H]
        hbT[cb] = hout[H:]
        last = lambda x: jnp.concatenate([x[:H, :, L - 1:L], x[H:, :, 0:1]], 0)
        m_new = last(m_t)
        b_last = last(br)
        decay = jnp.exp(b_last + m - m_new)
        wk = jnp.exp(b_last - br + ir - m_new)
        wk_hi = wk.astype(BF16)
        wk_lo = (wk - wk_hi.astype(F32)).astype(BF16)
        lhs = jnp.concatenate([(vt.astype(F32) * wk).astype(BF16), wk_hi, wk_lo, jnp.zeros((G, 6, L), BF16)], 1)
        upd = lax.dot_general(lhs, k, bnn, preferred_element_type=F32)
        CT[...] = decay * ct + upd[:, :MA_DV]
        ns[...] = decay * n + upd[:, MA_DV:MA_DV + 1] + upd[:, MA_DV + 1:MA_DV + 2]
        ms[...] = jnp.broadcast_to(m_new, (G, 1, 128))

    def body(i, carry):
        chunk_pair(i, NC - 1 - i)
        return carry

    lax.fori_loop(0, NC, body, 0, unroll=4)

    def out_block(tb, carry):
        r0 = pl.multiple_of(tb * 128, 128)
        hsum = jnp.concatenate([hfT[tb * PER + j] + hbT[tb * PER + j] for j in range(PER)], 2)
        outs = [hsum[h].T for h in range(MA_HEADS)]
        gate = jax.nn.sigmoid(p_ref[pl.ds(r0, 128), 3 * W:4 * W].astype(F32))
        a_ref[pl.ds(r0, 128), :] = (gate * jnp.concatenate(outs, 1)).astype(BF16)
        return carry

    lax.fori_loop(0, T // 128, out_block, 0)
    if not latent:
        for d in range(2):
            for h in range(MA_HEADS):
                sidx = d * MA_HEADS + h
                co_ref[d, h] = CT[sidx].T
                no_ref[sidx:sidx + 1, :] = ns[sidx]
                mo_ref[sidx:sidx + 1, :] = ms[sidx]


def _mlstm(proj, gcol, gt3, fbias_l, l, latent, C0=None, n0=None, m0=None, a_out=None, prev=None):
    T = DEC_SEQ if latent else SEQ
    B = DEC_BATCH if latent else BATCH
    rb0 = N_CTX // DEC_SEQ if latent else 0
    fb = fbias_l.astype(F32)
    fbc = jnp.zeros((1, 128), F32).at[0, MA_HEADS:2 * MA_HEADS].set(fb[0]).at[0, 3 * MA_HEADS:4 * MA_HEADS].set(fb[1])
    fbr = fbc[0, :N_GATES].reshape(N_GATES, 1)
    full2 = lambda b: (0, 0)
    any_spec = pl.BlockSpec(memory_space=pl.ANY)
    in_specs = [pl.BlockSpec((T, 4 * BRANCH_W), lambda b: (rb0 + b, 0)),
                pl.BlockSpec((T, 128), lambda b: (rb0 + b, 0)),
                pl.BlockSpec((T // MA_CHUNK, N_GATES, MA_CHUNK), lambda b: (rb0 + b, 0, 0)),
                pl.BlockSpec((1, 128), full2), pl.BlockSpec((N_GATES, 1), full2)]
    args = [proj, gcol, gt3, fbc, fbr]
    a_shape = jax.ShapeDtypeStruct((N_TOK, BRANCH_W), BF16)
    a_spec = pl.BlockSpec((T, BRANCH_W), lambda b: (rb0 + b, 0))
    c_spec = pl.BlockSpec((None, None, 2, MA_HEADS, MA_DK, MA_DV), lambda b: (b, l, 0, 0, 0, 0))
    nm_spec = pl.BlockSpec((None, None, 2 * MA_HEADS, 128), lambda b: (b, l, 0, 0))
    aliases = {}
    if latent:
        cos, sin = _rope_tables(T)
        nb = 2 * MA_HEADS
        in_specs += [pl.BlockSpec((T, MA_DK), full2), pl.BlockSpec((T, MA_DK), full2), c_spec, nm_spec, nm_spec, any_spec]
        args += [cos, sin, C0, n0.reshape(B, DEPTH, nb, MA_DK),
                 jnp.broadcast_to(m0.reshape(B, DEPTH, nb, 1), (B, DEPTH, nb, 128)), a_out]
        aliases = {len(args) - 1: 0}
        out_shape, out_specs = a_shape, a_spec
    else:
        nm_shape = jax.ShapeDtypeStruct((B, DEPTH, 2 * MA_HEADS, 128), F32)
        out_shape = (a_shape, jax.ShapeDtypeStruct((B, DEPTH, 2, MA_HEADS, MA_DK, MA_DV), F32), nm_shape, nm_shape)
        out_specs = (a_spec, c_spec, nm_spec, nm_spec)
        if prev is not None:
            in_specs += [any_spec] * 3
            args += list(prev)
            aliases = {len(args) - 3: 1, len(args) - 2: 2, len(args) - 1: 3}
    nc = T // MA_CHUNK
    scratch = [pltpu.VMEM((T, BRANCH_W), BF16), pltpu.VMEM((T, BRANCH_W), BF16),
               pltpu.VMEM((nc, MA_HEADS, MA_DV, MA_CHUNK), BF16),
               pltpu.VMEM((nc, MA_HEADS, MA_DV, MA_CHUNK), F32), pltpu.VMEM((nc, MA_HEADS, MA_DV, MA_CHUNK), F32),
               pltpu.VMEM((2 * MA_HEADS, MA_DV, MA_DK), F32), pltpu.VMEM((2 * MA_HEADS, 1, MA_DK), F32),
               pltpu.VMEM((2 * MA_HEADS, 1, 128), F32),
               pltpu.VMEM((2, nc, N_GATES, MA_CHUNK), F32), pltpu.VMEM((2, T, 128), F32)]
    return pl.pallas_call(
        functools.partial(_mlstm_kernel, T=T, latent=latent),
        out_shape=out_shape, grid=(B,), in_specs=in_specs, out_specs=out_specs, scratch_shapes=scratch,
        input_output_aliases=aliases,
        compiler_params=_cparams("arbitrary"),
        name="mlstm_lat" if latent else "mlstm_ctx",
    )(*args)


HG_SUB = 8


def _hgrn_kernel(*refs, T, latent):
    ff_ref, fb_ref, q_ref, i_ref, g_ref, lbf_ref, lbb_ref = refs[:7]
    if latent:
        s0_ref = refs[7]
        c_ref, of, ob, ST, iT, As, Bs = refs[-7:]
    else:
        c_ref, so_ref, of, ob, ST, iT, As, Bs = refs[-8:]
    L = HG_CHUNK
    NC = T // L
    NB = L // HG_SUB
    DK = HG_DK

    for d in range(2):
        for h in range(HG_HEADS):
            ST[d * HG_HEADS + h] = s0_ref[d, h].T if latent else jnp.zeros((HG_DV, DK), F32)

    PER = 128 // L

    def i_block(tb, carry):
        r0 = pl.multiple_of(tb * 128, 128)
        for h in range(HG_HEADS):
            blk = i_ref[pl.ds(r0, 128), h * HG_DV:(h + 1) * HG_DV].astype(F32).T.astype(BF16)
            for j in range(PER):
                iT[tb * PER + j, h] = blk[:, j * L:(j + 1) * L]
        return carry

    lax.fori_loop(0, T // 128, i_block, 0)

    low = _tri(L, False)
    upp = _tri(L, True)
    row8 = lax.broadcasted_iota(jnp.int32, (HG_SUB, L), 0)
    lane_s = lax.broadcasted_iota(jnp.int32, (HG_SUB, L), 1)
    heads = range(HG_HEADS)
    bnt = (((2,), (2,)), ((0,), (0,)))
    bnn = (((2,), (1,)), ((0,), (0,)))
    LOG2E = 1.4426950408889634

    def split(x):
        return jnp.stack([x[:, h * DK:(h + 1) * DK] for h in heads], 0)

    def decay_sums(c, carry):
        t0 = pl.multiple_of(c * L, L)
        for d in range(2):
            fpre = (ff_ref if d == 0 else fb_ref)[pl.ds(t0, L), :]
            lb = (lbf_ref if d == 0 else lbb_ref)[...]
            f = lb + (1.0 - lb) * jax.nn.sigmoid(fpre)
            a = _dot3(low if d == 0 else upp, jnp.log(f) * LOG2E, True)
            As[d, pl.ds(t0, L), :] = a
            Bs[d, pl.ds(t0, L), :] = a - jnp.log(1.0 - f) * LOG2E
        return carry

    lax.fori_loop(0, NC, decay_sums, 0, unroll=4)

    def chunk_pair(cf, cb):
        tf = pl.multiple_of(cf * L, L)
        tb = pl.multiple_of(cb * L, L)
        both = lambda fn: jnp.concatenate([fn(0, tf), fn(1, tb)], 0)
        A = both(lambda d, t: split(As[d, pl.ds(t, L), :]))
        B = both(lambda d, t: split(Bs[d, pl.ds(t, L), :]))
        q = both(lambda d, t: split(_silu(q_ref[pl.ds(t, L), :].astype(F32))))
        iv = both(lambda d, t: split(i_ref[pl.ds(t, L), :].astype(BF16)))
        ivT = jnp.concatenate([iT[cf], iT[cb]], 0)
        H = HG_HEADS
        st = ST[...]
        o = lax.dot_general((q * jnp.exp2(A)).astype(BF16), st.astype(BF16), bnt, preferred_element_type=F32)
        a_last = jnp.concatenate([A[:H, L - 1:L], A[H:, 0:1]], 0)
        kd = jnp.exp2(a_last - B).astype(BF16)
        rows = []
        for I in range(NB):
            lo, hi = I * HG_SUB, (I + 1) * HG_SUB
            A_I, q_I = A[:, lo:hi], q[:, lo:hi]
            att_f = jnp.zeros((H, HG_SUB, L), F32)
            att_b = jnp.zeros((H, HG_SUB, L), F32)
            for j in range(HG_SUB):
                s = lo + j
                col = jnp.sum(q_I * jnp.exp2(A_I - B[:, s:s + 1]), -1, keepdims=True)
                att_f = jnp.where((lane_s == s) & (row8 >= j), col[:H], att_f)
                att_b = jnp.where((lane_s == s) & (row8 <= j), col[H:], att_b)
            rf, rb = max(lo - 1, 0), min(hi, L - 1)
            R = jnp.concatenate([A[:H, rf:rf + 1], A[H:, rb:rb + 1]], 0)
            zeros = lambda n: jnp.zeros((H, n, DK), BF16)
            ksc_f = jnp.concatenate([jnp.exp2(R[:H] - B[:H, :lo]).astype(BF16), zeros(L - lo)], 1) if I > 0 else zeros(L)
            ksc_b = jnp.concatenate([zeros(hi), jnp.exp2(R[H:] - B[H:, hi:]).astype(BF16)], 1) if I < NB - 1 else zeros(L)
            ksc = jnp.concatenate([ksc_f, ksc_b], 0)
            off = lax.dot_general((q_I * jnp.exp2(A_I - R)).astype(BF16), ksc, bnt, preferred_element_type=F32)
            rows.append(jnp.concatenate([att_f, att_b], 0) + off)
        att = jnp.concatenate(rows, 1)
        o = o + lax.dot_general(att.astype(BF16), iv, bnn, preferred_element_type=F32)
        for h in heads:
            of[pl.ds(tf, L), h * HG_DV:(h + 1) * HG_DV] = o[h]
            ob[pl.ds(tb, L), h * HG_DV:(h + 1) * HG_DV] = o[H + h]
        ST[...] = st * jnp.exp2(a_last) + lax.dot_general(ivT, kd, bnn, preferred_element_type=F32)

    def body(i, carry):
        chunk_pair(i, NC - 1 - i)
        return carry

    lax.fori_loop(0, NC, body, 0, unroll=8)

    def epilogue(r, carry):
        t0 = pl.multiple_of(r * 128, 128)
        o = of[pl.ds(t0, 128), :] + ob[pl.ds(t0, 128), :]
        gsil = _silu(g_ref[pl.ds(t0, 128), :].astype(F32))
        outs = []
        for h in range(HG_HEADS):
            oh = o[:, h * HG_DV:(h + 1) * HG_DV]
            outs.append(oh * lax.rsqrt(jnp.mean(oh * oh, -1, keepdims=True) + RMS_EPS))
        c_ref[pl.ds(t0, 128), :] = (jnp.concatenate(outs, -1) * gsil).astype(BF16)
        return carry

    lax.fori_loop(0, T // 128, epilogue, 0)
    if not latent:
        for d in range(2):
            for h in range(HG_HEADS):
                so_ref[d, h] = ST[d * HG_HEADS + h].T


def _hgrn(p16, p32, lb_l, l, latent, S0=None, c_out=None, prev=None):
    T = DEC_SEQ if latent else SEQ
    B = DEC_BATCH if latent else BATCH
    rb0 = N_CTX // DEC_SEQ if latent else 0
    W = BRANCH_W
    full2 = lambda b: (0, 0)
    any_spec = pl.BlockSpec(memory_space=pl.ANY)
    col = lambda j: pl.BlockSpec((T, W), lambda b: (rb0 + b, j))
    s_spec = pl.BlockSpec((None, None, 2, HG_HEADS, HG_DK, HG_DV), lambda b: (b, l, 0, 0, 0, 0))
    in_specs = [col(0), col(1), col(7), col(8), col(9), pl.BlockSpec((1, W), full2), pl.BlockSpec((1, W), full2)]
    args = [p32, p32, p16, p16, p16, lb_l[0][None, :], lb_l[1][None, :]]
    c_shape = jax.ShapeDtypeStruct((N_TOK, W), BF16)
    c_spec = pl.BlockSpec((T, W), lambda b: (rb0 + b, 0))
    aliases = {}
    if latent:
        in_specs += [s_spec, any_spec]
        args += [S0, c_out]
        aliases = {8: 0}
        out_shape, out_specs = c_shape, c_spec
    else:
        out_shape = (c_shape, jax.ShapeDtypeStruct((B, DEPTH, 2, HG_HEADS, HG_DK, HG_DV), F32))
        out_specs = (c_spec, s_spec)
        if prev is not None:
            in_specs.append(any_spec)
            args.append(prev)
            aliases = {7: 1}
    scratch = [pltpu.VMEM((T, W), F32), pltpu.VMEM((T, W), F32), pltpu.VMEM((2 * HG_HEADS, HG_DV, HG_DK), F32),
               pltpu.VMEM((T // HG_CHUNK, HG_HEADS, HG_DV, HG_CHUNK), BF16),
               pltpu.VMEM((2, T, W), F32), pltpu.VMEM((2, T, W), F32)]
    return pl.pallas_call(
        functools.partial(_hgrn_kernel, T=T, latent=latent),
        out_shape=out_shape, grid=(B,), in_specs=in_specs, out_specs=out_specs, scratch_shapes=scratch,
        input_output_aliases=aliases,
        compiler_params=_cparams("arbitrary"),
        name="hgrn_lat" if latent else "hgrn_ctx",
    )(*args)


def _merge_kernel(a_ref, b_ref, c_ref, ga_ref, gb_ref, gc_ref, xc_ref, xl_ref, g1_ref, sh2_ref, sc2_ref,
                  wb_ref, wo_ref, lng_ref, lnb_ref, wr_ref, x1_ref, h2_ref, *, tm):
    def br(v_ref, g_ref, k):
        return jax.nn.sigmoid(g_ref[...].astype(F32)) * jnp.dot(v_ref[...], wb_ref[k], preferred_element_type=F32)

    mix = br(a_ref, ga_ref, 0) + br(b_ref, gb_ref, 1) + br(c_ref, gc_ref, 2)
    y = jnp.dot(mix.astype(BF16), wo_ref[...], preferred_element_type=F32)
    x = _pair_read(pl.program_id(0), tm, xc_ref, xl_ref)
    x1 = _layer_norm(DEEPNORM_ALPHA * x + g1_ref[...] * y, lng_ref[...], lnb_ref[...])
    x1_ref[...] = x1
    h2 = x1 * (1.0 + sc2_ref[...]) + sh2_ref[...]
    h2_ref[:, :D_MODEL] = h2.astype(h2_ref.dtype)
    lt = lax.dot_general(wr_ref[...], h2, _NT, preferred_element_type=F32, precision=HI)
    r = lax.broadcasted_iota(jnp.int32, lt.shape, 0)
    neg = -jnp.inf
    lg = jnp.where(r < N_GROUPS, lt, neg)
    mg = jnp.max(lg, 0, keepdims=True)
    g_sel = jnp.min(jnp.where(lg == mg, r, ROUTER_ROWS), 0, keepdims=True)
    p_sel = 1.0 / jnp.sum(jnp.where(r < N_GROUPS, jnp.exp(lg - mg), 0.0), 0, keepdims=True)
    lo = ROUTER_E0 + EXP_PER_GROUP * g_sel
    le = jnp.where((r >= lo) & (r < lo + EXP_PER_GROUP), lt, neg)
    v1 = jnp.max(le, 0, keepdims=True)
    i1 = jnp.min(jnp.where(le == v1, r, ROUTER_ROWS), 0, keepdims=True)
    le2 = jnp.where(r == i1, neg, le)
    v2 = jnp.max(le2, 0, keepdims=True)
    i2 = jnp.min(jnp.where(le2 == v2, r, ROUTER_ROWS), 0, keepdims=True)
    e2 = jnp.exp(v2 - v1)
    w1 = p_sel / (1.0 + e2)
    w2 = p_sel * e2 / (1.0 + e2)
    w1_hi = w1.astype(BF16).astype(F32)
    w2_hi = w2.astype(BF16).astype(F32)
    j1, j2 = i1 - lo, i2 - lo
    packed = jnp.where(r == j1, w1_hi, jnp.where(r == j2, w2_hi, jnp.where(
        r == j1 + EXP_PER_GROUP, w1 - w1_hi, jnp.where(r == j2 + EXP_PER_GROUP, w2 - w2_hi, jnp.where(
            r == 2 * EXP_PER_GROUP, g_sel.astype(F32), 0.0)))))
    packed = jnp.concatenate([packed, jnp.zeros((128 - ROUTER_ROWS, packed.shape[1]), F32)], 0)
    h2_ref[:, D_MODEL:] = packed.T.astype(h2_ref.dtype)


def _merge(a, b, c, p16, xc, xl, mod, wb, wo, lng, lnb, wr, l):
    tm = 512
    tok = lambda i: (i, 0)
    ln_spec = pl.BlockSpec((None, None, 1, D_MODEL), lambda i: (l, 0, 0, 0))
    return pl.pallas_call(
        functools.partial(_merge_kernel, tm=tm),
        out_shape=(jax.ShapeDtypeStruct((N_TOK, D_MODEL), F32), jax.ShapeDtypeStruct((N_TOK, MOE_XW), MOE_XDT)),
        grid=(N_TOK // tm,),
        in_specs=[pl.BlockSpec((tm, BRANCH_W), tok), pl.BlockSpec((tm, BRANCH_W), tok), pl.BlockSpec((tm, BRANCH_W), tok),
                  pl.BlockSpec((tm, D_MODEL), lambda i: (i, 5)), pl.BlockSpec((tm, D_MODEL), lambda i: (i, 6)),
                  pl.BlockSpec((tm, D_MODEL), lambda i: (i, 7)),
                  *_pair_specs(tm), _mod_spec(l, 2, tm), _mod_spec(l, 3, tm), _mod_spec(l, 4, tm),
                  pl.BlockSpec((None, 3, BRANCH_W, D_MODEL), lambda i: (l, 0, 0, 0)),
                  pl.BlockSpec((None, D_MODEL, D_MODEL), lambda i: (l, 0, 0)),
                  ln_spec, ln_spec,
                  pl.BlockSpec((None, ROUTER_ROWS, D_MODEL), lambda i: (l, 0, 0))],
        out_specs=(pl.BlockSpec((tm, D_MODEL), tok), pl.BlockSpec((tm, MOE_XW), tok)),
        compiler_params=_cparams("arbitrary"),
        name="merge",
    )(a, b, c, p16, p16, p16, xc, xl, mod, mod, mod, wb, wo, lng, lnb, wr)


def _moe_up_kernel(gid_ref, nused_ref, x_ref, w1_ref, w3_ref, hid_ref, w1b, w3b):
    f = pl.program_id(0)
    t = pl.program_id(1)

    @pl.when(t < nused_ref[0])
    def _():
        first = jnp.logical_or(t == 0, gid_ref[t] != gid_ref[jnp.maximum(t - 1, 0)])

        @pl.when(first)
        def _():
            w1b[...] = w1_ref[...].astype(BF16)
            w3b[...] = w3_ref[...].astype(BF16)

        x = x_ref[:, :D_MODEL].astype(BF16)
        rec = x_ref[:, D_MODEL:].astype(F32)
        lane = lax.broadcasted_iota(jnp.int32, rec.shape, 1)
        for j in range(MOE_FE):
            e = f * MOE_FE + j
            a = jnp.dot(x, w1b[j], preferred_element_type=F32)
            b = jnp.dot(x, w3b[j], preferred_element_type=F32)
            gcol = jnp.sum(jnp.where(jnp.logical_or(lane == e, lane == e + EXP_PER_GROUP), rec, 0.0), -1, keepdims=True)
            hid_ref[:, j * D_EXPERT:(j + 1) * D_EXPERT] = (_silu(a) * b * gcol).astype(BF16)


def _moe_tile(t, n):
    return jnp.minimum(t, n[0] - 1)


def _moe_up(gid, nused, xs, w1, w3, l):
    tm = MOE_TM
    npad = MOE_NT * tm
    nf = EXP_PER_GROUP // MOE_FE
    w_spec = pl.BlockSpec((None, MOE_FE, D_MODEL, D_EXPERT),
                          lambda f, t, g, n: (l, nf * g[_moe_tile(t, n)] + f, 0, 0))
    grid_spec = pltpu.PrefetchScalarGridSpec(
        num_scalar_prefetch=2,
        grid=(nf, MOE_NT),
        in_specs=[pl.BlockSpec((tm, MOE_XW), lambda f, t, g, n: (_moe_tile(t, n), 0)), w_spec, w_spec],
        out_specs=pl.BlockSpec((tm, MOE_FE * D_EXPERT), lambda f, t, g, n: (_moe_tile(t, n), f)),
        scratch_shapes=[pltpu.VMEM((MOE_FE, D_MODEL, D_EXPERT), BF16), pltpu.VMEM((MOE_FE, D_MODEL, D_EXPERT), BF16)],
    )
    return pl.pallas_call(
        _moe_up_kernel,
        out_shape=jax.ShapeDtypeStruct((npad, EXP_PER_GROUP * D_EXPERT), BF16),
        grid_spec=grid_spec,
        compiler_params=_cparams("arbitrary", "arbitrary"),
        name="moe_up",
    )(gid, nused, xs, w1, w3)


def _moe_down_kernel(gid_ref, nused_ref, hid_ref, w2_ref, y_ref, w2b):
    t = pl.program_id(0)

    @pl.when(t < nused_ref[0])
    def _():
        first = jnp.logical_or(t == 0, gid_ref[t] != gid_ref[jnp.maximum(t - 1, 0)])

        @pl.when(first)
        def _():
            w2b[...] = w2_ref[...].astype(BF16)

        y_ref[...] = jnp.dot(hid_ref[...], w2b[...], preferred_element_type=F32)


def _moe_down(gid, nused, hid, w2g, l):
    tm = MOE_TM
    npad = MOE_NT * tm
    hw = EXP_PER_GROUP * D_EXPERT
    grid_spec = pltpu.PrefetchScalarGridSpec(
        num_scalar_prefetch=2,
        grid=(MOE_NT,),
        in_specs=[pl.BlockSpec((tm, hw), lambda t, g, n: (_moe_tile(t, n), 0)),
                  pl.BlockSpec((None, None, hw, D_MODEL), lambda t, g, n: (l, g[_moe_tile(t, n)], 0, 0))],
        out_specs=pl.BlockSpec((tm, D_MODEL), lambda t, g, n: (_moe_tile(t, n), 0)),
        scratch_shapes=[pltpu.VMEM((hw, D_MODEL), BF16)],
    )
    return pl.pallas_call(
        _moe_down_kernel,
        out_shape=jax.ShapeDtypeStruct((npad, D_MODEL), F32),
        grid_spec=grid_spec,
        compiler_params=_cparams("arbitrary"),
        name="moe_down",
    )(gid, nused, hid, w2g)


def _moe(h2x, w1, w3, w2g, l):
    tm = MOE_TM
    npad = MOE_NT * tm
    g = h2x[:, D_MODEL + 2 * EXP_PER_GROUP].astype(jnp.int32)
    onehot = (g[:, None] == jnp.arange(N_GROUPS)[None, :]).astype(jnp.int32)
    counts = jnp.sum(onehot, 0)
    rank = jnp.sum((jnp.cumsum(onehot, 0) - onehot) * onehot, 1)
    padded = (counts + tm - 1) // tm * tm
    ends = jnp.cumsum(padded)
    offs = ends - padded
    dest = offs[g] + rank
    src = (jnp.arange(npad, dtype=jnp.int32) % N_TOK).at[dest].set(jnp.arange(N_TOK, dtype=jnp.int32),
                                                                   unique_indices=True)
    starts = jnp.arange(MOE_NT, dtype=jnp.int32) * tm
    tile_gid = jnp.minimum(jnp.sum((ends[None, :] <= starts[:, None]).astype(jnp.int32), 1), N_GROUPS - 1)
    nused = (ends[-1:] // tm).astype(jnp.int32)
    take = lambda arr, idx: arr.at[idx].get(mode="promise_in_bounds", unique_indices=False)
    hid = _moe_up(tile_gid, nused, take(h2x, src), w1, w3, l)
    ys = _moe_down(tile_gid, nused, hid, w2g, l)
    return take(ys, dest)


def _final_kernel(*refs, tm, with_h):
    x1_ref, y_ref, g2_ref, lng_ref, lnb_ref = refs[:5]
    x2 = _layer_norm(DEEPNORM_ALPHA * x1_ref[...] + g2_ref[...] * y_ref[...], lng_ref[...], lnb_ref[...])
    i = pl.program_id(0)
    if with_h:
        sh_ref, sc_ref, xc_ref, xl_ref, h_ref = refs[5:]
        h_ref[...] = (x2 * (1.0 + sc_ref[...]) + sh_ref[...]).astype(BF16)
    else:
        xc_ref, xl_ref = refs[5:]

    @pl.when(i < N_CTX // tm)
    def _():
        xc_ref[...] = x2

    @pl.when(i >= N_CTX // tm)
    def _():
        xl_ref[...] = x2


def _final(x1, y, mod, lng, lnb, l):
    tm = 1024
    tok = lambda i: (i, 0)
    with_h = l + 1 < DEPTH
    ln_spec = pl.BlockSpec((None, None, 1, D_MODEL), lambda i: (l, 1, 0, 0))
    half = jax.ShapeDtypeStruct((N_CTX, D_MODEL), F32)
    in_specs = [pl.BlockSpec((tm, D_MODEL), tok), pl.BlockSpec((tm, D_MODEL), tok), _mod_spec(l, 5, tm), ln_spec, ln_spec]
    args = [x1, y, mod, lng, lnb]
    out_shape = [half, half]
    out_specs = list(_pair_specs(tm))
    if with_h:
        in_specs += [_mod_spec(l + 1, 0, tm), _mod_spec(l + 1, 1, tm)]
        args += [mod, mod]
        out_shape.append(jax.ShapeDtypeStruct((N_TOK, D_MODEL), BF16))
        out_specs.append(pl.BlockSpec((tm, D_MODEL), tok))
    return pl.pallas_call(
        functools.partial(_final_kernel, tm=tm, with_h=with_h),
        out_shape=tuple(out_shape), grid=(N_TOK // tm,), in_specs=in_specs, out_specs=tuple(out_specs),
        compiler_params=_cparams("arbitrary"),
        name="final",
    )(*args)


def kernel(x_prompt, x_sample, c, cache_na_k, cache_na_v, state_mlstm_C, state_mlstm_n, state_mlstm_m, state_hgrn,
           c_ctx, w_mod, b_mod, w_in, b_in, mlstm_fbias, hgrn_lb_logits, na_rpb, w_branch, w_out, ln_g, ln_b,
           w_rg, w_re, w_e1, w_e3, w_e2):
    assert N_CTX == N_LAT
    lb_cum = jnp.cumsum(jax.nn.softmax(hgrn_lb_logits.astype(F32), axis=1), axis=1)
    lb_all = lb_cum - lb_cum[:, :1]

    cs = jnp.zeros((N_MODROWS, D_MODEL), F32).at[0].set(c_ctx).at[1:1 + DEC_BATCH].set(c)
    mod = _modulation(cs, w_mod, b_mod).reshape(DEPTH, N_MODROWS, 6, 1, D_MODEL)

    wb = w_branch.astype(BF16)
    wo = w_out.astype(BF16)
    lng = ln_g.reshape(DEPTH, 2, 1, D_MODEL)
    lnb = ln_b.reshape(DEPTH, 2, 1, D_MODEL)
    wr = jnp.zeros((DEPTH, ROUTER_ROWS, D_MODEL), F32)
    wr = wr.at[:, :N_GROUPS].set(jnp.swapaxes(w_rg, 1, 2)).at[:, ROUTER_E0:ROUTER_E0 + N_EXPERTS].set(jnp.swapaxes(w_re, 1, 2))
    w2g = w_e2.reshape(DEPTH, N_GROUPS, EXP_PER_GROUP * D_EXPERT, D_MODEL)
    b_main = jnp.concatenate([b_in[:, :GATE_COL0], b_in[:, GATE_COL0 + N_GATES:]], 1)

    xc = x_prompt.reshape(N_CTX, D_MODEL)
    xl = x_sample.reshape(N_LAT, D_MODEL)
    w_t = jnp.swapaxes(w_in, 1, 2)
    na_bias = jax.vmap(_na_bias_table)(na_rpb)
    h = _prep(xc, xl, mod)

    kv = (None, None)
    ma_states = None
    hg_state = None
    for l in range(DEPTH):
        p16, p32 = _inproj(h, w_t, b_main[l][None, :], l)
        gcol, gt = _gates(h, w_t, b_in.reshape(DEPTH, 1, N_IN), l)
        gt3 = gt.reshape(N_GATES, N_TOK // MA_CHUNK, MA_CHUNK).transpose(1, 0, 2)

        a, *ma_states = _mlstm(p16, gcol, gt3, mlstm_fbias[l], l, False, prev=ma_states)
        a = _mlstm(p16, gcol, gt3, mlstm_fbias[l], l, True, state_mlstm_C, state_mlstm_n, state_mlstm_m, a_out=a)
        b, *kv = _ctx_attention(p16, l, *kv)
        b = _lat_attention(p16, cache_na_k, cache_na_v, na_bias, l, b)
        cc, hg_state = _hgrn(p16, p32, lb_all[:, l], l, False, prev=hg_state)
        cc = _hgrn(p16, p32, lb_all[:, l], l, True, state_hgrn, c_out=cc)

        x1, h2x = _merge(a, b, cc, p16, xc, xl, mod, wb, wo, lng, lnb, wr, l)
        y2 = _moe(h2x, w_e1, w_e3, w2g, l)
        outs = _final(x1, y2, mod, lng, lnb, l)
        xc, xl = outs[0], outs[1]
        if l + 1 < DEPTH:
            h = outs[2]

    dt = x_prompt.dtype
    new_C, new_n, new_m = ma_states
    new_n = new_n.reshape(BATCH, DEPTH, 2, MA_HEADS, MA_DK)
    new_m = new_m[:, :, :, 0].reshape(BATCH, DEPTH, 2, MA_HEADS)
    return (xc.reshape(BATCH, SEQ, D_MODEL), xl.reshape(DEC_BATCH, DEC_SEQ, D_MODEL), kv[0], kv[1],
            new_C.astype(dt), new_n.astype(dt), new_m.astype(dt), hg_state.astype(dt))
```

```python
import functools

import numpy as np
import jax
import jax.numpy as jnp
from jax import lax
from jax.experimental import pallas as pl
from jax.experimental.pallas import tpu as pltpu

F32 = jnp.float32
BF16 = jnp.bfloat16
HI = lax.Precision.HIGHEST

D_MODEL = 1024
BATCH = 16
SEQ = 256
DEPTH = 2
DEC_BATCH = 4
DEC_SEQ = 1024
PAST_LEN = 256
GRID_W = 64
MA_HEADS = 4
MA_DK = 128
MA_DV = 128
MA_CHUNK = 128
NA_HEADS = 8
NA_DH = 64
NA_KR_MAX = 8
NA_KC = 16
HG_HEADS = 4
HG_DK = 128
HG_DV = 128
HG_CHUNK = 32
BRANCH_W = 512
N_GROUPS = 4
EXP_PER_GROUP = 4
N_EXPERTS = N_GROUPS * EXP_PER_GROUP
D_EXPERT = 512
ROPE_BASE = 10000.0
LN_EPS = 1e-5
RMS_EPS = 1e-6
DEEPNORM_ALPHA = (2 * DEPTH) ** 0.25

N_CTX = BATCH * SEQ
N_LAT = DEC_BATCH * DEC_SEQ
N_TOK = N_CTX + N_LAT
N_MODROWS = 8
GATE_COL0 = 4 * BRANCH_W
N_GATES = 4 * MA_HEADS
N_IN = 9232
P_COLS = N_IN - N_GATES
MOE_TM = 512
MOE_NT = N_TOK // MOE_TM + N_GROUPS
MOE_FE = 4
MOE_XW = D_MODEL + 128
MOE_XDT = F32
ROUTER_ROWS = 32
ROUTER_E0 = 8
VMEM_LIMIT = 56 * 1024 * 1024

_NT = (((1,), (1,)), ((), ()))


def _cparams(*sem):
    return pltpu.CompilerParams(dimension_semantics=sem, vmem_limit_bytes=VMEM_LIMIT)


def _mod_row(tile, tm):
    return jnp.maximum((tile * tm) // DEC_SEQ - (N_CTX // DEC_SEQ - 1), 0)


def _mod_spec(l, part, tm):
    return pl.BlockSpec((None, None, None, 1, D_MODEL), lambda i: (l, _mod_row(i, tm), part, 0, 0))


def _pair_specs(tm):
    nc = N_CTX // tm
    return (pl.BlockSpec((tm, D_MODEL), lambda i: (jnp.minimum(i, nc - 1), 0)),
            pl.BlockSpec((tm, D_MODEL), lambda i: (jnp.maximum(i - nc, 0), 0)))


def _pair_read(i, tm, c_ref, l_ref):
    return jnp.where(i < N_CTX // tm, c_ref[...], l_ref[...])


def _silu(x):
    return x * jax.nn.sigmoid(x)


def _layer_norm(x, g, b):
    mu = jnp.mean(x, -1, keepdims=True)
    xc = x - mu
    var = jnp.mean(xc * xc, -1, keepdims=True)
    return xc * lax.rsqrt(var + LN_EPS) * g + b


def _log_sigmoid(x):
    return jnp.minimum(x, 0.0) - jnp.log(1.0 + jnp.exp(-jnp.abs(x)))


def _tri(n, upper):
    r = lax.broadcasted_iota(jnp.int32, (n, n), 0)
    c = lax.broadcasted_iota(jnp.int32, (n, n), 1)
    return jnp.where((r <= c) if upper else (r >= c), 1.0, 0.0).astype(F32)


def _dot3(a, b, split_b):
    x = b if split_b else a
    hi = x.astype(BF16)
    r = x - hi.astype(F32)
    mid = r.astype(BF16)
    lo = (r - mid.astype(F32)).astype(BF16)
    one = (a if split_b else b).astype(BF16)
    dot = (lambda p: jnp.dot(one, p, preferred_element_type=F32)) if split_b else (
        lambda p: jnp.dot(p, one, preferred_element_type=F32))
    return dot(hi) + dot(mid) + dot(lo)


def _mod_kernel(c_ref, w_ref, b_ref, o_ref):
    s = _silu(c_ref[...])
    o_ref[...] = jnp.dot(s.astype(BF16), w_ref[...].astype(BF16), preferred_element_type=F32) + b_ref[...]


def _modulation(cs, w_mod, b_mod):
    tn = 1024
    return pl.pallas_call(
        _mod_kernel,
        out_shape=jax.ShapeDtypeStruct((DEPTH, N_MODROWS, 6 * D_MODEL), F32),
        grid=(DEPTH, 6 * D_MODEL // tn),
        in_specs=[pl.BlockSpec((N_MODROWS, D_MODEL), lambda l, j: (0, 0)),
                  pl.BlockSpec((None, D_MODEL, tn), lambda l, j: (l, 0, j)),
                  pl.BlockSpec((None, 1, tn), lambda l, j: (l, 0, j))],
        out_specs=pl.BlockSpec((None, N_MODROWS, tn), lambda l, j: (l, 0, j)),
        compiler_params=_cparams("arbitrary", "arbitrary"),
        name="modulation",
    )(cs, w_mod, b_mod.reshape(DEPTH, 1, 6 * D_MODEL))


def _prep_kernel(xc_ref, xl_ref, sh_ref, sc_ref, h_ref, *, tm):
    x = _pair_read(pl.program_id(0), tm, xc_ref, xl_ref)
    h_ref[...] = (x * (1.0 + sc_ref[...]) + sh_ref[...]).astype(BF16)


def _prep(xc, xl, mod):
    tm = 1024
    return pl.pallas_call(
        functools.partial(_prep_kernel, tm=tm),
        out_shape=jax.ShapeDtypeStruct((N_TOK, D_MODEL), BF16),
        grid=(N_TOK // tm,),
        in_specs=[*_pair_specs(tm), _mod_spec(0, 0, tm), _mod_spec(0, 1, tm)],
        out_specs=pl.BlockSpec((tm, D_MODEL), lambda i: (i, 0)),
        compiler_params=_cparams("arbitrary"),
        name="prep",
    )(xc, xl, mod, mod)


INPROJ_TN = 512
N_PLAIN_TILES = GATE_COL0 // INPROJ_TN


F32_TILE0 = 7
N_F32_TILES = 2
P16_COLS = P_COLS - N_F32_TILES * INPROJ_TN


def _inproj_kernel(h_ref, wa_ref, wb_ref, b_ref, o_ref, *, src_tile):
    j = src_tile(pl.program_id(1))

    @pl.when(j < N_PLAIN_TILES)
    def _():
        o_ref[...] = (lax.dot_general(h_ref[...], wa_ref[...].astype(BF16), _NT, preferred_element_type=F32)
                      + b_ref[...]).astype(o_ref.dtype)

    @pl.when(j >= N_PLAIN_TILES)
    def _():
        w = jnp.concatenate([wa_ref[N_GATES:, :], wb_ref[...]], 0)
        o_ref[...] = (lax.dot_general(h_ref[...], w.astype(BF16), _NT, preferred_element_type=F32)
                      + b_ref[...]).astype(o_ref.dtype)


def _inproj_call(h, w_t, b_main, l, tm, n_tiles, src_tile, dtype, name):
    tn = INPROJ_TN
    return pl.pallas_call(
        functools.partial(_inproj_kernel, src_tile=src_tile),
        out_shape=jax.ShapeDtypeStruct((N_TOK, n_tiles * tn), dtype),
        grid=(N_TOK // tm, n_tiles),
        in_specs=[pl.BlockSpec((tm, D_MODEL), lambda i, j: (i, 0)),
                  pl.BlockSpec((None, tn, D_MODEL), lambda i, j: (l, src_tile(j), 0)),
                  pl.BlockSpec((None, N_GATES, D_MODEL), lambda i, j: (l, (src_tile(j) + 1) * (tn // N_GATES), 0)),
                  pl.BlockSpec((1, tn), lambda i, j: (0, src_tile(j)))],
        out_specs=pl.BlockSpec((tm, tn), lambda i, j: (i, j)),
        compiler_params=_cparams("arbitrary", "arbitrary"),
        name=name,
    )(h, w_t, w_t, b_main)


def _inproj(h, w_t, b_main, l):
    skip_f32 = lambda j: jnp.where(j < F32_TILE0, j, j + N_F32_TILES)
    p16 = _inproj_call(h, w_t, b_main, l, 4096, P16_COLS // INPROJ_TN, skip_f32, BF16, "inproj")
    p32 = _inproj_call(h, w_t, b_main, l, 4096, N_F32_TILES, lambda j: j + F32_TILE0, F32, "inproj_f32")
    return p16, p32


def _gates_kernel(h_ref, w_ref, b_ref, gc_ref, gt_ref):
    g = lax.dot_general(h_ref[...], w_ref[...].astype(BF16), _NT, preferred_element_type=F32) + b_ref[...]
    gc_ref[...] = g
    gt_ref[...] = g.T[:N_GATES]


def _gates(h, w_t, b_in3, l):
    tm = 1024
    gblk = GATE_COL0 // 128
    return pl.pallas_call(
        _gates_kernel,
        out_shape=(jax.ShapeDtypeStruct((N_TOK, 128), F32), jax.ShapeDtypeStruct((N_GATES, N_TOK), F32)),
        grid=(N_TOK // tm,),
        in_specs=[pl.BlockSpec((tm, D_MODEL), lambda i: (i, 0)),
                  pl.BlockSpec((None, 128, D_MODEL), lambda i: (l, gblk, 0)),
                  pl.BlockSpec((None, 1, 128), lambda i: (l, 0, gblk))],
        out_specs=(pl.BlockSpec((tm, 128), lambda i: (i, 0)), pl.BlockSpec((N_GATES, tm), lambda i: (0, i))),
        compiler_params=_cparams("arbitrary"),
        name="gates",
    )(h, w_t, b_in3)


HEADS_PER_BLK = 128 // NA_DH
NA_NBLK = NA_HEADS // HEADS_PER_BLK
NA_QSCALE = NA_DH ** -0.5
Q_COL, K_COL, V_COL = 16, 20, 24
QKV_COL = 4


def _ctx_attn_kernel(*refs):
    q_ref, k_ref, v_ref = refs[:3]
    o_ref, ko_ref, vo_ref = refs[-3:]
    heads = range(NA_HEADS)
    split = lambda x: jnp.stack([x[:, h * NA_DH:(h + 1) * NA_DH] for h in heads], 0)
    q = split(q_ref[...] * NA_QSCALE)
    k = split(k_ref[...])
    v = split(v_ref[...])
    ko_ref[...] = k.astype(F32)
    vo_ref[...] = v.astype(F32)
    s = lax.dot_general(q, k, (((2,), (2,)), ((0,), (0,))), preferred_element_type=F32)
    e = jnp.exp(s - jnp.max(s, -1, keepdims=True))
    p = e * (1.0 / jnp.sum(e, -1, keepdims=True))
    o = lax.dot_general(p.astype(BF16), v, (((2,), (1,)), ((0,), (0,))), preferred_element_type=F32)
    o_ref[...] = jnp.concatenate([o[h] for h in heads], -1).astype(BF16)


def _ctx_attention(p16, l, prev_k=None, prev_v=None):
    kv_shape = jax.ShapeDtypeStruct((BATCH, DEPTH, NA_HEADS, SEQ, NA_DH), F32)
    kv_spec = pl.BlockSpec((None, None, NA_HEADS, SEQ, NA_DH), lambda b: (b, l, 0, 0, 0))
    col = lambda j: pl.BlockSpec((SEQ, BRANCH_W), lambda b: (b, j))
    in_specs = [col(QKV_COL), col(QKV_COL + 1), col(QKV_COL + 2)]
    args = [p16, p16, p16]
    aliases = {}
    if prev_k is not None:
        in_specs += [pl.BlockSpec(memory_space=pl.ANY)] * 2
        args += [prev_k, prev_v]
        aliases = {3: 1, 4: 2}
    return pl.pallas_call(
        _ctx_attn_kernel,
        out_shape=(jax.ShapeDtypeStruct((N_TOK, BRANCH_W), BF16), kv_shape, kv_shape),
        grid=(BATCH,),
        in_specs=in_specs,
        out_specs=(pl.BlockSpec((SEQ, BRANCH_W), lambda b: (b, 0)), kv_spec, kv_spec),
        input_output_aliases=aliases,
        compiler_params=_cparams("arbitrary"),
        name="ctx_attention",
    )(*args)


NA_ROWS = DEC_SEQ // GRID_W
NA_KR = min(NA_KR_MAX, NA_ROWS)
NA_QROWS = 4
NA_QT = NA_ROWS // NA_QROWS
NA_WROWS = NA_KR + NA_QROWS - 1
NA_WKEYS = NA_WROWS * GRID_W


def _na_window_start(t):
    return min(max(t * NA_QROWS - NA_KR // 2, 0), NA_ROWS - NA_WROWS)


def _na_bias_table(rpb):
    c = np.arange(GRID_W)
    c0 = np.clip(c - NA_KC // 2, 0, GRID_W - NA_KC)
    kc = np.arange(GRID_W)
    valid = (kc[None, :] >= c0[:, None]) & (kc[None, :] < c0[:, None] + NA_KC)
    dc = kc[None, :] - c[:, None] + NA_KC - 1
    onehot = (dc[None] == np.arange(2 * NA_KC - 1)[:, None, None]) & valid[None]
    toep = jnp.einsum('hrd,dcx->hrcx', rpb.astype(F32), jnp.asarray(onehot, F32), precision=HI)
    toep = jnp.where(valid[None, None], toep, -jnp.inf)
    ninf = jnp.full((NA_HEADS, GRID_W, GRID_W), -jnp.inf, F32)
    tiles = []
    for t in range(NA_QT):
        w0 = _na_window_start(t)
        qrows = []
        for r in range(t * NA_QROWS, (t + 1) * NA_QROWS):
            r0 = min(max(r - NA_KR // 2, 0), NA_ROWS - NA_KR)
            blocks = []
            for kr in range(w0, w0 + NA_WROWS):
                inside = r0 <= kr < r0 + NA_KR
                blocks.append(toep[:, kr - r + NA_KR_MAX - 1] if inside else ninf)
            qrows.append(jnp.concatenate(blocks, -1))
        tiles.append(jnp.concatenate(qrows, 1))
    return jnp.stack(tiles, 1).astype(BF16)


def _lat_attn_kernel(q_ref, k_ref, v_ref, ck_ref, cv_ref, bias_ref, prev_ref, o_ref):
    heads = range(HEADS_PER_BLK)
    split = lambda x: jnp.stack([x[:, h * NA_DH:(h + 1) * NA_DH] for h in heads], 0)
    bnt = (((2,), (2,)), ((0,), (0,)))
    bnn = (((2,), (1,)), ((0,), (0,)))
    q = (q_ref[...] * NA_QSCALE).astype(BF16)
    k = k_ref[...].astype(BF16)
    v = v_ref[...].astype(BF16)
    ck = ck_ref[...].astype(BF16)
    cv = cv_ref[...].astype(BF16)
    nq = NA_QROWS * GRID_W
    for t in range(NA_QT):
        w0 = _na_window_start(t)
        qs = slice(t * nq, (t + 1) * nq)
        ws = slice(w0 * GRID_W, (w0 + NA_WROWS) * GRID_W)
        qh = split(q[qs])
        s_loc = lax.dot_general(qh, split(k[ws]), bnt, preferred_element_type=F32) + bias_ref[:, t]
        s_ctx = lax.dot_general(qh, ck, bnt, preferred_element_type=F32)
        m = jnp.maximum(jnp.max(s_loc, -1, keepdims=True), jnp.max(s_ctx, -1, keepdims=True))
        e_loc = jnp.exp(s_loc - m)
        e_ctx = jnp.exp(s_ctx - m)
        inv = 1.0 / (jnp.sum(e_loc, -1, keepdims=True) + jnp.sum(e_ctx, -1, keepdims=True))
        acc = (lax.dot_general(e_loc.astype(BF16), split(v[ws]), bnn, preferred_element_type=F32)
               + lax.dot_general(e_ctx.astype(BF16), cv, bnn, preferred_element_type=F32)) * inv
        o_ref[qs, :] = jnp.concatenate([acc[h] for h in heads], -1).astype(BF16)


def _lat_attention(proj, ck, cv, bias, l, b_out):
    rb0 = N_CTX // DEC_SEQ
    cb = lambda base: (lambda j, b: (rb0 + b, base + j))
    c_spec = pl.BlockSpec((None, None, HEADS_PER_BLK, PAST_LEN, NA_DH), lambda j, b: (b, l, j, 0, 0))
    return pl.pallas_call(
        _lat_attn_kernel,
        out_shape=jax.ShapeDtypeStruct((N_TOK, BRANCH_W), BF16),
        grid=(NA_NBLK, DEC_BATCH),
        in_specs=[pl.BlockSpec((DEC_SEQ, 128), cb(Q_COL)), pl.BlockSpec((DEC_SEQ, 128), cb(K_COL)),
                  pl.BlockSpec((DEC_SEQ, 128), cb(V_COL)), c_spec, c_spec,
                  pl.BlockSpec((None, HEADS_PER_BLK, NA_QT, NA_QROWS * GRID_W, NA_WKEYS), lambda j, b: (l, j, 0, 0, 0)),
                  pl.BlockSpec(memory_space=pl.ANY)],
        out_specs=pl.BlockSpec((DEC_SEQ, 128), lambda j, b: (rb0 + b, j)),
        input_output_aliases={6: 0},
        compiler_params=_cparams("arbitrary", "arbitrary"),
        name="lat_attention",
    )(proj, proj, proj, ck, cv, bias, b_out)


MA_KSCALE = MA_DK ** -0.5


def _rope_tables(T):
    t = np.arange(T)
    half = MA_DK // 2
    inv = ROPE_BASE ** (-jnp.arange(0, half, 2, dtype=F32) / half)
    ang_r = jnp.asarray((t // GRID_W).astype(np.float32))[:, None] * inv[None, :]
    ang_c = jnp.asarray((t % GRID_W).astype(np.float32))[:, None] * inv[None, :]
    cos = jnp.concatenate([jnp.cos(ang_r)] * 2 + [jnp.cos(ang_c)] * 2, -1)
    sin = jnp.concatenate([-jnp.sin(ang_r), jnp.sin(ang_r), -jnp.sin(ang_c), jnp.sin(ang_c)], -1)
    return cos, sin


def _mlstm_kernel(*refs, T, latent):
    if latent:
        (p_ref, gc_ref, gt_ref, fbc_ref, fbr_ref, cos_ref, sin_ref, c0_ref, n0_ref, m0_ref, prev_ref,
         a_ref, qs, ks, vT, hfT, hbT, CT, ns, ms, brs, kcs) = refs
    else:
        p_ref, gc_ref, gt_ref, fbc_ref, fbr_ref = refs[:5]
        a_ref, co_ref, no_ref, mo_ref, qs, ks, vT, hfT, hbT, CT, ns, ms, brs, kcs = refs[-14:]
    L = MA_CHUNK
    NC = T // L
    W = BRANCH_W
    PER = 128 // L

    lane = lax.broadcasted_iota(jnp.int32, (T, MA_DK), 1)
    lo_half = (lane % (MA_DK // 2)) < (MA_DK // 4)

    def rope(x):
        if not latent:
            return x
        swapped = jnp.where(lo_half, pltpu.roll(x, MA_DK - MA_DK // 4, 1), pltpu.roll(x, MA_DK // 4, 1))
        return x * cos_ref[...] + swapped * sin_ref[...]

    for h in range(MA_HEADS):
        hs = slice(h * MA_DK, (h + 1) * MA_DK)
        qs[:, hs] = rope(p_ref[:, hs].astype(F32)).astype(BF16)
        ks[:, hs] = rope(p_ref[:, W + h * MA_DK:W + (h + 1) * MA_DK].astype(F32) * MA_KSCALE).astype(BF16)

    def v_block(tb, carry):
        r0 = pl.multiple_of(tb * 128, 128)
        for h in range(MA_HEADS):
            hs = slice(h * MA_DV, (h + 1) * MA_DV)
            blk = p_ref[pl.ds(r0, 128), 2 * W + h * MA_DV:2 * W + (h + 1) * MA_DV].astype(F32).T.astype(BF16)
            for j in range(PER):
                vT[tb * PER + j, h] = blk[:, j * L:(j + 1) * L]
        return carry

    lax.fori_loop(0, T // 128, v_block, 0)

    for d in range(2):
        for h in range(MA_HEADS):
            sidx = d * MA_HEADS + h
            CT[sidx] = c0_ref[d, h].T if latent else jnp.zeros((MA_DV, MA_DK), F32)
            ns[sidx] = n0_ref[sidx:sidx + 1, :] if latent else jnp.zeros((1, MA_DK), F32)
            ms[sidx] = m0_ref[sidx:sidx + 1, :] if latent else jnp.zeros((1, 128), F32)

    low = _tri(L, False)
    upp = _tri(L, True)
    rr = lax.broadcasted_iota(jnp.int32, (L, L), 0)
    cc = lax.broadcasted_iota(jnp.int32, (L, L), 1)
    tri_cols = jnp.concatenate([upp, low], 1)
    tri_rows = jnp.concatenate([low, upp], 0)
    fbc = fbc_ref[...]
    fbr = fbr_ref[...]

    def gate_sums(c, carry):
        t0 = pl.multiple_of(c * L, L)
        gc = gc_ref[pl.ds(t0, L), :]
        lfc = _log_sigmoid(gc + fbc)
        lfr = _log_sigmoid(gt_ref[c] + fbr)
        ish = pltpu.roll(gc, MA_HEADS, 1)
        brow = _dot3(lfr, tri_cols, False)
        bcol = _dot3(tri_rows, lfc, True)
        brs[0, c] = brow[:, :L]
        brs[1, c] = brow[:, L:]
        kcs[0, pl.ds(t0, L), :] = ish - bcol[:L]
        kcs[1, pl.ds(t0, L), :] = ish - bcol[L:]
        return carry

    lax.fori_loop(0, NC, gate_sums, 0, unroll=2)

    def chunk_pair(cf, cb):
        H = MA_HEADS
        G = 2 * H
        heads = range(H)
        cs, ts = (cf, cb), (pl.multiple_of(cf * L, L), pl.multiple_of(cb * L, L))
        rows_of = lambda x, r0: [x[r0 + h:r0 + h + 1, :] for h in heads]
        cols_of = lambda x, c0: [x[:, c0 + h:c0 + h + 1] for h in heads]
        br = jnp.stack(sum([rows_of(brs[d, cs[d]], 2 * d * H + H) for d in range(2)], []), 0)
        ir = jnp.stack(sum([rows_of(gt_ref[cs[d]], 2 * d * H) for d in range(2)], []), 0)
        kcol = jnp.stack(sum([cols_of(kcs[d, pl.ds(ts[d], L), :], 2 * d * H + H) for d in range(2)], []), 0)
        split = lambda x: [x[:, h * MA_DK:(h + 1) * MA_DK] for h in heads]
        q = jnp.stack(split(qs[pl.ds(ts[0], L), :]) + split(qs[pl.ds(ts[1], L), :]), 0)
        k = jnp.stack(split(ks[pl.ds(ts[0], L), :]) + split(ks[pl.ds(ts[1], L), :]), 0)
        vt = jnp.concatenate([vT[cf], vT[cb]], 0)
        m = ms[...][:, :, 0:1]
        n = ns[...]
        ct = CT[...]
        bnt = (((2,), (2,)), ((0,), (0,)))
        bnn = (((2,), (1,)), ((0,), (0,)))
        pre = br + kcol
        dmat = jnp.concatenate([jnp.where(rr <= cc, pre[:H], -jnp.inf), jnp.where(rr >= cc, pre[H:], -jnp.inf)], 0)
        g = br + m
        m_t = jnp.maximum(g, jnp.max(dmat, 1, keepdims=True))
        w_inter = jnp.exp(g - m_t)
        s = lax.dot_general(k, q, bnt, preferred_element_type=F32) * jnp.exp(dmat - m_t)
        ctn = jnp.concatenate([ct.astype(BF16), jnp.broadcast_to(n, (G, 8, MA_DK)).astype(BF16)], 1)
        cq = lax.dot_general(ctn, q, bnt, preferred_element_type=F32)
        num = w_inter * cq[:, :MA_DV] + lax.dot_general(vt, s.astype(BF16), bnn, preferred_element_type=F32)
        den = w_inter * cq[:, MA_DV:MA_DV + 1] + jnp.sum(s, 1, keepdims=True)
        hout = num / jnp.maximum(jnp.abs(den), jnp.exp(-m_t))
        hfT[cf] = hout[:H]
        hbT[cb] = hout[H:]
        last = lambda x: jnp.concatenate([x[:H, :, L - 1:L], x[H:, :, 0:1]], 0)
        m_new = last(m_t)
        b_last = last(br)
        decay = jnp.exp(b_last + m - m_new)
        wk = jnp.exp(b_last - br + ir - m_new)
        wk_hi = wk.astype(BF16)
        wk_lo = (wk - wk_hi.astype(F32)).astype(BF16)
        lhs = jnp.concatenate([(vt.astype(F32) * wk).astype(BF16), wk_hi, wk_lo, jnp.zeros((G, 6, L), BF16)], 1)
        upd = lax.dot_general(lhs, k, bnn, preferred_element_type=F32)
        CT[...] = decay * ct + upd[:, :MA_DV]
        ns[...] = decay * n + upd[:, MA_DV:MA_DV + 1] + upd[:, MA_DV + 1:MA_DV + 2]
        ms[...] = jnp.broadcast_to(m_new, (G, 1, 128))

    def body(i, carry):
        chunk_pair(i, NC - 1 - i)
        return carry

    lax.fori_loop(0, NC, body, 0, unroll=2)

    def out_block(tb, carry):
        r0 = pl.multiple_of(tb * 128, 128)
        hsum = jnp.concatenate([hfT[tb * PER + j] + hbT[tb * PER + j] for j in range(PER)], 2)
        outs = [hsum[h].T for h in range(MA_HEADS)]
        gate = jax.nn.sigmoid(p_ref[pl.ds(r0, 128), 3 * W:4 * W].astype(F32))
        a_ref[pl.ds(r0, 128), :] = (gate * jnp.concatenate(outs, 1)).astype(BF16)
        return carry

    lax.fori_loop(0, T // 128, out_block, 0)
    if not latent:
        for d in range(2):
            for h in range(MA_HEADS):
                sidx = d * MA_HEADS + h
                co_ref[d, h] = CT[sidx].T
                no_ref[sidx:sidx + 1, :] = ns[sidx]
                mo_ref[sidx:sidx + 1, :] = ms[sidx]


def _mlstm(proj, gcol, gt3, fbias_l, l, latent, C0=None, n0=None, m0=None, a_out=None, prev=None):
    T = DEC_SEQ if latent else SEQ
    B = DEC_BATCH if latent else BATCH
    rb0 = N_CTX // DEC_SEQ if latent else 0
    fb = fbias_l.astype(F32)
    fbc = jnp.zeros((1, 128), F32).at[0, MA_HEADS:2 * MA_HEADS].set(fb[0]).at[0, 3 * MA_HEADS:4 * MA_HEADS].set(fb[1])
    fbr = fbc[0, :N_GATES].reshape(N_GATES, 1)
    full2 = lambda b: (0, 0)
    any_spec = pl.BlockSpec(memory_space=pl.ANY)
    in_specs = [pl.BlockSpec((T, 4 * BRANCH_W), lambda b: (rb0 + b, 0)),
                pl.BlockSpec((T, 128), lambda b: (rb0 + b, 0)),
                pl.BlockSpec((T // MA_CHUNK, N_GATES, MA_CHUNK), lambda b: (rb0 + b, 0, 0)),
                pl.BlockSpec((1, 128), full2), pl.BlockSpec((N_GATES, 1), full2)]
    args = [proj, gcol, gt3, fbc, fbr]
    a_shape = jax.ShapeDtypeStruct((N_TOK, BRANCH_W), BF16)
    a_spec = pl.BlockSpec((T, BRANCH_W), lambda b: (rb0 + b, 0))
    c_spec = pl.BlockSpec((None, None, 2, MA_HEADS, MA_DK, MA_DV), lambda b: (b, l, 0, 0, 0, 0))
    nm_spec = pl.BlockSpec((None, None, 2 * MA_HEADS, 128), lambda b: (b, l, 0, 0))
    aliases = {}
    if latent:
        cos, sin = _rope_tables(T)
        nb = 2 * MA_HEADS
        in_specs += [pl.BlockSpec((T, MA_DK), full2), pl.BlockSpec((T, MA_DK), full2), c_spec, nm_spec, nm_spec, any_spec]
        args += [cos, sin, C0, n0.reshape(B, DEPTH, nb, MA_DK),
                 jnp.broadcast_to(m0.reshape(B, DEPTH, nb, 1), (B, DEPTH, nb, 128)), a_out]
        aliases = {len(args) - 1: 0}
        out_shape, out_specs = a_shape, a_spec
    else:
        nm_shape = jax.ShapeDtypeStruct((B, DEPTH, 2 * MA_HEADS, 128), F32)
        out_shape = (a_shape, jax.ShapeDtypeStruct((B, DEPTH, 2, MA_HEADS, MA_DK, MA_DV), F32), nm_shape, nm_shape)
        out_specs = (a_spec, c_spec, nm_spec, nm_spec)
        if prev is not None:
            in_specs += [any_spec] * 3
            args += list(prev)
            aliases = {len(args) - 3: 1, len(args) - 2: 2, len(args) - 1: 3}
    nc = T // MA_CHUNK
    scratch = [pltpu.VMEM((T, BRANCH_W), BF16), pltpu.VMEM((T, BRANCH_W), BF16),
               pltpu.VMEM((nc, MA_HEADS, MA_DV, MA_CHUNK), BF16),
               pltpu.VMEM((nc, MA_HEADS, MA_DV, MA_CHUNK), F32), pltpu.VMEM((nc, MA_HEADS, MA_DV, MA_CHUNK), F32),
               pltpu.VMEM((2 * MA_HEADS, MA_DV, MA_DK), F32), pltpu.VMEM((2 * MA_HEADS, 1, MA_DK), F32),
               pltpu.VMEM((2 * MA_HEADS, 1, 128), F32),
               pltpu.VMEM((2, nc, N_GATES, MA_CHUNK), F32), pltpu.VMEM((2, T, 128), F32)]
    return pl.pallas_call(
        functools.partial(_mlstm_kernel, T=T, latent=latent),
        out_shape=out_shape, grid=(B,), in_specs=in_specs, out_specs=out_specs, scratch_shapes=scratch,
        input_output_aliases=aliases,
        compiler_params=_cparams("arbitrary"),
        name="mlstm_lat" if latent else "mlstm_ctx",
    )(*args)


HG_SUB = 8


def _hgrn_kernel(*refs, T, latent):
    ff_ref, fb_ref, q_ref, i_ref, g_ref, lbf_ref, lbb_ref = refs[:7]
    if latent:
        s0_ref = refs[7]
        c_ref, of, ob, ST, iT, As, Bs = refs[-7:]
    else:
        c_ref, so_ref, of, ob, ST, iT, As, Bs = refs[-8:]
    L = HG_CHUNK
    NC = T // L
    NB = L // HG_SUB
    DK = HG_DK

    for d in range(2):
        for h in range(HG_HEADS):
            ST[d * HG_HEADS + h] = s0_ref[d, h].T if latent else jnp.zeros((HG_DV, DK), F32)

    PER = 128 // L

    def i_block(tb, carry):
        r0 = pl.multiple_of(tb * 128, 128)
        for h in range(HG_HEADS):
            blk = i_ref[pl.ds(r0, 128), h * HG_DV:(h + 1) * HG_DV].astype(F32).T.astype(BF16)
            for j in range(PER):
                iT[tb * PER + j, h] = blk[:, j * L:(j + 1) * L]
        return carry

    lax.fori_loop(0, T // 128, i_block, 0)

    low = _tri(L, False)
    upp = _tri(L, True)
    row8 = lax.broadcasted_iota(jnp.int32, (HG_SUB, L), 0)
    lane_s = lax.broadcasted_iota(jnp.int32, (HG_SUB, L), 1)
    heads = range(HG_HEADS)
    bnt = (((2,), (2,)), ((0,), (0,)))
    bnn = (((2,), (1,)), ((0,), (0,)))
    LOG2E = 1.4426950408889634

    def split(x):
        return jnp.stack([x[:, h * DK:(h + 1) * DK] for h in heads], 0)

    def decay_sums(c, carry):
        t0 = pl.multiple_of(c * L, L)
        for d in range(2):
            fpre = (ff_ref if d == 0 else fb_ref)[pl.ds(t0, L), :]
            lb = (lbf_ref if d == 0 else lbb_ref)[...]
            f = lb + (1.0 - lb) * jax.nn.sigmoid(fpre)
            a = _dot3(low if d == 0 else upp, jnp.log(f) * LOG2E, True)
            As[d, pl.ds(t0, L), :] = a
            Bs[d, pl.ds(t0, L), :] = a - jnp.log(1.0 - f) * LOG2E
        return carry

    lax.fori_loop(0, NC, decay_sums, 0, unroll=4)

    def chunk_pair(cf, cb):
        tf = pl.multiple_of(cf * L, L)
        tb = pl.multiple_of(cb * L, L)
        both = lambda fn: jnp.concatenate([fn(0, tf), fn(1, tb)], 0)
        A = both(lambda d, t: split(As[d, pl.ds(t, L), :]))
        B = both(lambda d, t: split(Bs[d, pl.ds(t, L), :]))
        q = both(lambda d, t: split(_silu(q_ref[pl.ds(t, L), :].astype(F32))))
        iv = both(lambda d, t: split(i_ref[pl.ds(t, L), :].astype(BF16)))
        ivT = jnp.concatenate([iT[cf], iT[cb]], 0)
        H = HG_HEADS
        st = ST[...]
        o = lax.dot_general((q * jnp.exp2(A)).astype(BF16), st.astype(BF16), bnt, preferred_element_type=F32)
        a_last = jnp.concatenate([A[:H, L - 1:L], A[H:, 0:1]], 0)
        kd = jnp.exp2(a_last - B).astype(BF16)
        rows = []
        for I in range(NB):
            lo, hi = I * HG_SUB, (I + 1) * HG_SUB
            A_I, q_I = A[:, lo:hi], q[:, lo:hi]
            att_f = jnp.zeros((H, HG_SUB, L), F32)
            att_b = jnp.zeros((H, HG_SUB, L), F32)
            for j in range(HG_SUB):
                s = lo + j
                col = jnp.sum(q_I * jnp.exp2(A_I - B[:, s:s + 1]), -1, keepdims=True)
                att_f = jnp.where((lane_s == s) & (row8 >= j), col[:H], att_f)
                att_b = jnp.where((lane_s == s) & (row8 <= j), col[H:], att_b)
            rf, rb = max(lo - 1, 0), min(hi, L - 1)
            R = jnp.concatenate([A[:H, rf:rf + 1], A[H:, rb:rb + 1]], 0)
            zeros = lambda n: jnp.zeros((H, n, DK), BF16)
            ksc_f = jnp.concatenate([jnp.exp2(R[:H] - B[:H, :lo]).astype(BF16), zeros(L - lo)], 1) if I > 0 else zeros(L)
            ksc_b = jnp.concatenate([zeros(hi), jnp.exp2(R[H:] - B[H:, hi:]).astype(BF16)], 1) if I < NB - 1 else zeros(L)
            ksc = jnp.concatenate([ksc_f, ksc_b], 0)
            off = lax.dot_general((q_I * jnp.exp2(A_I - R)).astype(BF16), ksc, bnt, preferred_element_type=F32)
            rows.append(jnp.concatenate([att_f, att_b], 0) + off)
        att = jnp.concatenate(rows, 1)
        o = o + lax.dot_general(att.astype(BF16), iv, bnn, preferred_element_type=F32)
        for h in heads:
            of[pl.ds(tf, L), h * HG_DV:(h + 1) * HG_DV] = o[h]
            ob[pl.ds(tb, L), h * HG_DV:(h + 1) * HG_DV] = o[H + h]
        ST[...] = st * jnp.exp2(a_last) + lax.dot_general(ivT, kd, bnn, preferred_element_type=F32)

    def body(i, carry):
        chunk_pair(i, NC - 1 - i)
        return carry

    lax.fori_loop(0, NC, body, 0, unroll=8)

    def epilogue(r, carry):
        t0 = pl.multiple_of(r * 128, 128)
        o = of[pl.ds(t0, 128), :] + ob[pl.ds(t0, 128), :]
        gsil = _silu(g_ref[pl.ds(t0, 128), :].astype(F32))
        outs = []
        for h in range(HG_HEADS):
            oh = o[:, h * HG_DV:(h + 1) * HG_DV]
            outs.append(oh * lax.rsqrt(jnp.mean(oh * oh, -1, keepdims=True) + RMS_EPS))
        c_ref[pl.ds(t0, 128), :] = (jnp.concatenate(outs, -1) * gsil).astype(BF16)
        return carry

    lax.fori_loop(0, T // 128, epilogue, 0)
    if not latent:
        for d in range(2):
            for h in range(HG_HEADS):
                so_ref[d, h] = ST[d * HG_HEADS + h].T


def _hgrn(p16, p32, lb_l, l, latent, S0=None, c_out=None, prev=None):
    T = DEC_SEQ if latent else SEQ
    B = DEC_BATCH if latent else BATCH
    rb0 = N_CTX // DEC_SEQ if latent else 0
    W = BRANCH_W
    full2 = lambda b: (0, 0)
    any_spec = pl.BlockSpec(memory_space=pl.ANY)
    col = lambda j: pl.BlockSpec((T, W), lambda b: (rb0 + b, j))
    s_spec = pl.BlockSpec((None, None, 2, HG_HEADS, HG_DK, HG_DV), lambda b: (b, l, 0, 0, 0, 0))
    in_specs = [col(0), col(1), col(7), col(8), col(9), pl.BlockSpec((1, W), full2), pl.BlockSpec((1, W), full2)]
    args = [p32, p32, p16, p16, p16, lb_l[0][None, :], lb_l[1][None, :]]
    c_shape = jax.ShapeDtypeStruct((N_TOK, W), BF16)
    c_spec = pl.BlockSpec((T, W), lambda b: (rb0 + b, 0))
    aliases = {}
    if latent:
        in_specs += [s_spec, any_spec]
        args += [S0, c_out]
        aliases = {8: 0}
        out_shape, out_specs = c_shape, c_spec
    else:
        out_shape = (c_shape, jax.ShapeDtypeStruct((B, DEPTH, 2, HG_HEADS, HG_DK, HG_DV), F32))
        out_specs = (c_spec, s_spec)
        if prev is not None:
            in_specs.append(any_spec)
            args.append(prev)
            aliases = {7: 1}
    scratch = [pltpu.VMEM((T, W), F32), pltpu.VMEM((T, W), F32), pltpu.VMEM((2 * HG_HEADS, HG_DV, HG_DK), F32),
               pltpu.VMEM((T // HG_CHUNK, HG_HEADS, HG_DV, HG_CHUNK), BF16),
               pltpu.VMEM((2, T, W), F32), pltpu.VMEM((2, T, W), F32)]
    return pl.pallas_call(
        functools.partial(_hgrn_kernel, T=T, latent=latent),
        out_shape=out_shape, grid=(B,), in_specs=in_specs, out_specs=out_specs, scratch_shapes=scratch,
        input_output_aliases=aliases,
        compiler_params=_cparams("arbitrary"),
        name="hgrn_lat" if latent else "hgrn_ctx",
    )(*args)


def _merge_kernel(a_ref, b_ref, c_ref, ga_ref, gb_ref, gc_ref, xc_ref, xl_ref, g1_ref, sh2_ref, sc2_ref,
                  wb_ref, wo_ref, lng_ref, lnb_ref, wr_ref, x1_ref, h2_ref, *, tm):
    def br(v_ref, g_ref, k):
        return jax.nn.sigmoid(g_ref[...].astype(F32)) * jnp.dot(v_ref[...], wb_ref[k], preferred_element_type=F32)

    mix = br(a_ref, ga_ref, 0) + br(b_ref, gb_ref, 1) + br(c_ref, gc_ref, 2)
    y = jnp.dot(mix.astype(BF16), wo_ref[...], preferred_element_type=F32)
    x = _pair_read(pl.program_id(0), tm, xc_ref, xl_ref)
    x1 = _layer_norm(DEEPNORM_ALPHA * x + g1_ref[...] * y, lng_ref[...], lnb_ref[...])
    x1_ref[...] = x1
    h2 = x1 * (1.0 + sc2_ref[...]) + sh2_ref[...]
    h2_ref[:, :D_MODEL] = h2.astype(h2_ref.dtype)
    lt = lax.dot_general(wr_ref[...], h2, _NT, preferred_element_type=F32, precision=HI)
    r = lax.broadcasted_iota(jnp.int32, lt.shape, 0)
    neg = -jnp.inf
    lg = jnp.where(r < N_GROUPS, lt, neg)
    mg = jnp.max(lg, 0, keepdims=True)
    g_sel = jnp.min(jnp.where(lg == mg, r, ROUTER_ROWS), 0, keepdims=True)
    p_sel = 1.0 / jnp.sum(jnp.where(r < N_GROUPS, jnp.exp(lg - mg), 0.0), 0, keepdims=True)
    lo = ROUTER_E0 + EXP_PER_GROUP * g_sel
    le = jnp.where((r >= lo) & (r < lo + EXP_PER_GROUP), lt, neg)
    v1 = jnp.max(le, 0, keepdims=True)
    i1 = jnp.min(jnp.where(le == v1, r, ROUTER_ROWS), 0, keepdims=True)
    le2 = jnp.where(r == i1, neg, le)
    v2 = jnp.max(le2, 0, keepdims=True)
    i2 = jnp.min(jnp.where(le2 == v2, r, ROUTER_ROWS), 0, keepdims=True)
    e2 = jnp.exp(v2 - v1)
    w1 = p_sel / (1.0 + e2)
    w2 = p_sel * e2 / (1.0 + e2)
    w1_hi = w1.astype(BF16).astype(F32)
    w2_hi = w2.astype(BF16).astype(F32)
    j1, j2 = i1 - lo, i2 - lo
    packed = jnp.where(r == j1, w1_hi, jnp.where(r == j2, w2_hi, jnp.where(
        r == j1 + EXP_PER_GROUP, w1 - w1_hi, jnp.where(r == j2 + EXP_PER_GROUP, w2 - w2_hi, jnp.where(
            r == 2 * EXP_PER_GROUP, g_sel.astype(F32), 0.0)))))
    packed = jnp.concatenate([packed, jnp.zeros((128 - ROUTER_ROWS, packed.shape[1]), F32)], 0)
    h2_ref[:, D_MODEL:] = packed.T.astype(h2_ref.dtype)


def _merge(a, b, c, p16, xc, xl, mod, wb, wo, lng, lnb, wr, l):
    tm = 512
    tok = lambda i: (i, 0)
    ln_spec = pl.BlockSpec((None, None, 1, D_MODEL), lambda i: (l, 0, 0, 0))
    return pl.pallas_call(
        functools.partial(_merge_kernel, tm=tm),
        out_shape=(jax.ShapeDtypeStruct((N_TOK, D_MODEL), F32), jax.ShapeDtypeStruct((N_TOK, MOE_XW), MOE_XDT)),
        grid=(N_TOK // tm,),
        in_specs=[pl.BlockSpec((tm, BRANCH_W), tok), pl.BlockSpec((tm, BRANCH_W), tok), pl.BlockSpec((tm, BRANCH_W), tok),
                  pl.BlockSpec((tm, D_MODEL), lambda i: (i, 5)), pl.BlockSpec((tm, D_MODEL), lambda i: (i, 6)),
                  pl.BlockSpec((tm, D_MODEL), lambda i: (i, 7)),
                  *_pair_specs(tm), _mod_spec(l, 2, tm), _mod_spec(l, 3, tm), _mod_spec(l, 4, tm),
                  pl.BlockSpec((None, 3, BRANCH_W, D_MODEL), lambda i: (l, 0, 0, 0)),
                  pl.BlockSpec((None, D_MODEL, D_MODEL), lambda i: (l, 0, 0)),
                  ln_spec, ln_spec,
                  pl.BlockSpec((None, ROUTER_ROWS, D_MODEL), lambda i: (l, 0, 0))],
        out_specs=(pl.BlockSpec((tm, D_MODEL), tok), pl.BlockSpec((tm, MOE_XW), tok)),
        compiler_params=_cparams("arbitrary"),
        name="merge",
    )(a, b, c, p16, p16, p16, xc, xl, mod, mod, mod, wb, wo, lng, lnb, wr)


def _moe_up_kernel(gid_ref, nused_ref, x_ref, w1_ref, w3_ref, hid_ref, w1b, w3b):
    f = pl.program_id(0)
    t = pl.program_id(1)

    @pl.when(t < nused_ref[0])
    def _():
        first = jnp.logical_or(t == 0, gid_ref[t] != gid_ref[jnp.maximum(t - 1, 0)])

        @pl.when(first)
        def _():
            w1b[...] = w1_ref[...].astype(BF16)
            w3b[...] = w3_ref[...].astype(BF16)

        x = x_ref[:, :D_MODEL].astype(BF16)
        rec = x_ref[:, D_MODEL:].astype(F32)
        lane = lax.broadcasted_iota(jnp.int32, rec.shape, 1)
        for j in range(MOE_FE):
            e = f * MOE_FE + j
            a = jnp.dot(x, w1b[j], preferred_element_type=F32)
            b = jnp.dot(x, w3b[j], preferred_element_type=F32)
            gcol = jnp.sum(jnp.where(jnp.logical_or(lane == e, lane == e + EXP_PER_GROUP), rec, 0.0), -1, keepdims=True)
            hid_ref[:, j * D_EXPERT:(j + 1) * D_EXPERT] = (_silu(a) * b * gcol).astype(BF16)


def _moe_tile(t, n):
    return jnp.minimum(t, n[0] - 1)


def _moe_up(gid, nused, xs, w1, w3, l):
    tm = MOE_TM
    npad = MOE_NT * tm
    nf = EXP_PER_GROUP // MOE_FE
    w_spec = pl.BlockSpec((None, MOE_FE, D_MODEL, D_EXPERT),
                          lambda f, t, g, n: (l, nf * g[_moe_tile(t, n)] + f, 0, 0))
    grid_spec = pltpu.PrefetchScalarGridSpec(
        num_scalar_prefetch=2,
        grid=(nf, MOE_NT),
        in_specs=[pl.BlockSpec((tm, MOE_XW), lambda f, t, g, n: (_moe_tile(t, n), 0)), w_spec, w_spec],
        out_specs=pl.BlockSpec((tm, MOE_FE * D_EXPERT), lambda f, t, g, n: (_moe_tile(t, n), f)),
        scratch_shapes=[pltpu.VMEM((MOE_FE, D_MODEL, D_EXPERT), BF16), pltpu.VMEM((MOE_FE, D_MODEL, D_EXPERT), BF16)],
    )
    return pl.pallas_call(
        _moe_up_kernel,
        out_shape=jax.ShapeDtypeStruct((npad, EXP_PER_GROUP * D_EXPERT), BF16),
        grid_spec=grid_spec,
        compiler_params=_cparams("arbitrary", "arbitrary"),
        name="moe_up",
    )(gid, nused, xs, w1, w3)


def _moe_down_kernel(gid_ref, nused_ref, hid_ref, w2_ref, y_ref, w2b):
    t = pl.program_id(0)

    @pl.when(t < nused_ref[0])
    def _():
        first = jnp.logical_or(t == 0, gid_ref[t] != gid_ref[jnp.maximum(t - 1, 0)])

        @pl.when(first)
        def _():
            w2b[...] = w2_ref[...].astype(BF16)

        y_ref[...] = jnp.dot(hid_ref[...], w2b[...], preferred_element_type=F32)


def _moe_down(gid, nused, hid, w2g, l):
    tm = MOE_TM
    npad = MOE_NT * tm
    hw = EXP_PER_GROUP * D_EXPERT
    grid_spec = pltpu.PrefetchScalarGridSpec(
        num_scalar_prefetch=2,
        grid=(MOE_NT,),
        in_specs=[pl.BlockSpec((tm, hw), lambda t, g, n: (_moe_tile(t, n), 0)),
                  pl.BlockSpec((None, None, hw, D_MODEL), lambda t, g, n: (l, g[_moe_tile(t, n)], 0, 0))],
        out_specs=pl.BlockSpec((tm, D_MODEL), lambda t, g, n: (_moe_tile(t, n), 0)),
        scratch_shapes=[pltpu.VMEM((hw, D_MODEL), BF16)],
    )
    return pl.pallas_call(
        _moe_down_kernel,
        out_shape=jax.ShapeDtypeStruct((npad, D_MODEL), F32),
        grid_spec=grid_spec,
        compiler_params=_cparams("arbitrary"),
        name="moe_down",
    )(gid, nused, hid, w2g)


def _moe(h2x, w1, w3, w2g, l):
    tm = MOE_TM
    npad = MOE_NT * tm
    g = h2x[:, D_MODEL + 2 * EXP_PER_GROUP].astype(jnp.int32)
    onehot = (g[:, None] == jnp.arange(N_GROUPS)[None, :]).astype(jnp.int32)
    counts = jnp.sum(onehot, 0)
    rank = jnp.sum((jnp.cumsum(onehot, 0) - onehot) * onehot, 1)
    padded = (counts + tm - 1) // tm * tm
    ends = jnp.cumsum(padded)
    offs = ends - padded
    dest = offs[g] + rank
    src = (jnp.arange(npad, dtype=jnp.int32) % N_TOK).at[dest].set(jnp.arange(N_TOK, dtype=jnp.int32),
                                                                   unique_indices=True)
    starts = jnp.arange(MOE_NT, dtype=jnp.int32) * tm
    tile_gid = jnp.minimum(jnp.sum((ends[None, :] <= starts[:, None]).astype(jnp.int32), 1), N_GROUPS - 1)
    nused = (ends[-1:] // tm).astype(jnp.int32)
    take = lambda arr, idx: arr.at[idx].get(mode="promise_in_bounds", unique_indices=False)
    hid = _moe_up(tile_gid, nused, take(h2x, src), w1, w3, l)
    ys = _moe_down(tile_gid, nused, hid, w2g, l)
    return take(ys, dest)


def _final_kernel(*refs, tm, with_h):
    x1_ref, y_ref, g2_ref, lng_ref, lnb_ref = refs[:5]
    x2 = _layer_norm(DEEPNORM_ALPHA * x1_ref[...] + g2_ref[...] * y_ref[...], lng_ref[...], lnb_ref[...])
    i = pl.program_id(0)
    if with_h:
        sh_ref, sc_ref, xc_ref, xl_ref, h_ref = refs[5:]
        h_ref[...] = (x2 * (1.0 + sc_ref[...]) + sh_ref[...]).astype(BF16)
    else:
        xc_ref, xl_ref = refs[5:]

    @pl.when(i < N_CTX // tm)
    def _():
        xc_ref[...] = x2

    @pl.when(i >= N_CTX // tm)
    def _():
        xl_ref[...] = x2


def _final(x1, y, mod, lng, lnb, l):
    tm = 1024
    tok = lambda i: (i, 0)
    with_h = l + 1 < DEPTH
    ln_spec = pl.BlockSpec((None, None, 1, D_MODEL), lambda i: (l, 1, 0, 0))
    half = jax.ShapeDtypeStruct((N_CTX, D_MODEL), F32)
    in_specs = [pl.BlockSpec((tm, D_MODEL), tok), pl.BlockSpec((tm, D_MODEL), tok), _mod_spec(l, 5, tm), ln_spec, ln_spec]
    args = [x1, y, mod, lng, lnb]
    out_shape = [half, half]
    out_specs = list(_pair_specs(tm))
    if with_h:
        in_specs += [_mod_spec(l + 1, 0, tm), _mod_spec(l + 1, 1, tm)]
        args += [mod, mod]
        out_shape.append(jax.ShapeDtypeStruct((N_TOK, D_MODEL), BF16))
        out_specs.append(pl.BlockSpec((tm, D_MODEL), tok))
    return pl.pallas_call(
        functools.partial(_final_kernel, tm=tm, with_h=with_h),
        out_shape=tuple(out_shape), grid=(N_TOK // tm,), in_specs=in_specs, out_specs=tuple(out_specs),
        compiler_params=_cparams("arbitrary"),
        name="final",
    )(*args)


def kernel(x_prompt, x_sample, c, cache_na_k, cache_na_v, state_mlstm_C, state_mlstm_n, state_mlstm_m, state_hgrn,
           c_ctx, w_mod, b_mod, w_in, b_in, mlstm_fbias, hgrn_lb_logits, na_rpb, w_branch, w_out, ln_g, ln_b,
           w_rg, w_re, w_e1, w_e3, w_e2):
    assert N_CTX == N_LAT
    lb_cum = jnp.cumsum(jax.nn.softmax(hgrn_lb_logits.astype(F32), axis=1), axis=1)
    lb_all = lb_cum - lb_cum[:, :1]

    cs = jnp.zeros((N_MODROWS, D_MODEL), F32).at[0].set(c_ctx).at[1:1 + DEC_BATCH].set(c)
    mod = _modulation(cs, w_mod, b_mod).reshape(DEPTH, N_MODROWS, 6, 1, D_MODEL)

    wb = w_branch.astype(BF16)
    wo = w_out.astype(BF16)
    lng = ln_g.reshape(DEPTH, 2, 1, D_MODEL)
    lnb = ln_b.reshape(DEPTH, 2, 1, D_MODEL)
    wr = jnp.zeros((DEPTH, ROUTER_ROWS, D_MODEL), F32)
    wr = wr.at[:, :N_GROUPS].set(jnp.swapaxes(w_rg, 1, 2)).at[:, ROUTER_E0:ROUTER_E0 + N_EXPERTS].set(jnp.swapaxes(w_re, 1, 2))
    w2g = w_e2.reshape(DEPTH, N_GROUPS, EXP_PER_GROUP * D_EXPERT, D_MODEL)
    b_main = jnp.concatenate([b_in[:, :GATE_COL0], b_in[:, GATE_COL0 + N_GATES:]], 1)

    xc = x_prompt.reshape(N_CTX, D_MODEL)
    xl = x_sample.reshape(N_LAT, D_MODEL)
    w_t = jnp.swapaxes(w_in, 1, 2)
    na_bias = jax.vmap(_na_bias_table)(na_rpb)
    h = _prep(xc, xl, mod)

    kv = (None, None)
    ma_states = None
    hg_state = None
    for l in range(DEPTH):
        p16, p32 = _inproj(h, w_t, b_main[l][None, :], l)
        gcol, gt = _gates(h, w_t, b_in.reshape(DEPTH, 1, N_IN), l)
        gt3 = gt.reshape(N_GATES, N_TOK // MA_CHUNK, MA_CHUNK).transpose(1, 0, 2)

        a, *ma_states = _mlstm(p16, gcol, gt3, mlstm_fbias[l], l, False, prev=ma_states)
        a = _mlstm(p16, gcol, gt3, mlstm_fbias[l], l, True, state_mlstm_C, state_mlstm_n, state_mlstm_m, a_out=a)
        b, *kv = _ctx_attention(p16, l, *kv)
        b = _lat_attention(p16, cache_na_k, cache_na_v, na_bias, l, b)
        cc, hg_state = _hgrn(p16, p32, lb_all[:, l], l, False, prev=hg_state)
        cc = _hgrn(p16, p32, lb_all[:, l], l, True, state_hgrn, c_out=cc)

        x1, h2x = _merge(a, b, cc, p16, xc, xl, mod, wb, wo, lng, lnb, wr, l)
        y2 = _moe(h2x, w_e1, w_e3, w2g, l)
        outs = _final(x1, y2, mod, lng, lnb, l)
        xc, xl = outs[0], outs[1]
        if l + 1 < DEPTH:
            h = outs[2]

    dt = x_prompt.dtype
    new_C, new_n, new_m = ma_states
    new_n = new_n.reshape(BATCH, DEPTH, 2, MA_HEADS, MA_DK)
    new_m = new_m[:, :, :, 0].reshape(BATCH, DEPTH, 2, MA_HEADS)
    return (xc.reshape(BATCH, SEQ, D_MODEL), xl.reshape(DEC_BATCH, DEC_SEQ, D_MODEL), kv[0], kv[1],
            new_C.astype(dt), new_n.astype(dt), new_m.astype(dt), hg_state.astype(dt))
```

```python
import functools

import numpy as np
import jax
import jax.numpy as jnp
from jax import lax
from jax.experimental import pallas as pl
from jax.experimental.pallas import tpu as pltpu

F32 = jnp.float32
BF16 = jnp.bfloat16
HI = lax.Precision.HIGHEST

D_MODEL = 1024
BATCH = 16
SEQ = 256
DEPTH = 2
DEC_BATCH = 4
DEC_SEQ = 1024
PAST_LEN = 256
GRID_W = 64
MA_HEADS = 4
MA_DK = 128
MA_DV = 128
MA_CHUNK = 128
NA_HEADS = 8
NA_DH = 64
NA_KR_MAX = 8
NA_KC = 16
HG_HEADS = 4
HG_DK = 128
HG_DV = 128
HG_CHUNK = 64
BRANCH_W = 512
N_GROUPS = 4
EXP_PER_GROUP = 4
N_EXPERTS = N_GROUPS * EXP_PER_GROUP
D_EXPERT = 512
ROPE_BASE = 10000.0
LN_EPS = 1e-5
RMS_EPS = 1e-6
DEEPNORM_ALPHA = (2 * DEPTH) ** 0.25

N_CTX = BATCH * SEQ
N_LAT = DEC_BATCH * DEC_SEQ
N_TOK = N_CTX + N_LAT
N_MODROWS = 8
GATE_COL0 = 4 * BRANCH_W
N_GATES = 4 * MA_HEADS
N_IN = 9232
P_COLS = N_IN - N_GATES
MOE_TM = 512
MOE_NT = N_TOK // MOE_TM + N_GROUPS
MOE_FE = 4
MOE_XW = D_MODEL + 128
MOE_XDT = F32
ROUTER_ROWS = 32
ROUTER_E0 = 8
VMEM_LIMIT = 56 * 1024 * 1024

_NT = (((1,), (1,)), ((), ()))


def _cparams(*sem):
    return pltpu.CompilerParams(dimension_semantics=sem, vmem_limit_bytes=VMEM_LIMIT)


def _mod_row(tile, tm):
    return jnp.maximum((tile * tm) // DEC_SEQ - (N_CTX // DEC_SEQ - 1), 0)


def _mod_spec(l, part, tm):
    return pl.BlockSpec((None, None, None, 1, D_MODEL), lambda i: (l, _mod_row(i, tm), part, 0, 0))


def _pair_specs(tm):
    nc = N_CTX // tm
    return (pl.BlockSpec((tm, D_MODEL), lambda i: (jnp.minimum(i, nc - 1), 0)),
            pl.BlockSpec((tm, D_MODEL), lambda i: (jnp.maximum(i - nc, 0), 0)))


def _pair_read(i, tm, c_ref, l_ref):
    return jnp.where(i < N_CTX // tm, c_ref[...], l_ref[...])


def _silu(x):
    return x * jax.nn.sigmoid(x)


def _layer_norm(x, g, b):
    mu = jnp.mean(x, -1, keepdims=True)
    xc = x - mu
    var = jnp.mean(xc * xc, -1, keepdims=True)
    return xc * lax.rsqrt(var + LN_EPS) * g + b


def _log_sigmoid(x):
    return jnp.minimum(x, 0.0) - jnp.log(1.0 + jnp.exp(-jnp.abs(x)))


def _tri(n, upper):
    r = lax.broadcasted_iota(jnp.int32, (n, n), 0)
    c = lax.broadcasted_iota(jnp.int32, (n, n), 1)
    return jnp.where((r <= c) if upper else (r >= c), 1.0, 0.0).astype(F32)


def _dot3(a, b, split_b):
    x = b if split_b else a
    hi = x.astype(BF16)
    r = x - hi.astype(F32)
    mid = r.astype(BF16)
    lo = (r - mid.astype(F32)).astype(BF16)
    one = (a if split_b else b).astype(BF16)
    dot = (lambda p: jnp.dot(one, p, preferred_element_type=F32)) if split_b else (
        lambda p: jnp.dot(p, one, preferred_element_type=F32))
    return dot(hi) + dot(mid) + dot(lo)


def _mod_kernel(c_ref, w_ref, b_ref, o_ref):
    s = _silu(c_ref[...])
    o_ref[...] = jnp.dot(s.astype(BF16), w_ref[...].astype(BF16), preferred_element_type=F32) + b_ref[...]


def _modulation(cs, w_mod, b_mod):
    tn = 1024
    return pl.pallas_call(
        _mod_kernel,
        out_shape=jax.ShapeDtypeStruct((DEPTH, N_MODROWS, 6 * D_MODEL), F32),
        grid=(DEPTH, 6 * D_MODEL // tn),
        in_specs=[pl.BlockSpec((N_MODROWS, D_MODEL), lambda l, j: (0, 0)),
                  pl.BlockSpec((None, D_MODEL, tn), lambda l, j: (l, 0, j)),
                  pl.BlockSpec((None, 1, tn), lambda l, j: (l, 0, j))],
        out_specs=pl.BlockSpec((None, N_MODROWS, tn), lambda l, j: (l, 0, j)),
        compiler_params=_cparams("arbitrary", "arbitrary"),
        name="modulation",
    )(cs, w_mod, b_mod.reshape(DEPTH, 1, 6 * D_MODEL))


def _prep_kernel(xc_ref, xl_ref, sh_ref, sc_ref, h_ref, *, tm):
    x = _pair_read(pl.program_id(0), tm, xc_ref, xl_ref)
    h_ref[...] = (x * (1.0 + sc_ref[...]) + sh_ref[...]).astype(BF16)


def _prep(xc, xl, mod):
    tm = 1024
    return pl.pallas_call(
        functools.partial(_prep_kernel, tm=tm),
        out_shape=jax.ShapeDtypeStruct((N_TOK, D_MODEL), BF16),
        grid=(N_TOK // tm,),
        in_specs=[*_pair_specs(tm), _mod_spec(0, 0, tm), _mod_spec(0, 1, tm)],
        out_specs=pl.BlockSpec((tm, D_MODEL), lambda i: (i, 0)),
        compiler_params=_cparams("arbitrary"),
        name="prep",
    )(xc, xl, mod, mod)


INPROJ_TN = 512
N_PLAIN_TILES = GATE_COL0 // INPROJ_TN


F32_TILE0 = 7
N_F32_TILES = 2
P16_COLS = P_COLS - N_F32_TILES * INPROJ_TN


def _inproj_kernel(h_ref, wa_ref, wb_ref, b_ref, o_ref, *, src_tile):
    j = src_tile(pl.program_id(1))

    @pl.when(j < N_PLAIN_TILES)
    def _():
        o_ref[...] = (lax.dot_general(h_ref[...], wa_ref[...].astype(BF16), _NT, preferred_element_type=F32)
                      + b_ref[...]).astype(o_ref.dtype)

    @pl.when(j >= N_PLAIN_TILES)
    def _():
        w = jnp.concatenate([wa_ref[N_GATES:, :], wb_ref[...]], 0)
        o_ref[...] = (lax.dot_general(h_ref[...], w.astype(BF16), _NT, preferred_element_type=F32)
                      + b_ref[...]).astype(o_ref.dtype)


def _inproj_call(h, w_t, b_main, l, tm, n_tiles, src_tile, dtype, name):
    tn = INPROJ_TN
    return pl.pallas_call(
        functools.partial(_inproj_kernel, src_tile=src_tile),
        out_shape=jax.ShapeDtypeStruct((N_TOK, n_tiles * tn), dtype),
        grid=(N_TOK // tm, n_tiles),
        in_specs=[pl.BlockSpec((tm, D_MODEL), lambda i, j: (i, 0)),
                  pl.BlockSpec((None, tn, D_MODEL), lambda i, j: (l, src_tile(j), 0)),
                  pl.BlockSpec((None, N_GATES, D_MODEL), lambda i, j: (l, (src_tile(j) + 1) * (tn // N_GATES), 0)),
                  pl.BlockSpec((1, tn), lambda i, j: (0, src_tile(j)))],
        out_specs=pl.BlockSpec((tm, tn), lambda i, j: (i, j)),
        compiler_params=_cparams("arbitrary", "arbitrary"),
        name=name,
    )(h, w_t, w_t, b_main)


def _inproj(h, w_t, b_main, l):
    skip_f32 = lambda j: jnp.where(j < F32_TILE0, j, j + N_F32_TILES)
    p16 = _inproj_call(h, w_t, b_main, l, 4096, P16_COLS // INPROJ_TN, skip_f32, BF16, "inproj")
    p32 = _inproj_call(h, w_t, b_main, l, 4096, N_F32_TILES, lambda j: j + F32_TILE0, F32, "inproj_f32")
    return p16, p32


def _gates_kernel(h_ref, w_ref, b_ref, gc_ref, gt_ref):
    g = lax.dot_general(h_ref[...], w_ref[...].astype(BF16), _NT, preferred_element_type=F32) + b_ref[...]
    gc_ref[...] = g
    gt_ref[...] = g.T[:N_GATES]


def _gates(h, w_t, b_in3, l):
    tm = 1024
    gblk = GATE_COL0 // 128
    return pl.pallas_call(
        _gates_kernel,
        out_shape=(jax.ShapeDtypeStruct((N_TOK, 128), F32), jax.ShapeDtypeStruct((N_GATES, N_TOK), F32)),
        grid=(N_TOK // tm,),
        in_specs=[pl.BlockSpec((tm, D_MODEL), lambda i: (i, 0)),
                  pl.BlockSpec((None, 128, D_MODEL), lambda i: (l, gblk, 0)),
                  pl.BlockSpec((None, 1, 128), lambda i: (l, 0, gblk))],
        out_specs=(pl.BlockSpec((tm, 128), lambda i: (i, 0)), pl.BlockSpec((N_GATES, tm), lambda i: (0, i))),
        compiler_params=_cparams("arbitrary"),
        name="gates",
    )(h, w_t, b_in3)


HEADS_PER_BLK = 128 // NA_DH
NA_NBLK = NA_HEADS // HEADS_PER_BLK
NA_QSCALE = NA_DH ** -0.5
Q_COL, K_COL, V_COL = 16, 20, 24
QKV_COL = 4


def _ctx_attn_kernel(*refs):
    q_ref, k_ref, v_ref = refs[:3]
    o_ref, ko_ref, vo_ref = refs[-3:]
    heads = range(NA_HEADS)
    split = lambda x: jnp.stack([x[:, h * NA_DH:(h + 1) * NA_DH] for h in heads], 0)
    q = split(q_ref[...] * NA_QSCALE)
    k = split(k_ref[...])
    v = split(v_ref[...])
    ko_ref[...] = k.astype(F32)
    vo_ref[...] = v.astype(F32)
    s = lax.dot_general(q, k, (((2,), (2,)), ((0,), (0,))), preferred_element_type=F32)
    e = jnp.exp(s - jnp.max(s, -1, keepdims=True))
    p = e * (1.0 / jnp.sum(e, -1, keepdims=True))
    o = lax.dot_general(p.astype(BF16), v, (((2,), (1,)), ((0,), (0,))), preferred_element_type=F32)
    o_ref[...] = jnp.concatenate([o[h] for h in heads], -1).astype(BF16)


def _ctx_attention(p16, l, prev_k=None, prev_v=None):
    kv_shape = jax.ShapeDtypeStruct((BATCH, DEPTH, NA_HEADS, SEQ, NA_DH), F32)
    kv_spec = pl.BlockSpec((None, None, NA_HEADS, SEQ, NA_DH), lambda b: (b, l, 0, 0, 0))
    col = lambda j: pl.BlockSpec((SEQ, BRANCH_W), lambda b: (b, j))
    in_specs = [col(QKV_COL), col(QKV_COL + 1), col(QKV_COL + 2)]
    args = [p16, p16, p16]
    aliases = {}
    if prev_k is not None:
        in_specs += [pl.BlockSpec(memory_space=pl.ANY)] * 2
        args += [prev_k, prev_v]
        aliases = {3: 1, 4: 2}
    return pl.pallas_call(
        _ctx_attn_kernel,
        out_shape=(jax.ShapeDtypeStruct((N_TOK, BRANCH_W), BF16), kv_shape, kv_shape),
        grid=(BATCH,),
        in_specs=in_specs,
        out_specs=(pl.BlockSpec((SEQ, BRANCH_W), lambda b: (b, 0)), kv_spec, kv_spec),
        input_output_aliases=aliases,
        compiler_params=_cparams("arbitrary"),
        name="ctx_attention",
    )(*args)


NA_ROWS = DEC_SEQ // GRID_W
NA_KR = min(NA_KR_MAX, NA_ROWS)
NA_QROWS = 4
NA_QT = NA_ROWS // NA_QROWS
NA_WROWS = NA_KR + NA_QROWS - 1
NA_WKEYS = NA_WROWS * GRID_W


def _na_window_start(t):
    return min(max(t * NA_QROWS - NA_KR // 2, 0), NA_ROWS - NA_WROWS)


def _na_bias_table(rpb):
    c = np.arange(GRID_W)
    c0 = np.clip(c - NA_KC // 2, 0, GRID_W - NA_KC)
    kc = np.arange(GRID_W)
    valid = (kc[None, :] >= c0[:, None]) & (kc[None, :] < c0[:, None] + NA_KC)
    dc = kc[None, :] - c[:, None] + NA_KC - 1
    onehot = (dc[None] == np.arange(2 * NA_KC - 1)[:, None, None]) & valid[None]
    toep = jnp.einsum('hrd,dcx->hrcx', rpb.astype(F32), jnp.asarray(onehot, F32), precision=HI)
    toep = jnp.where(valid[None, None], toep, -jnp.inf)
    ninf = jnp.full((NA_HEADS, GRID_W, GRID_W), -jnp.inf, F32)
    tiles = []
    for t in range(NA_QT):
        w0 = _na_window_start(t)
        qrows = []
        for r in range(t * NA_QROWS, (t + 1) * NA_QROWS):
            r0 = min(max(r - NA_KR // 2, 0), NA_ROWS - NA_KR)
            blocks = []
            for kr in range(w0, w0 + NA_WROWS):
                inside = r0 <= kr < r0 + NA_KR
                blocks.append(toep[:, kr - r + NA_KR_MAX - 1] if inside else ninf)
            qrows.append(jnp.concatenate(blocks, -1))
        tiles.append(jnp.concatenate(qrows, 1))
    return jnp.stack(tiles, 1).astype(BF16)


def _lat_attn_kernel(q_ref, k_ref, v_ref, ck_ref, cv_ref, bias_ref, prev_ref, o_ref):
    heads = range(HEADS_PER_BLK)
    split = lambda x: jnp.stack([x[:, h * NA_DH:(h + 1) * NA_DH] for h in heads], 0)
    bnt = (((2,), (2,)), ((0,), (0,)))
    bnn = (((2,), (1,)), ((0,), (0,)))
    q = (q_ref[...] * NA_QSCALE).astype(BF16)
    k = k_ref[...].astype(BF16)
    v = v_ref[...].astype(BF16)
    ck = ck_ref[...].astype(BF16)
    cv = cv_ref[...].astype(BF16)
    nq = NA_QROWS * GRID_W
    for t in range(NA_QT):
        w0 = _na_window_start(t)
        qs = slice(t * nq, (t + 1) * nq)
        ws = slice(w0 * GRID_W, (w0 + NA_WROWS) * GRID_W)
        qh = split(q[qs])
        s_loc = lax.dot_general(qh, split(k[ws]), bnt, preferred_element_type=F32) + bias_ref[:, t]
        s_ctx = lax.dot_general(qh, ck, bnt, preferred_element_type=F32)
        m = jnp.maximum(jnp.max(s_loc, -1, keepdims=True), jnp.max(s_ctx, -1, keepdims=True))
        e_loc = jnp.exp(s_loc - m)
        e_ctx = jnp.exp(s_ctx - m)
        inv = 1.0 / (jnp.sum(e_loc, -1, keepdims=True) + jnp.sum(e_ctx, -1, keepdims=True))
        acc = (lax.dot_general(e_loc.astype(BF16), split(v[ws]), bnn, preferred_element_type=F32)
               + lax.dot_general(e_ctx.astype(BF16), cv, bnn, preferred_element_type=F32)) * inv
        o_ref[qs, :] = jnp.concatenate([acc[h] for h in heads], -1).astype(BF16)


def _lat_attention(proj, ck, cv, bias, l, b_out):
    rb0 = N_CTX // DEC_SEQ
    cb = lambda base: (lambda j, b: (rb0 + b, base + j))
    c_spec = pl.BlockSpec((None, None, HEADS_PER_BLK, PAST_LEN, NA_DH), lambda j, b: (b, l, j, 0, 0))
    return pl.pallas_call(
        _lat_attn_kernel,
        out_shape=jax.ShapeDtypeStruct((N_TOK, BRANCH_W), BF16),
        grid=(NA_NBLK, DEC_BATCH),
        in_specs=[pl.BlockSpec((DEC_SEQ, 128), cb(Q_COL)), pl.BlockSpec((DEC_SEQ, 128), cb(K_COL)),
                  pl.BlockSpec((DEC_SEQ, 128), cb(V_COL)), c_spec, c_spec,
                  pl.BlockSpec((None, HEADS_PER_BLK, NA_QT, NA_QROWS * GRID_W, NA_WKEYS), lambda j, b: (l, j, 0, 0, 0)),
                  pl.BlockSpec(memory_space=pl.ANY)],
        out_specs=pl.BlockSpec((DEC_SEQ, 128), lambda j, b: (rb0 + b, j)),
        input_output_aliases={6: 0},
        compiler_params=_cparams("arbitrary", "arbitrary"),
        name="lat_attention",
    )(proj, proj, proj, ck, cv, bias, b_out)


MA_KSCALE = MA_DK ** -0.5


def _rope_tables(T):
    t = np.arange(T)
    half = MA_DK // 2
    inv = ROPE_BASE ** (-jnp.arange(0, half, 2, dtype=F32) / half)
    ang_r = jnp.asarray((t // GRID_W).astype(np.float32))[:, None] * inv[None, :]
    ang_c = jnp.asarray((t % GRID_W).astype(np.float32))[:, None] * inv[None, :]
    cos = jnp.concatenate([jnp.cos(ang_r)] * 2 + [jnp.cos(ang_c)] * 2, -1)
    sin = jnp.concatenate([-jnp.sin(ang_r), jnp.sin(ang_r), -jnp.sin(ang_c), jnp.sin(ang_c)], -1)
    return cos, sin


def _mlstm_kernel(*refs, T, latent):
    if latent:
        (p_ref, gc_ref, gt_ref, fbc_ref, fbr_ref, cos_ref, sin_ref, c0_ref, n0_ref, m0_ref, prev_ref,
         a_ref, qs, ks, vT, hfT, hbT, CT, ns, ms, brs, kcs) = refs
    else:
        p_ref, gc_ref, gt_ref, fbc_ref, fbr_ref = refs[:5]
        a_ref, co_ref, no_ref, mo_ref, qs, ks, vT, hfT, hbT, CT, ns, ms, brs, kcs = refs[-14:]
    L = MA_CHUNK
    NC = T // L
    W = BRANCH_W
    PER = 128 // L

    lane = lax.broadcasted_iota(jnp.int32, (T, MA_DK), 1)
    lo_half = (lane % (MA_DK // 2)) < (MA_DK // 4)

    def rope(x):
        if not latent:
            return x
        swapped = jnp.where(lo_half, pltpu.roll(x, MA_DK - MA_DK // 4, 1), pltpu.roll(x, MA_DK // 4, 1))
        return x * cos_ref[...] + swapped * sin_ref[...]

    for h in range(MA_HEADS):
        hs = slice(h * MA_DK, (h + 1) * MA_DK)
        qs[:, hs] = rope(p_ref[:, hs].astype(F32)).astype(BF16)
        ks[:, hs] = rope(p_ref[:, W + h * MA_DK:W + (h + 1) * MA_DK].astype(F32) * MA_KSCALE).astype(BF16)

    def v_block(tb, carry):
        r0 = pl.multiple_of(tb * 128, 128)
        for h in range(MA_HEADS):
            hs = slice(h * MA_DV, (h + 1) * MA_DV)
            blk = p_ref[pl.ds(r0, 128), 2 * W + h * MA_DV:2 * W + (h + 1) * MA_DV].astype(F32).T.astype(BF16)
            for j in range(PER):
                vT[tb * PER + j, h] = blk[:, j * L:(j + 1) * L]
        return carry

    lax.fori_loop(0, T // 128, v_block, 0)

    for d in range(2):
        for h in range(MA_HEADS):
            sidx = d * MA_HEADS + h
            CT[sidx] = c0_ref[d, h].T if latent else jnp.zeros((MA_DV, MA_DK), F32)
            ns[sidx] = n0_ref[sidx:sidx + 1, :] if latent else jnp.zeros((1, MA_DK), F32)
            ms[sidx] = m0_ref[sidx:sidx + 1, :] if latent else jnp.zeros((1, 128), F32)

    low = _tri(L, False)
    upp = _tri(L, True)
    rr = lax.broadcasted_iota(jnp.int32, (L, L), 0)
    cc = lax.broadcasted_iota(jnp.int32, (L, L), 1)
    tri_cols = jnp.concatenate([upp, low], 1)
    tri_rows = jnp.concatenate([low, upp], 0)
    fbc = fbc_ref[...]
    fbr = fbr_ref[...]

    def gate_sums(c, carry):
        t0 = pl.multiple_of(c * L, L)
        gc = gc_ref[pl.ds(t0, L), :]
        lfc = _log_sigmoid(gc + fbc)
        lfr = _log_sigmoid(gt_ref[c] + fbr)
        ish = pltpu.roll(gc, MA_HEADS, 1)
        brow = _dot3(lfr, tri_cols, False)
        bcol = _dot3(tri_rows, lfc, True)
        brs[0, c] = brow[:, :L]
        brs[1, c] = brow[:, L:]
        kcs[0, pl.ds(t0, L), :] = ish - bcol[:L]
        kcs[1, pl.ds(t0, L), :] = ish - bcol[L:]
        return carry

    lax.fori_loop(0, NC, gate_sums, 0, unroll=2)

    def chunk_pair(cf, cb):
        H = MA_HEADS
        G = 2 * H
        heads = range(H)
        cs, ts = (cf, cb), (pl.multiple_of(cf * L, L), pl.multiple_of(cb * L, L))
        rows_of = lambda x, r0: [x[r0 + h:r0 + h + 1, :] for h in heads]
        cols_of = lambda x, c0: [x[:, c0 + h:c0 + h + 1] for h in heads]
        br = jnp.stack(sum([rows_of(brs[d, cs[d]], 2 * d * H + H) for d in range(2)], []), 0)
        ir = jnp.stack(sum([rows_of(gt_ref[cs[d]], 2 * d * H) for d in range(2)], []), 0)
        kcol = jnp.stack(sum([cols_of(kcs[d, pl.ds(ts[d], L), :], 2 * d * H + H) for d in range(2)], []), 0)
        split = lambda x: [x[:, h * MA_DK:(h + 1) * MA_DK] for h in heads]
        q = jnp.stack(split(qs[pl.ds(ts[0], L), :]) + split(qs[pl.ds(ts[1], L), :]), 0)
        k = jnp.stack(split(ks[pl.ds(ts[0], L), :]) + split(ks[pl.ds(ts[1], L), :]), 0)
        vt = jnp.concatenate([vT[cf], vT[cb]], 0)
        m = ms[...][:, :, 0:1]
        n = ns[...]
        ct = CT[...]
        bnt = (((2,), (2,)), ((0,), (0,)))
        bnn = (((2,), (1,)), ((0,), (0,)))
        pre = br + kcol
        dmat = jnp.concatenate([jnp.where(rr <= cc, pre[:H], -jnp.inf), jnp.where(rr >= cc, pre[H:], -jnp.inf)], 0)
        g = br + m
        m_t = jnp.maximum(g, jnp.max(dmat, 1, keepdims=True))
        w_inter = jnp.exp(g - m_t)
        s = lax.dot_general(k, q, bnt, preferred_element_type=F32) * jnp.exp(dmat - m_t)
        ctn = jnp.concatenate([ct.astype(BF16), jnp.broadcast_to(n, (G, 8, MA_DK)).astype(BF16)], 1)
        cq = lax.dot_general(ctn, q, bnt, preferred_element_type=F32)
        num = w_inter * cq[:, :MA_DV] + lax.dot_general(vt, s.astype(BF16), bnn, preferred_element_type=F32)
        den = w_inter * cq[:, MA_DV:MA_DV + 1] + jnp.sum(s, 1, keepdims=True)
        hout = num / jnp.maximum(jnp.abs(den), jnp.exp(-m_t))
        hfT[cf] = hout[:H]
        hbT[cb] = hout[H:]
        last = lambda x: jnp.concatenate([x[:H, :, L - 1:L], x[H:, :, 0:1]], 0)
        m_new = last(m_t)
        b_last = last(br)
        decay = jnp.exp(b_last + m - m_new)
        wk = jnp.exp(b_last - br + ir - m_new)
        wk_hi = wk.astype(BF16)
        wk_lo = (wk - wk_hi.astype(F32)).astype(BF16)
        lhs = jnp.concatenate([(vt.astype(F32) * wk).astype(BF16), wk_hi, wk_lo, jnp.zeros((G, 6, L), BF16)], 1)
        upd = lax.dot_general(lhs, k, bnn, preferred_element_type=F32)
        CT[...] = decay * ct + upd[:, :MA_DV]
        ns[...] = decay * n + upd[:, MA_DV:MA_DV + 1] + upd[:, MA_DV + 1:MA_DV + 2]
        ms[...] = jnp.broadcast_to(m_new, (G, 1, 128))

    def body(i, carry):
        chunk_pair(i, NC - 1 - i)
        return carry

    lax.fori_loop(0, NC, body, 0, unroll=2)

    def out_block(tb, carry):
        r0 = pl.multiple_of(tb * 128, 128)
        hsum = jnp.concatenate([hfT[tb * PER + j] + hbT[tb * PER + j] for j in range(PER)], 2)
        outs = [hsum[h].T for h in range(MA_HEADS)]
        gate = jax.nn.sigmoid(p_ref[pl.ds(r0, 128), 3 * W:4 * W].astype(F32))
        a_ref[pl.ds(r0, 128), :] = (gate * jnp.concatenate(outs, 1)).astype(BF16)
        return carry

    lax.fori_loop(0, T // 128, out_block, 0)
    if not latent:
        for d in range(2):
            for h in range(MA_HEADS):
                sidx = d * MA_HEADS + h
                co_ref[d, h] = CT[sidx].T
                no_ref[sidx:sidx + 1, :] = ns[sidx]
                mo_ref[sidx:sidx + 1, :] = ms[sidx]


def _mlstm(proj, gcol, gt3, fbias_l, l, latent, C0=None, n0=None, m0=None, a_out=None, prev=None):
    T = DEC_SEQ if latent else SEQ
    B = DEC_BATCH if latent else BATCH
    rb0 = N_CTX // DEC_SEQ if latent else 0
    fb = fbias_l.astype(F32)
    fbc = jnp.zeros((1, 128), F32).at[0, MA_HEADS:2 * MA_HEADS].set(fb[0]).at[0, 3 * MA_HEADS:4 * MA_HEADS].set(fb[1])
    fbr = fbc[0, :N_GATES].reshape(N_GATES, 1)
    full2 = lambda b: (0, 0)
    any_spec = pl.BlockSpec(memory_space=pl.ANY)
    in_specs = [pl.BlockSpec((T, 4 * BRANCH_W), lambda b: (rb0 + b, 0)),
                pl.BlockSpec((T, 128), lambda b: (rb0 + b, 0)),
                pl.BlockSpec((T // MA_CHUNK, N_GATES, MA_CHUNK), lambda b: (rb0 + b, 0, 0)),
                pl.BlockSpec((1, 128), full2), pl.BlockSpec((N_GATES, 1), full2)]
    args = [proj, gcol, gt3, fbc, fbr]
    a_shape = jax.ShapeDtypeStruct((N_TOK, BRANCH_W), BF16)
    a_spec = pl.BlockSpec((T, BRANCH_W), lambda b: (rb0 + b, 0))
    c_spec = pl.BlockSpec((None, None, 2, MA_HEADS, MA_DK, MA_DV), lambda b: (b, l, 0, 0, 0, 0))
    nm_spec = pl.BlockSpec((None, None, 2 * MA_HEADS, 128), lambda b: (b, l, 0, 0))
    aliases = {}
    if latent:
        cos, sin = _rope_tables(T)
        nb = 2 * MA_HEADS
        in_specs += [pl.BlockSpec((T, MA_DK), full2), pl.BlockSpec((T, MA_DK), full2), c_spec, nm_spec, nm_spec, any_spec]
        args += [cos, sin, C0, n0.reshape(B, DEPTH, nb, MA_DK),
                 jnp.broadcast_to(m0.reshape(B, DEPTH, nb, 1), (B, DEPTH, nb, 128)), a_out]
        aliases = {len(args) - 1: 0}
        out_shape, out_specs = a_shape, a_spec
    else:
        nm_shape = jax.ShapeDtypeStruct((B, DEPTH, 2 * MA_HEADS, 128), F32)
        out_shape = (a_shape, jax.ShapeDtypeStruct((B, DEPTH, 2, MA_HEADS, MA_DK, MA_DV), F32), nm_shape, nm_shape)
        out_specs = (a_spec, c_spec, nm_spec, nm_spec)
        if prev is not None:
            in_specs += [any_spec] * 3
            args += list(prev)
            aliases = {len(args) - 3: 1, len(args) - 2: 2, len(args) - 1: 3}
    nc = T // MA_CHUNK
    scratch = [pltpu.VMEM((T, BRANCH_W), BF16), pltpu.VMEM((T, BRANCH_W), BF16),
               pltpu.VMEM((nc, MA_HEADS, MA_DV, MA_CHUNK), BF16),
               pltpu.VMEM((nc, MA_HEADS, MA_DV, MA_CHUNK), F32), pltpu.VMEM((nc, MA_HEADS, MA_DV, MA_CHUNK), F32),
               pltpu.VMEM((2 * MA_HEADS, MA_DV, MA_DK), F32), pltpu.VMEM((2 * MA_HEADS, 1, MA_DK), F32),
               pltpu.VMEM((2 * MA_HEADS, 1, 128), F32),
               pltpu.VMEM((2, nc, N_GATES, MA_CHUNK), F32), pltpu.VMEM((2, T, 128), F32)]
    return pl.pallas_call(
        functools.partial(_mlstm_kernel, T=T, latent=latent),
        out_shape=out_shape, grid=(B,), in_specs=in_specs, out_specs=out_specs, scratch_shapes=scratch,
        input_output_aliases=aliases,
        compiler_params=_cparams("arbitrary"),
        name="mlstm_lat" if latent else "mlstm_ctx",
    )(*args)


HG_SUB = 8


def _hgrn_kernel(*refs, T, latent):
    ff_ref, fb_ref, q_ref, i_ref, g_ref, lbf_ref, lbb_ref = refs[:7]
    if latent:
        s0_ref = refs[7]
        c_ref, of, ob, ST, iT, As, Bs = refs[-7:]
    else:
        c_ref, so_ref, of, ob, ST, iT, As, Bs = refs[-8:]
    L = HG_CHUNK
    NC = T // L
    NB = L // HG_SUB
    DK = HG_DK

    for d in range(2):
        for h in range(HG_HEADS):
            ST[d * HG_HEADS + h] = s0_ref[d, h].T if latent else jnp.zeros((HG_DV, DK), F32)

    PER = 128 // L

    def i_block(tb, carry):
        r0 = pl.multiple_of(tb * 128, 128)
        for h in range(HG_HEADS):
            blk = i_ref[pl.ds(r0, 128), h * HG_DV:(h + 1) * HG_DV].astype(F32).T.astype(BF16)
            for j in range(PER):
                iT[tb * PER + j, h] = blk[:, j * L:(j + 1) * L]
        return carry

    lax.fori_loop(0, T // 128, i_block, 0)

    low = _tri(L, False)
    upp = _tri(L, True)
    row8 = lax.broadcasted_iota(jnp.int32, (HG_SUB, L), 0)
    lane_s = lax.broadcasted_iota(jnp.int32, (HG_SUB, L), 1)
    heads = range(HG_HEADS)
    bnt = (((2,), (2,)), ((0,), (0,)))
    bnn = (((2,), (1,)), ((0,), (0,)))
    LOG2E = 1.4426950408889634

    def split(x):
        return jnp.stack([x[:, h * DK:(h + 1) * DK] for h in heads], 0)

    def decay_sums(c, carry):
        t0 = pl.multiple_of(c * L, L)
        for d in range(2):
            fpre = (ff_ref if d == 0 else fb_ref)[pl.ds(t0, L), :]
            lb = (lbf_ref if d == 0 else lbb_ref)[...]
            f = lb + (1.0 - lb) * jax.nn.sigmoid(fpre)
            a = _dot3(low if d == 0 else upp, jnp.log(f) * LOG2E, True)
            As[d, pl.ds(t0, L), :] = a
            Bs[d, pl.ds(t0, L), :] = a - jnp.log(1.0 - f) * LOG2E
        return carry

    lax.fori_loop(0, NC, decay_sums, 0, unroll=4)

    def chunk_pair(cf, cb):
        tf = pl.multiple_of(cf * L, L)
        tb = pl.multiple_of(cb * L, L)
        both = lambda fn: jnp.concatenate([fn(0, tf), fn(1, tb)], 0)
        A = both(lambda d, t: split(As[d, pl.ds(t, L), :]))
        B = both(lambda d, t: split(Bs[d, pl.ds(t, L), :]))
        q = both(lambda d, t: split(_silu(q_ref[pl.ds(t, L), :].astype(F32))))
        iv = both(lambda d, t: split(i_ref[pl.ds(t, L), :].astype(BF16)))
        ivT = jnp.concatenate([iT[cf], iT[cb]], 0)
        H = HG_HEADS
        st = ST[...]
        o = lax.dot_general((q * jnp.exp2(A)).astype(BF16), st.astype(BF16), bnt, preferred_element_type=F32)
        a_last = jnp.concatenate([A[:H, L - 1:L], A[H:, 0:1]], 0)
        kd = jnp.exp2(a_last - B).astype(BF16)
        rows = []
        for I in range(NB):
            lo, hi = I * HG_SUB, (I + 1) * HG_SUB
            A_I, q_I = A[:, lo:hi], q[:, lo:hi]
            att_f = jnp.zeros((H, HG_SUB, L), F32)
            att_b = jnp.zeros((H, HG_SUB, L), F32)
            for j in range(HG_SUB):
                s = lo + j
                col = jnp.sum(q_I * jnp.exp2(A_I - B[:, s:s + 1]), -1, keepdims=True)
                att_f = jnp.where((lane_s == s) & (row8 >= j), col[:H], att_f)
                att_b = jnp.where((lane_s == s) & (row8 <= j), col[H:], att_b)
            rf, rb = max(lo - 1, 0), min(hi, L - 1)
            R = jnp.concatenate([A[:H, rf:rf + 1], A[H:, rb:rb + 1]], 0)
            zeros = lambda n: jnp.zeros((H, n, DK), BF16)
            ksc_f = jnp.concatenate([jnp.exp2(R[:H] - B[:H, :lo]).astype(BF16), zeros(L - lo)], 1) if I > 0 else zeros(L)
            ksc_b = jnp.concatenate([zeros(hi), jnp.exp2(R[H:] - B[H:, hi:]).astype(BF16)], 1) if I < NB - 1 else zeros(L)
            ksc = jnp.concatenate([ksc_f, ksc_b], 0)
            off = lax.dot_general((q_I * jnp.exp2(A_I - R)).astype(BF16), ksc, bnt, preferred_element_type=F32)
            rows.append(jnp.concatenate([att_f, att_b], 0) + off)
        att = jnp.concatenate(rows, 1)
        o = o + lax.dot_general(att.astype(BF16), iv, bnn, preferred_element_type=F32)
        for h in heads:
            of[pl.ds(tf, L), h * HG_DV:(h + 1) * HG_DV] = o[h]
            ob[pl.ds(tb, L), h * HG_DV:(h + 1) * HG_DV] = o[H + h]
        ST[...] = st * jnp.exp2(a_last) + lax.dot_general(ivT, kd, bnn, preferred_element_type=F32)

    def body(i, carry):
        chunk_pair(i, NC - 1 - i)
        return carry

    lax.fori_loop(0, NC, body, 0, unroll=4)

    def epilogue(r, carry):
        t0 = pl.multiple_of(r * 128, 128)
        o = of[pl.ds(t0, 128), :] + ob[pl.ds(t0, 128), :]
        gsil = _silu(g_ref[pl.ds(t0, 128), :].astype(F32))
        outs = []
        for h in range(HG_HEADS):
            oh = o[:, h * HG_DV:(h + 1) * HG_DV]
            outs.append(oh * lax.rsqrt(jnp.mean(oh * oh, -1, keepdims=True) + RMS_EPS))
        c_ref[pl.ds(t0, 128), :] = (jnp.concatenate(outs, -1) * gsil).astype(BF16)
        return carry

    lax.fori_loop(0, T // 128, epilogue, 0)
    if not latent:
        for d in range(2):
            for h in range(HG_HEADS):
                so_ref[d, h] = ST[d * HG_HEADS + h].T


def _hgrn(p16, p32, lb_l, l, latent, S0=None, c_out=None, prev=None):
    T = DEC_SEQ if latent else SEQ
    B = DEC_BATCH if latent else BATCH
    rb0 = N_CTX // DEC_SEQ if latent else 0
    W = BRANCH_W
    full2 = lambda b: (0, 0)
    any_spec = pl.BlockSpec(memory_space=pl.ANY)
    col = lambda j: pl.BlockSpec((T, W), lambda b: (rb0 + b, j))
    s_spec = pl.BlockSpec((None, None, 2, HG_HEADS, HG_DK, HG_DV), lambda b: (b, l, 0, 0, 0, 0))
    in_specs = [col(0), col(1), col(7), col(8), col(9), pl.BlockSpec((1, W), full2), pl.BlockSpec((1, W), full2)]
    args = [p32, p32, p16, p16, p16, lb_l[0][None, :], lb_l[1][None, :]]
    c_shape = jax.ShapeDtypeStruct((N_TOK, W), BF16)
    c_spec = pl.BlockSpec((T, W), lambda b: (rb0 + b, 0))
    aliases = {}
    if latent:
        in_specs += [s_spec, any_spec]
        args += [S0, c_out]
        aliases = {8: 0}
        out_shape, out_specs = c_shape, c_spec
    else:
        out_shape = (c_shape, jax.ShapeDtypeStruct((B, DEPTH, 2, HG_HEADS, HG_DK, HG_DV), F32))
        out_specs = (c_spec, s_spec)
        if prev is not None:
            in_specs.append(any_spec)
            args.append(prev)
            aliases = {7: 1}
    scratch = [pltpu.VMEM((T, W), F32), pltpu.VMEM((T, W), F32), pltpu.VMEM((2 * HG_HEADS, HG_DV, HG_DK), F32),
               pltpu.VMEM((T // HG_CHUNK, HG_HEADS, HG_DV, HG_CHUNK), BF16),
               pltpu.VMEM((2, T, W), F32), pltpu.VMEM((2, T, W), F32)]
    return pl.pallas_call(
        functools.partial(_hgrn_kernel, T=T, latent=latent),
        out_shape=out_shape, grid=(B,), in_specs=in_specs, out_specs=out_specs, scratch_shapes=scratch,
        input_output_aliases=aliases,
        compiler_params=_cparams("arbitrary"),
        name="hgrn_lat" if latent else "hgrn_ctx",
    )(*args)


def _merge_kernel(a_ref, b_ref, c_ref, ga_ref, gb_ref, gc_ref, xc_ref, xl_ref, g1_ref, sh2_ref, sc2_ref,
                  wb_ref, wo_ref, lng_ref, lnb_ref, wr_ref, x1_ref, h2_ref, *, tm):
    def br(v_ref, g_ref, k):
        return jax.nn.sigmoid(g_ref[...].astype(F32)) * jnp.dot(v_ref[...], wb_ref[k], preferred_element_type=F32)

    mix = br(a_ref, ga_ref, 0) + br(b_ref, gb_ref, 1) + br(c_ref, gc_ref, 2)
    y = jnp.dot(mix.astype(BF16), wo_ref[...], preferred_element_type=F32)
    x = _pair_read(pl.program_id(0), tm, xc_ref, xl_ref)
    x1 = _layer_norm(DEEPNORM_ALPHA * x + g1_ref[...] * y, lng_ref[...], lnb_ref[...])
    x1_ref[...] = x1
    h2 = x1 * (1.0 + sc2_ref[...]) + sh2_ref[...]
    h2_ref[:, :D_MODEL] = h2.astype(h2_ref.dtype)
    lt = lax.dot_general(wr_ref[...], h2, _NT, preferred_element_type=F32, precision=HI)
    r = lax.broadcasted_iota(jnp.int32, lt.shape, 0)
    neg = -jnp.inf
    lg = jnp.where(r < N_GROUPS, lt, neg)
    mg = jnp.max(lg, 0, keepdims=True)
    g_sel = jnp.min(jnp.where(lg == mg, r, ROUTER_ROWS), 0, keepdims=True)
    p_sel = 1.0 / jnp.sum(jnp.where(r < N_GROUPS, jnp.exp(lg - mg), 0.0), 0, keepdims=True)
    lo = ROUTER_E0 + EXP_PER_GROUP * g_sel
    le = jnp.where((r >= lo) & (r < lo + EXP_PER_GROUP), lt, neg)
    v1 = jnp.max(le, 0, keepdims=True)
    i1 = jnp.min(jnp.where(le == v1, r, ROUTER_ROWS), 0, keepdims=True)
    le2 = jnp.where(r == i1, neg, le)
    v2 = jnp.max(le2, 0, keepdims=True)
    i2 = jnp.min(jnp.where(le2 == v2, r, ROUTER_ROWS), 0, keepdims=True)
    e2 = jnp.exp(v2 - v1)
    w1 = p_sel / (1.0 + e2)
    w2 = p_sel * e2 / (1.0 + e2)
    w1_hi = w1.astype(BF16).astype(F32)
    w2_hi = w2.astype(BF16).astype(F32)
    j1, j2 = i1 - lo, i2 - lo
    packed = jnp.where(r == j1, w1_hi, jnp.where(r == j2, w2_hi, jnp.where(
        r == j1 + EXP_PER_GROUP, w1 - w1_hi, jnp.where(r == j2 + EXP_PER_GROUP, w2 - w2_hi, jnp.where(
            r == 2 * EXP_PER_GROUP, g_sel.astype(F32), 0.0)))))
    packed = jnp.concatenate([packed, jnp.zeros((128 - ROUTER_ROWS, packed.shape[1]), F32)], 0)
    h2_ref[:, D_MODEL:] = packed.T.astype(h2_ref.dtype)


def _merge(a, b, c, p16, xc, xl, mod, wb, wo, lng, lnb, wr, l):
    tm = 512
    tok = lambda i: (i, 0)
    ln_spec = pl.BlockSpec((None, None, 1, D_MODEL), lambda i: (l, 0, 0, 0))
    return pl.pallas_call(
        functools.partial(_merge_kernel, tm=tm),
        out_shape=(jax.ShapeDtypeStruct((N_TOK, D_MODEL), F32), jax.ShapeDtypeStruct((N_TOK, MOE_XW), MOE_XDT)),
        grid=(N_TOK // tm,),
        in_specs=[pl.BlockSpec((tm, BRANCH_W), tok), pl.BlockSpec((tm, BRANCH_W), tok), pl.BlockSpec((tm, BRANCH_W), tok),
                  pl.BlockSpec((tm, D_MODEL), lambda i: (i, 5)), pl.BlockSpec((tm, D_MODEL), lambda i: (i, 6)),
                  pl.BlockSpec((tm, D_MODEL), lambda i: (i, 7)),
                  *_pair_specs(tm), _mod_spec(l, 2, tm), _mod_spec(l, 3, tm), _mod_spec(l, 4, tm),
                  pl.BlockSpec((None, 3, BRANCH_W, D_MODEL), lambda i: (l, 0, 0, 0)),
                  pl.BlockSpec((None, D_MODEL, D_MODEL), lambda i: (l, 0, 0)),
                  ln_spec, ln_spec,
                  pl.BlockSpec((None, ROUTER_ROWS, D_MODEL), lambda i: (l, 0, 0))],
        out_specs=(pl.BlockSpec((tm, D_MODEL), tok), pl.BlockSpec((tm, MOE_XW), tok)),
        compiler_params=_cparams("arbitrary"),
        name="merge",
    )(a, b, c, p16, p16, p16, xc, xl, mod, mod, mod, wb, wo, lng, lnb, wr)


def _moe_up_kernel(gid_ref, nused_ref, x_ref, w1_ref, w3_ref, hid_ref, w1b, w3b):
    f = pl.program_id(0)
    t = pl.program_id(1)

    @pl.when(t < nused_ref[0])
    def _():
        first = jnp.logical_or(t == 0, gid_ref[t] != gid_ref[jnp.maximum(t - 1, 0)])

        @pl.when(first)
        def _():
            w1b[...] = w1_ref[...].astype(BF16)
            w3b[...] = w3_ref[...].astype(BF16)

        x = x_ref[:, :D_MODEL].astype(BF16)
        rec = x_ref[:, D_MODEL:].astype(F32)
        lane = lax.broadcasted_iota(jnp.int32, rec.shape, 1)
        for j in range(MOE_FE):
            e = f * MOE_FE + j
            a = jnp.dot(x, w1b[j], preferred_element_type=F32)
            b = jnp.dot(x, w3b[j], preferred_element_type=F32)
            gcol = jnp.sum(jnp.where(jnp.logical_or(lane == e, lane == e + EXP_PER_GROUP), rec, 0.0), -1, keepdims=True)
            hid_ref[:, j * D_EXPERT:(j + 1) * D_EXPERT] = (_silu(a) * b * gcol).astype(BF16)


def _moe_tile(t, n):
    return jnp.minimum(t, n[0] - 1)


def _moe_up(gid, nused, xs, w1, w3, l):
    tm = MOE_TM
    npad = MOE_NT * tm
    nf = EXP_PER_GROUP // MOE_FE
    w_spec = pl.BlockSpec((None, MOE_FE, D_MODEL, D_EXPERT),
                          lambda f, t, g, n: (l, nf * g[_moe_tile(t, n)] + f, 0, 0))
    grid_spec = pltpu.PrefetchScalarGridSpec(
        num_scalar_prefetch=2,
        grid=(nf, MOE_NT),
        in_specs=[pl.BlockSpec((tm, MOE_XW), lambda f, t, g, n: (_moe_tile(t, n), 0)), w_spec, w_spec],
        out_specs=pl.BlockSpec((tm, MOE_FE * D_EXPERT), lambda f, t, g, n: (_moe_tile(t, n), f)),
        scratch_shapes=[pltpu.VMEM((MOE_FE, D_MODEL, D_EXPERT), BF16), pltpu.VMEM((MOE_FE, D_MODEL, D_EXPERT), BF16)],
    )
    return pl.pallas_call(
        _moe_up_kernel,
        out_shape=jax.ShapeDtypeStruct((npad, EXP_PER_GROUP * D_EXPERT), BF16),
        grid_spec=grid_spec,
        compiler_params=_cparams("arbitrary", "arbitrary"),
        name="moe_up",
    )(gid, nused, xs, w1, w3)


def _moe_down_kernel(gid_ref, nused_ref, hid_ref, w2_ref, y_ref, w2b):
    t = pl.program_id(0)

    @pl.when(t < nused_ref[0])
    def _():
        first = jnp.logical_or(t == 0, gid_ref[t] != gid_ref[jnp.maximum(t - 1, 0)])

        @pl.when(first)
        def _():
            w2b[...] = w2_ref[...].astype(BF16)

        y_ref[...] = jnp.dot(hid_ref[...], w2b[...], preferred_element_type=F32)


def _moe_down(gid, nused, hid, w2g, l):
    tm = MOE_TM
    npad = MOE_NT * tm
    hw = EXP_PER_GROUP * D_EXPERT
    grid_spec = pltpu.PrefetchScalarGridSpec(
        num_scalar_prefetch=2,
        grid=(MOE_NT,),
        in_specs=[pl.BlockSpec((tm, hw), lambda t, g, n: (_moe_tile(t, n), 0)),
                  pl.BlockSpec((None, None, hw, D_MODEL), lambda t, g, n: (l, g[_moe_tile(t, n)], 0, 0))],
        out_specs=pl.BlockSpec((tm, D_MODEL), lambda t, g, n: (_moe_tile(t, n), 0)),
        scratch_shapes=[pltpu.VMEM((hw, D_MODEL), BF16)],
    )
    return pl.pallas_call(
        _moe_down_kernel,
        out_shape=jax.ShapeDtypeStruct((npad, D_MODEL), F32),
        grid_spec=grid_spec,
        compiler_params=_cparams("arbitrary"),
        name="moe_down",
    )(gid, nused, hid, w2g)


def _moe(h2x, w1, w3, w2g, l):
    tm = MOE_TM
    npad = MOE_NT * tm
    g = h2x[:, D_MODEL + 2 * EXP_PER_GROUP].astype(jnp.int32)
    onehot = (g[:, None] == jnp.arange(N_GROUPS)[None, :]).astype(jnp.int32)
    counts = jnp.sum(onehot, 0)
    rank = jnp.sum((jnp.cumsum(onehot, 0) - onehot) * onehot, 1)
    padded = (counts + tm - 1) // tm * tm
    ends = jnp.cumsum(padded)
    offs = ends - padded
    dest = offs[g] + rank
    src = (jnp.arange(npad, dtype=jnp.int32) % N_TOK).at[dest].set(jnp.arange(N_TOK, dtype=jnp.int32),
                                                                   unique_indices=True)
    starts = jnp.arange(MOE_NT, dtype=jnp.int32) * tm
    tile_gid = jnp.minimum(jnp.sum((ends[None, :] <= starts[:, None]).astype(jnp.int32), 1), N_GROUPS - 1)
    nused = (ends[-1:] // tm).astype(jnp.int32)
    take = lambda arr, idx: arr.at[idx].get(mode="promise_in_bounds", unique_indices=False)
    hid = _moe_up(tile_gid, nused, take(h2x, src), w1, w3, l)
    ys = _moe_down(tile_gid, nused, hid, w2g, l)
    return take(ys, dest)


def _final_kernel(*refs, tm, with_h):
    x1_ref, y_ref, g2_ref, lng_ref, lnb_ref = refs[:5]
    x2 = _layer_norm(DEEPNORM_ALPHA * x1_ref[...] + g2_ref[...] * y_ref[...], lng_ref[...], lnb_ref[...])
    i = pl.program_id(0)
    if with_h:
        sh_ref, sc_ref, xc_ref, xl_ref, h_ref = refs[5:]
        h_ref[...] = (x2 * (1.0 + sc_ref[...]) + sh_ref[...]).astype(BF16)
    else:
        xc_ref, xl_ref = refs[5:]

    @pl.when(i < N_CTX // tm)
    def _():
        xc_ref[...] = x2

    @pl.when(i >= N_CTX // tm)
    def _():
        xl_ref[...] = x2


def _final(x1, y, mod, lng, lnb, l):
    tm = 1024
    tok = lambda i: (i, 0)
    with_h = l + 1 < DEPTH
    ln_spec = pl.BlockSpec((None, None, 1, D_MODEL), lambda i: (l, 1, 0, 0))
    half = jax.ShapeDtypeStruct((N_CTX, D_MODEL), F32)
    in_specs = [pl.BlockSpec((tm, D_MODEL), tok), pl.BlockSpec((tm, D_MODEL), tok), _mod_spec(l, 5, tm), ln_spec, ln_spec]
    args = [x1, y, mod, lng, lnb]
    out_shape = [half, half]
    out_specs = list(_pair_specs(tm))
    if with_h:
        in_specs += [_mod_spec(l + 1, 0, tm), _mod_spec(l + 1, 1, tm)]
        args += [mod, mod]
        out_shape.append(jax.ShapeDtypeStruct((N_TOK, D_MODEL), BF16))
        out_specs.append(pl.BlockSpec((tm, D_MODEL), tok))
    return pl.pallas_call(
        functools.partial(_final_kernel, tm=tm, with_h=with_h),
        out_shape=tuple(out_shape), grid=(N_TOK // tm,), in_specs=in_specs, out_specs=tuple(out_specs),
        compiler_params=_cparams("arbitrary"),
        name="final",
    )(*args)


def kernel(x_prompt, x_sample, c, cache_na_k, cache_na_v, state_mlstm_C, state_mlstm_n, state_mlstm_m, state_hgrn,
           c_ctx, w_mod, b_mod, w_in, b_in, mlstm_fbias, hgrn_lb_logits, na_rpb, w_branch, w_out, ln_g, ln_b,
           w_rg, w_re, w_e1, w_e3, w_e2):
    assert N_CTX == N_LAT
    lb_cum = jnp.cumsum(jax.nn.softmax(hgrn_lb_logits.astype(F32), axis=1), axis=1)
    lb_all = lb_cum - lb_cum[:, :1]

    cs = jnp.zeros((N_MODROWS, D_MODEL), F32).at[0].set(c_ctx).at[1:1 + DEC_BATCH].set(c)
    mod = _modulation(cs, w_mod, b_mod).reshape(DEPTH, N_MODROWS, 6, 1, D_MODEL)

    wb = w_branch.astype(BF16)
    wo = w_out.astype(BF16)
    lng = ln_g.reshape(DEPTH, 2, 1, D_MODEL)
    lnb = ln_b.reshape(DEPTH, 2, 1, D_MODEL)
    wr = jnp.zeros((DEPTH, ROUTER_ROWS, D_MODEL), F32)
    wr = wr.at[:, :N_GROUPS].set(jnp.swapaxes(w_rg, 1, 2)).at[:, ROUTER_E0:ROUTER_E0 + N_EXPERTS].set(jnp.swapaxes(w_re, 1, 2))
    w2g = w_e2.reshape(DEPTH, N_GROUPS, EXP_PER_GROUP * D_EXPERT, D_MODEL)
    b_main = jnp.concatenate([b_in[:, :GATE_COL0], b_in[:, GATE_COL0 + N_GATES:]], 1)

    xc = x_prompt.reshape(N_CTX, D_MODEL)
    xl = x_sample.reshape(N_LAT, D_MODEL)
    w_t = jnp.swapaxes(w_in, 1, 2)
    na_bias = jax.vmap(_na_bias_table)(na_rpb)
    h = _prep(xc, xl, mod)

    kv = (None, None)
    ma_states = None
    hg_state = None
    for l in range(DEPTH):
        p16, p32 = _inproj(h, w_t, b_main[l][None, :], l)
        gcol, gt = _gates(h, w_t, b_in.reshape(DEPTH, 1, N_IN), l)
        gt3 = gt.reshape(N_GATES, N_TOK // MA_CHUNK, MA_CHUNK).transpose(1, 0, 2)

        a, *ma_states = _mlstm(p16, gcol, gt3, mlstm_fbias[l], l, False, prev=ma_states)
        a = _mlstm(p16, gcol, gt3, mlstm_fbias[l], l, True, state_mlstm_C, state_mlstm_n, state_mlstm_m, a_out=a)
        b, *kv = _ctx_attention(p16, l, *kv)
        b = _lat_attention(p16, cache_na_k, cache_na_v, na_bias, l, b)
        cc, hg_state = _hgrn(p16, p32, lb_all[:, l], l, False, prev=hg_state)
        cc = _hgrn(p16, p32, lb_all[:, l], l, True, state_hgrn, c_out=cc)

        x1, h2x = _merge(a, b, cc, p16, xc, xl, mod, wb, wo, lng, lnb, wr, l)
        y2 = _moe(h2x, w_e1, w_e3, w2g, l)
        outs = _final(x1, y2, mod, lng, lnb, l)
        xc, xl = outs[0], outs[1]
        if l + 1 < DEPTH:
            h = outs[2]

    dt = x_prompt.dtype
    new_C, new_n, new_m = ma_states
    new_n = new_n.reshape(BATCH, DEPTH, 2, MA_HEADS, MA_DK)
    new_m = new_m[:, :, :, 0].reshape(BATCH, DEPTH, 2, MA_HEADS)
    return (xc.reshape(BATCH, SEQ, D_MODEL), xl.reshape(DEC_BATCH, DEC_SEQ, D_MODEL), kv[0], kv[1],
            new_C.astype(dt), new_n.astype(dt), new_m.astype(dt), hg_state.astype(dt))
```

```python
import functools

import numpy as np
import jax
import jax.numpy as jnp
from jax import lax
from jax.experimental import pallas as pl
from jax.experimental.pallas import tpu as pltpu

F32 = jnp.float32
BF16 = jnp.bfloat16
HI = lax.Precision.HIGHEST

D_MODEL = 1024
BATCH = 16
SEQ = 256
DEPTH = 2
DEC_BATCH = 4
DEC_SEQ = 1024
PAST_LEN = 256
GRID_W = 64
MA_HEADS = 4
MA_DK = 128
MA_DV = 128
MA_CHUNK = 128
NA_HEADS = 8
NA_DH = 64
NA_KR_MAX = 8
NA_KC = 16
HG_HEADS = 4
HG_DK = 128
HG_DV = 128
HG_CHUNK = 64
BRANCH_W = 512
N_GROUPS = 4
EXP_PER_GROUP = 4
N_EXPERTS = N_GROUPS * EXP_PER_GROUP
D_EXPERT = 512
ROPE_BASE = 10000.0
LN_EPS = 1e-5
RMS_EPS = 1e-6
DEEPNORM_ALPHA = (2 * DEPTH) ** 0.25

N_CTX = BATCH * SEQ
N_LAT = DEC_BATCH * DEC_SEQ
N_TOK = N_CTX + N_LAT
N_MODROWS = 8
GATE_COL0 = 4 * BRANCH_W
N_GATES = 4 * MA_HEADS
N_IN = 9232
P_COLS = N_IN - N_GATES
MOE_TM = 512
MOE_NT = N_TOK // MOE_TM + N_GROUPS
MOE_FE = 4
MOE_XW = D_MODEL + 128
MOE_XDT = F32
ROUTER_ROWS = 32
ROUTER_E0 = 8
VMEM_LIMIT = 56 * 1024 * 1024

_NT = (((1,), (1,)), ((), ()))


def _cparams(*sem):
    return pltpu.CompilerParams(dimension_semantics=sem, vmem_limit_bytes=VMEM_LIMIT)


def _mod_row(tile, tm):
    return jnp.maximum((tile * tm) // DEC_SEQ - (N_CTX // DEC_SEQ - 1), 0)


def _mod_spec(l, part, tm):
    return pl.BlockSpec((None, None, None, 1, D_MODEL), lambda i: (l, _mod_row(i, tm), part, 0, 0))


def _pair_specs(tm):
    nc = N_CTX // tm
    return (pl.BlockSpec((tm, D_MODEL), lambda i: (jnp.minimum(i, nc - 1), 0)),
            pl.BlockSpec((tm, D_MODEL), lambda i: (jnp.maximum(i - nc, 0), 0)))


def _pair_read(i, tm, c_ref, l_ref):
    return jnp.where(i < N_CTX // tm, c_ref[...], l_ref[...])


def _silu(x):
    return x * jax.nn.sigmoid(x)


def _layer_norm(x, g, b):
    mu = jnp.mean(x, -1, keepdims=True)
    xc = x - mu
    var = jnp.mean(xc * xc, -1, keepdims=True)
    return xc * lax.rsqrt(var + LN_EPS) * g + b


def _log_sigmoid(x):
    return jnp.minimum(x, 0.0) - jnp.log(1.0 + jnp.exp(-jnp.abs(x)))


def _tri(n, upper):
    r = lax.broadcasted_iota(jnp.int32, (n, n), 0)
    c = lax.broadcasted_iota(jnp.int32, (n, n), 1)
    return jnp.where((r <= c) if upper else (r >= c), 1.0, 0.0).astype(F32)


def _dot3(a, b, split_b):
    x = b if split_b else a
    hi = x.astype(BF16)
    r = x - hi.astype(F32)
    mid = r.astype(BF16)
    lo = (r - mid.astype(F32)).astype(BF16)
    one = (a if split_b else b).astype(BF16)
    dot = (lambda p: jnp.dot(one, p, preferred_element_type=F32)) if split_b else (
        lambda p: jnp.dot(p, one, preferred_element_type=F32))
    return dot(hi) + dot(mid) + dot(lo)


def _mod_kernel(c_ref, w_ref, b_ref, o_ref):
    s = _silu(c_ref[...])
    o_ref[...] = jnp.dot(s.astype(BF16), w_ref[...].astype(BF16), preferred_element_type=F32) + b_ref[...]


def _modulation(cs, w_mod, b_mod):
    tn = 1024
    return pl.pallas_call(
        _mod_kernel,
        out_shape=jax.ShapeDtypeStruct((DEPTH, N_MODROWS, 6 * D_MODEL), F32),
        grid=(DEPTH, 6 * D_MODEL // tn),
        in_specs=[pl.BlockSpec((N_MODROWS, D_MODEL), lambda l, j: (0, 0)),
                  pl.BlockSpec((None, D_MODEL, tn), lambda l, j: (l, 0, j)),
                  pl.BlockSpec((None, 1, tn), lambda l, j: (l, 0, j))],
        out_specs=pl.BlockSpec((None, N_MODROWS, tn), lambda l, j: (l, 0, j)),
        compiler_params=_cparams("arbitrary", "arbitrary"),
        name="modulation",
    )(cs, w_mod, b_mod.reshape(DEPTH, 1, 6 * D_MODEL))


def _prep_kernel(xc_ref, xl_ref, sh_ref, sc_ref, h_ref, *, tm):
    x = _pair_read(pl.program_id(0), tm, xc_ref, xl_ref)
    h_ref[...] = (x * (1.0 + sc_ref[...]) + sh_ref[...]).astype(BF16)


def _prep(xc, xl, mod):
    tm = 1024
    return pl.pallas_call(
        functools.partial(_prep_kernel, tm=tm),
        out_shape=jax.ShapeDtypeStruct((N_TOK, D_MODEL), BF16),
        grid=(N_TOK // tm,),
        in_specs=[*_pair_specs(tm), _mod_spec(0, 0, tm), _mod_spec(0, 1, tm)],
        out_specs=pl.BlockSpec((tm, D_MODEL), lambda i: (i, 0)),
        compiler_params=_cparams("arbitrary"),
        name="prep",
    )(xc, xl, mod, mod)


INPROJ_TN = 512
N_PLAIN_TILES = GATE_COL0 // INPROJ_TN


F32_TILE0 = 7
N_F32_TILES = 2
P16_COLS = P_COLS - N_F32_TILES * INPROJ_TN


def _inproj_kernel(h_ref, wa_ref, wb_ref, b_ref, o_ref, *, src_tile):
    j = src_tile(pl.program_id(1))

    @pl.when(j < N_PLAIN_TILES)
    def _():
        o_ref[...] = (lax.dot_general(h_ref[...], wa_ref[...].astype(BF16), _NT, preferred_element_type=F32)
                      + b_ref[...]).astype(o_ref.dtype)

    @pl.when(j >= N_PLAIN_TILES)
    def _():
        w = jnp.concatenate([wa_ref[N_GATES:, :], wb_ref[...]], 0)
        o_ref[...] = (lax.dot_general(h_ref[...], w.astype(BF16), _NT, preferred_element_type=F32)
                      + b_ref[...]).astype(o_ref.dtype)


def _inproj_call(h, w_t, b_main, l, tm, n_tiles, src_tile, dtype, name):
    tn = INPROJ_TN
    return pl.pallas_call(
        functools.partial(_inproj_kernel, src_tile=src_tile),
        out_shape=jax.ShapeDtypeStruct((N_TOK, n_tiles * tn), dtype),
        grid=(N_TOK // tm, n_tiles),
        in_specs=[pl.BlockSpec((tm, D_MODEL), lambda i, j: (i, 0)),
                  pl.BlockSpec((None, tn, D_MODEL), lambda i, j: (l, src_tile(j), 0)),
                  pl.BlockSpec((None, N_GATES, D_MODEL), lambda i, j: (l, (src_tile(j) + 1) * (tn // N_GATES), 0)),
                  pl.BlockSpec((1, tn), lambda i, j: (0, src_tile(j)))],
        out_specs=pl.BlockSpec((tm, tn), lambda i, j: (i, j)),
        compiler_params=_cparams("arbitrary", "arbitrary"),
        name=name,
    )(h, w_t, w_t, b_main)


def _inproj(h, w_t, b_main, l):
    skip_f32 = lambda j: jnp.where(j < F32_TILE0, j, j + N_F32_TILES)
    p16 = _inproj_call(h, w_t, b_main, l, 4096, P16_COLS // INPROJ_TN, skip_f32, BF16, "inproj")
    p32 = _inproj_call(h, w_t, b_main, l, 4096, N_F32_TILES, lambda j: j + F32_TILE0, F32, "inproj_f32")
    return p16, p32


def _gates_kernel(h_ref, w_ref, b_ref, gc_ref, gt_ref):
    g = lax.dot_general(h_ref[...], w_ref[...].astype(BF16), _NT, preferred_element_type=F32) + b_ref[...]
    gc_ref[...] = g
    gt_ref[...] = g.T[:N_GATES]


def _gates(h, w_t, b_in3, l):
    tm = 1024
    gblk = GATE_COL0 // 128
    return pl.pallas_call(
        _gates_kernel,
        out_shape=(jax.ShapeDtypeStruct((N_TOK, 128), F32), jax.ShapeDtypeStruct((N_GATES, N_TOK), F32)),
        grid=(N_TOK // tm,),
        in_specs=[pl.BlockSpec((tm, D_MODEL), lambda i: (i, 0)),
                  pl.BlockSpec((None, 128, D_MODEL), lambda i: (l, gblk, 0)),
                  pl.BlockSpec((None, 1, 128), lambda i: (l, 0, gblk))],
        out_specs=(pl.BlockSpec((tm, 128), lambda i: (i, 0)), pl.BlockSpec((N_GATES, tm), lambda i: (0, i))),
        compiler_params=_cparams("arbitrary"),
        name="gates",
    )(h, w_t, b_in3)


HEADS_PER_BLK = 128 // NA_DH
NA_NBLK = NA_HEADS // HEADS_PER_BLK
NA_QSCALE = NA_DH ** -0.5
Q_COL, K_COL, V_COL = 16, 20, 24
QKV_COL = 4


def _ctx_attn_kernel(*refs):
    q_ref, k_ref, v_ref = refs[:3]
    o_ref, ko_ref, vo_ref = refs[-3:]
    heads = range(NA_HEADS)
    split = lambda x: jnp.stack([x[:, h * NA_DH:(h + 1) * NA_DH] for h in heads], 0)
    q = split(q_ref[...] * NA_QSCALE)
    k = split(k_ref[...])
    v = split(v_ref[...])
    ko_ref[...] = k.astype(F32)
    vo_ref[...] = v.astype(F32)
    s = lax.dot_general(q, k, (((2,), (2,)), ((0,), (0,))), preferred_element_type=F32)
    e = jnp.exp(s - jnp.max(s, -1, keepdims=True))
    p = e * (1.0 / jnp.sum(e, -1, keepdims=True))
    o = lax.dot_general(p.astype(BF16), v, (((2,), (1,)), ((0,), (0,))), preferred_element_type=F32)
    o_ref[...] = jnp.concatenate([o[h] for h in heads], -1).astype(BF16)


def _ctx_attention(p16, l, prev_k=None, prev_v=None):
    kv_shape = jax.ShapeDtypeStruct((BATCH, DEPTH, NA_HEADS, SEQ, NA_DH), F32)
    kv_spec = pl.BlockSpec((None, None, NA_HEADS, SEQ, NA_DH), lambda b: (b, l, 0, 0, 0))
    col = lambda j: pl.BlockSpec((SEQ, BRANCH_W), lambda b: (b, j))
    in_specs = [col(QKV_COL), col(QKV_COL + 1), col(QKV_COL + 2)]
    args = [p16, p16, p16]
    aliases = {}
    if prev_k is not None:
        in_specs += [pl.BlockSpec(memory_space=pl.ANY)] * 2
        args += [prev_k, prev_v]
        aliases = {3: 1, 4: 2}
    return pl.pallas_call(
        _ctx_attn_kernel,
        out_shape=(jax.ShapeDtypeStruct((N_TOK, BRANCH_W), BF16), kv_shape, kv_shape),
        grid=(BATCH,),
        in_specs=in_specs,
        out_specs=(pl.BlockSpec((SEQ, BRANCH_W), lambda b: (b, 0)), kv_spec, kv_spec),
        input_output_aliases=aliases,
        compiler_params=_cparams("arbitrary"),
        name="ctx_attention",
    )(*args)


NA_ROWS = DEC_SEQ // GRID_W
NA_KR = min(NA_KR_MAX, NA_ROWS)
NA_QROWS = 4
NA_QT = NA_ROWS // NA_QROWS
NA_WROWS = NA_KR + NA_QROWS - 1
NA_WKEYS = NA_WROWS * GRID_W


def _na_window_start(t):
    return min(max(t * NA_QROWS - NA_KR // 2, 0), NA_ROWS - NA_WROWS)


def _na_bias_table(rpb):
    c = np.arange(GRID_W)
    c0 = np.clip(c - NA_KC // 2, 0, GRID_W - NA_KC)
    kc = np.arange(GRID_W)
    valid = (kc[None, :] >= c0[:, None]) & (kc[None, :] < c0[:, None] + NA_KC)
    dc = kc[None, :] - c[:, None] + NA_KC - 1
    onehot = (dc[None] == np.arange(2 * NA_KC - 1)[:, None, None]) & valid[None]
    toep = jnp.einsum('hrd,dcx->hrcx', rpb.astype(F32), jnp.asarray(onehot, F32), precision=HI)
    toep = jnp.where(valid[None, None], toep, -jnp.inf)
    ninf = jnp.full((NA_HEADS, GRID_W, GRID_W), -jnp.inf, F32)
    tiles = []
    for t in range(NA_QT):
        w0 = _na_window_start(t)
        qrows = []
        for r in range(t * NA_QROWS, (t + 1) * NA_QROWS):
            r0 = min(max(r - NA_KR // 2, 0), NA_ROWS - NA_KR)
            blocks = []
            for kr in range(w0, w0 + NA_WROWS):
                inside = r0 <= kr < r0 + NA_KR
                blocks.append(toep[:, kr - r + NA_KR_MAX - 1] if inside else ninf)
            qrows.append(jnp.concatenate(blocks, -1))
        tiles.append(jnp.concatenate(qrows, 1))
    return jnp.stack(tiles, 1).astype(BF16)


def _lat_attn_kernel(q_ref, k_ref, v_ref, ck_ref, cv_ref, bias_ref, prev_ref, o_ref):
    heads = range(HEADS_PER_BLK)
    split = lambda x: jnp.stack([x[:, h * NA_DH:(h + 1) * NA_DH] for h in heads], 0)
    bnt = (((2,), (2,)), ((0,), (0,)))
    bnn = (((2,), (1,)), ((0,), (0,)))
    q = (q_ref[...] * NA_QSCALE).astype(BF16)
    k = k_ref[...].astype(BF16)
    v = v_ref[...].astype(BF16)
    ck = ck_ref[...].astype(BF16)
    cv = cv_ref[...].astype(BF16)
    nq = NA_QROWS * GRID_W
    for t in range(NA_QT):
        w0 = _na_window_start(t)
        qs = slice(t * nq, (t + 1) * nq)
        ws = slice(w0 * GRID_W, (w0 + NA_WROWS) * GRID_W)
        qh = split(q[qs])
        s_loc = lax.dot_general(qh, split(k[ws]), bnt, preferred_element_type=F32) + bias_ref[:, t]
        s_ctx = lax.dot_general(qh, ck, bnt, preferred_element_type=F32)
        m = jnp.maximum(jnp.max(s_loc, -1, keepdims=True), jnp.max(s_ctx, -1, keepdims=True))
        e_loc = jnp.exp(s_loc - m)
        e_ctx = jnp.exp(s_ctx - m)
        inv = 1.0 / (jnp.sum(e_loc, -1, keepdims=True) + jnp.sum(e_ctx, -1, keepdims=True))
        acc = (lax.dot_general(e_loc.astype(BF16), split(v[ws]), bnn, preferred_element_type=F32)
               + lax.dot_general(e_ctx.astype(BF16), cv, bnn, preferred_element_type=F32)) * inv
        o_ref[qs, :] = jnp.concatenate([acc[h] for h in heads], -1).astype(BF16)


def _lat_attention(proj, ck, cv, bias, l, b_out):
    rb0 = N_CTX // DEC_SEQ
    cb = lambda base: (lambda j, b: (rb0 + b, base + j))
    c_spec = pl.BlockSpec((None, None, HEADS_PER_BLK, PAST_LEN, NA_DH), lambda j, b: (b, l, j, 0, 0))
    return pl.pallas_call(
        _lat_attn_kernel,
        out_shape=jax.ShapeDtypeStruct((N_TOK, BRANCH_W), BF16),
        grid=(NA_NBLK, DEC_BATCH),
        in_specs=[pl.BlockSpec((DEC_SEQ, 128), cb(Q_COL)), pl.BlockSpec((DEC_SEQ, 128), cb(K_COL)),
                  pl.BlockSpec((DEC_SEQ, 128), cb(V_COL)), c_spec, c_spec,
                  pl.BlockSpec((None, HEADS_PER_BLK, NA_QT, NA_QROWS * GRID_W, NA_WKEYS), lambda j, b: (l, j, 0, 0, 0)),
                  pl.BlockSpec(memory_space=pl.ANY)],
        out_specs=pl.BlockSpec((DEC_SEQ, 128), lambda j, b: (rb0 + b, j)),
        input_output_aliases={6: 0},
        compiler_params=_cparams("arbitrary", "arbitrary"),
        name="lat_attention",
    )(proj, proj, proj, ck, cv, bias, b_out)


MA_KSCALE = MA_DK ** -0.5


def _rope_tables(T):
    t = np.arange(T)
    half = MA_DK // 2
    inv = ROPE_BASE ** (-jnp.arange(0, half, 2, dtype=F32) / half)
    ang_r = jnp.asarray((t // GRID_W).astype(np.float32))[:, None] * inv[None, :]
    ang_c = jnp.asarray((t % GRID_W).astype(np.float32))[:, None] * inv[None, :]
    cos = jnp.concatenate([jnp.cos(ang_r)] * 2 + [jnp.cos(ang_c)] * 2, -1)
    sin = jnp.concatenate([-jnp.sin(ang_r), jnp.sin(ang_r), -jnp.sin(ang_c), jnp.sin(ang_c)], -1)
    return cos, sin


def _mlstm_kernel(*refs, T, latent):
    if latent:
        (p_ref, gc_ref, gt_ref, fbc_ref, fbr_ref, cos_ref, sin_ref, c0_ref, n0_ref, m0_ref, prev_ref,
         a_ref, qs, ks, vT, hfT, hbT, CT, ns, ms, brs, kcs) = refs
    else:
        p_ref, gc_ref, gt_ref, fbc_ref, fbr_ref = refs[:5]
        a_ref, co_ref, no_ref, mo_ref, qs, ks, vT, hfT, hbT, CT, ns, ms, brs, kcs = refs[-14:]
    L = MA_CHUNK
    NC = T // L
    W = BRANCH_W
    PER = 128 // L

    lane = lax.broadcasted_iota(jnp.int32, (T, MA_DK), 1)
    lo_half = (lane % (MA_DK // 2)) < (MA_DK // 4)

    def rope(x):
        if not latent:
            return x
        swapped = jnp.where(lo_half, pltpu.roll(x, MA_DK - MA_DK // 4, 1), pltpu.roll(x, MA_DK // 4, 1))
        return x * cos_ref[...] + swapped * sin_ref[...]

    for h in range(MA_HEADS):
        hs = slice(h * MA_DK, (h + 1) * MA_DK)
        qs[:, hs] = rope(p_ref[:, hs].astype(F32)).astype(BF16)
        ks[:, hs] = rope(p_ref[:, W + h * MA_DK:W + (h + 1) * MA_DK].astype(F32) * MA_KSCALE).astype(BF16)

    def v_block(tb, carry):
        r0 = pl.multiple_of(tb * 128, 128)
        for h in range(MA_HEADS):
            hs = slice(h * MA_DV, (h + 1) * MA_DV)
            blk = p_ref[pl.ds(r0, 128), 2 * W + h * MA_DV:2 * W + (h + 1) * MA_DV].astype(F32).T.astype(BF16)
            for j in range(PER):
                vT[tb * PER + j, h] = blk[:, j * L:(j + 1) * L]
        return carry

    lax.fori_loop(0, T // 128, v_block, 0)

    for d in range(2):
        for h in range(MA_HEADS):
            sidx = d * MA_HEADS + h
            CT[sidx] = c0_ref[d, h].T if latent else jnp.zeros((MA_DV, MA_DK), F32)
            ns[sidx] = n0_ref[sidx:sidx + 1, :] if latent else jnp.zeros((1, MA_DK), F32)
            ms[sidx] = m0_ref[sidx:sidx + 1, :] if latent else jnp.zeros((1, 128), F32)

    low = _tri(L, False)
    upp = _tri(L, True)
    rr = lax.broadcasted_iota(jnp.int32, (L, L), 0)
    cc = lax.broadcasted_iota(jnp.int32, (L, L), 1)
    tri_cols = jnp.concatenate([upp, low], 1)
    tri_rows = jnp.concatenate([low, upp], 0)
    fbc = fbc_ref[...]
    fbr = fbr_ref[...]

    def gate_sums(c, carry):
        t0 = pl.multiple_of(c * L, L)
        gc = gc_ref[pl.ds(t0, L), :]
        lfc = _log_sigmoid(gc + fbc)
        lfr = _log_sigmoid(gt_ref[c] + fbr)
        ish = pltpu.roll(gc, MA_HEADS, 1)
        brow = _dot3(lfr, tri_cols, False)
        bcol = _dot3(tri_rows, lfc, True)
        brs[0, c] = brow[:, :L]
        brs[1, c] = brow[:, L:]
        kcs[0, pl.ds(t0, L), :] = ish - bcol[:L]
        kcs[1, pl.ds(t0, L), :] = ish - bcol[L:]
        return carry

    lax.fori_loop(0, NC, gate_sums, 0, unroll=2)

    def chunk_pair(cf, cb):
        H = MA_HEADS
        G = 2 * H
        heads = range(H)
        cs, ts = (cf, cb), (pl.multiple_of(cf * L, L), pl.multiple_of(cb * L, L))
        rows_of = lambda x, r0: [x[r0 + h:r0 + h + 1, :] for h in heads]
        cols_of = lambda x, c0: [x[:, c0 + h:c0 + h + 1] for h in heads]
        br = jnp.stack(sum([rows_of(brs[d, cs[d]], 2 * d * H + H) for d in range(2)], []), 0)
        ir = jnp.stack(sum([rows_of(gt_ref[cs[d]], 2 * d * H) for d in range(2)], []), 0)
        kcol = jnp.stack(sum([cols_of(kcs[d, pl.ds(ts[d], L), :], 2 * d * H + H) for d in range(2)], []), 0)
        split = lambda x: [x[:, h * MA_DK:(h + 1) * MA_DK] for h in heads]
        q = jnp.stack(split(qs[pl.ds(ts[0], L), :]) + split(qs[pl.ds(ts[1], L), :]), 0)
        k = jnp.stack(split(ks[pl.ds(ts[0], L), :]) + split(ks[pl.ds(ts[1], L), :]), 0)
        vt = jnp.concatenate([vT[cf], vT[cb]], 0)
        m = ms[...][:, :, 0:1]
        n = ns[...]
        ct = CT[...]
        bnt = (((2,), (2,)), ((0,), (0,)))
        bnn = (((2,), (1,)), ((0,), (0,)))
        pre = br + kcol
        dmat = jnp.concatenate([jnp.where(rr <= cc, pre[:H], -jnp.inf), jnp.where(rr >= cc, pre[H:], -jnp.inf)], 0)
        g = br + m
        m_t = jnp.maximum(g, jnp.max(dmat, 1, keepdims=True))
        w_inter = jnp.exp(g - m_t)
        s = lax.dot_general(k, q, bnt, preferred_element_type=F32) * jnp.exp(dmat - m_t)
        ctn = jnp.concatenate([ct.astype(BF16), jnp.broadcast_to(n, (G, 8, MA_DK)).astype(BF16)], 1)
        cq = lax.dot_general(ctn, q, bnt, preferred_element_type=F32)
        num = w_inter * cq[:, :MA_DV] + lax.dot_general(vt, s.astype(BF16), bnn, preferred_element_type=F32)
        den = w_inter * cq[:, MA_DV:MA_DV + 1] + jnp.sum(s, 1, keepdims=True)
        hout = num / jnp.maximum(jnp.abs(den), jnp.exp(-m_t))
        hfT[cf] = hout[:H]
        hbT[cb] = hout[H:]
        last = lambda x: jnp.concatenate([x[:H, :, L - 1:L], x[H:, :, 0:1]], 0)
        m_new = last(m_t)
        b_last = last(br)
        decay = jnp.exp(b_last + m - m_new)
        wk = jnp.exp(b_last - br + ir - m_new)
        wk_hi = wk.astype(BF16)
        wk_lo = (wk - wk_hi.astype(F32)).astype(BF16)
        lhs = jnp.concatenate([(vt.astype(F32) * wk).astype(BF16), wk_hi, wk_lo, jnp.zeros((G, 6, L), BF16)], 1)
        upd = lax.dot_general(lhs, k, bnn, preferred_element_type=F32)
        CT[...] = decay * ct + upd[:, :MA_DV]
        ns[...] = decay * n + upd[:, MA_DV:MA_DV + 1] + upd[:, MA_DV + 1:MA_DV + 2]
        ms[...] = jnp.broadcast_to(m_new, (G, 1, 128))

    def body(i, carry):
        chunk_pair(i, NC - 1 - i)
        return carry

    lax.fori_loop(0, NC, body, 0, unroll=min(4, NC))

    def out_block(tb, carry):
        r0 = pl.multiple_of(tb * 128, 128)
        hsum = jnp.concatenate([hfT[tb * PER + j] + hbT[tb * PER + j] for j in range(PER)], 2)
        outs = [hsum[h].T for h in range(MA_HEADS)]
        gate = jax.nn.sigmoid(p_ref[pl.ds(r0, 128), 3 * W:4 * W].astype(F32))
        a_ref[pl.ds(r0, 128), :] = (gate * jnp.concatenate(outs, 1)).astype(BF16)
        return carry

    lax.fori_loop(0, T // 128, out_block, 0)
    if not latent:
        for d in range(2):
            for h in range(MA_HEADS):
                sidx = d * MA_HEADS + h
                co_ref[d, h] = CT[sidx].T
                no_ref[sidx:sidx + 1, :] = ns[sidx]
                mo_ref[sidx:sidx + 1, :] = ms[sidx]


def _mlstm(proj, gcol, gt3, fbias_l, l, latent, C0=None, n0=None, m0=None, a_out=None, prev=None):
    T = DEC_SEQ if latent else SEQ
    B = DEC_BATCH if latent else BATCH
    rb0 = N_CTX // DEC_SEQ if latent else 0
    fb = fbias_l.astype(F32)
    fbc = jnp.zeros((1, 128), F32).at[0, MA_HEADS:2 * MA_HEADS].set(fb[0]).at[0, 3 * MA_HEADS:4 * MA_HEADS].set(fb[1])
    fbr = fbc[0, :N_GATES].reshape(N_GATES, 1)
    full2 = lambda b: (0, 0)
    any_spec = pl.BlockSpec(memory_space=pl.ANY)
    in_specs = [pl.BlockSpec((T, 4 * BRANCH_W), lambda b: (rb0 + b, 0)),
                pl.BlockSpec((T, 128), lambda b: (rb0 + b, 0)),
                pl.BlockSpec((T // MA_CHUNK, N_GATES, MA_CHUNK), lambda b: (rb0 + b, 0, 0)),
                pl.BlockSpec((1, 128), full2), pl.BlockSpec((N_GATES, 1), full2)]
    args = [proj, gcol, gt3, fbc, fbr]
    a_shape = jax.ShapeDtypeStruct((N_TOK, BRANCH_W), BF16)
    a_spec = pl.BlockSpec((T, BRANCH_W), lambda b: (rb0 + b, 0))
    c_spec = pl.BlockSpec((None, None, 2, MA_HEADS, MA_DK, MA_DV), lambda b: (b, l, 0, 0, 0, 0))
    nm_spec = pl.BlockSpec((None, None, 2 * MA_HEADS, 128), lambda b: (b, l, 0, 0))
    aliases = {}
    if latent:
        cos, sin = _rope_tables(T)
        nb = 2 * MA_HEADS
        in_specs += [pl.BlockSpec((T, MA_DK), full2), pl.BlockSpec((T, MA_DK), full2), c_spec, nm_spec, nm_spec, any_spec]
        args += [cos, sin, C0, n0.reshape(B, DEPTH, nb, MA_DK),
                 jnp.broadcast_to(m0.reshape(B, DEPTH, nb, 1), (B, DEPTH, nb, 128)), a_out]
        aliases = {len(args) - 1: 0}
        out_shape, out_specs = a_shape, a_spec
    else:
        nm_shape = jax.ShapeDtypeStruct((B, DEPTH, 2 * MA_HEADS, 128), F32)
        out_shape = (a_shape, jax.ShapeDtypeStruct((B, DEPTH, 2, MA_HEADS, MA_DK, MA_DV), F32), nm_shape, nm_shape)
        out_specs = (a_spec, c_spec, nm_spec, nm_spec)
        if prev is not None:
            in_specs += [any_spec] * 3
            args += list(prev)
            aliases = {len(args) - 3: 1, len(args) - 2: 2, len(args) - 1: 3}
    nc = T // MA_CHUNK
    scratch = [pltpu.VMEM((T, BRANCH_W), BF16), pltpu.VMEM((T, BRANCH_W), BF16),
               pltpu.VMEM((nc, MA_HEADS, MA_DV, MA_CHUNK), BF16),
               pltpu.VMEM((nc, MA_HEADS, MA_DV, MA_CHUNK), F32), pltpu.VMEM((nc, MA_HEADS, MA_DV, MA_CHUNK), F32),
               pltpu.VMEM((2 * MA_HEADS, MA_DV, MA_DK), F32), pltpu.VMEM((2 * MA_HEADS, 1, MA_DK), F32),
               pltpu.VMEM((2 * MA_HEADS, 1, 128), F32),
               pltpu.VMEM((2, nc, N_GATES, MA_CHUNK), F32), pltpu.VMEM((2, T, 128), F32)]
    return pl.pallas_call(
        functools.partial(_mlstm_kernel, T=T, latent=latent),
        out_shape=out_shape, grid=(B,), in_specs=in_specs, out_specs=out_specs, scratch_shapes=scratch,
        input_output_aliases=aliases,
        compiler_params=_cparams("arbitrary"),
        name="mlstm_lat" if latent else "mlstm_ctx",
    )(*args)


HG_SUB = 8


def _hgrn_kernel(*refs, T, latent):
    ff_ref, fb_ref, q_ref, i_ref, g_ref, lbf_ref, lbb_ref = refs[:7]
    if latent:
        s0_ref = refs[7]
        c_ref, of, ob, ST, iT, As, Bs = refs[-7:]
    else:
        c_ref, so_ref, of, ob, ST, iT, As, Bs = refs[-8:]
    L = HG_CHUNK
    NC = T // L
    NB = L // HG_SUB
    DK = HG_DK

    for d in range(2):
        for h in range(HG_HEADS):
            ST[d * HG_HEADS + h] = s0_ref[d, h].T if latent else jnp.zeros((HG_DV, DK), F32)

    PER = 128 // L

    def i_block(tb, carry):
        r0 = pl.multiple_of(tb * 128, 128)
        for h in range(HG_HEADS):
            blk = i_ref[pl.ds(r0, 128), h * HG_DV:(h + 1) * HG_DV].astype(F32).T.astype(BF16)
            for j in range(PER):
                iT[tb * PER + j, h] = blk[:, j * L:(j + 1) * L]
        return carry

    lax.fori_loop(0, T // 128, i_block, 0)

    low = _tri(L, False)
    upp = _tri(L, True)
    row8 = lax.broadcasted_iota(jnp.int32, (HG_SUB, L), 0)
    lane_s = lax.broadcasted_iota(jnp.int32, (HG_SUB, L), 1)
    heads = range(HG_HEADS)
    bnt = (((2,), (2,)), ((0,), (0,)))
    bnn = (((2,), (1,)), ((0,), (0,)))
    LOG2E = 1.4426950408889634

    def split(x):
        return jnp.stack([x[:, h * DK:(h + 1) * DK] for h in heads], 0)

    def decay_sums(c, carry):
        t0 = pl.multiple_of(c * L, L)
        for d in range(2):
            fpre = (ff_ref if d == 0 else fb_ref)[pl.ds(t0, L), :]
            lb = (lbf_ref if d == 0 else lbb_ref)[...]
            f = lb + (1.0 - lb) * jax.nn.sigmoid(fpre)
            a = _dot3(low if d == 0 else upp, jnp.log(f) * LOG2E, True)
            As[d, pl.ds(t0, L), :] = a
            Bs[d, pl.ds(t0, L), :] = a - jnp.log(1.0 - f) * LOG2E
        return carry

    lax.fori_loop(0, NC, decay_sums, 0, unroll=4)

    def chunk_pair(cf, cb):
        tf = pl.multiple_of(cf * L, L)
        tb = pl.multiple_of(cb * L, L)
        both = lambda fn: jnp.concatenate([fn(0, tf), fn(1, tb)], 0)
        A = both(lambda d, t: split(As[d, pl.ds(t, L), :]))
        B = both(lambda d, t: split(Bs[d, pl.ds(t, L), :]))
        q = both(lambda d, t: split(_silu(q_ref[pl.ds(t, L), :].astype(F32))))
        iv = both(lambda d, t: split(i_ref[pl.ds(t, L), :].astype(BF16)))
        ivT = jnp.concatenate([iT[cf], iT[cb]], 0)
        H = HG_HEADS
        st = ST[...]
        o = lax.dot_general((q * jnp.exp2(A)).astype(BF16), st.astype(BF16), bnt, preferred_element_type=F32)
        a_last = jnp.concatenate([A[:H, L - 1:L], A[H:, 0:1]], 0)
        kd = jnp.exp2(a_last - B).astype(BF16)
        rows = []
        for I in range(NB):
            lo, hi = I * HG_SUB, (I + 1) * HG_SUB
            A_I, q_I = A[:, lo:hi], q[:, lo:hi]
            att_f = jnp.zeros((H, HG_SUB, L), F32)
            att_b = jnp.zeros((H, HG_SUB, L), F32)
            for j in range(HG_SUB):
                s = lo + j
                col = jnp.sum(q_I * jnp.exp2(A_I - B[:, s:s + 1]), -1, keepdims=True)
                att_f = jnp.where((lane_s == s) & (row8 >= j), col[:H], att_f)
                att_b = jnp.where((lane_s == s) & (row8 <= j), col[H:], att_b)
            rf, rb = max(lo - 1, 0), min(hi, L - 1)
            R = jnp.concatenate([A[:H, rf:rf + 1], A[H:, rb:rb + 1]], 0)
            zeros = lambda n: jnp.zeros((H, n, DK), BF16)
            ksc_f = jnp.concatenate([jnp.exp2(R[:H] - B[:H, :lo]).astype(BF16), zeros(L - lo)], 1) if I > 0 else zeros(L)
            ksc_b = jnp.concatenate([zeros(hi), jnp.exp2(R[H:] - B[H:, hi:]).astype(BF16)], 1) if I < NB - 1 else zeros(L)
            ksc = jnp.concatenate([ksc_f, ksc_b], 0)
            off = lax.dot_general((q_I * jnp.exp2(A_I - R)).astype(BF16), ksc, bnt, preferred_element_type=F32)
            rows.append(jnp.concatenate([att_f, att_b], 0) + off)
        att = jnp.concatenate(rows, 1)
        o = o + lax.dot_general(att.astype(BF16), iv, bnn, preferred_element_type=F32)
        for h in heads:
            of[pl.ds(tf, L), h * HG_DV:(h + 1) * HG_DV] = o[h]
            ob[pl.ds(tb, L), h * HG_DV:(h + 1) * HG_DV] = o[H + h]
        ST[...] = st * jnp.exp2(a_last) + lax.dot_general(ivT, kd, bnn, preferred_element_type=F32)

    def body(i, carry):
        chunk_pair(i, NC - 1 - i)
        return carry

    lax.fori_loop(0, NC, body, 0, unroll=min(8, NC))

    def epilogue(r, carry):
        t0 = pl.multiple_of(r * 128, 128)
        o = of[pl.ds(t0, 128), :] + ob[pl.ds(t0, 128), :]
        gsil = _silu(g_ref[pl.ds(t0, 128), :].astype(F32))
        outs = []
        for h in range(HG_HEADS):
            oh = o[:, h * HG_DV:(h + 1) * HG_DV]
            outs.append(oh * lax.rsqrt(jnp.mean(oh * oh, -1, keepdims=True) + RMS_EPS))
        c_ref[pl.ds(t0, 128), :] = (jnp.concatenate(outs, -1) * gsil).astype(BF16)
        return carry

    lax.fori_loop(0, T // 128, epilogue, 0)
    if not latent:
        for d in range(2):
            for h in range(HG_HEADS):
                so_ref[d, h] = ST[d * HG_HEADS + h].T


def _hgrn(p16, p32, lb_l, l, latent, S0=None, c_out=None, prev=None):
    T = DEC_SEQ if latent else SEQ
    B = DEC_BATCH if latent else BATCH
    rb0 = N_CTX // DEC_SEQ if latent else 0
    W = BRANCH_W
    full2 = lambda b: (0, 0)
    any_spec = pl.BlockSpec(memory_space=pl.ANY)
    col = lambda j: pl.BlockSpec((T, W), lambda b: (rb0 + b, j))
    s_spec = pl.BlockSpec((None, None, 2, HG_HEADS, HG_DK, HG_DV), lambda b: (b, l, 0, 0, 0, 0))
    in_specs = [col(0), col(1), col(7), col(8), col(9), pl.BlockSpec((1, W), full2), pl.BlockSpec((1, W), full2)]
    args = [p32, p32, p16, p16, p16, lb_l[0][None, :], lb_l[1][None, :]]
    c_shape = jax.ShapeDtypeStruct((N_TOK, W), BF16)
    c_spec = pl.BlockSpec((T, W), lambda b: (rb0 + b, 0))
    aliases = {}
    if latent:
        in_specs += [s_spec, any_spec]
        args += [S0, c_out]
        aliases = {8: 0}
        out_shape, out_specs = c_shape, c_spec
    else:
        out_shape = (c_shape, jax.ShapeDtypeStruct((B, DEPTH, 2, HG_HEADS, HG_DK, HG_DV), F32))
        out_specs = (c_spec, s_spec)
        if prev is not None:
            in_specs.append(any_spec)
            args.append(prev)
            aliases = {7: 1}
    scratch = [pltpu.VMEM((T, W), F32), pltpu.VMEM((T, W), F32), pltpu.VMEM((2 * HG_HEADS, HG_DV, HG_DK), F32),
               pltpu.VMEM((T // HG_CHUNK, HG_HEADS, HG_DV, HG_CHUNK), BF16),
               pltpu.VMEM((2, T, W), F32), pltpu.VMEM((2, T, W), F32)]
    return pl.pallas_call(
        functools.partial(_hgrn_kernel, T=T, latent=latent),
        out_shape=out_shape, grid=(B,), in_specs=in_specs, out_specs=out_specs, scratch_shapes=scratch,
        input_output_aliases=aliases,
        compiler_params=_cparams("arbitrary"),
        name="hgrn_lat" if latent else "hgrn_ctx",
    )(*args)


def _merge_kernel(a_ref, b_ref, c_ref, ga_ref, gb_ref, gc_ref, xc_ref, xl_ref, g1_ref, sh2_ref, sc2_ref,
                  wb_ref, wo_ref, lng_ref, lnb_ref, wr_ref, x1_ref, h2_ref, *, tm):
    def br(v_ref, g_ref, k):
        return jax.nn.sigmoid(g_ref[...].astype(F32)) * jnp.dot(v_ref[...], wb_ref[k], preferred_element_type=F32)

    mix = br(a_ref, ga_ref, 0) + br(b_ref, gb_ref, 1) + br(c_ref, gc_ref, 2)
    y = jnp.dot(mix.astype(BF16), wo_ref[...], preferred_element_type=F32)
    x = _pair_read(pl.program_id(0), tm, xc_ref, xl_ref)
    x1 = _layer_norm(DEEPNORM_ALPHA * x + g1_ref[...] * y, lng_ref[...], lnb_ref[...])
    x1_ref[...] = x1
    h2 = x1 * (1.0 + sc2_ref[...]) + sh2_ref[...]
    h2_ref[:, :D_MODEL] = h2.astype(h2_ref.dtype)
    lt = lax.dot_general(wr_ref[...], h2, _NT, preferred_element_type=F32, precision=HI)
    r = lax.broadcasted_iota(jnp.int32, lt.shape, 0)
    neg = -jnp.inf
    lg = jnp.where(r < N_GROUPS, lt, neg)
    mg = jnp.max(lg, 0, keepdims=True)
    g_sel = jnp.min(jnp.where(lg == mg, r, ROUTER_ROWS), 0, keepdims=True)
    p_sel = 1.0 / jnp.sum(jnp.where(r < N_GROUPS, jnp.exp(lg - mg), 0.0), 0, keepdims=True)
    lo = ROUTER_E0 + EXP_PER_GROUP * g_sel
    le = jnp.where((r >= lo) & (r < lo + EXP_PER_GROUP), lt, neg)
    v1 = jnp.max(le, 0, keepdims=True)
    i1 = jnp.min(jnp.where(le == v1, r, ROUTER_ROWS), 0, keepdims=True)
    le2 = jnp.where(r == i1, neg, le)
    v2 = jnp.max(le2, 0, keepdims=True)
    i2 = jnp.min(jnp.where(le2 == v2, r, ROUTER_ROWS), 0, keepdims=True)
    e2 = jnp.exp(v2 - v1)
    w1 = p_sel / (1.0 + e2)
    w2 = p_sel * e2 / (1.0 + e2)
    w1_hi = w1.astype(BF16).astype(F32)
    w2_hi = w2.astype(BF16).astype(F32)
    j1, j2 = i1 - lo, i2 - lo
    packed = jnp.where(r == j1, w1_hi, jnp.where(r == j2, w2_hi, jnp.where(
        r == j1 + EXP_PER_GROUP, w1 - w1_hi, jnp.where(r == j2 + EXP_PER_GROUP, w2 - w2_hi, jnp.where(
            r == 2 * EXP_PER_GROUP, g_sel.astype(F32), 0.0)))))
    packed = jnp.concatenate([packed, jnp.zeros((128 - ROUTER_ROWS, packed.shape[1]), F32)], 0)
    h2_ref[:, D_MODEL:] = packed.T.astype(h2_ref.dtype)


def _merge(a, b, c, p16, xc, xl, mod, wb, wo, lng, lnb, wr, l):
    tm = 512
    tok = lambda i: (i, 0)
    ln_spec = pl.BlockSpec((None, None, 1, D_MODEL), lambda i: (l, 0, 0, 0))
    return pl.pallas_call(
        functools.partial(_merge_kernel, tm=tm),
        out_shape=(jax.ShapeDtypeStruct((N_TOK, D_MODEL), F32), jax.ShapeDtypeStruct((N_TOK, MOE_XW), MOE_XDT)),
        grid=(N_TOK // tm,),
        in_specs=[pl.BlockSpec((tm, BRANCH_W), tok), pl.BlockSpec((tm, BRANCH_W), tok), pl.BlockSpec((tm, BRANCH_W), tok),
                  pl.BlockSpec((tm, D_MODEL), lambda i: (i, 5)), pl.BlockSpec((tm, D_MODEL), lambda i: (i, 6)),
                  pl.BlockSpec((tm, D_MODEL), lambda i: (i, 7)),
                  *_pair_specs(tm), _mod_spec(l, 2, tm), _mod_spec(l, 3, tm), _mod_spec(l, 4, tm),
                  pl.BlockSpec((None, 3, BRANCH_W, D_MODEL), lambda i: (l, 0, 0, 0)),
                  pl.BlockSpec((None, D_MODEL, D_MODEL), lambda i: (l, 0, 0)),
                  ln_spec, ln_spec,
                  pl.BlockSpec((None, ROUTER_ROWS, D_MODEL), lambda i: (l, 0, 0))],
        out_specs=(pl.BlockSpec((tm, D_MODEL), tok), pl.BlockSpec((tm, MOE_XW), tok)),
        compiler_params=_cparams("arbitrary"),
        name="merge",
    )(a, b, c, p16, p16, p16, xc, xl, mod, mod, mod, wb, wo, lng, lnb, wr)


def _moe_up_kernel(gid_ref, nused_ref, x_ref, w1_ref, w3_ref, hid_ref, w1b, w3b):
    f = pl.program_id(0)
    t = pl.program_id(1)

    @pl.when(t < nused_ref[0])
    def _():
        first = jnp.logical_or(t == 0, gid_ref[t] != gid_ref[jnp.maximum(t - 1, 0)])

        @pl.when(first)
        def _():
            w1b[...] = w1_ref[...].astype(BF16)
            w3b[...] = w3_ref[...].astype(BF16)

        x = x_ref[:, :D_MODEL].astype(BF16)
        rec = x_ref[:, D_MODEL:].astype(F32)
        lane = lax.broadcasted_iota(jnp.int32, rec.shape, 1)
        for j in range(MOE_FE):
            e = f * MOE_FE + j
            a = jnp.dot(x, w1b[j], preferred_element_type=F32)
            b = jnp.dot(x, w3b[j], preferred_element_type=F32)
            gcol = jnp.sum(jnp.where(jnp.logical_or(lane == e, lane == e + EXP_PER_GROUP), rec, 0.0), -1, keepdims=True)
            hid_ref[:, j * D_EXPERT:(j + 1) * D_EXPERT] = (_silu(a) * b * gcol).astype(BF16)


def _moe_tile(t, n):
    return jnp.minimum(t, n[0] - 1)


def _moe_up(gid, nused, xs, w1, w3, l):
    tm = MOE_TM
    npad = MOE_NT * tm
    nf = EXP_PER_GROUP // MOE_FE
    w_spec = pl.BlockSpec((None, MOE_FE, D_MODEL, D_EXPERT),
                          lambda f, t, g, n: (l, nf * g[_moe_tile(t, n)] + f, 0, 0))
    grid_spec = pltpu.PrefetchScalarGridSpec(
        num_scalar_prefetch=2,
        grid=(nf, MOE_NT),
        in_specs=[pl.BlockSpec((tm, MOE_XW), lambda f, t, g, n: (_moe_tile(t, n), 0)), w_spec, w_spec],
        out_specs=pl.BlockSpec((tm, MOE_FE * D_EXPERT), lambda f, t, g, n: (_moe_tile(t, n), f)),
        scratch_shapes=[pltpu.VMEM((MOE_FE, D_MODEL, D_EXPERT), BF16), pltpu.VMEM((MOE_FE, D_MODEL, D_EXPERT), BF16)],
    )
    return pl.pallas_call(
        _moe_up_kernel,
        out_shape=jax.ShapeDtypeStruct((npad, EXP_PER_GROUP * D_EXPERT), BF16),
        grid_spec=grid_spec,
        compiler_params=_cparams("arbitrary", "arbitrary"),
        name="moe_up",
    )(gid, nused, xs, w1, w3)


def _moe_down_kernel(gid_ref, nused_ref, hid_ref, w2_ref, y_ref, w2b):
    t = pl.program_id(0)

    @pl.when(t < nused_ref[0])
    def _():
        first = jnp.logical_or(t == 0, gid_ref[t] != gid_ref[jnp.maximum(t - 1, 0)])

        @pl.when(first)
        def _():
            w2b[...] = w2_ref[...].astype(BF16)

        y_ref[...] = jnp.dot(hid_ref[...], w2b[...], preferred_element_type=F32)


def _moe_down(gid, nused, hid, w2g, l):
    tm = MOE_TM
    npad = MOE_NT * tm
    hw = EXP_PER_GROUP * D_EXPERT
    grid_spec = pltpu.PrefetchScalarGridSpec(
        num_scalar_prefetch=2,
        grid=(MOE_NT,),
        in_specs=[pl.BlockSpec((tm, hw), lambda t, g, n: (_moe_tile(t, n), 0)),
                  pl.BlockSpec((None, None, hw, D_MODEL), lambda t, g, n: (l, g[_moe_tile(t, n)], 0, 0))],
        out_specs=pl.BlockSpec((tm, D_MODEL), lambda t, g, n: (_moe_tile(t, n), 0)),
        scratch_shapes=[pltpu.VMEM((hw, D_MODEL), BF16)],
    )
    return pl.pallas_call(
        _moe_down_kernel,
        out_shape=jax.ShapeDtypeStruct((npad, D_MODEL), F32),
        grid_spec=grid_spec,
        compiler_params=_cparams("arbitrary"),
        name="moe_down",
    )(gid, nused, hid, w2g)


def _moe(h2x, w1, w3, w2g, l):
    tm = MOE_TM
    npad = MOE_NT * tm
    g = h2x[:, D_MODEL + 2 * EXP_PER_GROUP].astype(jnp.int32)
    onehot = (g[:, None] == jnp.arange(N_GROUPS)[None, :]).astype(jnp.int32)
    counts = jnp.sum(onehot, 0)
    rank = jnp.sum((jnp.cumsum(onehot, 0) - onehot) * onehot, 1)
    padded = (counts + tm - 1) // tm * tm
    ends = jnp.cumsum(padded)
    offs = ends - padded
    dest = offs[g] + rank
    src = (jnp.arange(npad, dtype=jnp.int32) % N_TOK).at[dest].set(jnp.arange(N_TOK, dtype=jnp.int32),
                                                                   unique_indices=True)
    starts = jnp.arange(MOE_NT, dtype=jnp.int32) * tm
    tile_gid = jnp.minimum(jnp.sum((ends[None, :] <= starts[:, None]).astype(jnp.int32), 1), N_GROUPS - 1)
    nused = (ends[-1:] // tm).astype(jnp.int32)
    take = lambda arr, idx: arr.at[idx].get(mode="promise_in_bounds", unique_indices=False)
    hid = _moe_up(tile_gid, nused, take(h2x, src), w1, w3, l)
    ys = _moe_down(tile_gid, nused, hid, w2g, l)
    return take(ys, dest)


def _final_kernel(*refs, tm, with_h):
    x1_ref, y_ref, g2_ref, lng_ref, lnb_ref = refs[:5]
    x2 = _layer_norm(DEEPNORM_ALPHA * x1_ref[...] + g2_ref[...] * y_ref[...], lng_ref[...], lnb_ref[...])
    i = pl.program_id(0)
    if with_h:
        sh_ref, sc_ref, xc_ref, xl_ref, h_ref = refs[5:]
        h_ref[...] = (x2 * (1.0 + sc_ref[...]) + sh_ref[...]).astype(BF16)
    else:
        xc_ref, xl_ref = refs[5:]

    @pl.when(i < N_CTX // tm)
    def _():
        xc_ref[...] = x2

    @pl.when(i >= N_CTX // tm)
    def _():
        xl_ref[...] = x2


def _final(x1, y, mod, lng, lnb, l):
    tm = 1024
    tok = lambda i: (i, 0)
    with_h = l + 1 < DEPTH
    ln_spec = pl.BlockSpec((None, None, 1, D_MODEL), lambda i: (l, 1, 0, 0))
    half = jax.ShapeDtypeStruct((N_CTX, D_MODEL), F32)
    in_specs = [pl.BlockSpec((tm, D_MODEL), tok), pl.BlockSpec((tm, D_MODEL), tok), _mod_spec(l, 5, tm), ln_spec, ln_spec]
    args = [x1, y, mod, lng, lnb]
    out_shape = [half, half]
    out_specs = list(_pair_specs(tm))
    if with_h:
        in_specs += [_mod_spec(l + 1, 0, tm), _mod_spec(l + 1, 1, tm)]
        args += [mod, mod]
        out_shape.append(jax.ShapeDtypeStruct((N_TOK, D_MODEL), BF16))
        out_specs.append(pl.BlockSpec((tm, D_MODEL), tok))
    return pl.pallas_call(
        functools.partial(_final_kernel, tm=tm, with_h=with_h),
        out_shape=tuple(out_shape), grid=(N_TOK // tm,), in_specs=in_specs, out_specs=tuple(out_specs),
        compiler_params=_cparams("arbitrary"),
        name="final",
    )(*args)


def kernel(x_prompt, x_sample, c, cache_na_k, cache_na_v, state_mlstm_C, state_mlstm_n, state_mlstm_m, state_hgrn,
           c_ctx, w_mod, b_mod, w_in, b_in, mlstm_fbias, hgrn_lb_logits, na_rpb, w_branch, w_out, ln_g, ln_b,
           w_rg, w_re, w_e1, w_e3, w_e2):
    assert N_CTX == N_LAT
    lb_cum = jnp.cumsum(jax.nn.softmax(hgrn_lb_logits.astype(F32), axis=1), axis=1)
    lb_all = lb_cum - lb_cum[:, :1]

    cs = jnp.zeros((N_MODROWS, D_MODEL), F32).at[0].set(c_ctx).at[1:1 + DEC_BATCH].set(c)
    mod = _modulation(cs, w_mod, b_mod).reshape(DEPTH, N_MODROWS, 6, 1, D_MODEL)

    wb = w_branch.astype(BF16)
    wo = w_out.astype(BF16)
    lng = ln_g.reshape(DEPTH, 2, 1, D_MODEL)
    lnb = ln_b.reshape(DEPTH, 2, 1, D_MODEL)
    wr = jnp.zeros((DEPTH, ROUTER_ROWS, D_MODEL), F32)
    wr = wr.at[:, :N_GROUPS].set(jnp.swapaxes(w_rg, 1, 2)).at[:, ROUTER_E0:ROUTER_E0 + N_EXPERTS].set(jnp.swapaxes(w_re, 1, 2))
    w2g = w_e2.reshape(DEPTH, N_GROUPS, EXP_PER_GROUP * D_EXPERT, D_MODEL)
    b_main = jnp.concatenate([b_in[:, :GATE_COL0], b_in[:, GATE_COL0 + N_GATES:]], 1)

    xc = x_prompt.reshape(N_CTX, D_MODEL)
    xl = x_sample.reshape(N_LAT, D_MODEL)
    w_t = jnp.swapaxes(w_in, 1, 2)
    na_bias = jax.vmap(_na_bias_table)(na_rpb)
    h = _prep(xc, xl, mod)

    kv = (None, None)
    ma_states = None
    hg_state = None
    for l in range(DEPTH):
        p16, p32 = _inproj(h, w_t, b_main[l][None, :], l)
        gcol, gt = _gates(h, w_t, b_in.reshape(DEPTH, 1, N_IN), l)
        gt3 = gt.reshape(N_GATES, N_TOK // MA_CHUNK, MA_CHUNK).transpose(1, 0, 2)

        a, *ma_states = _mlstm(p16, gcol, gt3, mlstm_fbias[l], l, False, prev=ma_states)
        a = _mlstm(p16, gcol, gt3, mlstm_fbias[l], l, True, state_mlstm_C, state_mlstm_n, state_mlstm_m, a_out=a)
        b, *kv = _ctx_attention(p16, l, *kv)
        b = _lat_attention(p16, cache_na_k, cache_na_v, na_bias, l, b)
        cc, hg_state = _hgrn(p16, p32, lb_all[:, l], l, False, prev=hg_state)
        cc = _hgrn(p16, p32, lb_all[:, l], l, True, state_hgrn, c_out=cc)

        x1, h2x = _merge(a, b, cc, p16, xc, xl, mod, wb, wo, lng, lnb, wr, l)
        y2 = _moe(h2x, w_e1, w_e3, w2g, l)
        outs = _final(x1, y2, mod, lng, lnb, l)
        xc, xl = outs[0], outs[1]
        if l + 1 < DEPTH:
            h = outs[2]

    dt = x_prompt.dtype
    new_C, new_n, new_m = ma_states
    new_n = new_n.reshape(BATCH, DEPTH, 2, MA_HEADS, MA_DK)
    new_m = new_m[:, :, :, 0].reshape(BATCH, DEPTH, 2, MA_HEADS)
    return (xc.reshape(BATCH, SEQ, D_MODEL), xl.reshape(DEC_BATCH, DEC_SEQ, D_MODEL), kv[0], kv[1],
            new_C.astype(dt), new_n.astype(dt), new_m.astype(dt), hg_state.astype(dt))
```

```python
import functools

import numpy as np
import jax
import jax.numpy as jnp
from jax import lax
from jax.experimental import pallas as pl
from jax.experimental.pallas import tpu as pltpu

F32 = jnp.float32
BF16 = jnp.bfloat16
HI = lax.Precision.HIGHEST

D_MODEL = 1024
BATCH = 16
SEQ = 256
DEPTH = 2
DEC_BATCH = 4
DEC_SEQ = 1024
PAST_LEN = 256
GRID_W = 64
MA_HEADS = 4
MA_DK = 128
MA_DV = 128
MA_CHUNK = 128
NA_HEADS = 8
NA_DH = 64
NA_KR_MAX = 8
NA_KC = 16
HG_HEADS = 4
HG_DK = 128
HG_DV = 128
HG_CHUNK = 64
BRANCH_W = 512
N_GROUPS = 4
EXP_PER_GROUP = 4
N_EXPERTS = N_GROUPS * EXP_PER_GROUP
D_EXPERT = 512
ROPE_BASE = 10000.0
LN_EPS = 1e-5
RMS_EPS = 1e-6
DEEPNORM_ALPHA = (2 * DEPTH) ** 0.25

N_CTX = BATCH * SEQ
N_LAT = DEC_BATCH * DEC_SEQ
N_TOK = N_CTX + N_LAT
N_MODROWS = 8
GATE_COL0 = 4 * BRANCH_W
N_GATES = 4 * MA_HEADS
N_IN = 9232
P_COLS = N_IN - N_GATES
MOE_TM = 512
MOE_NT = N_TOK // MOE_TM + N_GROUPS
MOE_FE = 4
MOE_XW = D_MODEL + 128
MOE_XDT = F32
ROUTER_ROWS = 32
ROUTER_E0 = 8
VMEM_LIMIT = 56 * 1024 * 1024

_NT = (((1,), (1,)), ((), ()))


def _cparams(*sem):
    return pltpu.CompilerParams(dimension_semantics=sem, vmem_limit_bytes=VMEM_LIMIT)


def _mod_row(tile, tm):
    return jnp.maximum((tile * tm) // DEC_SEQ - (N_CTX // DEC_SEQ - 1), 0)


def _mod_spec(l, part, tm):
    return pl.BlockSpec((None, None, None, 1, D_MODEL), lambda i: (l, _mod_row(i, tm), part, 0, 0))


def _pair_specs(tm):
    nc = N_CTX // tm
    return (pl.BlockSpec((tm, D_MODEL), lambda i: (jnp.minimum(i, nc - 1), 0)),
            pl.BlockSpec((tm, D_MODEL), lambda i: (jnp.maximum(i - nc, 0), 0)))


def _pair_read(i, tm, c_ref, l_ref):
    return jnp.where(i < N_CTX // tm, c_ref[...], l_ref[...])


def _silu(x):
    return x * jax.nn.sigmoid(x)


def _layer_norm(x, g, b):
    mu = jnp.mean(x, -1, keepdims=True)
    xc = x - mu
    var = jnp.mean(xc * xc, -1, keepdims=True)
    return xc * lax.rsqrt(var + LN_EPS) * g + b


def _log_sigmoid(x):
    return jnp.minimum(x, 0.0) - jnp.log(1.0 + jnp.exp(-jnp.abs(x)))


def _tri(n, upper):
    r = lax.broadcasted_iota(jnp.int32, (n, n), 0)
    c = lax.broadcasted_iota(jnp.int32, (n, n), 1)
    return jnp.where((r <= c) if upper else (r >= c), 1.0, 0.0).astype(F32)


def _dot3(a, b, split_b):
    x = b if split_b else a
    hi = x.astype(BF16)
    r = x - hi.astype(F32)
    mid = r.astype(BF16)
    lo = (r - mid.astype(F32)).astype(BF16)
    one = (a if split_b else b).astype(BF16)
    dot = (lambda p: jnp.dot(one, p, preferred_element_type=F32)) if split_b else (
        lambda p: jnp.dot(p, one, preferred_element_type=F32))
    return dot(hi) + dot(mid) + dot(lo)


def _mod_kernel(c_ref, w_ref, b_ref, o_ref):
    s = _silu(c_ref[...])
    o_ref[...] = jnp.dot(s.astype(BF16), w_ref[...].astype(BF16), preferred_element_type=F32) + b_ref[...]


def _modulation(cs, w_mod, b_mod):
    tn = 1024
    return pl.pallas_call(
        _mod_kernel,
        out_shape=jax.ShapeDtypeStruct((DEPTH, N_MODROWS, 6 * D_MODEL), F32),
        grid=(DEPTH, 6 * D_MODEL // tn),
        in_specs=[pl.BlockSpec((N_MODROWS, D_MODEL), lambda l, j: (0, 0)),
                  pl.BlockSpec((None, D_MODEL, tn), lambda l, j: (l, 0, j)),
                  pl.BlockSpec((None, 1, tn), lambda l, j: (l, 0, j))],
        out_specs=pl.BlockSpec((None, N_MODROWS, tn), lambda l, j: (l, 0, j)),
        compiler_params=_cparams("arbitrary", "arbitrary"),
        name="modulation",
    )(cs, w_mod, b_mod.reshape(DEPTH, 1, 6 * D_MODEL))


def _prep_kernel(xc_ref, xl_ref, sh_ref, sc_ref, h_ref, *, tm):
    x = _pair_read(pl.program_id(0), tm, xc_ref, xl_ref)
    h_ref[...] = (x * (1.0 + sc_ref[...]) + sh_ref[...]).astype(BF16)


def _prep(xc, xl, mod):
    tm = 1024
    return pl.pallas_call(
        functools.partial(_prep_kernel, tm=tm),
        out_shape=jax.ShapeDtypeStruct((N_TOK, D_MODEL), BF16),
        grid=(N_TOK // tm,),
        in_specs=[*_pair_specs(tm), _mod_spec(0, 0, tm), _mod_spec(0, 1, tm)],
        out_specs=pl.BlockSpec((tm, D_MODEL), lambda i: (i, 0)),
        compiler_params=_cparams("arbitrary"),
        name="prep",
    )(xc, xl, mod, mod)


INPROJ_TN = 512
N_PLAIN_TILES = GATE_COL0 // INPROJ_TN


F32_TILE0 = 7
N_F32_TILES = 2
P16_COLS = P_COLS - N_F32_TILES * INPROJ_TN


def _inproj_kernel(h_ref, wa_ref, wb_ref, b_ref, o_ref, *, src_tile):
    j = src_tile(pl.program_id(1))

    @pl.when(j < N_PLAIN_TILES)
    def _():
        o_ref[...] = (lax.dot_general(h_ref[...], wa_ref[...].astype(BF16), _NT, preferred_element_type=F32)
                      + b_ref[...]).astype(o_ref.dtype)

    @pl.when(j >= N_PLAIN_TILES)
    def _():
        w = jnp.concatenate([wa_ref[N_GATES:, :], wb_ref[...]], 0)
        o_ref[...] = (lax.dot_general(h_ref[...], w.astype(BF16), _NT, preferred_element_type=F32)
                      + b_ref[...]).astype(o_ref.dtype)


def _inproj_call(h, w_t, b_main, l, tm, n_tiles, src_tile, dtype, name):
    tn = INPROJ_TN
    return pl.pallas_call(
        functools.partial(_inproj_kernel, src_tile=src_tile),
        out_shape=jax.ShapeDtypeStruct((N_TOK, n_tiles * tn), dtype),
        grid=(N_TOK // tm, n_tiles),
        in_specs=[pl.BlockSpec((tm, D_MODEL), lambda i, j: (i, 0)),
                  pl.BlockSpec((None, tn, D_MODEL), lambda i, j: (l, src_tile(j), 0)),
                  pl.BlockSpec((None, N_GATES, D_MODEL), lambda i, j: (l, (src_tile(j) + 1) * (tn // N_GATES), 0)),
                  pl.BlockSpec((1, tn), lambda i, j: (0, src_tile(j)))],
        out_specs=pl.BlockSpec((tm, tn), lambda i, j: (i, j)),
        compiler_params=_cparams("arbitrary", "arbitrary"),
        name=name,
    )(h, w_t, w_t, b_main)


def _inproj(h, w_t, b_main, l):
    skip_f32 = lambda j: jnp.where(j < F32_TILE0, j, j + N_F32_TILES)
    p16 = _inproj_call(h, w_t, b_main, l, 4096, P16_COLS // INPROJ_TN, skip_f32, BF16, "inproj")
    p32 = _inproj_call(h, w_t, b_main, l, 4096, N_F32_TILES, lambda j: j + F32_TILE0, F32, "inproj_f32")
    return p16, p32


def _gates_kernel(h_ref, w_ref, b_ref, gc_ref, gt_ref):
    g = lax.dot_general(h_ref[...], w_ref[...].astype(BF16), _NT, preferred_element_type=F32) + b_ref[...]
    gc_ref[...] = g
    gt_ref[...] = g.T[:N_GATES]


def _gates(h, w_t, b_in3, l):
    tm = 1024
    gblk = GATE_COL0 // 128
    return pl.pallas_call(
        _gates_kernel,
        out_shape=(jax.ShapeDtypeStruct((N_TOK, 128), F32), jax.ShapeDtypeStruct((N_GATES, N_TOK), F32)),
        grid=(N_TOK // tm,),
        in_specs=[pl.BlockSpec((tm, D_MODEL), lambda i: (i, 0)),
                  pl.BlockSpec((None, 128, D_MODEL), lambda i: (l, gblk, 0)),
                  pl.BlockSpec((None, 1, 128), lambda i: (l, 0, gblk))],
        out_specs=(pl.BlockSpec((tm, 128), lambda i: (i, 0)), pl.BlockSpec((N_GATES, tm), lambda i: (0, i))),
        compiler_params=_cparams("arbitrary"),
        name="gates",
    )(h, w_t, b_in3)


HEADS_PER_BLK = 128 // NA_DH
NA_NBLK = NA_HEADS // HEADS_PER_BLK
NA_QSCALE = NA_DH ** -0.5
Q_COL, K_COL, V_COL = 16, 20, 24
QKV_COL = 4


def _ctx_attn_kernel(*refs):
    q_ref, k_ref, v_ref = refs[:3]
    o_ref, ko_ref, vo_ref = refs[-3:]
    heads = range(NA_HEADS)
    split = lambda x: jnp.stack([x[:, h * NA_DH:(h + 1) * NA_DH] for h in heads], 0)
    q = split(q_ref[...] * NA_QSCALE)
    k = split(k_ref[...])
    v = split(v_ref[...])
    ko_ref[...] = k.astype(F32)
    vo_ref[...] = v.astype(F32)
    s = lax.dot_general(q, k, (((2,), (2,)), ((0,), (0,))), preferred_element_type=F32)
    e = jnp.exp(s - jnp.max(s, -1, keepdims=True))
    p = e * (1.0 / jnp.sum(e, -1, keepdims=True))
    o = lax.dot_general(p.astype(BF16), v, (((2,), (1,)), ((0,), (0,))), preferred_element_type=F32)
    o_ref[...] = jnp.concatenate([o[h] for h in heads], -1).astype(BF16)


def _ctx_attention(p16, l, prev_k=None, prev_v=None):
    kv_shape = jax.ShapeDtypeStruct((BATCH, DEPTH, NA_HEADS, SEQ, NA_DH), F32)
    kv_spec = pl.BlockSpec((None, None, NA_HEADS, SEQ, NA_DH), lambda b: (b, l, 0, 0, 0))
    col = lambda j: pl.BlockSpec((SEQ, BRANCH_W), lambda b: (b, j))
    in_specs = [col(QKV_COL), col(QKV_COL + 1), col(QKV_COL + 2)]
    args = [p16, p16, p16]
    aliases = {}
    if prev_k is not None:
        in_specs += [pl.BlockSpec(memory_space=pl.ANY)] * 2
        args += [prev_k, prev_v]
        aliases = {3: 1, 4: 2}
    return pl.pallas_call(
        _ctx_attn_kernel,
        out_shape=(jax.ShapeDtypeStruct((N_TOK, BRANCH_W), BF16), kv_shape, kv_shape),
        grid=(BATCH,),
        in_specs=in_specs,
        out_specs=(pl.BlockSpec((SEQ, BRANCH_W), lambda b: (b, 0)), kv_spec, kv_spec),
        input_output_aliases=aliases,
        compiler_params=_cparams("arbitrary"),
        name="ctx_attention",
    )(*args)


NA_ROWS = DEC_SEQ // GRID_W
NA_KR = min(NA_KR_MAX, NA_ROWS)
NA_QROWS = 4
NA_QT = NA_ROWS // NA_QROWS
NA_WROWS = NA_KR + NA_QROWS - 1
NA_WKEYS = NA_WROWS * GRID_W


def _na_window_start(t):
    return min(max(t * NA_QROWS - NA_KR // 2, 0), NA_ROWS - NA_WROWS)


def _na_bias_table(rpb):
    c = np.arange(GRID_W)
    c0 = np.clip(c - NA_KC // 2, 0, GRID_W - NA_KC)
    kc = np.arange(GRID_W)
    valid = (kc[None, :] >= c0[:, None]) & (kc[None, :] < c0[:, None] + NA_KC)
    dc = kc[None, :] - c[:, None] + NA_KC - 1
    onehot = (dc[None] == np.arange(2 * NA_KC - 1)[:, None, None]) & valid[None]
    toep = jnp.einsum('hrd,dcx->hrcx', rpb.astype(F32), jnp.asarray(onehot, F32), precision=HI)
    toep = jnp.where(valid[None, None], toep, -jnp.inf)
    ninf = jnp.full((NA_HEADS, GRID_W, GRID_W), -jnp.inf, F32)
    tiles = []
    for t in range(NA_QT):
        w0 = _na_window_start(t)
        qrows = []
        for r in range(t * NA_QROWS, (t + 1) * NA_QROWS):
            r0 = min(max(r - NA_KR // 2, 0), NA_ROWS - NA_KR)
            blocks = []
            for kr in range(w0, w0 + NA_WROWS):
                inside = r0 <= kr < r0 + NA_KR
                blocks.append(toep[:, kr - r + NA_KR_MAX - 1] if inside else ninf)
            qrows.append(jnp.concatenate(blocks, -1))
        tiles.append(jnp.concatenate(qrows, 1))
    return jnp.stack(tiles, 1).astype(BF16)


def _lat_attn_kernel(q_ref, k_ref, v_ref, ck_ref, cv_ref, bias_ref, prev_ref, o_ref):
    heads = range(HEADS_PER_BLK)
    split = lambda x: jnp.stack([x[:, h * NA_DH:(h + 1) * NA_DH] for h in heads], 0)
    bnt = (((2,), (2,)), ((0,), (0,)))
    bnn = (((2,), (1,)), ((0,), (0,)))
    q = (q_ref[...] * NA_QSCALE).astype(BF16)
    k = k_ref[...].astype(BF16)
    v = v_ref[...].astype(BF16)
    ck = ck_ref[...].astype(BF16)
    cv = cv_ref[...].astype(BF16)
    nq = NA_QROWS * GRID_W
    for t in range(NA_QT):
        w0 = _na_window_start(t)
        qs = slice(t * nq, (t + 1) * nq)
        ws = slice(w0 * GRID_W, (w0 + NA_WROWS) * GRID_W)
        qh = split(q[qs])
        s_loc = lax.dot_general(qh, split(k[ws]), bnt, preferred_element_type=F32) + bias_ref[:, t]
        s_ctx = lax.dot_general(qh, ck, bnt, preferred_element_type=F32)
        m = jnp.maximum(jnp.max(s_loc, -1, keepdims=True), jnp.max(s_ctx, -1, keepdims=True))
        e_loc = jnp.exp(s_loc - m)
        e_ctx = jnp.exp(s_ctx - m)
        inv = 1.0 / (jnp.sum(e_loc, -1, keepdims=True) + jnp.sum(e_ctx, -1, keepdims=True))
        acc = (lax.dot_general(e_loc.astype(BF16), split(v[ws]), bnn, preferred_element_type=F32)
               + lax.dot_general(e_ctx.astype(BF16), cv, bnn, preferred_element_type=F32)) * inv
        o_ref[qs, :] = jnp.concatenate([acc[h] for h in heads], -1).astype(BF16)


def _lat_attention(proj, ck, cv, bias, l, b_out):
    rb0 = N_CTX // DEC_SEQ
    cb = lambda base: (lambda j, b: (rb0 + b, base + j))
    c_spec = pl.BlockSpec((None, None, HEADS_PER_BLK, PAST_LEN, NA_DH), lambda j, b: (b, l, j, 0, 0))
    return pl.pallas_call(
        _lat_attn_kernel,
        out_shape=jax.ShapeDtypeStruct((N_TOK, BRANCH_W), BF16),
        grid=(NA_NBLK, DEC_BATCH),
        in_specs=[pl.BlockSpec((DEC_SEQ, 128), cb(Q_COL)), pl.BlockSpec((DEC_SEQ, 128), cb(K_COL)),
                  pl.BlockSpec((DEC_SEQ, 128), cb(V_COL)), c_spec, c_spec,
                  pl.BlockSpec((None, HEADS_PER_BLK, NA_QT, NA_QROWS * GRID_W, NA_WKEYS), lambda j, b: (l, j, 0, 0, 0)),
                  pl.BlockSpec(memory_space=pl.ANY)],
        out_specs=pl.BlockSpec((DEC_SEQ, 128), lambda j, b: (rb0 + b, j)),
        input_output_aliases={6: 0},
        compiler_params=_cparams("arbitrary", "arbitrary"),
        name="lat_attention",
    )(proj, proj, proj, ck, cv, bias, b_out)


MA_KSCALE = MA_DK ** -0.5


def _rope_tables(T):
    t = np.arange(T)
    half = MA_DK // 2
    inv = ROPE_BASE ** (-jnp.arange(0, half, 2, dtype=F32) / half)
    ang_r = jnp.asarray((t // GRID_W).astype(np.float32))[:, None] * inv[None, :]
    ang_c = jnp.asarray((t % GRID_W).astype(np.float32))[:, None] * inv[None, :]
    cos = jnp.concatenate([jnp.cos(ang_r)] * 2 + [jnp.cos(ang_c)] * 2, -1)
    sin = jnp.concatenate([-jnp.sin(ang_r), jnp.sin(ang_r), -jnp.sin(ang_c), jnp.sin(ang_c)], -1)
    return cos, sin


def _mlstm_kernel(*refs, T, latent):
    if latent:
        (p_ref, gc_ref, gt_ref, fbc_ref, fbr_ref, cos_ref, sin_ref, c0_ref, n0_ref, m0_ref, prev_ref,
         a_ref, qs, ks, vT, hfT, hbT, CT, ns, ms, brs, kcs) = refs
    else:
        p_ref, gc_ref, gt_ref, fbc_ref, fbr_ref = refs[:5]
        a_ref, co_ref, no_ref, mo_ref, qs, ks, vT, hfT, hbT, CT, ns, ms, brs, kcs = refs[-14:]
    L = MA_CHUNK
    NC = T // L
    W = BRANCH_W
    PER = 128 // L

    lane = lax.broadcasted_iota(jnp.int32, (128, MA_DK), 1)
    lo_half = (lane % (MA_DK // 2)) < (MA_DK // 4)

    def v_block(tb, carry):
        r0 = pl.multiple_of(tb * 128, 128)
        rows = pl.ds(r0, 128)

        def rope(x):
            if not latent:
                return x
            swapped = jnp.where(lo_half, pltpu.roll(x, MA_DK - MA_DK // 4, 1), pltpu.roll(x, MA_DK // 4, 1))
            return x * cos_ref[rows, :] + swapped * sin_ref[rows, :]

        for h in range(MA_HEADS):
            hs = slice(h * MA_DK, (h + 1) * MA_DK)
            qs[rows, hs] = rope(p_ref[rows, hs].astype(F32)).astype(BF16)
            ks[rows, hs] = rope(p_ref[rows, W + h * MA_DK:W + (h + 1) * MA_DK].astype(F32) * MA_KSCALE).astype(BF16)
        for h in range(MA_HEADS):
            hs = slice(h * MA_DV, (h + 1) * MA_DV)
            blk = p_ref[pl.ds(r0, 128), 2 * W + h * MA_DV:2 * W + (h + 1) * MA_DV].astype(F32).T.astype(BF16)
            for j in range(PER):
                vT[tb * PER + j, h] = blk[:, j * L:(j + 1) * L]
        return carry

    lax.fori_loop(0, T // 128, v_block, 0)

    for d in range(2):
        for h in range(MA_HEADS):
            sidx = d * MA_HEADS + h
            CT[sidx] = c0_ref[d, h].T if latent else jnp.zeros((MA_DV, MA_DK), F32)
            ns[sidx] = n0_ref[sidx:sidx + 1, :] if latent else jnp.zeros((1, MA_DK), F32)
            ms[sidx] = m0_ref[sidx:sidx + 1, :] if latent else jnp.zeros((1, 128), F32)

    low = _tri(L, False)
    upp = _tri(L, True)
    rr = lax.broadcasted_iota(jnp.int32, (L, L), 0)
    cc = lax.broadcasted_iota(jnp.int32, (L, L), 1)
    tri_cols = jnp.concatenate([upp, low], 1)
    tri_rows = jnp.concatenate([low, upp], 0)
    fbc = fbc_ref[...]
    fbr = fbr_ref[...]

    def gate_sums(c, carry):
        t0 = pl.multiple_of(c * L, L)
        gc = gc_ref[pl.ds(t0, L), :]
        lfc = _log_sigmoid(gc + fbc)
        lfr = _log_sigmoid(gt_ref[c] + fbr)
        ish = pltpu.roll(gc, MA_HEADS, 1)
        brow = _dot3(lfr, tri_cols, False)
        bcol = _dot3(tri_rows, lfc, True)
        brs[0, c] = brow[:, :L]
        brs[1, c] = brow[:, L:]
        kcs[0, pl.ds(t0, L), :] = ish - bcol[:L]
        kcs[1, pl.ds(t0, L), :] = ish - bcol[L:]
        return carry

    lax.fori_loop(0, NC, gate_sums, 0, unroll=2)

    def chunk_pair(cf, cb):
        H = MA_HEADS
        G = 2 * H
        heads = range(H)
        cs, ts = (cf, cb), (pl.multiple_of(cf * L, L), pl.multiple_of(cb * L, L))
        rows_of = lambda x, r0: [x[r0 + h:r0 + h + 1, :] for h in heads]
        cols_of = lambda x, c0: [x[:, c0 + h:c0 + h + 1] for h in heads]
        br = jnp.stack(sum([rows_of(brs[d, cs[d]], 2 * d * H + H) for d in range(2)], []), 0)
        ir = jnp.stack(sum([rows_of(gt_ref[cs[d]], 2 * d * H) for d in range(2)], []), 0)
        kcol = jnp.stack(sum([cols_of(kcs[d, pl.ds(ts[d], L), :], 2 * d * H + H) for d in range(2)], []), 0)
        split = lambda x: [x[:, h * MA_DK:(h + 1) * MA_DK] for h in heads]
        q = jnp.stack(split(qs[pl.ds(ts[0], L), :]) + split(qs[pl.ds(ts[1], L), :]), 0)
        k = jnp.stack(split(ks[pl.ds(ts[0], L), :]) + split(ks[pl.ds(ts[1], L), :]), 0)
        vt = jnp.concatenate([vT[cf], vT[cb]], 0)
        m = ms[...][:, :, 0:1]
        n = ns[...]
        ct = CT[...]
        bnt = (((2,), (2,)), ((0,), (0,)))
        bnn = (((2,), (1,)), ((0,), (0,)))
        pre = br + kcol
        dmat = jnp.concatenate([jnp.where(rr <= cc, pre[:H], -jnp.inf), jnp.where(rr >= cc, pre[H:], -jnp.inf)], 0)
        g = br + m
        m_t = jnp.maximum(g, jnp.max(dmat, 1, keepdims=True))
        w_inter = jnp.exp(g - m_t)
        s = lax.dot_general(k, q, bnt, preferred_element_type=F32) * jnp.exp(dmat - m_t)
        ctn = jnp.concatenate([ct.astype(BF16), jnp.broadcast_to(n, (G, 8, MA_DK)).astype(BF16)], 1)
        cq = lax.dot_general(ctn, q, bnt, preferred_element_type=F32)
        num = w_inter * cq[:, :MA_DV] + lax.dot_general(vt, s.astype(BF16), bnn, preferred_element_type=F32)
        den = w_inter * cq[:, MA_DV:MA_DV + 1] + jnp.sum(s, 1, keepdims=True)
        hout = num / jnp.maximum(jnp.abs(den), jnp.exp(-m_t))
        hfT[cf] = hout[:H]
        hbT[cb] = hout[H:]
        last = lambda x: jnp.concatenate([x[:H, :, L - 1:L], x[H:, :, 0:1]], 0)
        m_new = last(m_t)
        b_last = last(br)
        decay = jnp.exp(b_last + m - m_new)
        wk = jnp.exp(b_last - br + ir - m_new)
        wk_hi = wk.astype(BF16)
        wk_lo = (wk - wk_hi.astype(F32)).astype(BF16)
        lhs = jnp.concatenate([(vt.astype(F32) * wk).astype(BF16), wk_hi, wk_lo, jnp.zeros((G, 6, L), BF16)], 1)
        upd = lax.dot_general(lhs, k, bnn, preferred_element_type=F32)
        CT[...] = decay * ct + upd[:, :MA_DV]
        ns[...] = decay * n + upd[:, MA_DV:MA_DV + 1] + upd[:, MA_DV + 1:MA_DV + 2]
        ms[...] = jnp.broadcast_to(m_new, (G, 1, 128))

    def body(i, carry):
        chunk_pair(i, NC - 1 - i)
        return carry

    lax.fori_loop(0, NC, body, 0, unroll=min(4, NC))

    def out_block(tb, carry):
        r0 = pl.multiple_of(tb * 128, 128)
        hsum = jnp.concatenate([hfT[tb * PER + j] + hbT[tb * PER + j] for j in range(PER)], 2)
        outs = [hsum[h].T for h in range(MA_HEADS)]
        gate = jax.nn.sigmoid(p_ref[pl.ds(r0, 128), 3 * W:4 * W].astype(F32))
        a_ref[pl.ds(r0, 128), :] = (gate * jnp.concatenate(outs, 1)).astype(BF16)
        return carry

    lax.fori_loop(0, T // 128, out_block, 0)
    if not latent:
        for d in range(2):
            for h in range(MA_HEADS):
                sidx = d * MA_HEADS + h
                co_ref[d, h] = CT[sidx].T
                no_ref[sidx:sidx + 1, :] = ns[sidx]
                mo_ref[sidx:sidx + 1, :] = ms[sidx]


def _mlstm(proj, gcol, gt3, fbias_l, l, latent, C0=None, n0=None, m0=None, a_out=None, prev=None):
    T = DEC_SEQ if latent else SEQ
    B = DEC_BATCH if latent else BATCH
    rb0 = N_CTX // DEC_SEQ if latent else 0
    fb = fbias_l.astype(F32)
    fbc = jnp.zeros((1, 128), F32).at[0, MA_HEADS:2 * MA_HEADS].set(fb[0]).at[0, 3 * MA_HEADS:4 * MA_HEADS].set(fb[1])
    fbr = fbc[0, :N_GATES].reshape(N_GATES, 1)
    full2 = lambda b: (0, 0)
    any_spec = pl.BlockSpec(memory_space=pl.ANY)
    in_specs = [pl.BlockSpec((T, 4 * BRANCH_W), lambda b: (rb0 + b, 0)),
                pl.BlockSpec((T, 128), lambda b: (rb0 + b, 0)),
                pl.BlockSpec((T // MA_CHUNK, N_GATES, MA_CHUNK), lambda b: (rb0 + b, 0, 0)),
                pl.BlockSpec((1, 128), full2), pl.BlockSpec((N_GATES, 1), full2)]
    args = [proj, gcol, gt3, fbc, fbr]
    a_shape = jax.ShapeDtypeStruct((N_TOK, BRANCH_W), BF16)
    a_spec = pl.BlockSpec((T, BRANCH_W), lambda b: (rb0 + b, 0))
    c_spec = pl.BlockSpec((None, None, 2, MA_HEADS, MA_DK, MA_DV), lambda b: (b, l, 0, 0, 0, 0))
    nm_spec = pl.BlockSpec((None, None, 2 * MA_HEADS, 128), lambda b: (b, l, 0, 0))
    aliases = {}
    if latent:
        cos, sin = _rope_tables(T)
        nb = 2 * MA_HEADS
        in_specs += [pl.BlockSpec((T, MA_DK), full2), pl.BlockSpec((T, MA_DK), full2), c_spec, nm_spec, nm_spec, any_spec]
        args += [cos, sin, C0, n0.reshape(B, DEPTH, nb, MA_DK),
                 jnp.broadcast_to(m0.reshape(B, DEPTH, nb, 1), (B, DEPTH, nb, 128)), a_out]
        aliases = {len(args) - 1: 0}
        out_shape, out_specs = a_shape, a_spec
    else:
        nm_shape = jax.ShapeDtypeStruct((B, DEPTH, 2 * MA_HEADS, 128), F32)
        out_shape = (a_shape, jax.ShapeDtypeStruct((B, DEPTH, 2, MA_HEADS, MA_DK, MA_DV), F32), nm_shape, nm_shape)
        out_specs = (a_spec, c_spec, nm_spec, nm_spec)
        if prev is not None:
            in_specs += [any_spec] * 3
            args += list(prev)
            aliases = {len(args) - 3: 1, len(args) - 2: 2, len(args) - 1: 3}
    nc = T // MA_CHUNK
    scratch = [pltpu.VMEM((T, BRANCH_W), BF16), pltpu.VMEM((T, BRANCH_W), BF16),
               pltpu.VMEM((nc, MA_HEADS, MA_DV, MA_CHUNK), BF16),
               pltpu.VMEM((nc, MA_HEADS, MA_DV, MA_CHUNK), F32), pltpu.VMEM((nc, MA_HEADS, MA_DV, MA_CHUNK), F32),
               pltpu.VMEM((2 * MA_HEADS, MA_DV, MA_DK), F32), pltpu.VMEM((2 * MA_HEADS, 1, MA_DK), F32),
               pltpu.VMEM((2 * MA_HEADS, 1, 128), F32),
               pltpu.VMEM((2, nc, N_GATES, MA_CHUNK), F32), pltpu.VMEM((2, T, 128), F32)]
    return pl.pallas_call(
        functools.partial(_mlstm_kernel, T=T, latent=latent),
        out_shape=out_shape, grid=(B,), in_specs=in_specs, out_specs=out_specs, scratch_shapes=scratch,
        input_output_aliases=aliases,
        compiler_params=_cparams("arbitrary"),
        name="mlstm_lat" if latent else "mlstm_ctx",
    )(*args)


HG_SUB = 8


def _hgrn_kernel(*refs, T, latent):
    ff_ref, fb_ref, q_ref, i_ref, g_ref, lbf_ref, lbb_ref = refs[:7]
    if latent:
        s0_ref = refs[7]
        c_ref, of, ob, ST, iT, As, Bs = refs[-7:]
    else:
        c_ref, so_ref, of, ob, ST, iT, As, Bs = refs[-8:]
    L = HG_CHUNK
    NC = T // L
    NB = L // HG_SUB
    DK = HG_DK

    for d in range(2):
        for h in range(HG_HEADS):
            ST[d * HG_HEADS + h] = s0_ref[d, h].T if latent else jnp.zeros((HG_DV, DK), F32)

    PER = 128 // L

    def i_block(tb, carry):
        r0 = pl.multiple_of(tb * 128, 128)
        for h in range(HG_HEADS):
            blk = i_ref[pl.ds(r0, 128), h * HG_DV:(h + 1) * HG_DV].astype(F32).T.astype(BF16)
            for j in range(PER):
                iT[tb * PER + j, h] = blk[:, j * L:(j + 1) * L]
        return carry

    lax.fori_loop(0, T // 128, i_block, 0)

    low = _tri(L, False)
    upp = _tri(L, True)
    row8 = lax.broadcasted_iota(jnp.int32, (HG_SUB, L), 0)
    lane_s = lax.broadcasted_iota(jnp.int32, (HG_SUB, L), 1)
    heads = range(HG_HEADS)
    bnt = (((2,), (2,)), ((0,), (0,)))
    bnn = (((2,), (1,)), ((0,), (0,)))
    LOG2E = 1.4426950408889634

    def split(x):
        return jnp.stack([x[:, h * DK:(h + 1) * DK] for h in heads], 0)

    def decay_sums(c, carry):
        t0 = pl.multiple_of(c * L, L)
        for d in range(2):
            fpre = (ff_ref if d == 0 else fb_ref)[pl.ds(t0, L), :]
            lb = (lbf_ref if d == 0 else lbb_ref)[...]
            f = lb + (1.0 - lb) * jax.nn.sigmoid(fpre)
            a = _dot3(low if d == 0 else upp, jnp.log(f) * LOG2E, True)
            As[d, pl.ds(t0, L), :] = a
            Bs[d, pl.ds(t0, L), :] = a - jnp.log(1.0 - f) * LOG2E
        return carry

    lax.fori_loop(0, NC, decay_sums, 0, unroll=4)

    def chunk_pair(cf, cb):
        tf = pl.multiple_of(cf * L, L)
        tb = pl.multiple_of(cb * L, L)
        both = lambda fn: jnp.concatenate([fn(0, tf), fn(1, tb)], 0)
        A = both(lambda d, t: split(As[d, pl.ds(t, L), :]))
        B = both(lambda d, t: split(Bs[d, pl.ds(t, L), :]))
        q = both(lambda d, t: split(_silu(q_ref[pl.ds(t, L), :].astype(F32))))
        iv = both(lambda d, t: split(i_ref[pl.ds(t, L), :].astype(BF16)))
        ivT = jnp.concatenate([iT[cf], iT[cb]], 0)
        H = HG_HEADS
        st = ST[...]
        o = lax.dot_general((q * jnp.exp2(A)).astype(BF16), st.astype(BF16), bnt, preferred_element_type=F32)
        a_last = jnp.concatenate([A[:H, L - 1:L], A[H:, 0:1]], 0)
        kd = jnp.exp2(a_last - B).astype(BF16)
        rows = []
        for I in range(NB):
            lo, hi = I * HG_SUB, (I + 1) * HG_SUB
            A_I, q_I = A[:, lo:hi], q[:, lo:hi]
            att_f = jnp.zeros((H, HG_SUB, L), F32)
            att_b = jnp.zeros((H, HG_SUB, L), F32)
            for j in range(HG_SUB):
                s = lo + j
                col = jnp.sum(q_I * jnp.exp2(A_I - B[:, s:s + 1]), -1, keepdims=True)
                att_f = jnp.where((lane_s == s) & (row8 >= j), col[:H], att_f)
                att_b = jnp.where((lane_s == s) & (row8 <= j), col[H:], att_b)
            rf, rb = max(lo - 1, 0), min(hi, L - 1)
            R = jnp.concatenate([A[:H, rf:rf + 1], A[H:, rb:rb + 1]], 0)
            zeros = lambda n: jnp.zeros((H, n, DK), BF16)
            ksc_f = jnp.concatenate([jnp.exp2(R[:H] - B[:H, :lo]).astype(BF16), zeros(L - lo)], 1) if I > 0 else zeros(L)
            ksc_b = jnp.concatenate([zeros(hi), jnp.exp2(R[H:] - B[H:, hi:]).astype(BF16)], 1) if I < NB - 1 else zeros(L)
            ksc = jnp.concatenate([ksc_f, ksc_b], 0)
            off = lax.dot_general((q_I * jnp.exp2(A_I - R)).astype(BF16), ksc, bnt, preferred_element_type=F32)
            rows.append(jnp.concatenate([att_f, att_b], 0) + off)
        att = jnp.concatenate(rows, 1)
        o = o + lax.dot_general(att.astype(BF16), iv, bnn, preferred_element_type=F32)
        for h in heads:
            of[pl.ds(tf, L), h * HG_DV:(h + 1) * HG_DV] = o[h]
            ob[pl.ds(tb, L), h * HG_DV:(h + 1) * HG_DV] = o[H + h]
        ST[...] = st * jnp.exp2(a_last) + lax.dot_general(ivT, kd, bnn, preferred_element_type=F32)

    def body(i, carry):
        chunk_pair(i, NC - 1 - i)
        return carry

    lax.fori_loop(0, NC, body, 0, unroll=min(8, NC))

    def epilogue(r, carry):
        t0 = pl.multiple_of(r * 128, 128)
        o = of[pl.ds(t0, 128), :] + ob[pl.ds(t0, 128), :]
        gsil = _silu(g_ref[pl.ds(t0, 128), :].astype(F32))
        outs = []
        for h in range(HG_HEADS):
            oh = o[:, h * HG_DV:(h + 1) * HG_DV]
            outs.append(oh * lax.rsqrt(jnp.mean(oh * oh, -1, keepdims=True) + RMS_EPS))
        c_ref[pl.ds(t0, 128), :] = (jnp.concatenate(outs, -1) * gsil).astype(BF16)
        return carry

    lax.fori_loop(0, T // 128, epilogue, 0)
    if not latent:
        for d in range(2):
            for h in range(HG_HEADS):
                so_ref[d, h] = ST[d * HG_HEADS + h].T


def _hgrn(p16, p32, lb_l, l, latent, S0=None, c_out=None, prev=None):
    T = DEC_SEQ if latent else SEQ
    B = DEC_BATCH if latent else BATCH
    rb0 = N_CTX // DEC_SEQ if latent else 0
    W = BRANCH_W
    full2 = lambda b: (0, 0)
    any_spec = pl.BlockSpec(memory_space=pl.ANY)
    col = lambda j: pl.BlockSpec((T, W), lambda b: (rb0 + b, j))
    s_spec = pl.BlockSpec((None, None, 2, HG_HEADS, HG_DK, HG_DV), lambda b: (b, l, 0, 0, 0, 0))
    in_specs = [col(0), col(1), col(7), col(8), col(9), pl.BlockSpec((1, W), full2), pl.BlockSpec((1, W), full2)]
    args = [p32, p32, p16, p16, p16, lb_l[0][None, :], lb_l[1][None, :]]
    c_shape = jax.ShapeDtypeStruct((N_TOK, W), BF16)
    c_spec = pl.BlockSpec((T, W), lambda b: (rb0 + b, 0))
    aliases = {}
    if latent:
        in_specs += [s_spec, any_spec]
        args += [S0, c_out]
        aliases = {8: 0}
        out_shape, out_specs = c_shape, c_spec
    else:
        out_shape = (c_shape, jax.ShapeDtypeStruct((B, DEPTH, 2, HG_HEADS, HG_DK, HG_DV), F32))
        out_specs = (c_spec, s_spec)
        if prev is not None:
            in_specs.append(any_spec)
            args.append(prev)
            aliases = {7: 1}
    scratch = [pltpu.VMEM((T, W), F32), pltpu.VMEM((T, W), F32), pltpu.VMEM((2 * HG_HEADS, HG_DV, HG_DK), F32),
               pltpu.VMEM((T // HG_CHUNK, HG_HEADS, HG_DV, HG_CHUNK), BF16),
               pltpu.VMEM((2, T, W), F32), pltpu.VMEM((2, T, W), F32)]
    return pl.pallas_call(
        functools.partial(_hgrn_kernel, T=T, latent=latent),
        out_shape=out_shape, grid=(B,), in_specs=in_specs, out_specs=out_specs, scratch_shapes=scratch,
        input_output_aliases=aliases,
        compiler_params=_cparams("arbitrary"),
        name="hgrn_lat" if latent else "hgrn_ctx",
    )(*args)


def _merge_kernel(a_ref, b_ref, c_ref, ga_ref, gb_ref, gc_ref, xc_ref, xl_ref, g1_ref, sh2_ref, sc2_ref,
                  wb_ref, wo_ref, lng_ref, lnb_ref, wr_ref, x1_ref, h2_ref, *, tm):
    def br(v_ref, g_ref, k):
        return jax.nn.sigmoid(g_ref[...].astype(F32)) * jnp.dot(v_ref[...], wb_ref[k], preferred_element_type=F32)

    mix = br(a_ref, ga_ref, 0) + br(b_ref, gb_ref, 1) + br(c_ref, gc_ref, 2)
    y = jnp.dot(mix.astype(BF16), wo_ref[...], preferred_element_type=F32)
    x = _pair_read(pl.program_id(0), tm, xc_ref, xl_ref)
    x1 = _layer_norm(DEEPNORM_ALPHA * x + g1_ref[...] * y, lng_ref[...], lnb_ref[...])
    x1_ref[...] = x1
    h2 = x1 * (1.0 + sc2_ref[...]) + sh2_ref[...]
    h2_ref[:, :D_MODEL] = h2.astype(h2_ref.dtype)
    lt = lax.dot_general(wr_ref[...], h2, _NT, preferred_element_type=F32, precision=HI)
    r = lax.broadcasted_iota(jnp.int32, lt.shape, 0)
    neg = -jnp.inf
    lg = jnp.where(r < N_GROUPS, lt, neg)
    mg = jnp.max(lg, 0, keepdims=True)
    g_sel = jnp.min(jnp.where(lg == mg, r, ROUTER_ROWS), 0, keepdims=True)
    p_sel = 1.0 / jnp.sum(jnp.where(r < N_GROUPS, jnp.exp(lg - mg), 0.0), 0, keepdims=True)
    lo = ROUTER_E0 + EXP_PER_GROUP * g_sel
    le = jnp.where((r >= lo) & (r < lo + EXP_PER_GROUP), lt, neg)
    v1 = jnp.max(le, 0, keepdims=True)
    i1 = jnp.min(jnp.where(le == v1, r, ROUTER_ROWS), 0, keepdims=True)
    le2 = jnp.where(r == i1, neg, le)
    v2 = jnp.max(le2, 0, keepdims=True)
    i2 = jnp.min(jnp.where(le2 == v2, r, ROUTER_ROWS), 0, keepdims=True)
    e2 = jnp.exp(v2 - v1)
    w1 = p_sel / (1.0 + e2)
    w2 = p_sel * e2 / (1.0 + e2)
    w1_hi = w1.astype(BF16).astype(F32)
    w2_hi = w2.astype(BF16).astype(F32)
    j1, j2 = i1 - lo, i2 - lo
    packed = jnp.where(r == j1, w1_hi, jnp.where(r == j2, w2_hi, jnp.where(
        r == j1 + EXP_PER_GROUP, w1 - w1_hi, jnp.where(r == j2 + EXP_PER_GROUP, w2 - w2_hi, jnp.where(
            r == 2 * EXP_PER_GROUP, g_sel.astype(F32), 0.0)))))
    packed = jnp.concatenate([packed, jnp.zeros((128 - ROUTER_ROWS, packed.shape[1]), F32)], 0)
    h2_ref[:, D_MODEL:] = packed.T.astype(h2_ref.dtype)


def _merge(a, b, c, p16, xc, xl, mod, wb, wo, lng, lnb, wr, l):
    tm = 512
    tok = lambda i: (i, 0)
    ln_spec = pl.BlockSpec((None, None, 1, D_MODEL), lambda i: (l, 0, 0, 0))
    return pl.pallas_call(
        functools.partial(_merge_kernel, tm=tm),
        out_shape=(jax.ShapeDtypeStruct((N_TOK, D_MODEL), F32), jax.ShapeDtypeStruct((N_TOK, MOE_XW), MOE_XDT)),
        grid=(N_TOK // tm,),
        in_specs=[pl.BlockSpec((tm, BRANCH_W), tok), pl.BlockSpec((tm, BRANCH_W), tok), pl.BlockSpec((tm, BRANCH_W), tok),
                  pl.BlockSpec((tm, D_MODEL), lambda i: (i, 5)), pl.BlockSpec((tm, D_MODEL), lambda i: (i, 6)),
                  pl.BlockSpec((tm, D_MODEL), lambda i: (i, 7)),
                  *_pair_specs(tm), _mod_spec(l, 2, tm), _mod_spec(l, 3, tm), _mod_spec(l, 4, tm),
                  pl.BlockSpec((None, 3, BRANCH_W, D_MODEL), lambda i: (l, 0, 0, 0)),
                  pl.BlockSpec((None, D_MODEL, D_MODEL), lambda i: (l, 0, 0)),
                  ln_spec, ln_spec,
                  pl.BlockSpec((None, ROUTER_ROWS, D_MODEL), lambda i: (l, 0, 0))],
        out_specs=(pl.BlockSpec((tm, D_MODEL), tok), pl.BlockSpec((tm, MOE_XW), tok)),
        compiler_params=_cparams("arbitrary"),
        name="merge",
    )(a, b, c, p16, p16, p16, xc, xl, mod, mod, mod, wb, wo, lng, lnb, wr)


def _moe_up_kernel(gid_ref, nused_ref, x_ref, w1_ref, w3_ref, hid_ref, w1b, w3b):
    f = pl.program_id(0)
    t = pl.program_id(1)

    @pl.when(t < nused_ref[0])
    def _():
        first = jnp.logical_or(t == 0, gid_ref[t] != gid_ref[jnp.maximum(t - 1, 0)])

        @pl.when(first)
        def _():
            w1b[...] = w1_ref[...].astype(BF16)
            w3b[...] = w3_ref[...].astype(BF16)

        x = x_ref[:, :D_MODEL].astype(BF16)
        rec = x_ref[:, D_MODEL:].astype(F32)
        lane = lax.broadcasted_iota(jnp.int32, rec.shape, 1)
        for j in range(MOE_FE):
            e = f * MOE_FE + j
            a = jnp.dot(x, w1b[j], preferred_element_type=F32)
            b = jnp.dot(x, w3b[j], preferred_element_type=F32)
            gcol = jnp.sum(jnp.where(jnp.logical_or(lane == e, lane == e + EXP_PER_GROUP), rec, 0.0), -1, keepdims=True)
            hid_ref[:, j * D_EXPERT:(j + 1) * D_EXPERT] = (_silu(a) * b * gcol).astype(BF16)


def _moe_tile(t, n):
    return jnp.minimum(t, n[0] - 1)


def _moe_up(gid, nused, xs, w1, w3, l):
    tm = MOE_TM
    npad = MOE_NT * tm
    nf = EXP_PER_GROUP // MOE_FE
    w_spec = pl.BlockSpec((None, MOE_FE, D_MODEL, D_EXPERT),
                          lambda f, t, g, n: (l, nf * g[_moe_tile(t, n)] + f, 0, 0))
    grid_spec = pltpu.PrefetchScalarGridSpec(
        num_scalar_prefetch=2,
        grid=(nf, MOE_NT),
        in_specs=[pl.BlockSpec((tm, MOE_XW), lambda f, t, g, n: (_moe_tile(t, n), 0)), w_spec, w_spec],
        out_specs=pl.BlockSpec((tm, MOE_FE * D_EXPERT), lambda f, t, g, n: (_moe_tile(t, n), f)),
        scratch_shapes=[pltpu.VMEM((MOE_FE, D_MODEL, D_EXPERT), BF16), pltpu.VMEM((MOE_FE, D_MODEL, D_EXPERT), BF16)],
    )
    return pl.pallas_call(
        _moe_up_kernel,
        out_shape=jax.ShapeDtypeStruct((npad, EXP_PER_GROUP * D_EXPERT), BF16),
        grid_spec=grid_spec,
        compiler_params=_cparams("arbitrary", "arbitrary"),
        name="moe_up",
    )(gid, nused, xs, w1, w3)


def _moe_down_kernel(gid_ref, nused_ref, hid_ref, w2_ref, y_ref, w2b):
    t = pl.program_id(0)

    @pl.when(t < nused_ref[0])
    def _():
        first = jnp.logical_or(t == 0, gid_ref[t] != gid_ref[jnp.maximum(t - 1, 0)])

        @pl.when(first)
        def _():
            w2b[...] = w2_ref[...].astype(BF16)

        y_ref[...] = jnp.dot(hid_ref[...], w2b[...], preferred_element_type=F32)


def _moe_down(gid, nused, hid, w2g, l):
    tm = MOE_TM
    npad = MOE_NT * tm
    hw = EXP_PER_GROUP * D_EXPERT
    grid_spec = pltpu.PrefetchScalarGridSpec(
        num_scalar_prefetch=2,
        grid=(MOE_NT,),
        in_specs=[pl.BlockSpec((tm, hw), lambda t, g, n: (_moe_tile(t, n), 0)),
                  pl.BlockSpec((None, None, hw, D_MODEL), lambda t, g, n: (l, g[_moe_tile(t, n)], 0, 0))],
        out_specs=pl.BlockSpec((tm, D_MODEL), lambda t, g, n: (_moe_tile(t, n), 0)),
        scratch_shapes=[pltpu.VMEM((hw, D_MODEL), BF16)],
    )
    return pl.pallas_call(
        _moe_down_kernel,
        out_shape=jax.ShapeDtypeStruct((npad, D_MODEL), F32),
        grid_spec=grid_spec,
        compiler_params=_cparams("arbitrary"),
        name="moe_down",
    )(gid, nused, hid, w2g)


def _moe(h2x, w1, w3, w2g, l):
    tm = MOE_TM
    npad = MOE_NT * tm
    g = h2x[:, D_MODEL + 2 * EXP_PER_GROUP].astype(jnp.int32)
    onehot = (g[:, None] == jnp.arange(N_GROUPS)[None, :]).astype(jnp.int32)
    counts = jnp.sum(onehot, 0)
    rank = jnp.sum((jnp.cumsum(onehot, 0) - onehot) * onehot, 1)
    padded = (counts + tm - 1) // tm * tm
    ends = jnp.cumsum(padded)
    offs = ends - padded
    dest = offs[g] + rank
    src = (jnp.arange(npad, dtype=jnp.int32) % N_TOK).at[dest].set(jnp.arange(N_TOK, dtype=jnp.int32),
                                                                   unique_indices=True)
    starts = jnp.arange(MOE_NT, dtype=jnp.int32) * tm
    tile_gid = jnp.minimum(jnp.sum((ends[None, :] <= starts[:, None]).astype(jnp.int32), 1), N_GROUPS - 1)
    nused = (ends[-1:] // tm).astype(jnp.int32)
    take = lambda arr, idx: arr.at[idx].get(mode="promise_in_bounds", unique_indices=False)
    hid = _moe_up(tile_gid, nused, take(h2x, src), w1, w3, l)
    ys = _moe_down(tile_gid, nused, hid, w2g, l)
    return take(ys, dest)


def _final_kernel(*refs, tm, with_h):
    x1_ref, y_ref, g2_ref, lng_ref, lnb_ref = refs[:5]
    x2 = _layer_norm(DEEPNORM_ALPHA * x1_ref[...] + g2_ref[...] * y_ref[...], lng_ref[...], lnb_ref[...])
    i = pl.program_id(0)
    if with_h:
        sh_ref, sc_ref, xc_ref, xl_ref, h_ref = refs[5:]
        h_ref[...] = (x2 * (1.0 + sc_ref[...]) + sh_ref[...]).astype(BF16)
    else:
        xc_ref, xl_ref = refs[5:]

    @pl.when(i < N_CTX // tm)
    def _():
        xc_ref[...] = x2

    @pl.when(i >= N_CTX // tm)
    def _():
        xl_ref[...] = x2


def _final(x1, y, mod, lng, lnb, l):
    tm = 1024
    tok = lambda i: (i, 0)
    with_h = l + 1 < DEPTH
    ln_spec = pl.BlockSpec((None, None, 1, D_MODEL), lambda i: (l, 1, 0, 0))
    half = jax.ShapeDtypeStruct((N_CTX, D_MODEL), F32)
    in_specs = [pl.BlockSpec((tm, D_MODEL), tok), pl.BlockSpec((tm, D_MODEL), tok), _mod_spec(l, 5, tm), ln_spec, ln_spec]
    args = [x1, y, mod, lng, lnb]
    out_shape = [half, half]
    out_specs = list(_pair_specs(tm))
    if with_h:
        in_specs += [_mod_spec(l + 1, 0, tm), _mod_spec(l + 1, 1, tm)]
        args += [mod, mod]
        out_shape.append(jax.ShapeDtypeStruct((N_TOK, D_MODEL), BF16))
        out_specs.append(pl.BlockSpec((tm, D_MODEL), tok))
    return pl.pallas_call(
        functools.partial(_final_kernel, tm=tm, with_h=with_h),
        out_shape=tuple(out_shape), grid=(N_TOK // tm,), in_specs=in_specs, out_specs=tuple(out_specs),
        compiler_params=_cparams("arbitrary"),
        name="final",
    )(*args)


def kernel(x_prompt, x_sample, c, cache_na_k, cache_na_v, state_mlstm_C, state_mlstm_n, state_mlstm_m, state_hgrn,
           c_ctx, w_mod, b_mod, w_in, b_in, mlstm_fbias, hgrn_lb_logits, na_rpb, w_branch, w_out, ln_g, ln_b,
           w_rg, w_re, w_e1, w_e3, w_e2):
    assert N_CTX == N_LAT
    lb_cum = jnp.cumsum(jax.nn.softmax(hgrn_lb_logits.astype(F32), axis=1), axis=1)
    lb_all = lb_cum - lb_cum[:, :1]

    cs = jnp.zeros((N_MODROWS, D_MODEL), F32).at[0].set(c_ctx).at[1:1 + DEC_BATCH].set(c)
    mod = _modulation(cs, w_mod, b_mod).reshape(DEPTH, N_MODROWS, 6, 1, D_MODEL)

    wb = w_branch.astype(BF16)
    wo = w_out.astype(BF16)
    lng = ln_g.reshape(DEPTH, 2, 1, D_MODEL)
    lnb = ln_b.reshape(DEPTH, 2, 1, D_MODEL)
    wr = jnp.zeros((DEPTH, ROUTER_ROWS, D_MODEL), F32)
    wr = wr.at[:, :N_GROUPS].set(jnp.swapaxes(w_rg, 1, 2)).at[:, ROUTER_E0:ROUTER_E0 + N_EXPERTS].set(jnp.swapaxes(w_re, 1, 2))
    w2g = w_e2.reshape(DEPTH, N_GROUPS, EXP_PER_GROUP * D_EXPERT, D_MODEL)
    b_main = jnp.concatenate([b_in[:, :GATE_COL0], b_in[:, GATE_COL0 + N_GATES:]], 1)

    xc = x_prompt.reshape(N_CTX, D_MODEL)
    xl = x_sample.reshape(N_LAT, D_MODEL)
    w_t = jnp.swapaxes(w_in, 1, 2)
    na_bias = jax.vmap(_na_bias_table)(na_rpb)
    h = _prep(xc, xl, mod)

    kv = (None, None)
    ma_states = None
    hg_state = None
    for l in range(DEPTH):
        p16, p32 = _inproj(h, w_t, b_main[l][None, :], l)
        gcol, gt = _gates(h, w_t, b_in.reshape(DEPTH, 1, N_IN), l)
        gt3 = gt.reshape(N_GATES, N_TOK // MA_CHUNK, MA_CHUNK).transpose(1, 0, 2)

        a, *ma_states = _mlstm(p16, gcol, gt3, mlstm_fbias[l], l, False, prev=ma_states)
        a = _mlstm(p16, gcol, gt3, mlstm_fbias[l], l, True, state_mlstm_C, state_mlstm_n, state_mlstm_m, a_out=a)
        b, *kv = _ctx_attention(p16, l, *kv)
        b = _lat_attention(p16, cache_na_k, cache_na_v, na_bias, l, b)
        cc, hg_state = _hgrn(p16, p32, lb_all[:, l], l, False, prev=hg_state)
        cc = _hgrn(p16, p32, lb_all[:, l], l, True, state_hgrn, c_out=cc)

        x1, h2x = _merge(a, b, cc, p16, xc, xl, mod, wb, wo, lng, lnb, wr, l)
        y2 = _moe(h2x, w_e1, w_e3, w2g, l)
        outs = _final(x1, y2, mod, lng, lnb, l)
        xc, xl = outs[0], outs[1]
        if l + 1 < DEPTH:
            h = outs[2]

    dt = x_prompt.dtype
    new_C, new_n, new_m = ma_states
    new_n = new_n.reshape(BATCH, DEPTH, 2, MA_HEADS, MA_DK)
    new_m = new_m[:, :, :, 0].reshape(BATCH, DEPTH, 2, MA_HEADS)
    return (xc.reshape(BATCH, SEQ, D_MODEL), xl.reshape(DEC_BATCH, DEC_SEQ, D_MODEL), kv[0], kv[1],
            new_C.astype(dt), new_n.astype(dt), new_m.astype(dt), hg_state.astype(dt))
```

```python
import functools

import numpy as np
import jax
import jax.numpy as jnp
from jax import lax
from jax.experimental import pallas as pl
from jax.experimental.pallas import tpu as pltpu

F32 = jnp.float32
BF16 = jnp.bfloat16
HI = lax.Precision.HIGHEST

D_MODEL = 1024
BATCH = 16
SEQ = 256
DEPTH = 2
DEC_BATCH = 4
DEC_SEQ = 1024
PAST_LEN = 256
GRID_W = 64
MA_HEADS = 4
MA_DK = 128
MA_DV = 128
MA_CHUNK = 128
NA_HEADS = 8
NA_DH = 64
NA_KR_MAX = 8
NA_KC = 16
HG_HEADS = 4
HG_DK = 128
HG_DV = 128
HG_CHUNK = 64
BRANCH_W = 512
N_GROUPS = 4
EXP_PER_GROUP = 4
N_EXPERTS = N_GROUPS * EXP_PER_GROUP
D_EXPERT = 512
ROPE_BASE = 10000.0
LN_EPS = 1e-5
RMS_EPS = 1e-6
DEEPNORM_ALPHA = (2 * DEPTH) ** 0.25

N_CTX = BATCH * SEQ
N_LAT = DEC_BATCH * DEC_SEQ
N_TOK = N_CTX + N_LAT
N_MODROWS = 8
GATE_COL0 = 4 * BRANCH_W
N_GATES = 4 * MA_HEADS
N_IN = 9232
P_COLS = N_IN - N_GATES
MOE_TM = 512
MOE_NT = N_TOK // MOE_TM + N_GROUPS
MOE_FE = 4
MOE_XW = D_MODEL + 128
MOE_XDT = F32
ROUTER_ROWS = 32
ROUTER_E0 = 8
VMEM_LIMIT = 56 * 1024 * 1024

_NT = (((1,), (1,)), ((), ()))


def _cparams(*sem):
    return pltpu.CompilerParams(dimension_semantics=sem, vmem_limit_bytes=VMEM_LIMIT)


def _mod_row(tile, tm):
    return jnp.maximum((tile * tm) // DEC_SEQ - (N_CTX // DEC_SEQ - 1), 0)


def _mod_spec(l, part, tm):
    return pl.BlockSpec((None, None, None, 1, D_MODEL), lambda i: (l, _mod_row(i, tm), part, 0, 0))


def _pair_specs(tm):
    nc = N_CTX // tm
    return (pl.BlockSpec((tm, D_MODEL), lambda i: (jnp.minimum(i, nc - 1), 0)),
            pl.BlockSpec((tm, D_MODEL), lambda i: (jnp.maximum(i - nc, 0), 0)))


def _pair_read(i, tm, c_ref, l_ref):
    return jnp.where(i < N_CTX // tm, c_ref[...], l_ref[...])


def _silu(x):
    return x * jax.nn.sigmoid(x)


def _layer_norm(x, g, b):
    mu = jnp.mean(x, -1, keepdims=True)
    xc = x - mu
    var = jnp.mean(xc * xc, -1, keepdims=True)
    return xc * lax.rsqrt(var + LN_EPS) * g + b


def _log_sigmoid(x):
    return jnp.minimum(x, 0.0) - jnp.log(1.0 + jnp.exp(-jnp.abs(x)))


def _tri(n, upper):
    r = lax.broadcasted_iota(jnp.int32, (n, n), 0)
    c = lax.broadcasted_iota(jnp.int32, (n, n), 1)
    return jnp.where((r <= c) if upper else (r >= c), 1.0, 0.0).astype(F32)


def _dot3(a, b, split_b):
    x = b if split_b else a
    hi = x.astype(BF16)
    r = x - hi.astype(F32)
    mid = r.astype(BF16)
    lo = (r - mid.astype(F32)).astype(BF16)
    one = (a if split_b else b).astype(BF16)
    dot = (lambda p: jnp.dot(one, p, preferred_element_type=F32)) if split_b else (
        lambda p: jnp.dot(p, one, preferred_element_type=F32))
    return dot(hi) + dot(mid) + dot(lo)


def _mod_kernel(c_ref, w_ref, b_ref, o_ref):
    s = _silu(c_ref[...])
    o_ref[...] = jnp.dot(s.astype(BF16), w_ref[...].astype(BF16), preferred_element_type=F32) + b_ref[...]


def _modulation(cs, w_mod, b_mod):
    tn = 1024
    return pl.pallas_call(
        _mod_kernel,
        out_shape=jax.ShapeDtypeStruct((DEPTH, N_MODROWS, 6 * D_MODEL), F32),
        grid=(DEPTH, 6 * D_MODEL // tn),
        in_specs=[pl.BlockSpec((N_MODROWS, D_MODEL), lambda l, j: (0, 0)),
                  pl.BlockSpec((None, D_MODEL, tn), lambda l, j: (l, 0, j)),
                  pl.BlockSpec((None, 1, tn), lambda l, j: (l, 0, j))],
        out_specs=pl.BlockSpec((None, N_MODROWS, tn), lambda l, j: (l, 0, j)),
        compiler_params=_cparams("arbitrary", "arbitrary"),
        name="modulation",
    )(cs, w_mod, b_mod.reshape(DEPTH, 1, 6 * D_MODEL))


def _prep_kernel(xc_ref, xl_ref, sh_ref, sc_ref, h_ref, *, tm):
    x = _pair_read(pl.program_id(0), tm, xc_ref, xl_ref)
    h_ref[...] = (x * (1.0 + sc_ref[...]) + sh_ref[...]).astype(BF16)


def _prep(xc, xl, mod):
    tm = 1024
    return pl.pallas_call(
        functools.partial(_prep_kernel, tm=tm),
        out_shape=jax.ShapeDtypeStruct((N_TOK, D_MODEL), BF16),
        grid=(N_TOK // tm,),
        in_specs=[*_pair_specs(tm), _mod_spec(0, 0, tm), _mod_spec(0, 1, tm)],
        out_specs=pl.BlockSpec((tm, D_MODEL), lambda i: (i, 0)),
        compiler_params=_cparams("arbitrary"),
        name="prep",
    )(xc, xl, mod, mod)


INPROJ_TN = 512
N_PLAIN_TILES = GATE_COL0 // INPROJ_TN


F32_TILE0 = 7
N_F32_TILES = 2
P16_COLS = P_COLS - N_F32_TILES * INPROJ_TN


def _inproj_kernel(h_ref, wa_ref, wb_ref, b_ref, o_ref, *, src_tile):
    j = src_tile(pl.program_id(1))

    @pl.when(j < N_PLAIN_TILES)
    def _():
        o_ref[...] = (lax.dot_general(h_ref[...], wa_ref[...].astype(BF16), _NT, preferred_element_type=F32)
                      + b_ref[...]).astype(o_ref.dtype)

    @pl.when(j >= N_PLAIN_TILES)
    def _():
        w = jnp.concatenate([wa_ref[N_GATES:, :], wb_ref[...]], 0)
        o_ref[...] = (lax.dot_general(h_ref[...], w.astype(BF16), _NT, preferred_element_type=F32)
                      + b_ref[...]).astype(o_ref.dtype)


def _inproj_call(h, w_t, b_main, l, tm, n_tiles, src_tile, dtype, name):
    tn = INPROJ_TN
    return pl.pallas_call(
        functools.partial(_inproj_kernel, src_tile=src_tile),
        out_shape=jax.ShapeDtypeStruct((N_TOK, n_tiles * tn), dtype),
        grid=(N_TOK // tm, n_tiles),
        in_specs=[pl.BlockSpec((tm, D_MODEL), lambda i, j: (i, 0)),
                  pl.BlockSpec((None, tn, D_MODEL), lambda i, j: (l, src_tile(j), 0)),
                  pl.BlockSpec((None, N_GATES, D_MODEL), lambda i, j: (l, (src_tile(j) + 1) * (tn // N_GATES), 0)),
                  pl.BlockSpec((1, tn), lambda i, j: (0, src_tile(j)))],
        out_specs=pl.BlockSpec((tm, tn), lambda i, j: (i, j)),
        compiler_params=_cparams("arbitrary", "arbitrary"),
        name=name,
    )(h, w_t, w_t, b_main)


def _inproj(h, w_t, b_main, l):
    skip_f32 = lambda j: jnp.where(j < F32_TILE0, j, j + N_F32_TILES)
    p16 = _inproj_call(h, w_t, b_main, l, 4096, P16_COLS // INPROJ_TN, skip_f32, BF16, "inproj")
    p32 = _inproj_call(h, w_t, b_main, l, 4096, N_F32_TILES, lambda j: j + F32_TILE0, F32, "inproj_f32")
    return p16, p32


def _gates_kernel(h_ref, w_ref, b_ref, gc_ref, gt_ref):
    g = lax.dot_general(h_ref[...], w_ref[...].astype(BF16), _NT, preferred_element_type=F32) + b_ref[...]
    gc_ref[...] = g
    gt_ref[...] = g.T[:N_GATES]


def _gates(h, w_t, b_in3, l):
    tm = 1024
    gblk = GATE_COL0 // 128
    return pl.pallas_call(
        _gates_kernel,
        out_shape=(jax.ShapeDtypeStruct((N_TOK, 128), F32), jax.ShapeDtypeStruct((N_GATES, N_TOK), F32)),
        grid=(N_TOK // tm,),
        in_specs=[pl.BlockSpec((tm, D_MODEL), lambda i: (i, 0)),
                  pl.BlockSpec((None, 128, D_MODEL), lambda i: (l, gblk, 0)),
                  pl.BlockSpec((None, 1, 128), lambda i: (l, 0, gblk))],
        out_specs=(pl.BlockSpec((tm, 128), lambda i: (i, 0)), pl.BlockSpec((N_GATES, tm), lambda i: (0, i))),
        compiler_params=_cparams("arbitrary"),
        name="gates",
    )(h, w_t, b_in3)


HEADS_PER_BLK = 128 // NA_DH
NA_NBLK = NA_HEADS // HEADS_PER_BLK
NA_QSCALE = NA_DH ** -0.5
Q_COL, K_COL, V_COL = 16, 20, 24
QKV_COL = 4


def _ctx_attn_kernel(*refs):
    q_ref, k_ref, v_ref = refs[:3]
    o_ref, ko_ref, vo_ref = refs[-3:]
    heads = range(NA_HEADS)
    split = lambda x: jnp.stack([x[:, h * NA_DH:(h + 1) * NA_DH] for h in heads], 0)
    q = split(q_ref[...] * NA_QSCALE)
    k = split(k_ref[...])
    v = split(v_ref[...])
    ko_ref[...] = k.astype(F32)
    vo_ref[...] = v.astype(F32)
    s = lax.dot_general(q, k, (((2,), (2,)), ((0,), (0,))), preferred_element_type=F32)
    e = jnp.exp(s - jnp.max(s, -1, keepdims=True))
    p = e * (1.0 / jnp.sum(e, -1, keepdims=True))
    o = lax.dot_general(p.astype(BF16), v, (((2,), (1,)), ((0,), (0,))), preferred_element_type=F32)
    o_ref[...] = jnp.concatenate([o[h] for h in heads], -1).astype(BF16)


def _ctx_attention(p16, l, prev_k=None, prev_v=None):
    kv_shape = jax.ShapeDtypeStruct((BATCH, DEPTH, NA_HEADS, SEQ, NA_DH), F32)
    kv_spec = pl.BlockSpec((None, None, NA_HEADS, SEQ, NA_DH), lambda b: (b, l, 0, 0, 0))
    col = lambda j: pl.BlockSpec((SEQ, BRANCH_W), lambda b: (b, j))
    in_specs = [col(QKV_COL), col(QKV_COL + 1), col(QKV_COL + 2)]
    args = [p16, p16, p16]
    aliases = {}
    if prev_k is not None:
        in_specs += [pl.BlockSpec(memory_space=pl.ANY)] * 2
        args += [prev_k, prev_v]
        aliases = {3: 1, 4: 2}
    return pl.pallas_call(
        _ctx_attn_kernel,
        out_shape=(jax.ShapeDtypeStruct((N_TOK, BRANCH_W), BF16), kv_shape, kv_shape),
        grid=(BATCH,),
        in_specs=in_specs,
        out_specs=(pl.BlockSpec((SEQ, BRANCH_W), lambda b: (b, 0)), kv_spec, kv_spec),
        input_output_aliases=aliases,
        compiler_params=_cparams("arbitrary"),
        name="ctx_attention",
    )(*args)


NA_ROWS = DEC_SEQ // GRID_W
NA_KR = min(NA_KR_MAX, NA_ROWS)
NA_QROWS = 4
NA_QT = NA_ROWS // NA_QROWS
NA_WROWS = NA_KR + NA_QROWS - 1
NA_WKEYS = NA_WROWS * GRID_W


def _na_window_start(t):
    return min(max(t * NA_QROWS - NA_KR // 2, 0), NA_ROWS - NA_WROWS)


def _na_bias_table(rpb):
    c = np.arange(GRID_W)
    c0 = np.clip(c - NA_KC // 2, 0, GRID_W - NA_KC)
    kc = np.arange(GRID_W)
    valid = (kc[None, :] >= c0[:, None]) & (kc[None, :] < c0[:, None] + NA_KC)
    dc = kc[None, :] - c[:, None] + NA_KC - 1
    onehot = (dc[None] == np.arange(2 * NA_KC - 1)[:, None, None]) & valid[None]
    toep = jnp.einsum('hrd,dcx->hrcx', rpb.astype(F32), jnp.asarray(onehot, F32), precision=HI)
    toep = jnp.where(valid[None, None], toep, -jnp.inf)
    ninf = jnp.full((NA_HEADS, GRID_W, GRID_W), -jnp.inf, F32)
    tiles = []
    for t in range(NA_QT):
        w0 = _na_window_start(t)
        qrows = []
        for r in range(t * NA_QROWS, (t + 1) * NA_QROWS):
            r0 = min(max(r - NA_KR // 2, 0), NA_ROWS - NA_KR)
            blocks = []
            for kr in range(w0, w0 + NA_WROWS):
                inside = r0 <= kr < r0 + NA_KR
                blocks.append(toep[:, kr - r + NA_KR_MAX - 1] if inside else ninf)
            qrows.append(jnp.concatenate(blocks, -1))
        tiles.append(jnp.concatenate(qrows, 1))
    return jnp.stack(tiles, 1).astype(BF16)


def _lat_attn_kernel(q_ref, k_ref, v_ref, ck_ref, cv_ref, bias_ref, prev_ref, o_ref):
    heads = range(HEADS_PER_BLK)
    split = lambda x: jnp.stack([x[:, h * NA_DH:(h + 1) * NA_DH] for h in heads], 0)
    bnt = (((2,), (2,)), ((0,), (0,)))
    bnn = (((2,), (1,)), ((0,), (0,)))
    q = (q_ref[...] * NA_QSCALE).astype(BF16)
    k = k_ref[...].astype(BF16)
    v = v_ref[...].astype(BF16)
    ck = ck_ref[...].astype(BF16)
    cv = cv_ref[...].astype(BF16)
    nq = NA_QROWS * GRID_W
    for t in range(NA_QT):
        w0 = _na_window_start(t)
        qs = slice(t * nq, (t + 1) * nq)
        ws = slice(w0 * GRID_W, (w0 + NA_WROWS) * GRID_W)
        qh = split(q[qs])
        s_loc = lax.dot_general(qh, split(k[ws]), bnt, preferred_element_type=F32) + bias_ref[:, t]
        s_ctx = lax.dot_general(qh, ck, bnt, preferred_element_type=F32)
        m = jnp.maximum(jnp.max(s_loc, -1, keepdims=True), jnp.max(s_ctx, -1, keepdims=True))
        e_loc = jnp.exp(s_loc - m)
        e_ctx = jnp.exp(s_ctx - m)
        inv = 1.0 / (jnp.sum(e_loc, -1, keepdims=True) + jnp.sum(e_ctx, -1, keepdims=True))
        acc = (lax.dot_general(e_loc.astype(BF16), split(v[ws]), bnn, preferred_element_type=F32)
               + lax.dot_general(e_ctx.astype(BF16), cv, bnn, preferred_element_type=F32)) * inv
        o_ref[qs, :] = jnp.concatenate([acc[h] for h in heads], -1).astype(BF16)


def _lat_attention(proj, ck, cv, bias, l, b_out):
    rb0 = N_CTX // DEC_SEQ
    cb = lambda base: (lambda j, b: (rb0 + b, base + j))
    c_spec = pl.BlockSpec((None, None, HEADS_PER_BLK, PAST_LEN, NA_DH), lambda j, b: (b, l, j, 0, 0))
    return pl.pallas_call(
        _lat_attn_kernel,
        out_shape=jax.ShapeDtypeStruct((N_TOK, BRANCH_W), BF16),
        grid=(NA_NBLK, DEC_BATCH),
        in_specs=[pl.BlockSpec((DEC_SEQ, 128), cb(Q_COL)), pl.BlockSpec((DEC_SEQ, 128), cb(K_COL)),
                  pl.BlockSpec((DEC_SEQ, 128), cb(V_COL)), c_spec, c_spec,
                  pl.BlockSpec((None, HEADS_PER_BLK, NA_QT, NA_QROWS * GRID_W, NA_WKEYS), lambda j, b: (l, j, 0, 0, 0)),
                  pl.BlockSpec(memory_space=pl.ANY)],
        out_specs=pl.BlockSpec((DEC_SEQ, 128), lambda j, b: (rb0 + b, j)),
        input_output_aliases={6: 0},
        compiler_params=_cparams("arbitrary", "arbitrary"),
        name="lat_attention",
    )(proj, proj, proj, ck, cv, bias, b_out)


MA_KSCALE = MA_DK ** -0.5


def _rope_tables(T):
    t = np.arange(T)
    half = MA_DK // 2
    inv = ROPE_BASE ** (-jnp.arange(0, half, 2, dtype=F32) / half)
    ang_r = jnp.asarray((t // GRID_W).astype(np.float32))[:, None] * inv[None, :]
    ang_c = jnp.asarray((t % GRID_W).astype(np.float32))[:, None] * inv[None, :]
    cos = jnp.concatenate([jnp.cos(ang_r)] * 2 + [jnp.cos(ang_c)] * 2, -1)
    sin = jnp.concatenate([-jnp.sin(ang_r), jnp.sin(ang_r), -jnp.sin(ang_c), jnp.sin(ang_c)], -1)
    return cos, sin


def _mlstm_kernel(*refs, T, latent):
    if latent:
        (p_ref, gc_ref, gt_ref, fbc_ref, fbr_ref, cos_ref, sin_ref, c0_ref, n0_ref, m0_ref, prev_ref,
         a_ref, qs, ks, vT, hfT, hbT, CT, ns, ms, brs, kcs) = refs
    else:
        p_ref, gc_ref, gt_ref, fbc_ref, fbr_ref = refs[:5]
        a_ref, co_ref, no_ref, mo_ref, qs, ks, vT, hfT, hbT, CT, ns, ms, brs, kcs = refs[-14:]
    L = MA_CHUNK
    NC = T // L
    W = BRANCH_W
    PER = 128 // L

    lane = lax.broadcasted_iota(jnp.int32, (128, MA_DK), 1)
    lo_half = (lane % (MA_DK // 2)) < (MA_DK // 4)

    def v_block(tb, carry):
        r0 = pl.multiple_of(tb * 128, 128)
        rows = pl.ds(r0, 128)

        def rope(x):
            if not latent:
                return x
            swapped = jnp.where(lo_half, pltpu.roll(x, MA_DK - MA_DK // 4, 1), pltpu.roll(x, MA_DK // 4, 1))
            return x * cos_ref[rows, :] + swapped * sin_ref[rows, :]

        for h in range(MA_HEADS):
            hs = slice(h * MA_DK, (h + 1) * MA_DK)
            qs[rows, hs] = rope(p_ref[rows, hs].astype(F32)).astype(BF16)
            ks[rows, hs] = rope(p_ref[rows, W + h * MA_DK:W + (h + 1) * MA_DK].astype(F32) * MA_KSCALE).astype(BF16)
        for h in range(MA_HEADS):
            hs = slice(h * MA_DV, (h + 1) * MA_DV)
            blk = p_ref[pl.ds(r0, 128), 2 * W + h * MA_DV:2 * W + (h + 1) * MA_DV].astype(F32).T.astype(BF16)
            for j in range(PER):
                vT[tb * PER + j, h] = blk[:, j * L:(j + 1) * L]
        return carry

    lax.fori_loop(0, T // 128, v_block, 0)

    for d in range(2):
        for h in range(MA_HEADS):
            sidx = d * MA_HEADS + h
            CT[sidx] = c0_ref[d, h].T if latent else jnp.zeros((MA_DV, MA_DK), F32)
            ns[sidx] = n0_ref[sidx:sidx + 1, :] if latent else jnp.zeros((1, MA_DK), F32)
            ms[sidx] = m0_ref[sidx:sidx + 1, :] if latent else jnp.zeros((1, 128), F32)

    low = _tri(L, False)
    upp = _tri(L, True)
    rr = lax.broadcasted_iota(jnp.int32, (L, L), 0)
    cc = lax.broadcasted_iota(jnp.int32, (L, L), 1)
    tri_cols = jnp.concatenate([upp, low], 1)
    tri_rows = jnp.concatenate([low, upp], 0)
    fbc = fbc_ref[...]
    fbr = fbr_ref[...]

    def gate_sums(c, carry):
        t0 = pl.multiple_of(c * L, L)
        gc = gc_ref[pl.ds(t0, L), :]
        lfc = _log_sigmoid(gc + fbc)
        lfr = _log_sigmoid(gt_ref[c] + fbr)
        ish = pltpu.roll(gc, MA_HEADS, 1)
        brow = _dot3(lfr, tri_cols, False)
        bcol = _dot3(tri_rows, lfc, True)
        brs[0, c] = brow[:, :L]
        brs[1, c] = brow[:, L:]
        kcs[0, pl.ds(t0, L), :] = ish - bcol[:L]
        kcs[1, pl.ds(t0, L), :] = ish - bcol[L:]
        return carry

    lax.fori_loop(0, NC, gate_sums, 0, unroll=min(4, NC))

    def chunk_pair(cf, cb):
        H = MA_HEADS
        G = 2 * H
        heads = range(H)
        cs, ts = (cf, cb), (pl.multiple_of(cf * L, L), pl.multiple_of(cb * L, L))
        rows_of = lambda x, r0: [x[r0 + h:r0 + h + 1, :] for h in heads]
        cols_of = lambda x, c0: [x[:, c0 + h:c0 + h + 1] for h in heads]
        br = jnp.stack(sum([rows_of(brs[d, cs[d]], 2 * d * H + H) for d in range(2)], []), 0)
        ir = jnp.stack(sum([rows_of(gt_ref[cs[d]], 2 * d * H) for d in range(2)], []), 0)
        kcol = jnp.stack(sum([cols_of(kcs[d, pl.ds(ts[d], L), :], 2 * d * H + H) for d in range(2)], []), 0)
        split = lambda x: [x[:, h * MA_DK:(h + 1) * MA_DK] for h in heads]
        q = jnp.stack(split(qs[pl.ds(ts[0], L), :]) + split(qs[pl.ds(ts[1], L), :]), 0)
        k = jnp.stack(split(ks[pl.ds(ts[0], L), :]) + split(ks[pl.ds(ts[1], L), :]), 0)
        vt = jnp.concatenate([vT[cf], vT[cb]], 0)
        m = ms[...][:, :, 0:1]
        n = ns[...]
        ct = CT[...]
        bnt = (((2,), (2,)), ((0,), (0,)))
        bnn = (((2,), (1,)), ((0,), (0,)))
        pre = br + kcol
        dmat = jnp.concatenate([jnp.where(rr <= cc, pre[:H], -jnp.inf), jnp.where(rr >= cc, pre[H:], -jnp.inf)], 0)
        g = br + m
        m_t = jnp.maximum(g, jnp.max(dmat, 1, keepdims=True))
        w_inter = jnp.exp(g - m_t)
        s = lax.dot_general(k, q, bnt, preferred_element_type=F32) * jnp.exp(dmat - m_t)
        ctn = jnp.concatenate([ct.astype(BF16), jnp.broadcast_to(n, (G, 8, MA_DK)).astype(BF16)], 1)
        cq = lax.dot_general(ctn, q, bnt, preferred_element_type=F32)
        num = w_inter * cq[:, :MA_DV] + lax.dot_general(vt, s.astype(BF16), bnn, preferred_element_type=F32)
        den = w_inter * cq[:, MA_DV:MA_DV + 1] + jnp.sum(s, 1, keepdims=True)
        hout = num / jnp.maximum(jnp.abs(den), jnp.exp(-m_t))
        hfT[cf] = hout[:H]
        hbT[cb] = hout[H:]
        last = lambda x: jnp.concatenate([x[:H, :, L - 1:L], x[H:, :, 0:1]], 0)
        m_new = last(m_t)
        b_last = last(br)
        decay = jnp.exp(b_last + m - m_new)
        wk = jnp.exp(b_last - br + ir - m_new)
        wk_hi = wk.astype(BF16)
        wk_lo = (wk - wk_hi.astype(F32)).astype(BF16)
        lhs = jnp.concatenate([(vt.astype(F32) * wk).astype(BF16), wk_hi, wk_lo, jnp.zeros((G, 6, L), BF16)], 1)
        upd = lax.dot_general(lhs, k, bnn, preferred_element_type=F32)
        CT[...] = decay * ct + upd[:, :MA_DV]
        ns[...] = decay * n + upd[:, MA_DV:MA_DV + 1] + upd[:, MA_DV + 1:MA_DV + 2]
        ms[...] = jnp.broadcast_to(m_new, (G, 1, 128))

    def body(i, carry):
        chunk_pair(i, NC - 1 - i)
        return carry

    lax.fori_loop(0, NC, body, 0, unroll=min(4, NC))

    def out_block(tb, carry):
        r0 = pl.multiple_of(tb * 128, 128)
        hsum = jnp.concatenate([hfT[tb * PER + j] + hbT[tb * PER + j] for j in range(PER)], 2)
        outs = [hsum[h].T for h in range(MA_HEADS)]
        gate = jax.nn.sigmoid(p_ref[pl.ds(r0, 128), 3 * W:4 * W].astype(F32))
        a_ref[pl.ds(r0, 128), :] = (gate * jnp.concatenate(outs, 1)).astype(BF16)
        return carry

    lax.fori_loop(0, T // 128, out_block, 0)
    if not latent:
        for d in range(2):
            for h in range(MA_HEADS):
                sidx = d * MA_HEADS + h
                co_ref[d, h] = CT[sidx].T
                no_ref[sidx:sidx + 1, :] = ns[sidx]
                mo_ref[sidx:sidx + 1, :] = ms[sidx]


def _mlstm(proj, gcol, gt3, fbias_l, l, latent, C0=None, n0=None, m0=None, a_out=None, prev=None):
    T = DEC_SEQ if latent else SEQ
    B = DEC_BATCH if latent else BATCH
    rb0 = N_CTX // DEC_SEQ if latent else 0
    fb = fbias_l.astype(F32)
    fbc = jnp.zeros((1, 128), F32).at[0, MA_HEADS:2 * MA_HEADS].set(fb[0]).at[0, 3 * MA_HEADS:4 * MA_HEADS].set(fb[1])
    fbr = fbc[0, :N_GATES].reshape(N_GATES, 1)
    full2 = lambda b: (0, 0)
    any_spec = pl.BlockSpec(memory_space=pl.ANY)
    in_specs = [pl.BlockSpec((T, 4 * BRANCH_W), lambda b: (rb0 + b, 0)),
                pl.BlockSpec((T, 128), lambda b: (rb0 + b, 0)),
                pl.BlockSpec((T // MA_CHUNK, N_GATES, MA_CHUNK), lambda b: (rb0 + b, 0, 0)),
                pl.BlockSpec((1, 128), full2), pl.BlockSpec((N_GATES, 1), full2)]
    args = [proj, gcol, gt3, fbc, fbr]
    a_shape = jax.ShapeDtypeStruct((N_TOK, BRANCH_W), BF16)
    a_spec = pl.BlockSpec((T, BRANCH_W), lambda b: (rb0 + b, 0))
    c_spec = pl.BlockSpec((None, None, 2, MA_HEADS, MA_DK, MA_DV), lambda b: (b, l, 0, 0, 0, 0))
    nm_spec = pl.BlockSpec((None, None, 2 * MA_HEADS, 128), lambda b: (b, l, 0, 0))
    aliases = {}
    if latent:
        cos, sin = _rope_tables(T)
        nb = 2 * MA_HEADS
        in_specs += [pl.BlockSpec((T, MA_DK), full2), pl.BlockSpec((T, MA_DK), full2), c_spec, nm_spec, nm_spec, any_spec]
        args += [cos, sin, C0, n0.reshape(B, DEPTH, nb, MA_DK),
                 jnp.broadcast_to(m0.reshape(B, DEPTH, nb, 1), (B, DEPTH, nb, 128)), a_out]
        aliases = {len(args) - 1: 0}
        out_shape, out_specs = a_shape, a_spec
    else:
        nm_shape = jax.ShapeDtypeStruct((B, DEPTH, 2 * MA_HEADS, 128), F32)
        out_shape = (a_shape, jax.ShapeDtypeStruct((B, DEPTH, 2, MA_HEADS, MA_DK, MA_DV), F32), nm_shape, nm_shape)
        out_specs = (a_spec, c_spec, nm_spec, nm_spec)
        if prev is not None:
            in_specs += [any_spec] * 3
            args += list(prev)
            aliases = {len(args) - 3: 1, len(args) - 2: 2, len(args) - 1: 3}
    nc = T // MA_CHUNK
    scratch = [pltpu.VMEM((T, BRANCH_W), BF16), pltpu.VMEM((T, BRANCH_W), BF16),
               pltpu.VMEM((nc, MA_HEADS, MA_DV, MA_CHUNK), BF16),
               pltpu.VMEM((nc, MA_HEADS, MA_DV, MA_CHUNK), F32), pltpu.VMEM((nc, MA_HEADS, MA_DV, MA_CHUNK), F32),
               pltpu.VMEM((2 * MA_HEADS, MA_DV, MA_DK), F32), pltpu.VMEM((2 * MA_HEADS, 1, MA_DK), F32),
               pltpu.VMEM((2 * MA_HEADS, 1, 128), F32),
               pltpu.VMEM((2, nc, N_GATES, MA_CHUNK), F32), pltpu.VMEM((2, T, 128), F32)]
    return pl.pallas_call(
        functools.partial(_mlstm_kernel, T=T, latent=latent),
        out_shape=out_shape, grid=(B,), in_specs=in_specs, out_specs=out_specs, scratch_shapes=scratch,
        input_output_aliases=aliases,
        compiler_params=_cparams("arbitrary"),
        name="mlstm_lat" if latent else "mlstm_ctx",
    )(*args)


HG_SUB = 8


def _hgrn_kernel(*refs, T, latent):
    ff_ref, fb_ref, q_ref, i_ref, g_ref, lbf_ref, lbb_ref = refs[:7]
    if latent:
        s0_ref = refs[7]
        c_ref, of, ob, ST, iT, As, Bs = refs[-7:]
    else:
        c_ref, so_ref, of, ob, ST, iT, As, Bs = refs[-8:]
    L = HG_CHUNK
    NC = T // L
    NB = L // HG_SUB
    DK = HG_DK

    for d in range(2):
        for h in range(HG_HEADS):
            ST[d * HG_HEADS + h] = s0_ref[d, h].T if latent else jnp.zeros((HG_DV, DK), F32)

    PER = 128 // L

    def i_block(tb, carry):
        r0 = pl.multiple_of(tb * 128, 128)
        for h in range(HG_HEADS):
            blk = i_ref[pl.ds(r0, 128), h * HG_DV:(h + 1) * HG_DV].astype(F32).T.astype(BF16)
            for j in range(PER):
                iT[tb * PER + j, h] = blk[:, j * L:(j + 1) * L]
        return carry

    lax.fori_loop(0, T // 128, i_block, 0)

    low = _tri(L, False)
    upp = _tri(L, True)
    row8 = lax.broadcasted_iota(jnp.int32, (HG_SUB, L), 0)
    lane_s = lax.broadcasted_iota(jnp.int32, (HG_SUB, L), 1)
    heads = range(HG_HEADS)
    bnt = (((2,), (2,)), ((0,), (0,)))
    bnn = (((2,), (1,)), ((0,), (0,)))
    LOG2E = 1.4426950408889634

    def split(x):
        return jnp.stack([x[:, h * DK:(h + 1) * DK] for h in heads], 0)

    def decay_sums(c, carry):
        t0 = pl.multiple_of(c * L, L)
        for d in range(2):
            fpre = (ff_ref if d == 0 else fb_ref)[pl.ds(t0, L), :]
            lb = (lbf_ref if d == 0 else lbb_ref)[...]
            f = lb + (1.0 - lb) * jax.nn.sigmoid(fpre)
            a = _dot3(low if d == 0 else upp, jnp.log(f) * LOG2E, True)
            As[d, pl.ds(t0, L), :] = a
            Bs[d, pl.ds(t0, L), :] = a - jnp.log(1.0 - f) * LOG2E
        return carry

    lax.fori_loop(0, NC, decay_sums, 0, unroll=4)

    def chunk_pair(cf, cb):
        tf = pl.multiple_of(cf * L, L)
        tb = pl.multiple_of(cb * L, L)
        both = lambda fn: jnp.concatenate([fn(0, tf), fn(1, tb)], 0)
        A = both(lambda d, t: split(As[d, pl.ds(t, L), :]))
        B = both(lambda d, t: split(Bs[d, pl.ds(t, L), :]))
        q = both(lambda d, t: split(_silu(q_ref[pl.ds(t, L), :].astype(F32))))
        iv = both(lambda d, t: split(i_ref[pl.ds(t, L), :].astype(BF16)))
        ivT = jnp.concatenate([iT[cf], iT[cb]], 0)
        H = HG_HEADS
        st = ST[...]
        o = lax.dot_general((q * jnp.exp2(A)).astype(BF16), st.astype(BF16), bnt, preferred_element_type=F32)
        a_last = jnp.concatenate([A[:H, L - 1:L], A[H:, 0:1]], 0)
        kd = jnp.exp2(a_last - B).astype(BF16)
        rows = []
        for I in range(NB):
            lo, hi = I * HG_SUB, (I + 1) * HG_SUB
            A_I, q_I = A[:, lo:hi], q[:, lo:hi]
            att_f = jnp.zeros((H, HG_SUB, L), F32)
            att_b = jnp.zeros((H, HG_SUB, L), F32)
            for j in range(HG_SUB):
                s = lo + j
                col = jnp.sum(q_I * jnp.exp2(A_I - B[:, s:s + 1]), -1, keepdims=True)
                att_f = jnp.where((lane_s == s) & (row8 >= j), col[:H], att_f)
                att_b = jnp.where((lane_s == s) & (row8 <= j), col[H:], att_b)
            rf, rb = max(lo - 1, 0), min(hi, L - 1)
            R = jnp.concatenate([A[:H, rf:rf + 1], A[H:, rb:rb + 1]], 0)
            zeros = lambda n: jnp.zeros((H, n, DK), BF16)
            ksc_f = jnp.concatenate([jnp.exp2(R[:H] - B[:H, :lo]).astype(BF16), zeros(L - lo)], 1) if I > 0 else zeros(L)
            ksc_b = jnp.concatenate([zeros(hi), jnp.exp2(R[H:] - B[H:, hi:]).astype(BF16)], 1) if I < NB - 1 else zeros(L)
            ksc = jnp.concatenate([ksc_f, ksc_b], 0)
            off = lax.dot_general((q_I * jnp.exp2(A_I - R)).astype(BF16), ksc, bnt, preferred_element_type=F32)
            rows.append(jnp.concatenate([att_f, att_b], 0) + off)
        att = jnp.concatenate(rows, 1)
        o = o + lax.dot_general(att.astype(BF16), iv, bnn, preferred_element_type=F32)
        for h in heads:
            of[pl.ds(tf, L), h * HG_DV:(h + 1) * HG_DV] = o[h]
            ob[pl.ds(tb, L), h * HG_DV:(h + 1) * HG_DV] = o[H + h]
        ST[...] = st * jnp.exp2(a_last) + lax.dot_general(ivT, kd, bnn, preferred_element_type=F32)

    def body(i, carry):
        chunk_pair(i, NC - 1 - i)
        return carry

    lax.fori_loop(0, NC, body, 0, unroll=min(8, NC))

    def epilogue(r, carry):
        t0 = pl.multiple_of(r * 128, 128)
        o = of[pl.ds(t0, 128), :] + ob[pl.ds(t0, 128), :]
        gsil = _silu(g_ref[pl.ds(t0, 128), :].astype(F32))
        outs = []
        for h in range(HG_HEADS):
            oh = o[:, h * HG_DV:(h + 1) * HG_DV]
            outs.append(oh * lax.rsqrt(jnp.mean(oh * oh, -1, keepdims=True) + RMS_EPS))
        c_ref[pl.ds(t0, 128), :] = (jnp.concatenate(outs, -1) * gsil).astype(BF16)
        return carry

    lax.fori_loop(0, T // 128, epilogue, 0)
    if not latent:
        for d in range(2):
            for h in range(HG_HEADS):
                so_ref[d, h] = ST[d * HG_HEADS + h].T


def _hgrn(p16, p32, lb_l, l, latent, S0=None, c_out=None, prev=None):
    T = DEC_SEQ if latent else SEQ
    B = DEC_BATCH if latent else BATCH
    rb0 = N_CTX // DEC_SEQ if latent else 0
    W = BRANCH_W
    full2 = lambda b: (0, 0)
    any_spec = pl.BlockSpec(memory_space=pl.ANY)
    col = lambda j: pl.BlockSpec((T, W), lambda b: (rb0 + b, j))
    s_spec = pl.BlockSpec((None, None, 2, HG_HEADS, HG_DK, HG_DV), lambda b: (b, l, 0, 0, 0, 0))
    in_specs = [col(0), col(1), col(7), col(8), col(9), pl.BlockSpec((1, W), full2), pl.BlockSpec((1, W), full2)]
    args = [p32, p32, p16, p16, p16, lb_l[0][None, :], lb_l[1][None, :]]
    c_shape = jax.ShapeDtypeStruct((N_TOK, W), BF16)
    c_spec = pl.BlockSpec((T, W), lambda b: (rb0 + b, 0))
    aliases = {}
    if latent:
        in_specs += [s_spec, any_spec]
        args += [S0, c_out]
        aliases = {8: 0}
        out_shape, out_specs = c_shape, c_spec
    else:
        out_shape = (c_shape, jax.ShapeDtypeStruct((B, DEPTH, 2, HG_HEADS, HG_DK, HG_DV), F32))
        out_specs = (c_spec, s_spec)
        if prev is not None:
            in_specs.append(any_spec)
            args.append(prev)
            aliases = {7: 1}
    scratch = [pltpu.VMEM((T, W), F32), pltpu.VMEM((T, W), F32), pltpu.VMEM((2 * HG_HEADS, HG_DV, HG_DK), F32),
               pltpu.VMEM((T // HG_CHUNK, HG_HEADS, HG_DV, HG_CHUNK), BF16),
               pltpu.VMEM((2, T, W), F32), pltpu.VMEM((2, T, W), F32)]
    return pl.pallas_call(
        functools.partial(_hgrn_kernel, T=T, latent=latent),
        out_shape=out_shape, grid=(B,), in_specs=in_specs, out_specs=out_specs, scratch_shapes=scratch,
        input_output_aliases=aliases,
        compiler_params=_cparams("arbitrary"),
        name="hgrn_lat" if latent else "hgrn_ctx",
    )(*args)


def _merge_kernel(a_ref, b_ref, c_ref, ga_ref, gb_ref, gc_ref, xc_ref, xl_ref, g1_ref, sh2_ref, sc2_ref,
                  wb_ref, wo_ref, lng_ref, lnb_ref, wr_ref, x1_ref, h2_ref, *, tm):
    def br(v_ref, g_ref, k):
        return jax.nn.sigmoid(g_ref[...].astype(F32)) * jnp.dot(v_ref[...], wb_ref[k], preferred_element_type=F32)

    mix = br(a_ref, ga_ref, 0) + br(b_ref, gb_ref, 1) + br(c_ref, gc_ref, 2)
    y = jnp.dot(mix.astype(BF16), wo_ref[...], preferred_element_type=F32)
    x = _pair_read(pl.program_id(0), tm, xc_ref, xl_ref)
    x1 = _layer_norm(DEEPNORM_ALPHA * x + g1_ref[...] * y, lng_ref[...], lnb_ref[...])
    x1_ref[...] = x1
    h2 = x1 * (1.0 + sc2_ref[...]) + sh2_ref[...]
    h2_ref[:, :D_MODEL] = h2.astype(h2_ref.dtype)
    lt = lax.dot_general(wr_ref[...], h2, _NT, preferred_element_type=F32, precision=HI)
    r = lax.broadcasted_iota(jnp.int32, lt.shape, 0)
    neg = -jnp.inf
    lg = jnp.where(r < N_GROUPS, lt, neg)
    mg = jnp.max(lg, 0, keepdims=True)
    g_sel = jnp.min(jnp.where(lg == mg, r, ROUTER_ROWS), 0, keepdims=True)
    p_sel = 1.0 / jnp.sum(jnp.where(r < N_GROUPS, jnp.exp(lg - mg), 0.0), 0, keepdims=True)
    lo = ROUTER_E0 + EXP_PER_GROUP * g_sel
    le = jnp.where((r >= lo) & (r < lo + EXP_PER_GROUP), lt, neg)
    v1 = jnp.max(le, 0, keepdims=True)
    i1 = jnp.min(jnp.where(le == v1, r, ROUTER_ROWS), 0, keepdims=True)
    le2 = jnp.where(r == i1, neg, le)
    v2 = jnp.max(le2, 0, keepdims=True)
    i2 = jnp.min(jnp.where(le2 == v2, r, ROUTER_ROWS), 0, keepdims=True)
    e2 = jnp.exp(v2 - v1)
    w1 = p_sel / (1.0 + e2)
    w2 = p_sel * e2 / (1.0 + e2)
    w1_hi = w1.astype(BF16).astype(F32)
    w2_hi = w2.astype(BF16).astype(F32)
    j1, j2 = i1 - lo, i2 - lo
    packed = jnp.where(r == j1, w1_hi, jnp.where(r == j2, w2_hi, jnp.where(
        r == j1 + EXP_PER_GROUP, w1 - w1_hi, jnp.where(r == j2 + EXP_PER_GROUP, w2 - w2_hi, jnp.where(
            r == 2 * EXP_PER_GROUP, g_sel.astype(F32), 0.0)))))
    packed = jnp.concatenate([packed, jnp.zeros((128 - ROUTER_ROWS, packed.shape[1]), F32)], 0)
    h2_ref[:, D_MODEL:] = packed.T.astype(h2_ref.dtype)


def _merge(a, b, c, p16, xc, xl, mod, wb, wo, lng, lnb, wr, l):
    tm = 512
    tok = lambda i: (i, 0)
    ln_spec = pl.BlockSpec((None, None, 1, D_MODEL), lambda i: (l, 0, 0, 0))
    return pl.pallas_call(
        functools.partial(_merge_kernel, tm=tm),
        out_shape=(jax.ShapeDtypeStruct((N_TOK, D_MODEL), F32), jax.ShapeDtypeStruct((N_TOK, MOE_XW), MOE_XDT)),
        grid=(N_TOK // tm,),
        in_specs=[pl.BlockSpec((tm, BRANCH_W), tok), pl.BlockSpec((tm, BRANCH_W), tok), pl.BlockSpec((tm, BRANCH_W), tok),
                  pl.BlockSpec((tm, D_MODEL), lambda i: (i, 5)), pl.BlockSpec((tm, D_MODEL), lambda i: (i, 6)),
                  pl.BlockSpec((tm, D_MODEL), lambda i: (i, 7)),
                  *_pair_specs(tm), _mod_spec(l, 2, tm), _mod_spec(l, 3, tm), _mod_spec(l, 4, tm),
                  pl.BlockSpec((None, 3, BRANCH_W, D_MODEL), lambda i: (l, 0, 0, 0)),
                  pl.BlockSpec((None, D_MODEL, D_MODEL), lambda i: (l, 0, 0)),
                  ln_spec, ln_spec,
                  pl.BlockSpec((None, ROUTER_ROWS, D_MODEL), lambda i: (l, 0, 0))],
        out_specs=(pl.BlockSpec((tm, D_MODEL), tok), pl.BlockSpec((tm, MOE_XW), tok)),
        compiler_params=_cparams("arbitrary"),
        name="merge",
    )(a, b, c, p16, p16, p16, xc, xl, mod, mod, mod, wb, wo, lng, lnb, wr)


def _moe_up_kernel(gid_ref, nused_ref, x_ref, w1_ref, w3_ref, hid_ref, w1b, w3b):
    f = pl.program_id(0)
    t = pl.program_id(1)

    @pl.when(t < nused_ref[0])
    def _():
        first = jnp.logical_or(t == 0, gid_ref[t] != gid_ref[jnp.maximum(t - 1, 0)])

        @pl.when(first)
        def _():
            w1b[...] = w1_ref[...].astype(BF16)
            w3b[...] = w3_ref[...].astype(BF16)

        x = x_ref[:, :D_MODEL].astype(BF16)
        rec = x_ref[:, D_MODEL:].astype(F32)
        lane = lax.broadcasted_iota(jnp.int32, rec.shape, 1)
        for j in range(MOE_FE):
            e = f * MOE_FE + j
            a = jnp.dot(x, w1b[j], preferred_element_type=F32)
            b = jnp.dot(x, w3b[j], preferred_element_type=F32)
            gcol = jnp.sum(jnp.where(jnp.logical_or(lane == e, lane == e + EXP_PER_GROUP), rec, 0.0), -1, keepdims=True)
            hid_ref[:, j * D_EXPERT:(j + 1) * D_EXPERT] = (_silu(a) * b * gcol).astype(BF16)


def _moe_tile(t, n):
    return jnp.minimum(t, n[0] - 1)


def _moe_up(gid, nused, xs, w1, w3, l):
    tm = MOE_TM
    npad = MOE_NT * tm
    nf = EXP_PER_GROUP // MOE_FE
    w_spec = pl.BlockSpec((None, MOE_FE, D_MODEL, D_EXPERT),
                          lambda f, t, g, n: (l, nf * g[_moe_tile(t, n)] + f, 0, 0))
    grid_spec = pltpu.PrefetchScalarGridSpec(
        num_scalar_prefetch=2,
        grid=(nf, MOE_NT),
        in_specs=[pl.BlockSpec((tm, MOE_XW), lambda f, t, g, n: (_moe_tile(t, n), 0)), w_spec, w_spec],
        out_specs=pl.BlockSpec((tm, MOE_FE * D_EXPERT), lambda f, t, g, n: (_moe_tile(t, n), f)),
        scratch_shapes=[pltpu.VMEM((MOE_FE, D_MODEL, D_EXPERT), BF16), pltpu.VMEM((MOE_FE, D_MODEL, D_EXPERT), BF16)],
    )
    return pl.pallas_call(
        _moe_up_kernel,
        out_shape=jax.ShapeDtypeStruct((npad, EXP_PER_GROUP * D_EXPERT), BF16),
        grid_spec=grid_spec,
        compiler_params=_cparams("arbitrary", "arbitrary"),
        name="moe_up",
    )(gid, nused, xs, w1, w3)


def _moe_down_kernel(gid_ref, nused_ref, hid_ref, w2_ref, y_ref, w2b):
    t = pl.program_id(0)

    @pl.when(t < nused_ref[0])
    def _():
        first = jnp.logical_or(t == 0, gid_ref[t] != gid_ref[jnp.maximum(t - 1, 0)])

        @pl.when(first)
        def _():
            w2b[...] = w2_ref[...].astype(BF16)

        y_ref[...] = jnp.dot(hid_ref[...], w2b[...], preferred_element_type=F32)


def _moe_down(gid, nused, hid, w2g, l):
    tm = MOE_TM
    npad = MOE_NT * tm
    hw = EXP_PER_GROUP * D_EXPERT
    grid_spec = pltpu.PrefetchScalarGridSpec(
        num_scalar_prefetch=2,
        grid=(MOE_NT,),
        in_specs=[pl.BlockSpec((tm, hw), lambda t, g, n: (_moe_tile(t, n), 0)),
                  pl.BlockSpec((None, None, hw, D_MODEL), lambda t, g, n: (l, g[_moe_tile(t, n)], 0, 0))],
        out_specs=pl.BlockSpec((tm, D_MODEL), lambda t, g, n: (_moe_tile(t, n), 0)),
        scratch_shapes=[pltpu.VMEM((hw, D_MODEL), BF16)],
    )
    return pl.pallas_call(
        _moe_down_kernel,
        out_shape=jax.ShapeDtypeStruct((npad, D_MODEL), F32),
        grid_spec=grid_spec,
        compiler_params=_cparams("arbitrary"),
        name="moe_down",
    )(gid, nused, hid, w2g)


def _moe(h2x, w1, w3, w2g, l):
    tm = MOE_TM
    npad = MOE_NT * tm
    g = h2x[:, D_MODEL + 2 * EXP_PER_GROUP].astype(jnp.int32)
    onehot = (g[:, None] == jnp.arange(N_GROUPS)[None, :]).astype(jnp.int32)
    counts = jnp.sum(onehot, 0)
    rank = jnp.sum((jnp.cumsum(onehot, 0) - onehot) * onehot, 1)
    padded = (counts + tm - 1) // tm * tm
    ends = jnp.cumsum(padded)
    offs = ends - padded
    dest = offs[g] + rank
    src = (jnp.arange(npad, dtype=jnp.int32) % N_TOK).at[dest].set(jnp.arange(N_TOK, dtype=jnp.int32),
                                                                   unique_indices=True)
    starts = jnp.arange(MOE_NT, dtype=jnp.int32) * tm
    tile_gid = jnp.minimum(jnp.sum((ends[None, :] <= starts[:, None]).astype(jnp.int32), 1), N_GROUPS - 1)
    nused = (ends[-1:] // tm).astype(jnp.int32)
    take = lambda arr, idx: arr.at[idx].get(mode="promise_in_bounds", unique_indices=False)
    hid = _moe_up(tile_gid, nused, take(h2x, src), w1, w3, l)
    ys = _moe_down(tile_gid, nused, hid, w2g, l)
    return take(ys, dest)


def _final_kernel(*refs, tm, with_h):
    x1_ref, y_ref, g2_ref, lng_ref, lnb_ref = refs[:5]
    x2 = _layer_norm(DEEPNORM_ALPHA * x1_ref[...] + g2_ref[...] * y_ref[...], lng_ref[...], lnb_ref[...])
    i = pl.program_id(0)
    if with_h:
        sh_ref, sc_ref, xc_ref, xl_ref, h_ref = refs[5:]
        h_ref[...] = (x2 * (1.0 + sc_ref[...]) + sh_ref[...]).astype(BF16)
    else:
        xc_ref, xl_ref = refs[5:]

    @pl.when(i < N_CTX // tm)
    def _():
        xc_ref[...] = x2

    @pl.when(i >= N_CTX // tm)
    def _():
        xl_ref[...] = x2


def _final(x1, y, mod, lng, lnb, l):
    tm = 1024
    tok = lambda i: (i, 0)
    with_h = l + 1 < DEPTH
    ln_spec = pl.BlockSpec((None, None, 1, D_MODEL), lambda i: (l, 1, 0, 0))
    half = jax.ShapeDtypeStruct((N_CTX, D_MODEL), F32)
    in_specs = [pl.BlockSpec((tm, D_MODEL), tok), pl.BlockSpec((tm, D_MODEL), tok), _mod_spec(l, 5, tm), ln_spec, ln_spec]
    args = [x1, y, mod, lng, lnb]
    out_shape = [half, half]
    out_specs = list(_pair_specs(tm))
    if with_h:
        in_specs += [_mod_spec(l + 1, 0, tm), _mod_spec(l + 1, 1, tm)]
        args += [mod, mod]
        out_shape.append(jax.ShapeDtypeStruct((N_TOK, D_MODEL), BF16))
        out_specs.append(pl.BlockSpec((tm, D_MODEL), tok))
    return pl.pallas_call(
        functools.partial(_final_kernel, tm=tm, with_h=with_h),
        out_shape=tuple(out_shape), grid=(N_TOK // tm,), in_specs=in_specs, out_specs=tuple(out_specs),
        compiler_params=_cparams("arbitrary"),
        name="final",
    )(*args)


def kernel(x_prompt, x_sample, c, cache_na_k, cache_na_v, state_mlstm_C, state_mlstm_n, state_mlstm_m, state_hgrn,
           c_ctx, w_mod, b_mod, w_in, b_in, mlstm_fbias, hgrn_lb_logits, na_rpb, w_branch, w_out, ln_g, ln_b,
           w_rg, w_re, w_e1, w_e3, w_e2):
    assert N_CTX == N_LAT
    lb_cum = jnp.cumsum(jax.nn.softmax(hgrn_lb_logits.astype(F32), axis=1), axis=1)
    lb_all = lb_cum - lb_cum[:, :1]

    cs = jnp.zeros((N_MODROWS, D_MODEL), F32).at[0].set(c_ctx).at[1:1 + DEC_BATCH].set(c)
    mod = _modulation(cs, w_mod, b_mod).reshape(DEPTH, N_MODROWS, 6, 1, D_MODEL)

    wb = w_branch.astype(BF16)
    wo = w_out.astype(BF16)
    lng = ln_g.reshape(DEPTH, 2, 1, D_MODEL)
    lnb = ln_b.reshape(DEPTH, 2, 1, D_MODEL)
    wr = jnp.zeros((DEPTH, ROUTER_ROWS, D_MODEL), F32)
    wr = wr.at[:, :N_GROUPS].set(jnp.swapaxes(w_rg, 1, 2)).at[:, ROUTER_E0:ROUTER_E0 + N_EXPERTS].set(jnp.swapaxes(w_re, 1, 2))
    w2g = w_e2.reshape(DEPTH, N_GROUPS, EXP_PER_GROUP * D_EXPERT, D_MODEL)
    b_main = jnp.concatenate([b_in[:, :GATE_COL0], b_in[:, GATE_COL0 + N_GATES:]], 1)

    xc = x_prompt.reshape(N_CTX, D_MODEL)
    xl = x_sample.reshape(N_LAT, D_MODEL)
    w_t = jnp.swapaxes(w_in, 1, 2)
    na_bias = jax.vmap(_na_bias_table)(na_rpb)
    h = _prep(xc, xl, mod)

    kv = (None, None)
    ma_states = None
    hg_state = None
    for l in range(DEPTH):
        p16, p32 = _inproj(h, w_t, b_main[l][None, :], l)
        gcol, gt = _gates(h, w_t, b_in.reshape(DEPTH, 1, N_IN), l)
        gt3 = gt.reshape(N_GATES, N_TOK // MA_CHUNK, MA_CHUNK).transpose(1, 0, 2)

        a, *ma_states = _mlstm(p16, gcol, gt3, mlstm_fbias[l], l, False, prev=ma_states)
        a = _mlstm(p16, gcol, gt3, mlstm_fbias[l], l, True, state_mlstm_C, state_mlstm_n, state_mlstm_m, a_out=a)
        b, *kv = _ctx_attention(p16, l, *kv)
        b = _lat_attention(p16, cache_na_k, cache_na_v, na_bias, l, b)
        cc, hg_state = _hgrn(p16, p32, lb_all[:, l], l, False, prev=hg_state)
        cc = _hgrn(p16, p32, lb_all[:, l], l, True, state_hgrn, c_out=cc)

        x1, h2x = _merge(a, b, cc, p16, xc, xl, mod, wb, wo, lng, lnb, wr, l)
        y2 = _moe(h2x, w_e1, w_e3, w2g, l)
        outs = _final(x1, y2, mod, lng, lnb, l)
        xc, xl = outs[0], outs[1]
        if l + 1 < DEPTH:
            h = outs[2]

    dt = x_prompt.dtype
    new_C, new_n, new_m = ma_states
    new_n = new_n.reshape(BATCH, DEPTH, 2, MA_HEADS, MA_DK)
    new_m = new_m[:, :, :, 0].reshape(BATCH, DEPTH, 2, MA_HEADS)
    return (xc.reshape(BATCH, SEQ, D_MODEL), xl.reshape(DEC_BATCH, DEC_SEQ, D_MODEL), kv[0], kv[1],
            new_C.astype(dt), new_n.astype(dt), new_m.astype(dt), hg_state.astype(dt))
```
